```python
import jax, jax.numpy as jnp
from jax import lax
import numpy as np

D_MODEL = 2048
BATCH = 8
SEQ = 4096
DEPTH = 2

D_MIX = D_MODEL
GROUP_W = D_MIX // 4
EPS = 1e-6
ROPE_THETA = 10000.0

RET_HEADS = 4
RET_HD = GROUP_W // RET_HEADS
RET_CHUNK = 128

GLA_HEADS = 4
GLA_DK = GROUP_W // 2 // GLA_HEADS
GLA_DV = GROUP_W // GLA_HEADS
GLA_GATE_RANK = 16
GLA_TAU = 16.0
GLA_CHUNK = 64

POOL_GROUPS = 4
POOL_GW = GROUP_W // POOL_GROUPS
POOL_WINDOWS = (2, 4, 8, 16)

MLA_HEADS = 4
MLA_NOPE = 128
MLA_ROPE = 64
MLA_V = GROUP_W // MLA_HEADS
MLA_Q_RANK = D_MODEL // 4
MLA_KV_RANK = D_MODEL // 8
MLA_QBLOCK = 128

SPLIT_SIZES = (
    GROUP_W, GROUP_W, GROUP_W, GROUP_W,
    GLA_HEADS * GLA_DK, GLA_HEADS * GLA_DK, GROUP_W, GROUP_W,
    GLA_GATE_RANK, GLA_GATE_RANK,
    GROUP_W, GROUP_W,
    MLA_Q_RANK, MLA_KV_RANK, MLA_ROPE, GROUP_W,
)
IN_COLS = sum(SPLIT_SIZES)

kernel_name = "bidir_hybrid_headgroup_block"


def _rmsnorm(x, g):
    xf = x.astype(jnp.float32)
    y = xf * lax.rsqrt(jnp.mean(xf * xf, axis=-1, keepdims=True) + EPS)
    return y * g.astype(jnp.float32)


def _rope_tables(dim, s):
    inv = 1.0 / (ROPE_THETA ** (jnp.arange(0, dim, 2, dtype=jnp.float32) / dim))
    ang = jnp.arange(s, dtype=jnp.float32)[:, None] * inv[None, :]
    return jnp.cos(ang), jnp.sin(ang)


def _rotate(x, cos, sin):
    half = x.shape[-1] // 2
    x1, x2 = x[..., :half], x[..., half:]
    return jnp.concatenate([x1 * cos - x2 * sin, x2 * cos + x1 * sin], axis=-1)


def _heads(t, h, d):
    b, s, _ = t.shape
    return t.reshape(b, s, h, d).transpose(0, 2, 1, 3)


def _head_rmsnorm(o, g):
    b, h, s, d = o.shape
    t = _rmsnorm(o.transpose(0, 2, 1, 3), g.reshape(h, d))
    return t.reshape(b, s, h * d)


def _retention_scan(q, k, v, gamma):
    b, h, s, dk = q.shape
    dv = v.shape[-1]
    c = RET_CHUNK
    n = s // c
    q = q.reshape(b, h, n, c, dk)
    k = k.reshape(b, h, n, c, dk)
    v = v.reshape(b, h, n, c, dv)
    log_g = jnp.log(gamma)
    idx = jnp.arange(c, dtype=jnp.float32)
    diff = idx[:, None] - idx[None, :]
    dmat = jnp.where(diff >= 0, jnp.exp(jnp.maximum(diff, 0.0)[None] * log_g[:, None, None]), 0.0)
    scores = jnp.einsum('bhnid,bhnjd->bhnij', q, k) * dmat[None, :, None]
    o_inner = jnp.einsum('bhnij,bhnje->bhnie', scores, v)
    k_dec = k * jnp.exp((c - 1 - idx)[None, :] * log_g[:, None])[None, :, None, :, None]
    kv = jnp.einsum('bhncd,bhnce->bhnde', k_dec, v)
    chunk_decay = jnp.exp(c * log_g)[:, None, None]

    def step(state, kv_n):
        return chunk_decay * state + kv_n, state

    _, s_before = lax.scan(step, jnp.zeros((b, h, dk, dv), jnp.float32), jnp.moveaxis(kv, 2, 0))
    s_before = jnp.moveaxis(s_before, 0, 2)
    q_dec = q * jnp.exp((idx + 1.0)[None, :] * log_g[:, None])[None, :, None, :, None]
    o_cross = jnp.einsum('bhncd,bhnde->bhnce', q_dec, s_before)
    return (o_inner + o_cross).reshape(b, h, s, dv)


def _gla_scan(q, k, v, log_a):
    b, h, s, dk = q.shape
    dv = v.shape[-1]
    c = GLA_CHUNK
    n = s // c
    q = q.reshape(b, h, n, c, dk)
    k = k.reshape(b, h, n, c, dk)
    v = v.reshape(b, h, n, c, dv)
    cum = jnp.cumsum(log_a.reshape(b, h, n, c, dk), axis=3)
    q_t = q * jnp.exp(cum)
    k_t = k * jnp.exp(-cum)
    mask = jnp.tril(jnp.ones((c, c), dtype=bool))
    scores = jnp.where(mask, jnp.einsum('bhnid,bhnjd->bhnij', q_t, k_t), 0.0)
    o_inner = jnp.einsum('bhnij,bhnje->bhnie', scores, v)
    last = cum[..., -1:, :]
    kv = jnp.einsum('bhncd,bhnce->bhnde', k * jnp.exp(last - cum), v)
    decay = jnp.exp(last[..., 0, :])

    def step(state, inp):
        kv_n, d_n = inp
        return d_n[..., None] * state + kv_n, state

    _, s_before = lax.scan(step, jnp.zeros((b, h, dk, dv), jnp.float32),
                           (jnp.moveaxis(kv, 2, 0), jnp.moveaxis(decay, 2, 0)))
    s_before = jnp.moveaxis(s_before, 0, 2)
    o_cross = jnp.einsum('bhncd,bhnde->bhnce', q_t, s_before)
    return (o_inner + o_cross).reshape(b, h, s, dv)


def _retention_branch(q, k, v, gate, norm_g, cos, sin):
    qh = _rotate(_heads(q, RET_HEADS, RET_HD), cos, sin)
    kh = _rotate(_heads(k, RET_HEADS, RET_HD), cos, sin) * (RET_HD ** -0.5)
    vh = _heads(v, RET_HEADS, RET_HD)
    gamma_f = 1.0 - 2.0 ** (-5.0 - jnp.arange(RET_HEADS, dtype=jnp.float32))
    gamma_b = gamma_f[::-1]
    o_f = _retention_scan(qh, kh, vh, gamma_f)
    o_b = jnp.flip(_retention_scan(jnp.flip(qh, 2), jnp.flip(kh, 2), jnp.flip(vh, 2), gamma_b), 2)
    return jax.nn.silu(gate) * _head_rmsnorm(o_f + o_b, norm_g)


def _gla_branch(q, k, v, gate, a_f, a_b, wa2_f, ba_f, wa2_b, ba_b, norm_g):
    qh = _heads(q, GLA_HEADS, GLA_DK) * (GLA_DK ** -0.5)
    kh = _heads(k, GLA_HEADS, GLA_DK)
    vh = _heads(v, GLA_HEADS, GLA_DV)
    la_f = _heads(jax.nn.log_sigmoid(a_f @ wa2_f + ba_f) / GLA_TAU, GLA_HEADS, GLA_DK)
    la_b = _heads(jax.nn.log_sigmoid(a_b @ wa2_b + ba_b) / GLA_TAU, GLA_HEADS, GLA_DK)
    o_f = _gla_scan(qh, kh, vh, la_f)
    o_b = jnp.flip(_gla_scan(jnp.flip(qh, 2), jnp.flip(kh, 2), jnp.flip(vh, 2), jnp.flip(la_b, 2)), 2)
    return jax.nn.silu(gate) * _head_rmsnorm(o_f + o_b, norm_g)


def _pool_branch(u, gate, pool_w, pool_scale):
    b, s, _ = u.shape
    ug = u.reshape(b, s, POOL_GROUPS, POOL_GW)
    cs = jnp.concatenate([jnp.zeros((b, 1, POOL_GROUPS, POOL_GW), jnp.float32),
                          jnp.cumsum(ug, axis=1)], axis=1)
    t = jnp.arange(s)
    outs = []
    for g, w_len in enumerate(POOL_WINDOWS):
        lo = jnp.clip(t - w_len // 2, 0, s)
        hi = jnp.clip(t + w_len // 2, 0, s)
        csg = cs[:, :, g]
        win_sum = jnp.take(csg, hi, axis=1) - jnp.take(csg, lo, axis=1)
        cnt = (hi - lo).astype(jnp.float32)[None, :, None]
        outs.append(win_sum / cnt - ug[:, :, g])
    pooled = jnp.stack(outs, axis=2)
    mixed = jnp.einsum('bsgc,gcd->bsgd', pooled, pool_w).reshape(b, s, GROUP_W)
    return jax.nn.silu(gate) * (mixed * pool_scale)


def _mla_branch(q_lat, kv_lat, k_pe, gate, q_norm_g, wq_b, kv_norm_g, wkv_b, qn_g, kn_g, cos, sin):
    b, s, _ = q_lat.shape
    q = (_rmsnorm(q_lat, q_norm_g) @ wq_b).reshape(b, s, MLA_HEADS, MLA_NOPE + MLA_ROPE)
    kv = (_rmsnorm(kv_lat, kv_norm_g) @ wkv_b).reshape(b, s, MLA_HEADS, MLA_NOPE + MLA_V)
    k_nope, v = kv[..., :MLA_NOPE], kv[..., MLA_NOPE:]
    k = jnp.concatenate([k_nope, jnp.broadcast_to(k_pe[:, :, None, :], (b, s, MLA_HEADS, MLA_ROPE))], axis=-1)
    q = _rmsnorm(q, qn_g)
    k = _rmsnorm(k, kn_g)
    c, sn = cos[:, None, :], sin[:, None, :]
    q = jnp.concatenate([q[..., :MLA_NOPE], _rotate(q[..., MLA_NOPE:], c, sn)], axis=-1)
    k = jnp.concatenate([k[..., :MLA_NOPE], _rotate(k[..., MLA_NOPE:], c, sn)], axis=-1)
    q = q.transpose(0, 2, 1, 3)
    k = k.transpose(0, 2, 1, 3)
    v = v.transpose(0, 2, 1, 3)
    scale = (MLA_NOPE + MLA_ROPE) ** -0.5
    nb = s // MLA_QBLOCK
    qb = q.reshape(b, MLA_HEADS, nb, MLA_QBLOCK, MLA_NOPE + MLA_ROPE).transpose(2, 0, 1, 3, 4)

    def attend(q_blk):
        sc = jnp.einsum('bhqd,bhkd->bhqk', q_blk, k).astype(jnp.float32) * scale
        p = jax.nn.softmax(sc, axis=-1)
        return jnp.einsum('bhqk,bhkd->bhqd', p, v)

    o = lax.map(attend, qb)
    o = o.transpose(1, 0, 3, 2, 4).reshape(b, s, MLA_HEADS * MLA_V)
    return jax.nn.silu(gate) * o


def _fwd_setup_inputs(seed: int = 0) -> dict:
    key = jax.random.key(seed)
    ks = jax.random.split(key, 20)
    f32 = jnp.float32

    def nrm(k, shape, scale):
        return jax.random.normal(k, shape, f32) * scale

    def gain(k, shape):
        return 1.0 + 0.02 * jax.random.normal(k, shape, f32)

    return {
        "x": jax.random.normal(ks[0], (BATCH, SEQ, D_MODEL), f32),
        "norm_g": gain(ks[1], (DEPTH, D_MODEL)),
        "w_in": nrm(ks[2], (DEPTH, D_MODEL, IN_COLS), D_MODEL ** -0.5),
        "ret_norm_g": gain(ks[3], (DEPTH, GROUP_W)),
        "gla_wa2_f": nrm(ks[4], (DEPTH, GLA_GATE_RANK, GLA_HEADS * GLA_DK), GLA_GATE_RANK ** -0.5),
        "gla_ba_f": nrm(ks[5], (DEPTH, GLA_HEADS * GLA_DK), 0.1),
        "gla_wa2_b": nrm(ks[6], (DEPTH, GLA_GATE_RANK, GLA_HEADS * GLA_DK), GLA_GATE_RANK ** -0.5),
        "gla_ba_b": nrm(ks[7], (DEPTH, GLA_HEADS * GLA_DK), 0.1),
        "gla_norm_g": gain(ks[8], (DEPTH, GROUP_W)),
        "pool_w": nrm(ks[9], (DEPTH, POOL_GROUPS, POOL_GW, POOL_GW), POOL_GW ** -0.5),
        "pool_scale": gain(ks[10], (DEPTH, GROUP_W)),
        "mla_q_norm_g": gain(ks[11], (DEPTH, MLA_Q_RANK)),
        "mla_wq_b": nrm(ks[12], (DEPTH, MLA_Q_RANK, MLA_HEADS * (MLA_NOPE + MLA_ROPE)), MLA_Q_RANK ** -0.5),
        "mla_kv_norm_g": gain(ks[13], (DEPTH, MLA_KV_RANK)),
        "mla_wkv_b": nrm(ks[14], (DEPTH, MLA_KV_RANK, MLA_HEADS * (MLA_NOPE + MLA_V)), MLA_KV_RANK ** -0.5),
        "mla_qk_norm_q": gain(ks[15], (DEPTH, MLA_NOPE + MLA_ROPE)),
        "mla_qk_norm_k": gain(ks[16], (DEPTH, MLA_NOPE + MLA_ROPE)),
        "w_out": nrm(ks[17], (DEPTH, D_MIX, D_MODEL), D_MIX ** -0.5),
    }


def _fwd_reference(x, norm_g, w_in, ret_norm_g, gla_wa2_f, gla_ba_f, gla_wa2_b, gla_ba_b, gla_norm_g,
              pool_w, pool_scale, mla_q_norm_g, mla_wq_b, mla_kv_norm_g, mla_wkv_b,
              mla_qk_norm_q, mla_qk_norm_k, w_out):
    s = x.shape[1]
    cos_r, sin_r = _rope_tables(RET_HD, s)
    cos_m, sin_m = _rope_tables(MLA_ROPE, s)
    points = [int(p) for p in np.cumsum(SPLIT_SIZES)[:-1]]
    for l in range(DEPTH):
        h = _rmsnorm(x, norm_g[l]).astype(x.dtype)
        z = (h @ w_in[l]).astype(jnp.float32)
        (rq, rk, rv, rg, gq, gk, gv, gg, ga_f, ga_b, pv, pg, mq, mkv, mkr, mg) = jnp.split(z, points, axis=-1)
        y_a = _retention_branch(rq, rk, rv, rg, ret_norm_g[l], cos_r, sin_r)
        y_b = _gla_branch(gq, gk, gv, gg, ga_f, ga_b, gla_wa2_f[l], gla_ba_f[l],
                          gla_wa2_b[l], gla_ba_b[l], gla_norm_g[l])
        y_c = _pool_branch(pv, pg, pool_w[l], pool_scale[l])
        y_d = _mla_branch(mq, mkv, mkr, mg, mla_q_norm_g[l], mla_wq_b[l], mla_kv_norm_g[l], mla_wkv_b[l],
                          mla_qk_norm_q[l], mla_qk_norm_k[l], cos_m, sin_m)
        y = jnp.concatenate([y_a, y_b, y_c, y_d], axis=-1).astype(x.dtype)
        x = x + y @ w_out[l]
    return x


import jax as _jax
import jax.numpy as _jnp

TWIN_FORMAT = 'train_step'
FWD_PARAMS = ['x', 'norm_g', 'w_in', 'ret_norm_g', 'gla_wa2_f', 'gla_ba_f', 'gla_wa2_b', 'gla_ba_b', 'gla_norm_g', 'pool_w', 'pool_scale', 'mla_q_norm_g', 'mla_wq_b', 'mla_kv_norm_g', 'mla_wkv_b', 'mla_qk_norm_q', 'mla_qk_norm_k', 'w_out']
TWIN_WEIGHTS = ['norm_g', 'w_in', 'ret_norm_g', 'gla_wa2_f', 'gla_ba_f', 'gla_wa2_b', 'gla_ba_b', 'gla_norm_g', 'pool_w', 'pool_scale', 'mla_q_norm_g', 'mla_wq_b', 'mla_kv_norm_g', 'mla_wkv_b', 'mla_qk_norm_q', 'mla_qk_norm_k', 'w_out']
TWIN_DIFF_INPUT = 'x'
TWIN_INPUTS = ['x', 'norm_g', 'w_in', 'ret_norm_g', 'gla_wa2_f', 'gla_ba_f', 'gla_wa2_b', 'gla_ba_b', 'gla_norm_g', 'pool_w', 'pool_scale', 'mla_q_norm_g', 'mla_wq_b', 'mla_kv_norm_g', 'mla_wkv_b', 'mla_qk_norm_q', 'mla_qk_norm_k', 'w_out', 'loss_target', 'm_norm_g', 'm_w_in', 'm_ret_norm_g', 'm_gla_wa2_f', 'm_gla_ba_f', 'm_gla_wa2_b', 'm_gla_ba_b', 'm_gla_norm_g', 'm_pool_w', 'm_pool_scale', 'm_mla_q_norm_g', 'm_mla_wq_b', 'm_mla_kv_norm_g', 'm_mla_wkv_b', 'm_mla_qk_norm_q', 'm_mla_qk_norm_k', 'm_w_out', 'v_norm_g', 'v_w_in', 'v_ret_norm_g', 'v_gla_wa2_f', 'v_gla_ba_f', 'v_gla_wa2_b', 'v_gla_ba_b', 'v_gla_norm_g', 'v_pool_w', 'v_pool_scale', 'v_mla_q_norm_g', 'v_mla_wq_b', 'v_mla_kv_norm_g', 'v_mla_wkv_b', 'v_mla_qk_norm_q', 'v_mla_qk_norm_k', 'v_w_out']
TWIN_OUTPUTS = ['loss', 'grad_x', 'grad_norm_g', 'grad_w_in', 'grad_ret_norm_g', 'grad_gla_wa2_f', 'grad_gla_ba_f', 'grad_gla_wa2_b', 'grad_gla_ba_b', 'grad_gla_norm_g', 'grad_pool_w', 'grad_pool_scale', 'grad_mla_q_norm_g', 'grad_mla_wq_b', 'grad_mla_kv_norm_g', 'grad_mla_wkv_b', 'grad_mla_qk_norm_q', 'grad_mla_qk_norm_k', 'grad_w_out', 'delta_norm_g', 'delta_w_in', 'delta_ret_norm_g', 'delta_gla_wa2_f', 'delta_gla_ba_f', 'delta_gla_wa2_b', 'delta_gla_ba_b', 'delta_gla_norm_g', 'delta_pool_w', 'delta_pool_scale', 'delta_mla_q_norm_g', 'delta_mla_wq_b', 'delta_mla_kv_norm_g', 'delta_mla_wkv_b', 'delta_mla_qk_norm_q', 'delta_mla_qk_norm_k', 'delta_w_out', 'new_m_norm_g', 'new_m_w_in', 'new_m_ret_norm_g', 'new_m_gla_wa2_f', 'new_m_gla_ba_f', 'new_m_gla_wa2_b', 'new_m_gla_ba_b', 'new_m_gla_norm_g', 'new_m_pool_w', 'new_m_pool_scale', 'new_m_mla_q_norm_g', 'new_m_mla_wq_b', 'new_m_mla_kv_norm_g', 'new_m_mla_wkv_b', 'new_m_mla_qk_norm_q', 'new_m_mla_qk_norm_k', 'new_m_w_out', 'new_v_norm_g', 'new_v_w_in', 'new_v_ret_norm_g', 'new_v_gla_wa2_f', 'new_v_gla_ba_f', 'new_v_gla_wa2_b', 'new_v_gla_ba_b', 'new_v_gla_norm_g', 'new_v_pool_w', 'new_v_pool_scale', 'new_v_mla_q_norm_g', 'new_v_mla_wq_b', 'new_v_mla_kv_norm_g', 'new_v_mla_wkv_b', 'new_v_mla_qk_norm_q', 'new_v_mla_qk_norm_k', 'new_v_w_out']
TWIN_LEAF_KINDS = {'loss': 'loss', 'grad_x': 'grad_x', 'grad_norm_g': 'grad_w', 'grad_w_in': 'grad_w', 'grad_ret_norm_g': 'grad_w', 'grad_gla_wa2_f': 'grad_w', 'grad_gla_ba_f': 'grad_w', 'grad_gla_wa2_b': 'grad_w', 'grad_gla_ba_b': 'grad_w', 'grad_gla_norm_g': 'grad_w', 'grad_pool_w': 'grad_w', 'grad_pool_scale': 'grad_w', 'grad_mla_q_norm_g': 'grad_w', 'grad_mla_wq_b': 'grad_w', 'grad_mla_kv_norm_g': 'grad_w', 'grad_mla_wkv_b': 'grad_w', 'grad_mla_qk_norm_q': 'grad_w', 'grad_mla_qk_norm_k': 'grad_w', 'grad_w_out': 'grad_w', 'delta_norm_g': 'delta_w', 'delta_w_in': 'delta_w', 'delta_ret_norm_g': 'delta_w', 'delta_gla_wa2_f': 'delta_w', 'delta_gla_ba_f': 'delta_w', 'delta_gla_wa2_b': 'delta_w', 'delta_gla_ba_b': 'delta_w', 'delta_gla_norm_g': 'delta_w', 'delta_pool_w': 'delta_w', 'delta_pool_scale': 'delta_w', 'delta_mla_q_norm_g': 'delta_w', 'delta_mla_wq_b': 'delta_w', 'delta_mla_kv_norm_g': 'delta_w', 'delta_mla_wkv_b': 'delta_w', 'delta_mla_qk_norm_q': 'delta_w', 'delta_mla_qk_norm_k': 'delta_w', 'delta_w_out': 'delta_w', 'new_m_norm_g': 'new_m', 'new_m_w_in': 'new_m', 'new_m_ret_norm_g': 'new_m', 'new_m_gla_wa2_f': 'new_m', 'new_m_gla_ba_f': 'new_m', 'new_m_gla_wa2_b': 'new_m', 'new_m_gla_ba_b': 'new_m', 'new_m_gla_norm_g': 'new_m', 'new_m_pool_w': 'new_m', 'new_m_pool_scale': 'new_m', 'new_m_mla_q_norm_g': 'new_m', 'new_m_mla_wq_b': 'new_m', 'new_m_mla_kv_norm_g': 'new_m', 'new_m_mla_wkv_b': 'new_m', 'new_m_mla_qk_norm_q': 'new_m', 'new_m_mla_qk_norm_k': 'new_m', 'new_m_w_out': 'new_m', 'new_v_norm_g': 'new_v', 'new_v_w_in': 'new_v', 'new_v_ret_norm_g': 'new_v', 'new_v_gla_wa2_f': 'new_v', 'new_v_gla_ba_f': 'new_v', 'new_v_gla_wa2_b': 'new_v', 'new_v_gla_ba_b': 'new_v', 'new_v_gla_norm_g': 'new_v', 'new_v_pool_w': 'new_v', 'new_v_pool_scale': 'new_v', 'new_v_mla_q_norm_g': 'new_v', 'new_v_mla_wq_b': 'new_v', 'new_v_mla_kv_norm_g': 'new_v', 'new_v_mla_wkv_b': 'new_v', 'new_v_mla_qk_norm_q': 'new_v', 'new_v_mla_qk_norm_k': 'new_v', 'new_v_w_out': 'new_v'}


def _forward(args):
    return _fwd_reference(*[args[k] for k in FWD_PARAMS])


def _output_shape():
    def fwd():
        inp = _fwd_setup_inputs(0)
        return _fwd_reference(*[inp[k] for k in FWD_PARAMS])
    out = _jax.eval_shape(fwd)
    return out.shape, out.dtype

N_MICROBATCH = 1
ADAM_LR = 0.001
ADAM_B1 = 0.9
ADAM_B2 = 0.999
ADAM_EPS = 1e-08
ADAM_WD = 0.01
ADAM_STEP = 10
PER_EXAMPLE_BATCH_AXIS = {'x': 0, 'loss_target': 0}
SHARED_INPUTS = []
_WEIGHT_DTYPES = {'norm_g': _jnp.float32, 'w_in': _jnp.float32, 'ret_norm_g': _jnp.float32, 'gla_wa2_f': _jnp.float32, 'gla_ba_f': _jnp.float32, 'gla_wa2_b': _jnp.float32, 'gla_ba_b': _jnp.float32, 'gla_norm_g': _jnp.float32, 'pool_w': _jnp.float32, 'pool_scale': _jnp.float32, 'mla_q_norm_g': _jnp.float32, 'mla_wq_b': _jnp.float32, 'mla_kv_norm_g': _jnp.float32, 'mla_wkv_b': _jnp.float32, 'mla_qk_norm_q': _jnp.float32, 'mla_qk_norm_k': _jnp.float32, 'w_out': _jnp.float32}
MOMENT_SCALE = {'norm_g': 5.706011e+00, 'w_in': 1.373548e-01, 'ret_norm_g': 5.727190e+00, 'gla_wa2_f': 1.417448e-02, 'gla_ba_f': 5.571580e-02, 'gla_wa2_b': 1.438194e-02, 'gla_ba_b': 5.536901e-02, 'gla_norm_g': 5.716476e+00, 'pool_w': 3.639638e-01, 'pool_scale': 4.511520e+00, 'mla_q_norm_g': 1.169776e-02, 'mla_wq_b': 9.997516e-03, 'mla_kv_norm_g': 4.860390e-02, 'mla_wkv_b': 1.187004e-02, 'mla_qk_norm_q': 4.846679e-02, 'mla_qk_norm_k': 4.890838e-02, 'w_out': 1.188877e-01}


def _to_microbatches(a, axis):
    t = _jnp.moveaxis(a, axis, 0)
    t = t.reshape((N_MICROBATCH, t.shape[0] // N_MICROBATCH) + t.shape[1:])
    return _jnp.moveaxis(t, 1, axis + 1)


def setup_inputs(seed: int = 0) -> dict:
    inp = _fwd_setup_inputs(seed)
    key = _jax.random.fold_in(_jax.random.key(seed), 7919)
    shape, _ = _output_shape()
    out = dict(inp)
    out["loss_target"] = _jax.random.normal(_jax.random.fold_in(key, 0), shape, _jnp.float32)
    for i, name in enumerate(TWIN_WEIGHTS):
        w = inp[name].astype(_jnp.float32)
        if MOMENT_SCALE is None:
            s = _jnp.sqrt(_jnp.mean(_jnp.square(w)) + 1e-30)
        else:
            s = MOMENT_SCALE[name]
        km, kv = _jax.random.split(_jax.random.fold_in(key, i + 1))
        out[name] = w
        out["m_" + name] = s * _jax.random.normal(km, w.shape, _jnp.float32)
        out["v_" + name] = (s * s) * _jax.random.uniform(kv, w.shape, _jnp.float32, 0.5, 1.5)
    if N_MICROBATCH > 1:
        for name, axis in PER_EXAMPLE_BATCH_AXIS.items():
            out[name] = _to_microbatches(out[name], axis)
    return {'x': out['x'], 'norm_g': out['norm_g'], 'w_in': out['w_in'], 'ret_norm_g': out['ret_norm_g'], 'gla_wa2_f': out['gla_wa2_f'], 'gla_ba_f': out['gla_ba_f'], 'gla_wa2_b': out['gla_wa2_b'], 'gla_ba_b': out['gla_ba_b'], 'gla_norm_g': out['gla_norm_g'], 'pool_w': out['pool_w'], 'pool_scale': out['pool_scale'], 'mla_q_norm_g': out['mla_q_norm_g'], 'mla_wq_b': out['mla_wq_b'], 'mla_kv_norm_g': out['mla_kv_norm_g'], 'mla_wkv_b': out['mla_wkv_b'], 'mla_qk_norm_q': out['mla_qk_norm_q'], 'mla_qk_norm_k': out['mla_qk_norm_k'], 'w_out': out['w_out'], 'loss_target': out['loss_target'], 'm_norm_g': out['m_norm_g'], 'm_w_in': out['m_w_in'], 'm_ret_norm_g': out['m_ret_norm_g'], 'm_gla_wa2_f': out['m_gla_wa2_f'], 'm_gla_ba_f': out['m_gla_ba_f'], 'm_gla_wa2_b': out['m_gla_wa2_b'], 'm_gla_ba_b': out['m_gla_ba_b'], 'm_gla_norm_g': out['m_gla_norm_g'], 'm_pool_w': out['m_pool_w'], 'm_pool_scale': out['m_pool_scale'], 'm_mla_q_norm_g': out['m_mla_q_norm_g'], 'm_mla_wq_b': out['m_mla_wq_b'], 'm_mla_kv_norm_g': out['m_mla_kv_norm_g'], 'm_mla_wkv_b': out['m_mla_wkv_b'], 'm_mla_qk_norm_q': out['m_mla_qk_norm_q'], 'm_mla_qk_norm_k': out['m_mla_qk_norm_k'], 'm_w_out': out['m_w_out'], 'v_norm_g': out['v_norm_g'], 'v_w_in': out['v_w_in'], 'v_ret_norm_g': out['v_ret_norm_g'], 'v_gla_wa2_f': out['v_gla_wa2_f'], 'v_gla_ba_f': out['v_gla_ba_f'], 'v_gla_wa2_b': out['v_gla_wa2_b'], 'v_gla_ba_b': out['v_gla_ba_b'], 'v_gla_norm_g': out['v_gla_norm_g'], 'v_pool_w': out['v_pool_w'], 'v_pool_scale': out['v_pool_scale'], 'v_mla_q_norm_g': out['v_mla_q_norm_g'], 'v_mla_wq_b': out['v_mla_wq_b'], 'v_mla_kv_norm_g': out['v_mla_kv_norm_g'], 'v_mla_wkv_b': out['v_mla_wkv_b'], 'v_mla_qk_norm_q': out['v_mla_qk_norm_q'], 'v_mla_qk_norm_k': out['v_mla_qk_norm_k'], 'v_w_out': out['v_w_out']}


def _loss(weights, diff, rest, loss_target):
    with _jax.named_scope("forward"):
        args = {**rest, TWIN_DIFF_INPUT: diff, **{k: w.astype(_WEIGHT_DTYPES[k]) for k, w in weights.items()}}
        y = _forward(args)
    with _jax.named_scope("loss_head"):
        err = _jnp.square(y.astype(_jnp.float32) - loss_target)
        return 0.5 * _jnp.sum(_jnp.mean(err, axis=-1)) if err.ndim else 0.5 * err


def _adamw(w, g, m, v):
    m = ADAM_B1 * m + (1.0 - ADAM_B1) * g
    v = ADAM_B2 * v + (1.0 - ADAM_B2) * _jnp.square(g)
    m_hat = m / (1.0 - ADAM_B1 ** ADAM_STEP)
    v_hat = v / (1.0 - ADAM_B2 ** ADAM_STEP)
    delta = -ADAM_LR * (m_hat / (_jnp.sqrt(v_hat) + ADAM_EPS) + ADAM_WD * w)
    return delta, m, v


def reference(x, norm_g, w_in, ret_norm_g, gla_wa2_f, gla_ba_f, gla_wa2_b, gla_ba_b, gla_norm_g, pool_w, pool_scale, mla_q_norm_g, mla_wq_b, mla_kv_norm_g, mla_wkv_b, mla_qk_norm_q, mla_qk_norm_k, w_out, loss_target, m_norm_g, m_w_in, m_ret_norm_g, m_gla_wa2_f, m_gla_ba_f, m_gla_wa2_b, m_gla_ba_b, m_gla_norm_g, m_pool_w, m_pool_scale, m_mla_q_norm_g, m_mla_wq_b, m_mla_kv_norm_g, m_mla_wkv_b, m_mla_qk_norm_q, m_mla_qk_norm_k, m_w_out, v_norm_g, v_w_in, v_ret_norm_g, v_gla_wa2_f, v_gla_ba_f, v_gla_wa2_b, v_gla_ba_b, v_gla_norm_g, v_pool_w, v_pool_scale, v_mla_q_norm_g, v_mla_wq_b, v_mla_kv_norm_g, v_mla_wkv_b, v_mla_qk_norm_q, v_mla_qk_norm_k, v_w_out):
    given = dict(x=x, norm_g=norm_g, w_in=w_in, ret_norm_g=ret_norm_g, gla_wa2_f=gla_wa2_f, gla_ba_f=gla_ba_f, gla_wa2_b=gla_wa2_b, gla_ba_b=gla_ba_b, gla_norm_g=gla_norm_g, pool_w=pool_w, pool_scale=pool_scale, mla_q_norm_g=mla_q_norm_g, mla_wq_b=mla_wq_b, mla_kv_norm_g=mla_kv_norm_g, mla_wkv_b=mla_wkv_b, mla_qk_norm_q=mla_qk_norm_q, mla_qk_norm_k=mla_qk_norm_k, w_out=w_out, loss_target=loss_target, m_norm_g=m_norm_g, m_w_in=m_w_in, m_ret_norm_g=m_ret_norm_g, m_gla_wa2_f=m_gla_wa2_f, m_gla_ba_f=m_gla_ba_f, m_gla_wa2_b=m_gla_wa2_b, m_gla_ba_b=m_gla_ba_b, m_gla_norm_g=m_gla_norm_g, m_pool_w=m_pool_w, m_pool_scale=m_pool_scale, m_mla_q_norm_g=m_mla_q_norm_g, m_mla_wq_b=m_mla_wq_b, m_mla_kv_norm_g=m_mla_kv_norm_g, m_mla_wkv_b=m_mla_wkv_b, m_mla_qk_norm_q=m_mla_qk_norm_q, m_mla_qk_norm_k=m_mla_qk_norm_k, m_w_out=m_w_out, v_norm_g=v_norm_g, v_w_in=v_w_in, v_ret_norm_g=v_ret_norm_g, v_gla_wa2_f=v_gla_wa2_f, v_gla_ba_f=v_gla_ba_f, v_gla_wa2_b=v_gla_wa2_b, v_gla_ba_b=v_gla_ba_b, v_gla_norm_g=v_gla_norm_g, v_pool_w=v_pool_w, v_pool_scale=v_pool_scale, v_mla_q_norm_g=v_mla_q_norm_g, v_mla_wq_b=v_mla_wq_b, v_mla_kv_norm_g=v_mla_kv_norm_g, v_mla_wkv_b=v_mla_wkv_b, v_mla_qk_norm_q=v_mla_qk_norm_q, v_mla_qk_norm_k=v_mla_qk_norm_k, v_w_out=v_w_out)
    weights = {n: given[n] for n in TWIN_WEIGHTS}
    shared = {n: given[n] for n in SHARED_INPUTS}
    per_example = {n: given[n] for n in ['x']}
    grad_fn = _jax.value_and_grad(_loss, argnums=(0, 1))

    def one_microbatch(ex, loss_target):
        ex = dict(ex)
        diff = ex.pop(TWIN_DIFF_INPUT)
        return grad_fn(weights, diff, {**shared, **ex}, loss_target)

    if N_MICROBATCH == 1:
        loss, (grad_w, grad_x) = one_microbatch(per_example, given["loss_target"])
    else:
        def body(carry, xs):
            loss_sum, grad_sum = carry
            l_k, (gw_k, gx_k) = one_microbatch(xs[0], xs[1])
            with _jax.named_scope("update"):
                return (loss_sum + l_k, _jax.tree.map(_jnp.add, grad_sum, gw_k)), gx_k

        init = (_jnp.zeros((), _jnp.float32), _jax.tree.map(_jnp.zeros_like, weights))
        (loss, grad_w), grad_x = _jax.lax.scan(body, init, (per_example, given["loss_target"]))
    with _jax.named_scope("update"):
        delta_w, new_m, new_v = {}, {}, {}
        for n in TWIN_WEIGHTS:
            delta_w[n], new_m[n], new_v[n] = _adamw(weights[n], grad_w[n], given["m_" + n], given["v_" + n])
    return (loss, grad_x, *[grad_w[n] for n in TWIN_WEIGHTS], *[delta_w[n] for n in TWIN_WEIGHTS],
            *[new_m[n] for n in TWIN_WEIGHTS], *[new_v[n] for n in TWIN_WEIGHTS])
```

```python
import functools
import math

import numpy as np
import jax
import jax.numpy as jnp
from jax import lax
from jax.experimental import pallas as pl
from jax.experimental.pallas import tpu as pltpu

F32 = jnp.float32
BF16 = jnp.bfloat16

N_DEV = 8
D_MODEL = 2048
DEPTH = 2
GROUP_W = 512
EPS = 1e-6
ROPE_THETA = 10000.0
LANES = 128

RET_HD = 128
RET_CHUNK = 128
GLA_CHUNK = 64
GLA_DK = 64
GLA_TAU = 16.0
GLA_RANK = 16
POOL_WINDOWS = (2, 4, 8, 16)
MLA_QK = 192
MLA_ROPE = 64
IN_COLS = 5984

ADAM_LR = 0.001
ADAM_B1 = 0.9
ADAM_B2 = 0.999
ADAM_EPS = 1e-08
ADAM_WD = 0.01
ADAM_STEP = 10

A_Q, A_K, A_V, A_G = 0, 4, 8, 12
B_Q, B_K, B_V, B_G = 16, 18, 20, 24
M_Q, C_V, C_G, M_G = 28, 32, 36, 40
M_KV, GA, M_KR = 44, 46, 47
ZP_COLS = 48 * LANES

VMEM_LIMIT = 56 * 1024 * 1024


def _params(sem, vmem=VMEM_LIMIT):
    return pltpu.CompilerParams(dimension_semantics=sem, vmem_limit_bytes=vmem)


def _sigmoid(x):
    return 1.0 / (1.0 + jnp.exp(-x))


def _silu(x):
    return x * _sigmoid(x)


def _silu_grad(x):
    s = _sigmoid(x)
    return s * (1.0 + x * (1.0 - s))


def _dot(a, b, dims=(((1,), (0,)), ((), ()))):
    return lax.dot_general(a, b, dims, preferred_element_type=F32)


NT = (((1,), (1,)), ((), ()))
TN = (((0,), (0,)), ((), ()))


def _dot_exact(a, b):
    return lax.dot_general(a, b, (((1,), (0,)), ((), ())), precision=lax.Precision.HIGHEST,
                           preferred_element_type=F32)


def _roll_lanes_half(x):
    return pltpu.roll(x, 64, 1)


def _pad_w_in(w):
    z = lambda n: jnp.zeros((w.shape[0], n), w.dtype)
    return jnp.concatenate([
        w[:, 0:3584],
        w[:, 4640:5152], w[:, 3616:4640], w[:, 5472:5984], w[:, 5152:5408],
        w[:, 3584:3616], z(96),
        w[:, 5408:5440], z(32), w[:, 5440:5472], z(32)], axis=1)


def _unpad_w_in(wp):
    b = lambda blk, n: wp[:, blk * LANES:blk * LANES + n]
    return jnp.concatenate([wp[:, 0:3584], b(GA, 32), b(C_V, 1024), b(M_Q, 512), b(M_KV, 256),
                            b(M_KR, 32), wp[:, M_KR * LANES + 64:M_KR * LANES + 96], b(M_G, 512)], axis=1)


def _wq_perm():
    idx = np.zeros((1024,), np.int32)
    ok = np.zeros((1024,), bool)
    for h in range(4):
        idx[128 * h:128 * h + 128] = 192 * h + np.arange(128)
        ok[128 * h:128 * h + 128] = True
        base = 512 + 128 * h
        idx[base:base + 32] = 192 * h + 128 + np.arange(32)
        ok[base:base + 32] = True
        idx[base + 64:base + 96] = 192 * h + 160 + np.arange(32)
        ok[base + 64:base + 96] = True
    inv = np.zeros((768,), np.int32)
    inv[idx[ok]] = np.nonzero(ok)[0]
    return idx, ok, inv


_WQ_IDX, _WQ_OK, _WQ_INV = _wq_perm()


def _pad_wq(wq):
    return jnp.where(jnp.asarray(_WQ_OK)[None, :], wq[:, _WQ_IDX], 0).astype(wq.dtype)


def _unpad_wq(wqp):
    return wqp[:, _WQ_INV]


def _qk_idx():
    idx = np.zeros((256,), np.int32)
    ok = np.zeros((256,), bool)
    idx[0:128] = np.arange(128)
    ok[0:128] = True
    idx[128:160] = 128 + np.arange(32)
    ok[128:160] = True
    idx[192:224] = 160 + np.arange(32)
    ok[192:224] = True
    inv = np.zeros((192,), np.int32)
    inv[idx[ok]] = np.nonzero(ok)[0]
    return idx, ok, inv


_QK_IDX, _QK_OK, _QK_INV = _qk_idx()


def _pad_qk_gain(g):
    return jnp.where(jnp.asarray(_QK_OK), g[_QK_IDX], 0.0).reshape(1, 256)


def _rope_tables(s):
    def tabs(dim):
        inv = 1.0 / (ROPE_THETA ** (jnp.arange(0, dim, 2, dtype=F32) / dim))
        ang = jnp.arange(s, dtype=F32)[:, None] * inv[None, :]
        return jnp.cos(ang), jnp.sin(ang)
    cr, sr = tabs(RET_HD)
    cos_r = jnp.concatenate([cr, cr], axis=1)
    sin_r = jnp.concatenate([-sr, sr], axis=1)
    cm, sm = tabs(MLA_ROPE)
    zz = jnp.zeros_like(cm)
    cos_m = jnp.concatenate([cm, zz, cm, zz], axis=1)
    sin_m = jnp.concatenate([-sm, zz, sm, zz], axis=1)
    return cos_r, sin_r, cos_m, sin_m


def _rope(x, cos, sin):
    return x * cos + _roll_lanes_half(x) * sin


def _rope_t(x, cos, sin):
    return x * cos + _roll_lanes_half(x * sin)


def _ret_tables():
    c = RET_CHUNK
    gamma_f = 1.0 - 2.0 ** (-5.0 - jnp.arange(4, dtype=F32))
    gamma_b = gamma_f[::-1]
    idx = jnp.arange(c, dtype=F32)
    diff = idx[:, None] - idx[None, :]

    def build(g1, g2):
        l1 = jnp.log(g1)[:, None, None]
        l2 = jnp.log(g2)[:, None, None]
        d1 = jnp.where(diff >= 0, jnp.exp(jnp.maximum(diff, 0.0)[None] * l1), 0.0)
        d2 = jnp.where(diff <= 0, jnp.exp(jnp.maximum(-diff, 0.0)[None] * l2), 0.0)
        ones = jnp.ones((1, c, c), F32)
        col = idx[None, :, None]
        qdf = jnp.exp((col + 1.0) * l1) * ones
        kdf = jnp.exp((c - 1.0 - col) * l1) * ones
        qdb = jnp.exp((c - col) * l2) * ones
        kdb = jnp.exp(col * l2) * ones
        cd1 = jnp.exp(c * l1) * ones
        cd2 = jnp.exp(c * l2) * ones
        return jnp.stack([d1 + d2, qdf, kdf, qdb, kdb, cd1, cd2], axis=1)

    return build(gamma_f, gamma_b), build(gamma_b, gamma_f)


def _inproj(x, g, wp, tm, tn=512):
    s, d = x.shape
    n = wp.shape[1]

    def body(x_ref, g_ref, w_ref, z_ref, h_ref, hs):
        @pl.when(pl.program_id(1) == 0)
        def _():
            xv = x_ref[...]
            r = lax.rsqrt(jnp.mean(xv * xv, axis=-1, keepdims=True) + EPS)
            hv = (xv * r * g_ref[...]).astype(BF16)
            hs[...] = hv
            h_ref[...] = hv
        z_ref[...] = _dot(hs[...], w_ref[...])

    return pl.pallas_call(
        body, name="inproj", grid=(s // tm, n // tn),
        in_specs=[pl.BlockSpec((tm, d), lambda i, j: (i, 0)),
                  pl.BlockSpec((1, d), lambda i, j: (0, 0)),
                  pl.BlockSpec((d, tn), lambda i, j: (0, j))],
        out_specs=[pl.BlockSpec((tm, tn), lambda i, j: (i, j)),
                   pl.BlockSpec((tm, d), lambda i, j: (i, 0))],
        out_shape=[jax.ShapeDtypeStruct((s, n), F32), jax.ShapeDtypeStruct((s, d), BF16)],
        scratch_shapes=[pltpu.VMEM((tm, d), BF16)],
        compiler_params=_params(("parallel", "arbitrary")),
    )(x, g, wp)


def _mm(a, b, mode, name, tm, tn, tk, add=None, out_dtype=F32):
    if mode == "tn":
        k, m = a.shape
    else:
        m, k = a.shape
    n = b.shape[0] if mode == "nt" else b.shape[1]
    tm, tn, tk = min(tm, m), min(tn, n), min(tk, k)
    nk = k // tk
    dims = {"nn": (((1,), (0,)), ((), ())), "nt": NT, "tn": TN}[mode]

    def body(*refs):
        if add is None:
            a_ref, b_ref, o_ref, acc = refs
        else:
            a_ref, b_ref, add_ref, o_ref, acc = refs
        kk = pl.program_id(2)

        @pl.when(kk == 0)
        def _():
            acc[...] = jnp.zeros_like(acc)

        acc[...] += _dot(a_ref[...].astype(BF16), b_ref[...].astype(BF16), dims)

        @pl.when(kk == nk - 1)
        def _():
            r = acc[...]
            if add is not None:
                r = r + add_ref[...]
            o_ref[...] = r.astype(out_dtype)

    a_spec = (pl.BlockSpec((tk, tm), lambda i, j, kk: (kk, i)) if mode == "tn"
              else pl.BlockSpec((tm, tk), lambda i, j, kk: (i, kk)))
    b_spec = (pl.BlockSpec((tn, tk), lambda i, j, kk: (j, kk)) if mode == "nt"
              else pl.BlockSpec((tk, tn), lambda i, j, kk: (kk, j)))
    in_specs = [a_spec, b_spec]
    args = [a, b]
    if add is not None:
        in_specs.append(pl.BlockSpec((tm, tn), lambda i, j, kk: (i, j)))
        args.append(add)
    return pl.pallas_call(
        body, name=name, grid=(m // tm, n // tn, nk),
        in_specs=in_specs,
        out_specs=pl.BlockSpec((tm, tn), lambda i, j, kk: (i, j)),
        out_shape=jax.ShapeDtypeStruct((m, n), out_dtype),
        scratch_shapes=[pltpu.VMEM((tm, tn), F32)],
        compiler_params=_params(("parallel", "parallel", "arbitrary")),
    )(*args)


def _ret_core(q_ref, k_ref, v_ref, tab_ref, out_ref, st_ref, nchunk):
    c = RET_CHUNK

    def rows(n):
        return pl.ds(pl.multiple_of(n * c, c), c)

    st_ref[...] = jnp.zeros_like(st_ref)

    def fwd(n, carry):
        r = rows(n)
        q, k, vb = q_ref[r, :], k_ref[r, :], v_ref[r, :].astype(BF16)
        sc = _dot(q.astype(BF16), k.astype(BF16), NT) * tab_ref[0]
        o = _dot(sc.astype(BF16), vb)
        o = o + _dot((q * tab_ref[1]).astype(BF16), st_ref[...].astype(BF16))
        out_ref[r, :] = o
        st_ref[...] = st_ref[...] * tab_ref[5] + _dot((k * tab_ref[2]).astype(BF16), vb, TN)
        return carry

    lax.fori_loop(0, nchunk, fwd, 0)
    st_ref[...] = jnp.zeros_like(st_ref)

    def bwd(i, carry):
        r = rows(nchunk - 1 - i)
        q, k, vb = q_ref[r, :], k_ref[r, :], v_ref[r, :].astype(BF16)
        out_ref[r, :] += _dot((q * tab_ref[3]).astype(BF16), st_ref[...].astype(BF16))
        st_ref[...] = st_ref[...] * tab_ref[6] + _dot((k * tab_ref[4]).astype(BF16), vb, TN)
        return carry

    lax.fori_loop(0, nchunk, bwd, 0)


def _ret_fwd(z, cos_r, sin_r, tab, norm_g):
    s = z.shape[0]
    nchunk = s // RET_CHUNK
    scale = RET_HD ** -0.5
    col = lambda base: pl.BlockSpec((s, LANES), lambda h: (0, base + h), pipeline_mode=pl.Buffered(1))

    def body(q_ref, k_ref, v_ref, g_ref, cos_ref, sin_ref, tab_ref, ng_ref, o_ref, y_ref, qh, kh, st):
        qh[...] = _rope(q_ref[...], cos_ref[...], sin_ref[...])
        kh[...] = _rope(k_ref[...], cos_ref[...], sin_ref[...]) * scale
        _ret_core(qh, kh, v_ref, tab_ref, o_ref, st, nchunk)
        o = o_ref[...]
        r = lax.rsqrt(jnp.mean(o * o, axis=-1, keepdims=True) + EPS)
        y_ref[...] = (_silu(g_ref[...]) * (o * r * ng_ref[...])).astype(BF16)

    return pl.pallas_call(
        body, name="ret_fwd", grid=(4,),
        in_specs=[col(A_Q), col(A_K), col(A_V), col(A_G),
                  pl.BlockSpec((s, LANES), lambda h: (0, 0), pipeline_mode=pl.Buffered(1)),
                  pl.BlockSpec((s, LANES), lambda h: (0, 0), pipeline_mode=pl.Buffered(1)),
                  pl.BlockSpec((None, 7, LANES, LANES), lambda h: (h, 0, 0, 0)),
                  pl.BlockSpec((1, LANES), lambda h: (0, h))],
        out_specs=[pl.BlockSpec((s, LANES), lambda h: (0, h)), pl.BlockSpec((s, LANES), lambda h: (0, h))],
        out_shape=[jax.ShapeDtypeStruct((s, GROUP_W), F32), jax.ShapeDtypeStruct((s, GROUP_W), BF16)],
        scratch_shapes=[pltpu.VMEM((s, LANES), F32), pltpu.VMEM((s, LANES), F32), pltpu.VMEM((LANES, LANES), F32)],
        compiler_params=_params(("arbitrary",)),
    )(z, z, z, z, cos_r, sin_r, tab, norm_g)


def _ret_bwd(z, d_o, cos_r, sin_r, tab, tab_sw):
    s = z.shape[0]
    nchunk = s // RET_CHUNK
    scale = RET_HD ** -0.5
    col = lambda base: pl.BlockSpec((s, LANES), lambda h: (0, base + h), pipeline_mode=pl.Buffered(1))
    whole = lambda: pl.BlockSpec((s, LANES), lambda h: (0, 0), pipeline_mode=pl.Buffered(1))
    tabspec = lambda: pl.BlockSpec((None, 7, LANES, LANES), lambda h: (h, 0, 0, 0))
    outspec = lambda: pl.BlockSpec((s, LANES), lambda h: (0, h))

    def body(q_ref, k_ref, v_ref, do_ref, cos_ref, sin_ref, tab_ref, tsw_ref, dq_ref, dk_ref, dv_ref,
             qh, kh, tmp, st):
        cos, sin = cos_ref[...], sin_ref[...]
        qh[...] = _rope(q_ref[...], cos, sin)
        kh[...] = _rope(k_ref[...], cos, sin) * scale
        _ret_core(kh, qh, do_ref, tsw_ref, dv_ref, st, nchunk)
        _ret_core(do_ref, v_ref, kh, tab_ref, tmp, st, nchunk)
        dq_ref[...] = _rope_t(tmp[...], cos, sin)
        _ret_core(v_ref, do_ref, qh, tsw_ref, tmp, st, nchunk)
        dk_ref[...] = _rope_t(tmp[...] * scale, cos, sin)

    return pl.pallas_call(
        body, name="ret_bwd", grid=(4,),
        in_specs=[col(A_Q), col(A_K), col(A_V),
                  pl.BlockSpec((s, LANES), lambda h: (0, h), pipeline_mode=pl.Buffered(1)),
                  whole(), whole(), tabspec(), tabspec()],
        out_specs=[outspec(), outspec(), outspec()],
        out_shape=[jax.ShapeDtypeStruct((s, GROUP_W), F32)] * 3,
        scratch_shapes=[pltpu.VMEM((s, LANES), F32), pltpu.VMEM((s, LANES), F32), pltpu.VMEM((s, LANES), F32),
                        pltpu.VMEM((LANES, LANES), F32)],
        compiler_params=_params(("arbitrary",)),
    )(z, z, z, d_o, cos_r, sin_r, tab, tab_sw)


def _normgate_bwd(o, z, gate_blk, dy, dy_blk, norm_g, tm):
    s = o.shape[0]

    def body(o_ref, g_ref, dy_ref, ng_ref, do_ref, dg_ref, dng_ref):
        @pl.when(pl.program_id(0) == 0)
        def _():
            dng_ref[...] = jnp.zeros_like(dng_ref)

        for h in range(4):
            sl = slice(LANES * h, LANES * (h + 1))
            ov, gv, dyv, ng = o_ref[:, sl], g_ref[:, sl], dy_ref[:, sl], ng_ref[:, sl]
            r = lax.rsqrt(jnp.mean(ov * ov, axis=-1, keepdims=True) + EPS)
            on = ov * r
            dn = dyv * _silu(gv)
            u = dn * ng
            do_ref[:, sl] = r * (u - on * jnp.mean(u * on, axis=-1, keepdims=True))
            dg_ref[:, sl] = dyv * (on * ng) * _silu_grad(gv)
            dng_ref[:, sl] += jnp.sum(dn * on, axis=0, keepdims=True)

    return pl.pallas_call(
        body, name="normgate_bwd", grid=(s // tm,),
        in_specs=[pl.BlockSpec((tm, GROUP_W), lambda i: (i, 0)),
                  pl.BlockSpec((tm, GROUP_W), lambda i: (i, gate_blk // 4)),
                  pl.BlockSpec((tm, GROUP_W), lambda i: (i, dy_blk)),
                  pl.BlockSpec((1, GROUP_W), lambda i: (0, 0))],
        out_specs=[pl.BlockSpec((tm, GROUP_W), lambda i: (i, 0)), pl.BlockSpec((tm, GROUP_W), lambda i: (i, 0)),
                   pl.BlockSpec((1, GROUP_W), lambda i: (0, 0))],
        out_shape=[jax.ShapeDtypeStruct((s, GROUP_W), F32), jax.ShapeDtypeStruct((s, GROUP_W), F32),
                   jax.ShapeDtypeStruct((1, GROUP_W), F32)],
        compiler_params=_params(("arbitrary",)),
    )(o, z, dy, norm_g)


def _log_sigmoid(x):
    return jnp.minimum(x, 0.0) - jnp.log(1.0 + jnp.exp(-jnp.abs(x)))


def _gla_consts():
    c = GLA_CHUNK
    row = lax.broadcasted_iota(jnp.int32, (c, c), 0)
    colm = lax.broadcasted_iota(jnp.int32, (c, c), 1)
    lane = lax.broadcasted_iota(jnp.int32, (1, LANES), 1)
    low = row >= colm
    up = colm >= row
    heads = ((lane < GLA_DK).astype(F32), (lane >= GLA_DK).astype(F32))
    return low, up, heads


def _gla_chunk(q, k, la, tri_f):
    cum = _dot_exact(tri_f, la)
    last = jnp.sum(la, axis=0, keepdims=True)
    eq = jnp.exp(cum)
    ek = jnp.exp(-cum)
    el = jnp.exp(last - cum)
    dec = jnp.exp(last)
    return eq, ek, el, dec


def _gla_gates(ga_ref, wa_ref, ba_ref, la_ref, s, tm):
    def step(i, carry):
        r = pl.ds(pl.multiple_of(i * tm, tm), tm)
        pre = _dot(ga_ref[r, :].astype(BF16), wa_ref[...].astype(BF16)) + ba_ref[...]
        la_ref[r, :] = _log_sigmoid(pre) * (1.0 / GLA_TAU)
        return carry
    lax.fori_loop(0, s // tm, step, 0)


def _gla_fwd(z, wa_f, wa_b, ba_f, ba_b, norm_g):
    s = z.shape[0]
    c = GLA_CHUNK
    nchunk = s // c
    scale = GLA_DK ** -0.5
    tm = min(s, 512)
    one = pl.Buffered(1)

    def body(q_ref, k_ref, v_ref, ga_ref, g_ref, waf_ref, wab_ref, baf_ref, bab_ref, ng_ref, o_ref, y_ref,
             la_s, st):
        low, up, heads = _gla_consts()
        _gla_gates(ga_ref, waf_ref, baf_ref, la_s.at[0], s, tm)
        _gla_gates(ga_ref, wab_ref, bab_ref, la_s.at[1], s, tm)
        for d in range(2):
            tri = (low, up)[d]
            tri_f = tri.astype(F32)
            st[...] = jnp.zeros_like(st)

            def step(i, carry):
                n = i if d == 0 else nchunk - 1 - i
                r = pl.ds(pl.multiple_of(n * c, c), c)
                q = q_ref[r, :] * scale
                k = k_ref[r, :]
                eq, ek, el, dec = _gla_chunk(q, k, la_s[d, r, :], tri_f)
                qt = q * eq
                ktb = (k * ek).astype(BF16)
                kl = k * el
                for hh in range(2):
                    cols = slice(LANES * hh, LANES * (hh + 1))
                    vb = v_ref[r, cols].astype(BF16)
                    qm = (qt * heads[hh]).astype(BF16)
                    a = jnp.where(tri, _dot(qm, ktb, NT), 0.0)
                    o = _dot(a.astype(BF16), vb) + _dot(qm, st[hh].astype(BF16), NT)
                    if d == 0:
                        o_ref[r, cols] = o
                    else:
                        o_ref[r, cols] += o
                    st[hh] = st[hh] * dec + _dot(vb, (kl * heads[hh]).astype(BF16), TN)
                return carry

            lax.fori_loop(0, nchunk, step, 0)

        def epi(i, carry):
            r = pl.ds(pl.multiple_of(i * tm, tm), tm)
            for hh in range(2):
                cols = slice(LANES * hh, LANES * (hh + 1))
                o = o_ref[r, cols]
                rr = lax.rsqrt(jnp.mean(o * o, axis=-1, keepdims=True) + EPS)
                y_ref[r, cols] = (_silu(g_ref[r, cols]) * (o * rr * ng_ref[:, cols])).astype(BF16)
            return carry

        lax.fori_loop(0, s // tm, epi, 0)

    w2 = 2 * LANES
    return pl.pallas_call(
        body, name="gla_fwd", grid=(2,),
        in_specs=[pl.BlockSpec((s, LANES), lambda p: (0, B_Q + p), pipeline_mode=one),
                  pl.BlockSpec((s, LANES), lambda p: (0, B_K + p), pipeline_mode=one),
                  pl.BlockSpec((s, w2), lambda p: (0, B_V // 2 + p), pipeline_mode=one),
                  pl.BlockSpec((s, LANES), lambda p: (0, GA), pipeline_mode=one),
                  pl.BlockSpec((s, w2), lambda p: (0, B_G // 2 + p), pipeline_mode=one),
                  pl.BlockSpec((LANES, LANES), lambda p: (0, p)),
                  pl.BlockSpec((LANES, LANES), lambda p: (0, p)),
                  pl.BlockSpec((1, LANES), lambda p: (0, p)),
                  pl.BlockSpec((1, LANES), lambda p: (0, p)),
                  pl.BlockSpec((1, w2), lambda p: (0, p))],
        out_specs=[pl.BlockSpec((s, w2), lambda p: (0, p)), pl.BlockSpec((s, w2), lambda p: (0, p))],
        out_shape=[jax.ShapeDtypeStruct((s, GROUP_W), F32), jax.ShapeDtypeStruct((s, GROUP_W), BF16)],
        scratch_shapes=[pltpu.VMEM((2, s, LANES), F32), pltpu.VMEM((2, LANES, LANES), F32)],
        compiler_params=_params(("arbitrary",)),
    )(z, z, z, z, z, wa_f, wa_b, ba_f, ba_b, norm_g)


def _gla_bwd(z, d_o, wa_f, wa_b, ba_f, ba_b):
    s = z.shape[0]
    c = GLA_CHUNK
    nchunk = s // c
    scale = GLA_DK ** -0.5
    tm = min(s, 512)
    one = pl.Buffered(1)

    def body(q_ref, k_ref, v_ref, ga_ref, do_ref, waf_ref, wab_ref, baf_ref, bab_ref,
             dq_ref, dk_ref, dv_ref, dga_ref, dwaf_ref, dwab_ref, dbaf_ref, dbab_ref,
             la_s, dla_s, stash, st):
        low, up, heads = _gla_consts()
        rowi = lax.broadcasted_iota(jnp.int32, (c, 1), 0)
        _gla_gates(ga_ref, waf_ref, baf_ref, la_s.at[0], s, tm)
        _gla_gates(ga_ref, wab_ref, bab_ref, la_s.at[1], s, tm)
        for d in range(2):
            tri = (low, up)[d]
            tri_f = tri.astype(F32)
            tri_t = (up, low)[d].astype(F32)
            last_row = (rowi == (c - 1 if d == 0 else 0)).astype(F32)
            order = (lambda i: i) if d == 0 else (lambda i: nchunk - 1 - i)
            st[...] = jnp.zeros_like(st)

            def states(i, carry):
                n = order(i)
                r = pl.ds(pl.multiple_of(n * c, c), c)
                k = k_ref[r, :]
                la = la_s[d, r, :]
                _, _, el, dec = _gla_chunk(k, k, la, tri_f)
                kl = k * el
                for hh in range(2):
                    cols = slice(LANES * hh, LANES * (hh + 1))
                    stash[hh, n] = st[hh]
                    st[hh] = st[hh] * dec + _dot(v_ref[r, cols].astype(BF16), (kl * heads[hh]).astype(BF16), TN)
                return carry

            lax.fori_loop(0, nchunk, states, 0)
            st[...] = jnp.zeros_like(st)

            def step(i, carry):
                n = order(nchunk - 1 - i)
                r = pl.ds(pl.multiple_of(n * c, c), c)
                q = q_ref[r, :] * scale
                k = k_ref[r, :]
                eq, ek, el, dec = _gla_chunk(q, k, la_s[d, r, :], tri_f)
                qt = q * eq
                kt = k * ek
                kl = k * el
                ktb = kt.astype(BF16)
                dqt = jnp.zeros((c, LANES), F32)
                dkt = jnp.zeros((c, LANES), F32)
                dkl = jnp.zeros((c, LANES), F32)
                ddec = jnp.zeros((1, LANES), F32)
                for hh in range(2):
                    cols = slice(LANES * hh, LANES * (hh + 1))
                    vb = v_ref[r, cols].astype(BF16)
                    dob = do_ref[r, cols].astype(BF16)
                    qm = (qt * heads[hh]).astype(BF16)
                    a = jnp.where(tri, _dot(qm, ktb, NT), 0.0).astype(BF16)
                    da = jnp.where(tri, _dot(dob, vb, NT), 0.0).astype(BF16)
                    sn = stash[hh, n]
                    dst = st[hh]
                    dstb = dst.astype(BF16)
                    dqt = dqt + (_dot(da, ktb) + _dot(dob, sn.astype(BF16))) * heads[hh]
                    dkt = dkt + _dot(da, qm, TN)
                    dv = _dot(a, dob, TN) + _dot((kl * heads[hh]).astype(BF16), dstb, NT)
                    dkl = dkl + _dot(vb, dstb)
                    ddec = ddec + jnp.sum(dst * sn, axis=0, keepdims=True)
                    st[hh] = dst * dec + _dot(dob, qm, TN)
                    if d == 0:
                        dv_ref[r, cols] = dv
                    else:
                        dv_ref[r, cols] += dv
                dlast = ddec * dec + jnp.sum(dkl * kl, axis=0, keepdims=True)
                dq = dqt * eq * scale
                dk = dkt * ek + dkl * el
                dcum = dqt * qt - dkt * kt - dkl * kl + last_row * dlast
                dla_s[d, r, :] = _dot_exact(tri_t, dcum)
                if d == 0:
                    dq_ref[r, :] = dq
                    dk_ref[r, :] = dk
                else:
                    dq_ref[r, :] += dq
                    dk_ref[r, :] += dk
                return carry

            lax.fori_loop(0, nchunk, step, 0)

        first = pl.program_id(0) == 0
        for d, (wa_ref, ba_ref, dwa_ref, dba_ref) in enumerate(
                ((waf_ref, baf_ref, dwaf_ref, dbaf_ref), (wab_ref, bab_ref, dwab_ref, dbab_ref))):
            dwa_ref[...] = jnp.zeros_like(dwa_ref)
            dba_ref[...] = jnp.zeros_like(dba_ref)

            def gates(i, carry):
                r = pl.ds(pl.multiple_of(i * tm, tm), tm)
                gab = ga_ref[r, :].astype(BF16)
                wab16 = wa_ref[...].astype(BF16)
                pre = _dot(gab, wab16) + ba_ref[...]
                dpre = dla_s[d, r, :] * (1.0 / GLA_TAU) * _sigmoid(-pre)
                dpb = dpre.astype(BF16)
                dwa_ref[...] += _dot(gab, dpb, TN)
                dba_ref[...] += jnp.sum(dpre, axis=0, keepdims=True)
                dga = _dot(dpb, wab16, NT)
                if d == 0:
                    @pl.when(first)
                    def _():
                        dga_ref[r, :] = dga

                    @pl.when(jnp.logical_not(first))
                    def _():
                        dga_ref[r, :] += dga
                else:
                    dga_ref[r, :] += dga
                return carry

            lax.fori_loop(0, s // tm, gates, 0)

    w2 = 2 * LANES
    return pl.pallas_call(
        body, name="gla_bwd", grid=(2,),
        in_specs=[pl.BlockSpec((s, LANES), lambda p: (0, B_Q + p), pipeline_mode=one),
                  pl.BlockSpec((s, LANES), lambda p: (0, B_K + p), pipeline_mode=one),
                  pl.BlockSpec((s, w2), lambda p: (0, B_V // 2 + p), pipeline_mode=one),
                  pl.BlockSpec((s, LANES), lambda p: (0, GA), pipeline_mode=one),
                  pl.BlockSpec((s, w2), lambda p: (0, p), pipeline_mode=one),
                  pl.BlockSpec((LANES, LANES), lambda p: (0, p)),
                  pl.BlockSpec((LANES, LANES), lambda p: (0, p)),
                  pl.BlockSpec((1, LANES), lambda p: (0, p)),
                  pl.BlockSpec((1, LANES), lambda p: (0, p))],
        out_specs=[pl.BlockSpec((s, LANES), lambda p: (0, p), pipeline_mode=one),
                   pl.BlockSpec((s, LANES), lambda p: (0, p), pipeline_mode=one),
                   pl.BlockSpec((s, w2), lambda p: (0, p), pipeline_mode=one),
                   pl.BlockSpec((s, LANES), lambda p: (0, 0), pipeline_mode=one),
                   pl.BlockSpec((LANES, LANES), lambda p: (0, p)),
                   pl.BlockSpec((LANES, LANES), lambda p: (0, p)),
                   pl.BlockSpec((1, LANES), lambda p: (0, p)),
                   pl.BlockSpec((1, LANES), lambda p: (0, p))],
        out_shape=[jax.ShapeDtypeStruct((s, w2), F32), jax.ShapeDtypeStruct((s, w2), F32),
                   jax.ShapeDtypeStruct((s, GROUP_W), F32), jax.ShapeDtypeStruct((s, LANES), F32),
                   jax.ShapeDtypeStruct((LANES, w2), F32), jax.ShapeDtypeStruct((LANES, w2), F32),
                   jax.ShapeDtypeStruct((1, w2), F32), jax.ShapeDtypeStruct((1, w2), F32)],
        scratch_shapes=[pltpu.VMEM((2, s, LANES), F32), pltpu.VMEM((2, s, LANES), F32),
                        pltpu.VMEM((2, nchunk, LANES, LANES), F32), pltpu.VMEM((2, LANES, LANES), F32)],
        compiler_params=_params(("arbitrary",)),
    )(z, z, z, z, d_o, wa_f, wa_b, ba_f, ba_b)


def _shift_rows(x, d, rowi):
    s = x.shape[0]
    if d == 0:
        return x
    y = pltpu.roll(x, d % s, 0)
    keep = (rowi >= d) if d > 0 else (rowi < s + d)
    return jnp.where(keep, y, 0.0)


def _run_sum(x, m, step, rowi):
    acc, n = x, 1
    while n < m:
        acc = acc + _shift_rows(acc, step * n, rowi)
        n *= 2
    return acc


def _pool_counts(s, w, rowi):
    hi = jnp.minimum(rowi + w // 2, s)
    lo = jnp.maximum(rowi - w // 2, 0)
    return (hi - lo).astype(F32)


def _pooled(u, w, rowi):
    s = u.shape[0]
    win = _shift_rows(_run_sum(u, w // 2, 1, rowi), 1, rowi) + _run_sum(u, w // 2, -1, rowi)
    return win / _pool_counts(s, w, rowi) - u


def _pool_fwd(z, pool_w, pool_scale):
    s = z.shape[0]
    one = pl.Buffered(1)

    def body(u_ref, g_ref, w_ref, sc_ref, y_ref):
        rowi = lax.broadcasted_iota(jnp.int32, (s, 1), 0)
        for g, w in enumerate(POOL_WINDOWS):
            cols = slice(LANES * g, LANES * (g + 1))
            pooled = _pooled(u_ref[:, cols], w, rowi)
            mixed = _dot(pooled.astype(BF16), w_ref[g].astype(BF16))
            y_ref[:, cols] = (_silu(g_ref[:, cols]) * (mixed * sc_ref[:, cols])).astype(BF16)

    return pl.pallas_call(
        body, name="pool_fwd", grid=(1,),
        in_specs=[pl.BlockSpec((s, GROUP_W), lambda i: (0, C_V // 4), pipeline_mode=one),
                  pl.BlockSpec((s, GROUP_W), lambda i: (0, C_G // 4), pipeline_mode=one),
                  pl.BlockSpec((4, LANES, LANES), lambda i: (0, 0, 0)),
                  pl.BlockSpec((1, GROUP_W), lambda i: (0, 0))],
        out_specs=pl.BlockSpec((s, GROUP_W), lambda i: (0, 0), pipeline_mode=one),
        out_shape=jax.ShapeDtypeStruct((s, GROUP_W), BF16),
        compiler_params=_params(("arbitrary",)),
    )(z, z, pool_w, pool_scale)


def _pool_bwd(z, dy, pool_w, pool_scale):
    s = z.shape[0]
    one = pl.Buffered(1)

    def body(u_ref, g_ref, dy_ref, w_ref, sc_ref, du_ref, dg_ref, dw_ref, dsc_ref):
        rowi = lax.broadcasted_iota(jnp.int32, (s, 1), 0)
        for g, w in enumerate(POOL_WINDOWS):
            cols = slice(LANES * g, LANES * (g + 1))
            gate, dyv, sc = g_ref[:, cols], dy_ref[:, cols], sc_ref[:, cols]
            wb = w_ref[g].astype(BF16)
            pooled = _pooled(u_ref[:, cols], w, rowi)
            pb = pooled.astype(BF16)
            mixed = _dot(pb, wb)
            dg_ref[:, cols] = dyv * (mixed * sc) * _silu_grad(gate)
            dt = dyv * _silu(gate)
            dsc_ref[:, cols] = jnp.sum(dt * mixed, axis=0, keepdims=True)
            dmb = (dt * sc).astype(BF16)
            dw_ref[g] = _dot(pb, dmb, TN)
            dpool = _dot(dmb, wb, NT)
            e = dpool / _pool_counts(s, w, rowi)
            du_ref[:, cols] = (_run_sum(e, w // 2, 1, rowi) + _shift_rows(_run_sum(e, w // 2, -1, rowi), -1, rowi)
                               - dpool)

    return pl.pallas_call(
        body, name="pool_bwd", grid=(1,),
        in_specs=[pl.BlockSpec((s, GROUP_W), lambda i: (0, C_V // 4), pipeline_mode=one),
                  pl.BlockSpec((s, GROUP_W), lambda i: (0, C_G // 4), pipeline_mode=one),
                  pl.BlockSpec((s, GROUP_W), lambda i: (0, 2), pipeline_mode=one),
                  pl.BlockSpec((4, LANES, LANES), lambda i: (0, 0, 0)),
                  pl.BlockSpec((1, GROUP_W), lambda i: (0, 0))],
        out_specs=[pl.BlockSpec((s, GROUP_W), lambda i: (0, 0), pipeline_mode=one),
                   pl.BlockSpec((s, GROUP_W), lambda i: (0, 0), pipeline_mode=one),
                   pl.BlockSpec((4, LANES, LANES), lambda i: (0, 0, 0)),
                   pl.BlockSpec((1, GROUP_W), lambda i: (0, 0))],
        out_shape=[jax.ShapeDtypeStruct((s, GROUP_W), F32), jax.ShapeDtypeStruct((s, GROUP_W), F32),
                   jax.ShapeDtypeStruct((4, LANES, LANES), F32), jax.ShapeDtypeStruct((1, GROUP_W), F32)],
        compiler_params=_params(("arbitrary",)),
    )(z, z, dy, pool_w, pool_scale)


def _mla_heads(qf, kv, kpe, qg, kg, cos, sin):
    out = []
    for h in range(4):
        qa = qf[:, LANES * h:LANES * (h + 1)]
        qb = qf[:, 512 + LANES * h:512 + LANES * (h + 1)]
        ka = kv[:, 256 * h:256 * h + LANES]
        rq = lax.rsqrt((jnp.sum(qa * qa, axis=-1, keepdims=True) + jnp.sum(qb * qb, axis=-1, keepdims=True))
                       * (1.0 / MLA_QK) + EPS)
        rk = lax.rsqrt((jnp.sum(ka * ka, axis=-1, keepdims=True) + jnp.sum(kpe * kpe, axis=-1, keepdims=True))
                       * (1.0 / MLA_QK) + EPS)
        out.append((qa, qb, rq, ka, rk))
    return out


def _mla_latents(mq_ref, mkv_ref, gq_ref, gkv_ref, wq_ref, wkv_ref):
    mq = mq_ref[...]
    rq = lax.rsqrt(jnp.mean(mq * mq, axis=-1, keepdims=True) + EPS)
    qn = mq * rq
    qnb = (qn * gq_ref[...]).astype(BF16)
    mkv = mkv_ref[...]
    rk = lax.rsqrt(jnp.mean(mkv * mkv, axis=-1, keepdims=True) + EPS)
    kvn = mkv * rk
    kvnb = (kvn * gkv_ref[...]).astype(BF16)
    qf = _dot(qnb, wq_ref[...])
    kv = _dot(kvnb, wkv_ref[...])
    return qn, rq, qnb, kvn, rk, kvnb, qf, kv


def _mla_prep(z, cos_m, sin_m, gq, wq, gkv, wkv, qg, kg, tm):
    s = z.shape[0]

    def body(mq_ref, mkv_ref, mkr_ref, cos_ref, sin_ref, gq_ref, wq_ref, gkv_ref, wkv_ref, qg_ref, kg_ref,
             q_ref, k_ref, v_ref):
        _, _, _, _, _, _, qf, kv = _mla_latents(mq_ref, mkv_ref, gq_ref, gkv_ref, wq_ref, wkv_ref)
        kpe = mkr_ref[...]
        cos, sin = cos_ref[...], sin_ref[...]
        qg, kg = qg_ref[...], kg_ref[...]
        for h, (qa, qb, rq, ka, rk) in enumerate(_mla_heads(qf, kv, kpe, qg, kg, cos, sin)):
            q_ref[h, :, 0:LANES] = (qa * rq * qg[:, 0:LANES]).astype(BF16)
            q_ref[h, :, LANES:] = _rope(qb * rq * qg[:, LANES:], cos, sin).astype(BF16)
            k_ref[h, :, 0:LANES] = (ka * rk * kg[:, 0:LANES]).astype(BF16)
            k_ref[h, :, LANES:] = _rope(kpe * rk * kg[:, LANES:], cos, sin).astype(BF16)
            v_ref[h] = kv[:, 256 * h + LANES:256 * (h + 1)].astype(BF16)

    full = lambda shape: pl.BlockSpec(shape, lambda i: (0,) * len(shape))
    return pl.pallas_call(
        body, name="mla_prep", grid=(s // tm,),
        in_specs=[pl.BlockSpec((tm, 512), lambda i: (i, M_Q // 4)),
                  pl.BlockSpec((tm, 256), lambda i: (i, M_KV // 2)),
                  pl.BlockSpec((tm, LANES), lambda i: (i, M_KR)),
                  pl.BlockSpec((tm, LANES), lambda i: (i, 0)),
                  pl.BlockSpec((tm, LANES), lambda i: (i, 0)),
                  full((1, 512)), full((512, 1024)), full((1, 256)), full((256, 1024)), full((1, 256)), full((1, 256))],
        out_specs=[pl.BlockSpec((4, tm, 256), lambda i: (0, i, 0)), pl.BlockSpec((4, tm, 256), lambda i: (0, i, 0)),
                   pl.BlockSpec((4, tm, LANES), lambda i: (0, i, 0))],
        out_shape=[jax.ShapeDtypeStruct((4, s, 256), BF16), jax.ShapeDtypeStruct((4, s, 256), BF16),
                   jax.ShapeDtypeStruct((4, s, LANES), BF16)],
        compiler_params=_params(("parallel",)),
    )(z, z, z, cos_m, sin_m, gq, wq, gkv, wkv, qg, kg)


def _mla_prep_bwd(z, cos_m, sin_m, gq, wq, gkv, wkv, qg, kg, dq, dk, dv, tm):
    s = z.shape[0]

    def body(mq_ref, mkv_ref, mkr_ref, cos_ref, sin_ref, gq_ref, wq_ref, gkv_ref, wkv_ref, qg_ref, kg_ref,
             dq_ref, dk_ref, dv_ref,
             dmq_ref, dmkv_ref, dmkr_ref, dwq_ref, dwkv_ref, dgq_ref, dgkv_ref, dqg_ref, dkg_ref, dqf, dkv):
        @pl.when(pl.program_id(0) == 0)
        def _():
            for r in (dwq_ref, dwkv_ref, dgq_ref, dgkv_ref, dqg_ref, dkg_ref):
                r[...] = jnp.zeros_like(r)

        qn, rq0, qnb, kvn, rk0, kvnb, qf, kv = _mla_latents(mq_ref, mkv_ref, gq_ref, gkv_ref, wq_ref, wkv_ref)
        kpe = mkr_ref[...]
        cos, sin = cos_ref[...], sin_ref[...]
        qg, kg = qg_ref[...], kg_ref[...]
        dkpe = jnp.zeros_like(kpe)
        inv = 1.0 / MLA_QK

        def norm_bwd(a, b, r, da_n, db_n, g):
            ga, gb = g[:, 0:LANES], g[:, LANES:]
            dg_a = jnp.sum(da_n * a * r, axis=0, keepdims=True)
            dg_b = jnp.sum(db_n * b * r, axis=0, keepdims=True)
            ua, ub = da_n * ga, db_n * gb
            dt = (jnp.sum(ua * a, axis=-1, keepdims=True) + jnp.sum(ub * b, axis=-1, keepdims=True)) * inv
            r3 = r * r * r
            return r * ua - a * (r3 * dt), r * ub - b * (r3 * dt), dg_a, dg_b

        for h, (qa, qb, rq, ka, rk) in enumerate(_mla_heads(qf, kv, kpe, qg, kg, cos, sin)):
            dqa, dqb, dga, dgb = norm_bwd(qa, qb, rq, dq_ref[h, :, 0:LANES], _rope_t(dq_ref[h, :, LANES:], cos, sin), qg)
            dqf[:, LANES * h:LANES * (h + 1)] = dqa
            dqf[:, 512 + LANES * h:512 + LANES * (h + 1)] = dqb
            dqg_ref[:, 0:LANES] += dga
            dqg_ref[:, LANES:] += dgb
            dka, dkb, dga, dgb = norm_bwd(ka, kpe, rk, dk_ref[h, :, 0:LANES], _rope_t(dk_ref[h, :, LANES:], cos, sin), kg)
            dkv[:, 256 * h:256 * h + LANES] = dka
            dkv[:, 256 * h + LANES:256 * (h + 1)] = dv_ref[h]
            dkpe = dkpe + dkb
            dkg_ref[:, 0:LANES] += dga
            dkg_ref[:, LANES:] += dgb
        dmkr_ref[...] = dkpe

        def latent_bwd(dfull, w_ref, nb, n, r, g_ref, dw_ref, dg_ref, dlat_ref):
            db = dfull.astype(BF16)
            dn = _dot(db, w_ref[...], NT)
            dw_ref[...] += _dot(nb, db, TN)
            dg_ref[...] += jnp.sum(dn * n, axis=0, keepdims=True)
            u = dn * g_ref[...]
            dlat_ref[...] = r * (u - n * jnp.mean(u * n, axis=-1, keepdims=True))

        latent_bwd(dqf[...], wq_ref, qnb, qn, rq0, gq_ref, dwq_ref, dgq_ref, dmq_ref)
        latent_bwd(dkv[...], wkv_ref, kvnb, kvn, rk0, gkv_ref, dwkv_ref, dgkv_ref, dmkv_ref)

    full = lambda shape: pl.BlockSpec(shape, lambda i: (0,) * len(shape))
    return pl.pallas_call(
        body, name="mla_prep_bwd", grid=(s // tm,),
        in_specs=[pl.BlockSpec((tm, 512), lambda i: (i, M_Q // 4)),
                  pl.BlockSpec((tm, 256), lambda i: (i, M_KV // 2)),
                  pl.BlockSpec((tm, LANES), lambda i: (i, M_KR)),
                  pl.BlockSpec((tm, LANES), lambda i: (i, 0)),
                  pl.BlockSpec((tm, LANES), lambda i: (i, 0)),
                  full((1, 512)), full((512, 1024)), full((1, 256)), full((256, 1024)), full((1, 256)), full((1, 256)),
                  pl.BlockSpec((4, tm, 256), lambda i: (0, i, 0)), pl.BlockSpec((4, tm, 256), lambda i: (0, i, 0)),
                  pl.BlockSpec((4, tm, LANES), lambda i: (0, i, 0))],
        out_specs=[pl.BlockSpec((tm, 512), lambda i: (i, 0)), pl.BlockSpec((tm, 256), lambda i: (i, 0)),
                   pl.BlockSpec((tm, LANES), lambda i: (i, 0)),
                   full((512, 1024)), full((256, 1024)), full((1, 512)), full((1, 256)), full((1, 256)), full((1, 256))],
        out_shape=[jax.ShapeDtypeStruct((s, 512), F32), jax.ShapeDtypeStruct((s, 256), F32),
                   jax.ShapeDtypeStruct((s, LANES), F32),
                   jax.ShapeDtypeStruct((512, 1024), F32), jax.ShapeDtypeStruct((256, 1024), F32),
                   jax.ShapeDtypeStruct((1, 512), F32), jax.ShapeDtypeStruct((1, 256), F32),
                   jax.ShapeDtypeStruct((1, 256), F32), jax.ShapeDtypeStruct((1, 256), F32)],
        scratch_shapes=[pltpu.VMEM((tm, 1024), F32), pltpu.VMEM((tm, 1024), F32)],
        compiler_params=_params(("arbitrary",)),
    )(z, z, z, cos_m, sin_m, gq, wq, gkv, wkv, qg, kg, dq, dk, dv)


def _attn_fwd(q, k, v, z, tq):
    s = q.shape[1]
    scale = MLA_QK ** -0.5

    def body(q_ref, k_ref, v_ref, g_ref, o_ref, y_ref, lse_ref):
        sc = _dot(q_ref[...], k_ref[...], NT) * scale
        m = jnp.max(sc, axis=-1, keepdims=True)
        p = jnp.exp(sc - m)
        l = jnp.sum(p, axis=-1, keepdims=True)
        o = _dot(p.astype(BF16), v_ref[...]) / l
        o_ref[...] = o
        y_ref[...] = (_silu(g_ref[...]) * o).astype(BF16)
        lse_ref[...] = m + jnp.log(l)

    return pl.pallas_call(
        body, name="attn_fwd", grid=(4, s // tq),
        in_specs=[pl.BlockSpec((None, tq, 256), lambda h, i: (h, i, 0)),
                  pl.BlockSpec((None, s, 256), lambda h, i: (h, 0, 0)),
                  pl.BlockSpec((None, s, LANES), lambda h, i: (h, 0, 0)),
                  pl.BlockSpec((tq, LANES), lambda h, i: (i, M_G + h))],
        out_specs=[pl.BlockSpec((tq, LANES), lambda h, i: (i, h)), pl.BlockSpec((tq, LANES), lambda h, i: (i, h)),
                   pl.BlockSpec((None, tq, 1), lambda h, i: (h, i, 0))],
        out_shape=[jax.ShapeDtypeStruct((s, GROUP_W), F32), jax.ShapeDtypeStruct((s, GROUP_W), BF16),
                   jax.ShapeDtypeStruct((4, s, 1), F32)],
        compiler_params=_params(("parallel", "parallel")),
    )(q, k, v, z)


def _attn_bwd(q, k, v, z, o, lse, dy, tq):
    s = q.shape[1]
    scale = MLA_QK ** -0.5

    def body(q_ref, k_ref, v_ref, g_ref, o_ref, lse_ref, dy_ref, dq_ref, dk_ref, dv_ref, dg_ref):
        @pl.when(pl.program_id(1) == 0)
        def _():
            dk_ref[...] = jnp.zeros_like(dk_ref)
            dv_ref[...] = jnp.zeros_like(dv_ref)

        gate, ov, dyv = g_ref[...], o_ref[...], dy_ref[...]
        do = dyv * _silu(gate)
        dg_ref[...] = dyv * ov * _silu_grad(gate)
        delta = jnp.sum(do * ov, axis=-1, keepdims=True)
        dob = do.astype(BF16)
        qb, kb = q_ref[...], k_ref[...]
        p = jnp.exp(_dot(qb, kb, NT) * scale - lse_ref[...])
        dp = _dot(dob, v_ref[...], NT)
        ds = (p * (dp - delta) * scale).astype(BF16)
        dq_ref[...] = _dot(ds, kb)
        dk_ref[...] += _dot(ds, qb, TN)
        dv_ref[...] += _dot(p.astype(BF16), dob, TN)

    return pl.pallas_call(
        body, name="attn_bwd", grid=(4, s // tq),
        in_specs=[pl.BlockSpec((None, tq, 256), lambda h, i: (h, i, 0)),
                  pl.BlockSpec((None, s, 256), lambda h, i: (h, 0, 0)),
                  pl.BlockSpec((None, s, LANES), lambda h, i: (h, 0, 0)),
                  pl.BlockSpec((tq, LANES), lambda h, i: (i, M_G + h)),
                  pl.BlockSpec((tq, LANES), lambda h, i: (i, h)),
                  pl.BlockSpec((None, tq, 1), lambda h, i: (h, i, 0)),
                  pl.BlockSpec((tq, LANES), lambda h, i: (i, 12 + h))],
        out_specs=[pl.BlockSpec((None, tq, 256), lambda h, i: (h, i, 0)),
                   pl.BlockSpec((None, s, 256), lambda h, i: (h, 0, 0)),
                   pl.BlockSpec((None, s, LANES), lambda h, i: (h, 0, 0)),
                   pl.BlockSpec((tq, LANES), lambda h, i: (i, h))],
        out_shape=[jax.ShapeDtypeStruct((4, s, 256), F32), jax.ShapeDtypeStruct((4, s, 256), F32),
                   jax.ShapeDtypeStruct((4, s, LANES), F32), jax.ShapeDtypeStruct((s, GROUP_W), F32)],
        compiler_params=_params(("parallel", "arbitrary")),
    )(q, k, v, z, o, lse, dy)


def _loss_head(x, target, tm):
    s, d = x.shape

    def body(x_ref, t_ref, dx_ref, loss_ref):
        @pl.when(pl.program_id(0) == 0)
        def _():
            loss_ref[...] = jnp.zeros_like(loss_ref)
        err = x_ref[...] - t_ref[...]
        dx_ref[...] = err * (1.0 / d)
        per_tok = jnp.mean(err * err, axis=-1, keepdims=True)
        loss_ref[...] += 0.5 * jnp.sum(per_tok, axis=0, keepdims=True)

    return pl.pallas_call(
        body, name="loss_head", grid=(s // tm,),
        in_specs=[pl.BlockSpec((tm, d), lambda i: (i, 0)), pl.BlockSpec((tm, d), lambda i: (i, 0))],
        out_specs=[pl.BlockSpec((tm, d), lambda i: (i, 0)), pl.BlockSpec((1, LANES), lambda i: (0, 0))],
        out_shape=[jax.ShapeDtypeStruct((s, d), F32), jax.ShapeDtypeStruct((1, LANES), F32)],
        compiler_params=_params(("arbitrary",)),
    )(x, target)


def _norm_bwd(x, g, dh, dres, tm):
    s, d = x.shape

    def body(x_ref, g_ref, dh_ref, dres_ref, dx_ref, dg_ref):
        @pl.when(pl.program_id(0) == 0)
        def _():
            dg_ref[...] = jnp.zeros_like(dg_ref)
        xv, dhv = x_ref[...], dh_ref[...]
        r = lax.rsqrt(jnp.mean(xv * xv, axis=-1, keepdims=True) + EPS)
        n = xv * r
        dg_ref[...] += jnp.sum(dhv * n, axis=0, keepdims=True)
        u = dhv * g_ref[...]
        dx_ref[...] = dres_ref[...] + r * (u - n * jnp.mean(u * n, axis=-1, keepdims=True))

    row = lambda: pl.BlockSpec((tm, d), lambda i: (i, 0))
    return pl.pallas_call(
        body, name="norm_bwd", grid=(s // tm,),
        in_specs=[row(), pl.BlockSpec((1, d), lambda i: (0, 0)), row(), row()],
        out_specs=[row(), pl.BlockSpec((1, d), lambda i: (0, 0))],
        out_shape=[jax.ShapeDtypeStruct((s, d), F32), jax.ShapeDtypeStruct((1, d), F32)],
        compiler_params=_params(("arbitrary",)),
    )(x, g, dh, dres)


def _adam(parts, w, m, v, name, tr):
    r, c = w.shape
    tr = min(tr, r)
    c1 = 1.0 - ADAM_B1 ** ADAM_STEP
    c2 = 1.0 - ADAM_B2 ** ADAM_STEP

    def body(p_ref, w_ref, m_ref, v_ref, g_ref, d_ref, nm_ref, nv_ref):
        g = p_ref[0]
        for i in range(1, N_DEV):
            g = g + p_ref[i]
        nm = ADAM_B1 * m_ref[...] + (1.0 - ADAM_B1) * g
        nv = ADAM_B2 * v_ref[...] + (1.0 - ADAM_B2) * (g * g)
        g_ref[...] = g
        nm_ref[...] = nm
        nv_ref[...] = nv
        d_ref[...] = -ADAM_LR * ((nm / c1) / (jnp.sqrt(nv / c2) + ADAM_EPS) + ADAM_WD * w_ref[...])

    blk = lambda: pl.BlockSpec((tr, c), lambda i: (i, 0))
    return pl.pallas_call(
        body, name=name, grid=(r // tr,),
        in_specs=[pl.BlockSpec((N_DEV, tr, c), lambda i: (0, i, 0)), blk(), blk(), blk()],
        out_specs=[blk(), blk(), blk(), blk()],
        out_shape=[jax.ShapeDtypeStruct((r, c), F32)] * 4,
        compiler_params=_params(("parallel",)),
    )(parts, w, m, v)


MESH = pl.DeviceIdType.MESH
ANY = pl.BlockSpec(memory_space=pl.ANY)


def _position():
    return lax.axis_index("x"), lax.axis_index("y"), lax.axis_index("c")


def _all_gather(xs, name):
    r, c = xs.shape

    def body(x_ref, out_ref, send_sems, recv_sems, local_sem):
        x, y, cc = _position()
        me, sibling = (x, y, cc), (x, y, 1 - cc)
        chips = [(1 - x, y), (x, 1 - y), (1 - x, 1 - y)]

        def slab(px, py, pc):
            return out_ref.at[4 * px + 2 * py + pc]

        def copy(k, block, to, src=None):
            return pltpu.make_async_remote_copy(
                src_ref=slab(*block) if src is None else src, dst_ref=slab(*block),
                send_sem=send_sems.at[k], recv_sem=recv_sems.at[k], device_id=to, device_id_type=MESH)

        mine = pltpu.make_async_copy(x_ref, slab(*me), local_sem)
        mine.start()
        first = [copy(0, me, sibling, src=x_ref)]
        first += [copy(1 + j, me, (*chip, cc), src=x_ref) for j, chip in enumerate(chips)]
        for cp in first:
            cp.start()
        passed = [copy(4 + j, (*chip, cc), sibling) for j, chip in enumerate(chips)]
        for j, chip in enumerate(chips):
            copy(1 + j, (*chip, cc), me).wait_recv()
            passed[j].start()
        copy(0, sibling, me).wait_recv()
        for j, chip in enumerate(chips):
            copy(4 + j, (*chip, 1 - cc), me).wait_recv()
        for cp in first + passed:
            cp.wait_send()
        mine.wait()

    return pl.pallas_call(
        body, name=name, out_shape=jax.ShapeDtypeStruct((N_DEV, r, c), xs.dtype),
        in_specs=[ANY], out_specs=ANY,
        scratch_shapes=[pltpu.SemaphoreType.DMA((7,)), pltpu.SemaphoreType.DMA((7,)), pltpu.SemaphoreType.DMA],
    )(xs)


_RELATIONS = ((0, 0, 1), (1, 0, 0), (0, 1, 0), (1, 1, 0), (1, 0, 1), (0, 1, 1), (1, 1, 1))


def _all_to_all(cs, name):
    _, r, c = cs.shape

    def body(c_ref, out_ref, send_sems, recv_sems, local_sem):
        x, y, cc = _position()
        me = 4 * x + 2 * y + cc
        mine = pltpu.make_async_copy(c_ref.at[me], out_ref.at[me], local_sem)
        mine.start()
        copies = []
        for k, (fx, fy, fc) in enumerate(_RELATIONS):
            px = 1 - x if fx else x
            py = 1 - y if fy else y
            pc = 1 - cc if fc else cc
            copies.append(pltpu.make_async_remote_copy(
                src_ref=c_ref.at[4 * px + 2 * py + pc], dst_ref=out_ref.at[me],
                send_sem=send_sems.at[k], recv_sem=recv_sems.at[k], device_id=(px, py, pc), device_id_type=MESH))
        for cp in copies:
            cp.start()
        for cp in copies:
            cp.wait()
        mine.wait()

    return pl.pallas_call(
        body, name=name, out_shape=jax.ShapeDtypeStruct(cs.shape, cs.dtype),
        in_specs=[ANY], out_specs=ANY,
        scratch_shapes=[pltpu.SemaphoreType.DMA((7,)), pltpu.SemaphoreType.DMA((7,)), pltpu.SemaphoreType.DMA],
    )(cs)


REPLICATED = ("norm_g", "ret_norm_g", "gla_ba_f", "gla_ba_b", "gla_norm_g", "pool_w", "pool_scale",
              "mla_q_norm_g", "mla_kv_norm_g", "mla_qk_norm_q", "mla_qk_norm_k")
SMALL_SHARDED = ("mla_wq_b", "mla_wkv_b", "gla_wa2_f", "gla_wa2_b")
WEIGHTS = ("norm_g", "w_in", "ret_norm_g", "gla_wa2_f", "gla_ba_f", "gla_wa2_b", "gla_ba_b", "gla_norm_g", "pool_w",
           "pool_scale", "mla_q_norm_g", "mla_wq_b", "mla_kv_norm_g", "mla_wkv_b", "mla_qk_norm_q", "mla_qk_norm_k",
           "w_out")


def _pack(arrays, dtype):
    flat = jnp.concatenate([a.reshape(-1) for a in arrays]).astype(dtype)
    return flat.reshape(-1, LANES)


def _unpack(packed, like):
    flat = packed.reshape(-1)
    out, at = [], 0
    for a in like:
        out.append(flat[at:at + a.size].reshape(a.shape))
        at += a.size
    return out


def _columns_by_device(g):
    l, r, n = g.shape
    return g.reshape(l, r, N_DEV, n // N_DEV).transpose(2, 0, 1, 3)


def _gathered_columns(g, l, r, c):
    return g.reshape(N_DEV, l, r, c).transpose(1, 2, 0, 3).reshape(l, r, N_DEV * c)


def _layer_forward(x, wts, tables, tm, tq):
    cos_r, sin_r, cos_m, sin_m, tab, _ = tables
    z, h = _inproj(x, wts["norm_g"], wts["w_in"], tm)
    o_a, y_a = _ret_fwd(z, cos_r, sin_r, tab, wts["ret_norm_g"])
    o_b, y_b = _gla_fwd(z, wts["wa_f"], wts["wa_b"], wts["gla_ba_f"], wts["gla_ba_b"], wts["gla_norm_g"])
    y_c = _pool_fwd(z, wts["pool_w"], wts["pool_scale"])
    q, k, v = _mla_prep(z, cos_m, sin_m, wts["mla_q_norm_g"], wts["wq"], wts["mla_kv_norm_g"], wts["wkv"],
                        wts["qk_q"], wts["qk_k"], tm)
    o_d, y_d, lse = _attn_fwd(q, k, v, z, tq)
    y = jnp.concatenate([y_a, y_b, y_c, y_d], axis=1)
    x_next = _mm(y, wts["w_out"], "nn", "outproj", tm, 512, 1024, add=x)
    saved = dict(x=x, z=z, h=h, o_a=o_a, o_b=o_b, o_d=o_d, lse=lse, q=q, k=k, v=v, y=y)
    return x_next, saved


def _layer_backward(dx, sv, wts, tables, tm, tq):
    cos_r, sin_r, cos_m, sin_m, tab, tab_sw = tables
    z = sv["z"]
    g = {}
    g["w_out"] = _mm(sv["y"], dx, "tn", "d_w_out", 512, 512, 1024)
    dy = _mm(dx, wts["w_out"], "nt", "d_y", tm, 512, 1024)

    do_a, dg_a, g["ret_norm_g"] = _normgate_bwd(sv["o_a"], z, A_G, dy, 0, wts["ret_norm_g"], tm)
    dq_a, dk_a, dv_a = _ret_bwd(z, do_a, cos_r, sin_r, tab, tab_sw)

    do_b, dg_b, g["gla_norm_g"] = _normgate_bwd(sv["o_b"], z, B_G, dy, 1, wts["gla_norm_g"], tm)
    dq_b, dk_b, dv_b, d_ga, d_waf, d_wab, g["gla_ba_f"], g["gla_ba_b"] = _gla_bwd(
        z, do_b, wts["wa_f"], wts["wa_b"], wts["gla_ba_f"], wts["gla_ba_b"])
    g["gla_wa2_f"] = d_waf[0:GLA_RANK]
    g["gla_wa2_b"] = d_wab[GLA_RANK:2 * GLA_RANK]

    du_c, dg_c, g["pool_w"], g["pool_scale"] = _pool_bwd(z, dy, wts["pool_w"], wts["pool_scale"])

    d_q, d_k, d_v, dg_d = _attn_bwd(sv["q"], sv["k"], sv["v"], z, sv["o_d"], sv["lse"], dy, tq)
    (d_mq, d_mkv, d_mkr, d_wq, g["mla_wkv_b"], g["mla_q_norm_g"], g["mla_kv_norm_g"], d_qg, d_kg) = _mla_prep_bwd(
        z, cos_m, sin_m, wts["mla_q_norm_g"], wts["wq"], wts["mla_kv_norm_g"], wts["wkv"], wts["qk_q"], wts["qk_k"],
        d_q, d_k, d_v, tm)
    g["mla_wq_b"] = _unpad_wq(d_wq)
    g["mla_qk_norm_q"] = d_qg[:, _QK_INV]
    g["mla_qk_norm_k"] = d_kg[:, _QK_INV]

    dz = jnp.concatenate([dq_a, dk_a, dv_a, dg_a, dq_b, dk_b, dv_b, dg_b, d_mq, du_c, dg_c, dg_d, d_mkv, d_ga, d_mkr],
                         axis=1)
    g["w_in"] = _unpad_w_in(_mm(sv["h"], dz, "tn", "d_w_in", 512, 512, 1024))
    dh = _mm(dz, wts["w_in"], "nt", "d_h", tm, 512, 1024)
    dx_in, g["norm_g"] = _norm_bwd(sv["x"], wts["norm_g"], dh, dx, tm)
    return dx_in, g


def kernel(x, norm_g, w_in, ret_norm_g, gla_wa2_f, gla_ba_f, gla_wa2_b, gla_ba_b, gla_norm_g, pool_w, pool_scale, mla_q_norm_g, mla_wq_b, mla_kv_norm_g, mla_wkv_b, mla_qk_norm_q, mla_qk_norm_k, w_out, loss_target, m_norm_g, m_w_in, m_ret_norm_g, m_gla_wa2_f, m_gla_ba_f, m_gla_wa2_b, m_gla_ba_b, m_gla_norm_g, m_pool_w, m_pool_scale, m_mla_q_norm_g, m_mla_wq_b, m_mla_kv_norm_g, m_mla_wkv_b, m_mla_qk_norm_q, m_mla_qk_norm_k, m_w_out, v_norm_g, v_w_in, v_ret_norm_g, v_gla_wa2_f, v_gla_ba_f, v_gla_wa2_b, v_gla_ba_b, v_gla_norm_g, v_pool_w, v_pool_scale, v_mla_q_norm_g, v_mla_wq_b, v_mla_kv_norm_g, v_mla_wkv_b, v_mla_qk_norm_q, v_mla_qk_norm_k, v_w_out):
    w = dict(norm_g=norm_g, w_in=w_in, ret_norm_g=ret_norm_g, gla_wa2_f=gla_wa2_f, gla_ba_f=gla_ba_f,
             gla_wa2_b=gla_wa2_b, gla_ba_b=gla_ba_b, gla_norm_g=gla_norm_g, pool_w=pool_w, pool_scale=pool_scale,
             mla_q_norm_g=mla_q_norm_g, mla_wq_b=mla_wq_b, mla_kv_norm_g=mla_kv_norm_g, mla_wkv_b=mla_wkv_b,
             mla_qk_norm_q=mla_qk_norm_q, mla_qk_norm_k=mla_qk_norm_k, w_out=w_out)
    m = dict(norm_g=m_norm_g, w_in=m_w_in, ret_norm_g=m_ret_norm_g, gla_wa2_f=m_gla_wa2_f, gla_ba_f=m_gla_ba_f,
             gla_wa2_b=m_gla_wa2_b, gla_ba_b=m_gla_ba_b, gla_norm_g=m_gla_norm_g, pool_w=m_pool_w,
             pool_scale=m_pool_scale, mla_q_norm_g=m_mla_q_norm_g, mla_wq_b=m_mla_wq_b, mla_kv_norm_g=m_mla_kv_norm_g,
             mla_wkv_b=m_mla_wkv_b, mla_qk_norm_q=m_mla_qk_norm_q, mla_qk_norm_k=m_mla_qk_norm_k, w_out=m_w_out)
    v = dict(norm_g=v_norm_g, w_in=v_w_in, ret_norm_g=v_ret_norm_g, gla_wa2_f=v_gla_wa2_f, gla_ba_f=v_gla_ba_f,
             gla_wa2_b=v_gla_wa2_b, gla_ba_b=v_gla_ba_b, gla_norm_g=v_gla_norm_g, pool_w=v_pool_w,
             pool_scale=v_pool_scale, mla_q_norm_g=v_mla_q_norm_g, mla_wq_b=v_mla_wq_b, mla_kv_norm_g=v_mla_kv_norm_g,
             mla_wkv_b=v_mla_wkv_b, mla_qk_norm_q=v_mla_qk_norm_q, mla_qk_norm_k=v_mla_qk_norm_k, w_out=v_w_out)
    xs, target = x[0], loss_target[0]
    s = xs.shape[0]
    tm, tq = min(s, 512), min(s, 256)
    c_in = w_in.shape[2]

    w_in_g = _all_gather(w_in.astype(BF16).reshape(DEPTH * D_MODEL, c_in), "gather_w_in")
    w_out_g = _all_gather(w_out.astype(BF16).reshape(-1, D_MODEL), "gather_w_out")
    small_g = _all_gather(_pack([w[n] for n in SMALL_SHARDED], BF16), "gather_small")
    w_in_full = _gathered_columns(w_in_g, DEPTH, D_MODEL, c_in)
    w_out_full = w_out_g.reshape(N_DEV, DEPTH, -1, D_MODEL).transpose(1, 0, 2, 3).reshape(DEPTH, -1, D_MODEL)
    sizes = [w[n].size for n in SMALL_SHARDED]
    offs = np.cumsum([0] + sizes)
    small_flat = small_g.reshape(N_DEV, -1)
    small_full = {n: _gathered_columns(small_flat[:, offs[i]:offs[i + 1]], *w[n].shape)
                  for i, n in enumerate(SMALL_SHARDED)}

    tables = _rope_tables(s) + _ret_tables()
    layers = []
    for l in range(DEPTH):
        wa_f = jnp.zeros((LANES, 2 * LANES), BF16).at[0:GLA_RANK].set(small_full["gla_wa2_f"][l])
        wa_b = jnp.zeros((LANES, 2 * LANES), BF16).at[GLA_RANK:2 * GLA_RANK].set(small_full["gla_wa2_b"][l])
        layers.append(dict(
            norm_g=norm_g[l][None], w_in=_pad_w_in(w_in_full[l]), ret_norm_g=ret_norm_g[l][None],
            wa_f=wa_f, wa_b=wa_b, gla_ba_f=gla_ba_f[l][None], gla_ba_b=gla_ba_b[l][None],
            gla_norm_g=gla_norm_g[l][None], pool_w=pool_w[l], pool_scale=pool_scale[l][None],
            mla_q_norm_g=mla_q_norm_g[l][None], wq=_pad_wq(small_full["mla_wq_b"][l]),
            mla_kv_norm_g=mla_kv_norm_g[l][None], wkv=small_full["mla_wkv_b"][l],
            qk_q=_pad_qk_gain(mla_qk_norm_q[l]), qk_k=_pad_qk_gain(mla_qk_norm_k[l]), w_out=w_out_full[l]))

    saved = []
    xl = xs
    for l in range(DEPTH):
        xl, sv = _layer_forward(xl, layers[l], tables, tm, tq)
        saved.append(sv)
    dx, loss_row = _loss_head(xl, target, tm)
    grads = [None] * DEPTH
    for l in reversed(range(DEPTH)):
        dx, grads[l] = _layer_backward(dx, saved[l], layers[l], tables, tm, tq)
    loss = lax.psum(loss_row[0, 0], ("x", "y", "c"))

    full = {n: jnp.stack([grads[l][n].reshape(w[n].shape[1:]) if n in REPLICATED else grads[l][n]
                          for l in range(DEPTH)]) for n in WEIGHTS}

    in_parts = _all_to_all(_columns_by_device(full["w_in"]).reshape(N_DEV, DEPTH * D_MODEL, c_in), "exchange_w_in")
    out_parts = _all_to_all(full["w_out"].reshape(DEPTH, N_DEV, -1, D_MODEL).transpose(1, 0, 2, 3)
                            .reshape(N_DEV, -1, D_MODEL), "exchange_w_out")
    small_c = jnp.concatenate([_columns_by_device(full[n]).reshape(N_DEV, -1) for n in SMALL_SHARDED], axis=1)
    small_parts = _all_to_all(small_c.reshape(N_DEV, -1, LANES), "exchange_small")
    rep_parts = _all_gather(_pack([full[n] for n in REPLICATED], F32), "gather_replicated")

    out = {}
    res = _adam(in_parts, w_in.reshape(-1, c_in), m_w_in.reshape(-1, c_in), v_w_in.reshape(-1, c_in), "adam_w_in", 256)
    out["w_in"] = [a.reshape(w_in.shape) for a in res]
    res = _adam(out_parts, w_out.reshape(-1, D_MODEL), m_w_out.reshape(-1, D_MODEL), v_w_out.reshape(-1, D_MODEL),
                "adam_w_out", 128)
    out["w_out"] = [a.reshape(w_out.shape) for a in res]
    for names, parts, label in ((SMALL_SHARDED, small_parts, "adam_small"), (REPLICATED, rep_parts, "adam_replicated")):
        res = _adam(parts, _pack([w[n] for n in names], F32), _pack([m[n] for n in names], F32),
                    _pack([v[n] for n in names], F32), label, 2048)
        for n, *vals in zip(names, *[_unpack(a, [w[n] for n in names]) for a in res]):
            out[n] = vals

    return (loss, dx[None], *[out[n][0] for n in WEIGHTS], *[out[n][1] for n in WEIGHTS],
            *[out[n][2] for n in WEIGHTS], *[out[n][3] for n in WEIGHTS])
```

```python
import functools
import math

import numpy as np
import jax
import jax.numpy as jnp
from jax import lax
from jax.experimental import pallas as pl
from jax.experimental.pallas import tpu as pltpu

F32 = jnp.float32
BF16 = jnp.bfloat16

N_DEV = 8
D_MODEL = 2048
DEPTH = 2
GROUP_W = 512
EPS = 1e-6
ROPE_THETA = 10000.0
LANES = 128

RET_HD = 128
RET_CHUNK = 128
GLA_CHUNK = 64
GLA_DK = 64
GLA_TAU = 16.0
GLA_RANK = 16
POOL_WINDOWS = (2, 4, 8, 16)
MLA_QK = 192
MLA_ROPE = 64
IN_COLS = 5984

ADAM_LR = 0.001
ADAM_B1 = 0.9
ADAM_B2 = 0.999
ADAM_EPS = 1e-08
ADAM_WD = 0.01
ADAM_STEP = 10

A_Q, A_K, A_V, A_G = 0, 4, 8, 12
B_Q, B_K, B_V, B_G = 16, 18, 20, 24
M_Q, C_V, C_G, M_G = 28, 32, 36, 40
M_KV, GA, M_KR = 44, 46, 47
ZP_COLS = 48 * LANES

VMEM_LIMIT = 56 * 1024 * 1024


def _params(sem, vmem=VMEM_LIMIT):
    return pltpu.CompilerParams(dimension_semantics=sem, vmem_limit_bytes=vmem)


def _sigmoid(x):
    return 1.0 / (1.0 + jnp.exp(-x))


def _silu(x):
    return x * _sigmoid(x)


def _silu_grad(x):
    s = _sigmoid(x)
    return s * (1.0 + x * (1.0 - s))


def _dot(a, b, dims=(((1,), (0,)), ((), ()))):
    return lax.dot_general(a, b, dims, preferred_element_type=F32)


NT = (((1,), (1,)), ((), ()))
TN = (((0,), (0,)), ((), ()))


def _dot_exact(a, b):
    return lax.dot_general(a, b, (((1,), (0,)), ((), ())), precision=lax.Precision.HIGHEST,
                           preferred_element_type=F32)


def _roll_lanes_half(x):
    return pltpu.roll(x, 64, 1)


def _pad_w_in(w):
    z = lambda n: jnp.zeros((w.shape[0], n), w.dtype)
    return jnp.concatenate([
        w[:, 0:3584],
        w[:, 4640:5152], w[:, 3616:4640], w[:, 5472:5984], w[:, 5152:5408],
        w[:, 3584:3616], z(96),
        w[:, 5408:5440], z(32), w[:, 5440:5472], z(32)], axis=1)


def _unpad_w_in(wp):
    b = lambda blk, n: wp[:, blk * LANES:blk * LANES + n]
    return jnp.concatenate([wp[:, 0:3584], b(GA, 32), b(C_V, 1024), b(M_Q, 512), b(M_KV, 256),
                            b(M_KR, 32), wp[:, M_KR * LANES + 64:M_KR * LANES + 96], b(M_G, 512)], axis=1)


def _wq_perm():
    idx = np.zeros((1024,), np.int32)
    ok = np.zeros((1024,), bool)
    for h in range(4):
        idx[128 * h:128 * h + 128] = 192 * h + np.arange(128)
        ok[128 * h:128 * h + 128] = True
        base = 512 + 128 * h
        idx[base:base + 32] = 192 * h + 128 + np.arange(32)
        ok[base:base + 32] = True
        idx[base + 64:base + 96] = 192 * h + 160 + np.arange(32)
        ok[base + 64:base + 96] = True
    inv = np.zeros((768,), np.int32)
    inv[idx[ok]] = np.nonzero(ok)[0]
    return idx, ok, inv


_WQ_IDX, _WQ_OK, _WQ_INV = _wq_perm()


def _pad_wq(wq):
    return jnp.where(jnp.asarray(_WQ_OK)[None, :], wq[:, _WQ_IDX], 0).astype(wq.dtype)


def _unpad_wq(wqp):
    return wqp[:, _WQ_INV]


def _qk_idx():
    idx = np.zeros((256,), np.int32)
    ok = np.zeros((256,), bool)
    idx[0:128] = np.arange(128)
    ok[0:128] = True
    idx[128:160] = 128 + np.arange(32)
    ok[128:160] = True
    idx[192:224] = 160 + np.arange(32)
    ok[192:224] = True
    inv = np.zeros((192,), np.int32)
    inv[idx[ok]] = np.nonzero(ok)[0]
    return idx, ok, inv


_QK_IDX, _QK_OK, _QK_INV = _qk_idx()


def _pad_qk_gain(g):
    return jnp.where(jnp.asarray(_QK_OK), g[_QK_IDX], 0.0).reshape(1, 256)


def _rope_tables(s):
    def tabs(dim):
        inv = 1.0 / (ROPE_THETA ** (jnp.arange(0, dim, 2, dtype=F32) / dim))
        ang = jnp.arange(s, dtype=F32)[:, None] * inv[None, :]
        return jnp.cos(ang), jnp.sin(ang)
    cr, sr = tabs(RET_HD)
    cos_r = jnp.concatenate([cr, cr], axis=1)
    sin_r = jnp.concatenate([-sr, sr], axis=1)
    cm, sm = tabs(MLA_ROPE)
    zz = jnp.zeros_like(cm)
    cos_m = jnp.concatenate([cm, zz, cm, zz], axis=1)
    sin_m = jnp.concatenate([-sm, zz, sm, zz], axis=1)
    return cos_r, sin_r, cos_m, sin_m


def _rope(x, cos, sin):
    return x * cos + _roll_lanes_half(x) * sin


def _rope_t(x, cos, sin):
    return x * cos + _roll_lanes_half(x * sin)


def _ret_tables():
    c = RET_CHUNK
    gamma_f = 1.0 - 2.0 ** (-5.0 - jnp.arange(4, dtype=F32))
    gamma_b = gamma_f[::-1]
    idx = jnp.arange(c, dtype=F32)
    diff = idx[:, None] - idx[None, :]

    def build(g1, g2):
        l1 = jnp.log(g1)[:, None, None]
        l2 = jnp.log(g2)[:, None, None]
        d1 = jnp.where(diff >= 0, jnp.exp(jnp.maximum(diff, 0.0)[None] * l1), 0.0)
        d2 = jnp.where(diff <= 0, jnp.exp(jnp.maximum(-diff, 0.0)[None] * l2), 0.0)
        ones = jnp.ones((1, c, c), F32)
        col = idx[None, :, None]
        qdf = jnp.exp((col + 1.0) * l1) * ones
        kdf = jnp.exp((c - 1.0 - col) * l1) * ones
        qdb = jnp.exp((c - col) * l2) * ones
        kdb = jnp.exp(col * l2) * ones
        cd1 = jnp.exp(c * l1) * ones
        cd2 = jnp.exp(c * l2) * ones
        return jnp.stack([d1 + d2, qdf, kdf, qdb, kdb, cd1, cd2], axis=1)

    return build(gamma_f, gamma_b), build(gamma_b, gamma_f)


def _inproj(x, g, wp, tm, tn=512):
    s, d = x.shape
    n = wp.shape[1]

    def body(x_ref, g_ref, w_ref, z_ref, h_ref, hs):
        @pl.when(pl.program_id(1) == 0)
        def _():
            xv = x_ref[...]
            r = lax.rsqrt(jnp.mean(xv * xv, axis=-1, keepdims=True) + EPS)
            hv = (xv * r * g_ref[...]).astype(BF16)
            hs[...] = hv
            h_ref[...] = hv
        z_ref[...] = _dot(hs[...], w_ref[...])

    return pl.pallas_call(
        body, name="inproj", grid=(s // tm, n // tn),
        in_specs=[pl.BlockSpec((tm, d), lambda i, j: (i, 0)),
                  pl.BlockSpec((1, d), lambda i, j: (0, 0)),
                  pl.BlockSpec((d, tn), lambda i, j: (0, j))],
        out_specs=[pl.BlockSpec((tm, tn), lambda i, j: (i, j)),
                   pl.BlockSpec((tm, d), lambda i, j: (i, 0))],
        out_shape=[jax.ShapeDtypeStruct((s, n), F32), jax.ShapeDtypeStruct((s, d), BF16)],
        scratch_shapes=[pltpu.VMEM((tm, d), BF16)],
        compiler_params=_params(("parallel", "arbitrary")),
    )(x, g, wp)


def _mm(a, b, mode, name, tm, tn, tk, add=None, out_dtype=F32):
    if mode == "tn":
        k, m = a.shape
    else:
        m, k = a.shape
    n = b.shape[0] if mode == "nt" else b.shape[1]
    tm, tn, tk = min(tm, m), min(tn, n), min(tk, k)
    nk = k // tk
    dims = {"nn": (((1,), (0,)), ((), ())), "nt": NT, "tn": TN}[mode]

    def body(*refs):
        if add is None:
            a_ref, b_ref, o_ref, acc = refs
        else:
            a_ref, b_ref, add_ref, o_ref, acc = refs
        kk = pl.program_id(2)

        @pl.when(kk == 0)
        def _():
            acc[...] = jnp.zeros_like(acc)

        acc[...] += _dot(a_ref[...].astype(BF16), b_ref[...].astype(BF16), dims)

        @pl.when(kk == nk - 1)
        def _():
            r = acc[...]
            if add is not None:
                r = r + add_ref[...]
            o_ref[...] = r.astype(out_dtype)

    a_spec = (pl.BlockSpec((tk, tm), lambda i, j, kk: (kk, i)) if mode == "tn"
              else pl.BlockSpec((tm, tk), lambda i, j, kk: (i, kk)))
    b_spec = (pl.BlockSpec((tn, tk), lambda i, j, kk: (j, kk)) if mode == "nt"
              else pl.BlockSpec((tk, tn), lambda i, j, kk: (kk, j)))
    in_specs = [a_spec, b_spec]
    args = [a, b]
    if add is not None:
        in_specs.append(pl.BlockSpec((tm, tn), lambda i, j, kk: (i, j)))
        args.append(add)
    return pl.pallas_call(
        body, name=name, grid=(m // tm, n // tn, nk),
        in_specs=in_specs,
        out_specs=pl.BlockSpec((tm, tn), lambda i, j, kk: (i, j)),
        out_shape=jax.ShapeDtypeStruct((m, n), out_dtype),
        scratch_shapes=[pltpu.VMEM((tm, tn), F32)],
        compiler_params=_params(("parallel", "parallel", "arbitrary")),
    )(*args)


def _ret_core(q_ref, k_ref, v_ref, tab_ref, out_ref, st_ref, nchunk):
    c = RET_CHUNK

    def rows(n):
        return pl.ds(pl.multiple_of(n * c, c), c)

    st_ref[...] = jnp.zeros_like(st_ref)

    def fwd(n, carry):
        r = rows(n)
        q, k, vb = q_ref[r, :], k_ref[r, :], v_ref[r, :].astype(BF16)
        sc = _dot(q.astype(BF16), k.astype(BF16), NT) * tab_ref[0]
        o = _dot(sc.astype(BF16), vb)
        o = o + _dot((q * tab_ref[1]).astype(BF16), st_ref[...].astype(BF16))
        out_ref[r, :] = o
        st_ref[...] = st_ref[...] * tab_ref[5] + _dot((k * tab_ref[2]).astype(BF16), vb, TN)
        return carry

    lax.fori_loop(0, nchunk, fwd, 0)
    st_ref[...] = jnp.zeros_like(st_ref)

    def bwd(i, carry):
        r = rows(nchunk - 1 - i)
        q, k, vb = q_ref[r, :], k_ref[r, :], v_ref[r, :].astype(BF16)
        out_ref[r, :] += _dot((q * tab_ref[3]).astype(BF16), st_ref[...].astype(BF16))
        st_ref[...] = st_ref[...] * tab_ref[6] + _dot((k * tab_ref[4]).astype(BF16), vb, TN)
        return carry

    lax.fori_loop(0, nchunk, bwd, 0)


def _ret_fwd(z, cos_r, sin_r, tab, norm_g):
    s = z.shape[0]
    nchunk = s // RET_CHUNK
    scale = RET_HD ** -0.5
    col = lambda base: pl.BlockSpec((s, LANES), lambda h: (0, base + h), pipeline_mode=pl.Buffered(1))

    def body(q_ref, k_ref, v_ref, g_ref, cos_ref, sin_ref, tab_ref, ng_ref, o_ref, y_ref, qh, kh, st):
        qh[...] = _rope(q_ref[...], cos_ref[...], sin_ref[...])
        kh[...] = _rope(k_ref[...], cos_ref[...], sin_ref[...]) * scale
        _ret_core(qh, kh, v_ref, tab_ref, o_ref, st, nchunk)
        o = o_ref[...]
        r = lax.rsqrt(jnp.mean(o * o, axis=-1, keepdims=True) + EPS)
        y_ref[...] = (_silu(g_ref[...]) * (o * r * ng_ref[...])).astype(BF16)

    return pl.pallas_call(
        body, name="ret_fwd", grid=(4,),
        in_specs=[col(A_Q), col(A_K), col(A_V), col(A_G),
                  pl.BlockSpec((s, LANES), lambda h: (0, 0), pipeline_mode=pl.Buffered(1)),
                  pl.BlockSpec((s, LANES), lambda h: (0, 0), pipeline_mode=pl.Buffered(1)),
                  pl.BlockSpec((None, 7, LANES, LANES), lambda h: (h, 0, 0, 0)),
                  pl.BlockSpec((1, LANES), lambda h: (0, h))],
        out_specs=[pl.BlockSpec((s, LANES), lambda h: (0, h)), pl.BlockSpec((s, LANES), lambda h: (0, h))],
        out_shape=[jax.ShapeDtypeStruct((s, GROUP_W), F32), jax.ShapeDtypeStruct((s, GROUP_W), BF16)],
        scratch_shapes=[pltpu.VMEM((s, LANES), F32), pltpu.VMEM((s, LANES), F32), pltpu.VMEM((LANES, LANES), F32)],
        compiler_params=_params(("arbitrary",)),
    )(z, z, z, z, cos_r, sin_r, tab, norm_g)


def _ret_bwd(z, d_o, cos_r, sin_r, tab, tab_sw):
    s = z.shape[0]
    nchunk = s // RET_CHUNK
    scale = RET_HD ** -0.5
    col = lambda base: pl.BlockSpec((s, LANES), lambda h: (0, base + h), pipeline_mode=pl.Buffered(1))
    whole = lambda: pl.BlockSpec((s, LANES), lambda h: (0, 0), pipeline_mode=pl.Buffered(1))
    tabspec = lambda: pl.BlockSpec((None, 7, LANES, LANES), lambda h: (h, 0, 0, 0))
    outspec = lambda: pl.BlockSpec((s, LANES), lambda h: (0, h))

    def body(q_ref, k_ref, v_ref, do_ref, cos_ref, sin_ref, tab_ref, tsw_ref, dq_ref, dk_ref, dv_ref,
             qh, kh, tmp, st):
        cos, sin = cos_ref[...], sin_ref[...]
        qh[...] = _rope(q_ref[...], cos, sin)
        kh[...] = _rope(k_ref[...], cos, sin) * scale
        _ret_core(kh, qh, do_ref, tsw_ref, tmp, st, nchunk)
        dv_ref[...] = tmp[...].astype(BF16)
        _ret_core(do_ref, v_ref, kh, tab_ref, tmp, st, nchunk)
        dq_ref[...] = _rope_t(tmp[...], cos, sin).astype(BF16)
        _ret_core(v_ref, do_ref, qh, tsw_ref, tmp, st, nchunk)
        dk_ref[...] = _rope_t(tmp[...] * scale, cos, sin).astype(BF16)

    return pl.pallas_call(
        body, name="ret_bwd", grid=(4,),
        in_specs=[col(A_Q), col(A_K), col(A_V),
                  pl.BlockSpec((s, LANES), lambda h: (0, h), pipeline_mode=pl.Buffered(1)),
                  whole(), whole(), tabspec(), tabspec()],
        out_specs=[outspec(), outspec(), outspec()],
        out_shape=[jax.ShapeDtypeStruct((s, GROUP_W), BF16)] * 3,
        scratch_shapes=[pltpu.VMEM((s, LANES), F32), pltpu.VMEM((s, LANES), F32), pltpu.VMEM((s, LANES), F32),
                        pltpu.VMEM((LANES, LANES), F32)],
        compiler_params=_params(("arbitrary",)),
    )(z, z, z, d_o, cos_r, sin_r, tab, tab_sw)


def _normgate_bwd(o, z, gate_blk, dy, dy_blk, norm_g, tm):
    s = o.shape[0]

    def body(o_ref, g_ref, dy_ref, ng_ref, do_ref, dg_ref, dng_ref):
        @pl.when(pl.program_id(0) == 0)
        def _():
            dng_ref[...] = jnp.zeros_like(dng_ref)

        for h in range(4):
            sl = slice(LANES * h, LANES * (h + 1))
            ov, gv, dyv, ng = o_ref[:, sl], g_ref[:, sl], dy_ref[:, sl], ng_ref[:, sl]
            r = lax.rsqrt(jnp.mean(ov * ov, axis=-1, keepdims=True) + EPS)
            on = ov * r
            dn = dyv * _silu(gv)
            u = dn * ng
            do_ref[:, sl] = r * (u - on * jnp.mean(u * on, axis=-1, keepdims=True))
            dg_ref[:, sl] = (dyv * (on * ng) * _silu_grad(gv)).astype(BF16)
            dng_ref[:, sl] += jnp.sum(dn * on, axis=0, keepdims=True)

    return pl.pallas_call(
        body, name="normgate_bwd", grid=(s // tm,),
        in_specs=[pl.BlockSpec((tm, GROUP_W), lambda i: (i, 0)),
                  pl.BlockSpec((tm, GROUP_W), lambda i: (i, gate_blk // 4)),
                  pl.BlockSpec((tm, GROUP_W), lambda i: (i, dy_blk)),
                  pl.BlockSpec((1, GROUP_W), lambda i: (0, 0))],
        out_specs=[pl.BlockSpec((tm, GROUP_W), lambda i: (i, 0)), pl.BlockSpec((tm, GROUP_W), lambda i: (i, 0)),
                   pl.BlockSpec((1, GROUP_W), lambda i: (0, 0))],
        out_shape=[jax.ShapeDtypeStruct((s, GROUP_W), F32), jax.ShapeDtypeStruct((s, GROUP_W), BF16),
                   jax.ShapeDtypeStruct((1, GROUP_W), F32)],
        compiler_params=_params(("arbitrary",)),
    )(o, z, dy, norm_g)


def _log_sigmoid(x):
    return jnp.minimum(x, 0.0) - jnp.log(1.0 + jnp.exp(-jnp.abs(x)))


def _gla_consts():
    c = GLA_CHUNK
    row = lax.broadcasted_iota(jnp.int32, (c, c), 0)
    colm = lax.broadcasted_iota(jnp.int32, (c, c), 1)
    lane = lax.broadcasted_iota(jnp.int32, (1, LANES), 1)
    low = row >= colm
    up = colm >= row
    heads = ((lane < GLA_DK).astype(F32), (lane >= GLA_DK).astype(F32))
    return low, up, heads


def _gla_chunk(q, k, la, tri_f):
    cum = _dot_exact(tri_f, la)
    last = jnp.sum(la, axis=0, keepdims=True)
    eq = jnp.exp(cum)
    ek = jnp.exp(-cum)
    el = jnp.exp(last - cum)
    dec = jnp.exp(last)
    return eq, ek, el, dec


def _gla_gates(ga_ref, wa_ref, ba_ref, la_ref, s, tm):
    def step(i, carry):
        r = pl.ds(pl.multiple_of(i * tm, tm), tm)
        pre = _dot(ga_ref[r, :].astype(BF16), wa_ref[...].astype(BF16)) + ba_ref[...]
        la_ref[r, :] = _log_sigmoid(pre) * (1.0 / GLA_TAU)
        return carry
    lax.fori_loop(0, s // tm, step, 0)


def _gla_fwd(z, wa_f, wa_b, ba_f, ba_b, norm_g):
    s = z.shape[0]
    c = GLA_CHUNK
    nchunk = s // c
    scale = GLA_DK ** -0.5
    tm = min(s, 512)
    one = pl.Buffered(1)

    def body(q_ref, k_ref, v_ref, ga_ref, g_ref, waf_ref, wab_ref, baf_ref, bab_ref, ng_ref, o_ref, y_ref,
             la_s, st):
        low, up, heads = _gla_consts()
        _gla_gates(ga_ref, waf_ref, baf_ref, la_s.at[0], s, tm)
        _gla_gates(ga_ref, wab_ref, bab_ref, la_s.at[1], s, tm)
        for d in range(2):
            tri = (low, up)[d]
            tri_f = tri.astype(F32)
            st[...] = jnp.zeros_like(st)

            def step(i, carry):
                n = i if d == 0 else nchunk - 1 - i
                r = pl.ds(pl.multiple_of(n * c, c), c)
                q = q_ref[r, :] * scale
                k = k_ref[r, :]
                eq, ek, el, dec = _gla_chunk(q, k, la_s[d, r, :], tri_f)
                qt = q * eq
                ktb = (k * ek).astype(BF16)
                kl = k * el
                for hh in range(2):
                    cols = slice(LANES * hh, LANES * (hh + 1))
                    vb = v_ref[r, cols].astype(BF16)
                    qm = (qt * heads[hh]).astype(BF16)
                    a = jnp.where(tri, _dot(qm, ktb, NT), 0.0)
                    o = _dot(a.astype(BF16), vb) + _dot(qm, st[hh].astype(BF16), NT)
                    if d == 0:
                        o_ref[r, cols] = o
                    else:
                        o_ref[r, cols] += o
                    st[hh] = st[hh] * dec + _dot(vb, (kl * heads[hh]).astype(BF16), TN)
                return carry

            lax.fori_loop(0, nchunk, step, 0)

        def epi(i, carry):
            r = pl.ds(pl.multiple_of(i * tm, tm), tm)
            for hh in range(2):
                cols = slice(LANES * hh, LANES * (hh + 1))
                o = o_ref[r, cols]
                rr = lax.rsqrt(jnp.mean(o * o, axis=-1, keepdims=True) + EPS)
                y_ref[r, cols] = (_silu(g_ref[r, cols]) * (o * rr * ng_ref[:, cols])).astype(BF16)
            return carry

        lax.fori_loop(0, s // tm, epi, 0)

    w2 = 2 * LANES
    return pl.pallas_call(
        body, name="gla_fwd", grid=(2,),
        in_specs=[pl.BlockSpec((s, LANES), lambda p: (0, B_Q + p), pipeline_mode=one),
                  pl.BlockSpec((s, LANES), lambda p: (0, B_K + p), pipeline_mode=one),
                  pl.BlockSpec((s, w2), lambda p: (0, B_V // 2 + p), pipeline_mode=one),
                  pl.BlockSpec((s, LANES), lambda p: (0, GA), pipeline_mode=one),
                  pl.BlockSpec((s, w2), lambda p: (0, B_G // 2 + p), pipeline_mode=one),
                  pl.BlockSpec((LANES, LANES), lambda p: (0, p)),
                  pl.BlockSpec((LANES, LANES), lambda p: (0, p)),
                  pl.BlockSpec((1, LANES), lambda p: (0, p)),
                  pl.BlockSpec((1, LANES), lambda p: (0, p)),
                  pl.BlockSpec((1, w2), lambda p: (0, p))],
        out_specs=[pl.BlockSpec((s, w2), lambda p: (0, p)), pl.BlockSpec((s, w2), lambda p: (0, p))],
        out_shape=[jax.ShapeDtypeStruct((s, GROUP_W), F32), jax.ShapeDtypeStruct((s, GROUP_W), BF16)],
        scratch_shapes=[pltpu.VMEM((2, s, LANES), F32), pltpu.VMEM((2, LANES, LANES), F32)],
        compiler_params=_params(("arbitrary",)),
    )(z, z, z, z, z, wa_f, wa_b, ba_f, ba_b, norm_g)


def _gla_bwd(z, d_o, wa_f, wa_b, ba_f, ba_b):
    s = z.shape[0]
    c = GLA_CHUNK
    nchunk = s // c
    scale = GLA_DK ** -0.5
    tm = min(s, 512)
    one = pl.Buffered(1)

    def body(q_ref, k_ref, v_ref, ga_ref, do_ref, waf_ref, wab_ref, baf_ref, bab_ref,
             dq_ref, dk_ref, dv_ref, dga_ref, dwaf_ref, dwab_ref, dbaf_ref, dbab_ref,
             la_s, dla_s, stash, st, dq_s, dk_s, dv_s):
        low, up, heads = _gla_consts()
        rowi = lax.broadcasted_iota(jnp.int32, (c, 1), 0)
        _gla_gates(ga_ref, waf_ref, baf_ref, la_s.at[0], s, tm)
        _gla_gates(ga_ref, wab_ref, bab_ref, la_s.at[1], s, tm)
        for d in range(2):
            tri = (low, up)[d]
            tri_f = tri.astype(F32)
            tri_t = (up, low)[d].astype(F32)
            last_row = (rowi == (c - 1 if d == 0 else 0)).astype(F32)
            order = (lambda i: i) if d == 0 else (lambda i: nchunk - 1 - i)
            st[...] = jnp.zeros_like(st)

            def states(i, carry):
                n = order(i)
                r = pl.ds(pl.multiple_of(n * c, c), c)
                k = k_ref[r, :]
                la = la_s[d, r, :]
                _, _, el, dec = _gla_chunk(k, k, la, tri_f)
                kl = k * el
                for hh in range(2):
                    cols = slice(LANES * hh, LANES * (hh + 1))
                    stash[hh, n] = st[hh]
                    st[hh] = st[hh] * dec + _dot(v_ref[r, cols].astype(BF16), (kl * heads[hh]).astype(BF16), TN)
                return carry

            lax.fori_loop(0, nchunk, states, 0)
            st[...] = jnp.zeros_like(st)

            def step(i, carry):
                n = order(nchunk - 1 - i)
                r = pl.ds(pl.multiple_of(n * c, c), c)
                q = q_ref[r, :] * scale
                k = k_ref[r, :]
                eq, ek, el, dec = _gla_chunk(q, k, la_s[d, r, :], tri_f)
                qt = q * eq
                kt = k * ek
                kl = k * el
                ktb = kt.astype(BF16)
                dqt = jnp.zeros((c, LANES), F32)
                dkt = jnp.zeros((c, LANES), F32)
                dkl = jnp.zeros((c, LANES), F32)
                ddec = jnp.zeros((1, LANES), F32)
                for hh in range(2):
                    cols = slice(LANES * hh, LANES * (hh + 1))
                    vb = v_ref[r, cols].astype(BF16)
                    dob = do_ref[r, cols].astype(BF16)
                    qm = (qt * heads[hh]).astype(BF16)
                    a = jnp.where(tri, _dot(qm, ktb, NT), 0.0).astype(BF16)
                    da = jnp.where(tri, _dot(dob, vb, NT), 0.0).astype(BF16)
                    sn = stash[hh, n]
                    dst = st[hh]
                    dstb = dst.astype(BF16)
                    dqt = dqt + (_dot(da, ktb) + _dot(dob, sn.astype(BF16))) * heads[hh]
                    dkt = dkt + _dot(da, qm, TN)
                    dv = _dot(a, dob, TN) + _dot((kl * heads[hh]).astype(BF16), dstb, NT)
                    dkl = dkl + _dot(vb, dstb)
                    ddec = ddec + jnp.sum(dst * sn, axis=0, keepdims=True)
                    st[hh] = dst * dec + _dot(dob, qm, TN)
                    if d == 0:
                        dv_s[r, cols] = dv
                    else:
                        dv_ref[r, cols] = (dv_s[r, cols] + dv).astype(BF16)
                dlast = ddec * dec + jnp.sum(dkl * kl, axis=0, keepdims=True)
                dq = dqt * eq * scale
                dk = dkt * ek + dkl * el
                dcum = dqt * qt - dkt * kt - dkl * kl + last_row * dlast
                dla_s[d, r, :] = _dot_exact(tri_t, dcum)
                if d == 0:
                    dq_s[r, :] = dq
                    dk_s[r, :] = dk
                else:
                    dq_ref[r, :] = (dq_s[r, :] + dq).astype(BF16)
                    dk_ref[r, :] = (dk_s[r, :] + dk).astype(BF16)
                return carry

            lax.fori_loop(0, nchunk, step, 0)

        first = pl.program_id(0) == 0
        for d, (wa_ref, ba_ref, dwa_ref, dba_ref) in enumerate(
                ((waf_ref, baf_ref, dwaf_ref, dbaf_ref), (wab_ref, bab_ref, dwab_ref, dbab_ref))):
            dwa_ref[...] = jnp.zeros_like(dwa_ref)
            dba_ref[...] = jnp.zeros_like(dba_ref)

            def gates(i, carry):
                r = pl.ds(pl.multiple_of(i * tm, tm), tm)
                gab = ga_ref[r, :].astype(BF16)
                wab16 = wa_ref[...].astype(BF16)
                pre = _dot(gab, wab16) + ba_ref[...]
                dpre = dla_s[d, r, :] * (1.0 / GLA_TAU) * _sigmoid(-pre)
                dpb = dpre.astype(BF16)
                dwa_ref[...] += _dot(gab, dpb, TN)
                dba_ref[...] += jnp.sum(dpre, axis=0, keepdims=True)
                dga = _dot(dpb, wab16, NT)
                if d == 0:
                    @pl.when(first)
                    def _():
                        dga_ref[r, :] = dga

                    @pl.when(jnp.logical_not(first))
                    def _():
                        dga_ref[r, :] += dga
                else:
                    dga_ref[r, :] += dga
                return carry

            lax.fori_loop(0, s // tm, gates, 0)

    w2 = 2 * LANES
    return pl.pallas_call(
        body, name="gla_bwd", grid=(2,),
        in_specs=[pl.BlockSpec((s, LANES), lambda p: (0, B_Q + p), pipeline_mode=one),
                  pl.BlockSpec((s, LANES), lambda p: (0, B_K + p), pipeline_mode=one),
                  pl.BlockSpec((s, w2), lambda p: (0, B_V // 2 + p), pipeline_mode=one),
                  pl.BlockSpec((s, LANES), lambda p: (0, GA), pipeline_mode=one),
                  pl.BlockSpec((s, w2), lambda p: (0, p), pipeline_mode=one),
                  pl.BlockSpec((LANES, LANES), lambda p: (0, p)),
                  pl.BlockSpec((LANES, LANES), lambda p: (0, p)),
                  pl.BlockSpec((1, LANES), lambda p: (0, p)),
                  pl.BlockSpec((1, LANES), lambda p: (0, p))],
        out_specs=[pl.BlockSpec((s, LANES), lambda p: (0, p), pipeline_mode=one),
                   pl.BlockSpec((s, LANES), lambda p: (0, p), pipeline_mode=one),
                   pl.BlockSpec((s, w2), lambda p: (0, p), pipeline_mode=one),
                   pl.BlockSpec((s, LANES), lambda p: (0, 0), pipeline_mode=one),
                   pl.BlockSpec((LANES, LANES), lambda p: (0, p)),
                   pl.BlockSpec((LANES, LANES), lambda p: (0, p)),
                   pl.BlockSpec((1, LANES), lambda p: (0, p)),
                   pl.BlockSpec((1, LANES), lambda p: (0, p))],
        out_shape=[jax.ShapeDtypeStruct((s, w2), BF16), jax.ShapeDtypeStruct((s, w2), BF16),
                   jax.ShapeDtypeStruct((s, GROUP_W), BF16), jax.ShapeDtypeStruct((s, LANES), F32),
                   jax.ShapeDtypeStruct((LANES, w2), F32), jax.ShapeDtypeStruct((LANES, w2), F32),
                   jax.ShapeDtypeStruct((1, w2), F32), jax.ShapeDtypeStruct((1, w2), F32)],
        scratch_shapes=[pltpu.VMEM((2, s, LANES), F32), pltpu.VMEM((2, s, LANES), F32),
                        pltpu.VMEM((2, nchunk, LANES, LANES), F32), pltpu.VMEM((2, LANES, LANES), F32),
                        pltpu.VMEM((s, LANES), F32), pltpu.VMEM((s, LANES), F32), pltpu.VMEM((s, w2), F32)],
        compiler_params=_params(("arbitrary",)),
    )(z, z, z, z, d_o, wa_f, wa_b, ba_f, ba_b)


def _shift_rows(x, d, rowi):
    s = x.shape[0]
    if d == 0:
        return x
    y = pltpu.roll(x, d % s, 0)
    keep = (rowi >= d) if d > 0 else (rowi < s + d)
    return jnp.where(keep, y, 0.0)


def _run_sum(x, m, step, rowi):
    acc, n = x, 1
    while n < m:
        acc = acc + _shift_rows(acc, step * n, rowi)
        n *= 2
    return acc


def _pool_counts(s, w, rowi):
    hi = jnp.minimum(rowi + w // 2, s)
    lo = jnp.maximum(rowi - w // 2, 0)
    return (hi - lo).astype(F32)


def _pooled(u, w, rowi):
    s = u.shape[0]
    win = _shift_rows(_run_sum(u, w // 2, 1, rowi), 1, rowi) + _run_sum(u, w // 2, -1, rowi)
    return win / _pool_counts(s, w, rowi) - u


def _pool_fwd(z, pool_w, pool_scale):
    s = z.shape[0]
    one = pl.Buffered(1)

    def body(u_ref, g_ref, w_ref, sc_ref, y_ref):
        rowi = lax.broadcasted_iota(jnp.int32, (s, 1), 0)
        for g, w in enumerate(POOL_WINDOWS):
            cols = slice(LANES * g, LANES * (g + 1))
            pooled = _pooled(u_ref[:, cols], w, rowi)
            mixed = _dot(pooled.astype(BF16), w_ref[g].astype(BF16))
            y_ref[:, cols] = (_silu(g_ref[:, cols]) * (mixed * sc_ref[:, cols])).astype(BF16)

    return pl.pallas_call(
        body, name="pool_fwd", grid=(1,),
        in_specs=[pl.BlockSpec((s, GROUP_W), lambda i: (0, C_V // 4), pipeline_mode=one),
                  pl.BlockSpec((s, GROUP_W), lambda i: (0, C_G // 4), pipeline_mode=one),
                  pl.BlockSpec((4, LANES, LANES), lambda i: (0, 0, 0)),
                  pl.BlockSpec((1, GROUP_W), lambda i: (0, 0))],
        out_specs=pl.BlockSpec((s, GROUP_W), lambda i: (0, 0), pipeline_mode=one),
        out_shape=jax.ShapeDtypeStruct((s, GROUP_W), BF16),
        compiler_params=_params(("arbitrary",)),
    )(z, z, pool_w, pool_scale)


def _pool_bwd(z, dy, pool_w, pool_scale):
    s = z.shape[0]
    one = pl.Buffered(1)

    def body(u_ref, g_ref, dy_ref, w_ref, sc_ref, du_ref, dg_ref, dw_ref, dsc_ref):
        rowi = lax.broadcasted_iota(jnp.int32, (s, 1), 0)
        for g, w in enumerate(POOL_WINDOWS):
            cols = slice(LANES * g, LANES * (g + 1))
            gate, dyv, sc = g_ref[:, cols], dy_ref[:, cols], sc_ref[:, cols]
            wb = w_ref[g].astype(BF16)
            pooled = _pooled(u_ref[:, cols], w, rowi)
            pb = pooled.astype(BF16)
            mixed = _dot(pb, wb)
            dg_ref[:, cols] = (dyv * (mixed * sc) * _silu_grad(gate)).astype(BF16)
            dt = dyv * _silu(gate)
            dsc_ref[:, cols] = jnp.sum(dt * mixed, axis=0, keepdims=True)
            dmb = (dt * sc).astype(BF16)
            dw_ref[g] = _dot(pb, dmb, TN)
            dpool = _dot(dmb, wb, NT)
            e = dpool / _pool_counts(s, w, rowi)
            du_ref[:, cols] = (_run_sum(e, w // 2, 1, rowi) + _shift_rows(_run_sum(e, w // 2, -1, rowi), -1, rowi)
                               - dpool).astype(BF16)

    return pl.pallas_call(
        body, name="pool_bwd", grid=(1,),
        in_specs=[pl.BlockSpec((s, GROUP_W), lambda i: (0, C_V // 4), pipeline_mode=one),
                  pl.BlockSpec((s, GROUP_W), lambda i: (0, C_G // 4), pipeline_mode=one),
                  pl.BlockSpec((s, GROUP_W), lambda i: (0, 2), pipeline_mode=one),
                  pl.BlockSpec((4, LANES, LANES), lambda i: (0, 0, 0)),
                  pl.BlockSpec((1, GROUP_W), lambda i: (0, 0))],
        out_specs=[pl.BlockSpec((s, GROUP_W), lambda i: (0, 0), pipeline_mode=one),
                   pl.BlockSpec((s, GROUP_W), lambda i: (0, 0), pipeline_mode=one),
                   pl.BlockSpec((4, LANES, LANES), lambda i: (0, 0, 0)),
                   pl.BlockSpec((1, GROUP_W), lambda i: (0, 0))],
        out_shape=[jax.ShapeDtypeStruct((s, GROUP_W), BF16), jax.ShapeDtypeStruct((s, GROUP_W), BF16),
                   jax.ShapeDtypeStruct((4, LANES, LANES), F32), jax.ShapeDtypeStruct((1, GROUP_W), F32)],
        compiler_params=_params(("arbitrary",)),
    )(z, z, dy, pool_w, pool_scale)


def _mla_heads(qf, kv, kpe, qg, kg, cos, sin):
    out = []
    for h in range(4):
        qa = qf[:, LANES * h:LANES * (h + 1)]
        qb = qf[:, 512 + LANES * h:512 + LANES * (h + 1)]
        ka = kv[:, 256 * h:256 * h + LANES]
        rq = lax.rsqrt((jnp.sum(qa * qa, axis=-1, keepdims=True) + jnp.sum(qb * qb, axis=-1, keepdims=True))
                       * (1.0 / MLA_QK) + EPS)
        rk = lax.rsqrt((jnp.sum(ka * ka, axis=-1, keepdims=True) + jnp.sum(kpe * kpe, axis=-1, keepdims=True))
                       * (1.0 / MLA_QK) + EPS)
        out.append((qa, qb, rq, ka, rk))
    return out


def _mla_latents(mq_ref, mkv_ref, gq_ref, gkv_ref, wq_ref, wkv_ref):
    mq = mq_ref[...]
    rq = lax.rsqrt(jnp.mean(mq * mq, axis=-1, keepdims=True) + EPS)
    qn = mq * rq
    qnb = (qn * gq_ref[...]).astype(BF16)
    mkv = mkv_ref[...]
    rk = lax.rsqrt(jnp.mean(mkv * mkv, axis=-1, keepdims=True) + EPS)
    kvn = mkv * rk
    kvnb = (kvn * gkv_ref[...]).astype(BF16)
    qf = _dot(qnb, wq_ref[...])
    kv = _dot(kvnb, wkv_ref[...])
    return qn, rq, qnb, kvn, rk, kvnb, qf, kv


def _mla_prep(z, cos_m, sin_m, gq, wq, gkv, wkv, qg, kg, tm):
    s = z.shape[0]

    def body(mq_ref, mkv_ref, mkr_ref, cos_ref, sin_ref, gq_ref, wq_ref, gkv_ref, wkv_ref, qg_ref, kg_ref,
             q_ref, k_ref, v_ref):
        _, _, _, _, _, _, qf, kv = _mla_latents(mq_ref, mkv_ref, gq_ref, gkv_ref, wq_ref, wkv_ref)
        kpe = mkr_ref[...]
        cos, sin = cos_ref[...], sin_ref[...]
        qg, kg = qg_ref[...], kg_ref[...]
        for h, (qa, qb, rq, ka, rk) in enumerate(_mla_heads(qf, kv, kpe, qg, kg, cos, sin)):
            q_ref[h, :, 0:LANES] = (qa * rq * qg[:, 0:LANES]).astype(BF16)
            q_ref[h, :, LANES:] = _rope(qb * rq * qg[:, LANES:], cos, sin).astype(BF16)
            k_ref[h, :, 0:LANES] = (ka * rk * kg[:, 0:LANES]).astype(BF16)
            k_ref[h, :, LANES:] = _rope(kpe * rk * kg[:, LANES:], cos, sin).astype(BF16)
            v_ref[h] = kv[:, 256 * h + LANES:256 * (h + 1)].astype(BF16)

    full = lambda shape: pl.BlockSpec(shape, lambda i: (0,) * len(shape))
    return pl.pallas_call(
        body, name="mla_prep", grid=(s // tm,),
        in_specs=[pl.BlockSpec((tm, 512), lambda i: (i, M_Q // 4)),
                  pl.BlockSpec((tm, 256), lambda i: (i, M_KV // 2)),
                  pl.BlockSpec((tm, LANES), lambda i: (i, M_KR)),
                  pl.BlockSpec((tm, LANES), lambda i: (i, 0)),
                  pl.BlockSpec((tm, LANES), lambda i: (i, 0)),
                  full((1, 512)), full((512, 1024)), full((1, 256)), full((256, 1024)), full((1, 256)), full((1, 256))],
        out_specs=[pl.BlockSpec((4, tm, 256), lambda i: (0, i, 0)), pl.BlockSpec((4, tm, 256), lambda i: (0, i, 0)),
                   pl.BlockSpec((4, tm, LANES), lambda i: (0, i, 0))],
        out_shape=[jax.ShapeDtypeStruct((4, s, 256), BF16), jax.ShapeDtypeStruct((4, s, 256), BF16),
                   jax.ShapeDtypeStruct((4, s, LANES), BF16)],
        compiler_params=_params(("parallel",)),
    )(z, z, z, cos_m, sin_m, gq, wq, gkv, wkv, qg, kg)


def _mla_prep_bwd(z, cos_m, sin_m, gq, wq, gkv, wkv, qg, kg, dq, dk, dv, tm):
    s = z.shape[0]

    def body(mq_ref, mkv_ref, mkr_ref, cos_ref, sin_ref, gq_ref, wq_ref, gkv_ref, wkv_ref, qg_ref, kg_ref,
             dq_ref, dk_ref, dv_ref,
             dmq_ref, dmkv_ref, dmkr_ref, dwq_ref, dwkv_ref, dgq_ref, dgkv_ref, dqg_ref, dkg_ref, dqf, dkv):
        @pl.when(pl.program_id(0) == 0)
        def _():
            for r in (dwq_ref, dwkv_ref, dgq_ref, dgkv_ref, dqg_ref, dkg_ref):
                r[...] = jnp.zeros_like(r)

        qn, rq0, qnb, kvn, rk0, kvnb, qf, kv = _mla_latents(mq_ref, mkv_ref, gq_ref, gkv_ref, wq_ref, wkv_ref)
        kpe = mkr_ref[...]
        cos, sin = cos_ref[...], sin_ref[...]
        qg, kg = qg_ref[...], kg_ref[...]
        dkpe = jnp.zeros_like(kpe)
        inv = 1.0 / MLA_QK

        def norm_bwd(a, b, r, da_n, db_n, g):
            ga, gb = g[:, 0:LANES], g[:, LANES:]
            dg_a = jnp.sum(da_n * a * r, axis=0, keepdims=True)
            dg_b = jnp.sum(db_n * b * r, axis=0, keepdims=True)
            ua, ub = da_n * ga, db_n * gb
            dt = (jnp.sum(ua * a, axis=-1, keepdims=True) + jnp.sum(ub * b, axis=-1, keepdims=True)) * inv
            r3 = r * r * r
            return r * ua - a * (r3 * dt), r * ub - b * (r3 * dt), dg_a, dg_b

        for h, (qa, qb, rq, ka, rk) in enumerate(_mla_heads(qf, kv, kpe, qg, kg, cos, sin)):
            dqa, dqb, dga, dgb = norm_bwd(qa, qb, rq, dq_ref[h, :, 0:LANES], _rope_t(dq_ref[h, :, LANES:], cos, sin), qg)
            dqf[:, LANES * h:LANES * (h + 1)] = dqa
            dqf[:, 512 + LANES * h:512 + LANES * (h + 1)] = dqb
            dqg_ref[:, 0:LANES] += dga
            dqg_ref[:, LANES:] += dgb
            dka, dkb, dga, dgb = norm_bwd(ka, kpe, rk, dk_ref[h, :, 0:LANES], _rope_t(dk_ref[h, :, LANES:], cos, sin), kg)
            dkv[:, 256 * h:256 * h + LANES] = dka
            dkv[:, 256 * h + LANES:256 * (h + 1)] = dv_ref[h]
            dkpe = dkpe + dkb
            dkg_ref[:, 0:LANES] += dga
            dkg_ref[:, LANES:] += dgb
        dmkr_ref[...] = dkpe.astype(BF16)

        def latent_bwd(dfull, w_ref, nb, n, r, g_ref, dw_ref, dg_ref, dlat_ref):
            db = dfull.astype(BF16)
            dn = _dot(db, w_ref[...], NT)
            dw_ref[...] += _dot(nb, db, TN)
            dg_ref[...] += jnp.sum(dn * n, axis=0, keepdims=True)
            u = dn * g_ref[...]
            dlat_ref[...] = (r * (u - n * jnp.mean(u * n, axis=-1, keepdims=True))).astype(BF16)

        latent_bwd(dqf[...], wq_ref, qnb, qn, rq0, gq_ref, dwq_ref, dgq_ref, dmq_ref)
        latent_bwd(dkv[...], wkv_ref, kvnb, kvn, rk0, gkv_ref, dwkv_ref, dgkv_ref, dmkv_ref)

    full = lambda shape: pl.BlockSpec(shape, lambda i: (0,) * len(shape))
    return pl.pallas_call(
        body, name="mla_prep_bwd", grid=(s // tm,),
        in_specs=[pl.BlockSpec((tm, 512), lambda i: (i, M_Q // 4)),
                  pl.BlockSpec((tm, 256), lambda i: (i, M_KV // 2)),
                  pl.BlockSpec((tm, LANES), lambda i: (i, M_KR)),
                  pl.BlockSpec((tm, LANES), lambda i: (i, 0)),
                  pl.BlockSpec((tm, LANES), lambda i: (i, 0)),
                  full((1, 512)), full((512, 1024)), full((1, 256)), full((256, 1024)), full((1, 256)), full((1, 256)),
                  pl.BlockSpec((4, tm, 256), lambda i: (0, i, 0)), pl.BlockSpec((4, tm, 256), lambda i: (0, i, 0)),
                  pl.BlockSpec((4, tm, LANES), lambda i: (0, i, 0))],
        out_specs=[pl.BlockSpec((tm, 512), lambda i: (i, 0)), pl.BlockSpec((tm, 256), lambda i: (i, 0)),
                   pl.BlockSpec((tm, LANES), lambda i: (i, 0)),
                   full((512, 1024)), full((256, 1024)), full((1, 512)), full((1, 256)), full((1, 256)), full((1, 256))],
        out_shape=[jax.ShapeDtypeStruct((s, 512), BF16), jax.ShapeDtypeStruct((s, 256), BF16),
                   jax.ShapeDtypeStruct((s, LANES), BF16),
                   jax.ShapeDtypeStruct((512, 1024), F32), jax.ShapeDtypeStruct((256, 1024), F32),
                   jax.ShapeDtypeStruct((1, 512), F32), jax.ShapeDtypeStruct((1, 256), F32),
                   jax.ShapeDtypeStruct((1, 256), F32), jax.ShapeDtypeStruct((1, 256), F32)],
        scratch_shapes=[pltpu.VMEM((tm, 1024), F32), pltpu.VMEM((tm, 1024), F32)],
        compiler_params=_params(("arbitrary",)),
    )(z, z, z, cos_m, sin_m, gq, wq, gkv, wkv, qg, kg, dq, dk, dv)


def _attn_fwd(q, k, v, z, tq):
    s = q.shape[1]
    scale = MLA_QK ** -0.5

    def body(q_ref, k_ref, v_ref, g_ref, o_ref, y_ref, lse_ref):
        sc = _dot(q_ref[...], k_ref[...], NT) * scale
        m = jnp.max(sc, axis=-1, keepdims=True)
        p = jnp.exp(sc - m)
        l = jnp.sum(p, axis=-1, keepdims=True)
        o = _dot(p.astype(BF16), v_ref[...]) / l
        o_ref[...] = o
        y_ref[...] = (_silu(g_ref[...]) * o).astype(BF16)
        lse_ref[...] = m + jnp.log(l)

    return pl.pallas_call(
        body, name="attn_fwd", grid=(4, s // tq),
        in_specs=[pl.BlockSpec((None, tq, 256), lambda h, i: (h, i, 0)),
                  pl.BlockSpec((None, s, 256), lambda h, i: (h, 0, 0)),
                  pl.BlockSpec((None, s, LANES), lambda h, i: (h, 0, 0)),
                  pl.BlockSpec((tq, LANES), lambda h, i: (i, M_G + h))],
        out_specs=[pl.BlockSpec((tq, LANES), lambda h, i: (i, h)), pl.BlockSpec((tq, LANES), lambda h, i: (i, h)),
                   pl.BlockSpec((None, tq, 1), lambda h, i: (h, i, 0))],
        out_shape=[jax.ShapeDtypeStruct((s, GROUP_W), F32), jax.ShapeDtypeStruct((s, GROUP_W), BF16),
                   jax.ShapeDtypeStruct((4, s, 1), F32)],
        compiler_params=_params(("parallel", "parallel")),
    )(q, k, v, z)


def _attn_bwd(q, k, v, z, o, lse, dy, tq):
    s = q.shape[1]
    scale = MLA_QK ** -0.5

    def body(q_ref, k_ref, v_ref, g_ref, o_ref, lse_ref, dy_ref, dq_ref, dk_ref, dv_ref, dg_ref):
        @pl.when(pl.program_id(1) == 0)
        def _():
            dk_ref[...] = jnp.zeros_like(dk_ref)
            dv_ref[...] = jnp.zeros_like(dv_ref)

        gate, ov, dyv = g_ref[...], o_ref[...], dy_ref[...]
        do = dyv * _silu(gate)
        dg_ref[...] = (dyv * ov * _silu_grad(gate)).astype(BF16)
        delta = jnp.sum(do * ov, axis=-1, keepdims=True)
        dob = do.astype(BF16)
        qb, kb = q_ref[...], k_ref[...]
        p = jnp.exp(_dot(qb, kb, NT) * scale - lse_ref[...])
        dp = _dot(dob, v_ref[...], NT)
        ds = (p * (dp - delta) * scale).astype(BF16)
        dq_ref[...] = _dot(ds, kb)
        dk_ref[...] += _dot(ds, qb, TN)
        dv_ref[...] += _dot(p.astype(BF16), dob, TN)

    return pl.pallas_call(
        body, name="attn_bwd", grid=(4, s // tq),
        in_specs=[pl.BlockSpec((None, tq, 256), lambda h, i: (h, i, 0)),
                  pl.BlockSpec((None, s, 256), lambda h, i: (h, 0, 0)),
                  pl.BlockSpec((None, s, LANES), lambda h, i: (h, 0, 0)),
                  pl.BlockSpec((tq, LANES), lambda h, i: (i, M_G + h)),
                  pl.BlockSpec((tq, LANES), lambda h, i: (i, h)),
                  pl.BlockSpec((None, tq, 1), lambda h, i: (h, i, 0)),
                  pl.BlockSpec((tq, LANES), lambda h, i: (i, 12 + h))],
        out_specs=[pl.BlockSpec((None, tq, 256), lambda h, i: (h, i, 0)),
                   pl.BlockSpec((None, s, 256), lambda h, i: (h, 0, 0)),
                   pl.BlockSpec((None, s, LANES), lambda h, i: (h, 0, 0)),
                   pl.BlockSpec((tq, LANES), lambda h, i: (i, h))],
        out_shape=[jax.ShapeDtypeStruct((4, s, 256), F32), jax.ShapeDtypeStruct((4, s, 256), F32),
                   jax.ShapeDtypeStruct((4, s, LANES), F32), jax.ShapeDtypeStruct((s, GROUP_W), BF16)],
        compiler_params=_params(("parallel", "arbitrary")),
    )(q, k, v, z, o, lse, dy)


def _loss_head(x, target, tm):
    s, d = x.shape

    def body(x_ref, t_ref, dx_ref, loss_ref):
        @pl.when(pl.program_id(0) == 0)
        def _():
            loss_ref[...] = jnp.zeros_like(loss_ref)
        err = x_ref[...] - t_ref[...]
        dx_ref[...] = err * (1.0 / d)
        per_tok = jnp.mean(err * err, axis=-1, keepdims=True)
        loss_ref[...] += 0.5 * jnp.sum(per_tok, axis=0, keepdims=True)

    return pl.pallas_call(
        body, name="loss_head", grid=(s // tm,),
        in_specs=[pl.BlockSpec((tm, d), lambda i: (i, 0)), pl.BlockSpec((tm, d), lambda i: (i, 0))],
        out_specs=[pl.BlockSpec((tm, d), lambda i: (i, 0)), pl.BlockSpec((1, LANES), lambda i: (0, 0))],
        out_shape=[jax.ShapeDtypeStruct((s, d), F32), jax.ShapeDtypeStruct((1, LANES), F32)],
        compiler_params=_params(("arbitrary",)),
    )(x, target)


def _norm_bwd(x, g, dh, dres, tm):
    s, d = x.shape

    def body(x_ref, g_ref, dh_ref, dres_ref, dx_ref, dg_ref):
        @pl.when(pl.program_id(0) == 0)
        def _():
            dg_ref[...] = jnp.zeros_like(dg_ref)
        xv, dhv = x_ref[...], dh_ref[...]
        r = lax.rsqrt(jnp.mean(xv * xv, axis=-1, keepdims=True) + EPS)
        n = xv * r
        dg_ref[...] += jnp.sum(dhv * n, axis=0, keepdims=True)
        u = dhv * g_ref[...]
        dx_ref[...] = dres_ref[...] + r * (u - n * jnp.mean(u * n, axis=-1, keepdims=True))

    row = lambda: pl.BlockSpec((tm, d), lambda i: (i, 0))
    return pl.pallas_call(
        body, name="norm_bwd", grid=(s // tm,),
        in_specs=[row(), pl.BlockSpec((1, d), lambda i: (0, 0)), row(), row()],
        out_specs=[row(), pl.BlockSpec((1, d), lambda i: (0, 0))],
        out_shape=[jax.ShapeDtypeStruct((s, d), F32), jax.ShapeDtypeStruct((1, d), F32)],
        compiler_params=_params(("arbitrary",)),
    )(x, g, dh, dres)


def _adam(parts, w, m, v, name, tr):
    r, c = w.shape
    tr = min(tr, r)
    c1 = 1.0 - ADAM_B1 ** ADAM_STEP
    c2 = 1.0 - ADAM_B2 ** ADAM_STEP

    def body(p_ref, w_ref, m_ref, v_ref, g_ref, d_ref, nm_ref, nv_ref):
        g = p_ref[0].astype(F32)
        for i in range(1, N_DEV):
            g = g + p_ref[i].astype(F32)
        nm = ADAM_B1 * m_ref[...] + (1.0 - ADAM_B1) * g
        nv = ADAM_B2 * v_ref[...] + (1.0 - ADAM_B2) * (g * g)
        g_ref[...] = g
        nm_ref[...] = nm
        nv_ref[...] = nv
        d_ref[...] = -ADAM_LR * ((nm / c1) / (jnp.sqrt(nv / c2) + ADAM_EPS) + ADAM_WD * w_ref[...])

    blk = lambda: pl.BlockSpec((tr, c), lambda i: (i, 0))
    return pl.pallas_call(
        body, name=name, grid=(r // tr,),
        in_specs=[pl.BlockSpec((N_DEV, tr, c), lambda i: (0, i, 0)), blk(), blk(), blk()],
        out_specs=[blk(), blk(), blk(), blk()],
        out_shape=[jax.ShapeDtypeStruct((r, c), F32)] * 4,
        compiler_params=_params(("parallel",)),
    )(parts, w, m, v)


MESH = pl.DeviceIdType.MESH
ANY = pl.BlockSpec(memory_space=pl.ANY)


def _position():
    return lax.axis_index("x"), lax.axis_index("y"), lax.axis_index("c")


def _all_gather(xs, name):
    r, c = xs.shape

    def body(x_ref, out_ref, send_sems, recv_sems, local_sem):
        x, y, cc = _position()
        me, sibling = (x, y, cc), (x, y, 1 - cc)
        chips = [(1 - x, y), (x, 1 - y), (1 - x, 1 - y)]

        def slab(px, py, pc):
            return out_ref.at[4 * px + 2 * py + pc]

        def copy(k, block, to, src=None):
            return pltpu.make_async_remote_copy(
                src_ref=slab(*block) if src is None else src, dst_ref=slab(*block),
                send_sem=send_sems.at[k], recv_sem=recv_sems.at[k], device_id=to, device_id_type=MESH)

        mine = pltpu.make_async_copy(x_ref, slab(*me), local_sem)
        mine.start()
        first = [copy(0, me, sibling, src=x_ref)]
        first += [copy(1 + j, me, (*chip, cc), src=x_ref) for j, chip in enumerate(chips)]
        for cp in first:
            cp.start()
        passed = [copy(4 + j, (*chip, cc), sibling) for j, chip in enumerate(chips)]
        for j, chip in enumerate(chips):
            copy(1 + j, (*chip, cc), me).wait_recv()
            passed[j].start()
        copy(0, sibling, me).wait_recv()
        for j, chip in enumerate(chips):
            copy(4 + j, (*chip, 1 - cc), me).wait_recv()
        for cp in first + passed:
            cp.wait_send()
        mine.wait()

    return pl.pallas_call(
        body, name=name, out_shape=jax.ShapeDtypeStruct((N_DEV, r, c), xs.dtype),
        in_specs=[ANY], out_specs=ANY,
        scratch_shapes=[pltpu.SemaphoreType.DMA((7,)), pltpu.SemaphoreType.DMA((7,)), pltpu.SemaphoreType.DMA],
    )(xs)


_RELATIONS = ((0, 0, 1), (1, 0, 0), (0, 1, 0), (1, 1, 0), (1, 0, 1), (0, 1, 1), (1, 1, 1))


def _all_to_all(cs, name):
    _, r, c = cs.shape

    def body(c_ref, out_ref, send_sems, recv_sems, local_sem):
        x, y, cc = _position()
        me = 4 * x + 2 * y + cc
        mine = pltpu.make_async_copy(c_ref.at[me], out_ref.at[me], local_sem)
        mine.start()
        copies = []
        for k, (fx, fy, fc) in enumerate(_RELATIONS):
            px = 1 - x if fx else x
            py = 1 - y if fy else y
            pc = 1 - cc if fc else cc
            copies.append(pltpu.make_async_remote_copy(
                src_ref=c_ref.at[4 * px + 2 * py + pc], dst_ref=out_ref.at[me],
                send_sem=send_sems.at[k], recv_sem=recv_sems.at[k], device_id=(px, py, pc), device_id_type=MESH))
        for cp in copies:
            cp.start()
        for cp in copies:
            cp.wait()
        mine.wait()

    return pl.pallas_call(
        body, name=name, out_shape=jax.ShapeDtypeStruct(cs.shape, cs.dtype),
        in_specs=[ANY], out_specs=ANY,
        scratch_shapes=[pltpu.SemaphoreType.DMA((7,)), pltpu.SemaphoreType.DMA((7,)), pltpu.SemaphoreType.DMA],
    )(cs)


REPLICATED = ("norm_g", "ret_norm_g", "gla_ba_f", "gla_ba_b", "gla_norm_g", "pool_w", "pool_scale",
              "mla_q_norm_g", "mla_kv_norm_g", "mla_qk_norm_q", "mla_qk_norm_k")
SMALL_SHARDED = ("mla_wq_b", "mla_wkv_b", "gla_wa2_f", "gla_wa2_b")
WEIGHTS = ("norm_g", "w_in", "ret_norm_g", "gla_wa2_f", "gla_ba_f", "gla_wa2_b", "gla_ba_b", "gla_norm_g", "pool_w",
           "pool_scale", "mla_q_norm_g", "mla_wq_b", "mla_kv_norm_g", "mla_wkv_b", "mla_qk_norm_q", "mla_qk_norm_k",
           "w_out")


def _pack(arrays, dtype):
    flat = jnp.concatenate([a.reshape(-1) for a in arrays]).astype(dtype)
    return flat.reshape(-1, LANES)


def _unpack(packed, like):
    flat = packed.reshape(-1)
    out, at = [], 0
    for a in like:
        out.append(flat[at:at + a.size].reshape(a.shape))
        at += a.size
    return out


def _columns_by_device(g):
    l, r, n = g.shape
    return g.reshape(l, r, N_DEV, n // N_DEV).transpose(2, 0, 1, 3)


def _gathered_columns(g, l, r, c):
    return g.reshape(N_DEV, l, r, c).transpose(1, 2, 0, 3).reshape(l, r, N_DEV * c)


def _layer_forward(x, wts, tables, tm, tq):
    cos_r, sin_r, cos_m, sin_m, tab, _ = tables
    z, h = _inproj(x, wts["norm_g"], wts["w_in"], min(x.shape[0], 2 * tm))
    o_a, y_a = _ret_fwd(z, cos_r, sin_r, tab, wts["ret_norm_g"])
    o_b, y_b = _gla_fwd(z, wts["wa_f"], wts["wa_b"], wts["gla_ba_f"], wts["gla_ba_b"], wts["gla_norm_g"])
    y_c = _pool_fwd(z, wts["pool_w"], wts["pool_scale"])
    q, k, v = _mla_prep(z, cos_m, sin_m, wts["mla_q_norm_g"], wts["wq"], wts["mla_kv_norm_g"], wts["wkv"],
                        wts["qk_q"], wts["qk_k"], tm)
    o_d, y_d, lse = _attn_fwd(q, k, v, z, tq)
    y = jnp.concatenate([y_a, y_b, y_c, y_d], axis=1)
    x_next = _mm(y, wts["w_out"], "nn", "outproj", tm, 2048, 1024, add=x)
    saved = dict(x=x, z=z, h=h, o_a=o_a, o_b=o_b, o_d=o_d, lse=lse, q=q, k=k, v=v, y=y)
    return x_next, saved


def _layer_backward(dx, sv, wts, tables, tm, tq):
    cos_r, sin_r, cos_m, sin_m, tab, tab_sw = tables
    z = sv["z"]
    g = {}
    g["w_out"] = _mm(sv["y"], dx, "tn", "d_w_out", 2048, 1024, 512, out_dtype=BF16)
    dy = _mm(dx, wts["w_out"], "nt", "d_y", tm, 2048, 1024)

    do_a, dg_a, g["ret_norm_g"] = _normgate_bwd(sv["o_a"], z, A_G, dy, 0, wts["ret_norm_g"], tm)
    dq_a, dk_a, dv_a = _ret_bwd(z, do_a, cos_r, sin_r, tab, tab_sw)

    do_b, dg_b, g["gla_norm_g"] = _normgate_bwd(sv["o_b"], z, B_G, dy, 1, wts["gla_norm_g"], tm)
    dq_b, dk_b, dv_b, d_ga, d_waf, d_wab, g["gla_ba_f"], g["gla_ba_b"] = _gla_bwd(
        z, do_b, wts["wa_f"], wts["wa_b"], wts["gla_ba_f"], wts["gla_ba_b"])
    g["gla_wa2_f"] = d_waf[0:GLA_RANK]
    g["gla_wa2_b"] = d_wab[GLA_RANK:2 * GLA_RANK]

    du_c, dg_c, g["pool_w"], g["pool_scale"] = _pool_bwd(z, dy, wts["pool_w"], wts["pool_scale"])

    d_q, d_k, d_v, dg_d = _attn_bwd(sv["q"], sv["k"], sv["v"], z, sv["o_d"], sv["lse"], dy, tq)
    (d_mq, d_mkv, d_mkr, d_wq, g["mla_wkv_b"], g["mla_q_norm_g"], g["mla_kv_norm_g"], d_qg, d_kg) = _mla_prep_bwd(
        z, cos_m, sin_m, wts["mla_q_norm_g"], wts["wq"], wts["mla_kv_norm_g"], wts["wkv"], wts["qk_q"], wts["qk_k"],
        d_q, d_k, d_v, tm)
    g["mla_wq_b"] = _unpad_wq(d_wq)
    g["mla_qk_norm_q"] = d_qg[:, _QK_INV]
    g["mla_qk_norm_k"] = d_kg[:, _QK_INV]

    dz = jnp.concatenate([dq_a, dk_a, dv_a, dg_a, dq_b, dk_b, dv_b, dg_b, d_mq, du_c, dg_c, dg_d, d_mkv,
                          d_ga.astype(BF16), d_mkr], axis=1)
    g["w_in"] = _unpad_w_in(_mm(sv["h"], dz, "tn", "d_w_in", 2048, 1024, 512, out_dtype=BF16))
    dh = _mm(dz, wts["w_in"], "nt", "d_h", 2 * tm, 2048, 1024)
    dx_in, g["norm_g"] = _norm_bwd(sv["x"], wts["norm_g"], dh, dx, tm)
    return dx_in, g


def kernel(x, norm_g, w_in, ret_norm_g, gla_wa2_f, gla_ba_f, gla_wa2_b, gla_ba_b, gla_norm_g, pool_w, pool_scale, mla_q_norm_g, mla_wq_b, mla_kv_norm_g, mla_wkv_b, mla_qk_norm_q, mla_qk_norm_k, w_out, loss_target, m_norm_g, m_w_in, m_ret_norm_g, m_gla_wa2_f, m_gla_ba_f, m_gla_wa2_b, m_gla_ba_b, m_gla_norm_g, m_pool_w, m_pool_scale, m_mla_q_norm_g, m_mla_wq_b, m_mla_kv_norm_g, m_mla_wkv_b, m_mla_qk_norm_q, m_mla_qk_norm_k, m_w_out, v_norm_g, v_w_in, v_ret_norm_g, v_gla_wa2_f, v_gla_ba_f, v_gla_wa2_b, v_gla_ba_b, v_gla_norm_g, v_pool_w, v_pool_scale, v_mla_q_norm_g, v_mla_wq_b, v_mla_kv_norm_g, v_mla_wkv_b, v_mla_qk_norm_q, v_mla_qk_norm_k, v_w_out):
    w = dict(norm_g=norm_g, w_in=w_in, ret_norm_g=ret_norm_g, gla_wa2_f=gla_wa2_f, gla_ba_f=gla_ba_f,
             gla_wa2_b=gla_wa2_b, gla_ba_b=gla_ba_b, gla_norm_g=gla_norm_g, pool_w=pool_w, pool_scale=pool_scale,
             mla_q_norm_g=mla_q_norm_g, mla_wq_b=mla_wq_b, mla_kv_norm_g=mla_kv_norm_g, mla_wkv_b=mla_wkv_b,
             mla_qk_norm_q=mla_qk_norm_q, mla_qk_norm_k=mla_qk_norm_k, w_out=w_out)
    m = dict(norm_g=m_norm_g, w_in=m_w_in, ret_norm_g=m_ret_norm_g, gla_wa2_f=m_gla_wa2_f, gla_ba_f=m_gla_ba_f,
             gla_wa2_b=m_gla_wa2_b, gla_ba_b=m_gla_ba_b, gla_norm_g=m_gla_norm_g, pool_w=m_pool_w,
             pool_scale=m_pool_scale, mla_q_norm_g=m_mla_q_norm_g, mla_wq_b=m_mla_wq_b, mla_kv_norm_g=m_mla_kv_norm_g,
             mla_wkv_b=m_mla_wkv_b, mla_qk_norm_q=m_mla_qk_norm_q, mla_qk_norm_k=m_mla_qk_norm_k, w_out=m_w_out)
    v = dict(norm_g=v_norm_g, w_in=v_w_in, ret_norm_g=v_ret_norm_g, gla_wa2_f=v_gla_wa2_f, gla_ba_f=v_gla_ba_f,
             gla_wa2_b=v_gla_wa2_b, gla_ba_b=v_gla_ba_b, gla_norm_g=v_gla_norm_g, pool_w=v_pool_w,
             pool_scale=v_pool_scale, mla_q_norm_g=v_mla_q_norm_g, mla_wq_b=v_mla_wq_b, mla_kv_norm_g=v_mla_kv_norm_g,
             mla_wkv_b=v_mla_wkv_b, mla_qk_norm_q=v_mla_qk_norm_q, mla_qk_norm_k=v_mla_qk_norm_k, w_out=v_w_out)
    xs, target = x[0], loss_target[0]
    s = xs.shape[0]
    tm, tq = min(s, 512), min(s, 256)
    c_in = w_in.shape[2]

    w_in_g = _all_gather(w_in.astype(BF16).reshape(DEPTH * D_MODEL, c_in), "gather_w_in")
    w_out_g = _all_gather(w_out.astype(BF16).reshape(-1, D_MODEL), "gather_w_out")
    small_g = _all_gather(_pack([w[n] for n in SMALL_SHARDED], BF16), "gather_small")
    w_in_full = _gathered_columns(w_in_g, DEPTH, D_MODEL, c_in)
    w_out_full = w_out_g.reshape(N_DEV, DEPTH, -1, D_MODEL).transpose(1, 0, 2, 3).reshape(DEPTH, -1, D_MODEL)
    sizes = [w[n].size for n in SMALL_SHARDED]
    offs = np.cumsum([0] + sizes)
    small_flat = small_g.reshape(N_DEV, -1)
    small_full = {n: _gathered_columns(small_flat[:, offs[i]:offs[i + 1]], *w[n].shape)
                  for i, n in enumerate(SMALL_SHARDED)}

    tables = _rope_tables(s) + _ret_tables()
    layers = []
    for l in range(DEPTH):
        wa_f = jnp.zeros((LANES, 2 * LANES), BF16).at[0:GLA_RANK].set(small_full["gla_wa2_f"][l])
        wa_b = jnp.zeros((LANES, 2 * LANES), BF16).at[GLA_RANK:2 * GLA_RANK].set(small_full["gla_wa2_b"][l])
        layers.append(dict(
            norm_g=norm_g[l][None], w_in=_pad_w_in(w_in_full[l]), ret_norm_g=ret_norm_g[l][None],
            wa_f=wa_f, wa_b=wa_b, gla_ba_f=gla_ba_f[l][None], gla_ba_b=gla_ba_b[l][None],
            gla_norm_g=gla_norm_g[l][None], pool_w=pool_w[l], pool_scale=pool_scale[l][None],
            mla_q_norm_g=mla_q_norm_g[l][None], wq=_pad_wq(small_full["mla_wq_b"][l]),
            mla_kv_norm_g=mla_kv_norm_g[l][None], wkv=small_full["mla_wkv_b"][l],
            qk_q=_pad_qk_gain(mla_qk_norm_q[l]), qk_k=_pad_qk_gain(mla_qk_norm_k[l]), w_out=w_out_full[l]))

    saved = []
    xl = xs
    for l in range(DEPTH):
        xl, sv = _layer_forward(xl, layers[l], tables, tm, tq)
        saved.append(sv)
    dx, loss_row = _loss_head(xl, target, tm)
    grads = [None] * DEPTH
    for l in reversed(range(DEPTH)):
        dx, grads[l] = _layer_backward(dx, saved[l], layers[l], tables, tm, tq)
    loss = lax.psum(loss_row[0, 0], ("x", "y", "c"))

    full = {n: jnp.stack([grads[l][n].reshape(w[n].shape[1:]) if n in REPLICATED else grads[l][n]
                          for l in range(DEPTH)]) for n in WEIGHTS}

    in_parts = _all_to_all(_columns_by_device(full["w_in"]).reshape(N_DEV, DEPTH * D_MODEL, c_in), "exchange_w_in")
    out_parts = _all_to_all(full["w_out"].reshape(DEPTH, N_DEV, -1, D_MODEL).transpose(1, 0, 2, 3)
                            .reshape(N_DEV, -1, D_MODEL), "exchange_w_out")
    small_c = jnp.concatenate([_columns_by_device(full[n]).reshape(N_DEV, -1) for n in SMALL_SHARDED], axis=1)
    small_parts = _all_to_all(small_c.reshape(N_DEV, -1, LANES), "exchange_small")
    rep_parts = _all_gather(_pack([full[n] for n in REPLICATED], F32), "gather_replicated")

    out = {}
    res = _adam(in_parts, w_in.reshape(-1, c_in), m_w_in.reshape(-1, c_in), v_w_in.reshape(-1, c_in), "adam_w_in", 256)
    out["w_in"] = [a.reshape(w_in.shape) for a in res]
    res = _adam(out_parts, w_out.reshape(-1, D_MODEL), m_w_out.reshape(-1, D_MODEL), v_w_out.reshape(-1, D_MODEL),
                "adam_w_out", 128)
    out["w_out"] = [a.reshape(w_out.shape) for a in res]
    for names, parts, label in ((SMALL_SHARDED, small_parts, "adam_small"), (REPLICATED, rep_parts, "adam_replicated")):
        res = _adam(parts, _pack([w[n] for n in names], F32), _pack([m[n] for n in names], F32),
                    _pack([v[n] for n in names], F32), label, 2048)
        for n, *vals in zip(names, *[_unpack(a, [w[n] for n in names]) for a in res]):
            out[n] = vals

    return (loss, dx[None], *[out[n][0] for n in WEIGHTS], *[out[n][1] for n in WEIGHTS],
            *[out[n][2] for n in WEIGHTS], *[out[n][3] for n in WEIGHTS])
```

```python
import functools
import math

import numpy as np
import jax
import jax.numpy as jnp
from jax import lax
from jax.experimental import pallas as pl
from jax.experimental.pallas import tpu as pltpu

F32 = jnp.float32
BF16 = jnp.bfloat16

N_DEV = 8
D_MODEL = 2048
DEPTH = 2
GROUP_W = 512
EPS = 1e-6
ROPE_THETA = 10000.0
LANES = 128

RET_HD = 128
RET_CHUNK = 128
RET_UNROLL = 4
GLA_CHUNK = 64
GLA_UNROLL = 4
GLA_DK = 64
GLA_TAU = 16.0
GLA_RANK = 16
POOL_WINDOWS = (2, 4, 8, 16)
MLA_QK = 192
MLA_ROPE = 64
IN_COLS = 5984

ADAM_LR = 0.001
ADAM_B1 = 0.9
ADAM_B2 = 0.999
ADAM_EPS = 1e-08
ADAM_WD = 0.01
ADAM_STEP = 10

A_Q, A_K, A_V, A_G = 0, 4, 8, 12
B_Q, B_K, B_V, B_G = 16, 18, 20, 24
M_Q, C_V, C_G, M_G = 28, 32, 36, 40
M_KV, GA, M_KR = 44, 46, 47
ZP_COLS = 48 * LANES

VMEM_LIMIT = 56 * 1024 * 1024


def _params(sem, vmem=VMEM_LIMIT):
    return pltpu.CompilerParams(dimension_semantics=sem, vmem_limit_bytes=vmem)


def _sigmoid(x):
    return 1.0 / (1.0 + jnp.exp(-x))


def _silu(x):
    return x * _sigmoid(x)


def _silu_grad(x):
    s = _sigmoid(x)
    return s * (1.0 + x * (1.0 - s))


def _dot(a, b, dims=(((1,), (0,)), ((), ()))):
    return lax.dot_general(a, b, dims, preferred_element_type=F32)


NT = (((1,), (1,)), ((), ()))
TN = (((0,), (0,)), ((), ()))


def _dot_exact(a, b):
    return lax.dot_general(a, b, (((1,), (0,)), ((), ())), precision=lax.Precision.HIGHEST,
                           preferred_element_type=F32)


def _chunk_loop(n, body, init, unroll):
    unroll = math.gcd(n, unroll)

    def trip(t, carry):
        for u in range(unroll):
            carry = body(t * unroll + u, carry)
        return carry

    return lax.fori_loop(0, n // unroll, trip, init)


def _roll_lanes_half(x):
    return pltpu.roll(x, 64, 1)


def _pad_w_in(w):
    z = lambda n: jnp.zeros((w.shape[0], n), w.dtype)
    return jnp.concatenate([
        w[:, 0:3584],
        w[:, 4640:5152], w[:, 3616:4640], w[:, 5472:5984], w[:, 5152:5408],
        w[:, 3584:3616], z(96),
        w[:, 5408:5440], z(32), w[:, 5440:5472], z(32)], axis=1)


def _unpad_w_in(wp):
    b = lambda blk, n: wp[:, blk * LANES:blk * LANES + n]
    return jnp.concatenate([wp[:, 0:3584], b(GA, 32), b(C_V, 1024), b(M_Q, 512), b(M_KV, 256),
                            b(M_KR, 32), wp[:, M_KR * LANES + 64:M_KR * LANES + 96], b(M_G, 512)], axis=1)


def _wq_perm():
    idx = np.zeros((1024,), np.int32)
    ok = np.zeros((1024,), bool)
    for h in range(4):
        idx[128 * h:128 * h + 128] = 192 * h + np.arange(128)
        ok[128 * h:128 * h + 128] = True
        base = 512 + 128 * h
        idx[base:base + 32] = 192 * h + 128 + np.arange(32)
        ok[base:base + 32] = True
        idx[base + 64:base + 96] = 192 * h + 160 + np.arange(32)
        ok[base + 64:base + 96] = True
    inv = np.zeros((768,), np.int32)
    inv[idx[ok]] = np.nonzero(ok)[0]
    return idx, ok, inv


_WQ_IDX, _WQ_OK, _WQ_INV = _wq_perm()


def _pad_wq(wq):
    return jnp.where(jnp.asarray(_WQ_OK)[None, :], wq[:, _WQ_IDX], 0).astype(wq.dtype)


def _unpad_wq(wqp):
    return wqp[:, _WQ_INV]


def _qk_idx():
    idx = np.zeros((256,), np.int32)
    ok = np.zeros((256,), bool)
    idx[0:128] = np.arange(128)
    ok[0:128] = True
    idx[128:160] = 128 + np.arange(32)
    ok[128:160] = True
    idx[192:224] = 160 + np.arange(32)
    ok[192:224] = True
    inv = np.zeros((192,), np.int32)
    inv[idx[ok]] = np.nonzero(ok)[0]
    return idx, ok, inv


_QK_IDX, _QK_OK, _QK_INV = _qk_idx()


def _pad_qk_gain(g):
    return jnp.where(jnp.asarray(_QK_OK), g[_QK_IDX], 0.0).reshape(1, 256)


def _rope_tables(s):
    def tabs(dim):
        inv = 1.0 / (ROPE_THETA ** (jnp.arange(0, dim, 2, dtype=F32) / dim))
        ang = jnp.arange(s, dtype=F32)[:, None] * inv[None, :]
        return jnp.cos(ang), jnp.sin(ang)
    cr, sr = tabs(RET_HD)
    cos_r = jnp.concatenate([cr, cr], axis=1)
    sin_r = jnp.concatenate([-sr, sr], axis=1)
    cm, sm = tabs(MLA_ROPE)
    zz = jnp.zeros_like(cm)
    cos_m = jnp.concatenate([cm, zz, cm, zz], axis=1)
    sin_m = jnp.concatenate([-sm, zz, sm, zz], axis=1)
    return cos_r, sin_r, cos_m, sin_m


def _rope(x, cos, sin):
    return x * cos + _roll_lanes_half(x) * sin


def _rope_t(x, cos, sin):
    return x * cos + _roll_lanes_half(x * sin)


def _ret_tables():
    c = RET_CHUNK
    gamma_f = 1.0 - 2.0 ** (-5.0 - jnp.arange(4, dtype=F32))
    gamma_b = gamma_f[::-1]
    idx = jnp.arange(c, dtype=F32)
    diff = idx[:, None] - idx[None, :]

    def build(g1, g2):
        l1 = jnp.log(g1)[:, None, None]
        l2 = jnp.log(g2)[:, None, None]
        d1 = jnp.where(diff >= 0, jnp.exp(jnp.maximum(diff, 0.0)[None] * l1), 0.0)
        d2 = jnp.where(diff <= 0, jnp.exp(jnp.maximum(-diff, 0.0)[None] * l2), 0.0)
        ones = jnp.ones((1, c, c), F32)
        col = idx[None, :, None]
        qdf = jnp.exp((col + 1.0) * l1) * ones
        kdf = jnp.exp((c - 1.0 - col) * l1) * ones
        qdb = jnp.exp((c - col) * l2) * ones
        kdb = jnp.exp(col * l2) * ones
        cd1 = jnp.exp(c * l1) * ones
        cd2 = jnp.exp(c * l2) * ones
        return jnp.stack([d1 + d2, qdf, kdf, qdb, kdb, cd1, cd2], axis=1)

    return build(gamma_f, gamma_b), build(gamma_b, gamma_f)


MESH = pl.DeviceIdType.MESH
ANY = pl.BlockSpec(memory_space=pl.ANY)
_RELATIONS = ((0, 0, 1), (1, 0, 0), (0, 1, 0), (1, 1, 0), (1, 0, 1), (0, 1, 1), (1, 1, 1))


def _position():
    return lax.axis_index("x"), lax.axis_index("y"), lax.axis_index("c")


def _gather_copies(x_ref, out_ref, send_sems, recv_sems, local_sem, starting):
    x, y, cc = _position()
    me, sibling = (x, y, cc), (x, y, 1 - cc)
    chips = [(1 - x, y), (x, 1 - y), (1 - x, 1 - y)]

    def slab(px, py, pc):
        return out_ref.at[4 * px + 2 * py + pc]

    def copy(k, block, to, src=None):
        return pltpu.make_async_remote_copy(
            src_ref=slab(*block) if src is None else src, dst_ref=slab(*block),
            send_sem=send_sems.at[k], recv_sem=recv_sems.at[k], device_id=to, device_id_type=MESH)

    mine = pltpu.make_async_copy(x_ref, slab(*me), local_sem)
    first = [copy(0, me, sibling, src=x_ref)] + [copy(1 + j, me, (*chip, cc), src=x_ref) for j, chip in enumerate(chips)]
    if starting:
        return mine, first
    passed = [copy(4 + j, (*chip, cc), sibling) for j, chip in enumerate(chips)]
    arrivals = [copy(1 + j, (*chip, cc), me) for j, chip in enumerate(chips)]
    late = [copy(0, sibling, me)] + [copy(4 + j, (*chip, 1 - cc), me) for j, chip in enumerate(chips)]
    return mine, first, passed, arrivals, late


def _gather_start(*refs):
    mine, first = _gather_copies(*refs, starting=True)
    mine.start()
    for cp in first:
        cp.start()


def _gather_finish(*refs):
    mine, first, passed, arrivals, late = _gather_copies(*refs, starting=False)
    for arrived, onward in zip(arrivals, passed):
        arrived.wait_recv()
        onward.start()
    for cp in late:
        cp.wait_recv()
    for cp in first + passed:
        cp.wait_send()
    mine.wait()


def _scatter_copies(c_ref, out_ref, send_sems, recv_sems, local_sem):
    x, y, cc = _position()
    me = 4 * x + 2 * y + cc
    mine = pltpu.make_async_copy(c_ref.at[me], out_ref.at[me], local_sem)
    copies = []
    for k, (fx, fy, fc) in enumerate(_RELATIONS):
        px = 1 - x if fx else x
        py = 1 - y if fy else y
        pc = 1 - cc if fc else cc
        copies.append(pltpu.make_async_remote_copy(
            src_ref=c_ref.at[4 * px + 2 * py + pc], dst_ref=out_ref.at[me],
            send_sem=send_sems.at[k], recv_sem=recv_sems.at[k], device_id=(px, py, pc), device_id_type=MESH))
    return mine, copies


def _scatter_start(*refs):
    mine, copies = _scatter_copies(*refs)
    mine.start()
    for cp in copies:
        cp.start()


def _scatter_finish(*refs):
    mine, copies = _scatter_copies(*refs)
    for cp in copies:
        cp.wait()
    mine.wait()


_EXCHANGES = {"gather": (_gather_start, _gather_finish), "scatter": (_scatter_start, _scatter_finish)}


def _exchange_scratch():
    return [pltpu.SemaphoreType.DMA((7,)), pltpu.SemaphoreType.DMA((7,)), pltpu.SemaphoreType.DMA]


def _exchange_out(kind, src):
    return jax.ShapeDtypeStruct(((N_DEV,) + src.shape) if kind == "gather" else src.shape, src.dtype)


def _exchange(kind, src, name):
    start, finish = _EXCHANGES[kind]

    def body(*refs):
        start(*refs)
        finish(*refs)

    return pl.pallas_call(body, name=name, out_shape=_exchange_out(kind, src), in_specs=[ANY], out_specs=ANY,
                          scratch_shapes=_exchange_scratch())(src)


def _call(body, name, grid, in_specs, out_specs, out_shape, scratch, sem, args, rider=None):
    if rider is None:
        return pl.pallas_call(body, name=name, grid=grid, in_specs=in_specs, out_specs=out_specs, out_shape=out_shape,
                              scratch_shapes=scratch, compiler_params=_params(sem))(*args)
    kind, src = rider
    start, finish = _EXCHANGES[kind]
    ni, no, ns = len(in_specs), len(out_specs), len(scratch)

    def carried(*refs):
        ins, rsrc = refs[:ni], refs[ni]
        outs, rout = refs[ni + 1:ni + 1 + no], refs[ni + 1 + no]
        scr, sems = refs[ni + 2 + no:ni + 2 + no + ns], refs[ni + 2 + no + ns:]
        ids = [pl.program_id(a) for a in range(len(grid))]
        is_first = functools.reduce(jnp.logical_and, [i == 0 for i in ids])
        is_last = functools.reduce(jnp.logical_and, [i == g - 1 for i, g in zip(ids, grid)])

        @pl.when(is_first)
        def _():
            start(rsrc, rout, *sems)

        body(*ins, *outs, *scr)

        @pl.when(is_last)
        def _():
            finish(rsrc, rout, *sems)

    return pl.pallas_call(
        carried, name=name, grid=grid, in_specs=list(in_specs) + [ANY], out_specs=list(out_specs) + [ANY],
        out_shape=list(out_shape) + [_exchange_out(kind, src)], scratch_shapes=list(scratch) + _exchange_scratch(),
        compiler_params=_params(("arbitrary",) * len(grid)))(*args, src)


def _inproj(x, g, wp, tm, tn=512, rider=None):
    s, d = x.shape
    n = wp.shape[1]

    def body(x_ref, g_ref, w_ref, z_ref, h_ref, hs):
        @pl.when(pl.program_id(1) == 0)
        def _():
            xv = x_ref[...]
            r = lax.rsqrt(jnp.mean(xv * xv, axis=-1, keepdims=True) + EPS)
            hv = (xv * r * g_ref[...]).astype(BF16)
            hs[...] = hv
            h_ref[...] = hv
        z_ref[...] = _dot(hs[...], w_ref[...])

    return _call(
        body, "inproj", (s // tm, n // tn),
        [pl.BlockSpec((tm, d), lambda i, j: (i, 0)),
         pl.BlockSpec((1, d), lambda i, j: (0, 0)),
         pl.BlockSpec((d, tn), lambda i, j: (0, j))],
        [pl.BlockSpec((tm, tn), lambda i, j: (i, j)), pl.BlockSpec((tm, d), lambda i, j: (i, 0))],
        [jax.ShapeDtypeStruct((s, n), F32), jax.ShapeDtypeStruct((s, d), BF16)],
        [pltpu.VMEM((tm, d), BF16)], ("parallel", "arbitrary"), (x, g, wp), rider)


def _relayout_plan():
    runs = ((0, 3584, 0), (3584, 3616, GA * LANES), (3616, 4640, C_V * LANES), (4640, 5152, M_Q * LANES),
            (5152, 5408, M_KV * LANES), (5408, 5440, M_KR * LANES), (5440, 5472, M_KR * LANES + 64),
            (5472, 5984, M_G * LANES))
    shard = IN_COLS // N_DEV
    plan = []
    for d in range(N_DEV):
        lo, hi = shard * d, shard * (d + 1)
        for a, b, p in runs:
            s, e = max(a, lo), min(b, hi)
            if s < e:
                plan.append((d, s - lo, p + (s - a), e - s))
    return plan


def _assemble_w_in(g, tr=256):
    _, r, c = g.shape

    def body(g_ref, o_ref):
        o_ref[...] = jnp.zeros_like(o_ref)
        for d, at, to, w in _relayout_plan():
            o_ref[:, to:to + w] = g_ref[d, :, at:at + w]

    return pl.pallas_call(
        body, name="assemble_w_in", grid=(r // tr,),
        in_specs=[pl.BlockSpec((N_DEV, tr, c), lambda i: (0, i, 0))],
        out_specs=pl.BlockSpec((tr, ZP_COLS), lambda i: (i, 0)),
        out_shape=jax.ShapeDtypeStruct((r, ZP_COLS), g.dtype),
        compiler_params=_params(("parallel",)),
    )(g)


def _split_w_in(wp, tr=256):
    r = wp.shape[0]
    c = IN_COLS // N_DEV

    def body(w_ref, o_ref):
        for d, at, to, w in _relayout_plan():
            o_ref[d, :, at:at + w] = w_ref[:, to:to + w]

    return pl.pallas_call(
        body, name="split_w_in", grid=(r // tr,),
        in_specs=[pl.BlockSpec((tr, ZP_COLS), lambda i: (i, 0))],
        out_specs=pl.BlockSpec((N_DEV, tr, c), lambda i: (0, i, 0)),
        out_shape=jax.ShapeDtypeStruct((N_DEV, r, c), wp.dtype),
        compiler_params=_params(("parallel",)),
    )(wp)


def _mm(a, b, mode, name, tm, tn, tk, add=None, out_dtype=F32, rider=None):
    if mode == "tn":
        k, m = a.shape
    else:
        m, k = a.shape
    n = b.shape[0] if mode == "nt" else b.shape[1]
    tm, tn, tk = min(tm, m), min(tn, n), min(tk, k)
    nk = k // tk
    dims = {"nn": (((1,), (0,)), ((), ())), "nt": NT, "tn": TN}[mode]

    def body(*refs):
        if add is None:
            a_ref, b_ref, o_ref, acc = refs
        else:
            a_ref, b_ref, add_ref, o_ref, acc = refs
        kk = pl.program_id(2)

        @pl.when(kk == 0)
        def _():
            acc[...] = jnp.zeros_like(acc)

        acc[...] += _dot(a_ref[...].astype(BF16), b_ref[...].astype(BF16), dims)

        @pl.when(kk == nk - 1)
        def _():
            r = acc[...]
            if add is not None:
                r = r + add_ref[...]
            o_ref[...] = r.astype(out_dtype)

    a_spec = (pl.BlockSpec((tk, tm), lambda i, j, kk: (kk, i)) if mode == "tn"
              else pl.BlockSpec((tm, tk), lambda i, j, kk: (i, kk)))
    b_spec = (pl.BlockSpec((tn, tk), lambda i, j, kk: (j, kk)) if mode == "nt"
              else pl.BlockSpec((tk, tn), lambda i, j, kk: (kk, j)))
    in_specs = [a_spec, b_spec]
    args = [a, b]
    if add is not None:
        in_specs.append(pl.BlockSpec((tm, tn), lambda i, j, kk: (i, j)))
        args.append(add)
    res = _call(body, name, (m // tm, n // tn, nk), in_specs,
                [pl.BlockSpec((tm, tn), lambda i, j, kk: (i, j))], [jax.ShapeDtypeStruct((m, n), out_dtype)],
                [pltpu.VMEM((tm, tn), F32)], ("parallel", "parallel", "arbitrary"), args, rider)
    return res[0] if rider is None else res


def _ret_core(q_ref, k_ref, v_ref, tab_ref, out_ref, st_ref, nchunk):
    c = RET_CHUNK

    def rows(n):
        return pl.ds(pl.multiple_of(n * c, c), c)

    zero = jnp.zeros((c, c), F32)

    def fwd(n, st):
        r = rows(n)
        q, k, vb = q_ref[r, :], k_ref[r, :], v_ref[r, :].astype(BF16)
        sc = _dot(q.astype(BF16), k.astype(BF16), NT) * tab_ref[0]
        o = _dot(sc.astype(BF16), vb)
        o = o + _dot((q * tab_ref[1]).astype(BF16), st.astype(BF16))
        out_ref[r, :] = o
        return st * tab_ref[5] + _dot((k * tab_ref[2]).astype(BF16), vb, TN)

    _chunk_loop(nchunk, fwd, zero, RET_UNROLL)

    def bwd(i, st):
        r = rows(nchunk - 1 - i)
        q, k, vb = q_ref[r, :], k_ref[r, :], v_ref[r, :].astype(BF16)
        out_ref[r, :] += _dot((q * tab_ref[3]).astype(BF16), st.astype(BF16))
        return st * tab_ref[6] + _dot((k * tab_ref[4]).astype(BF16), vb, TN)

    _chunk_loop(nchunk, bwd, zero, RET_UNROLL)


def _ret_fwd(z, cos_r, sin_r, tab, norm_g):
    s = z.shape[0]
    nchunk = s // RET_CHUNK
    scale = RET_HD ** -0.5
    col = lambda base: pl.BlockSpec((s, LANES), lambda h: (0, base + h), pipeline_mode=pl.Buffered(1))

    def body(q_ref, k_ref, v_ref, g_ref, cos_ref, sin_ref, tab_ref, ng_ref, o_ref, y_ref, qh, kh, st):
        qh[...] = _rope(q_ref[...], cos_ref[...], sin_ref[...])
        kh[...] = _rope(k_ref[...], cos_ref[...], sin_ref[...]) * scale
        _ret_core(qh, kh, v_ref, tab_ref, o_ref, st, nchunk)
        o = o_ref[...]
        r = lax.rsqrt(jnp.mean(o * o, axis=-1, keepdims=True) + EPS)
        y_ref[...] = (_silu(g_ref[...]) * (o * r * ng_ref[...])).astype(BF16)

    return pl.pallas_call(
        body, name="ret_fwd", grid=(4,),
        in_specs=[col(A_Q), col(A_K), col(A_V), col(A_G),
                  pl.BlockSpec((s, LANES), lambda h: (0, 0), pipeline_mode=pl.Buffered(1)),
                  pl.BlockSpec((s, LANES), lambda h: (0, 0), pipeline_mode=pl.Buffered(1)),
                  pl.BlockSpec((None, 7, LANES, LANES), lambda h: (h, 0, 0, 0)),
                  pl.BlockSpec((1, LANES), lambda h: (0, h))],
        out_specs=[pl.BlockSpec((s, LANES), lambda h: (0, h)), pl.BlockSpec((s, LANES), lambda h: (0, h))],
        out_shape=[jax.ShapeDtypeStruct((s, GROUP_W), F32), jax.ShapeDtypeStruct((s, GROUP_W), BF16)],
        scratch_shapes=[pltpu.VMEM((s, LANES), F32), pltpu.VMEM((s, LANES), F32), pltpu.VMEM((LANES, LANES), F32)],
        compiler_params=_params(("arbitrary",)),
    )(z, z, z, z, cos_r, sin_r, tab, norm_g)


def _ret_bwd(z, d_o, cos_r, sin_r, tab, tab_sw):
    s = z.shape[0]
    nchunk = s // RET_CHUNK
    scale = RET_HD ** -0.5
    col = lambda base: pl.BlockSpec((s, LANES), lambda h: (0, base + h), pipeline_mode=pl.Buffered(1))
    whole = lambda: pl.BlockSpec((s, LANES), lambda h: (0, 0), pipeline_mode=pl.Buffered(1))
    tabspec = lambda: pl.BlockSpec((None, 7, LANES, LANES), lambda h: (h, 0, 0, 0))
    outspec = lambda: pl.BlockSpec((s, LANES), lambda h: (0, h))

    def body(q_ref, k_ref, v_ref, do_ref, cos_ref, sin_ref, tab_ref, tsw_ref, dq_ref, dk_ref, dv_ref,
             qh, kh, tmp, st):
        cos, sin = cos_ref[...], sin_ref[...]
        qh[...] = _rope(q_ref[...], cos, sin)
        kh[...] = _rope(k_ref[...], cos, sin) * scale
        _ret_core(kh, qh, do_ref, tsw_ref, tmp, st, nchunk)
        dv_ref[...] = tmp[...].astype(BF16)
        _ret_core(do_ref, v_ref, kh, tab_ref, tmp, st, nchunk)
        dq_ref[...] = _rope_t(tmp[...], cos, sin).astype(BF16)
        _ret_core(v_ref, do_ref, qh, tsw_ref, tmp, st, nchunk)
        dk_ref[...] = _rope_t(tmp[...] * scale, cos, sin).astype(BF16)

    return pl.pallas_call(
        body, name="ret_bwd", grid=(4,),
        in_specs=[col(A_Q), col(A_K), col(A_V),
                  pl.BlockSpec((s, LANES), lambda h: (0, h), pipeline_mode=pl.Buffered(1)),
                  whole(), whole(), tabspec(), tabspec()],
        out_specs=[outspec(), outspec(), outspec()],
        out_shape=[jax.ShapeDtypeStruct((s, GROUP_W), BF16)] * 3,
        scratch_shapes=[pltpu.VMEM((s, LANES), F32), pltpu.VMEM((s, LANES), F32), pltpu.VMEM((s, LANES), F32),
                        pltpu.VMEM((LANES, LANES), F32)],
        compiler_params=_params(("arbitrary",)),
    )(z, z, z, d_o, cos_r, sin_r, tab, tab_sw)


def _normgate_bwd(o, z, gate_blk, dy, dy_blk, norm_g, tm):
    s = o.shape[0]

    def body(o_ref, g_ref, dy_ref, ng_ref, do_ref, dg_ref, dng_ref):
        @pl.when(pl.program_id(0) == 0)
        def _():
            dng_ref[...] = jnp.zeros_like(dng_ref)

        for h in range(4):
            sl = slice(LANES * h, LANES * (h + 1))
            ov, gv, dyv, ng = o_ref[:, sl], g_ref[:, sl], dy_ref[:, sl], ng_ref[:, sl]
            r = lax.rsqrt(jnp.mean(ov * ov, axis=-1, keepdims=True) + EPS)
            on = ov * r
            dn = dyv * _silu(gv)
            u = dn * ng
            do_ref[:, sl] = r * (u - on * jnp.mean(u * on, axis=-1, keepdims=True))
            dg_ref[:, sl] = (dyv * (on * ng) * _silu_grad(gv)).astype(BF16)
            dng_ref[:, sl] += jnp.sum(dn * on, axis=0, keepdims=True)

    return pl.pallas_call(
        body, name="normgate_bwd", grid=(s // tm,),
        in_specs=[pl.BlockSpec((tm, GROUP_W), lambda i: (i, 0)),
                  pl.BlockSpec((tm, GROUP_W), lambda i: (i, gate_blk // 4)),
                  pl.BlockSpec((tm, GROUP_W), lambda i: (i, dy_blk)),
                  pl.BlockSpec((1, GROUP_W), lambda i: (0, 0))],
        out_specs=[pl.BlockSpec((tm, GROUP_W), lambda i: (i, 0)), pl.BlockSpec((tm, GROUP_W), lambda i: (i, 0)),
                   pl.BlockSpec((1, GROUP_W), lambda i: (0, 0))],
        out_shape=[jax.ShapeDtypeStruct((s, GROUP_W), F32), jax.ShapeDtypeStruct((s, GROUP_W), BF16),
                   jax.ShapeDtypeStruct((1, GROUP_W), F32)],
        compiler_params=_params(("arbitrary",)),
    )(o, z, dy, norm_g)


def _log_sigmoid(x):
    return jnp.minimum(x, 0.0) - jnp.log(1.0 + jnp.exp(-jnp.abs(x)))


def _gla_consts():
    c = GLA_CHUNK
    row = lax.broadcasted_iota(jnp.int32, (c, c), 0)
    colm = lax.broadcasted_iota(jnp.int32, (c, c), 1)
    lane = lax.broadcasted_iota(jnp.int32, (1, LANES), 1)
    low = row >= colm
    up = colm >= row
    heads = ((lane < GLA_DK).astype(F32), (lane >= GLA_DK).astype(F32))
    return low, up, heads


def _gla_chunk(q, k, la, tri_f):
    cum = _dot_exact(tri_f, la)
    last = jnp.sum(la, axis=0, keepdims=True)
    eq = jnp.exp(cum)
    ek = jnp.exp(-cum)
    el = jnp.exp(last - cum)
    dec = jnp.exp(last)
    return eq, ek, el, dec


def _gla_gates(ga_ref, wa_ref, ba_ref, la_ref, s, tm):
    def step(i, carry):
        r = pl.ds(pl.multiple_of(i * tm, tm), tm)
        pre = _dot(ga_ref[r, :].astype(BF16), wa_ref[...].astype(BF16)) + ba_ref[...]
        la_ref[r, :] = _log_sigmoid(pre) * (1.0 / GLA_TAU)
        return carry
    lax.fori_loop(0, s // tm, step, 0)


def _gla_fwd(z, wa_f, wa_b, ba_f, ba_b, norm_g):
    s = z.shape[0]
    c = GLA_CHUNK
    nchunk = s // c
    scale = GLA_DK ** -0.5
    tm = min(s, 512)
    one = pl.Buffered(1)

    def body(q_ref, k_ref, v_ref, ga_ref, g_ref, waf_ref, wab_ref, baf_ref, bab_ref, ng_ref, o_ref, y_ref,
             la_s, st):
        low, up, heads = _gla_consts()
        _gla_gates(ga_ref, waf_ref, baf_ref, la_s.at[0], s, tm)
        _gla_gates(ga_ref, wab_ref, bab_ref, la_s.at[1], s, tm)
        for d in range(2):
            tri = (low, up)[d]
            tri_f = tri.astype(F32)

            def step(i, states):
                n = i if d == 0 else nchunk - 1 - i
                r = pl.ds(pl.multiple_of(n * c, c), c)
                q = q_ref[r, :] * scale
                k = k_ref[r, :]
                eq, ek, el, dec = _gla_chunk(q, k, la_s[d, r, :], tri_f)
                qt = q * eq
                ktb = (k * ek).astype(BF16)
                kl = k * el
                new_states = []
                for hh in range(2):
                    cols = slice(LANES * hh, LANES * (hh + 1))
                    vb = v_ref[r, cols].astype(BF16)
                    qm = (qt * heads[hh]).astype(BF16)
                    a = jnp.where(tri, _dot(qm, ktb, NT), 0.0)
                    o = _dot(a.astype(BF16), vb) + _dot(qm, states[hh].astype(BF16), NT)
                    if d == 0:
                        o_ref[r, cols] = o
                    else:
                        o_ref[r, cols] += o
                    new_states.append(states[hh] * dec + _dot(vb, (kl * heads[hh]).astype(BF16), TN))
                return tuple(new_states)

            zero = jnp.zeros((LANES, LANES), F32)
            _chunk_loop(nchunk, step, (zero, zero), GLA_UNROLL)

        def epi(i, carry):
            r = pl.ds(pl.multiple_of(i * tm, tm), tm)
            for hh in range(2):
                cols = slice(LANES * hh, LANES * (hh + 1))
                o = o_ref[r, cols]
                rr = lax.rsqrt(jnp.mean(o * o, axis=-1, keepdims=True) + EPS)
                y_ref[r, cols] = (_silu(g_ref[r, cols]) * (o * rr * ng_ref[:, cols])).astype(BF16)
            return carry

        lax.fori_loop(0, s // tm, epi, 0)

    w2 = 2 * LANES
    return pl.pallas_call(
        body, name="gla_fwd", grid=(2,),
        in_specs=[pl.BlockSpec((s, LANES), lambda p: (0, B_Q + p), pipeline_mode=one),
                  pl.BlockSpec((s, LANES), lambda p: (0, B_K + p), pipeline_mode=one),
                  pl.BlockSpec((s, w2), lambda p: (0, B_V // 2 + p), pipeline_mode=one),
                  pl.BlockSpec((s, LANES), lambda p: (0, GA), pipeline_mode=one),
                  pl.BlockSpec((s, w2), lambda p: (0, B_G // 2 + p), pipeline_mode=one),
                  pl.BlockSpec((LANES, LANES), lambda p: (0, p)),
                  pl.BlockSpec((LANES, LANES), lambda p: (0, p)),
                  pl.BlockSpec((1, LANES), lambda p: (0, p)),
                  pl.BlockSpec((1, LANES), lambda p: (0, p)),
                  pl.BlockSpec((1, w2), lambda p: (0, p))],
        out_specs=[pl.BlockSpec((s, w2), lambda p: (0, p)), pl.BlockSpec((s, w2), lambda p: (0, p))],
        out_shape=[jax.ShapeDtypeStruct((s, GROUP_W), F32), jax.ShapeDtypeStruct((s, GROUP_W), BF16)],
        scratch_shapes=[pltpu.VMEM((2, s, LANES), F32), pltpu.VMEM((2, LANES, LANES), F32)],
        compiler_params=_params(("arbitrary",)),
    )(z, z, z, z, z, wa_f, wa_b, ba_f, ba_b, norm_g)


def _gla_bwd(z, d_o, wa_f, wa_b, ba_f, ba_b):
    s = z.shape[0]
    c = GLA_CHUNK
    nchunk = s // c
    scale = GLA_DK ** -0.5
    tm = min(s, 512)
    one = pl.Buffered(1)

    def body(q_ref, k_ref, v_ref, ga_ref, do_ref, waf_ref, wab_ref, baf_ref, bab_ref,
             dq_ref, dk_ref, dv_ref, dga_ref, dwaf_ref, dwab_ref, dbaf_ref, dbab_ref,
             la_s, dla_s, stash, st, dq_s, dk_s, dv_s):
        low, up, heads = _gla_consts()
        rowi = lax.broadcasted_iota(jnp.int32, (c, 1), 0)
        _gla_gates(ga_ref, waf_ref, baf_ref, la_s.at[0], s, tm)
        _gla_gates(ga_ref, wab_ref, bab_ref, la_s.at[1], s, tm)
        for d in range(2):
            tri = (low, up)[d]
            tri_f = tri.astype(F32)
            tri_t = (up, low)[d].astype(F32)
            last_row = (rowi == (c - 1 if d == 0 else 0)).astype(F32)
            order = (lambda i: i) if d == 0 else (lambda i: nchunk - 1 - i)
            zero = jnp.zeros((LANES, LANES), F32)

            def states(i, sts):
                n = order(i)
                r = pl.ds(pl.multiple_of(n * c, c), c)
                k = k_ref[r, :]
                la = la_s[d, r, :]
                _, _, el, dec = _gla_chunk(k, k, la, tri_f)
                kl = k * el
                new = []
                for hh in range(2):
                    cols = slice(LANES * hh, LANES * (hh + 1))
                    stash[hh, n] = sts[hh]
                    new.append(sts[hh] * dec + _dot(v_ref[r, cols].astype(BF16), (kl * heads[hh]).astype(BF16), TN))
                return tuple(new)

            _chunk_loop(nchunk, states, (zero, zero), GLA_UNROLL)

            def step(i, dsts):
                n = order(nchunk - 1 - i)
                r = pl.ds(pl.multiple_of(n * c, c), c)
                q = q_ref[r, :] * scale
                k = k_ref[r, :]
                eq, ek, el, dec = _gla_chunk(q, k, la_s[d, r, :], tri_f)
                qt = q * eq
                kt = k * ek
                kl = k * el
                ktb = kt.astype(BF16)
                dqt = jnp.zeros((c, LANES), F32)
                dkt = jnp.zeros((c, LANES), F32)
                dkl = jnp.zeros((c, LANES), F32)
                ddec = jnp.zeros((1, LANES), F32)
                new = []
                for hh in range(2):
                    cols = slice(LANES * hh, LANES * (hh + 1))
                    vb = v_ref[r, cols].astype(BF16)
                    dob = do_ref[r, cols].astype(BF16)
                    qm = (qt * heads[hh]).astype(BF16)
                    a = jnp.where(tri, _dot(qm, ktb, NT), 0.0).astype(BF16)
                    da = jnp.where(tri, _dot(dob, vb, NT), 0.0).astype(BF16)
                    sn = stash[hh, n]
                    dst = dsts[hh]
                    dstb = dst.astype(BF16)
                    dqt = dqt + (_dot(da, ktb) + _dot(dob, sn.astype(BF16))) * heads[hh]
                    dkt = dkt + _dot(da, qm, TN)
                    dv = _dot(a, dob, TN) + _dot((kl * heads[hh]).astype(BF16), dstb, NT)
                    dkl = dkl + _dot(vb, dstb)
                    ddec = ddec + jnp.sum(dst * sn, axis=0, keepdims=True)
                    new.append(dst * dec + _dot(dob, qm, TN))
                    if d == 0:
                        dv_s[r, cols] = dv
                    else:
                        dv_ref[r, cols] = (dv_s[r, cols] + dv).astype(BF16)
                dlast = ddec * dec + jnp.sum(dkl * kl, axis=0, keepdims=True)
                dq = dqt * eq * scale
                dk = dkt * ek + dkl * el
                dcum = dqt * qt - dkt * kt - dkl * kl + last_row * dlast
                dla_s[d, r, :] = _dot_exact(tri_t, dcum)
                if d == 0:
                    dq_s[r, :] = dq
                    dk_s[r, :] = dk
                else:
                    dq_ref[r, :] = (dq_s[r, :] + dq).astype(BF16)
                    dk_ref[r, :] = (dk_s[r, :] + dk).astype(BF16)
                return tuple(new)

            _chunk_loop(nchunk, step, (zero, zero), GLA_UNROLL)

        first = pl.program_id(0) == 0
        for d, (wa_ref, ba_ref, dwa_ref, dba_ref) in enumerate(
                ((waf_ref, baf_ref, dwaf_ref, dbaf_ref), (wab_ref, bab_ref, dwab_ref, dbab_ref))):
            dwa_ref[...] = jnp.zeros_like(dwa_ref)
            dba_ref[...] = jnp.zeros_like(dba_ref)

            def gates(i, carry):
                r = pl.ds(pl.multiple_of(i * tm, tm), tm)
                gab = ga_ref[r, :].astype(BF16)
                wab16 = wa_ref[...].astype(BF16)
                pre = _dot(gab, wab16) + ba_ref[...]
                dpre = dla_s[d, r, :] * (1.0 / GLA_TAU) * _sigmoid(-pre)
                dpb = dpre.astype(BF16)
                dwa_ref[...] += _dot(gab, dpb, TN)
                dba_ref[...] += jnp.sum(dpre, axis=0, keepdims=True)
                dga = _dot(dpb, wab16, NT)
                if d == 0:
                    @pl.when(first)
                    def _():
                        dga_ref[r, :] = dga

                    @pl.when(jnp.logical_not(first))
                    def _():
                        dga_ref[r, :] += dga
                else:
                    dga_ref[r, :] += dga
                return carry

            lax.fori_loop(0, s // tm, gates, 0)

    w2 = 2 * LANES
    return pl.pallas_call(
        body, name="gla_bwd", grid=(2,),
        in_specs=[pl.BlockSpec((s, LANES), lambda p: (0, B_Q + p), pipeline_mode=one),
                  pl.BlockSpec((s, LANES), lambda p: (0, B_K + p), pipeline_mode=one),
                  pl.BlockSpec((s, w2), lambda p: (0, B_V // 2 + p), pipeline_mode=one),
                  pl.BlockSpec((s, LANES), lambda p: (0, GA), pipeline_mode=one),
                  pl.BlockSpec((s, w2), lambda p: (0, p), pipeline_mode=one),
                  pl.BlockSpec((LANES, LANES), lambda p: (0, p)),
                  pl.BlockSpec((LANES, LANES), lambda p: (0, p)),
                  pl.BlockSpec((1, LANES), lambda p: (0, p)),
                  pl.BlockSpec((1, LANES), lambda p: (0, p))],
        out_specs=[pl.BlockSpec((s, LANES), lambda p: (0, p), pipeline_mode=one),
                   pl.BlockSpec((s, LANES), lambda p: (0, p), pipeline_mode=one),
                   pl.BlockSpec((s, w2), lambda p: (0, p), pipeline_mode=one),
                   pl.BlockSpec((s, LANES), lambda p: (0, 0), pipeline_mode=one),
                   pl.BlockSpec((LANES, LANES), lambda p: (0, p)),
                   pl.BlockSpec((LANES, LANES), lambda p: (0, p)),
                   pl.BlockSpec((1, LANES), lambda p: (0, p)),
                   pl.BlockSpec((1, LANES), lambda p: (0, p))],
        out_shape=[jax.ShapeDtypeStruct((s, w2), BF16), jax.ShapeDtypeStruct((s, w2), BF16),
                   jax.ShapeDtypeStruct((s, GROUP_W), BF16), jax.ShapeDtypeStruct((s, LANES), F32),
                   jax.ShapeDtypeStruct((LANES, w2), F32), jax.ShapeDtypeStruct((LANES, w2), F32),
                   jax.ShapeDtypeStruct((1, w2), F32), jax.ShapeDtypeStruct((1, w2), F32)],
        scratch_shapes=[pltpu.VMEM((2, s, LANES), F32), pltpu.VMEM((2, s, LANES), F32),
                        pltpu.VMEM((2, nchunk, LANES, LANES), F32), pltpu.VMEM((2, LANES, LANES), F32),
                        pltpu.VMEM((s, LANES), F32), pltpu.VMEM((s, LANES), F32), pltpu.VMEM((s, w2), F32)],
        compiler_params=_params(("arbitrary",)),
    )(z, z, z, z, d_o, wa_f, wa_b, ba_f, ba_b)


def _shift_rows(x, d, rowi):
    s = x.shape[0]
    if d == 0:
        return x
    y = pltpu.roll(x, d % s, 0)
    keep = (rowi >= d) if d > 0 else (rowi < s + d)
    return jnp.where(keep, y, 0.0)


def _run_sum(x, m, step, rowi):
    acc, n = x, 1
    while n < m:
        acc = acc + _shift_rows(acc, step * n, rowi)
        n *= 2
    return acc


def _pool_counts(s, w, rowi):
    hi = jnp.minimum(rowi + w // 2, s)
    lo = jnp.maximum(rowi - w // 2, 0)
    return (hi - lo).astype(F32)


def _pooled(u, w, rowi):
    s = u.shape[0]
    win = _shift_rows(_run_sum(u, w // 2, 1, rowi), 1, rowi) + _run_sum(u, w // 2, -1, rowi)
    return win / _pool_counts(s, w, rowi) - u


def _pool_fwd(z, pool_w, pool_scale):
    s = z.shape[0]
    one = pl.Buffered(1)

    def body(u_ref, g_ref, w_ref, sc_ref, y_ref):
        rowi = lax.broadcasted_iota(jnp.int32, (s, 1), 0)
        for g, w in enumerate(POOL_WINDOWS):
            cols = slice(LANES * g, LANES * (g + 1))
            pooled = _pooled(u_ref[:, cols], w, rowi)
            mixed = _dot(pooled.astype(BF16), w_ref[g].astype(BF16))
            y_ref[:, cols] = (_silu(g_ref[:, cols]) * (mixed * sc_ref[:, cols])).astype(BF16)

    return pl.pallas_call(
        body, name="pool_fwd", grid=(1,),
        in_specs=[pl.BlockSpec((s, GROUP_W), lambda i: (0, C_V // 4), pipeline_mode=one),
                  pl.BlockSpec((s, GROUP_W), lambda i: (0, C_G // 4), pipeline_mode=one),
                  pl.BlockSpec((4, LANES, LANES), lambda i: (0, 0, 0)),
                  pl.BlockSpec((1, GROUP_W), lambda i: (0, 0))],
        out_specs=pl.BlockSpec((s, GROUP_W), lambda i: (0, 0), pipeline_mode=one),
        out_shape=jax.ShapeDtypeStruct((s, GROUP_W), BF16),
        compiler_params=_params(("arbitrary",)),
    )(z, z, pool_w, pool_scale)


def _pool_bwd(z, dy, pool_w, pool_scale):
    s = z.shape[0]
    one = pl.Buffered(1)

    def body(u_ref, g_ref, dy_ref, w_ref, sc_ref, du_ref, dg_ref, dw_ref, dsc_ref):
        rowi = lax.broadcasted_iota(jnp.int32, (s, 1), 0)
        for g, w in enumerate(POOL_WINDOWS):
            cols = slice(LANES * g, LANES * (g + 1))
            gate, dyv, sc = g_ref[:, cols], dy_ref[:, cols], sc_ref[:, cols]
            wb = w_ref[g].astype(BF16)
            pooled = _pooled(u_ref[:, cols], w, rowi)
            pb = pooled.astype(BF16)
            mixed = _dot(pb, wb)
            dg_ref[:, cols] = (dyv * (mixed * sc) * _silu_grad(gate)).astype(BF16)
            dt = dyv * _silu(gate)
            dsc_ref[:, cols] = jnp.sum(dt * mixed, axis=0, keepdims=True)
            dmb = (dt * sc).astype(BF16)
            dw_ref[g] = _dot(pb, dmb, TN)
            dpool = _dot(dmb, wb, NT)
            e = dpool / _pool_counts(s, w, rowi)
            du_ref[:, cols] = (_run_sum(e, w // 2, 1, rowi) + _shift_rows(_run_sum(e, w // 2, -1, rowi), -1, rowi)
                               - dpool).astype(BF16)

    return pl.pallas_call(
        body, name="pool_bwd", grid=(1,),
        in_specs=[pl.BlockSpec((s, GROUP_W), lambda i: (0, C_V // 4), pipeline_mode=one),
                  pl.BlockSpec((s, GROUP_W), lambda i: (0, C_G // 4), pipeline_mode=one),
                  pl.BlockSpec((s, GROUP_W), lambda i: (0, 2), pipeline_mode=one),
                  pl.BlockSpec((4, LANES, LANES), lambda i: (0, 0, 0)),
                  pl.BlockSpec((1, GROUP_W), lambda i: (0, 0))],
        out_specs=[pl.BlockSpec((s, GROUP_W), lambda i: (0, 0), pipeline_mode=one),
                   pl.BlockSpec((s, GROUP_W), lambda i: (0, 0), pipeline_mode=one),
                   pl.BlockSpec((4, LANES, LANES), lambda i: (0, 0, 0)),
                   pl.BlockSpec((1, GROUP_W), lambda i: (0, 0))],
        out_shape=[jax.ShapeDtypeStruct((s, GROUP_W), BF16), jax.ShapeDtypeStruct((s, GROUP_W), BF16),
                   jax.ShapeDtypeStruct((4, LANES, LANES), F32), jax.ShapeDtypeStruct((1, GROUP_W), F32)],
        compiler_params=_params(("arbitrary",)),
    )(z, z, dy, pool_w, pool_scale)


def _mla_heads(qf, kv, kpe, qg, kg, cos, sin):
    out = []
    for h in range(4):
        qa = qf[:, LANES * h:LANES * (h + 1)]
        qb = qf[:, 512 + LANES * h:512 + LANES * (h + 1)]
        ka = kv[:, 256 * h:256 * h + LANES]
        rq = lax.rsqrt((jnp.sum(qa * qa, axis=-1, keepdims=True) + jnp.sum(qb * qb, axis=-1, keepdims=True))
                       * (1.0 / MLA_QK) + EPS)
        rk = lax.rsqrt((jnp.sum(ka * ka, axis=-1, keepdims=True) + jnp.sum(kpe * kpe, axis=-1, keepdims=True))
                       * (1.0 / MLA_QK) + EPS)
        out.append((qa, qb, rq, ka, rk))
    return out


def _mla_latents(mq_ref, mkv_ref, gq_ref, gkv_ref, wq_ref, wkv_ref):
    mq = mq_ref[...]
    rq = lax.rsqrt(jnp.mean(mq * mq, axis=-1, keepdims=True) + EPS)
    qn = mq * rq
    qnb = (qn * gq_ref[...]).astype(BF16)
    mkv = mkv_ref[...]
    rk = lax.rsqrt(jnp.mean(mkv * mkv, axis=-1, keepdims=True) + EPS)
    kvn = mkv * rk
    kvnb = (kvn * gkv_ref[...]).astype(BF16)
    qf = _dot(qnb, wq_ref[...])
    kv = _dot(kvnb, wkv_ref[...])
    return qn, rq, qnb, kvn, rk, kvnb, qf, kv


def _mla_prep(z, cos_m, sin_m, gq, wq, gkv, wkv, qg, kg, tm):
    s = z.shape[0]

    def body(mq_ref, mkv_ref, mkr_ref, cos_ref, sin_ref, gq_ref, wq_ref, gkv_ref, wkv_ref, qg_ref, kg_ref,
             q_ref, k_ref, v_ref):
        _, _, _, _, _, _, qf, kv = _mla_latents(mq_ref, mkv_ref, gq_ref, gkv_ref, wq_ref, wkv_ref)
        kpe = mkr_ref[...]
        cos, sin = cos_ref[...], sin_ref[...]
        qg, kg = qg_ref[...], kg_ref[...]
        for h, (qa, qb, rq, ka, rk) in enumerate(_mla_heads(qf, kv, kpe, qg, kg, cos, sin)):
            q_ref[h, :, 0:LANES] = (qa * rq * qg[:, 0:LANES]).astype(BF16)
            q_ref[h, :, LANES:] = _rope(qb * rq * qg[:, LANES:], cos, sin).astype(BF16)
            k_ref[h, :, 0:LANES] = (ka * rk * kg[:, 0:LANES]).astype(BF16)
            k_ref[h, :, LANES:] = _rope(kpe * rk * kg[:, LANES:], cos, sin).astype(BF16)
            v_ref[h] = kv[:, 256 * h + LANES:256 * (h + 1)].astype(BF16)

    full = lambda shape: pl.BlockSpec(shape, lambda i: (0,) * len(shape))
    return pl.pallas_call(
        body, name="mla_prep", grid=(s // tm,),
        in_specs=[pl.BlockSpec((tm, 512), lambda i: (i, M_Q // 4)),
                  pl.BlockSpec((tm, 256), lambda i: (i, M_KV // 2)),
                  pl.BlockSpec((tm, LANES), lambda i: (i, M_KR)),
                  pl.BlockSpec((tm, LANES), lambda i: (i, 0)),
                  pl.BlockSpec((tm, LANES), lambda i: (i, 0)),
                  full((1, 512)), full((512, 1024)), full((1, 256)), full((256, 1024)), full((1, 256)), full((1, 256))],
        out_specs=[pl.BlockSpec((4, tm, 256), lambda i: (0, i, 0)), pl.BlockSpec((4, tm, 256), lambda i: (0, i, 0)),
                   pl.BlockSpec((4, tm, LANES), lambda i: (0, i, 0))],
        out_shape=[jax.ShapeDtypeStruct((4, s, 256), BF16), jax.ShapeDtypeStruct((4, s, 256), BF16),
                   jax.ShapeDtypeStruct((4, s, LANES), BF16)],
        compiler_params=_params(("parallel",)),
    )(z, z, z, cos_m, sin_m, gq, wq, gkv, wkv, qg, kg)


def _mla_prep_bwd(z, cos_m, sin_m, gq, wq, gkv, wkv, qg, kg, dq, dk, dv, tm):
    s = z.shape[0]

    def body(mq_ref, mkv_ref, mkr_ref, cos_ref, sin_ref, gq_ref, wq_ref, gkv_ref, wkv_ref, qg_ref, kg_ref,
             dq_ref, dk_ref, dv_ref,
             dmq_ref, dmkv_ref, dmkr_ref, dwq_ref, dwkv_ref, dgq_ref, dgkv_ref, dqg_ref, dkg_ref, dqf, dkv):
        @pl.when(pl.program_id(0) == 0)
        def _():
            for r in (dwq_ref, dwkv_ref, dgq_ref, dgkv_ref, dqg_ref, dkg_ref):
                r[...] = jnp.zeros_like(r)

        qn, rq0, qnb, kvn, rk0, kvnb, qf, kv = _mla_latents(mq_ref, mkv_ref, gq_ref, gkv_ref, wq_ref, wkv_ref)
        kpe = mkr_ref[...]
        cos, sin = cos_ref[...], sin_ref[...]
        qg, kg = qg_ref[...], kg_ref[...]
        dkpe = jnp.zeros_like(kpe)
        inv = 1.0 / MLA_QK

        def norm_bwd(a, b, r, da_n, db_n, g):
            ga, gb = g[:, 0:LANES], g[:, LANES:]
            dg_a = jnp.sum(da_n * a * r, axis=0, keepdims=True)
            dg_b = jnp.sum(db_n * b * r, axis=0, keepdims=True)
            ua, ub = da_n * ga, db_n * gb
            dt = (jnp.sum(ua * a, axis=-1, keepdims=True) + jnp.sum(ub * b, axis=-1, keepdims=True)) * inv
            r3 = r * r * r
            return r * ua - a * (r3 * dt), r * ub - b * (r3 * dt), dg_a, dg_b

        for h, (qa, qb, rq, ka, rk) in enumerate(_mla_heads(qf, kv, kpe, qg, kg, cos, sin)):
            dqa, dqb, dga, dgb = norm_bwd(qa, qb, rq, dq_ref[h, :, 0:LANES], _rope_t(dq_ref[h, :, LANES:], cos, sin), qg)
            dqf[:, LANES * h:LANES * (h + 1)] = dqa
            dqf[:, 512 + LANES * h:512 + LANES * (h + 1)] = dqb
            dqg_ref[:, 0:LANES] += dga
            dqg_ref[:, LANES:] += dgb
            dka, dkb, dga, dgb = norm_bwd(ka, kpe, rk, dk_ref[h, :, 0:LANES], _rope_t(dk_ref[h, :, LANES:], cos, sin), kg)
            dkv[:, 256 * h:256 * h + LANES] = dka
            dkv[:, 256 * h + LANES:256 * (h + 1)] = dv_ref[h]
            dkpe = dkpe + dkb
            dkg_ref[:, 0:LANES] += dga
            dkg_ref[:, LANES:] += dgb
        dmkr_ref[...] = dkpe.astype(BF16)

        def latent_bwd(dfull, w_ref, nb, n, r, g_ref, dw_ref, dg_ref, dlat_ref):
            db = dfull.astype(BF16)
            dn = _dot(db, w_ref[...], NT)
            dw_ref[...] += _dot(nb, db, TN)
            dg_ref[...] += jnp.sum(dn * n, axis=0, keepdims=True)
            u = dn * g_ref[...]
            dlat_ref[...] = (r * (u - n * jnp.mean(u * n, axis=-1, keepdims=True))).astype(BF16)

        latent_bwd(dqf[...], wq_ref, qnb, qn, rq0, gq_ref, dwq_ref, dgq_ref, dmq_ref)
        latent_bwd(dkv[...], wkv_ref, kvnb, kvn, rk0, gkv_ref, dwkv_ref, dgkv_ref, dmkv_ref)

    full = lambda shape: pl.BlockSpec(shape, lambda i: (0,) * len(shape))
    return pl.pallas_call(
        body, name="mla_prep_bwd", grid=(s // tm,),
        in_specs=[pl.BlockSpec((tm, 512), lambda i: (i, M_Q // 4)),
                  pl.BlockSpec((tm, 256), lambda i: (i, M_KV // 2)),
                  pl.BlockSpec((tm, LANES), lambda i: (i, M_KR)),
                  pl.BlockSpec((tm, LANES), lambda i: (i, 0)),
                  pl.BlockSpec((tm, LANES), lambda i: (i, 0)),
                  full((1, 512)), full((512, 1024)), full((1, 256)), full((256, 1024)), full((1, 256)), full((1, 256)),
                  pl.BlockSpec((4, tm, 256), lambda i: (0, i, 0)), pl.BlockSpec((4, tm, 256), lambda i: (0, i, 0)),
                  pl.BlockSpec((4, tm, LANES), lambda i: (0, i, 0))],
        out_specs=[pl.BlockSpec((tm, 512), lambda i: (i, 0)), pl.BlockSpec((tm, 256), lambda i: (i, 0)),
                   pl.BlockSpec((tm, LANES), lambda i: (i, 0)),
                   full((512, 1024)), full((256, 1024)), full((1, 512)), full((1, 256)), full((1, 256)), full((1, 256))],
        out_shape=[jax.ShapeDtypeStruct((s, 512), BF16), jax.ShapeDtypeStruct((s, 256), BF16),
                   jax.ShapeDtypeStruct((s, LANES), BF16),
                   jax.ShapeDtypeStruct((512, 1024), F32), jax.ShapeDtypeStruct((256, 1024), F32),
                   jax.ShapeDtypeStruct((1, 512), F32), jax.ShapeDtypeStruct((1, 256), F32),
                   jax.ShapeDtypeStruct((1, 256), F32), jax.ShapeDtypeStruct((1, 256), F32)],
        scratch_shapes=[pltpu.VMEM((tm, 1024), F32), pltpu.VMEM((tm, 1024), F32)],
        compiler_params=_params(("arbitrary",)),
    )(z, z, z, cos_m, sin_m, gq, wq, gkv, wkv, qg, kg, dq, dk, dv)


def _attn_fwd(q, k, v, z, tq, rider=None):
    s = q.shape[1]
    scale = MLA_QK ** -0.5

    def body(q_ref, k_ref, v_ref, g_ref, o_ref, y_ref, lse_ref):
        sc = _dot(q_ref[...], k_ref[...], NT) * scale
        m = jnp.max(sc, axis=-1, keepdims=True)
        p = jnp.exp(sc - m)
        l = jnp.sum(p, axis=-1, keepdims=True)
        o = _dot(p.astype(BF16), v_ref[...]) / l
        o_ref[...] = o
        y_ref[...] = (_silu(g_ref[...]) * o).astype(BF16)
        lse_ref[...] = m + jnp.log(l)

    return _call(
        body, "attn_fwd", (4, s // tq),
        [pl.BlockSpec((None, tq, 256), lambda h, i: (h, i, 0)),
         pl.BlockSpec((None, s, 256), lambda h, i: (h, 0, 0)),
         pl.BlockSpec((None, s, LANES), lambda h, i: (h, 0, 0)),
         pl.BlockSpec((tq, LANES), lambda h, i: (i, M_G + h))],
        [pl.BlockSpec((tq, LANES), lambda h, i: (i, h)), pl.BlockSpec((tq, LANES), lambda h, i: (i, h)),
         pl.BlockSpec((None, tq, 1), lambda h, i: (h, i, 0))],
        [jax.ShapeDtypeStruct((s, GROUP_W), F32), jax.ShapeDtypeStruct((s, GROUP_W), BF16),
         jax.ShapeDtypeStruct((4, s, 1), F32)],
        [], ("parallel", "parallel"), (q, k, v, z), rider)


def _attn_bwd(q, k, v, z, o, lse, dy, tq, rider=None):
    s = q.shape[1]
    scale = MLA_QK ** -0.5

    def body(q_ref, k_ref, v_ref, g_ref, o_ref, lse_ref, dy_ref, dq_ref, dk_ref, dv_ref, dg_ref):
        @pl.when(pl.program_id(1) == 0)
        def _():
            dk_ref[...] = jnp.zeros_like(dk_ref)
            dv_ref[...] = jnp.zeros_like(dv_ref)

        gate, ov, dyv = g_ref[...], o_ref[...], dy_ref[...]
        do = dyv * _silu(gate)
        dg_ref[...] = (dyv * ov * _silu_grad(gate)).astype(BF16)
        delta = jnp.sum(do * ov, axis=-1, keepdims=True)
        dob = do.astype(BF16)
        qb, kb = q_ref[...], k_ref[...]
        p = jnp.exp(_dot(qb, kb, NT) * scale - lse_ref[...])
        dp = _dot(dob, v_ref[...], NT)
        ds = (p * (dp - delta) * scale).astype(BF16)
        dq_ref[...] = _dot(ds, kb)
        dk_ref[...] += _dot(ds, qb, TN)
        dv_ref[...] += _dot(p.astype(BF16), dob, TN)

    return _call(
        body, "attn_bwd", (4, s // tq),
        [pl.BlockSpec((None, tq, 256), lambda h, i: (h, i, 0)),
         pl.BlockSpec((None, s, 256), lambda h, i: (h, 0, 0)),
         pl.BlockSpec((None, s, LANES), lambda h, i: (h, 0, 0)),
         pl.BlockSpec((tq, LANES), lambda h, i: (i, M_G + h)),
         pl.BlockSpec((tq, LANES), lambda h, i: (i, h)),
         pl.BlockSpec((None, tq, 1), lambda h, i: (h, i, 0)),
         pl.BlockSpec((tq, LANES), lambda h, i: (i, 12 + h))],
        [pl.BlockSpec((None, tq, 256), lambda h, i: (h, i, 0)),
         pl.BlockSpec((None, s, 256), lambda h, i: (h, 0, 0)),
         pl.BlockSpec((None, s, LANES), lambda h, i: (h, 0, 0)),
         pl.BlockSpec((tq, LANES), lambda h, i: (i, h))],
        [jax.ShapeDtypeStruct((4, s, 256), F32), jax.ShapeDtypeStruct((4, s, 256), F32),
         jax.ShapeDtypeStruct((4, s, LANES), F32), jax.ShapeDtypeStruct((s, GROUP_W), BF16)],
        [], ("parallel", "arbitrary"), (q, k, v, z, o, lse, dy), rider)


def _loss_head(x, target, tm):
    s, d = x.shape

    def body(x_ref, t_ref, dx_ref, loss_ref):
        @pl.when(pl.program_id(0) == 0)
        def _():
            loss_ref[...] = jnp.zeros_like(loss_ref)
        err = x_ref[...] - t_ref[...]
        dx_ref[...] = err * (1.0 / d)
        per_tok = jnp.mean(err * err, axis=-1, keepdims=True)
        loss_ref[...] += 0.5 * jnp.sum(per_tok, axis=0, keepdims=True)

    return pl.pallas_call(
        body, name="loss_head", grid=(s // tm,),
        in_specs=[pl.BlockSpec((tm, d), lambda i: (i, 0)), pl.BlockSpec((tm, d), lambda i: (i, 0))],
        out_specs=[pl.BlockSpec((tm, d), lambda i: (i, 0)), pl.BlockSpec((1, LANES), lambda i: (0, 0))],
        out_shape=[jax.ShapeDtypeStruct((s, d), F32), jax.ShapeDtypeStruct((1, LANES), F32)],
        compiler_params=_params(("arbitrary",)),
    )(x, target)


def _norm_bwd(x, g, dh, dres, tm):
    s, d = x.shape

    def body(x_ref, g_ref, dh_ref, dres_ref, dx_ref, dg_ref):
        @pl.when(pl.program_id(0) == 0)
        def _():
            dg_ref[...] = jnp.zeros_like(dg_ref)
        xv, dhv = x_ref[...], dh_ref[...]
        r = lax.rsqrt(jnp.mean(xv * xv, axis=-1, keepdims=True) + EPS)
        n = xv * r
        dg_ref[...] += jnp.sum(dhv * n, axis=0, keepdims=True)
        u = dhv * g_ref[...]
        dx_ref[...] = dres_ref[...] + r * (u - n * jnp.mean(u * n, axis=-1, keepdims=True))

    row = lambda: pl.BlockSpec((tm, d), lambda i: (i, 0))
    return pl.pallas_call(
        body, name="norm_bwd", grid=(s // tm,),
        in_specs=[row(), pl.BlockSpec((1, d), lambda i: (0, 0)), row(), row()],
        out_specs=[row(), pl.BlockSpec((1, d), lambda i: (0, 0))],
        out_shape=[jax.ShapeDtypeStruct((s, d), F32), jax.ShapeDtypeStruct((1, d), F32)],
        compiler_params=_params(("arbitrary",)),
    )(x, g, dh, dres)


def _adam(parts, w, m, v, name, tr):
    r, c = w.shape
    tr = min(tr, r)
    c1 = 1.0 - ADAM_B1 ** ADAM_STEP
    c2 = 1.0 - ADAM_B2 ** ADAM_STEP

    def body(p_ref, w_ref, m_ref, v_ref, g_ref, d_ref, nm_ref, nv_ref):
        g = p_ref[0].astype(F32)
        for i in range(1, N_DEV):
            g = g + p_ref[i].astype(F32)
        nm = ADAM_B1 * m_ref[...] + (1.0 - ADAM_B1) * g
        nv = ADAM_B2 * v_ref[...] + (1.0 - ADAM_B2) * (g * g)
        g_ref[...] = g
        nm_ref[...] = nm
        nv_ref[...] = nv
        d_ref[...] = -ADAM_LR * ((nm / c1) / (jnp.sqrt(nv / c2) + ADAM_EPS) + ADAM_WD * w_ref[...])

    blk = lambda: pl.BlockSpec((tr, c), lambda i: (i, 0))
    return pl.pallas_call(
        body, name=name, grid=(r // tr,),
        in_specs=[pl.BlockSpec((N_DEV, tr, c), lambda i: (0, i, 0)), blk(), blk(), blk()],
        out_specs=[blk(), blk(), blk(), blk()],
        out_shape=[jax.ShapeDtypeStruct((r, c), F32)] * 4,
        compiler_params=_params(("parallel",)),
    )(parts, w, m, v)


def _adam_layers(parts, w, m, v, name, tr):
    nl, r, c = w.shape
    tr = min(tr, r)
    nr = r // tr
    c1 = 1.0 - ADAM_B1 ** ADAM_STEP
    c2 = 1.0 - ADAM_B2 ** ADAM_STEP

    def body(*refs):
        p_refs, (w_ref, m_ref, v_ref, g_ref, d_ref, nm_ref, nv_ref) = refs[:nl], refs[nl:]
        for l in range(nl):
            @pl.when(pl.program_id(0) == l)
            def _(p_ref=p_refs[l]):
                g = p_ref[0].astype(F32)
                for i in range(1, N_DEV):
                    g = g + p_ref[i].astype(F32)
                nm = ADAM_B1 * m_ref[...] + (1.0 - ADAM_B1) * g
                nv = ADAM_B2 * v_ref[...] + (1.0 - ADAM_B2) * (g * g)
                g_ref[...] = g
                nm_ref[...] = nm
                nv_ref[...] = nv
                d_ref[...] = -ADAM_LR * ((nm / c1) / (jnp.sqrt(nv / c2) + ADAM_EPS) + ADAM_WD * w_ref[...])

    def part_spec(l):
        return pl.BlockSpec((N_DEV, tr, c), lambda ll, i: (0, jnp.where(ll == l, i, 0 if l > 0 else nr - 1), 0))

    blk = lambda: pl.BlockSpec((None, tr, c), lambda ll, i: (ll, i, 0))
    return pl.pallas_call(
        body, name=name, grid=(nl, nr),
        in_specs=[part_spec(l) for l in range(nl)] + [blk(), blk(), blk()],
        out_specs=[blk(), blk(), blk(), blk()],
        out_shape=[jax.ShapeDtypeStruct((nl, r, c), F32)] * 4,
        compiler_params=_params(("arbitrary", "arbitrary")),
    )(*parts, w, m, v)


REPLICATED = ("norm_g", "ret_norm_g", "gla_ba_f", "gla_ba_b", "gla_norm_g", "pool_w", "pool_scale",
              "mla_q_norm_g", "mla_kv_norm_g", "mla_qk_norm_q", "mla_qk_norm_k")
SMALL_SHARDED = ("mla_wq_b", "mla_wkv_b", "gla_wa2_f", "gla_wa2_b")
WEIGHTS = ("norm_g", "w_in", "ret_norm_g", "gla_wa2_f", "gla_ba_f", "gla_wa2_b", "gla_ba_b", "gla_norm_g", "pool_w",
           "pool_scale", "mla_q_norm_g", "mla_wq_b", "mla_kv_norm_g", "mla_wkv_b", "mla_qk_norm_q", "mla_qk_norm_k",
           "w_out")


def _pack(arrays, dtype):
    flat = jnp.concatenate([a.reshape(-1) for a in arrays]).astype(dtype)
    return flat.reshape(-1, LANES)


def _unpack(packed, like):
    flat = packed.reshape(-1)
    out, at = [], 0
    for a in like:
        out.append(flat[at:at + a.size].reshape(a.shape))
        at += a.size
    return out


def _columns_by_device(g):
    l, r, n = g.shape
    return g.reshape(l, r, N_DEV, n // N_DEV).transpose(2, 0, 1, 3)


def _gathered_columns(g, l, r, c):
    return g.reshape(N_DEV, l, r, c).transpose(1, 2, 0, 3).reshape(l, r, N_DEV * c)


def _layer_forward(x, wts, w_out_of, tables, tm, tq, ride_inproj=None, ride_attn=None):
    cos_r, sin_r, cos_m, sin_m, tab, _ = tables
    z, h, *carried_in = _inproj(x, wts["norm_g"], wts["w_in"], min(x.shape[0], 2 * tm), rider=ride_inproj)
    o_a, y_a = _ret_fwd(z, cos_r, sin_r, tab, wts["ret_norm_g"])
    o_b, y_b = _gla_fwd(z, wts["wa_f"], wts["wa_b"], wts["gla_ba_f"], wts["gla_ba_b"], wts["gla_norm_g"])
    y_c = _pool_fwd(z, wts["pool_w"], wts["pool_scale"])
    q, k, v = _mla_prep(z, cos_m, sin_m, wts["mla_q_norm_g"], wts["wq"], wts["mla_kv_norm_g"], wts["wkv"],
                        wts["qk_q"], wts["qk_k"], tm)
    o_d, y_d, lse, *carried_attn = _attn_fwd(q, k, v, z, tq, rider=ride_attn)
    y = jnp.concatenate([y_a, y_b, y_c, y_d], axis=1)
    w_out = w_out_of(carried_in)
    x_next = _mm(y, w_out, "nn", "outproj", tm, 2048, 1024, add=x)
    saved = dict(x=x, z=z, h=h, o_a=o_a, o_b=o_b, o_d=o_d, lse=lse, q=q, k=k, v=v, y=y, w_out=w_out)
    return x_next, saved, carried_in, carried_attn


def _layer_backward(dx, sv, wts, tables, tm, tq, ride_attn=None, ride_dh=None):
    cos_r, sin_r, cos_m, sin_m, tab, tab_sw = tables
    z = sv["z"]
    g = {}
    g["w_out"] = _mm(sv["y"], dx, "tn", "d_w_out", 2048, 1024, 512, out_dtype=BF16)
    dy = _mm(dx, sv["w_out"], "nt", "d_y", tm, 2048, 1024)

    do_a, dg_a, g["ret_norm_g"] = _normgate_bwd(sv["o_a"], z, A_G, dy, 0, wts["ret_norm_g"], tm)
    dq_a, dk_a, dv_a = _ret_bwd(z, do_a, cos_r, sin_r, tab, tab_sw)

    do_b, dg_b, g["gla_norm_g"] = _normgate_bwd(sv["o_b"], z, B_G, dy, 1, wts["gla_norm_g"], tm)
    dq_b, dk_b, dv_b, d_ga, d_waf, d_wab, g["gla_ba_f"], g["gla_ba_b"] = _gla_bwd(
        z, do_b, wts["wa_f"], wts["wa_b"], wts["gla_ba_f"], wts["gla_ba_b"])
    g["gla_wa2_f"] = d_waf[0:GLA_RANK]
    g["gla_wa2_b"] = d_wab[GLA_RANK:2 * GLA_RANK]

    du_c, dg_c, g["pool_w"], g["pool_scale"] = _pool_bwd(z, dy, wts["pool_w"], wts["pool_scale"])

    d_q, d_k, d_v, dg_d, *carried_attn = _attn_bwd(sv["q"], sv["k"], sv["v"], z, sv["o_d"], sv["lse"], dy, tq,
                                                   rider=ride_attn(g) if ride_attn else None)
    (d_mq, d_mkv, d_mkr, d_wq, g["mla_wkv_b"], g["mla_q_norm_g"], g["mla_kv_norm_g"], d_qg, d_kg) = _mla_prep_bwd(
        z, cos_m, sin_m, wts["mla_q_norm_g"], wts["wq"], wts["mla_kv_norm_g"], wts["wkv"], wts["qk_q"], wts["qk_k"],
        d_q, d_k, d_v, tm)
    g["mla_wq_b"] = _unpad_wq(d_wq)
    g["mla_qk_norm_q"] = d_qg[:, _QK_INV]
    g["mla_qk_norm_k"] = d_kg[:, _QK_INV]

    dz = jnp.concatenate([dq_a, dk_a, dv_a, dg_a, dq_b, dk_b, dv_b, dg_b, d_mq, du_c, dg_c, dg_d, d_mkv,
                          d_ga.astype(BF16), d_mkr], axis=1)
    g["w_in"] = _split_w_in(_mm(sv["h"], dz, "tn", "d_w_in", 2048, 1024, 512, out_dtype=BF16))
    dh = _mm(dz, wts["w_in"], "nt", "d_h", 2 * tm, 2048, 1024, rider=ride_dh(g) if ride_dh else None)
    carried_dh = []
    if ride_dh:
        dh, *carried_dh = dh
    dx_in, g["norm_g"] = _norm_bwd(sv["x"], wts["norm_g"], dh, dx, tm)
    return dx_in, g, carried_attn, carried_dh


def kernel(x, norm_g, w_in, ret_norm_g, gla_wa2_f, gla_ba_f, gla_wa2_b, gla_ba_b, gla_norm_g, pool_w, pool_scale, mla_q_norm_g, mla_wq_b, mla_kv_norm_g, mla_wkv_b, mla_qk_norm_q, mla_qk_norm_k, w_out, loss_target, m_norm_g, m_w_in, m_ret_norm_g, m_gla_wa2_f, m_gla_ba_f, m_gla_wa2_b, m_gla_ba_b, m_gla_norm_g, m_pool_w, m_pool_scale, m_mla_q_norm_g, m_mla_wq_b, m_mla_kv_norm_g, m_mla_wkv_b, m_mla_qk_norm_q, m_mla_qk_norm_k, m_w_out, v_norm_g, v_w_in, v_ret_norm_g, v_gla_wa2_f, v_gla_ba_f, v_gla_wa2_b, v_gla_ba_b, v_gla_norm_g, v_pool_w, v_pool_scale, v_mla_q_norm_g, v_mla_wq_b, v_mla_kv_norm_g, v_mla_wkv_b, v_mla_qk_norm_q, v_mla_qk_norm_k, v_w_out):
    w = dict(norm_g=norm_g, w_in=w_in, ret_norm_g=ret_norm_g, gla_wa2_f=gla_wa2_f, gla_ba_f=gla_ba_f,
             gla_wa2_b=gla_wa2_b, gla_ba_b=gla_ba_b, gla_norm_g=gla_norm_g, pool_w=pool_w, pool_scale=pool_scale,
             mla_q_norm_g=mla_q_norm_g, mla_wq_b=mla_wq_b, mla_kv_norm_g=mla_kv_norm_g, mla_wkv_b=mla_wkv_b,
             mla_qk_norm_q=mla_qk_norm_q, mla_qk_norm_k=mla_qk_norm_k, w_out=w_out)
    m = dict(norm_g=m_norm_g, w_in=m_w_in, ret_norm_g=m_ret_norm_g, gla_wa2_f=m_gla_wa2_f, gla_ba_f=m_gla_ba_f,
             gla_wa2_b=m_gla_wa2_b, gla_ba_b=m_gla_ba_b, gla_norm_g=m_gla_norm_g, pool_w=m_pool_w,
             pool_scale=m_pool_scale, mla_q_norm_g=m_mla_q_norm_g, mla_wq_b=m_mla_wq_b, mla_kv_norm_g=m_mla_kv_norm_g,
             mla_wkv_b=m_mla_wkv_b, mla_qk_norm_q=m_mla_qk_norm_q, mla_qk_norm_k=m_mla_qk_norm_k, w_out=m_w_out)
    v = dict(norm_g=v_norm_g, w_in=v_w_in, ret_norm_g=v_ret_norm_g, gla_wa2_f=v_gla_wa2_f, gla_ba_f=v_gla_ba_f,
             gla_wa2_b=v_gla_wa2_b, gla_ba_b=v_gla_ba_b, gla_norm_g=v_gla_norm_g, pool_w=v_pool_w,
             pool_scale=v_pool_scale, mla_q_norm_g=v_mla_q_norm_g, mla_wq_b=v_mla_wq_b, mla_kv_norm_g=v_mla_kv_norm_g,
             mla_wkv_b=v_mla_wkv_b, mla_qk_norm_q=v_mla_qk_norm_q, mla_qk_norm_k=v_mla_qk_norm_k, w_out=v_w_out)
    xs, target = x[0], loss_target[0]
    s = xs.shape[0]
    tm, tq = min(s, 512), min(s, 256)
    c_in = w_in.shape[2]

    w_in_b = w_in.astype(BF16)
    w_out_b = w_out.astype(BF16).reshape(-1, D_MODEL)
    w_in_g0 = _exchange("gather", w_in_b[0], "gather_w_in")
    small_g = _exchange("gather", _pack([w[n] for n in SMALL_SHARDED], BF16), "gather_small")
    sizes = [w[n].size for n in SMALL_SHARDED]
    offs = np.cumsum([0] + sizes)
    small_flat = small_g.reshape(N_DEV, -1)
    small_full = {n: _gathered_columns(small_flat[:, offs[i]:offs[i + 1]], *w[n].shape)
                  for i, n in enumerate(SMALL_SHARDED)}
    tables = _rope_tables(s) + _ret_tables()

    def layer_weights(l, w_in_g):
        wa_f = jnp.zeros((LANES, 2 * LANES), BF16).at[0:GLA_RANK].set(small_full["gla_wa2_f"][l])
        wa_b = jnp.zeros((LANES, 2 * LANES), BF16).at[GLA_RANK:2 * GLA_RANK].set(small_full["gla_wa2_b"][l])
        return dict(
            norm_g=norm_g[l][None], w_in=_assemble_w_in(w_in_g), ret_norm_g=ret_norm_g[l][None],
            wa_f=wa_f, wa_b=wa_b, gla_ba_f=gla_ba_f[l][None], gla_ba_b=gla_ba_b[l][None],
            gla_norm_g=gla_norm_g[l][None], pool_w=pool_w[l], pool_scale=pool_scale[l][None],
            mla_q_norm_g=mla_q_norm_g[l][None], wq=_pad_wq(small_full["mla_wq_b"][l]),
            mla_kv_norm_g=mla_kv_norm_g[l][None], wkv=small_full["mla_wkv_b"][l],
            qk_q=_pad_qk_gain(mla_qk_norm_q[l]), qk_k=_pad_qk_gain(mla_qk_norm_k[l]))

    def w_out_layer(w_out_g, l):
        return w_out_g.reshape(N_DEV, DEPTH, -1, D_MODEL)[:, l].reshape(-1, D_MODEL)

    by_owner = lambda g_w_out: g_w_out.reshape(N_DEV, -1, D_MODEL)

    layers = [layer_weights(0, w_in_g0), None]
    x1, sv0, (w_out_g,), (w_in_g1,) = _layer_forward(
        xs, layers[0], lambda got: w_out_layer(got[0], 0), tables, tm, tq,
        ride_inproj=("gather", w_out_b), ride_attn=("gather", w_in_b[1]))
    layers[1] = layer_weights(1, w_in_g1)
    x2, sv1, _, _ = _layer_forward(x1, layers[1], lambda got: w_out_layer(w_out_g, 1), tables, tm, tq)
    dx, loss_row = _loss_head(x2, target, tm)
    loss = lax.psum(loss_row[0, 0], ("x", "y", "c"))

    dx, g1, (out_parts1,), _ = _layer_backward(
        dx, sv1, layers[1], tables, tm, tq, ride_attn=lambda g: ("scatter", by_owner(g["w_out"])))
    dx, g0, (in_parts1,), (out_parts0,) = _layer_backward(
        dx, sv0, layers[0], tables, tm, tq, ride_attn=lambda g: ("scatter", g1["w_in"]),
        ride_dh=lambda g: ("scatter", by_owner(g["w_out"])))
    in_parts0 = _exchange("scatter", g0["w_in"], "exchange_w_in")
    grads = (g0, g1)
    full = {n: jnp.stack([grads[l][n].reshape(w[n].shape[1:]) if n in REPLICATED else grads[l][n]
                          for l in range(DEPTH)]) for n in SMALL_SHARDED + REPLICATED}
    small_c = jnp.concatenate([_columns_by_device(full[n]).reshape(N_DEV, -1) for n in SMALL_SHARDED], axis=1)
    small_parts = _exchange("scatter", small_c.reshape(N_DEV, -1, LANES), "exchange_small")
    rep_parts = _exchange("gather", _pack([full[n] for n in REPLICATED], F32), "gather_replicated")

    out = {}
    out["w_in"] = _adam_layers((in_parts0, in_parts1), w_in, m_w_in, v_w_in, "adam_w_in", 256)
    out["w_out"] = _adam_layers((out_parts0, out_parts1), w_out, m_w_out, v_w_out, "adam_w_out", 128)
    for names, parts, label in ((SMALL_SHARDED, small_parts, "adam_small"), (REPLICATED, rep_parts, "adam_replicated")):
        res = _adam(parts, _pack([w[n] for n in names], F32), _pack([m[n] for n in names], F32),
                    _pack([v[n] for n in names], F32), label, 2048)
        for n, *vals in zip(names, *[_unpack(a, [w[n] for n in names]) for a in res]):
            out[n] = vals

    return (loss, dx[None], *[out[n][0] for n in WEIGHTS], *[out[n][1] for n in WEIGHTS],
            *[out[n][2] for n in WEIGHTS], *[out[n][3] for n in WEIGHTS])
```

```python
import functools
import math

import numpy as np
import jax
import jax.numpy as jnp
from jax import lax
from jax.experimental import pallas as pl
from jax.experimental.pallas import tpu as pltpu

F32 = jnp.float32
BF16 = jnp.bfloat16

N_DEV = 8
D_MODEL = 2048
DEPTH = 2
GROUP_W = 512
EPS = 1e-6
ROPE_THETA = 10000.0
LANES = 128

RET_HD = 128
RET_CHUNK = 256
RET_UNROLL = 2
GLA_CHUNK = 64
GLA_UNROLL = 4
GLA_DK = 64
GLA_TAU = 16.0
GLA_RANK = 16
POOL_WINDOWS = (2, 4, 8, 16)
MLA_QK = 192
MLA_ROPE = 64
ATTN_SCALE = MLA_QK ** -0.5
ATTN_Q_SCALE = ATTN_SCALE * math.log2(math.e)
IN_COLS = 5984

ADAM_LR = 0.001
ADAM_B1 = 0.9
ADAM_B2 = 0.999
ADAM_EPS = 1e-08
ADAM_WD = 0.01
ADAM_STEP = 10

A_Q, A_K, A_V, A_G = 0, 4, 8, 12
B_Q, B_K, B_V, B_G = 16, 18, 20, 24
M_Q, C_V, C_G, M_G = 28, 32, 36, 40
M_KV, GA, M_KR = 44, 46, 47
ZP_COLS = 48 * LANES

VMEM_LIMIT = 56 * 1024 * 1024


def _params(sem, vmem=VMEM_LIMIT):
    return pltpu.CompilerParams(dimension_semantics=sem, vmem_limit_bytes=vmem)


def _sigmoid(x):
    return 1.0 / (1.0 + jnp.exp(-x))


def _silu(x):
    return x * _sigmoid(x)


def _silu_grad(x):
    s = _sigmoid(x)
    return s * (1.0 + x * (1.0 - s))


def _dot(a, b, dims=(((1,), (0,)), ((), ()))):
    return lax.dot_general(a, b, dims, preferred_element_type=F32)


NT = (((1,), (1,)), ((), ()))
TN = (((0,), (0,)), ((), ()))


def _dot_exact(a, b):
    return lax.dot_general(a, b, (((1,), (0,)), ((), ())), precision=lax.Precision.HIGHEST,
                           preferred_element_type=F32)


def _chunk_loop(n, body, init, unroll):
    unroll = math.gcd(n, unroll)

    def trip(t, carry):
        for u in range(unroll):
            carry = body(t * unroll + u, carry)
        return carry

    return lax.fori_loop(0, n // unroll, trip, init)


def _roll_lanes_half(x):
    return pltpu.roll(x, 64, 1)


def _pad_w_in(w):
    z = lambda n: jnp.zeros((w.shape[0], n), w.dtype)
    return jnp.concatenate([
        w[:, 0:3584],
        w[:, 4640:5152], w[:, 3616:4640], w[:, 5472:5984], w[:, 5152:5408],
        w[:, 3584:3616], z(96),
        w[:, 5408:5440], z(32), w[:, 5440:5472], z(32)], axis=1)


def _unpad_w_in(wp):
    b = lambda blk, n: wp[:, blk * LANES:blk * LANES + n]
    return jnp.concatenate([wp[:, 0:3584], b(GA, 32), b(C_V, 1024), b(M_Q, 512), b(M_KV, 256),
                            b(M_KR, 32), wp[:, M_KR * LANES + 64:M_KR * LANES + 96], b(M_G, 512)], axis=1)


def _wq_perm():
    idx = np.zeros((1024,), np.int32)
    ok = np.zeros((1024,), bool)
    for h in range(4):
        idx[128 * h:128 * h + 128] = 192 * h + np.arange(128)
        ok[128 * h:128 * h + 128] = True
        base = 512 + 128 * h
        idx[base:base + 32] = 192 * h + 128 + np.arange(32)
        ok[base:base + 32] = True
        idx[base + 64:base + 96] = 192 * h + 160 + np.arange(32)
        ok[base + 64:base + 96] = True
    inv = np.zeros((768,), np.int32)
    inv[idx[ok]] = np.nonzero(ok)[0]
    return idx, ok, inv


_WQ_IDX, _WQ_OK, _WQ_INV = _wq_perm()


def _pad_wq(wq):
    return jnp.where(jnp.asarray(_WQ_OK)[None, :], wq[:, _WQ_IDX], 0).astype(wq.dtype)


def _unpad_wq(wqp):
    return wqp[:, _WQ_INV]


def _qk_idx():
    idx = np.zeros((256,), np.int32)
    ok = np.zeros((256,), bool)
    idx[0:128] = np.arange(128)
    ok[0:128] = True
    idx[128:160] = 128 + np.arange(32)
    ok[128:160] = True
    idx[192:224] = 160 + np.arange(32)
    ok[192:224] = True
    inv = np.zeros((192,), np.int32)
    inv[idx[ok]] = np.nonzero(ok)[0]
    return idx, ok, inv


_QK_IDX, _QK_OK, _QK_INV = _qk_idx()


def _pad_qk_gain(g):
    return jnp.where(jnp.asarray(_QK_OK), g[_QK_IDX], 0.0).reshape(1, 256)


def _rope_tables(s):
    def tabs(dim):
        inv = 1.0 / (ROPE_THETA ** (jnp.arange(0, dim, 2, dtype=F32) / dim))
        ang = jnp.arange(s, dtype=F32)[:, None] * inv[None, :]
        return jnp.cos(ang), jnp.sin(ang)
    cr, sr = tabs(RET_HD)
    cos_r = jnp.concatenate([cr, cr], axis=1)
    sin_r = jnp.concatenate([-sr, sr], axis=1)
    cm, sm = tabs(MLA_ROPE)
    zz = jnp.zeros_like(cm)
    cos_m = jnp.concatenate([cm, zz, cm, zz], axis=1)
    sin_m = jnp.concatenate([-sm, zz, sm, zz], axis=1)
    return cos_r, sin_r, cos_m, sin_m


def _rope(x, cos, sin):
    return x * cos + _roll_lanes_half(x) * sin


def _rope_t(x, cos, sin):
    return x * cos + _roll_lanes_half(x * sin)


def _ret_tables():
    c = RET_CHUNK
    gamma_f = 1.0 - 2.0 ** (-5.0 - jnp.arange(4, dtype=F32))
    gamma_b = gamma_f[::-1]
    idx = jnp.arange(c, dtype=F32)
    diff = idx[:, None] - idx[None, :]

    def build(g1, g2):
        l1 = jnp.log(g1)[:, None, None]
        l2 = jnp.log(g2)[:, None, None]
        d1 = jnp.where(diff >= 0, jnp.exp(jnp.maximum(diff, 0.0)[None] * l1), 0.0)
        d2 = jnp.where(diff <= 0, jnp.exp(jnp.maximum(-diff, 0.0)[None] * l2), 0.0)
        ones = jnp.ones((1, c, LANES), F32)
        col = idx[None, :, None]
        qdf = jnp.exp((col + 1.0) * l1) * ones
        kdf = jnp.exp((c - 1.0 - col) * l1) * ones
        qdb = jnp.exp((c - col) * l2) * ones
        kdb = jnp.exp(col * l2) * ones
        cd1 = jnp.exp(c * l1) * ones
        cd2 = jnp.exp(c * l2) * ones
        return jnp.concatenate([d1 + d2, qdf, kdf, qdb, kdb, cd1, cd2], axis=2)

    return build(gamma_f, gamma_b), build(gamma_b, gamma_f)


MESH = pl.DeviceIdType.MESH
ANY = pl.BlockSpec(memory_space=pl.ANY)
_RELATIONS = ((0, 0, 1), (1, 0, 0), (0, 1, 0), (1, 1, 0), (1, 0, 1), (0, 1, 1), (1, 1, 1))


def _position():
    return lax.axis_index("x"), lax.axis_index("y"), lax.axis_index("c")


def _gather_copies(x_ref, out_ref, send_sems, recv_sems, local_sem, starting):
    x, y, cc = _position()
    me, sibling = (x, y, cc), (x, y, 1 - cc)
    chips = [(1 - x, y), (x, 1 - y), (1 - x, 1 - y)]

    def slab(px, py, pc):
        return out_ref.at[4 * px + 2 * py + pc]

    def copy(k, block, to, src=None):
        return pltpu.make_async_remote_copy(
            src_ref=slab(*block) if src is None else src, dst_ref=slab(*block),
            send_sem=send_sems.at[k], recv_sem=recv_sems.at[k], device_id=to, device_id_type=MESH)

    mine = pltpu.make_async_copy(x_ref, slab(*me), local_sem)
    first = [copy(0, me, sibling, src=x_ref)] + [copy(1 + j, me, (*chip, cc), src=x_ref) for j, chip in enumerate(chips)]
    if starting:
        return mine, first
    passed = [copy(4 + j, (*chip, cc), sibling) for j, chip in enumerate(chips)]
    arrivals = [copy(1 + j, (*chip, cc), me) for j, chip in enumerate(chips)]
    late = [copy(0, sibling, me)] + [copy(4 + j, (*chip, 1 - cc), me) for j, chip in enumerate(chips)]
    return mine, first, passed, arrivals, late


def _gather_start(*refs):
    mine, first = _gather_copies(*refs, starting=True)
    mine.start()
    for cp in first:
        cp.start()


def _gather_finish(*refs):
    mine, first, passed, arrivals, late = _gather_copies(*refs, starting=False)
    for arrived, onward in zip(arrivals, passed):
        arrived.wait_recv()
        onward.start()
    for cp in late:
        cp.wait_recv()
    for cp in first + passed:
        cp.wait_send()
    mine.wait()


def _scatter_copies(c_ref, out_ref, send_sems, recv_sems, local_sem):
    x, y, cc = _position()
    me = 4 * x + 2 * y + cc
    mine = pltpu.make_async_copy(c_ref.at[me], out_ref.at[me], local_sem)
    copies = []
    for k, (fx, fy, fc) in enumerate(_RELATIONS):
        px = 1 - x if fx else x
        py = 1 - y if fy else y
        pc = 1 - cc if fc else cc
        copies.append(pltpu.make_async_remote_copy(
            src_ref=c_ref.at[4 * px + 2 * py + pc], dst_ref=out_ref.at[me],
            send_sem=send_sems.at[k], recv_sem=recv_sems.at[k], device_id=(px, py, pc), device_id_type=MESH))
    return mine, copies


def _scatter_start(*refs):
    mine, copies = _scatter_copies(*refs)
    mine.start()
    for cp in copies:
        cp.start()


def _scatter_finish(*refs):
    mine, copies = _scatter_copies(*refs)
    for cp in copies:
        cp.wait()
    mine.wait()


_EXCHANGES = {"gather": (_gather_start, _gather_finish), "scatter": (_scatter_start, _scatter_finish)}


def _exchange_scratch():
    return [pltpu.SemaphoreType.DMA((7,)), pltpu.SemaphoreType.DMA((7,)), pltpu.SemaphoreType.DMA]


def _exchange_out(kind, src):
    return jax.ShapeDtypeStruct(((N_DEV,) + src.shape) if kind == "gather" else src.shape, src.dtype)


def _exchange(jobs, name):
    n = len(jobs)

    def body(*refs):
        srcs, outs, sems = refs[:n], refs[n:2 * n], refs[2 * n:]
        for half in (0, 1):
            for i, (kind, _) in enumerate(jobs):
                _EXCHANGES[kind][half](srcs[i], outs[i], *sems[3 * i:3 * i + 3])

    return pl.pallas_call(
        body, name=name, out_shape=[_exchange_out(kind, src) for kind, src in jobs],
        in_specs=[ANY] * n, out_specs=[ANY] * n,
        scratch_shapes=[sem for _ in jobs for sem in _exchange_scratch()])(*[src for _, src in jobs])


def _call(body, name, grid, in_specs, out_specs, out_shape, scratch, sem, args, rider=None):
    if rider is None:
        return pl.pallas_call(body, name=name, grid=grid, in_specs=in_specs, out_specs=out_specs, out_shape=out_shape,
                              scratch_shapes=scratch, compiler_params=_params(sem))(*args)
    kind, src = rider
    start, finish = _EXCHANGES[kind]
    ni, no, ns = len(in_specs), len(out_specs), len(scratch)

    def carried(*refs):
        ins, rsrc = refs[:ni], refs[ni]
        outs, rout = refs[ni + 1:ni + 1 + no], refs[ni + 1 + no]
        scr, sems = refs[ni + 2 + no:ni + 2 + no + ns], refs[ni + 2 + no + ns:]
        ids = [pl.program_id(a) for a in range(len(grid))]
        is_first = functools.reduce(jnp.logical_and, [i == 0 for i in ids])
        is_last = functools.reduce(jnp.logical_and, [i == g - 1 for i, g in zip(ids, grid)])

        @pl.when(is_first)
        def _():
            start(rsrc, rout, *sems)

        body(*ins, *outs, *scr)

        @pl.when(is_last)
        def _():
            finish(rsrc, rout, *sems)

    return pl.pallas_call(
        carried, name=name, grid=grid, in_specs=list(in_specs) + [ANY], out_specs=list(out_specs) + [ANY],
        out_shape=list(out_shape) + [_exchange_out(kind, src)], scratch_shapes=list(scratch) + _exchange_scratch(),
        compiler_params=_params(("arbitrary",) * len(grid)))(*args, src)


def _inproj(x, g, wp, tm, tn=512, rider=None):
    s, d = x.shape
    n = wp.shape[1]

    def body(x_ref, g_ref, w_ref, z_ref, h_ref, hs):
        @pl.when(pl.program_id(1) == 0)
        def _():
            xv = x_ref[...]
            r = lax.rsqrt(jnp.mean(xv * xv, axis=-1, keepdims=True) + EPS)
            hv = (xv * r * g_ref[...]).astype(BF16)
            hs[...] = hv
            h_ref[...] = hv
        z_ref[...] = _dot(hs[...], w_ref[...])

    return _call(
        body, "inproj", (s // tm, n // tn),
        [pl.BlockSpec((tm, d), lambda i, j: (i, 0)),
         pl.BlockSpec((1, d), lambda i, j: (0, 0)),
         pl.BlockSpec((d, tn), lambda i, j: (0, j))],
        [pl.BlockSpec((tm, tn), lambda i, j: (i, j)), pl.BlockSpec((tm, d), lambda i, j: (i, 0))],
        [jax.ShapeDtypeStruct((s, n), F32), jax.ShapeDtypeStruct((s, d), BF16)],
        [pltpu.VMEM((tm, d), BF16)], ("parallel", "arbitrary"), (x, g, wp), rider)


def _relayout_plan():
    runs = ((0, 3584, 0), (3584, 3616, GA * LANES), (3616, 4640, C_V * LANES), (4640, 5152, M_Q * LANES),
            (5152, 5408, M_KV * LANES), (5408, 5440, M_KR * LANES), (5440, 5472, M_KR * LANES + 64),
            (5472, 5984, M_G * LANES))
    shard = IN_COLS // N_DEV
    plan = []
    for d in range(N_DEV):
        lo, hi = shard * d, shard * (d + 1)
        for a, b, p in runs:
            s, e = max(a, lo), min(b, hi)
            if s < e:
                plan.append((d, s - lo, p + (s - a), e - s))
    return plan


def _assemble_w_in(g, tr=256):
    _, r, c = g.shape

    def body(g_ref, o_ref):
        o_ref[...] = jnp.zeros_like(o_ref)
        for d, at, to, w in _relayout_plan():
            o_ref[:, to:to + w] = g_ref[d, :, at:at + w]

    return pl.pallas_call(
        body, name="assemble_w_in", grid=(r // tr,),
        in_specs=[pl.BlockSpec((N_DEV, tr, c), lambda i: (0, i, 0))],
        out_specs=pl.BlockSpec((tr, ZP_COLS), lambda i: (i, 0)),
        out_shape=jax.ShapeDtypeStruct((r, ZP_COLS), g.dtype),
        compiler_params=_params(("parallel",)),
    )(g)


def _split_w_in(wp, tr=256):
    r = wp.shape[0]
    c = IN_COLS // N_DEV

    def body(w_ref, o_ref):
        for d, at, to, w in _relayout_plan():
            o_ref[d, :, at:at + w] = w_ref[:, to:to + w]

    return pl.pallas_call(
        body, name="split_w_in", grid=(r // tr,),
        in_specs=[pl.BlockSpec((tr, ZP_COLS), lambda i: (i, 0))],
        out_specs=pl.BlockSpec((N_DEV, tr, c), lambda i: (0, i, 0)),
        out_shape=jax.ShapeDtypeStruct((N_DEV, r, c), wp.dtype),
        compiler_params=_params(("parallel",)),
    )(wp)


def _mm(a, b, mode, name, tm, tn, tk, add=None, out_dtype=F32, rider=None):
    if mode == "tn":
        k, m = a.shape
    else:
        m, k = a.shape
    n = b.shape[0] if mode == "nt" else b.shape[1]
    tm, tn, tk = min(tm, m), min(tn, n), min(tk, k)
    nk = k // tk
    dims = {"nn": (((1,), (0,)), ((), ())), "nt": NT, "tn": TN}[mode]

    def body(*refs):
        if add is None:
            a_ref, b_ref, o_ref, acc = refs
        else:
            a_ref, b_ref, add_ref, o_ref, acc = refs
        kk = pl.program_id(2)

        @pl.when(kk == 0)
        def _():
            acc[...] = jnp.zeros_like(acc)

        acc[...] += _dot(a_ref[...].astype(BF16), b_ref[...].astype(BF16), dims)

        @pl.when(kk == nk - 1)
        def _():
            r = acc[...]
            if add is not None:
                r = r + add_ref[...]
            o_ref[...] = r.astype(out_dtype)

    a_spec = (pl.BlockSpec((tk, tm), lambda i, j, kk: (kk, i)) if mode == "tn"
              else pl.BlockSpec((tm, tk), lambda i, j, kk: (i, kk)))
    b_spec = (pl.BlockSpec((tn, tk), lambda i, j, kk: (j, kk)) if mode == "nt"
              else pl.BlockSpec((tk, tn), lambda i, j, kk: (kk, j)))
    in_specs = [a_spec, b_spec]
    args = [a, b]
    if add is not None:
        in_specs.append(pl.BlockSpec((tm, tn), lambda i, j, kk: (i, j)))
        args.append(add)
    res = _call(body, name, (m // tm, n // tn, nk), in_specs,
                [pl.BlockSpec((tm, tn), lambda i, j, kk: (i, j))], [jax.ShapeDtypeStruct((m, n), out_dtype)],
                [pltpu.VMEM((tm, tn), F32)], ("parallel", "parallel", "arbitrary"), args, rider)
    return res[0] if rider is None else res


def _ret_core(q_ref, k_ref, v_ref, tab_ref, out_ref, st_ref, nchunk):
    c = RET_CHUNK

    def rows(n):
        return pl.ds(pl.multiple_of(n * c, c), c)

    zero = jnp.zeros((LANES, LANES), F32)

    def plane(i, n=c):
        return tab_ref[0:n, c + LANES * i:c + LANES * (i + 1)]

    def fwd(n, st):
        r = rows(n)
        q, k, vb = q_ref[r, :], k_ref[r, :], v_ref[r, :].astype(BF16)
        sc = _dot(q.astype(BF16), k.astype(BF16), NT) * tab_ref[:, 0:c]
        o = _dot(sc.astype(BF16), vb)
        o = o + _dot((q * plane(0)).astype(BF16), st.astype(BF16))
        out_ref[r, :] = o
        return st * plane(4, LANES) + _dot((k * plane(1)).astype(BF16), vb, TN)

    _chunk_loop(nchunk, fwd, zero, RET_UNROLL)

    def bwd(i, st):
        r = rows(nchunk - 1 - i)
        q, k, vb = q_ref[r, :], k_ref[r, :], v_ref[r, :].astype(BF16)
        out_ref[r, :] += _dot((q * plane(2)).astype(BF16), st.astype(BF16))
        return st * plane(5, LANES) + _dot((k * plane(3)).astype(BF16), vb, TN)

    _chunk_loop(nchunk, bwd, zero, RET_UNROLL)


def _ret_fwd(z, cos_r, sin_r, tab, norm_g):
    s = z.shape[0]
    nchunk = s // RET_CHUNK
    scale = RET_HD ** -0.5
    col = lambda base: pl.BlockSpec((s, LANES), lambda h: (0, base + h), pipeline_mode=pl.Buffered(1))

    def body(q_ref, k_ref, v_ref, g_ref, cos_ref, sin_ref, tab_ref, ng_ref, o_ref, y_ref, qh, kh, st):
        qh[...] = _rope(q_ref[...], cos_ref[...], sin_ref[...])
        kh[...] = _rope(k_ref[...], cos_ref[...], sin_ref[...]) * scale
        _ret_core(qh, kh, v_ref, tab_ref, o_ref, st, nchunk)
        o = o_ref[...]
        r = lax.rsqrt(jnp.mean(o * o, axis=-1, keepdims=True) + EPS)
        y_ref[...] = (_silu(g_ref[...]) * (o * r * ng_ref[...])).astype(BF16)

    return pl.pallas_call(
        body, name="ret_fwd", grid=(4,),
        in_specs=[col(A_Q), col(A_K), col(A_V), col(A_G),
                  pl.BlockSpec((s, LANES), lambda h: (0, 0), pipeline_mode=pl.Buffered(1)),
                  pl.BlockSpec((s, LANES), lambda h: (0, 0), pipeline_mode=pl.Buffered(1)),
                  pl.BlockSpec((None, RET_CHUNK, RET_CHUNK + 6 * LANES), lambda h: (h, 0, 0)),
                  pl.BlockSpec((1, LANES), lambda h: (0, h))],
        out_specs=[pl.BlockSpec((s, LANES), lambda h: (0, h)), pl.BlockSpec((s, LANES), lambda h: (0, h))],
        out_shape=[jax.ShapeDtypeStruct((s, GROUP_W), F32), jax.ShapeDtypeStruct((s, GROUP_W), BF16)],
        scratch_shapes=[pltpu.VMEM((s, LANES), F32), pltpu.VMEM((s, LANES), F32), pltpu.VMEM((LANES, LANES), F32)],
        compiler_params=_params(("arbitrary",)),
    )(z, z, z, z, cos_r, sin_r, tab, norm_g)


def _ret_bwd(z, d_o, cos_r, sin_r, tab, tab_sw, rider=None):
    s = z.shape[0]
    nchunk = s // RET_CHUNK
    scale = RET_HD ** -0.5
    col = lambda base: pl.BlockSpec((s, LANES), lambda h: (0, base + h), pipeline_mode=pl.Buffered(1))
    whole = lambda: pl.BlockSpec((s, LANES), lambda h: (0, 0), pipeline_mode=pl.Buffered(1))
    tabspec = lambda: pl.BlockSpec((None, RET_CHUNK, RET_CHUNK + 6 * LANES), lambda h: (h, 0, 0))
    outspec = lambda: pl.BlockSpec((s, LANES), lambda h: (0, h))

    def body(q_ref, k_ref, v_ref, do_ref, cos_ref, sin_ref, tab_ref, tsw_ref, dq_ref, dk_ref, dv_ref,
             qh, kh, tmp, st):
        cos, sin = cos_ref[...], sin_ref[...]
        qh[...] = _rope(q_ref[...], cos, sin)
        kh[...] = _rope(k_ref[...], cos, sin) * scale
        _ret_core(kh, qh, do_ref, tsw_ref, tmp, st, nchunk)
        dv_ref[...] = tmp[...].astype(BF16)
        _ret_core(do_ref, v_ref, kh, tab_ref, tmp, st, nchunk)
        dq_ref[...] = _rope_t(tmp[...], cos, sin).astype(BF16)
        _ret_core(v_ref, do_ref, qh, tsw_ref, tmp, st, nchunk)
        dk_ref[...] = _rope_t(tmp[...] * scale, cos, sin).astype(BF16)

    return _call(
        body, "ret_bwd", (4,),
        [col(A_Q), col(A_K), col(A_V),
         pl.BlockSpec((s, LANES), lambda h: (0, h), pipeline_mode=pl.Buffered(1)),
         whole(), whole(), tabspec(), tabspec()],
        [outspec(), outspec(), outspec()],
        [jax.ShapeDtypeStruct((s, GROUP_W), BF16)] * 3,
        [pltpu.VMEM((s, LANES), F32), pltpu.VMEM((s, LANES), F32), pltpu.VMEM((s, LANES), F32),
         pltpu.VMEM((LANES, LANES), F32)],
        ("arbitrary",), (z, z, z, d_o, cos_r, sin_r, tab, tab_sw), rider)


def _normgate_bwd(o, z, gate_blk, dy, dy_blk, norm_g, tm):
    s = o.shape[0]

    def body(o_ref, g_ref, dy_ref, ng_ref, do_ref, dg_ref, dng_ref):
        @pl.when(pl.program_id(0) == 0)
        def _():
            dng_ref[...] = jnp.zeros_like(dng_ref)

        for h in range(4):
            sl = slice(LANES * h, LANES * (h + 1))
            ov, gv, dyv, ng = o_ref[:, sl], g_ref[:, sl], dy_ref[:, sl], ng_ref[:, sl]
            r = lax.rsqrt(jnp.mean(ov * ov, axis=-1, keepdims=True) + EPS)
            on = ov * r
            dn = dyv * _silu(gv)
            u = dn * ng
            do_ref[:, sl] = r * (u - on * jnp.mean(u * on, axis=-1, keepdims=True))
            dg_ref[:, sl] = (dyv * (on * ng) * _silu_grad(gv)).astype(BF16)
            dng_ref[:, sl] += jnp.sum(dn * on, axis=0, keepdims=True)

    return pl.pallas_call(
        body, name="normgate_bwd", grid=(s // tm,),
        in_specs=[pl.BlockSpec((tm, GROUP_W), lambda i: (i, 0)),
                  pl.BlockSpec((tm, GROUP_W), lambda i: (i, gate_blk // 4)),
                  pl.BlockSpec((tm, GROUP_W), lambda i: (i, dy_blk)),
                  pl.BlockSpec((1, GROUP_W), lambda i: (0, 0))],
        out_specs=[pl.BlockSpec((tm, GROUP_W), lambda i: (i, 0)), pl.BlockSpec((tm, GROUP_W), lambda i: (i, 0)),
                   pl.BlockSpec((1, GROUP_W), lambda i: (0, 0))],
        out_shape=[jax.ShapeDtypeStruct((s, GROUP_W), F32), jax.ShapeDtypeStruct((s, GROUP_W), BF16),
                   jax.ShapeDtypeStruct((1, GROUP_W), F32)],
        compiler_params=_params(("arbitrary",)),
    )(o, z, dy, norm_g)


def _log_sigmoid(x):
    return jnp.minimum(x, 0.0) - jnp.log(1.0 + jnp.exp(-jnp.abs(x)))


def _gla_consts():
    c = GLA_CHUNK
    row = lax.broadcasted_iota(jnp.int32, (c, c), 0)
    colm = lax.broadcasted_iota(jnp.int32, (c, c), 1)
    lane = lax.broadcasted_iota(jnp.int32, (1, LANES), 1)
    low = row >= colm
    up = colm >= row
    heads = ((lane < GLA_DK).astype(F32), (lane >= GLA_DK).astype(F32))
    return low, up, heads


def _gla_chunk(q, k, la, tri_f):
    cum = _dot_exact(tri_f, la)
    last = jnp.sum(la, axis=0, keepdims=True)
    eq = jnp.exp(cum)
    ek = jnp.exp(-cum)
    el = jnp.exp(last - cum)
    dec = jnp.exp(last)
    return eq, ek, el, dec


def _gla_gates(ga_ref, wa_ref, ba_ref, la_ref, s, tm):
    def step(i, carry):
        r = pl.ds(pl.multiple_of(i * tm, tm), tm)
        pre = _dot(ga_ref[r, :].astype(BF16), wa_ref[...].astype(BF16)) + ba_ref[...]
        la_ref[r, :] = _log_sigmoid(pre) * (1.0 / GLA_TAU)
        return carry
    lax.fori_loop(0, s // tm, step, 0)


def _gla_fwd(z, wa_f, wa_b, ba_f, ba_b, norm_g):
    s = z.shape[0]
    c = GLA_CHUNK
    nchunk = s // c
    scale = GLA_DK ** -0.5
    tm = min(s, 512)
    one = pl.Buffered(1)

    def body(q_ref, k_ref, v_ref, ga_ref, g_ref, waf_ref, wab_ref, baf_ref, bab_ref, ng_ref, o_ref, y_ref,
             la_s, st):
        low, up, heads = _gla_consts()
        _gla_gates(ga_ref, waf_ref, baf_ref, la_s.at[0], s, tm)
        _gla_gates(ga_ref, wab_ref, bab_ref, la_s.at[1], s, tm)
        for d in range(2):
            tri = (low, up)[d]
            tri_f = tri.astype(F32)

            def step(i, states):
                n = i if d == 0 else nchunk - 1 - i
                r = pl.ds(pl.multiple_of(n * c, c), c)
                q = q_ref[r, :] * scale
                k = k_ref[r, :]
                eq, ek, el, dec = _gla_chunk(q, k, la_s[d, r, :], tri_f)
                qt = q * eq
                ktb = (k * ek).astype(BF16)
                kl = k * el
                new_states = []
                for hh in range(2):
                    cols = slice(LANES * hh, LANES * (hh + 1))
                    vb = v_ref[r, cols].astype(BF16)
                    qm = (qt * heads[hh]).astype(BF16)
                    a = jnp.where(tri, _dot(qm, ktb, NT), 0.0)
                    o = _dot(a.astype(BF16), vb) + _dot(qm, states[hh].astype(BF16), NT)
                    if d == 0:
                        o_ref[r, cols] = o
                    else:
                        o_ref[r, cols] += o
                    new_states.append(states[hh] * dec + _dot(vb, (kl * heads[hh]).astype(BF16), TN))
                return tuple(new_states)

            zero = jnp.zeros((LANES, LANES), F32)
            _chunk_loop(nchunk, step, (zero, zero), GLA_UNROLL)

        def epi(i, carry):
            r = pl.ds(pl.multiple_of(i * tm, tm), tm)
            for hh in range(2):
                cols = slice(LANES * hh, LANES * (hh + 1))
                o = o_ref[r, cols]
                rr = lax.rsqrt(jnp.mean(o * o, axis=-1, keepdims=True) + EPS)
                y_ref[r, cols] = (_silu(g_ref[r, cols]) * (o * rr * ng_ref[:, cols])).astype(BF16)
            return carry

        lax.fori_loop(0, s // tm, epi, 0)

    w2 = 2 * LANES
    return pl.pallas_call(
        body, name="gla_fwd", grid=(2,),
        in_specs=[pl.BlockSpec((s, LANES), lambda p: (0, B_Q + p), pipeline_mode=one),
                  pl.BlockSpec((s, LANES), lambda p: (0, B_K + p), pipeline_mode=one),
                  pl.BlockSpec((s, w2), lambda p: (0, B_V // 2 + p), pipeline_mode=one),
                  pl.BlockSpec((s, LANES), lambda p: (0, GA), pipeline_mode=one),
                  pl.BlockSpec((s, w2), lambda p: (0, B_G // 2 + p), pipeline_mode=one),
                  pl.BlockSpec((LANES, LANES), lambda p: (0, p)),
                  pl.BlockSpec((LANES, LANES), lambda p: (0, p)),
                  pl.BlockSpec((1, LANES), lambda p: (0, p)),
                  pl.BlockSpec((1, LANES), lambda p: (0, p)),
                  pl.BlockSpec((1, w2), lambda p: (0, p))],
        out_specs=[pl.BlockSpec((s, w2), lambda p: (0, p)), pl.BlockSpec((s, w2), lambda p: (0, p))],
        out_shape=[jax.ShapeDtypeStruct((s, GROUP_W), F32), jax.ShapeDtypeStruct((s, GROUP_W), BF16)],
        scratch_shapes=[pltpu.VMEM((2, s, LANES), F32), pltpu.VMEM((2, LANES, LANES), F32)],
        compiler_params=_params(("arbitrary",)),
    )(z, z, z, z, z, wa_f, wa_b, ba_f, ba_b, norm_g)


def _gla_bwd(z, d_o, wa_f, wa_b, ba_f, ba_b):
    s = z.shape[0]
    c = GLA_CHUNK
    nchunk = s // c
    scale = GLA_DK ** -0.5
    tm = min(s, 512)
    one = pl.Buffered(1)

    def body(q_ref, k_ref, v_ref, ga_ref, do_ref, waf_ref, wab_ref, baf_ref, bab_ref,
             dq_ref, dk_ref, dv_ref, dga_ref, dwaf_ref, dwab_ref, dbaf_ref, dbab_ref,
             la_s, dla_s, stash, st, dq_s, dk_s, dv_s):
        low, up, heads = _gla_consts()
        rowi = lax.broadcasted_iota(jnp.int32, (c, 1), 0)
        _gla_gates(ga_ref, waf_ref, baf_ref, la_s.at[0], s, tm)
        _gla_gates(ga_ref, wab_ref, bab_ref, la_s.at[1], s, tm)
        for d in range(2):
            tri = (low, up)[d]
            tri_f = tri.astype(F32)
            tri_t = (up, low)[d].astype(F32)
            last_row = (rowi == (c - 1 if d == 0 else 0)).astype(F32)
            order = (lambda i: i) if d == 0 else (lambda i: nchunk - 1 - i)
            zero = jnp.zeros((LANES, LANES), F32)

            def states(i, sts):
                n = order(i)
                r = pl.ds(pl.multiple_of(n * c, c), c)
                k = k_ref[r, :]
                la = la_s[d, r, :]
                _, _, el, dec = _gla_chunk(k, k, la, tri_f)
                kl = k * el
                new = []
                for hh in range(2):
                    cols = slice(LANES * hh, LANES * (hh + 1))
                    stash[hh, n] = sts[hh]
                    new.append(sts[hh] * dec + _dot(v_ref[r, cols].astype(BF16), (kl * heads[hh]).astype(BF16), TN))
                return tuple(new)

            _chunk_loop(nchunk, states, (zero, zero), GLA_UNROLL)

            def step(i, dsts):
                n = order(nchunk - 1 - i)
                r = pl.ds(pl.multiple_of(n * c, c), c)
                q = q_ref[r, :] * scale
                k = k_ref[r, :]
                eq, ek, el, dec = _gla_chunk(q, k, la_s[d, r, :], tri_f)
                qt = q * eq
                kt = k * ek
                kl = k * el
                ktb = kt.astype(BF16)
                dqt = jnp.zeros((c, LANES), F32)
                dkt = jnp.zeros((c, LANES), F32)
                dkl = jnp.zeros((c, LANES), F32)
                ddec = jnp.zeros((1, LANES), F32)
                new = []
                for hh in range(2):
                    cols = slice(LANES * hh, LANES * (hh + 1))
                    vb = v_ref[r, cols].astype(BF16)
                    dob = do_ref[r, cols].astype(BF16)
                    qm = (qt * heads[hh]).astype(BF16)
                    a = jnp.where(tri, _dot(qm, ktb, NT), 0.0).astype(BF16)
                    da = jnp.where(tri, _dot(dob, vb, NT), 0.0).astype(BF16)
                    sn = stash[hh, n]
                    dst = dsts[hh]
                    dstb = dst.astype(BF16)
                    dqt = dqt + (_dot(da, ktb) + _dot(dob, sn.astype(BF16))) * heads[hh]
                    dkt = dkt + _dot(da, qm, TN)
                    dv = _dot(a, dob, TN) + _dot((kl * heads[hh]).astype(BF16), dstb, NT)
                    dkl = dkl + _dot(vb, dstb)
                    ddec = ddec + jnp.sum(dst * sn, axis=0, keepdims=True)
                    new.append(dst * dec + _dot(dob, qm, TN))
                    if d == 0:
                        dv_s[r, cols] = dv
                    else:
                        dv_ref[r, cols] = (dv_s[r, cols] + dv).astype(BF16)
                dlast = ddec * dec + jnp.sum(dkl * kl, axis=0, keepdims=True)
                dq = dqt * eq * scale
                dk = dkt * ek + dkl * el
                dcum = dqt * qt - dkt * kt - dkl * kl + last_row * dlast
                dla_s[d, r, :] = _dot_exact(tri_t, dcum)
                if d == 0:
                    dq_s[r, :] = dq
                    dk_s[r, :] = dk
                else:
                    dq_ref[r, :] = (dq_s[r, :] + dq).astype(BF16)
                    dk_ref[r, :] = (dk_s[r, :] + dk).astype(BF16)
                return tuple(new)

            _chunk_loop(nchunk, step, (zero, zero), GLA_UNROLL)

        first = pl.program_id(0) == 0
        for d, (wa_ref, ba_ref, dwa_ref, dba_ref) in enumerate(
                ((waf_ref, baf_ref, dwaf_ref, dbaf_ref), (wab_ref, bab_ref, dwab_ref, dbab_ref))):
            dwa_ref[...] = jnp.zeros_like(dwa_ref)
            dba_ref[...] = jnp.zeros_like(dba_ref)

            def gates(i, carry):
                r = pl.ds(pl.multiple_of(i * tm, tm), tm)
                gab = ga_ref[r, :].astype(BF16)
                wab16 = wa_ref[...].astype(BF16)
                pre = _dot(gab, wab16) + ba_ref[...]
                dpre = dla_s[d, r, :] * (1.0 / GLA_TAU) * _sigmoid(-pre)
                dpb = dpre.astype(BF16)
                dwa_ref[...] += _dot(gab, dpb, TN)
                dba_ref[...] += jnp.sum(dpre, axis=0, keepdims=True)
                dga = _dot(dpb, wab16, NT)
                if d == 0:
                    @pl.when(first)
                    def _():
                        dga_ref[r, :] = dga

                    @pl.when(jnp.logical_not(first))
                    def _():
                        dga_ref[r, :] += dga
                else:
                    dga_ref[r, :] += dga
                return carry

            lax.fori_loop(0, s // tm, gates, 0)

    w2 = 2 * LANES
    return pl.pallas_call(
        body, name="gla_bwd", grid=(2,),
        in_specs=[pl.BlockSpec((s, LANES), lambda p: (0, B_Q + p), pipeline_mode=one),
                  pl.BlockSpec((s, LANES), lambda p: (0, B_K + p), pipeline_mode=one),
                  pl.BlockSpec((s, w2), lambda p: (0, B_V // 2 + p), pipeline_mode=one),
                  pl.BlockSpec((s, LANES), lambda p: (0, GA), pipeline_mode=one),
                  pl.BlockSpec((s, w2), lambda p: (0, p), pipeline_mode=one),
                  pl.BlockSpec((LANES, LANES), lambda p: (0, p)),
                  pl.BlockSpec((LANES, LANES), lambda p: (0, p)),
                  pl.BlockSpec((1, LANES), lambda p: (0, p)),
                  pl.BlockSpec((1, LANES), lambda p: (0, p))],
        out_specs=[pl.BlockSpec((s, LANES), lambda p: (0, p), pipeline_mode=one),
                   pl.BlockSpec((s, LANES), lambda p: (0, p), pipeline_mode=one),
                   pl.BlockSpec((s, w2), lambda p: (0, p), pipeline_mode=one),
                   pl.BlockSpec((s, LANES), lambda p: (0, 0), pipeline_mode=one),
                   pl.BlockSpec((LANES, LANES), lambda p: (0, p)),
                   pl.BlockSpec((LANES, LANES), lambda p: (0, p)),
                   pl.BlockSpec((1, LANES), lambda p: (0, p)),
                   pl.BlockSpec((1, LANES), lambda p: (0, p))],
        out_shape=[jax.ShapeDtypeStruct((s, w2), BF16), jax.ShapeDtypeStruct((s, w2), BF16),
                   jax.ShapeDtypeStruct((s, GROUP_W), BF16), jax.ShapeDtypeStruct((s, LANES), F32),
                   jax.ShapeDtypeStruct((LANES, w2), F32), jax.ShapeDtypeStruct((LANES, w2), F32),
                   jax.ShapeDtypeStruct((1, w2), F32), jax.ShapeDtypeStruct((1, w2), F32)],
        scratch_shapes=[pltpu.VMEM((2, s, LANES), F32), pltpu.VMEM((2, s, LANES), F32),
                        pltpu.VMEM((2, nchunk, LANES, LANES), F32), pltpu.VMEM((2, LANES, LANES), F32),
                        pltpu.VMEM((s, LANES), F32), pltpu.VMEM((s, LANES), F32), pltpu.VMEM((s, w2), F32)],
        compiler_params=_params(("arbitrary",)),
    )(z, z, z, z, d_o, wa_f, wa_b, ba_f, ba_b)


def _shift_rows(x, d, rowi):
    s = x.shape[0]
    if d == 0:
        return x
    y = pltpu.roll(x, d % s, 0)
    keep = (rowi >= d) if d > 0 else (rowi < s + d)
    return jnp.where(keep, y, 0.0)


def _run_sum(x, m, step, rowi):
    acc, n = x, 1
    while n < m:
        acc = acc + _shift_rows(acc, step * n, rowi)
        n *= 2
    return acc


def _pool_counts(s, w, rowi):
    hi = jnp.minimum(rowi + w // 2, s)
    lo = jnp.maximum(rowi - w // 2, 0)
    return (hi - lo).astype(F32)


def _pooled(u, w, rowi):
    s = u.shape[0]
    win = _shift_rows(_run_sum(u, w // 2, 1, rowi), 1, rowi) + _run_sum(u, w // 2, -1, rowi)
    return win / _pool_counts(s, w, rowi) - u


def _pool_fwd(z, pool_w, pool_scale):
    s = z.shape[0]
    one = pl.Buffered(1)

    def body(u_ref, g_ref, w_ref, sc_ref, y_ref):
        rowi = lax.broadcasted_iota(jnp.int32, (s, 1), 0)
        for g, w in enumerate(POOL_WINDOWS):
            cols = slice(LANES * g, LANES * (g + 1))
            pooled = _pooled(u_ref[:, cols], w, rowi)
            mixed = _dot(pooled.astype(BF16), w_ref[g].astype(BF16))
            y_ref[:, cols] = (_silu(g_ref[:, cols]) * (mixed * sc_ref[:, cols])).astype(BF16)

    return pl.pallas_call(
        body, name="pool_fwd", grid=(1,),
        in_specs=[pl.BlockSpec((s, GROUP_W), lambda i: (0, C_V // 4), pipeline_mode=one),
                  pl.BlockSpec((s, GROUP_W), lambda i: (0, C_G // 4), pipeline_mode=one),
                  pl.BlockSpec((4, LANES, LANES), lambda i: (0, 0, 0)),
                  pl.BlockSpec((1, GROUP_W), lambda i: (0, 0))],
        out_specs=pl.BlockSpec((s, GROUP_W), lambda i: (0, 0), pipeline_mode=one),
        out_shape=jax.ShapeDtypeStruct((s, GROUP_W), BF16),
        compiler_params=_params(("arbitrary",)),
    )(z, z, pool_w, pool_scale)


def _pool_bwd(z, dy, pool_w, pool_scale):
    s = z.shape[0]
    one = pl.Buffered(1)

    def body(u_ref, g_ref, dy_ref, w_ref, sc_ref, du_ref, dg_ref, dw_ref, dsc_ref):
        rowi = lax.broadcasted_iota(jnp.int32, (s, 1), 0)
        for g, w in enumerate(POOL_WINDOWS):
            cols = slice(LANES * g, LANES * (g + 1))
            gate, dyv, sc = g_ref[:, cols], dy_ref[:, cols], sc_ref[:, cols]
            wb = w_ref[g].astype(BF16)
            pooled = _pooled(u_ref[:, cols], w, rowi)
            pb = pooled.astype(BF16)
            mixed = _dot(pb, wb)
            dg_ref[:, cols] = (dyv * (mixed * sc) * _silu_grad(gate)).astype(BF16)
            dt = dyv * _silu(gate)
            dsc_ref[:, cols] = jnp.sum(dt * mixed, axis=0, keepdims=True)
            dmb = (dt * sc).astype(BF16)
            dw_ref[g] = _dot(pb, dmb, TN)
            dpool = _dot(dmb, wb, NT)
            e = dpool / _pool_counts(s, w, rowi)
            du_ref[:, cols] = (_run_sum(e, w // 2, 1, rowi) + _shift_rows(_run_sum(e, w // 2, -1, rowi), -1, rowi)
                               - dpool).astype(BF16)

    return pl.pallas_call(
        body, name="pool_bwd", grid=(1,),
        in_specs=[pl.BlockSpec((s, GROUP_W), lambda i: (0, C_V // 4), pipeline_mode=one),
                  pl.BlockSpec((s, GROUP_W), lambda i: (0, C_G // 4), pipeline_mode=one),
                  pl.BlockSpec((s, GROUP_W), lambda i: (0, 2), pipeline_mode=one),
                  pl.BlockSpec((4, LANES, LANES), lambda i: (0, 0, 0)),
                  pl.BlockSpec((1, GROUP_W), lambda i: (0, 0))],
        out_specs=[pl.BlockSpec((s, GROUP_W), lambda i: (0, 0), pipeline_mode=one),
                   pl.BlockSpec((s, GROUP_W), lambda i: (0, 0), pipeline_mode=one),
                   pl.BlockSpec((4, LANES, LANES), lambda i: (0, 0, 0)),
                   pl.BlockSpec((1, GROUP_W), lambda i: (0, 0))],
        out_shape=[jax.ShapeDtypeStruct((s, GROUP_W), BF16), jax.ShapeDtypeStruct((s, GROUP_W), BF16),
                   jax.ShapeDtypeStruct((4, LANES, LANES), F32), jax.ShapeDtypeStruct((1, GROUP_W), F32)],
        compiler_params=_params(("arbitrary",)),
    )(z, z, dy, pool_w, pool_scale)


def _mla_heads(qf, kv, kpe, qg, kg, cos, sin):
    out = []
    for h in range(4):
        qa = qf[:, LANES * h:LANES * (h + 1)]
        qb = qf[:, 512 + LANES * h:512 + LANES * (h + 1)]
        ka = kv[:, 256 * h:256 * h + LANES]
        rq = lax.rsqrt((jnp.sum(qa * qa, axis=-1, keepdims=True) + jnp.sum(qb * qb, axis=-1, keepdims=True))
                       * (1.0 / MLA_QK) + EPS)
        rk = lax.rsqrt((jnp.sum(ka * ka, axis=-1, keepdims=True) + jnp.sum(kpe * kpe, axis=-1, keepdims=True))
                       * (1.0 / MLA_QK) + EPS)
        out.append((qa, qb, rq, ka, rk))
    return out


def _mla_latents(mq_ref, mkv_ref, gq_ref, gkv_ref, wq_ref, wkv_ref):
    mq = mq_ref[...]
    rq = lax.rsqrt(jnp.mean(mq * mq, axis=-1, keepdims=True) + EPS)
    qn = mq * rq
    qnb = (qn * gq_ref[...]).astype(BF16)
    mkv = mkv_ref[...]
    rk = lax.rsqrt(jnp.mean(mkv * mkv, axis=-1, keepdims=True) + EPS)
    kvn = mkv * rk
    kvnb = (kvn * gkv_ref[...]).astype(BF16)
    qf = _dot(qnb, wq_ref[...])
    kv = _dot(kvnb, wkv_ref[...])
    return qn, rq, qnb, kvn, rk, kvnb, qf, kv


def _mla_prep(z, cos_m, sin_m, gq, wq, gkv, wkv, qg, kg, tm):
    s = z.shape[0]

    def body(mq_ref, mkv_ref, mkr_ref, cos_ref, sin_ref, gq_ref, wq_ref, gkv_ref, wkv_ref, qg_ref, kg_ref,
             q_ref, k_ref, v_ref):
        _, _, _, _, _, _, qf, kv = _mla_latents(mq_ref, mkv_ref, gq_ref, gkv_ref, wq_ref, wkv_ref)
        kpe = mkr_ref[...]
        cos, sin = cos_ref[...], sin_ref[...]
        qg, kg = qg_ref[...], kg_ref[...]
        for h, (qa, qb, rq, ka, rk) in enumerate(_mla_heads(qf, kv, kpe, qg, kg, cos, sin)):
            q_ref[h, :, 0:LANES] = (qa * rq * qg[:, 0:LANES] * ATTN_Q_SCALE).astype(BF16)
            q_ref[h, :, LANES:] = (_rope(qb * rq * qg[:, LANES:], cos, sin) * ATTN_Q_SCALE).astype(BF16)
            k_ref[h, :, 0:LANES] = (ka * rk * kg[:, 0:LANES]).astype(BF16)
            k_ref[h, :, LANES:] = _rope(kpe * rk * kg[:, LANES:], cos, sin).astype(BF16)
            v_ref[h] = kv[:, 256 * h + LANES:256 * (h + 1)].astype(BF16)

    full = lambda shape: pl.BlockSpec(shape, lambda i: (0,) * len(shape))
    return pl.pallas_call(
        body, name="mla_prep", grid=(s // tm,),
        in_specs=[pl.BlockSpec((tm, 512), lambda i: (i, M_Q // 4)),
                  pl.BlockSpec((tm, 256), lambda i: (i, M_KV // 2)),
                  pl.BlockSpec((tm, LANES), lambda i: (i, M_KR)),
                  pl.BlockSpec((tm, LANES), lambda i: (i, 0)),
                  pl.BlockSpec((tm, LANES), lambda i: (i, 0)),
                  full((1, 512)), full((512, 1024)), full((1, 256)), full((256, 1024)), full((1, 256)), full((1, 256))],
        out_specs=[pl.BlockSpec((4, tm, 256), lambda i: (0, i, 0)), pl.BlockSpec((4, tm, 256), lambda i: (0, i, 0)),
                   pl.BlockSpec((4, tm, LANES), lambda i: (0, i, 0))],
        out_shape=[jax.ShapeDtypeStruct((4, s, 256), BF16), jax.ShapeDtypeStruct((4, s, 256), BF16),
                   jax.ShapeDtypeStruct((4, s, LANES), BF16)],
        compiler_params=_params(("parallel",)),
    )(z, z, z, cos_m, sin_m, gq, wq, gkv, wkv, qg, kg)


def _mla_prep_bwd(z, cos_m, sin_m, gq, wq, gkv, wkv, qg, kg, dq, dk, dv, tm):
    s = z.shape[0]

    def body(mq_ref, mkv_ref, mkr_ref, cos_ref, sin_ref, gq_ref, wq_ref, gkv_ref, wkv_ref, qg_ref, kg_ref,
             dq_ref, dk_ref, dv_ref,
             dmq_ref, dmkv_ref, dmkr_ref, dwq_ref, dwkv_ref, dgq_ref, dgkv_ref, dqg_ref, dkg_ref, dqf, dkv):
        @pl.when(pl.program_id(0) == 0)
        def _():
            for r in (dwq_ref, dwkv_ref, dgq_ref, dgkv_ref, dqg_ref, dkg_ref):
                r[...] = jnp.zeros_like(r)

        qn, rq0, qnb, kvn, rk0, kvnb, qf, kv = _mla_latents(mq_ref, mkv_ref, gq_ref, gkv_ref, wq_ref, wkv_ref)
        kpe = mkr_ref[...]
        cos, sin = cos_ref[...], sin_ref[...]
        qg, kg = qg_ref[...], kg_ref[...]
        dkpe = jnp.zeros_like(kpe)
        inv = 1.0 / MLA_QK

        def norm_bwd(a, b, r, da_n, db_n, g):
            ga, gb = g[:, 0:LANES], g[:, LANES:]
            dg_a = jnp.sum(da_n * a * r, axis=0, keepdims=True)
            dg_b = jnp.sum(db_n * b * r, axis=0, keepdims=True)
            ua, ub = da_n * ga, db_n * gb
            dt = (jnp.sum(ua * a, axis=-1, keepdims=True) + jnp.sum(ub * b, axis=-1, keepdims=True)) * inv
            r3 = r * r * r
            return r * ua - a * (r3 * dt), r * ub - b * (r3 * dt), dg_a, dg_b

        for h, (qa, qb, rq, ka, rk) in enumerate(_mla_heads(qf, kv, kpe, qg, kg, cos, sin)):
            dqa, dqb, dga, dgb = norm_bwd(qa, qb, rq, dq_ref[h, :, 0:LANES] * ATTN_SCALE,
                                          _rope_t(dq_ref[h, :, LANES:] * ATTN_SCALE, cos, sin), qg)
            dqf[:, LANES * h:LANES * (h + 1)] = dqa
            dqf[:, 512 + LANES * h:512 + LANES * (h + 1)] = dqb
            dqg_ref[:, 0:LANES] += dga
            dqg_ref[:, LANES:] += dgb
            ln2 = math.log(2.0)
            dka, dkb, dga, dgb = norm_bwd(ka, kpe, rk, dk_ref[h, :, 0:LANES] * ln2,
                                          _rope_t(dk_ref[h, :, LANES:] * ln2, cos, sin), kg)
            dkv[:, 256 * h:256 * h + LANES] = dka
            dkv[:, 256 * h + LANES:256 * (h + 1)] = dv_ref[h]
            dkpe = dkpe + dkb
            dkg_ref[:, 0:LANES] += dga
            dkg_ref[:, LANES:] += dgb
        dmkr_ref[...] = dkpe.astype(BF16)

        def latent_bwd(dfull, w_ref, nb, n, r, g_ref, dw_ref, dg_ref, dlat_ref):
            db = dfull.astype(BF16)
            dn = _dot(db, w_ref[...], NT)
            dw_ref[...] += _dot(nb, db, TN)
            dg_ref[...] += jnp.sum(dn * n, axis=0, keepdims=True)
            u = dn * g_ref[...]
            dlat_ref[...] = (r * (u - n * jnp.mean(u * n, axis=-1, keepdims=True))).astype(BF16)

        latent_bwd(dqf[...], wq_ref, qnb, qn, rq0, gq_ref, dwq_ref, dgq_ref, dmq_ref)
        latent_bwd(dkv[...], wkv_ref, kvnb, kvn, rk0, gkv_ref, dwkv_ref, dgkv_ref, dmkv_ref)

    full = lambda shape: pl.BlockSpec(shape, lambda i: (0,) * len(shape))
    return pl.pallas_call(
        body, name="mla_prep_bwd", grid=(s // tm,),
        in_specs=[pl.BlockSpec((tm, 512), lambda i: (i, M_Q // 4)),
                  pl.BlockSpec((tm, 256), lambda i: (i, M_KV // 2)),
                  pl.BlockSpec((tm, LANES), lambda i: (i, M_KR)),
                  pl.BlockSpec((tm, LANES), lambda i: (i, 0)),
                  pl.BlockSpec((tm, LANES), lambda i: (i, 0)),
                  full((1, 512)), full((512, 1024)), full((1, 256)), full((256, 1024)), full((1, 256)), full((1, 256)),
                  pl.BlockSpec((4, tm, 256), lambda i: (0, i, 0)), pl.BlockSpec((4, tm, 256), lambda i: (0, i, 0)),
                  pl.BlockSpec((4, tm, LANES), lambda i: (0, i, 0))],
        out_specs=[pl.BlockSpec((tm, 512), lambda i: (i, 0)), pl.BlockSpec((tm, 256), lambda i: (i, 0)),
                   pl.BlockSpec((tm, LANES), lambda i: (i, 0)),
                   full((512, 1024)), full((256, 1024)), full((1, 512)), full((1, 256)), full((1, 256)), full((1, 256))],
        out_shape=[jax.ShapeDtypeStruct((s, 512), BF16), jax.ShapeDtypeStruct((s, 256), BF16),
                   jax.ShapeDtypeStruct((s, LANES), BF16),
                   jax.ShapeDtypeStruct((512, 1024), F32), jax.ShapeDtypeStruct((256, 1024), F32),
                   jax.ShapeDtypeStruct((1, 512), F32), jax.ShapeDtypeStruct((1, 256), F32),
                   jax.ShapeDtypeStruct((1, 256), F32), jax.ShapeDtypeStruct((1, 256), F32)],
        scratch_shapes=[pltpu.VMEM((tm, 1024), F32), pltpu.VMEM((tm, 1024), F32)],
        compiler_params=_params(("arbitrary",)),
    )(z, z, z, cos_m, sin_m, gq, wq, gkv, wkv, qg, kg, dq, dk, dv)


def _attn_fwd(q, k, v, z, tq, rider=None):
    s = q.shape[1]

    def body(q_ref, k_ref, v_ref, g_ref, o_ref, y_ref, lse_ref):
        sc = _dot(q_ref[...], k_ref[...], NT)
        m = jnp.max(sc, axis=-1, keepdims=True)
        p = jnp.exp2(sc - m)
        l = jnp.sum(p, axis=-1, keepdims=True)
        o = _dot(p.astype(BF16), v_ref[...]) / l
        o_ref[...] = o
        y_ref[...] = (_silu(g_ref[...]) * o).astype(BF16)
        lse_ref[...] = m + jnp.log2(l)

    return _call(
        body, "attn_fwd", (4, s // tq),
        [pl.BlockSpec((None, tq, 256), lambda h, i: (h, i, 0)),
         pl.BlockSpec((None, s, 256), lambda h, i: (h, 0, 0)),
         pl.BlockSpec((None, s, LANES), lambda h, i: (h, 0, 0)),
         pl.BlockSpec((tq, LANES), lambda h, i: (i, M_G + h))],
        [pl.BlockSpec((tq, LANES), lambda h, i: (i, h)), pl.BlockSpec((tq, LANES), lambda h, i: (i, h)),
         pl.BlockSpec((None, tq, 1), lambda h, i: (h, i, 0))],
        [jax.ShapeDtypeStruct((s, GROUP_W), F32), jax.ShapeDtypeStruct((s, GROUP_W), BF16),
         jax.ShapeDtypeStruct((4, s, 1), F32)],
        [], ("parallel", "parallel"), (q, k, v, z), rider)


def _attn_bwd(q, k, v, z, o, lse, dy, tq, rider=None):
    s = q.shape[1]

    def body(q_ref, k_ref, v_ref, g_ref, o_ref, lse_ref, dy_ref, dq_ref, dk_ref, dv_ref, dg_ref):
        @pl.when(pl.program_id(1) == 0)
        def _():
            dk_ref[...] = jnp.zeros_like(dk_ref)
            dv_ref[...] = jnp.zeros_like(dv_ref)

        gate, ov, dyv = g_ref[...], o_ref[...], dy_ref[...]
        do = dyv * _silu(gate)
        dg_ref[...] = (dyv * ov * _silu_grad(gate)).astype(BF16)
        delta = jnp.sum(do * ov, axis=-1, keepdims=True)
        dob = do.astype(BF16)
        qb, kb = q_ref[...], k_ref[...]
        p = jnp.exp2(_dot(qb, kb, NT) - lse_ref[...])
        dp = _dot(dob, v_ref[...], NT)
        ds = (p * (dp - delta)).astype(BF16)
        dq_ref[...] = _dot(ds, kb)
        dk_ref[...] += _dot(ds, qb, TN)
        dv_ref[...] += _dot(p.astype(BF16), dob, TN)

    return _call(
        body, "attn_bwd", (4, s // tq),
        [pl.BlockSpec((None, tq, 256), lambda h, i: (h, i, 0)),
         pl.BlockSpec((None, s, 256), lambda h, i: (h, 0, 0)),
         pl.BlockSpec((None, s, LANES), lambda h, i: (h, 0, 0)),
         pl.BlockSpec((tq, LANES), lambda h, i: (i, M_G + h)),
         pl.BlockSpec((tq, LANES), lambda h, i: (i, h)),
         pl.BlockSpec((None, tq, 1), lambda h, i: (h, i, 0)),
         pl.BlockSpec((tq, LANES), lambda h, i: (i, 12 + h))],
        [pl.BlockSpec((None, tq, 256), lambda h, i: (h, i, 0)),
         pl.BlockSpec((None, s, 256), lambda h, i: (h, 0, 0)),
         pl.BlockSpec((None, s, LANES), lambda h, i: (h, 0, 0)),
         pl.BlockSpec((tq, LANES), lambda h, i: (i, h))],
        [jax.ShapeDtypeStruct((4, s, 256), F32), jax.ShapeDtypeStruct((4, s, 256), F32),
         jax.ShapeDtypeStruct((4, s, LANES), F32), jax.ShapeDtypeStruct((s, GROUP_W), BF16)],
        [], ("parallel", "arbitrary"), (q, k, v, z, o, lse, dy), rider)


def _loss_head(x, target, tm):
    s, d = x.shape

    def body(x_ref, t_ref, dx_ref, loss_ref):
        @pl.when(pl.program_id(0) == 0)
        def _():
            loss_ref[...] = jnp.zeros_like(loss_ref)
        err = x_ref[...] - t_ref[...]
        dx_ref[...] = err * (1.0 / d)
        per_tok = jnp.mean(err * err, axis=-1, keepdims=True)
        loss_ref[...] += 0.5 * jnp.sum(per_tok, axis=0, keepdims=True)

    return pl.pallas_call(
        body, name="loss_head", grid=(s // tm,),
        in_specs=[pl.BlockSpec((tm, d), lambda i: (i, 0)), pl.BlockSpec((tm, d), lambda i: (i, 0))],
        out_specs=[pl.BlockSpec((tm, d), lambda i: (i, 0)), pl.BlockSpec((1, LANES), lambda i: (0, 0))],
        out_shape=[jax.ShapeDtypeStruct((s, d), F32), jax.ShapeDtypeStruct((1, LANES), F32)],
        compiler_params=_params(("arbitrary",)),
    )(x, target)


def _norm_bwd(x, g, dh, dres, tm):
    s, d = x.shape

    def body(x_ref, g_ref, dh_ref, dres_ref, dx_ref, dg_ref):
        @pl.when(pl.program_id(0) == 0)
        def _():
            dg_ref[...] = jnp.zeros_like(dg_ref)
        xv, dhv = x_ref[...], dh_ref[...]
        r = lax.rsqrt(jnp.mean(xv * xv, axis=-1, keepdims=True) + EPS)
        n = xv * r
        dg_ref[...] += jnp.sum(dhv * n, axis=0, keepdims=True)
        u = dhv * g_ref[...]
        dx_ref[...] = dres_ref[...] + r * (u - n * jnp.mean(u * n, axis=-1, keepdims=True))

    row = lambda: pl.BlockSpec((tm, d), lambda i: (i, 0))
    return pl.pallas_call(
        body, name="norm_bwd", grid=(s // tm,),
        in_specs=[row(), pl.BlockSpec((1, d), lambda i: (0, 0)), row(), row()],
        out_specs=[row(), pl.BlockSpec((1, d), lambda i: (0, 0))],
        out_shape=[jax.ShapeDtypeStruct((s, d), F32), jax.ShapeDtypeStruct((1, d), F32)],
        compiler_params=_params(("arbitrary",)),
    )(x, g, dh, dres)


def _adam(parts, w, m, v, name, tr):
    r, c = w.shape
    tr = min(tr, r)
    c1 = 1.0 - ADAM_B1 ** ADAM_STEP
    c2 = 1.0 - ADAM_B2 ** ADAM_STEP

    def body(p_ref, w_ref, m_ref, v_ref, g_ref, d_ref, nm_ref, nv_ref):
        g = p_ref[0].astype(F32)
        for i in range(1, N_DEV):
            g = g + p_ref[i].astype(F32)
        nm = ADAM_B1 * m_ref[...] + (1.0 - ADAM_B1) * g
        nv = ADAM_B2 * v_ref[...] + (1.0 - ADAM_B2) * (g * g)
        g_ref[...] = g
        nm_ref[...] = nm
        nv_ref[...] = nv
        d_ref[...] = -ADAM_LR * ((nm / c1) / (jnp.sqrt(nv / c2) + ADAM_EPS) + ADAM_WD * w_ref[...])

    blk = lambda: pl.BlockSpec((tr, c), lambda i: (i, 0))
    return pl.pallas_call(
        body, name=name, grid=(r // tr,),
        in_specs=[pl.BlockSpec((N_DEV, tr, c), lambda i: (0, i, 0)), blk(), blk(), blk()],
        out_specs=[blk(), blk(), blk(), blk()],
        out_shape=[jax.ShapeDtypeStruct((r, c), F32)] * 4,
        compiler_params=_params(("parallel",)),
    )(parts, w, m, v)


def _adam_layers(parts, w, m, v, name, tr):
    nl, r, c = w.shape
    tr = min(tr, r)
    nr = r // tr
    c1 = 1.0 - ADAM_B1 ** ADAM_STEP
    c2 = 1.0 - ADAM_B2 ** ADAM_STEP

    def body(*refs):
        p_refs, (w_ref, m_ref, v_ref, g_ref, d_ref, nm_ref, nv_ref) = refs[:nl], refs[nl:]
        for l in range(nl):
            @pl.when(pl.program_id(0) == l)
            def _(p_ref=p_refs[l]):
                g = p_ref[0].astype(F32)
                for i in range(1, N_DEV):
                    g = g + p_ref[i].astype(F32)
                nm = ADAM_B1 * m_ref[...] + (1.0 - ADAM_B1) * g
                nv = ADAM_B2 * v_ref[...] + (1.0 - ADAM_B2) * (g * g)
                g_ref[...] = g
                nm_ref[...] = nm
                nv_ref[...] = nv
                d_ref[...] = -ADAM_LR * ((nm / c1) / (jnp.sqrt(nv / c2) + ADAM_EPS) + ADAM_WD * w_ref[...])

    def part_spec(l):
        return pl.BlockSpec((N_DEV, tr, c), lambda ll, i: (0, jnp.where(ll == l, i, 0 if l > 0 else nr - 1), 0))

    blk = lambda: pl.BlockSpec((None, tr, c), lambda ll, i: (ll, i, 0))
    return pl.pallas_call(
        body, name=name, grid=(nl, nr),
        in_specs=[part_spec(l) for l in range(nl)] + [blk(), blk(), blk()],
        out_specs=[blk(), blk(), blk(), blk()],
        out_shape=[jax.ShapeDtypeStruct((nl, r, c), F32)] * 4,
        compiler_params=_params(("arbitrary", "arbitrary")),
    )(*parts, w, m, v)


REPLICATED = ("norm_g", "ret_norm_g", "gla_ba_f", "gla_ba_b", "gla_norm_g", "pool_w", "pool_scale",
              "mla_q_norm_g", "mla_kv_norm_g", "mla_qk_norm_q", "mla_qk_norm_k")
SMALL_SHARDED = ("mla_wq_b", "mla_wkv_b", "gla_wa2_f", "gla_wa2_b")
WEIGHTS = ("norm_g", "w_in", "ret_norm_g", "gla_wa2_f", "gla_ba_f", "gla_wa2_b", "gla_ba_b", "gla_norm_g", "pool_w",
           "pool_scale", "mla_q_norm_g", "mla_wq_b", "mla_kv_norm_g", "mla_wkv_b", "mla_qk_norm_q", "mla_qk_norm_k",
           "w_out")


def _pack(arrays, dtype):
    flat = jnp.concatenate([a.reshape(-1) for a in arrays]).astype(dtype)
    return flat.reshape(-1, LANES)


def _unpack(packed, like):
    flat = packed.reshape(-1)
    out, at = [], 0
    for a in like:
        out.append(flat[at:at + a.size].reshape(a.shape))
        at += a.size
    return out


def _columns_by_device(g):
    l, r, n = g.shape
    return g.reshape(l, r, N_DEV, n // N_DEV).transpose(2, 0, 1, 3)


def _gathered_columns(g, l, r, c):
    return g.reshape(N_DEV, l, r, c).transpose(1, 2, 0, 3).reshape(l, r, N_DEV * c)


def _layer_forward(x, wts, w_out_of, tables, tm, tq, ride_inproj=None, ride_attn=None):
    cos_r, sin_r, cos_m, sin_m, tab, _ = tables
    z, h, *carried_in = _inproj(x, wts["norm_g"], wts["w_in"], min(x.shape[0], 2 * tm), rider=ride_inproj)
    o_a, y_a = _ret_fwd(z, cos_r, sin_r, tab, wts["ret_norm_g"])
    o_b, y_b = _gla_fwd(z, wts["wa_f"], wts["wa_b"], wts["gla_ba_f"], wts["gla_ba_b"], wts["gla_norm_g"])
    y_c = _pool_fwd(z, wts["pool_w"], wts["pool_scale"])
    q, k, v = _mla_prep(z, cos_m, sin_m, wts["mla_q_norm_g"], wts["wq"], wts["mla_kv_norm_g"], wts["wkv"],
                        wts["qk_q"], wts["qk_k"], tm)
    o_d, y_d, lse, *carried_attn = _attn_fwd(q, k, v, z, tq, rider=ride_attn)
    y = jnp.concatenate([y_a, y_b, y_c, y_d], axis=1)
    w_out = w_out_of(carried_in)
    x_next = _mm(y, w_out, "nn", "outproj", tm, 2048, 1024, add=x)
    saved = dict(x=x, z=z, h=h, o_a=o_a, o_b=o_b, o_d=o_d, lse=lse, q=q, k=k, v=v, y=y, w_out=w_out)
    return x_next, saved, carried_in, carried_attn


def _layer_backward(dx, sv, wts, tables, tm, tq, ride_ret=None, ride_attn=None, ride_dh=None):
    cos_r, sin_r, cos_m, sin_m, tab, tab_sw = tables
    z = sv["z"]
    g = {}
    g["w_out"] = _mm(sv["y"], dx, "tn", "d_w_out", 2048, 1024, 512, out_dtype=BF16)
    dy = _mm(dx, sv["w_out"], "nt", "d_y", tm, 2048, 1024)

    do_a, dg_a, g["ret_norm_g"] = _normgate_bwd(sv["o_a"], z, A_G, dy, 0, wts["ret_norm_g"], tm)
    dq_a, dk_a, dv_a, *carried_ret = _ret_bwd(z, do_a, cos_r, sin_r, tab, tab_sw,
                                              rider=ride_ret(g) if ride_ret else None)

    do_b, dg_b, g["gla_norm_g"] = _normgate_bwd(sv["o_b"], z, B_G, dy, 1, wts["gla_norm_g"], tm)
    dq_b, dk_b, dv_b, d_ga, d_waf, d_wab, g["gla_ba_f"], g["gla_ba_b"] = _gla_bwd(
        z, do_b, wts["wa_f"], wts["wa_b"], wts["gla_ba_f"], wts["gla_ba_b"])
    g["gla_wa2_f"] = d_waf[0:GLA_RANK]
    g["gla_wa2_b"] = d_wab[GLA_RANK:2 * GLA_RANK]

    du_c, dg_c, g["pool_w"], g["pool_scale"] = _pool_bwd(z, dy, wts["pool_w"], wts["pool_scale"])

    d_q, d_k, d_v, dg_d, *carried_attn = _attn_bwd(sv["q"], sv["k"], sv["v"], z, sv["o_d"], sv["lse"], dy, tq,
                                                   rider=ride_attn(g) if ride_attn else None)
    (d_mq, d_mkv, d_mkr, d_wq, g["mla_wkv_b"], g["mla_q_norm_g"], g["mla_kv_norm_g"], d_qg, d_kg) = _mla_prep_bwd(
        z, cos_m, sin_m, wts["mla_q_norm_g"], wts["wq"], wts["mla_kv_norm_g"], wts["wkv"], wts["qk_q"], wts["qk_k"],
        d_q, d_k, d_v, tm)
    g["mla_wq_b"] = _unpad_wq(d_wq)
    g["mla_qk_norm_q"] = d_qg[:, _QK_INV]
    g["mla_qk_norm_k"] = d_kg[:, _QK_INV]

    dz = jnp.concatenate([dq_a, dk_a, dv_a, dg_a, dq_b, dk_b, dv_b, dg_b, d_mq, du_c, dg_c, dg_d, d_mkv,
                          d_ga.astype(BF16), d_mkr], axis=1)
    g["w_in"] = _split_w_in(_mm(sv["h"], dz, "tn", "d_w_in", 2048, 1024, 512, out_dtype=BF16))
    dh = _mm(dz, wts["w_in"], "nt", "d_h", 2 * tm, 2048, 1024, rider=ride_dh(g) if ride_dh else None)
    carried_dh = []
    if ride_dh:
        dh, *carried_dh = dh
    dx_in, g["norm_g"] = _norm_bwd(sv["x"], wts["norm_g"], dh, dx, tm)
    return dx_in, g, carried_ret + carried_attn + carried_dh


def kernel(x, norm_g, w_in, ret_norm_g, gla_wa2_f, gla_ba_f, gla_wa2_b, gla_ba_b, gla_norm_g, pool_w, pool_scale, mla_q_norm_g, mla_wq_b, mla_kv_norm_g, mla_wkv_b, mla_qk_norm_q, mla_qk_norm_k, w_out, loss_target, m_norm_g, m_w_in, m_ret_norm_g, m_gla_wa2_f, m_gla_ba_f, m_gla_wa2_b, m_gla_ba_b, m_gla_norm_g, m_pool_w, m_pool_scale, m_mla_q_norm_g, m_mla_wq_b, m_mla_kv_norm_g, m_mla_wkv_b, m_mla_qk_norm_q, m_mla_qk_norm_k, m_w_out, v_norm_g, v_w_in, v_ret_norm_g, v_gla_wa2_f, v_gla_ba_f, v_gla_wa2_b, v_gla_ba_b, v_gla_norm_g, v_pool_w, v_pool_scale, v_mla_q_norm_g, v_mla_wq_b, v_mla_kv_norm_g, v_mla_wkv_b, v_mla_qk_norm_q, v_mla_qk_norm_k, v_w_out):
    w = dict(norm_g=norm_g, w_in=w_in, ret_norm_g=ret_norm_g, gla_wa2_f=gla_wa2_f, gla_ba_f=gla_ba_f,
             gla_wa2_b=gla_wa2_b, gla_ba_b=gla_ba_b, gla_norm_g=gla_norm_g, pool_w=pool_w, pool_scale=pool_scale,
             mla_q_norm_g=mla_q_norm_g, mla_wq_b=mla_wq_b, mla_kv_norm_g=mla_kv_norm_g, mla_wkv_b=mla_wkv_b,
             mla_qk_norm_q=mla_qk_norm_q, mla_qk_norm_k=mla_qk_norm_k, w_out=w_out)
    m = dict(norm_g=m_norm_g, w_in=m_w_in, ret_norm_g=m_ret_norm_g, gla_wa2_f=m_gla_wa2_f, gla_ba_f=m_gla_ba_f,
             gla_wa2_b=m_gla_wa2_b, gla_ba_b=m_gla_ba_b, gla_norm_g=m_gla_norm_g, pool_w=m_pool_w,
             pool_scale=m_pool_scale, mla_q_norm_g=m_mla_q_norm_g, mla_wq_b=m_mla_wq_b, mla_kv_norm_g=m_mla_kv_norm_g,
             mla_wkv_b=m_mla_wkv_b, mla_qk_norm_q=m_mla_qk_norm_q, mla_qk_norm_k=m_mla_qk_norm_k, w_out=m_w_out)
    v = dict(norm_g=v_norm_g, w_in=v_w_in, ret_norm_g=v_ret_norm_g, gla_wa2_f=v_gla_wa2_f, gla_ba_f=v_gla_ba_f,
             gla_wa2_b=v_gla_wa2_b, gla_ba_b=v_gla_ba_b, gla_norm_g=v_gla_norm_g, pool_w=v_pool_w,
             pool_scale=v_pool_scale, mla_q_norm_g=v_mla_q_norm_g, mla_wq_b=v_mla_wq_b, mla_kv_norm_g=v_mla_kv_norm_g,
             mla_wkv_b=v_mla_wkv_b, mla_qk_norm_q=v_mla_qk_norm_q, mla_qk_norm_k=v_mla_qk_norm_k, w_out=v_w_out)
    xs, target = x[0], loss_target[0]
    s = xs.shape[0]
    tm, tq = min(s, 512), min(s, 256)
    c_in = w_in.shape[2]

    w_in_b = w_in.astype(BF16)
    w_out_b = w_out.astype(BF16).reshape(-1, D_MODEL)
    w_in_g0, small_g = _exchange([("gather", w_in_b[0]), ("gather", _pack([w[n] for n in SMALL_SHARDED], BF16))],
                                 "gather_first")
    sizes = [w[n].size for n in SMALL_SHARDED]
    offs = np.cumsum([0] + sizes)
    small_flat = small_g.reshape(N_DEV, -1)
    small_full = {n: _gathered_columns(small_flat[:, offs[i]:offs[i + 1]], *w[n].shape)
                  for i, n in enumerate(SMALL_SHARDED)}
    tables = _rope_tables(s) + _ret_tables()

    def layer_weights(l, w_in_g):
        wa_f = jnp.zeros((LANES, 2 * LANES), BF16).at[0:GLA_RANK].set(small_full["gla_wa2_f"][l])
        wa_b = jnp.zeros((LANES, 2 * LANES), BF16).at[GLA_RANK:2 * GLA_RANK].set(small_full["gla_wa2_b"][l])
        return dict(
            norm_g=norm_g[l][None], w_in=_assemble_w_in(w_in_g), ret_norm_g=ret_norm_g[l][None],
            wa_f=wa_f, wa_b=wa_b, gla_ba_f=gla_ba_f[l][None], gla_ba_b=gla_ba_b[l][None],
            gla_norm_g=gla_norm_g[l][None], pool_w=pool_w[l], pool_scale=pool_scale[l][None],
            mla_q_norm_g=mla_q_norm_g[l][None], wq=_pad_wq(small_full["mla_wq_b"][l]),
            mla_kv_norm_g=mla_kv_norm_g[l][None], wkv=small_full["mla_wkv_b"][l],
            qk_q=_pad_qk_gain(mla_qk_norm_q[l]), qk_k=_pad_qk_gain(mla_qk_norm_k[l]))

    def w_out_layer(w_out_g, l):
        return w_out_g.reshape(N_DEV, DEPTH, -1, D_MODEL)[:, l].reshape(-1, D_MODEL)

    by_owner = lambda g_w_out: g_w_out.reshape(N_DEV, -1, D_MODEL)

    layers = [layer_weights(0, w_in_g0), None]
    x1, sv0, (w_out_g,), (w_in_g1,) = _layer_forward(
        xs, layers[0], lambda got: w_out_layer(got[0], 0), tables, tm, tq,
        ride_inproj=("gather", w_out_b), ride_attn=("gather", w_in_b[1]))
    layers[1] = layer_weights(1, w_in_g1)
    x2, sv1, _, _ = _layer_forward(x1, layers[1], lambda got: w_out_layer(w_out_g, 1), tables, tm, tq)
    dx, loss_row = _loss_head(x2, target, tm)
    loss = lax.psum(loss_row[0, 0], ("x", "y", "c"))

    dx, g1, (out_parts1,) = _layer_backward(
        dx, sv1, layers[1], tables, tm, tq, ride_attn=lambda g: ("scatter", by_owner(g["w_out"])))
    dx, g0, (out_parts0, in_parts1, in_parts0) = _layer_backward(
        dx, sv0, layers[0], tables, tm, tq, ride_ret=lambda g: ("scatter", by_owner(g["w_out"])),
        ride_attn=lambda g: ("scatter", g1["w_in"]), ride_dh=lambda g: ("scatter", g["w_in"]))
    grads = (g0, g1)
    full = {n: jnp.stack([grads[l][n].reshape(w[n].shape[1:]) if n in REPLICATED else grads[l][n]
                          for l in range(DEPTH)]) for n in SMALL_SHARDED + REPLICATED}
    small_c = jnp.concatenate([_columns_by_device(full[n]).reshape(N_DEV, -1) for n in SMALL_SHARDED], axis=1)
    small_parts, rep_parts = _exchange([("scatter", small_c.reshape(N_DEV, -1, LANES)),
                                        ("gather", _pack([full[n] for n in REPLICATED], F32))], "exchange_last")

    out = {}
    out["w_in"] = _adam_layers((in_parts0, in_parts1), w_in, m_w_in, v_w_in, "adam_w_in", 256)
    out["w_out"] = _adam_layers((out_parts0, out_parts1), w_out, m_w_out, v_w_out, "adam_w_out", 128)
    for names, parts, label in ((SMALL_SHARDED, small_parts, "adam_small"), (REPLICATED, rep_parts, "adam_replicated")):
        res = _adam(parts, _pack([w[n] for n in names], F32), _pack([m[n] for n in names], F32),
                    _pack([v[n] for n in names], F32), label, 2048)
        for n, *vals in zip(names, *[_unpack(a, [w[n] for n in names]) for a in res]):
            out[n] = vals

    return (loss, dx[None], *[out[n][0] for n in WEIGHTS], *[out[n][1] for n in WEIGHTS],
            *[out[n][2] for n in WEIGHTS], *[out[n][3] for n in WEIGHTS])
```

```python
import functools
import math

import numpy as np
import jax
import jax.numpy as jnp
from jax import lax
from jax.experimental import pallas as pl
from jax.experimental.pallas import tpu as pltpu

F32 = jnp.float32
BF16 = jnp.bfloat16

N_DEV = 8
D_MODEL = 2048
DEPTH = 2
GROUP_W = 512
EPS = 1e-6
ROPE_THETA = 10000.0
LANES = 128

RET_HD = 128
RET_CHUNK = 256
RET_UNROLL = 2
GLA_CHUNK = 64
GLA_UNROLL = 4
GLA_CUM_ROWS = 256
GLA_DK = 64
GLA_TAU = 16.0
GLA_RANK = 16
POOL_WINDOWS = (2, 4, 8, 16)
MLA_QK = 192
MLA_ROPE = 64
ATTN_SCALE = MLA_QK ** -0.5
ATTN_Q_SCALE = ATTN_SCALE * math.log2(math.e)
IN_COLS = 5984

ADAM_LR = 0.001
ADAM_B1 = 0.9
ADAM_B2 = 0.999
ADAM_EPS = 1e-08
ADAM_WD = 0.01
ADAM_STEP = 10

A_Q, A_K, A_V, A_G = 0, 4, 8, 12
B_Q, B_K, B_V, B_G = 16, 18, 20, 24
M_Q, C_V, C_G, M_G = 28, 32, 36, 40
M_KV, GA, M_KR = 44, 46, 47
ZP_COLS = 48 * LANES

VMEM_LIMIT = 56 * 1024 * 1024


def _params(sem, vmem=VMEM_LIMIT):
    return pltpu.CompilerParams(dimension_semantics=sem, vmem_limit_bytes=vmem)


def _sigmoid(x):
    return 1.0 / (1.0 + jnp.exp(-x))


def _silu(x):
    return x * _sigmoid(x)


def _silu_grad(x):
    s = _sigmoid(x)
    return s * (1.0 + x * (1.0 - s))


def _dot(a, b, dims=(((1,), (0,)), ((), ()))):
    return lax.dot_general(a, b, dims, preferred_element_type=F32)


NT = (((1,), (1,)), ((), ()))
TN = (((0,), (0,)), ((), ()))


def _chunk_loop(n, body, init, unroll):
    unroll = math.gcd(n, unroll)

    def trip(t, carry):
        for u in range(unroll):
            carry = body(t * unroll + u, carry)
        return carry

    return lax.fori_loop(0, n // unroll, trip, init)


def _roll_lanes_half(x):
    return pltpu.roll(x, 64, 1)


def _wq_perm():
    idx = np.zeros((1024,), np.int32)
    ok = np.zeros((1024,), bool)
    for h in range(4):
        idx[128 * h:128 * h + 128] = 192 * h + np.arange(128)
        ok[128 * h:128 * h + 128] = True
        base = 512 + 128 * h
        idx[base:base + 32] = 192 * h + 128 + np.arange(32)
        ok[base:base + 32] = True
        idx[base + 64:base + 96] = 192 * h + 160 + np.arange(32)
        ok[base + 64:base + 96] = True
    inv = np.zeros((768,), np.int32)
    inv[idx[ok]] = np.nonzero(ok)[0]
    return idx, ok, inv


_WQ_IDX, _WQ_OK, _WQ_INV = _wq_perm()


def _pad_wq(wq):
    return jnp.where(jnp.asarray(_WQ_OK)[None, :], wq[:, _WQ_IDX], 0).astype(wq.dtype)


def _unpad_wq(wqp):
    return wqp[:, _WQ_INV]


def _qk_idx():
    idx = np.zeros((256,), np.int32)
    ok = np.zeros((256,), bool)
    idx[0:128] = np.arange(128)
    ok[0:128] = True
    idx[128:160] = 128 + np.arange(32)
    ok[128:160] = True
    idx[192:224] = 160 + np.arange(32)
    ok[192:224] = True
    inv = np.zeros((192,), np.int32)
    inv[idx[ok]] = np.nonzero(ok)[0]
    return idx, ok, inv


_QK_IDX, _QK_OK, _QK_INV = _qk_idx()


def _pad_qk_gain(g):
    return jnp.where(jnp.asarray(_QK_OK), g[_QK_IDX], 0.0).reshape(1, 256)


def _rope_tables(s):
    def tabs(dim):
        inv = 1.0 / (ROPE_THETA ** (jnp.arange(0, dim, 2, dtype=F32) / dim))
        ang = jnp.arange(s, dtype=F32)[:, None] * inv[None, :]
        return jnp.cos(ang), jnp.sin(ang)
    cr, sr = tabs(RET_HD)
    cos_r = jnp.concatenate([cr, cr], axis=1)
    sin_r = jnp.concatenate([-sr, sr], axis=1)
    cm, sm = tabs(MLA_ROPE)
    zz = jnp.zeros_like(cm)
    cos_m = jnp.concatenate([cm, zz, cm, zz], axis=1)
    sin_m = jnp.concatenate([-sm, zz, sm, zz], axis=1)
    return cos_r, sin_r, cos_m, sin_m


def _rope(x, cos, sin):
    return x * cos + _roll_lanes_half(x) * sin


def _rope_t(x, cos, sin):
    return x * cos + _roll_lanes_half(x * sin)


def _ret_tables():
    c = RET_CHUNK
    gamma_f = 1.0 - 2.0 ** (-5.0 - jnp.arange(4, dtype=F32))
    gamma_b = gamma_f[::-1]
    idx = jnp.arange(c, dtype=F32)
    diff = idx[:, None] - idx[None, :]

    def build(g1, g2):
        l1 = jnp.log(g1)[:, None, None]
        l2 = jnp.log(g2)[:, None, None]
        d1 = jnp.where(diff >= 0, jnp.exp(jnp.maximum(diff, 0.0)[None] * l1), 0.0)
        d2 = jnp.where(diff <= 0, jnp.exp(jnp.maximum(-diff, 0.0)[None] * l2), 0.0)
        ones = jnp.ones((1, c, LANES), F32)
        col = idx[None, :, None]
        qdf = jnp.exp((col + 1.0) * l1) * ones
        kdf = jnp.exp((c - 1.0 - col) * l1) * ones
        qdb = jnp.exp((c - col) * l2) * ones
        kdb = jnp.exp(col * l2) * ones
        cd1 = jnp.exp(c * l1) * ones
        cd2 = jnp.exp(c * l2) * ones
        return jnp.concatenate([d1 + d2, qdf, kdf, qdb, kdb, cd1, cd2], axis=2)

    return build(gamma_f, gamma_b), build(gamma_b, gamma_f)


MESH = pl.DeviceIdType.MESH
ANY = pl.BlockSpec(memory_space=pl.ANY)
_RELATIONS = ((0, 0, 1), (1, 0, 0), (0, 1, 0), (1, 1, 0), (1, 0, 1), (0, 1, 1), (1, 1, 1))


def _position():
    return lax.axis_index("x"), lax.axis_index("y"), lax.axis_index("c")


def _gather_copies(x_ref, out_ref, send_sems, recv_sems, local_sem, starting):
    x, y, cc = _position()
    me, sibling = (x, y, cc), (x, y, 1 - cc)
    chips = [(1 - x, y), (x, 1 - y), (1 - x, 1 - y)]

    def slab(px, py, pc):
        return out_ref.at[4 * px + 2 * py + pc]

    def copy(k, block, to, src=None):
        return pltpu.make_async_remote_copy(
            src_ref=slab(*block) if src is None else src, dst_ref=slab(*block),
            send_sem=send_sems.at[k], recv_sem=recv_sems.at[k], device_id=to, device_id_type=MESH)

    mine = pltpu.make_async_copy(x_ref, slab(*me), local_sem)
    first = [copy(0, me, sibling, src=x_ref)] + [copy(1 + j, me, (*chip, cc), src=x_ref) for j, chip in enumerate(chips)]
    if starting:
        return mine, first
    passed = [copy(4 + j, (*chip, cc), sibling) for j, chip in enumerate(chips)]
    arrivals = [copy(1 + j, (*chip, cc), me) for j, chip in enumerate(chips)]
    late = [copy(0, sibling, me)] + [copy(4 + j, (*chip, 1 - cc), me) for j, chip in enumerate(chips)]
    return mine, first, passed, arrivals, late


def _gather_start(*refs):
    mine, first = _gather_copies(*refs, starting=True)
    mine.start()
    for cp in first:
        cp.start()


def _gather_finish(*refs):
    mine, first, passed, arrivals, late = _gather_copies(*refs, starting=False)
    for arrived, onward in zip(arrivals, passed):
        arrived.wait_recv()
        onward.start()
    for cp in late:
        cp.wait_recv()
    for cp in first + passed:
        cp.wait_send()
    mine.wait()


def _scatter_copies(c_ref, out_ref, send_sems, recv_sems, local_sem):
    x, y, cc = _position()
    me = 4 * x + 2 * y + cc
    mine = pltpu.make_async_copy(c_ref.at[me], out_ref.at[me], local_sem)
    copies = []
    for k, (fx, fy, fc) in enumerate(_RELATIONS):
        px = 1 - x if fx else x
        py = 1 - y if fy else y
        pc = 1 - cc if fc else cc
        copies.append(pltpu.make_async_remote_copy(
            src_ref=c_ref.at[4 * px + 2 * py + pc], dst_ref=out_ref.at[me],
            send_sem=send_sems.at[k], recv_sem=recv_sems.at[k], device_id=(px, py, pc), device_id_type=MESH))
    return mine, copies


def _scatter_start(*refs):
    mine, copies = _scatter_copies(*refs)
    mine.start()
    for cp in copies:
        cp.start()


def _scatter_finish(*refs):
    mine, copies = _scatter_copies(*refs)
    for cp in copies:
        cp.wait()
    mine.wait()


_EXCHANGES = {"gather": (_gather_start, _gather_finish), "scatter": (_scatter_start, _scatter_finish)}


def _exchange_scratch():
    return [pltpu.SemaphoreType.DMA((7,)), pltpu.SemaphoreType.DMA((7,)), pltpu.SemaphoreType.DMA]


def _exchange_out(kind, src):
    return jax.ShapeDtypeStruct(((N_DEV,) + src.shape) if kind == "gather" else src.shape, src.dtype)


def _exchange(jobs, name):
    n = len(jobs)

    def body(*refs):
        srcs, outs, sems = refs[:n], refs[n:2 * n], refs[2 * n:]
        for half in (0, 1):
            for i, (kind, _) in enumerate(jobs):
                _EXCHANGES[kind][half](srcs[i], outs[i], *sems[3 * i:3 * i + 3])

    return pl.pallas_call(
        body, name=name, out_shape=[_exchange_out(kind, src) for kind, src in jobs],
        in_specs=[ANY] * n, out_specs=[ANY] * n,
        scratch_shapes=[sem for _ in jobs for sem in _exchange_scratch()])(*[src for _, src in jobs])


def _call(body, name, grid, in_specs, out_specs, out_shape, scratch, sem, args, rider=None):
    if rider is None:
        return pl.pallas_call(body, name=name, grid=grid, in_specs=in_specs, out_specs=out_specs, out_shape=out_shape,
                              scratch_shapes=scratch, compiler_params=_params(sem))(*args)
    kind, src = rider
    start, finish = _EXCHANGES[kind]
    ni, no, ns = len(in_specs), len(out_specs), len(scratch)

    def carried(*refs):
        ins, rsrc = refs[:ni], refs[ni]
        outs, rout = refs[ni + 1:ni + 1 + no], refs[ni + 1 + no]
        scr, sems = refs[ni + 2 + no:ni + 2 + no + ns], refs[ni + 2 + no + ns:]
        ids = [pl.program_id(a) for a in range(len(grid))]
        is_first = functools.reduce(jnp.logical_and, [i == 0 for i in ids])
        is_last = functools.reduce(jnp.logical_and, [i == g - 1 for i, g in zip(ids, grid)])

        @pl.when(is_first)
        def _():
            start(rsrc, rout, *sems)

        body(*ins, *outs, *scr)

        @pl.when(is_last)
        def _():
            finish(rsrc, rout, *sems)

    return pl.pallas_call(
        carried, name=name, grid=grid, in_specs=list(in_specs) + [ANY], out_specs=list(out_specs) + [ANY],
        out_shape=list(out_shape) + [_exchange_out(kind, src)], scratch_shapes=list(scratch) + _exchange_scratch(),
        compiler_params=_params(("arbitrary",) * len(grid)))(*args, src)


def _inproj(x, g, wp, tm, tn=512, rider=None):
    s, d = x.shape
    n = wp.shape[1]

    def body(x_ref, g_ref, w_ref, z_ref, h_ref, hs):
        @pl.when(pl.program_id(1) == 0)
        def _():
            xv = x_ref[...]
            r = lax.rsqrt(jnp.mean(xv * xv, axis=-1, keepdims=True) + EPS)
            hv = (xv * r * g_ref[...]).astype(BF16)
            hs[...] = hv
            h_ref[...] = hv
        z_ref[...] = _dot(hs[...], w_ref[...])

    return _call(
        body, "inproj", (s // tm, n // tn),
        [pl.BlockSpec((tm, d), lambda i, j: (i, 0)),
         pl.BlockSpec((1, d), lambda i, j: (0, 0)),
         pl.BlockSpec((d, tn), lambda i, j: (0, j))],
        [pl.BlockSpec((tm, tn), lambda i, j: (i, j)), pl.BlockSpec((tm, d), lambda i, j: (i, 0))],
        [jax.ShapeDtypeStruct((s, n), F32), jax.ShapeDtypeStruct((s, d), BF16)],
        [pltpu.VMEM((tm, d), BF16)], ("parallel", "arbitrary"), (x, g, wp), rider)


def _relayout_plan():
    runs = ((0, 3584, 0), (3584, 3616, GA * LANES), (3616, 4640, C_V * LANES), (4640, 5152, M_Q * LANES),
            (5152, 5408, M_KV * LANES), (5408, 5440, M_KR * LANES), (5440, 5472, M_KR * LANES + 64),
            (5472, 5984, M_G * LANES))
    shard = IN_COLS // N_DEV
    plan = []
    for d in range(N_DEV):
        lo, hi = shard * d, shard * (d + 1)
        for a, b, p in runs:
            s, e = max(a, lo), min(b, hi)
            if s < e:
                plan.append((d, s - lo, p + (s - a), e - s))
    return plan


def _assemble_w_in(g, tr=256):
    _, r, c = g.shape

    def body(g_ref, o_ref):
        o_ref[...] = jnp.zeros_like(o_ref)
        for d, at, to, w in _relayout_plan():
            o_ref[:, to:to + w] = g_ref[d, :, at:at + w]

    return pl.pallas_call(
        body, name="assemble_w_in", grid=(r // tr,),
        in_specs=[pl.BlockSpec((N_DEV, tr, c), lambda i: (0, i, 0))],
        out_specs=pl.BlockSpec((tr, ZP_COLS), lambda i: (i, 0)),
        out_shape=jax.ShapeDtypeStruct((r, ZP_COLS), g.dtype),
        compiler_params=_params(("parallel",)),
    )(g)


def _split_w_in(wp, tr=256):
    r = wp.shape[0]
    c = IN_COLS // N_DEV

    def body(w_ref, o_ref):
        for d, at, to, w in _relayout_plan():
            o_ref[d, :, at:at + w] = w_ref[:, to:to + w]

    return pl.pallas_call(
        body, name="split_w_in", grid=(r // tr,),
        in_specs=[pl.BlockSpec((tr, ZP_COLS), lambda i: (i, 0))],
        out_specs=pl.BlockSpec((N_DEV, tr, c), lambda i: (0, i, 0)),
        out_shape=jax.ShapeDtypeStruct((N_DEV, r, c), wp.dtype),
        compiler_params=_params(("parallel",)),
    )(wp)


def _mm(a, b, mode, name, tm, tn, tk, add=None, out_dtype=F32, rider=None):
    if mode == "tn":
        k, m = a.shape
    else:
        m, k = a.shape
    n = b.shape[0] if mode == "nt" else b.shape[1]
    tm, tn, tk = min(tm, m), min(tn, n), min(tk, k)
    nk = k // tk
    dims = {"nn": (((1,), (0,)), ((), ())), "nt": NT, "tn": TN}[mode]

    def body(*refs):
        if add is None:
            a_ref, b_ref, o_ref, acc = refs
        else:
            a_ref, b_ref, add_ref, o_ref, acc = refs
        kk = pl.program_id(2)

        @pl.when(kk == 0)
        def _():
            acc[...] = jnp.zeros_like(acc)

        acc[...] += _dot(a_ref[...].astype(BF16), b_ref[...].astype(BF16), dims)

        @pl.when(kk == nk - 1)
        def _():
            r = acc[...]
            if add is not None:
                r = r + add_ref[...]
            o_ref[...] = r.astype(out_dtype)

    a_spec = (pl.BlockSpec((tk, tm), lambda i, j, kk: (kk, i)) if mode == "tn"
              else pl.BlockSpec((tm, tk), lambda i, j, kk: (i, kk)))
    b_spec = (pl.BlockSpec((tn, tk), lambda i, j, kk: (j, kk)) if mode == "nt"
              else pl.BlockSpec((tk, tn), lambda i, j, kk: (kk, j)))
    in_specs = [a_spec, b_spec]
    args = [a, b]
    if add is not None:
        in_specs.append(pl.BlockSpec((tm, tn), lambda i, j, kk: (i, j)))
        args.append(add)
    res = _call(body, name, (m // tm, n // tn, nk), in_specs,
                [pl.BlockSpec((tm, tn), lambda i, j, kk: (i, j))], [jax.ShapeDtypeStruct((m, n), out_dtype)],
                [pltpu.VMEM((tm, tn), F32)], ("parallel", "parallel", "arbitrary"), args, rider)
    return res[0] if rider is None else res


def _ret_core(q_ref, k_ref, v_ref, tab_ref, out_ref, nchunk):
    c = RET_CHUNK

    def rows(n):
        return pl.ds(pl.multiple_of(n * c, c), c)

    zero = jnp.zeros((LANES, LANES), F32)

    def plane(i, n=c):
        return tab_ref[0:n, c + LANES * i:c + LANES * (i + 1)]

    def fwd(n, st):
        r = rows(n)
        q, k, vb = q_ref[r, :], k_ref[r, :], v_ref[r, :].astype(BF16)
        sc = _dot(q.astype(BF16), k.astype(BF16), NT) * tab_ref[:, 0:c]
        o = _dot(sc.astype(BF16), vb)
        o = o + _dot((q * plane(0)).astype(BF16), st.astype(BF16))
        out_ref[r, :] = o
        return st * plane(4, LANES) + _dot((k * plane(1)).astype(BF16), vb, TN)

    _chunk_loop(nchunk, fwd, zero, RET_UNROLL)

    def bwd(i, st):
        r = rows(nchunk - 1 - i)
        q, k, vb = q_ref[r, :], k_ref[r, :], v_ref[r, :].astype(BF16)
        out_ref[r, :] += _dot((q * plane(2)).astype(BF16), st.astype(BF16))
        return st * plane(5, LANES) + _dot((k * plane(3)).astype(BF16), vb, TN)

    _chunk_loop(nchunk, bwd, zero, RET_UNROLL)


def _ret_fwd(z, cos_r, sin_r, tab, norm_g):
    s = z.shape[0]
    nchunk = s // RET_CHUNK
    scale = RET_HD ** -0.5
    col = lambda base: pl.BlockSpec((s, LANES), lambda h: (0, base + h), pipeline_mode=pl.Buffered(1))

    def body(q_ref, k_ref, v_ref, g_ref, cos_ref, sin_ref, tab_ref, ng_ref, o_ref, y_ref, qh, kh):
        qh[...] = _rope(q_ref[...], cos_ref[...], sin_ref[...])
        kh[...] = _rope(k_ref[...], cos_ref[...], sin_ref[...]) * scale
        _ret_core(qh, kh, v_ref, tab_ref, o_ref, nchunk)
        o = o_ref[...]
        r = lax.rsqrt(jnp.mean(o * o, axis=-1, keepdims=True) + EPS)
        y_ref[...] = (_silu(g_ref[...]) * (o * r * ng_ref[...])).astype(BF16)

    return pl.pallas_call(
        body, name="ret_fwd", grid=(4,),
        in_specs=[col(A_Q), col(A_K), col(A_V), col(A_G),
                  pl.BlockSpec((s, LANES), lambda h: (0, 0), pipeline_mode=pl.Buffered(1)),
                  pl.BlockSpec((s, LANES), lambda h: (0, 0), pipeline_mode=pl.Buffered(1)),
                  pl.BlockSpec((None, RET_CHUNK, RET_CHUNK + 6 * LANES), lambda h: (h, 0, 0)),
                  pl.BlockSpec((1, LANES), lambda h: (0, h))],
        out_specs=[pl.BlockSpec((s, LANES), lambda h: (0, h)), pl.BlockSpec((s, LANES), lambda h: (0, h))],
        out_shape=[jax.ShapeDtypeStruct((s, GROUP_W), F32), jax.ShapeDtypeStruct((s, GROUP_W), BF16)],
        scratch_shapes=[pltpu.VMEM((s, LANES), F32), pltpu.VMEM((s, LANES), F32)],
        compiler_params=_params(("arbitrary",)),
    )(z, z, z, z, cos_r, sin_r, tab, norm_g)


def _ret_bwd(z, d_o, cos_r, sin_r, tab, tab_sw, rider=None):
    s = z.shape[0]
    nchunk = s // RET_CHUNK
    scale = RET_HD ** -0.5
    col = lambda base: pl.BlockSpec((s, LANES), lambda h: (0, base + h), pipeline_mode=pl.Buffered(1))
    whole = lambda: pl.BlockSpec((s, LANES), lambda h: (0, 0), pipeline_mode=pl.Buffered(1))
    tabspec = lambda: pl.BlockSpec((None, RET_CHUNK, RET_CHUNK + 6 * LANES), lambda h: (h, 0, 0))
    outspec = lambda: pl.BlockSpec((s, LANES), lambda h: (0, h))

    def body(q_ref, k_ref, v_ref, do_ref, cos_ref, sin_ref, tab_ref, tsw_ref, dq_ref, dk_ref, dv_ref,
             qh, kh, tmp):
        cos, sin = cos_ref[...], sin_ref[...]
        qh[...] = _rope(q_ref[...], cos, sin)
        kh[...] = _rope(k_ref[...], cos, sin) * scale
        _ret_core(kh, qh, do_ref, tsw_ref, tmp, nchunk)
        dv_ref[...] = tmp[...].astype(BF16)
        _ret_core(do_ref, v_ref, kh, tab_ref, tmp, nchunk)
        dq_ref[...] = _rope_t(tmp[...], cos, sin).astype(BF16)
        _ret_core(v_ref, do_ref, qh, tsw_ref, tmp, nchunk)
        dk_ref[...] = _rope_t(tmp[...] * scale, cos, sin).astype(BF16)

    return _call(
        body, "ret_bwd", (4,),
        [col(A_Q), col(A_K), col(A_V),
         pl.BlockSpec((s, LANES), lambda h: (0, h), pipeline_mode=pl.Buffered(1)),
         whole(), whole(), tabspec(), tabspec()],
        [outspec(), outspec(), outspec()],
        [jax.ShapeDtypeStruct((s, GROUP_W), BF16)] * 3,
        [pltpu.VMEM((s, LANES), F32), pltpu.VMEM((s, LANES), F32), pltpu.VMEM((s, LANES), F32)],
        ("arbitrary",), (z, z, z, d_o, cos_r, sin_r, tab, tab_sw), rider)


def _normgate_bwd(o, z, gate_blk, dy, dy_blk, norm_g, tm):
    s = o.shape[0]

    def body(o_ref, g_ref, dy_ref, ng_ref, do_ref, dg_ref, dng_ref):
        @pl.when(pl.program_id(0) == 0)
        def _():
            dng_ref[...] = jnp.zeros_like(dng_ref)

        for h in range(4):
            sl = slice(LANES * h, LANES * (h + 1))
            ov, gv, dyv, ng = o_ref[:, sl], g_ref[:, sl], dy_ref[:, sl], ng_ref[:, sl]
            r = lax.rsqrt(jnp.mean(ov * ov, axis=-1, keepdims=True) + EPS)
            on = ov * r
            dn = dyv * _silu(gv)
            u = dn * ng
            do_ref[:, sl] = r * (u - on * jnp.mean(u * on, axis=-1, keepdims=True))
            dg_ref[:, sl] = (dyv * (on * ng) * _silu_grad(gv)).astype(BF16)
            dng_ref[:, sl] += jnp.sum(dn * on, axis=0, keepdims=True)

    return pl.pallas_call(
        body, name="normgate_bwd", grid=(s // tm,),
        in_specs=[pl.BlockSpec((tm, GROUP_W), lambda i: (i, 0)),
                  pl.BlockSpec((tm, GROUP_W), lambda i: (i, gate_blk // 4)),
                  pl.BlockSpec((tm, GROUP_W), lambda i: (i, dy_blk)),
                  pl.BlockSpec((1, GROUP_W), lambda i: (0, 0))],
        out_specs=[pl.BlockSpec((tm, GROUP_W), lambda i: (i, 0)), pl.BlockSpec((tm, GROUP_W), lambda i: (i, 0)),
                   pl.BlockSpec((1, GROUP_W), lambda i: (0, 0))],
        out_shape=[jax.ShapeDtypeStruct((s, GROUP_W), F32), jax.ShapeDtypeStruct((s, GROUP_W), BF16),
                   jax.ShapeDtypeStruct((1, GROUP_W), F32)],
        compiler_params=_params(("arbitrary",)),
    )(o, z, dy, norm_g)


def _log_sigmoid(x):
    return jnp.minimum(x, 0.0) - jnp.log(1.0 + jnp.exp(-jnp.abs(x)))


def _gla_consts():
    c = GLA_CHUNK
    row = lax.broadcasted_iota(jnp.int32, (c, c), 0)
    colm = lax.broadcasted_iota(jnp.int32, (c, c), 1)
    lane = lax.broadcasted_iota(jnp.int32, (1, LANES), 1)
    low = row >= colm
    up = colm >= row
    heads = ((lane < GLA_DK).astype(F32), (lane >= GLA_DK).astype(F32))
    return low, up, heads


def _chunk_tri(upper):
    r = lax.broadcasted_iota(jnp.int32, (GLA_CUM_ROWS, GLA_CUM_ROWS), 0)
    c = lax.broadcasted_iota(jnp.int32, (GLA_CUM_ROWS, GLA_CUM_ROWS), 1)
    shift = GLA_CHUNK.bit_length() - 1
    same = jnp.right_shift(r, shift) == jnp.right_shift(c, shift)
    return jnp.where(jnp.logical_and(same, (c >= r) if upper else (r >= c)), 1.0, 0.0).astype(BF16)


def _exact_tri_matmul(tri, x):
    hi = x.astype(BF16)
    rest = x - hi.astype(F32)
    mid = rest.astype(BF16)
    lo = (rest - mid.astype(F32)).astype(BF16)
    return _dot(tri, hi) + _dot(tri, mid) + _dot(tri, lo)


def _gla_chunk(cum_ref, d, n):
    c = GLA_CHUNK
    cum = cum_ref[d, pl.ds(pl.multiple_of(n * c, c), c), :]
    last = cum_ref[d, pl.ds(n * c + (c - 1 if d == 0 else 0), 1), :]
    eq = jnp.exp(cum)
    ek = jnp.exp(-cum)
    el = jnp.exp(last - cum)
    dec = jnp.exp(last)
    return eq, ek, el, dec


def _gla_gates(ga_ref, wa_ref, ba_ref, cum_ref, s, upper):
    tri = _chunk_tri(upper)
    rows = min(s, GLA_CUM_ROWS)

    def step(i, carry):
        r = pl.ds(pl.multiple_of(i * rows, rows), rows)
        pre = _dot(ga_ref[r, :].astype(BF16), wa_ref[...].astype(BF16)) + ba_ref[...]
        cum_ref[r, :] = _exact_tri_matmul(tri[0:rows, 0:rows], _log_sigmoid(pre) * (1.0 / GLA_TAU))
        return carry
    lax.fori_loop(0, s // rows, step, 0)


def _gla_fwd(z, wa_f, wa_b, ba_f, ba_b, norm_g):
    s = z.shape[0]
    c = GLA_CHUNK
    nchunk = s // c
    scale = GLA_DK ** -0.5
    tm = min(s, 512)
    one = pl.Buffered(1)

    def body(q_ref, k_ref, v_ref, ga_ref, g_ref, waf_ref, wab_ref, baf_ref, bab_ref, ng_ref, o_ref, y_ref,
             la_s):
        low, up, heads = _gla_consts()
        _gla_gates(ga_ref, waf_ref, baf_ref, la_s.at[0], s, False)
        _gla_gates(ga_ref, wab_ref, bab_ref, la_s.at[1], s, True)
        for d in range(2):
            tri = (low, up)[d]

            def step(i, states):
                n = i if d == 0 else nchunk - 1 - i
                r = pl.ds(pl.multiple_of(n * c, c), c)
                q = q_ref[r, :] * scale
                k = k_ref[r, :]
                eq, ek, el, dec = _gla_chunk(la_s, d, n)
                qt = q * eq
                ktb = (k * ek).astype(BF16)
                kl = k * el
                new_states = []
                for hh in range(2):
                    cols = slice(LANES * hh, LANES * (hh + 1))
                    vb = v_ref[r, cols].astype(BF16)
                    qm = (qt * heads[hh]).astype(BF16)
                    a = jnp.where(tri, _dot(qm, ktb, NT), 0.0)
                    o = _dot(a.astype(BF16), vb) + _dot(qm, states[hh].astype(BF16), NT)
                    if d == 0:
                        o_ref[r, cols] = o
                    else:
                        o_ref[r, cols] += o
                    new_states.append(states[hh] * dec + _dot(vb, (kl * heads[hh]).astype(BF16), TN))
                return tuple(new_states)

            zero = jnp.zeros((LANES, LANES), F32)
            _chunk_loop(nchunk, step, (zero, zero), GLA_UNROLL)

        def epi(i, carry):
            r = pl.ds(pl.multiple_of(i * tm, tm), tm)
            for hh in range(2):
                cols = slice(LANES * hh, LANES * (hh + 1))
                o = o_ref[r, cols]
                rr = lax.rsqrt(jnp.mean(o * o, axis=-1, keepdims=True) + EPS)
                y_ref[r, cols] = (_silu(g_ref[r, cols]) * (o * rr * ng_ref[:, cols])).astype(BF16)
            return carry

        lax.fori_loop(0, s // tm, epi, 0)

    w2 = 2 * LANES
    return pl.pallas_call(
        body, name="gla_fwd", grid=(2,),
        in_specs=[pl.BlockSpec((s, LANES), lambda p: (0, B_Q + p), pipeline_mode=one),
                  pl.BlockSpec((s, LANES), lambda p: (0, B_K + p), pipeline_mode=one),
                  pl.BlockSpec((s, w2), lambda p: (0, B_V // 2 + p), pipeline_mode=one),
                  pl.BlockSpec((s, LANES), lambda p: (0, GA), pipeline_mode=one),
                  pl.BlockSpec((s, w2), lambda p: (0, B_G // 2 + p), pipeline_mode=one),
                  pl.BlockSpec((LANES, LANES), lambda p: (0, p)),
                  pl.BlockSpec((LANES, LANES), lambda p: (0, p)),
                  pl.BlockSpec((1, LANES), lambda p: (0, p)),
                  pl.BlockSpec((1, LANES), lambda p: (0, p)),
                  pl.BlockSpec((1, w2), lambda p: (0, p))],
        out_specs=[pl.BlockSpec((s, w2), lambda p: (0, p)), pl.BlockSpec((s, w2), lambda p: (0, p))],
        out_shape=[jax.ShapeDtypeStruct((s, GROUP_W), F32), jax.ShapeDtypeStruct((s, GROUP_W), BF16)],
        scratch_shapes=[pltpu.VMEM((2, s, LANES), F32)],
        compiler_params=_params(("arbitrary",)),
    )(z, z, z, z, z, wa_f, wa_b, ba_f, ba_b, norm_g)


def _gla_bwd(z, d_o, wa_f, wa_b, ba_f, ba_b, rider=None):
    s = z.shape[0]
    c = GLA_CHUNK
    nchunk = s // c
    scale = GLA_DK ** -0.5
    tm = min(s, GLA_CUM_ROWS)
    one = pl.Buffered(1)

    def body(q_ref, k_ref, v_ref, ga_ref, do_ref, waf_ref, wab_ref, baf_ref, bab_ref,
             dq_ref, dk_ref, dv_ref, dga_ref, dwaf_ref, dwab_ref, dbaf_ref, dbab_ref,
             la_s, dla_s, stash, dq_s, dk_s, dv_s):
        low, up, heads = _gla_consts()
        rowi = lax.broadcasted_iota(jnp.int32, (c, 1), 0)
        _gla_gates(ga_ref, waf_ref, baf_ref, la_s.at[0], s, False)
        _gla_gates(ga_ref, wab_ref, bab_ref, la_s.at[1], s, True)
        for d in range(2):
            tri = (low, up)[d]
            last_row = (rowi == (c - 1 if d == 0 else 0)).astype(F32)
            order = (lambda i: i) if d == 0 else (lambda i: nchunk - 1 - i)
            zero = jnp.zeros((LANES, LANES), F32)

            def states(i, sts):
                n = order(i)
                r = pl.ds(pl.multiple_of(n * c, c), c)
                k = k_ref[r, :]
                _, _, el, dec = _gla_chunk(la_s, d, n)
                kl = k * el
                new = []
                for hh in range(2):
                    cols = slice(LANES * hh, LANES * (hh + 1))
                    stash[hh, n] = sts[hh]
                    new.append(sts[hh] * dec + _dot(v_ref[r, cols].astype(BF16), (kl * heads[hh]).astype(BF16), TN))
                return tuple(new)

            _chunk_loop(nchunk, states, (zero, zero), GLA_UNROLL)

            def step(i, dsts):
                n = order(nchunk - 1 - i)
                r = pl.ds(pl.multiple_of(n * c, c), c)
                q = q_ref[r, :] * scale
                k = k_ref[r, :]
                eq, ek, el, dec = _gla_chunk(la_s, d, n)
                qt = q * eq
                kt = k * ek
                kl = k * el
                ktb = kt.astype(BF16)
                dqt = jnp.zeros((c, LANES), F32)
                dkt = jnp.zeros((c, LANES), F32)
                dkl = jnp.zeros((c, LANES), F32)
                ddec = jnp.zeros((1, LANES), F32)
                new = []
                for hh in range(2):
                    cols = slice(LANES * hh, LANES * (hh + 1))
                    vb = v_ref[r, cols].astype(BF16)
                    dob = do_ref[r, cols].astype(BF16)
                    qm = (qt * heads[hh]).astype(BF16)
                    a = jnp.where(tri, _dot(qm, ktb, NT), 0.0).astype(BF16)
                    da = jnp.where(tri, _dot(dob, vb, NT), 0.0).astype(BF16)
                    sn = stash[hh, n]
                    dst = dsts[hh]
                    dstb = dst.astype(BF16)
                    dqt = dqt + (_dot(da, ktb) + _dot(dob, sn.astype(BF16))) * heads[hh]
                    dkt = dkt + _dot(da, qm, TN)
                    dv = _dot(a, dob, TN) + _dot((kl * heads[hh]).astype(BF16), dstb, NT)
                    dkl = dkl + _dot(vb, dstb)
                    ddec = ddec + jnp.sum(dst * sn, axis=0, keepdims=True)
                    new.append(dst * dec + _dot(dob, qm, TN))
                    if d == 0:
                        dv_s[r, cols] = dv
                    else:
                        dv_ref[r, cols] = (dv_s[r, cols] + dv).astype(BF16)
                dlast = ddec * dec + jnp.sum(dkl * kl, axis=0, keepdims=True)
                dq = dqt * eq * scale
                dk = dkt * ek + dkl * el
                dcum = dqt * qt - dkt * kt - dkl * kl + last_row * dlast
                dla_s[d, r, :] = dcum
                if d == 0:
                    dq_s[r, :] = dq
                    dk_s[r, :] = dk
                else:
                    dq_ref[r, :] = (dq_s[r, :] + dq).astype(BF16)
                    dk_ref[r, :] = (dk_s[r, :] + dk).astype(BF16)
                return tuple(new)

            _chunk_loop(nchunk, step, (zero, zero), GLA_UNROLL)

        first = pl.program_id(0) == 0
        for d, (wa_ref, ba_ref, dwa_ref, dba_ref) in enumerate(
                ((waf_ref, baf_ref, dwaf_ref, dbaf_ref), (wab_ref, bab_ref, dwab_ref, dbab_ref))):
            dwa_ref[...] = jnp.zeros_like(dwa_ref)
            dba_ref[...] = jnp.zeros_like(dba_ref)
            tri_t = _chunk_tri(d == 0)[0:tm, 0:tm]

            def gates(i, carry):
                r = pl.ds(pl.multiple_of(i * tm, tm), tm)
                gab = ga_ref[r, :].astype(BF16)
                wab16 = wa_ref[...].astype(BF16)
                pre = _dot(gab, wab16) + ba_ref[...]
                dpre = _exact_tri_matmul(tri_t, dla_s[d, r, :]) * (1.0 / GLA_TAU) * _sigmoid(-pre)
                dpb = dpre.astype(BF16)
                dwa_ref[...] += _dot(gab, dpb, TN)
                dba_ref[...] += jnp.sum(dpre, axis=0, keepdims=True)
                dga = _dot(dpb, wab16, NT)
                if d == 0:
                    @pl.when(first)
                    def _():
                        dga_ref[r, :] = dga

                    @pl.when(jnp.logical_not(first))
                    def _():
                        dga_ref[r, :] += dga
                else:
                    dga_ref[r, :] += dga
                return carry

            lax.fori_loop(0, s // tm, gates, 0)

    w2 = 2 * LANES
    return _call(
        body, "gla_bwd", (2,),
        [pl.BlockSpec((s, LANES), lambda p: (0, B_Q + p), pipeline_mode=one),
         pl.BlockSpec((s, LANES), lambda p: (0, B_K + p), pipeline_mode=one),
         pl.BlockSpec((s, w2), lambda p: (0, B_V // 2 + p), pipeline_mode=one),
         pl.BlockSpec((s, LANES), lambda p: (0, GA), pipeline_mode=one),
         pl.BlockSpec((s, w2), lambda p: (0, p), pipeline_mode=one),
         pl.BlockSpec((LANES, LANES), lambda p: (0, p)),
         pl.BlockSpec((LANES, LANES), lambda p: (0, p)),
         pl.BlockSpec((1, LANES), lambda p: (0, p)),
         pl.BlockSpec((1, LANES), lambda p: (0, p))],
        [pl.BlockSpec((s, LANES), lambda p: (0, p), pipeline_mode=one),
         pl.BlockSpec((s, LANES), lambda p: (0, p), pipeline_mode=one),
         pl.BlockSpec((s, w2), lambda p: (0, p), pipeline_mode=one),
         pl.BlockSpec((s, LANES), lambda p: (0, 0), pipeline_mode=one),
         pl.BlockSpec((LANES, LANES), lambda p: (0, p)),
         pl.BlockSpec((LANES, LANES), lambda p: (0, p)),
         pl.BlockSpec((1, LANES), lambda p: (0, p)),
         pl.BlockSpec((1, LANES), lambda p: (0, p))],
        [jax.ShapeDtypeStruct((s, w2), BF16), jax.ShapeDtypeStruct((s, w2), BF16),
         jax.ShapeDtypeStruct((s, GROUP_W), BF16), jax.ShapeDtypeStruct((s, LANES), F32),
         jax.ShapeDtypeStruct((LANES, w2), F32), jax.ShapeDtypeStruct((LANES, w2), F32),
         jax.ShapeDtypeStruct((1, w2), F32), jax.ShapeDtypeStruct((1, w2), F32)],
        [pltpu.VMEM((2, s, LANES), F32), pltpu.VMEM((2, s, LANES), F32),
         pltpu.VMEM((2, nchunk, LANES, LANES), F32),
         pltpu.VMEM((s, LANES), F32), pltpu.VMEM((s, LANES), F32), pltpu.VMEM((s, w2), F32)],
        ("arbitrary",), (z, z, z, z, d_o, wa_f, wa_b, ba_f, ba_b), rider)


def _shift_rows(x, d, rowi):
    s = x.shape[0]
    if d == 0:
        return x
    y = pltpu.roll(x, d % s, 0)
    keep = (rowi >= d) if d > 0 else (rowi < s + d)
    return jnp.where(keep, y, 0.0)


def _run_sum(x, m, step, rowi):
    acc, n = x, 1
    while n < m:
        acc = acc + _shift_rows(acc, step * n, rowi)
        n *= 2
    return acc


def _pool_counts(s, w, rowi):
    hi = jnp.minimum(rowi + w // 2, s)
    lo = jnp.maximum(rowi - w // 2, 0)
    return (hi - lo).astype(F32)


def _pooled(u, w, rowi):
    s = u.shape[0]
    win = _shift_rows(_run_sum(u, w // 2, 1, rowi), 1, rowi) + _run_sum(u, w // 2, -1, rowi)
    return win / _pool_counts(s, w, rowi) - u


def _pool_fwd(z, pool_w, pool_scale):
    s = z.shape[0]
    one = pl.Buffered(1)

    def body(u_ref, g_ref, w_ref, sc_ref, y_ref):
        rowi = lax.broadcasted_iota(jnp.int32, (s, 1), 0)
        for g, w in enumerate(POOL_WINDOWS):
            cols = slice(LANES * g, LANES * (g + 1))
            pooled = _pooled(u_ref[:, cols], w, rowi)
            mixed = _dot(pooled.astype(BF16), w_ref[g].astype(BF16))
            y_ref[:, cols] = (_silu(g_ref[:, cols]) * (mixed * sc_ref[:, cols])).astype(BF16)

    return pl.pallas_call(
        body, name="pool_fwd", grid=(1,),
        in_specs=[pl.BlockSpec((s, GROUP_W), lambda i: (0, C_V // 4), pipeline_mode=one),
                  pl.BlockSpec((s, GROUP_W), lambda i: (0, C_G // 4), pipeline_mode=one),
                  pl.BlockSpec((4, LANES, LANES), lambda i: (0, 0, 0)),
                  pl.BlockSpec((1, GROUP_W), lambda i: (0, 0))],
        out_specs=pl.BlockSpec((s, GROUP_W), lambda i: (0, 0), pipeline_mode=one),
        out_shape=jax.ShapeDtypeStruct((s, GROUP_W), BF16),
        compiler_params=_params(("arbitrary",)),
    )(z, z, pool_w, pool_scale)


def _pool_bwd(z, dy, pool_w, pool_scale):
    s = z.shape[0]
    one = pl.Buffered(1)

    def body(u_ref, g_ref, dy_ref, w_ref, sc_ref, du_ref, dg_ref, dw_ref, dsc_ref):
        rowi = lax.broadcasted_iota(jnp.int32, (s, 1), 0)
        for g, w in enumerate(POOL_WINDOWS):
            cols = slice(LANES * g, LANES * (g + 1))
            gate, dyv, sc = g_ref[:, cols], dy_ref[:, cols], sc_ref[:, cols]
            wb = w_ref[g].astype(BF16)
            pooled = _pooled(u_ref[:, cols], w, rowi)
            pb = pooled.astype(BF16)
            mixed = _dot(pb, wb)
            dg_ref[:, cols] = (dyv * (mixed * sc) * _silu_grad(gate)).astype(BF16)
            dt = dyv * _silu(gate)
            dsc_ref[:, cols] = jnp.sum(dt * mixed, axis=0, keepdims=True)
            dmb = (dt * sc).astype(BF16)
            dw_ref[g] = _dot(pb, dmb, TN)
            dpool = _dot(dmb, wb, NT)
            e = dpool / _pool_counts(s, w, rowi)
            du_ref[:, cols] = (_run_sum(e, w // 2, 1, rowi) + _shift_rows(_run_sum(e, w // 2, -1, rowi), -1, rowi)
                               - dpool).astype(BF16)

    return pl.pallas_call(
        body, name="pool_bwd", grid=(1,),
        in_specs=[pl.BlockSpec((s, GROUP_W), lambda i: (0, C_V // 4), pipeline_mode=one),
                  pl.BlockSpec((s, GROUP_W), lambda i: (0, C_G // 4), pipeline_mode=one),
                  pl.BlockSpec((s, GROUP_W), lambda i: (0, 2), pipeline_mode=one),
                  pl.BlockSpec((4, LANES, LANES), lambda i: (0, 0, 0)),
                  pl.BlockSpec((1, GROUP_W), lambda i: (0, 0))],
        out_specs=[pl.BlockSpec((s, GROUP_W), lambda i: (0, 0), pipeline_mode=one),
                   pl.BlockSpec((s, GROUP_W), lambda i: (0, 0), pipeline_mode=one),
                   pl.BlockSpec((4, LANES, LANES), lambda i: (0, 0, 0)),
                   pl.BlockSpec((1, GROUP_W), lambda i: (0, 0))],
        out_shape=[jax.ShapeDtypeStruct((s, GROUP_W), BF16), jax.ShapeDtypeStruct((s, GROUP_W), BF16),
                   jax.ShapeDtypeStruct((4, LANES, LANES), F32), jax.ShapeDtypeStruct((1, GROUP_W), F32)],
        compiler_params=_params(("arbitrary",)),
    )(z, z, dy, pool_w, pool_scale)


def _mla_heads(qf, kv, kpe, qg, kg, cos, sin):
    out = []
    for h in range(4):
        qa = qf[:, LANES * h:LANES * (h + 1)]
        qb = qf[:, 512 + LANES * h:512 + LANES * (h + 1)]
        ka = kv[:, 256 * h:256 * h + LANES]
        rq = lax.rsqrt((jnp.sum(qa * qa, axis=-1, keepdims=True) + jnp.sum(qb * qb, axis=-1, keepdims=True))
                       * (1.0 / MLA_QK) + EPS)
        rk = lax.rsqrt((jnp.sum(ka * ka, axis=-1, keepdims=True) + jnp.sum(kpe * kpe, axis=-1, keepdims=True))
                       * (1.0 / MLA_QK) + EPS)
        out.append((qa, qb, rq, ka, rk))
    return out


def _mla_latents(mq_ref, mkv_ref, gq_ref, gkv_ref, wq_ref, wkv_ref):
    mq = mq_ref[...]
    rq = lax.rsqrt(jnp.mean(mq * mq, axis=-1, keepdims=True) + EPS)
    qn = mq * rq
    qnb = (qn * gq_ref[...]).astype(BF16)
    mkv = mkv_ref[...]
    rk = lax.rsqrt(jnp.mean(mkv * mkv, axis=-1, keepdims=True) + EPS)
    kvn = mkv * rk
    kvnb = (kvn * gkv_ref[...]).astype(BF16)
    qf = _dot(qnb, wq_ref[...])
    kv = _dot(kvnb, wkv_ref[...])
    return qn, rq, qnb, kvn, rk, kvnb, qf, kv


def _mla_prep(z, cos_m, sin_m, gq, wq, gkv, wkv, qg, kg, tm):
    s = z.shape[0]

    def body(mq_ref, mkv_ref, mkr_ref, cos_ref, sin_ref, gq_ref, wq_ref, gkv_ref, wkv_ref, qg_ref, kg_ref,
             q_ref, k_ref, v_ref):
        _, _, _, _, _, _, qf, kv = _mla_latents(mq_ref, mkv_ref, gq_ref, gkv_ref, wq_ref, wkv_ref)
        kpe = mkr_ref[...]
        cos, sin = cos_ref[...], sin_ref[...]
        qg, kg = qg_ref[...], kg_ref[...]
        for h, (qa, qb, rq, ka, rk) in enumerate(_mla_heads(qf, kv, kpe, qg, kg, cos, sin)):
            q_ref[h, :, 0:LANES] = (qa * rq * qg[:, 0:LANES] * ATTN_Q_SCALE).astype(BF16)
            q_ref[h, :, LANES:] = (_rope(qb * rq * qg[:, LANES:], cos, sin) * ATTN_Q_SCALE).astype(BF16)
            k_ref[h, :, 0:LANES] = (ka * rk * kg[:, 0:LANES]).astype(BF16)
            k_ref[h, :, LANES:] = _rope(kpe * rk * kg[:, LANES:], cos, sin).astype(BF16)
            v_ref[h] = kv[:, 256 * h + LANES:256 * (h + 1)].astype(BF16)

    full = lambda shape: pl.BlockSpec(shape, lambda i: (0,) * len(shape))
    return pl.pallas_call(
        body, name="mla_prep", grid=(s // tm,),
        in_specs=[pl.BlockSpec((tm, 512), lambda i: (i, M_Q // 4)),
                  pl.BlockSpec((tm, 256), lambda i: (i, M_KV // 2)),
                  pl.BlockSpec((tm, LANES), lambda i: (i, M_KR)),
                  pl.BlockSpec((tm, LANES), lambda i: (i, 0)),
                  pl.BlockSpec((tm, LANES), lambda i: (i, 0)),
                  full((1, 512)), full((512, 1024)), full((1, 256)), full((256, 1024)), full((1, 256)), full((1, 256))],
        out_specs=[pl.BlockSpec((4, tm, 256), lambda i: (0, i, 0)), pl.BlockSpec((4, tm, 256), lambda i: (0, i, 0)),
                   pl.BlockSpec((4, tm, LANES), lambda i: (0, i, 0))],
        out_shape=[jax.ShapeDtypeStruct((4, s, 256), BF16), jax.ShapeDtypeStruct((4, s, 256), BF16),
                   jax.ShapeDtypeStruct((4, s, LANES), BF16)],
        compiler_params=_params(("parallel",)),
    )(z, z, z, cos_m, sin_m, gq, wq, gkv, wkv, qg, kg)


def _mla_prep_bwd(z, cos_m, sin_m, gq, wq, gkv, wkv, qg, kg, dq, dk, dv, tm):
    s = z.shape[0]

    def body(mq_ref, mkv_ref, mkr_ref, cos_ref, sin_ref, gq_ref, wq_ref, gkv_ref, wkv_ref, qg_ref, kg_ref,
             dq_ref, dk_ref, dv_ref,
             dmq_ref, dmkv_ref, dmkr_ref, dwq_ref, dwkv_ref, dgq_ref, dgkv_ref, dqg_ref, dkg_ref, dqf, dkv):
        @pl.when(pl.program_id(0) == 0)
        def _():
            for r in (dwq_ref, dwkv_ref, dgq_ref, dgkv_ref, dqg_ref, dkg_ref):
                r[...] = jnp.zeros_like(r)

        qn, rq0, qnb, kvn, rk0, kvnb, qf, kv = _mla_latents(mq_ref, mkv_ref, gq_ref, gkv_ref, wq_ref, wkv_ref)
        kpe = mkr_ref[...]
        cos, sin = cos_ref[...], sin_ref[...]
        qg, kg = qg_ref[...], kg_ref[...]
        dkpe = jnp.zeros_like(kpe)
        inv = 1.0 / MLA_QK

        def norm_bwd(a, b, r, da_n, db_n, g):
            ga, gb = g[:, 0:LANES], g[:, LANES:]
            dg_a = jnp.sum(da_n * a * r, axis=0, keepdims=True)
            dg_b = jnp.sum(db_n * b * r, axis=0, keepdims=True)
            ua, ub = da_n * ga, db_n * gb
            dt = (jnp.sum(ua * a, axis=-1, keepdims=True) + jnp.sum(ub * b, axis=-1, keepdims=True)) * inv
            r3 = r * r * r
            return r * ua - a * (r3 * dt), r * ub - b * (r3 * dt), dg_a, dg_b

        for h, (qa, qb, rq, ka, rk) in enumerate(_mla_heads(qf, kv, kpe, qg, kg, cos, sin)):
            dqa, dqb, dga, dgb = norm_bwd(qa, qb, rq, dq_ref[h, :, 0:LANES] * ATTN_SCALE,
                                          _rope_t(dq_ref[h, :, LANES:] * ATTN_SCALE, cos, sin), qg)
            dqf[:, LANES * h:LANES * (h + 1)] = dqa
            dqf[:, 512 + LANES * h:512 + LANES * (h + 1)] = dqb
            dqg_ref[:, 0:LANES] += dga
            dqg_ref[:, LANES:] += dgb
            ln2 = math.log(2.0)
            dka, dkb, dga, dgb = norm_bwd(ka, kpe, rk, dk_ref[h, :, 0:LANES] * ln2,
                                          _rope_t(dk_ref[h, :, LANES:] * ln2, cos, sin), kg)
            dkv[:, 256 * h:256 * h + LANES] = dka
            dkv[:, 256 * h + LANES:256 * (h + 1)] = dv_ref[h]
            dkpe = dkpe + dkb
            dkg_ref[:, 0:LANES] += dga
            dkg_ref[:, LANES:] += dgb
        dmkr_ref[...] = dkpe.astype(BF16)

        def latent_bwd(dfull, w_ref, nb, n, r, g_ref, dw_ref, dg_ref, dlat_ref):
            db = dfull.astype(BF16)
            dn = _dot(db, w_ref[...], NT)
            dw_ref[...] += _dot(nb, db, TN)
            dg_ref[...] += jnp.sum(dn * n, axis=0, keepdims=True)
            u = dn * g_ref[...]
            dlat_ref[...] = (r * (u - n * jnp.mean(u * n, axis=-1, keepdims=True))).astype(BF16)

        latent_bwd(dqf[...], wq_ref, qnb, qn, rq0, gq_ref, dwq_ref, dgq_ref, dmq_ref)
        latent_bwd(dkv[...], wkv_ref, kvnb, kvn, rk0, gkv_ref, dwkv_ref, dgkv_ref, dmkv_ref)

    full = lambda shape: pl.BlockSpec(shape, lambda i: (0,) * len(shape))
    return pl.pallas_call(
        body, name="mla_prep_bwd", grid=(s // tm,),
        in_specs=[pl.BlockSpec((tm, 512), lambda i: (i, M_Q // 4)),
                  pl.BlockSpec((tm, 256), lambda i: (i, M_KV // 2)),
                  pl.BlockSpec((tm, LANES), lambda i: (i, M_KR)),
                  pl.BlockSpec((tm, LANES), lambda i: (i, 0)),
                  pl.BlockSpec((tm, LANES), lambda i: (i, 0)),
                  full((1, 512)), full((512, 1024)), full((1, 256)), full((256, 1024)), full((1, 256)), full((1, 256)),
                  pl.BlockSpec((4, tm, 256), lambda i: (0, i, 0)), pl.BlockSpec((4, tm, 256), lambda i: (0, i, 0)),
                  pl.BlockSpec((4, tm, LANES), lambda i: (0, i, 0))],
        out_specs=[pl.BlockSpec((tm, 512), lambda i: (i, 0)), pl.BlockSpec((tm, 256), lambda i: (i, 0)),
                   pl.BlockSpec((tm, LANES), lambda i: (i, 0)),
                   full((512, 1024)), full((256, 1024)), full((1, 512)), full((1, 256)), full((1, 256)), full((1, 256))],
        out_shape=[jax.ShapeDtypeStruct((s, 512), BF16), jax.ShapeDtypeStruct((s, 256), BF16),
                   jax.ShapeDtypeStruct((s, LANES), BF16),
                   jax.ShapeDtypeStruct((512, 1024), F32), jax.ShapeDtypeStruct((256, 1024), F32),
                   jax.ShapeDtypeStruct((1, 512), F32), jax.ShapeDtypeStruct((1, 256), F32),
                   jax.ShapeDtypeStruct((1, 256), F32), jax.ShapeDtypeStruct((1, 256), F32)],
        scratch_shapes=[pltpu.VMEM((tm, 1024), F32), pltpu.VMEM((tm, 1024), F32)],
        compiler_params=_params(("arbitrary",)),
    )(z, z, z, cos_m, sin_m, gq, wq, gkv, wkv, qg, kg, dq, dk, dv)


def _attn_fwd(q, k, v, z, tq, rider=None):
    s = q.shape[1]

    def body(q_ref, k_ref, v_ref, g_ref, o_ref, y_ref, lse_ref):
        sc = _dot(q_ref[...], k_ref[...], NT)
        m = jnp.max(sc, axis=-1, keepdims=True)
        p = jnp.exp2(sc - m)
        l = jnp.sum(p, axis=-1, keepdims=True)
        o = _dot(p.astype(BF16), v_ref[...]) / l
        o_ref[...] = o
        y_ref[...] = (_silu(g_ref[...]) * o).astype(BF16)
        lse_ref[...] = m + jnp.log2(l)

    return _call(
        body, "attn_fwd", (4, s // tq),
        [pl.BlockSpec((None, tq, 256), lambda h, i: (h, i, 0)),
         pl.BlockSpec((None, s, 256), lambda h, i: (h, 0, 0)),
         pl.BlockSpec((None, s, LANES), lambda h, i: (h, 0, 0)),
         pl.BlockSpec((tq, LANES), lambda h, i: (i, M_G + h))],
        [pl.BlockSpec((tq, LANES), lambda h, i: (i, h)), pl.BlockSpec((tq, LANES), lambda h, i: (i, h)),
         pl.BlockSpec((None, tq, 1), lambda h, i: (h, i, 0))],
        [jax.ShapeDtypeStruct((s, GROUP_W), F32), jax.ShapeDtypeStruct((s, GROUP_W), BF16),
         jax.ShapeDtypeStruct((4, s, 1), F32)],
        [], ("parallel", "parallel"), (q, k, v, z), rider)


def _attn_bwd(q, k, v, z, o, lse, dy, tq, rider=None):
    s = q.shape[1]

    def body(q_ref, k_ref, v_ref, g_ref, o_ref, lse_ref, dy_ref, dq_ref, dk_ref, dv_ref, dg_ref):
        @pl.when(pl.program_id(1) == 0)
        def _():
            dk_ref[...] = jnp.zeros_like(dk_ref)
            dv_ref[...] = jnp.zeros_like(dv_ref)

        gate, ov, dyv = g_ref[...], o_ref[...], dy_ref[...]
        do = dyv * _silu(gate)
        dg_ref[...] = (dyv * ov * _silu_grad(gate)).astype(BF16)
        delta = jnp.sum(do * ov, axis=-1, keepdims=True)
        dob = do.astype(BF16)
        qb, kb = q_ref[...], k_ref[...]
        p = jnp.exp2(_dot(qb, kb, NT) - lse_ref[...])
        dp = _dot(dob, v_ref[...], NT)
        ds = (p * (dp - delta)).astype(BF16)
        dq_ref[...] = _dot(ds, kb)
        dk_ref[...] += _dot(ds, qb, TN)
        dv_ref[...] += _dot(p.astype(BF16), dob, TN)

    return _call(
        body, "attn_bwd", (4, s // tq),
        [pl.BlockSpec((None, tq, 256), lambda h, i: (h, i, 0)),
         pl.BlockSpec((None, s, 256), lambda h, i: (h, 0, 0)),
         pl.BlockSpec((None, s, LANES), lambda h, i: (h, 0, 0)),
         pl.BlockSpec((tq, LANES), lambda h, i: (i, M_G + h)),
         pl.BlockSpec((tq, LANES), lambda h, i: (i, h)),
         pl.BlockSpec((None, tq, 1), lambda h, i: (h, i, 0)),
         pl.BlockSpec((tq, LANES), lambda h, i: (i, 12 + h))],
        [pl.BlockSpec((None, tq, 256), lambda h, i: (h, i, 0)),
         pl.BlockSpec((None, s, 256), lambda h, i: (h, 0, 0)),
         pl.BlockSpec((None, s, LANES), lambda h, i: (h, 0, 0)),
         pl.BlockSpec((tq, LANES), lambda h, i: (i, h))],
        [jax.ShapeDtypeStruct((4, s, 256), F32), jax.ShapeDtypeStruct((4, s, 256), F32),
         jax.ShapeDtypeStruct((4, s, LANES), F32), jax.ShapeDtypeStruct((s, GROUP_W), BF16)],
        [], ("parallel", "arbitrary"), (q, k, v, z, o, lse, dy), rider)


def _loss_head(x, target, tm):
    s, d = x.shape

    def body(x_ref, t_ref, dx_ref, loss_ref):
        @pl.when(pl.program_id(0) == 0)
        def _():
            loss_ref[...] = jnp.zeros_like(loss_ref)
        err = x_ref[...] - t_ref[...]
        dx_ref[...] = err * (1.0 / d)
        per_tok = jnp.mean(err * err, axis=-1, keepdims=True)
        loss_ref[...] += 0.5 * jnp.sum(per_tok, axis=0, keepdims=True)

    return pl.pallas_call(
        body, name="loss_head", grid=(s // tm,),
        in_specs=[pl.BlockSpec((tm, d), lambda i: (i, 0)), pl.BlockSpec((tm, d), lambda i: (i, 0))],
        out_specs=[pl.BlockSpec((tm, d), lambda i: (i, 0)), pl.BlockSpec((1, LANES), lambda i: (0, 0))],
        out_shape=[jax.ShapeDtypeStruct((s, d), F32), jax.ShapeDtypeStruct((1, LANES), F32)],
        compiler_params=_params(("arbitrary",)),
    )(x, target)


def _norm_bwd(x, g, dh, dres, tm):
    s, d = x.shape

    def body(x_ref, g_ref, dh_ref, dres_ref, dx_ref, dg_ref):
        @pl.when(pl.program_id(0) == 0)
        def _():
            dg_ref[...] = jnp.zeros_like(dg_ref)
        xv, dhv = x_ref[...], dh_ref[...]
        r = lax.rsqrt(jnp.mean(xv * xv, axis=-1, keepdims=True) + EPS)
        n = xv * r
        dg_ref[...] += jnp.sum(dhv * n, axis=0, keepdims=True)
        u = dhv * g_ref[...]
        dx_ref[...] = dres_ref[...] + r * (u - n * jnp.mean(u * n, axis=-1, keepdims=True))

    row = lambda: pl.BlockSpec((tm, d), lambda i: (i, 0))
    return pl.pallas_call(
        body, name="norm_bwd", grid=(s // tm,),
        in_specs=[row(), pl.BlockSpec((1, d), lambda i: (0, 0)), row(), row()],
        out_specs=[row(), pl.BlockSpec((1, d), lambda i: (0, 0))],
        out_shape=[jax.ShapeDtypeStruct((s, d), F32), jax.ShapeDtypeStruct((1, d), F32)],
        compiler_params=_params(("arbitrary",)),
    )(x, g, dh, dres)


def _adam(parts, w, m, v, name, tr):
    r, c = w.shape
    tr = min(tr, r)
    c1 = 1.0 - ADAM_B1 ** ADAM_STEP
    c2 = 1.0 - ADAM_B2 ** ADAM_STEP

    def body(p_ref, w_ref, m_ref, v_ref, g_ref, d_ref, nm_ref, nv_ref):
        g = p_ref[0].astype(F32)
        for i in range(1, N_DEV):
            g = g + p_ref[i].astype(F32)
        nm = ADAM_B1 * m_ref[...] + (1.0 - ADAM_B1) * g
        nv = ADAM_B2 * v_ref[...] + (1.0 - ADAM_B2) * (g * g)
        g_ref[...] = g
        nm_ref[...] = nm
        nv_ref[...] = nv
        d_ref[...] = -ADAM_LR * ((nm / c1) / (jnp.sqrt(nv / c2) + ADAM_EPS) + ADAM_WD * w_ref[...])

    blk = lambda: pl.BlockSpec((tr, c), lambda i: (i, 0))
    return pl.pallas_call(
        body, name=name, grid=(r // tr,),
        in_specs=[pl.BlockSpec((N_DEV, tr, c), lambda i: (0, i, 0)), blk(), blk(), blk()],
        out_specs=[blk(), blk(), blk(), blk()],
        out_shape=[jax.ShapeDtypeStruct((r, c), F32)] * 4,
        compiler_params=_params(("parallel",)),
    )(parts, w, m, v)


def _adam_layers(parts, w, m, v, name, tr):
    nl, r, c = w.shape
    pieces = [p for layer in parts for p in layer]
    rp = pieces[0].shape[1]
    tr = min(tr, rp)
    nr, nrp = r // tr, rp // tr
    c1 = 1.0 - ADAM_B1 ** ADAM_STEP
    c2 = 1.0 - ADAM_B2 ** ADAM_STEP

    def body(*refs):
        p_refs, (w_ref, m_ref, v_ref, g_ref, d_ref, nm_ref, nv_ref) = refs[:len(pieces)], refs[len(pieces):]
        at = pl.program_id(0) * nr + pl.program_id(1)
        for j in range(len(pieces)):
            @pl.when(jnp.logical_and(at >= j * nrp, at < (j + 1) * nrp))
            def _(p_ref=p_refs[j]):
                g = p_ref[0].astype(F32)
                for i in range(1, N_DEV):
                    g = g + p_ref[i].astype(F32)
                nm = ADAM_B1 * m_ref[...] + (1.0 - ADAM_B1) * g
                nv = ADAM_B2 * v_ref[...] + (1.0 - ADAM_B2) * (g * g)
                g_ref[...] = g
                nm_ref[...] = nm
                nv_ref[...] = nv
                d_ref[...] = -ADAM_LR * ((nm / c1) / (jnp.sqrt(nv / c2) + ADAM_EPS) + ADAM_WD * w_ref[...])

    def part_spec(j):
        return pl.BlockSpec((N_DEV, tr, c), lambda ll, i: (0, jnp.clip(ll * nr + i - j * nrp, 0, nrp - 1), 0))

    blk = lambda: pl.BlockSpec((None, tr, c), lambda ll, i: (ll, i, 0))
    return pl.pallas_call(
        body, name=name, grid=(nl, nr),
        in_specs=[part_spec(j) for j in range(len(pieces))] + [blk(), blk(), blk()],
        out_specs=[blk(), blk(), blk(), blk()],
        out_shape=[jax.ShapeDtypeStruct((nl, r, c), F32)] * 4,
        compiler_params=_params(("arbitrary", "arbitrary")),
    )(*pieces, w, m, v)


REPLICATED = ("norm_g", "ret_norm_g", "gla_ba_f", "gla_ba_b", "gla_norm_g", "pool_w", "pool_scale",
              "mla_q_norm_g", "mla_kv_norm_g", "mla_qk_norm_q", "mla_qk_norm_k")
SMALL_SHARDED = ("mla_wq_b", "mla_wkv_b", "gla_wa2_f", "gla_wa2_b")
WEIGHTS = ("norm_g", "w_in", "ret_norm_g", "gla_wa2_f", "gla_ba_f", "gla_wa2_b", "gla_ba_b", "gla_norm_g", "pool_w",
           "pool_scale", "mla_q_norm_g", "mla_wq_b", "mla_kv_norm_g", "mla_wkv_b", "mla_qk_norm_q", "mla_qk_norm_k",
           "w_out")


def _pack(arrays, dtype):
    flat = jnp.concatenate([a.reshape(-1) for a in arrays]).astype(dtype)
    return flat.reshape(-1, LANES)


def _unpack(packed, like):
    flat = packed.reshape(-1)
    out, at = [], 0
    for a in like:
        out.append(flat[at:at + a.size].reshape(a.shape))
        at += a.size
    return out


def _columns_by_device(g):
    l, r, n = g.shape
    return g.reshape(l, r, N_DEV, n // N_DEV).transpose(2, 0, 1, 3)


def _gathered_columns(g, l, r, c):
    return g.reshape(N_DEV, l, r, c).transpose(1, 2, 0, 3).reshape(l, r, N_DEV * c)


def _layer_forward(x, wts, w_out_of, tables, tm, tq, ride_inproj=None, ride_attn=None):
    cos_r, sin_r, cos_m, sin_m, tab, _ = tables
    z, h, *carried_in = _inproj(x, wts["norm_g"], wts["w_in"], min(x.shape[0], 2 * tm), rider=ride_inproj)
    o_a, y_a = _ret_fwd(z, cos_r, sin_r, tab, wts["ret_norm_g"])
    o_b, y_b = _gla_fwd(z, wts["wa_f"], wts["wa_b"], wts["gla_ba_f"], wts["gla_ba_b"], wts["gla_norm_g"])
    y_c = _pool_fwd(z, wts["pool_w"], wts["pool_scale"])
    q, k, v = _mla_prep(z, cos_m, sin_m, wts["mla_q_norm_g"], wts["wq"], wts["mla_kv_norm_g"], wts["wkv"],
                        wts["qk_q"], wts["qk_k"], tm)
    o_d, y_d, lse, *carried_attn = _attn_fwd(q, k, v, z, tq, rider=ride_attn)
    y = jnp.concatenate([y_a, y_b, y_c, y_d], axis=1)
    w_out = w_out_of(carried_in)
    x_next = _mm(y, w_out, "nn", "outproj", tm, 2048, 1024, add=x)
    saved = dict(x=x, z=z, h=h, o_a=o_a, o_b=o_b, o_d=o_d, lse=lse, q=q, k=k, v=v, y=y, w_out=w_out)
    return x_next, saved, carried_in, carried_attn


def _layer_backward(dx, sv, wts, tables, tm, tq, rides):
    cos_r, sin_r, cos_m, sin_m, tab, tab_sw = tables
    z = sv["z"]
    g = {}
    carried = {}

    def rider(name):
        return rides[name](g) if name in rides else None

    def landed(name, results, n_own):
        if name in rides:
            carried[name] = results[n_own]
        return results[:n_own]

    g["w_out"] = _mm(sv["y"], dx, "tn", "d_w_out", 2048, 1024, 512, out_dtype=BF16)
    dy = _mm(dx, sv["w_out"], "nt", "d_y", tm, 2048, 1024)

    do_a, dg_a, g["ret_norm_g"] = _normgate_bwd(sv["o_a"], z, A_G, dy, 0, wts["ret_norm_g"], tm)
    dq_a, dk_a, dv_a = landed("ret", _ret_bwd(z, do_a, cos_r, sin_r, tab, tab_sw, rider=rider("ret")), 3)

    do_b, dg_b, g["gla_norm_g"] = _normgate_bwd(sv["o_b"], z, B_G, dy, 1, wts["gla_norm_g"], tm)
    dq_b, dk_b, dv_b, d_ga, d_waf, d_wab, g["gla_ba_f"], g["gla_ba_b"] = landed("gla", _gla_bwd(
        z, do_b, wts["wa_f"], wts["wa_b"], wts["gla_ba_f"], wts["gla_ba_b"], rider=rider("gla")), 8)
    g["gla_wa2_f"] = d_waf[0:GLA_RANK]
    g["gla_wa2_b"] = d_wab[GLA_RANK:2 * GLA_RANK]

    du_c, dg_c, g["pool_w"], g["pool_scale"] = _pool_bwd(z, dy, wts["pool_w"], wts["pool_scale"])

    d_q, d_k, d_v, dg_d = landed("attn", _attn_bwd(sv["q"], sv["k"], sv["v"], z, sv["o_d"], sv["lse"], dy, tq,
                                                   rider=rider("attn")), 4)
    (d_mq, d_mkv, d_mkr, d_wq, g["mla_wkv_b"], g["mla_q_norm_g"], g["mla_kv_norm_g"], d_qg, d_kg) = _mla_prep_bwd(
        z, cos_m, sin_m, wts["mla_q_norm_g"], wts["wq"], wts["mla_kv_norm_g"], wts["wkv"], wts["qk_q"], wts["qk_k"],
        d_q, d_k, d_v, tm)
    g["mla_wq_b"] = _unpad_wq(d_wq)
    g["mla_qk_norm_q"] = d_qg[:, _QK_INV]
    g["mla_qk_norm_k"] = d_kg[:, _QK_INV]

    dz = jnp.concatenate([dq_a, dk_a, dv_a, dg_a, dq_b, dk_b, dv_b, dg_b, d_mq, du_c, dg_c, dg_d, d_mkv,
                          d_ga.astype(BF16), d_mkr], axis=1)
    h, half = sv["h"], D_MODEL // 2
    g["w_in_a"] = _split_w_in(_mm(h[:, :half], dz, "tn", "d_w_in_a", half, 1024, 512, out_dtype=BF16))
    res = _mm(h[:, half:], dz, "tn", "d_w_in_b", half, 1024, 512, out_dtype=BF16, rider=rider("d_w_in"))
    (d_w_in_b,) = landed("d_w_in", res if "d_w_in" in rides else [res], 1)
    g["w_in_b"] = _split_w_in(d_w_in_b)
    res = _mm(dz, wts["w_in"], "nt", "d_h", 2 * tm, 2048, 1024, rider=rider("d_h"))
    (dh,) = landed("d_h", res if "d_h" in rides else [res], 1)
    dx_in, g["norm_g"] = _norm_bwd(sv["x"], wts["norm_g"], dh, dx, tm)
    return dx_in, g, carried


def kernel(x, norm_g, w_in, ret_norm_g, gla_wa2_f, gla_ba_f, gla_wa2_b, gla_ba_b, gla_norm_g, pool_w, pool_scale, mla_q_norm_g, mla_wq_b, mla_kv_norm_g, mla_wkv_b, mla_qk_norm_q, mla_qk_norm_k, w_out, loss_target, m_norm_g, m_w_in, m_ret_norm_g, m_gla_wa2_f, m_gla_ba_f, m_gla_wa2_b, m_gla_ba_b, m_gla_norm_g, m_pool_w, m_pool_scale, m_mla_q_norm_g, m_mla_wq_b, m_mla_kv_norm_g, m_mla_wkv_b, m_mla_qk_norm_q, m_mla_qk_norm_k, m_w_out, v_norm_g, v_w_in, v_ret_norm_g, v_gla_wa2_f, v_gla_ba_f, v_gla_wa2_b, v_gla_ba_b, v_gla_norm_g, v_pool_w, v_pool_scale, v_mla_q_norm_g, v_mla_wq_b, v_mla_kv_norm_g, v_mla_wkv_b, v_mla_qk_norm_q, v_mla_qk_norm_k, v_w_out):
    w = dict(norm_g=norm_g, w_in=w_in, ret_norm_g=ret_norm_g, gla_wa2_f=gla_wa2_f, gla_ba_f=gla_ba_f,
             gla_wa2_b=gla_wa2_b, gla_ba_b=gla_ba_b, gla_norm_g=gla_norm_g, pool_w=pool_w, pool_scale=pool_scale,
             mla_q_norm_g=mla_q_norm_g, mla_wq_b=mla_wq_b, mla_kv_norm_g=mla_kv_norm_g, mla_wkv_b=mla_wkv_b,
             mla_qk_norm_q=mla_qk_norm_q, mla_qk_norm_k=mla_qk_norm_k, w_out=w_out)
    m = dict(norm_g=m_norm_g, w_in=m_w_in, ret_norm_g=m_ret_norm_g, gla_wa2_f=m_gla_wa2_f, gla_ba_f=m_gla_ba_f,
             gla_wa2_b=m_gla_wa2_b, gla_ba_b=m_gla_ba_b, gla_norm_g=m_gla_norm_g, pool_w=m_pool_w,
             pool_scale=m_pool_scale, mla_q_norm_g=m_mla_q_norm_g, mla_wq_b=m_mla_wq_b, mla_kv_norm_g=m_mla_kv_norm_g,
             mla_wkv_b=m_mla_wkv_b, mla_qk_norm_q=m_mla_qk_norm_q, mla_qk_norm_k=m_mla_qk_norm_k, w_out=m_w_out)
    v = dict(norm_g=v_norm_g, w_in=v_w_in, ret_norm_g=v_ret_norm_g, gla_wa2_f=v_gla_wa2_f, gla_ba_f=v_gla_ba_f,
             gla_wa2_b=v_gla_wa2_b, gla_ba_b=v_gla_ba_b, gla_norm_g=v_gla_norm_g, pool_w=v_pool_w,
             pool_scale=v_pool_scale, mla_q_norm_g=v_mla_q_norm_g, mla_wq_b=v_mla_wq_b, mla_kv_norm_g=v_mla_kv_norm_g,
             mla_wkv_b=v_mla_wkv_b, mla_qk_norm_q=v_mla_qk_norm_q, mla_qk_norm_k=v_mla_qk_norm_k, w_out=v_w_out)
    xs, target = x[0], loss_target[0]
    s = xs.shape[0]
    tm, tq = min(s, 512), min(s, 256)
    c_in = w_in.shape[2]

    w_in_b = w_in.astype(BF16)
    w_out_b = w_out.astype(BF16).reshape(-1, D_MODEL)
    w_in_g0, small_g = _exchange([("gather", w_in_b[0]), ("gather", _pack([w[n] for n in SMALL_SHARDED], BF16))],
                                 "gather_first")
    sizes = [w[n].size for n in SMALL_SHARDED]
    offs = np.cumsum([0] + sizes)
    small_flat = small_g.reshape(N_DEV, -1)
    small_full = {n: _gathered_columns(small_flat[:, offs[i]:offs[i + 1]], *w[n].shape)
                  for i, n in enumerate(SMALL_SHARDED)}
    tables = _rope_tables(s) + _ret_tables()

    def layer_weights(l, w_in_g):
        wa_f = jnp.zeros((LANES, 2 * LANES), BF16).at[0:GLA_RANK].set(small_full["gla_wa2_f"][l])
        wa_b = jnp.zeros((LANES, 2 * LANES), BF16).at[GLA_RANK:2 * GLA_RANK].set(small_full["gla_wa2_b"][l])
        return dict(
            norm_g=norm_g[l][None], w_in=_assemble_w_in(w_in_g), ret_norm_g=ret_norm_g[l][None],
            wa_f=wa_f, wa_b=wa_b, gla_ba_f=gla_ba_f[l][None], gla_ba_b=gla_ba_b[l][None],
            gla_norm_g=gla_norm_g[l][None], pool_w=pool_w[l], pool_scale=pool_scale[l][None],
            mla_q_norm_g=mla_q_norm_g[l][None], wq=_pad_wq(small_full["mla_wq_b"][l]),
            mla_kv_norm_g=mla_kv_norm_g[l][None], wkv=small_full["mla_wkv_b"][l],
            qk_q=_pad_qk_gain(mla_qk_norm_q[l]), qk_k=_pad_qk_gain(mla_qk_norm_k[l]))

    def w_out_layer(w_out_g, l):
        return w_out_g.reshape(N_DEV, DEPTH, -1, D_MODEL)[:, l].reshape(-1, D_MODEL)

    by_owner = lambda g_w_out: g_w_out.reshape(N_DEV, -1, D_MODEL)

    layers = [layer_weights(0, w_in_g0), None]
    x1, sv0, (w_out_g,), (w_in_g1,) = _layer_forward(
        xs, layers[0], lambda got: w_out_layer(got[0], 0), tables, tm, tq,
        ride_inproj=("gather", w_out_b), ride_attn=("gather", w_in_b[1]))
    layers[1] = layer_weights(1, w_in_g1)
    x2, sv1, _, _ = _layer_forward(x1, layers[1], lambda got: w_out_layer(w_out_g, 1), tables, tm, tq)
    dx, loss_row = _loss_head(x2, target, tm)
    loss = lax.psum(loss_row[0, 0], ("x", "y", "c"))

    dx, g1, got1 = _layer_backward(dx, sv1, layers[1], tables, tm, tq, {
        "attn": lambda g: ("scatter", by_owner(g["w_out"])),
        "d_h": lambda g: ("scatter", g["w_in_a"])})
    dx, g0, got0 = _layer_backward(dx, sv0, layers[0], tables, tm, tq, {
        "ret": lambda g: ("scatter", by_owner(g["w_out"])),
        "gla": lambda g: ("scatter", g1["w_in_b"]),
        "d_w_in": lambda g: ("scatter", g["w_in_a"]),
        "d_h": lambda g: ("scatter", g["w_in_b"])})
    in_parts = ((got0["d_w_in"], got0["d_h"]), (got1["d_h"], got0["gla"]))
    out_parts = ((got0["ret"],), (got1["attn"],))
    grads = (g0, g1)
    full = {n: jnp.stack([grads[l][n].reshape(w[n].shape[1:]) if n in REPLICATED else grads[l][n]
                          for l in range(DEPTH)]) for n in SMALL_SHARDED + REPLICATED}
    small_c = jnp.concatenate([_columns_by_device(full[n]).reshape(N_DEV, -1) for n in SMALL_SHARDED], axis=1)
    small_parts, rep_parts = _exchange([("scatter", small_c.reshape(N_DEV, -1, LANES)),
                                        ("gather", _pack([full[n] for n in REPLICATED], F32))], "exchange_last")

    out = {}
    out["w_in"] = _adam_layers(in_parts, w_in, m_w_in, v_w_in, "adam_w_in", 256)
    out["w_out"] = _adam_layers(out_parts, w_out, m_w_out, v_w_out, "adam_w_out", 128)
    for names, parts, label in ((SMALL_SHARDED, small_parts, "adam_small"), (REPLICATED, rep_parts, "adam_replicated")):
        res = _adam(parts, _pack([w[n] for n in names], F32), _pack([m[n] for n in names], F32),
                    _pack([v[n] for n in names], F32), label, 2048)
        for n, *vals in zip(names, *[_unpack(a, [w[n] for n in names]) for a in res]):
            out[n] = vals

    return (loss, dx[None], *[out[n][0] for n in WEIGHTS], *[out[n][1] for n in WEIGHTS],
            *[out[n][2] for n in WEIGHTS], *[out[n][3] for n in WEIGHTS])
```

```python
import functools
import math

import numpy as np
import jax
import jax.numpy as jnp
from jax import lax
from jax.experimental import pallas as pl
from jax.experimental.pallas import tpu as pltpu

F32 = jnp.float32
BF16 = jnp.bfloat16

N_DEV = 8
D_MODEL = 2048
DEPTH = 2
GROUP_W = 512
EPS = 1e-6
ROPE_THETA = 10000.0
LANES = 128

RET_HD = 128
RET_CHUNK = 256
RET_UNROLL = 2
GLA_CHUNK = 64
GLA_UNROLL = 4
GLA_CUM_ROWS = 256
GLA_DK = 64
GLA_TAU = 16.0
GLA_RANK = 16
POOL_WINDOWS = (2, 4, 8, 16)
MLA_QK = 192
MLA_ROPE = 64
ATTN_SCALE = MLA_QK ** -0.5
ATTN_Q_SCALE = ATTN_SCALE * math.log2(math.e)
IN_COLS = 5984

ADAM_LR = 0.001
ADAM_B1 = 0.9
ADAM_B2 = 0.999
ADAM_EPS = 1e-08
ADAM_WD = 0.01
ADAM_STEP = 10

A_Q, A_K, A_V, A_G = 0, 4, 8, 12
B_Q, B_K, B_V, B_G = 16, 18, 20, 24
M_Q, C_V, C_G, M_G = 28, 32, 36, 40
M_KV, GA, M_KR = 44, 46, 47
ZP_COLS = 48 * LANES

VMEM_LIMIT = 56 * 1024 * 1024


def _params(sem, vmem=VMEM_LIMIT):
    return pltpu.CompilerParams(dimension_semantics=sem, vmem_limit_bytes=vmem)


def _sigmoid(x):
    return 1.0 / (1.0 + jnp.exp(-x))


def _silu(x):
    return x * _sigmoid(x)


def _silu_grad(x):
    s = _sigmoid(x)
    return s * (1.0 + x * (1.0 - s))


def _dot(a, b, dims=(((1,), (0,)), ((), ()))):
    return lax.dot_general(a, b, dims, preferred_element_type=F32)


NT = (((1,), (1,)), ((), ()))
TN = (((0,), (0,)), ((), ()))


def _chunk_loop(n, body, init, unroll):
    unroll = math.gcd(n, unroll)

    def trip(t, carry):
        for u in range(unroll):
            carry = body(t * unroll + u, carry)
        return carry

    return lax.fori_loop(0, n // unroll, trip, init)


def _roll_lanes_half(x):
    return pltpu.roll(x, 64, 1)


def _wq_perm():
    idx = np.zeros((1024,), np.int32)
    ok = np.zeros((1024,), bool)
    for h in range(4):
        idx[128 * h:128 * h + 128] = 192 * h + np.arange(128)
        ok[128 * h:128 * h + 128] = True
        base = 512 + 128 * h
        idx[base:base + 32] = 192 * h + 128 + np.arange(32)
        ok[base:base + 32] = True
        idx[base + 64:base + 96] = 192 * h + 160 + np.arange(32)
        ok[base + 64:base + 96] = True
    inv = np.zeros((768,), np.int32)
    inv[idx[ok]] = np.nonzero(ok)[0]
    return idx, ok, inv


_WQ_IDX, _WQ_OK, _WQ_INV = _wq_perm()


def _pad_wq(wq):
    return jnp.where(jnp.asarray(_WQ_OK)[None, :], wq[:, _WQ_IDX], 0).astype(wq.dtype)


def _unpad_wq(wqp):
    return wqp[:, _WQ_INV]


def _qk_idx():
    idx = np.zeros((256,), np.int32)
    ok = np.zeros((256,), bool)
    idx[0:128] = np.arange(128)
    ok[0:128] = True
    idx[128:160] = 128 + np.arange(32)
    ok[128:160] = True
    idx[192:224] = 160 + np.arange(32)
    ok[192:224] = True
    inv = np.zeros((192,), np.int32)
    inv[idx[ok]] = np.nonzero(ok)[0]
    return idx, ok, inv


_QK_IDX, _QK_OK, _QK_INV = _qk_idx()


def _pad_qk_gain(g):
    return jnp.where(jnp.asarray(_QK_OK), g[_QK_IDX], 0.0).reshape(1, 256)


def _rope_tables(s):
    def tabs(dim):
        inv = 1.0 / (ROPE_THETA ** (jnp.arange(0, dim, 2, dtype=F32) / dim))
        ang = jnp.arange(s, dtype=F32)[:, None] * inv[None, :]
        return jnp.cos(ang), jnp.sin(ang)
    cr, sr = tabs(RET_HD)
    cos_r = jnp.concatenate([cr, cr], axis=1)
    sin_r = jnp.concatenate([-sr, sr], axis=1)
    cm, sm = tabs(MLA_ROPE)
    zz = jnp.zeros_like(cm)
    cos_m = jnp.concatenate([cm, zz, cm, zz], axis=1)
    sin_m = jnp.concatenate([-sm, zz, sm, zz], axis=1)
    return cos_r, sin_r, cos_m, sin_m


def _rope(x, cos, sin):
    return x * cos + _roll_lanes_half(x) * sin


def _rope_t(x, cos, sin):
    return x * cos + _roll_lanes_half(x * sin)


def _ret_tables():
    c = RET_CHUNK
    gamma_f = 1.0 - 2.0 ** (-5.0 - jnp.arange(4, dtype=F32))
    gamma_b = gamma_f[::-1]
    idx = jnp.arange(c, dtype=F32)
    diff = idx[:, None] - idx[None, :]

    def build(g1, g2):
        l1 = jnp.log(g1)[:, None, None]
        l2 = jnp.log(g2)[:, None, None]
        d1 = jnp.where(diff >= 0, jnp.exp(jnp.maximum(diff, 0.0)[None] * l1), 0.0)
        d2 = jnp.where(diff <= 0, jnp.exp(jnp.maximum(-diff, 0.0)[None] * l2), 0.0)
        ones = jnp.ones((1, c, LANES), F32)
        col = idx[None, :, None]
        qdf = jnp.exp((col + 1.0) * l1) * ones
        kdf = jnp.exp((c - 1.0 - col) * l1) * ones
        qdb = jnp.exp((c - col) * l2) * ones
        kdb = jnp.exp(col * l2) * ones
        cd1 = jnp.exp(c * l1) * ones
        cd2 = jnp.exp(c * l2) * ones
        return jnp.concatenate([d1 + d2, qdf, kdf, qdb, kdb, cd1, cd2], axis=2)

    return build(gamma_f, gamma_b), build(gamma_b, gamma_f)


MESH = pl.DeviceIdType.MESH
ANY = pl.BlockSpec(memory_space=pl.ANY)
_RELATIONS = ((0, 0, 1), (1, 0, 0), (0, 1, 0), (1, 1, 0), (1, 0, 1), (0, 1, 1), (1, 1, 1))


def _position():
    return lax.axis_index("x"), lax.axis_index("y"), lax.axis_index("c")


def _gather_copies(x_ref, out_ref, send_sems, recv_sems, local_sem, starting):
    x, y, cc = _position()
    me, sibling = (x, y, cc), (x, y, 1 - cc)
    chips = [(1 - x, y), (x, 1 - y), (1 - x, 1 - y)]

    def slab(px, py, pc):
        return out_ref.at[4 * px + 2 * py + pc]

    def copy(k, block, to, src=None):
        return pltpu.make_async_remote_copy(
            src_ref=slab(*block) if src is None else src, dst_ref=slab(*block),
            send_sem=send_sems.at[k], recv_sem=recv_sems.at[k], device_id=to, device_id_type=MESH)

    mine = pltpu.make_async_copy(x_ref, slab(*me), local_sem)
    first = [copy(0, me, sibling, src=x_ref)] + [copy(1 + j, me, (*chip, cc), src=x_ref) for j, chip in enumerate(chips)]
    if starting:
        return mine, first
    passed = [copy(4 + j, (*chip, cc), sibling) for j, chip in enumerate(chips)]
    arrivals = [copy(1 + j, (*chip, cc), me) for j, chip in enumerate(chips)]
    late = [copy(0, sibling, me)] + [copy(4 + j, (*chip, 1 - cc), me) for j, chip in enumerate(chips)]
    return mine, first, passed, arrivals, late


def _gather_start(*refs):
    mine, first = _gather_copies(*refs, starting=True)
    mine.start()
    for cp in first:
        cp.start()


def _gather_finish(*refs):
    mine, first, passed, arrivals, late = _gather_copies(*refs, starting=False)
    for arrived, onward in zip(arrivals, passed):
        arrived.wait_recv()
        onward.start()
    for cp in late:
        cp.wait_recv()
    for cp in first + passed:
        cp.wait_send()
    mine.wait()


def _scatter_copies(c_ref, out_ref, send_sems, recv_sems, local_sem):
    x, y, cc = _position()
    me = 4 * x + 2 * y + cc
    mine = pltpu.make_async_copy(c_ref.at[me], out_ref.at[me], local_sem)
    copies = []
    for k, (fx, fy, fc) in enumerate(_RELATIONS):
        px = 1 - x if fx else x
        py = 1 - y if fy else y
        pc = 1 - cc if fc else cc
        copies.append(pltpu.make_async_remote_copy(
            src_ref=c_ref.at[4 * px + 2 * py + pc], dst_ref=out_ref.at[me],
            send_sem=send_sems.at[k], recv_sem=recv_sems.at[k], device_id=(px, py, pc), device_id_type=MESH))
    return mine, copies


def _scatter_start(*refs):
    mine, copies = _scatter_copies(*refs)
    mine.start()
    for cp in copies:
        cp.start()


def _scatter_finish(*refs):
    mine, copies = _scatter_copies(*refs)
    for cp in copies:
        cp.wait()
    mine.wait()


_EXCHANGES = {"gather": (_gather_start, _gather_finish), "scatter": (_scatter_start, _scatter_finish)}


def _exchange_scratch():
    return [pltpu.SemaphoreType.DMA((7,)), pltpu.SemaphoreType.DMA((7,)), pltpu.SemaphoreType.DMA]


def _exchange_out(kind, src):
    return jax.ShapeDtypeStruct(((N_DEV,) + src.shape) if kind == "gather" else src.shape, src.dtype)


def _exchange(jobs, name):
    n = len(jobs)

    def body(*refs):
        srcs, outs, sems = refs[:n], refs[n:2 * n], refs[2 * n:]
        for half in (0, 1):
            for i, (kind, _) in enumerate(jobs):
                _EXCHANGES[kind][half](srcs[i], outs[i], *sems[3 * i:3 * i + 3])

    return pl.pallas_call(
        body, name=name, out_shape=[_exchange_out(kind, src) for kind, src in jobs],
        in_specs=[ANY] * n, out_specs=[ANY] * n,
        scratch_shapes=[sem for _ in jobs for sem in _exchange_scratch()])(*[src for _, src in jobs])


def _call(body, name, grid, in_specs, out_specs, out_shape, scratch, sem, args, rider=None):
    if rider is None:
        return pl.pallas_call(body, name=name, grid=grid, in_specs=in_specs, out_specs=out_specs, out_shape=out_shape,
                              scratch_shapes=scratch, compiler_params=_params(sem))(*args)
    jobs = rider if isinstance(rider, list) else [rider]
    ni, no, ns, nj = len(in_specs), len(out_specs), len(scratch), len(jobs)

    def carried(*refs):
        ins, rsrcs = refs[:ni], refs[ni:ni + nj]
        outs, routs = refs[ni + nj:ni + nj + no], refs[ni + nj + no:ni + 2 * nj + no]
        scr, sems = refs[ni + 2 * nj + no:ni + 2 * nj + no + ns], refs[ni + 2 * nj + no + ns:]
        ids = [pl.program_id(a) for a in range(len(grid))]
        is_first = functools.reduce(jnp.logical_and, [i == 0 for i in ids])
        is_last = functools.reduce(jnp.logical_and, [i == g - 1 for i, g in zip(ids, grid)])

        def half(which):
            for j, (kind, _) in enumerate(jobs):
                _EXCHANGES[kind][which](rsrcs[j], routs[j], *sems[3 * j:3 * j + 3])

        @pl.when(is_first)
        def _():
            half(0)

        body(*ins, *outs, *scr)

        @pl.when(is_last)
        def _():
            half(1)

    return pl.pallas_call(
        carried, name=name, grid=grid, in_specs=list(in_specs) + [ANY] * nj, out_specs=list(out_specs) + [ANY] * nj,
        out_shape=list(out_shape) + [_exchange_out(kind, src) for kind, src in jobs],
        scratch_shapes=list(scratch) + [sem for _ in jobs for sem in _exchange_scratch()],
        compiler_params=_params(("arbitrary",) * len(grid)))(*args, *[src for _, src in jobs])


def _inproj(x, g, wp, tm, tn=512, rider=None):
    s, d = x.shape
    n = wp.shape[1]

    def body(x_ref, g_ref, w_ref, z_ref, h_ref, hs):
        @pl.when(pl.program_id(1) == 0)
        def _():
            xv = x_ref[...]
            r = lax.rsqrt(jnp.mean(xv * xv, axis=-1, keepdims=True) + EPS)
            hv = (xv * r * g_ref[...]).astype(BF16)
            hs[...] = hv
            h_ref[...] = hv
        z_ref[...] = _dot(hs[...], w_ref[...])

    return _call(
        body, "inproj", (s // tm, n // tn),
        [pl.BlockSpec((tm, d), lambda i, j: (i, 0)),
         pl.BlockSpec((1, d), lambda i, j: (0, 0)),
         pl.BlockSpec((d, tn), lambda i, j: (0, j))],
        [pl.BlockSpec((tm, tn), lambda i, j: (i, j)), pl.BlockSpec((tm, d), lambda i, j: (i, 0))],
        [jax.ShapeDtypeStruct((s, n), F32), jax.ShapeDtypeStruct((s, d), BF16)],
        [pltpu.VMEM((tm, d), BF16)], ("parallel", "arbitrary"), (x, g, wp), rider)


def _relayout_plan():
    runs = ((0, 3584, 0), (3584, 3616, GA * LANES), (3616, 4640, C_V * LANES), (4640, 5152, M_Q * LANES),
            (5152, 5408, M_KV * LANES), (5408, 5440, M_KR * LANES), (5440, 5472, M_KR * LANES + 64),
            (5472, 5984, M_G * LANES))
    shard = IN_COLS // N_DEV
    plan = []
    for d in range(N_DEV):
        lo, hi = shard * d, shard * (d + 1)
        for a, b, p in runs:
            s, e = max(a, lo), min(b, hi)
            if s < e:
                plan.append((d, s - lo, p + (s - a), e - s))
    return plan


def _assemble_w_in(g, tr=256):
    _, r, c = g.shape

    def body(g_ref, o_ref):
        o_ref[...] = jnp.zeros_like(o_ref)
        for d, at, to, w in _relayout_plan():
            o_ref[:, to:to + w] = g_ref[d, :, at:at + w]

    return pl.pallas_call(
        body, name="assemble_w_in", grid=(r // tr,),
        in_specs=[pl.BlockSpec((N_DEV, tr, c), lambda i: (0, i, 0))],
        out_specs=pl.BlockSpec((tr, ZP_COLS), lambda i: (i, 0)),
        out_shape=jax.ShapeDtypeStruct((r, ZP_COLS), g.dtype),
        compiler_params=_params(("parallel",)),
    )(g)


def _split_w_in(wp, tr=256):
    r = wp.shape[0]
    c = IN_COLS // N_DEV

    def body(w_ref, o_ref):
        for d, at, to, w in _relayout_plan():
            o_ref[d, :, at:at + w] = w_ref[:, to:to + w]

    return pl.pallas_call(
        body, name="split_w_in", grid=(r // tr,),
        in_specs=[pl.BlockSpec((tr, ZP_COLS), lambda i: (i, 0))],
        out_specs=pl.BlockSpec((N_DEV, tr, c), lambda i: (0, i, 0)),
        out_shape=jax.ShapeDtypeStruct((N_DEV, r, c), wp.dtype),
        compiler_params=_params(("parallel",)),
    )(wp)


def _mm(a, b, mode, name, tm, tn, tk, add=None, out_dtype=F32, rider=None, tail=None):
    if mode == "tn":
        k, m = a.shape
    else:
        m, k = a.shape
    n = b.shape[0] if mode == "nt" else b.shape[1]
    tm, tn, tk = min(tm, m), min(tn, n), min(tk, k)
    nk = k // tk
    dims = {"nn": (((1,), (0,)), ((), ())), "nt": NT, "tn": TN}[mode]
    if tail is None:
        def plain(acc, i, extra_refs, out_refs):
            out_refs[0][...] = (acc + extra_refs[0][...] if extra_refs else acc).astype(out_dtype)
        tail = ([(add, "tile")] if add is not None else [], [(out_dtype, "tile")], plain)
    extra, outs, fn = tail
    spec = {"tile": pl.BlockSpec((tm, tn), lambda i, j, kk: (i, j)),
            "row": pl.BlockSpec((1, tn), lambda i, j, kk: (0, j)),
            "lanes": pl.BlockSpec((1, LANES), lambda i, j, kk: (0, 0))}
    shape = {"tile": (m, n), "row": (1, n), "lanes": (1, LANES)}
    ne, no = len(extra), len(outs)

    def body(*refs):
        a_ref, b_ref = refs[:2]
        extra_refs, out_refs, acc = refs[2:2 + ne], refs[2 + ne:2 + ne + no], refs[2 + ne + no]
        i, kk = pl.program_id(0), pl.program_id(2)

        @pl.when(kk == 0)
        def _():
            acc[...] = jnp.zeros_like(acc)

        acc[...] += _dot(a_ref[...].astype(BF16), b_ref[...].astype(BF16), dims)

        @pl.when(kk == nk - 1)
        def _():
            fn(acc[...], i, extra_refs, out_refs)

    a_spec = (pl.BlockSpec((tk, tm), lambda i, j, kk: (kk, i)) if mode == "tn"
              else pl.BlockSpec((tm, tk), lambda i, j, kk: (i, kk)))
    b_spec = (pl.BlockSpec((tn, tk), lambda i, j, kk: (j, kk)) if mode == "nt"
              else pl.BlockSpec((tk, tn), lambda i, j, kk: (kk, j)))
    summed = any(kind != "tile" for _, kind in outs)
    res = _call(body, name, (m // tm, n // tn, nk), [a_spec, b_spec] + [spec[kind] for _, kind in extra],
                [spec[kind] for _, kind in outs], [jax.ShapeDtypeStruct(shape[kind], dt) for dt, kind in outs],
                [pltpu.VMEM((tm, tn), F32)], ("arbitrary",) * 3 if summed else ("parallel", "parallel", "arbitrary"),
                [a, b] + [arr for arr, _ in extra], rider)
    return res[0] if (rider is None and no == 1) else res


def _norm_bwd_tail(x, g, dres):
    def fn(dh, i, extra_refs, out_refs):
        x_ref, g_ref, dres_ref = extra_refs
        dx_ref, dg_ref = out_refs

        @pl.when(i == 0)
        def _():
            dg_ref[...] = jnp.zeros_like(dg_ref)

        xv = x_ref[...]
        r = lax.rsqrt(jnp.mean(xv * xv, axis=-1, keepdims=True) + EPS)
        nv = xv * r
        dg_ref[...] += jnp.sum(dh * nv, axis=0, keepdims=True)
        u = dh * g_ref[...]
        dx_ref[...] = dres_ref[...] + r * (u - nv * jnp.mean(u * nv, axis=-1, keepdims=True))

    return [(x, "tile"), (g, "row"), (dres, "tile")], [(F32, "tile"), (F32, "row")], fn


def _loss_tail(x, target):
    d = x.shape[1]

    def fn(acc, i, extra_refs, out_refs):
        x_ref, t_ref = extra_refs
        dx_ref, loss_ref = out_refs

        @pl.when(i == 0)
        def _():
            loss_ref[...] = jnp.zeros_like(loss_ref)

        err = acc + x_ref[...] - t_ref[...]
        dx_ref[...] = err * (1.0 / d)
        per_tok = jnp.mean(err * err, axis=-1, keepdims=True)
        loss_ref[...] += 0.5 * jnp.sum(per_tok, axis=0, keepdims=True)

    return [(x, "tile"), (target, "tile")], [(F32, "tile"), (F32, "lanes")], fn


def _ret_core(q_ref, k_ref, v_ref, tab_ref, out_ref, nchunk):
    c = RET_CHUNK

    def rows(n):
        return pl.ds(pl.multiple_of(n * c, c), c)

    zero = jnp.zeros((LANES, LANES), F32)

    def plane(i, n=c):
        return tab_ref[0:n, c + LANES * i:c + LANES * (i + 1)]

    def fwd(n, st):
        r = rows(n)
        q, k, vb = q_ref[r, :], k_ref[r, :], v_ref[r, :].astype(BF16)
        sc = _dot(q.astype(BF16), k.astype(BF16), NT) * tab_ref[:, 0:c]
        o = _dot(sc.astype(BF16), vb)
        o = o + _dot((q * plane(0)).astype(BF16), st.astype(BF16))
        out_ref[r, :] = o
        return st * plane(4, LANES) + _dot((k * plane(1)).astype(BF16), vb, TN)

    _chunk_loop(nchunk, fwd, zero, RET_UNROLL)

    def bwd(i, st):
        r = rows(nchunk - 1 - i)
        q, k, vb = q_ref[r, :], k_ref[r, :], v_ref[r, :].astype(BF16)
        out_ref[r, :] += _dot((q * plane(2)).astype(BF16), st.astype(BF16))
        return st * plane(5, LANES) + _dot((k * plane(3)).astype(BF16), vb, TN)

    _chunk_loop(nchunk, bwd, zero, RET_UNROLL)


def _ret_fwd(z, cos_r, sin_r, tab, norm_g):
    s = z.shape[0]
    nchunk = s // RET_CHUNK
    scale = RET_HD ** -0.5
    col = lambda base: pl.BlockSpec((s, LANES), lambda h: (0, base + h), pipeline_mode=pl.Buffered(1))

    def body(q_ref, k_ref, v_ref, g_ref, cos_ref, sin_ref, tab_ref, ng_ref, o_ref, y_ref, qh, kh):
        qh[...] = _rope(q_ref[...], cos_ref[...], sin_ref[...])
        kh[...] = _rope(k_ref[...], cos_ref[...], sin_ref[...]) * scale
        _ret_core(qh, kh, v_ref, tab_ref, o_ref, nchunk)
        o = o_ref[...]
        r = lax.rsqrt(jnp.mean(o * o, axis=-1, keepdims=True) + EPS)
        y_ref[...] = (_silu(g_ref[...]) * (o * r * ng_ref[...])).astype(BF16)

    return pl.pallas_call(
        body, name="ret_fwd", grid=(4,),
        in_specs=[col(A_Q), col(A_K), col(A_V), col(A_G),
                  pl.BlockSpec((s, LANES), lambda h: (0, 0), pipeline_mode=pl.Buffered(1)),
                  pl.BlockSpec((s, LANES), lambda h: (0, 0), pipeline_mode=pl.Buffered(1)),
                  pl.BlockSpec((None, RET_CHUNK, RET_CHUNK + 6 * LANES), lambda h: (h, 0, 0)),
                  pl.BlockSpec((1, LANES), lambda h: (0, h))],
        out_specs=[pl.BlockSpec((s, LANES), lambda h: (0, h)), pl.BlockSpec((s, LANES), lambda h: (0, h))],
        out_shape=[jax.ShapeDtypeStruct((s, GROUP_W), F32), jax.ShapeDtypeStruct((s, GROUP_W), BF16)],
        scratch_shapes=[pltpu.VMEM((s, LANES), F32), pltpu.VMEM((s, LANES), F32)],
        compiler_params=_params(("arbitrary",)),
    )(z, z, z, z, cos_r, sin_r, tab, norm_g)


def _ret_bwd(z, d_o, cos_r, sin_r, tab, tab_sw, rider=None):
    s = z.shape[0]
    nchunk = s // RET_CHUNK
    scale = RET_HD ** -0.5
    col = lambda base: pl.BlockSpec((s, LANES), lambda h: (0, base + h), pipeline_mode=pl.Buffered(1))
    whole = lambda: pl.BlockSpec((s, LANES), lambda h: (0, 0), pipeline_mode=pl.Buffered(1))
    tabspec = lambda: pl.BlockSpec((None, RET_CHUNK, RET_CHUNK + 6 * LANES), lambda h: (h, 0, 0))
    outspec = lambda: pl.BlockSpec((s, LANES), lambda h: (0, h))

    def body(q_ref, k_ref, v_ref, do_ref, cos_ref, sin_ref, tab_ref, tsw_ref, dq_ref, dk_ref, dv_ref,
             qh, kh, tmp):
        cos, sin = cos_ref[...], sin_ref[...]
        qh[...] = _rope(q_ref[...], cos, sin)
        kh[...] = _rope(k_ref[...], cos, sin) * scale
        _ret_core(kh, qh, do_ref, tsw_ref, tmp, nchunk)
        dv_ref[...] = tmp[...].astype(BF16)
        _ret_core(do_ref, v_ref, kh, tab_ref, tmp, nchunk)
        dq_ref[...] = _rope_t(tmp[...], cos, sin).astype(BF16)
        _ret_core(v_ref, do_ref, qh, tsw_ref, tmp, nchunk)
        dk_ref[...] = _rope_t(tmp[...] * scale, cos, sin).astype(BF16)

    return _call(
        body, "ret_bwd", (4,),
        [col(A_Q), col(A_K), col(A_V),
         pl.BlockSpec((s, LANES), lambda h: (0, h), pipeline_mode=pl.Buffered(1)),
         whole(), whole(), tabspec(), tabspec()],
        [outspec(), outspec(), outspec()],
        [jax.ShapeDtypeStruct((s, GROUP_W), BF16)] * 3,
        [pltpu.VMEM((s, LANES), F32), pltpu.VMEM((s, LANES), F32), pltpu.VMEM((s, LANES), F32)],
        ("arbitrary",), (z, z, z, d_o, cos_r, sin_r, tab, tab_sw), rider)


def _normgate_bwd(o, z, gate_blk, dy, dy_blk, norm_g, tm):
    s = o.shape[0]

    def body(o_ref, g_ref, dy_ref, ng_ref, do_ref, dg_ref, dng_ref):
        @pl.when(pl.program_id(0) == 0)
        def _():
            dng_ref[...] = jnp.zeros_like(dng_ref)

        for h in range(4):
            sl = slice(LANES * h, LANES * (h + 1))
            ov, gv, dyv, ng = o_ref[:, sl], g_ref[:, sl], dy_ref[:, sl], ng_ref[:, sl]
            r = lax.rsqrt(jnp.mean(ov * ov, axis=-1, keepdims=True) + EPS)
            on = ov * r
            dn = dyv * _silu(gv)
            u = dn * ng
            do_ref[:, sl] = r * (u - on * jnp.mean(u * on, axis=-1, keepdims=True))
            dg_ref[:, sl] = (dyv * (on * ng) * _silu_grad(gv)).astype(BF16)
            dng_ref[:, sl] += jnp.sum(dn * on, axis=0, keepdims=True)

    return pl.pallas_call(
        body, name="normgate_bwd", grid=(s // tm,),
        in_specs=[pl.BlockSpec((tm, GROUP_W), lambda i: (i, 0)),
                  pl.BlockSpec((tm, GROUP_W), lambda i: (i, gate_blk // 4)),
                  pl.BlockSpec((tm, GROUP_W), lambda i: (i, dy_blk)),
                  pl.BlockSpec((1, GROUP_W), lambda i: (0, 0))],
        out_specs=[pl.BlockSpec((tm, GROUP_W), lambda i: (i, 0)), pl.BlockSpec((tm, GROUP_W), lambda i: (i, 0)),
                   pl.BlockSpec((1, GROUP_W), lambda i: (0, 0))],
        out_shape=[jax.ShapeDtypeStruct((s, GROUP_W), F32), jax.ShapeDtypeStruct((s, GROUP_W), BF16),
                   jax.ShapeDtypeStruct((1, GROUP_W), F32)],
        compiler_params=_params(("arbitrary",)),
    )(o, z, dy, norm_g)


def _log_sigmoid(x):
    return jnp.minimum(x, 0.0) - jnp.log(1.0 + jnp.exp(-jnp.abs(x)))


def _gla_consts():
    c = GLA_CHUNK
    row = lax.broadcasted_iota(jnp.int32, (c, c), 0)
    colm = lax.broadcasted_iota(jnp.int32, (c, c), 1)
    lane = lax.broadcasted_iota(jnp.int32, (1, LANES), 1)
    low = row >= colm
    up = colm >= row
    heads = ((lane < GLA_DK).astype(F32), (lane >= GLA_DK).astype(F32))
    return low, up, heads


def _chunk_tri(upper):
    r = lax.broadcasted_iota(jnp.int32, (GLA_CUM_ROWS, GLA_CUM_ROWS), 0)
    c = lax.broadcasted_iota(jnp.int32, (GLA_CUM_ROWS, GLA_CUM_ROWS), 1)
    shift = GLA_CHUNK.bit_length() - 1
    same = jnp.right_shift(r, shift) == jnp.right_shift(c, shift)
    return jnp.where(jnp.logical_and(same, (c >= r) if upper else (r >= c)), 1.0, 0.0).astype(BF16)


def _exact_tri_matmul(tri, x):
    hi = x.astype(BF16)
    rest = x - hi.astype(F32)
    mid = rest.astype(BF16)
    lo = (rest - mid.astype(F32)).astype(BF16)
    return _dot(tri, hi) + _dot(tri, mid) + _dot(tri, lo)


def _gla_chunk(cum_ref, d, n):
    c = GLA_CHUNK
    cum = cum_ref[d, pl.ds(pl.multiple_of(n * c, c), c), :]
    last = cum_ref[d, pl.ds(n * c + (c - 1 if d == 0 else 0), 1), :]
    eq = jnp.exp(cum)
    ek = jnp.exp(-cum)
    el = jnp.exp(last - cum)
    dec = jnp.exp(last)
    return eq, ek, el, dec


def _gla_gates(ga_ref, wa_ref, ba_ref, cum_ref, s, upper):
    tri = _chunk_tri(upper)
    rows = min(s, GLA_CUM_ROWS)

    def step(i, carry):
        r = pl.ds(pl.multiple_of(i * rows, rows), rows)
        pre = _dot(ga_ref[r, :].astype(BF16), wa_ref[...].astype(BF16)) + ba_ref[...]
        cum_ref[r, :] = _exact_tri_matmul(tri[0:rows, 0:rows], _log_sigmoid(pre) * (1.0 / GLA_TAU))
        return carry
    lax.fori_loop(0, s // rows, step, 0)


def _gla_fwd(z, wa_f, wa_b, ba_f, ba_b, norm_g):
    s = z.shape[0]
    c = GLA_CHUNK
    nchunk = s // c
    scale = GLA_DK ** -0.5
    tm = min(s, 512)
    one = pl.Buffered(1)

    def body(q_ref, k_ref, v_ref, ga_ref, g_ref, waf_ref, wab_ref, baf_ref, bab_ref, ng_ref, o_ref, y_ref,
             la_s):
        low, up, heads = _gla_consts()
        _gla_gates(ga_ref, waf_ref, baf_ref, la_s.at[0], s, False)
        _gla_gates(ga_ref, wab_ref, bab_ref, la_s.at[1], s, True)
        for d in range(2):
            tri = (low, up)[d]

            def step(i, states):
                n = i if d == 0 else nchunk - 1 - i
                r = pl.ds(pl.multiple_of(n * c, c), c)
                q = q_ref[r, :] * scale
                k = k_ref[r, :]
                eq, ek, el, dec = _gla_chunk(la_s, d, n)
                qt = q * eq
                ktb = (k * ek).astype(BF16)
                kl = k * el
                new_states = []
                for hh in range(2):
                    cols = slice(LANES * hh, LANES * (hh + 1))
                    vb = v_ref[r, cols].astype(BF16)
                    qm = (qt * heads[hh]).astype(BF16)
                    a = jnp.where(tri, _dot(qm, ktb, NT), 0.0)
                    o = _dot(a.astype(BF16), vb) + _dot(qm, states[hh].astype(BF16), NT)
                    if d == 0:
                        o_ref[r, cols] = o
                    else:
                        o_ref[r, cols] += o
                    new_states.append(states[hh] * dec + _dot(vb, (kl * heads[hh]).astype(BF16), TN))
                return tuple(new_states)

            zero = jnp.zeros((LANES, LANES), F32)
            _chunk_loop(nchunk, step, (zero, zero), GLA_UNROLL)

        def epi(i, carry):
            r = pl.ds(pl.multiple_of(i * tm, tm), tm)
            for hh in range(2):
                cols = slice(LANES * hh, LANES * (hh + 1))
                o = o_ref[r, cols]
                rr = lax.rsqrt(jnp.mean(o * o, axis=-1, keepdims=True) + EPS)
                y_ref[r, cols] = (_silu(g_ref[r, cols]) * (o * rr * ng_ref[:, cols])).astype(BF16)
            return carry

        lax.fori_loop(0, s // tm, epi, 0)

    w2 = 2 * LANES
    return pl.pallas_call(
        body, name="gla_fwd", grid=(2,),
        in_specs=[pl.BlockSpec((s, LANES), lambda p: (0, B_Q + p), pipeline_mode=one),
                  pl.BlockSpec((s, LANES), lambda p: (0, B_K + p), pipeline_mode=one),
                  pl.BlockSpec((s, w2), lambda p: (0, B_V // 2 + p), pipeline_mode=one),
                  pl.BlockSpec((s, LANES), lambda p: (0, GA), pipeline_mode=one),
                  pl.BlockSpec((s, w2), lambda p: (0, B_G // 2 + p), pipeline_mode=one),
                  pl.BlockSpec((LANES, LANES), lambda p: (0, p)),
                  pl.BlockSpec((LANES, LANES), lambda p: (0, p)),
                  pl.BlockSpec((1, LANES), lambda p: (0, p)),
                  pl.BlockSpec((1, LANES), lambda p: (0, p)),
                  pl.BlockSpec((1, w2), lambda p: (0, p))],
        out_specs=[pl.BlockSpec((s, w2), lambda p: (0, p)), pl.BlockSpec((s, w2), lambda p: (0, p))],
        out_shape=[jax.ShapeDtypeStruct((s, GROUP_W), F32), jax.ShapeDtypeStruct((s, GROUP_W), BF16)],
        scratch_shapes=[pltpu.VMEM((2, s, LANES), F32)],
        compiler_params=_params(("arbitrary",)),
    )(z, z, z, z, z, wa_f, wa_b, ba_f, ba_b, norm_g)


def _gla_bwd(z, d_o, wa_f, wa_b, ba_f, ba_b, rider=None):
    s = z.shape[0]
    c = GLA_CHUNK
    nchunk = s // c
    scale = GLA_DK ** -0.5
    tm = min(s, GLA_CUM_ROWS)
    one = pl.Buffered(1)

    def body(q_ref, k_ref, v_ref, ga_ref, do_ref, waf_ref, wab_ref, baf_ref, bab_ref,
             dq_ref, dk_ref, dv_ref, dga_ref, dwaf_ref, dwab_ref, dbaf_ref, dbab_ref,
             la_s, dla_s, stash, dq_s, dk_s, dv_s):
        low, up, heads = _gla_consts()
        rowi = lax.broadcasted_iota(jnp.int32, (c, 1), 0)
        _gla_gates(ga_ref, waf_ref, baf_ref, la_s.at[0], s, False)
        _gla_gates(ga_ref, wab_ref, bab_ref, la_s.at[1], s, True)
        for d in range(2):
            tri = (low, up)[d]
            last_row = (rowi == (c - 1 if d == 0 else 0)).astype(F32)
            order = (lambda i: i) if d == 0 else (lambda i: nchunk - 1 - i)
            zero = jnp.zeros((LANES, LANES), F32)

            def states(i, sts):
                n = order(i)
                r = pl.ds(pl.multiple_of(n * c, c), c)
                k = k_ref[r, :]
                _, _, el, dec = _gla_chunk(la_s, d, n)
                kl = k * el
                new = []
                for hh in range(2):
                    cols = slice(LANES * hh, LANES * (hh + 1))
                    stash[hh, n] = sts[hh]
                    new.append(sts[hh] * dec + _dot(v_ref[r, cols].astype(BF16), (kl * heads[hh]).astype(BF16), TN))
                return tuple(new)

            _chunk_loop(nchunk, states, (zero, zero), GLA_UNROLL)

            def step(i, dsts):
                n = order(nchunk - 1 - i)
                r = pl.ds(pl.multiple_of(n * c, c), c)
                q = q_ref[r, :] * scale
                k = k_ref[r, :]
                eq, ek, el, dec = _gla_chunk(la_s, d, n)
                qt = q * eq
                kt = k * ek
                kl = k * el
                ktb = kt.astype(BF16)
                dqt = jnp.zeros((c, LANES), F32)
                dkt = jnp.zeros((c, LANES), F32)
                dkl = jnp.zeros((c, LANES), F32)
                ddec = jnp.zeros((1, LANES), F32)
                new = []
                for hh in range(2):
                    cols = slice(LANES * hh, LANES * (hh + 1))
                    vb = v_ref[r, cols].astype(BF16)
                    dob = do_ref[r, cols].astype(BF16)
                    qm = (qt * heads[hh]).astype(BF16)
                    a = jnp.where(tri, _dot(qm, ktb, NT), 0.0).astype(BF16)
                    da = jnp.where(tri, _dot(dob, vb, NT), 0.0).astype(BF16)
                    sn = stash[hh, n]
                    dst = dsts[hh]
                    dstb = dst.astype(BF16)
                    dqt = dqt + (_dot(da, ktb) + _dot(dob, sn.astype(BF16))) * heads[hh]
                    dkt = dkt + _dot(da, qm, TN)
                    dv = _dot(a, dob, TN) + _dot((kl * heads[hh]).astype(BF16), dstb, NT)
                    dkl = dkl + _dot(vb, dstb)
                    ddec = ddec + jnp.sum(dst * sn, axis=0, keepdims=True)
                    new.append(dst * dec + _dot(dob, qm, TN))
                    if d == 0:
                        dv_s[r, cols] = dv
                    else:
                        dv_ref[r, cols] = (dv_s[r, cols] + dv).astype(BF16)
                dlast = ddec * dec + jnp.sum(dkl * kl, axis=0, keepdims=True)
                dq = dqt * eq * scale
                dk = dkt * ek + dkl * el
                dcum = dqt * qt - dkt * kt - dkl * kl + last_row * dlast
                dla_s[d, r, :] = dcum
                if d == 0:
                    dq_s[r, :] = dq
                    dk_s[r, :] = dk
                else:
                    dq_ref[r, :] = (dq_s[r, :] + dq).astype(BF16)
                    dk_ref[r, :] = (dk_s[r, :] + dk).astype(BF16)
                return tuple(new)

            _chunk_loop(nchunk, step, (zero, zero), GLA_UNROLL)

        first = pl.program_id(0) == 0
        for d, (wa_ref, ba_ref, dwa_ref, dba_ref) in enumerate(
                ((waf_ref, baf_ref, dwaf_ref, dbaf_ref), (wab_ref, bab_ref, dwab_ref, dbab_ref))):
            dwa_ref[...] = jnp.zeros_like(dwa_ref)
            dba_ref[...] = jnp.zeros_like(dba_ref)
            tri_t = _chunk_tri(d == 0)[0:tm, 0:tm]

            def gates(i, carry):
                r = pl.ds(pl.multiple_of(i * tm, tm), tm)
                gab = ga_ref[r, :].astype(BF16)
                wab16 = wa_ref[...].astype(BF16)
                pre = _dot(gab, wab16) + ba_ref[...]
                dpre = _exact_tri_matmul(tri_t, dla_s[d, r, :]) * (1.0 / GLA_TAU) * _sigmoid(-pre)
                dpb = dpre.astype(BF16)
                dwa_ref[...] += _dot(gab, dpb, TN)
                dba_ref[...] += jnp.sum(dpre, axis=0, keepdims=True)
                dga = _dot(dpb, wab16, NT)
                if d == 0:
                    @pl.when(first)
                    def _():
                        dga_ref[r, :] = dga

                    @pl.when(jnp.logical_not(first))
                    def _():
                        dga_ref[r, :] += dga
                else:
                    dga_ref[r, :] += dga
                return carry

            lax.fori_loop(0, s // tm, gates, 0)

    w2 = 2 * LANES
    return _call(
        body, "gla_bwd", (2,),
        [pl.BlockSpec((s, LANES), lambda p: (0, B_Q + p), pipeline_mode=one),
         pl.BlockSpec((s, LANES), lambda p: (0, B_K + p), pipeline_mode=one),
         pl.BlockSpec((s, w2), lambda p: (0, B_V // 2 + p), pipeline_mode=one),
         pl.BlockSpec((s, LANES), lambda p: (0, GA), pipeline_mode=one),
         pl.BlockSpec((s, w2), lambda p: (0, p), pipeline_mode=one),
         pl.BlockSpec((LANES, LANES), lambda p: (0, p)),
         pl.BlockSpec((LANES, LANES), lambda p: (0, p)),
         pl.BlockSpec((1, LANES), lambda p: (0, p)),
         pl.BlockSpec((1, LANES), lambda p: (0, p))],
        [pl.BlockSpec((s, LANES), lambda p: (0, p), pipeline_mode=one),
         pl.BlockSpec((s, LANES), lambda p: (0, p), pipeline_mode=one),
         pl.BlockSpec((s, w2), lambda p: (0, p), pipeline_mode=one),
         pl.BlockSpec((s, LANES), lambda p: (0, 0), pipeline_mode=one),
         pl.BlockSpec((LANES, LANES), lambda p: (0, p)),
         pl.BlockSpec((LANES, LANES), lambda p: (0, p)),
         pl.BlockSpec((1, LANES), lambda p: (0, p)),
         pl.BlockSpec((1, LANES), lambda p: (0, p))],
        [jax.ShapeDtypeStruct((s, w2), BF16), jax.ShapeDtypeStruct((s, w2), BF16),
         jax.ShapeDtypeStruct((s, GROUP_W), BF16), jax.ShapeDtypeStruct((s, LANES), F32),
         jax.ShapeDtypeStruct((LANES, w2), F32), jax.ShapeDtypeStruct((LANES, w2), F32),
         jax.ShapeDtypeStruct((1, w2), F32), jax.ShapeDtypeStruct((1, w2), F32)],
        [pltpu.VMEM((2, s, LANES), F32), pltpu.VMEM((2, s, LANES), F32),
         pltpu.VMEM((2, nchunk, LANES, LANES), F32),
         pltpu.VMEM((s, LANES), F32), pltpu.VMEM((s, LANES), F32), pltpu.VMEM((s, w2), F32)],
        ("arbitrary",), (z, z, z, z, d_o, wa_f, wa_b, ba_f, ba_b), rider)


def _shift_rows(x, d, rowi):
    s = x.shape[0]
    if d == 0:
        return x
    y = pltpu.roll(x, d % s, 0)
    keep = (rowi >= d) if d > 0 else (rowi < s + d)
    return jnp.where(keep, y, 0.0)


def _run_sum(x, m, step, rowi):
    acc, n = x, 1
    while n < m:
        acc = acc + _shift_rows(acc, step * n, rowi)
        n *= 2
    return acc


def _pool_counts(s, w, rowi):
    hi = jnp.minimum(rowi + w // 2, s)
    lo = jnp.maximum(rowi - w // 2, 0)
    return (hi - lo).astype(F32)


def _pooled(u, w, rowi):
    s = u.shape[0]
    win = _shift_rows(_run_sum(u, w // 2, 1, rowi), 1, rowi) + _run_sum(u, w // 2, -1, rowi)
    return win / _pool_counts(s, w, rowi) - u


def _pool_fwd(z, pool_w, pool_scale):
    s = z.shape[0]
    one = pl.Buffered(1)

    def body(u_ref, g_ref, w_ref, sc_ref, y_ref):
        rowi = lax.broadcasted_iota(jnp.int32, (s, 1), 0)
        for g, w in enumerate(POOL_WINDOWS):
            cols = slice(LANES * g, LANES * (g + 1))
            pooled = _pooled(u_ref[:, cols], w, rowi)
            mixed = _dot(pooled.astype(BF16), w_ref[g].astype(BF16))
            y_ref[:, cols] = (_silu(g_ref[:, cols]) * (mixed * sc_ref[:, cols])).astype(BF16)

    return pl.pallas_call(
        body, name="pool_fwd", grid=(1,),
        in_specs=[pl.BlockSpec((s, GROUP_W), lambda i: (0, C_V // 4), pipeline_mode=one),
                  pl.BlockSpec((s, GROUP_W), lambda i: (0, C_G // 4), pipeline_mode=one),
                  pl.BlockSpec((4, LANES, LANES), lambda i: (0, 0, 0)),
                  pl.BlockSpec((1, GROUP_W), lambda i: (0, 0))],
        out_specs=pl.BlockSpec((s, GROUP_W), lambda i: (0, 0), pipeline_mode=one),
        out_shape=jax.ShapeDtypeStruct((s, GROUP_W), BF16),
        compiler_params=_params(("arbitrary",)),
    )(z, z, pool_w, pool_scale)


def _pool_bwd(z, dy, pool_w, pool_scale):
    s = z.shape[0]
    one = pl.Buffered(1)

    def body(u_ref, g_ref, dy_ref, w_ref, sc_ref, du_ref, dg_ref, dw_ref, dsc_ref):
        rowi = lax.broadcasted_iota(jnp.int32, (s, 1), 0)
        for g, w in enumerate(POOL_WINDOWS):
            cols = slice(LANES * g, LANES * (g + 1))
            gate, dyv, sc = g_ref[:, cols], dy_ref[:, cols], sc_ref[:, cols]
            wb = w_ref[g].astype(BF16)
            pooled = _pooled(u_ref[:, cols], w, rowi)
            pb = pooled.astype(BF16)
            mixed = _dot(pb, wb)
            dg_ref[:, cols] = (dyv * (mixed * sc) * _silu_grad(gate)).astype(BF16)
            dt = dyv * _silu(gate)
            dsc_ref[:, cols] = jnp.sum(dt * mixed, axis=0, keepdims=True)
            dmb = (dt * sc).astype(BF16)
            dw_ref[g] = _dot(pb, dmb, TN)
            dpool = _dot(dmb, wb, NT)
            e = dpool / _pool_counts(s, w, rowi)
            du_ref[:, cols] = (_run_sum(e, w // 2, 1, rowi) + _shift_rows(_run_sum(e, w // 2, -1, rowi), -1, rowi)
                               - dpool).astype(BF16)

    return pl.pallas_call(
        body, name="pool_bwd", grid=(1,),
        in_specs=[pl.BlockSpec((s, GROUP_W), lambda i: (0, C_V // 4), pipeline_mode=one),
                  pl.BlockSpec((s, GROUP_W), lambda i: (0, C_G // 4), pipeline_mode=one),
                  pl.BlockSpec((s, GROUP_W), lambda i: (0, 2), pipeline_mode=one),
                  pl.BlockSpec((4, LANES, LANES), lambda i: (0, 0, 0)),
                  pl.BlockSpec((1, GROUP_W), lambda i: (0, 0))],
        out_specs=[pl.BlockSpec((s, GROUP_W), lambda i: (0, 0), pipeline_mode=one),
                   pl.BlockSpec((s, GROUP_W), lambda i: (0, 0), pipeline_mode=one),
                   pl.BlockSpec((4, LANES, LANES), lambda i: (0, 0, 0)),
                   pl.BlockSpec((1, GROUP_W), lambda i: (0, 0))],
        out_shape=[jax.ShapeDtypeStruct((s, GROUP_W), BF16), jax.ShapeDtypeStruct((s, GROUP_W), BF16),
                   jax.ShapeDtypeStruct((4, LANES, LANES), F32), jax.ShapeDtypeStruct((1, GROUP_W), F32)],
        compiler_params=_params(("arbitrary",)),
    )(z, z, dy, pool_w, pool_scale)


def _mla_heads(qf, kv, kpe, qg, kg, cos, sin):
    out = []
    for h in range(4):
        qa = qf[:, LANES * h:LANES * (h + 1)]
        qb = qf[:, 512 + LANES * h:512 + LANES * (h + 1)]
        ka = kv[:, 256 * h:256 * h + LANES]
        rq = lax.rsqrt((jnp.sum(qa * qa, axis=-1, keepdims=True) + jnp.sum(qb * qb, axis=-1, keepdims=True))
                       * (1.0 / MLA_QK) + EPS)
        rk = lax.rsqrt((jnp.sum(ka * ka, axis=-1, keepdims=True) + jnp.sum(kpe * kpe, axis=-1, keepdims=True))
                       * (1.0 / MLA_QK) + EPS)
        out.append((qa, qb, rq, ka, rk))
    return out


def _mla_latents(mq_ref, mkv_ref, gq_ref, gkv_ref, wq_ref, wkv_ref):
    mq = mq_ref[...]
    rq = lax.rsqrt(jnp.mean(mq * mq, axis=-1, keepdims=True) + EPS)
    qn = mq * rq
    qnb = (qn * gq_ref[...]).astype(BF16)
    mkv = mkv_ref[...]
    rk = lax.rsqrt(jnp.mean(mkv * mkv, axis=-1, keepdims=True) + EPS)
    kvn = mkv * rk
    kvnb = (kvn * gkv_ref[...]).astype(BF16)
    qf = _dot(qnb, wq_ref[...])
    kv = _dot(kvnb, wkv_ref[...])
    return qn, rq, qnb, kvn, rk, kvnb, qf, kv


def _mla_prep(z, cos_m, sin_m, gq, wq, gkv, wkv, qg, kg, tm):
    s = z.shape[0]

    def body(mq_ref, mkv_ref, mkr_ref, cos_ref, sin_ref, gq_ref, wq_ref, gkv_ref, wkv_ref, qg_ref, kg_ref,
             q_ref, k_ref, v_ref):
        _, _, _, _, _, _, qf, kv = _mla_latents(mq_ref, mkv_ref, gq_ref, gkv_ref, wq_ref, wkv_ref)
        kpe = mkr_ref[...]
        cos, sin = cos_ref[...], sin_ref[...]
        qg, kg = qg_ref[...], kg_ref[...]
        for h, (qa, qb, rq, ka, rk) in enumerate(_mla_heads(qf, kv, kpe, qg, kg, cos, sin)):
            q_ref[h, :, 0:LANES] = (qa * rq * qg[:, 0:LANES] * ATTN_Q_SCALE).astype(BF16)
            q_ref[h, :, LANES:] = (_rope(qb * rq * qg[:, LANES:], cos, sin) * ATTN_Q_SCALE).astype(BF16)
            k_ref[h, :, 0:LANES] = (ka * rk * kg[:, 0:LANES]).astype(BF16)
            k_ref[h, :, LANES:] = _rope(kpe * rk * kg[:, LANES:], cos, sin).astype(BF16)
            v_ref[h] = kv[:, 256 * h + LANES:256 * (h + 1)].astype(BF16)

    full = lambda shape: pl.BlockSpec(shape, lambda i: (0,) * len(shape))
    return pl.pallas_call(
        body, name="mla_prep", grid=(s // tm,),
        in_specs=[pl.BlockSpec((tm, 512), lambda i: (i, M_Q // 4)),
                  pl.BlockSpec((tm, 256), lambda i: (i, M_KV // 2)),
                  pl.BlockSpec((tm, LANES), lambda i: (i, M_KR)),
                  pl.BlockSpec((tm, LANES), lambda i: (i, 0)),
                  pl.BlockSpec((tm, LANES), lambda i: (i, 0)),
                  full((1, 512)), full((512, 1024)), full((1, 256)), full((256, 1024)), full((1, 256)), full((1, 256))],
        out_specs=[pl.BlockSpec((4, tm, 256), lambda i: (0, i, 0)), pl.BlockSpec((4, tm, 256), lambda i: (0, i, 0)),
                   pl.BlockSpec((4, tm, LANES), lambda i: (0, i, 0))],
        out_shape=[jax.ShapeDtypeStruct((4, s, 256), BF16), jax.ShapeDtypeStruct((4, s, 256), BF16),
                   jax.ShapeDtypeStruct((4, s, LANES), BF16)],
        compiler_params=_params(("parallel",)),
    )(z, z, z, cos_m, sin_m, gq, wq, gkv, wkv, qg, kg)


def _mla_prep_bwd(z, cos_m, sin_m, gq, wq, gkv, wkv, qg, kg, dq, dk, dv, tm):
    s = z.shape[0]

    def body(mq_ref, mkv_ref, mkr_ref, cos_ref, sin_ref, gq_ref, wq_ref, gkv_ref, wkv_ref, qg_ref, kg_ref,
             dq_ref, dk_ref, dv_ref,
             dmq_ref, dmkv_ref, dmkr_ref, dwq_ref, dwkv_ref, dgq_ref, dgkv_ref, dqg_ref, dkg_ref, dqf, dkv):
        @pl.when(pl.program_id(0) == 0)
        def _():
            for r in (dwq_ref, dwkv_ref, dgq_ref, dgkv_ref, dqg_ref, dkg_ref):
                r[...] = jnp.zeros_like(r)

        qn, rq0, qnb, kvn, rk0, kvnb, qf, kv = _mla_latents(mq_ref, mkv_ref, gq_ref, gkv_ref, wq_ref, wkv_ref)
        kpe = mkr_ref[...]
        cos, sin = cos_ref[...], sin_ref[...]
        qg, kg = qg_ref[...], kg_ref[...]
        dkpe = jnp.zeros_like(kpe)
        inv = 1.0 / MLA_QK

        def norm_bwd(a, b, r, da_n, db_n, g):
            ga, gb = g[:, 0:LANES], g[:, LANES:]
            dg_a = jnp.sum(da_n * a * r, axis=0, keepdims=True)
            dg_b = jnp.sum(db_n * b * r, axis=0, keepdims=True)
            ua, ub = da_n * ga, db_n * gb
            dt = (jnp.sum(ua * a, axis=-1, keepdims=True) + jnp.sum(ub * b, axis=-1, keepdims=True)) * inv
            r3 = r * r * r
            return r * ua - a * (r3 * dt), r * ub - b * (r3 * dt), dg_a, dg_b

        for h, (qa, qb, rq, ka, rk) in enumerate(_mla_heads(qf, kv, kpe, qg, kg, cos, sin)):
            dqa, dqb, dga, dgb = norm_bwd(qa, qb, rq, dq_ref[h, :, 0:LANES] * ATTN_SCALE,
                                          _rope_t(dq_ref[h, :, LANES:] * ATTN_SCALE, cos, sin), qg)
            dqf[:, LANES * h:LANES * (h + 1)] = dqa
            dqf[:, 512 + LANES * h:512 + LANES * (h + 1)] = dqb
            dqg_ref[:, 0:LANES] += dga
            dqg_ref[:, LANES:] += dgb
            ln2 = math.log(2.0)
            dka, dkb, dga, dgb = norm_bwd(ka, kpe, rk, dk_ref[h, :, 0:LANES] * ln2,
                                          _rope_t(dk_ref[h, :, LANES:] * ln2, cos, sin), kg)
            dkv[:, 256 * h:256 * h + LANES] = dka
            dkv[:, 256 * h + LANES:256 * (h + 1)] = dv_ref[h]
            dkpe = dkpe + dkb
            dkg_ref[:, 0:LANES] += dga
            dkg_ref[:, LANES:] += dgb
        dmkr_ref[...] = dkpe.astype(BF16)

        def latent_bwd(dfull, w_ref, nb, n, r, g_ref, dw_ref, dg_ref, dlat_ref):
            db = dfull.astype(BF16)
            dn = _dot(db, w_ref[...], NT)
            dw_ref[...] += _dot(nb, db, TN)
            dg_ref[...] += jnp.sum(dn * n, axis=0, keepdims=True)
            u = dn * g_ref[...]
            dlat_ref[...] = (r * (u - n * jnp.mean(u * n, axis=-1, keepdims=True))).astype(BF16)

        latent_bwd(dqf[...], wq_ref, qnb, qn, rq0, gq_ref, dwq_ref, dgq_ref, dmq_ref)
        latent_bwd(dkv[...], wkv_ref, kvnb, kvn, rk0, gkv_ref, dwkv_ref, dgkv_ref, dmkv_ref)

    full = lambda shape: pl.BlockSpec(shape, lambda i: (0,) * len(shape))
    return pl.pallas_call(
        body, name="mla_prep_bwd", grid=(s // tm,),
        in_specs=[pl.BlockSpec((tm, 512), lambda i: (i, M_Q // 4)),
                  pl.BlockSpec((tm, 256), lambda i: (i, M_KV // 2)),
                  pl.BlockSpec((tm, LANES), lambda i: (i, M_KR)),
                  pl.BlockSpec((tm, LANES), lambda i: (i, 0)),
                  pl.BlockSpec((tm, LANES), lambda i: (i, 0)),
                  full((1, 512)), full((512, 1024)), full((1, 256)), full((256, 1024)), full((1, 256)), full((1, 256)),
                  pl.BlockSpec((4, tm, 256), lambda i: (0, i, 0)), pl.BlockSpec((4, tm, 256), lambda i: (0, i, 0)),
                  pl.BlockSpec((4, tm, LANES), lambda i: (0, i, 0))],
        out_specs=[pl.BlockSpec((tm, 512), lambda i: (i, 0)), pl.BlockSpec((tm, 256), lambda i: (i, 0)),
                   pl.BlockSpec((tm, LANES), lambda i: (i, 0)),
                   full((512, 1024)), full((256, 1024)), full((1, 512)), full((1, 256)), full((1, 256)), full((1, 256))],
        out_shape=[jax.ShapeDtypeStruct((s, 512), BF16), jax.ShapeDtypeStruct((s, 256), BF16),
                   jax.ShapeDtypeStruct((s, LANES), BF16),
                   jax.ShapeDtypeStruct((512, 1024), F32), jax.ShapeDtypeStruct((256, 1024), F32),
                   jax.ShapeDtypeStruct((1, 512), F32), jax.ShapeDtypeStruct((1, 256), F32),
                   jax.ShapeDtypeStruct((1, 256), F32), jax.ShapeDtypeStruct((1, 256), F32)],
        scratch_shapes=[pltpu.VMEM((tm, 1024), F32), pltpu.VMEM((tm, 1024), F32)],
        compiler_params=_params(("arbitrary",)),
    )(z, z, z, cos_m, sin_m, gq, wq, gkv, wkv, qg, kg, dq, dk, dv)


def _attn_fwd(q, k, v, z, tq, rider=None):
    s = q.shape[1]

    def body(q_ref, k_ref, v_ref, g_ref, o_ref, y_ref, lse_ref):
        sc = _dot(q_ref[...], k_ref[...], NT)
        m = jnp.max(sc, axis=-1, keepdims=True)
        p = jnp.exp2(sc - m)
        l = jnp.sum(p, axis=-1, keepdims=True)
        o = _dot(p.astype(BF16), v_ref[...]) / l
        o_ref[...] = o
        y_ref[...] = (_silu(g_ref[...]) * o).astype(BF16)
        lse_ref[...] = m + jnp.log2(l)

    return _call(
        body, "attn_fwd", (4, s // tq),
        [pl.BlockSpec((None, tq, 256), lambda h, i: (h, i, 0)),
         pl.BlockSpec((None, s, 256), lambda h, i: (h, 0, 0)),
         pl.BlockSpec((None, s, LANES), lambda h, i: (h, 0, 0)),
         pl.BlockSpec((tq, LANES), lambda h, i: (i, M_G + h))],
        [pl.BlockSpec((tq, LANES), lambda h, i: (i, h)), pl.BlockSpec((tq, LANES), lambda h, i: (i, h)),
         pl.BlockSpec((None, tq, 1), lambda h, i: (h, i, 0))],
        [jax.ShapeDtypeStruct((s, GROUP_W), F32), jax.ShapeDtypeStruct((s, GROUP_W), BF16),
         jax.ShapeDtypeStruct((4, s, 1), F32)],
        [], ("parallel", "parallel"), (q, k, v, z), rider)


def _attn_bwd(q, k, v, z, o, lse, dy, tq, rider=None):
    s = q.shape[1]

    def body(q_ref, k_ref, v_ref, g_ref, o_ref, lse_ref, dy_ref, dq_ref, dk_ref, dv_ref, dg_ref):
        @pl.when(pl.program_id(1) == 0)
        def _():
            dk_ref[...] = jnp.zeros_like(dk_ref)
            dv_ref[...] = jnp.zeros_like(dv_ref)

        gate, ov, dyv = g_ref[...], o_ref[...], dy_ref[...]
        do = dyv * _silu(gate)
        dg_ref[...] = (dyv * ov * _silu_grad(gate)).astype(BF16)
        delta = jnp.sum(do * ov, axis=-1, keepdims=True)
        dob = do.astype(BF16)
        qb, kb = q_ref[...], k_ref[...]
        p = jnp.exp2(_dot(qb, kb, NT) - lse_ref[...])
        dp = _dot(dob, v_ref[...], NT)
        ds = (p * (dp - delta)).astype(BF16)
        dq_ref[...] = _dot(ds, kb)
        dk_ref[...] += _dot(ds, qb, TN)
        dv_ref[...] += _dot(p.astype(BF16), dob, TN)

    return _call(
        body, "attn_bwd", (4, s // tq),
        [pl.BlockSpec((None, tq, 256), lambda h, i: (h, i, 0)),
         pl.BlockSpec((None, s, 256), lambda h, i: (h, 0, 0)),
         pl.BlockSpec((None, s, LANES), lambda h, i: (h, 0, 0)),
         pl.BlockSpec((tq, LANES), lambda h, i: (i, M_G + h)),
         pl.BlockSpec((tq, LANES), lambda h, i: (i, h)),
         pl.BlockSpec((None, tq, 1), lambda h, i: (h, i, 0)),
         pl.BlockSpec((tq, LANES), lambda h, i: (i, 12 + h))],
        [pl.BlockSpec((None, tq, 256), lambda h, i: (h, i, 0)),
         pl.BlockSpec((None, s, 256), lambda h, i: (h, 0, 0)),
         pl.BlockSpec((None, s, LANES), lambda h, i: (h, 0, 0)),
         pl.BlockSpec((tq, LANES), lambda h, i: (i, h))],
        [jax.ShapeDtypeStruct((4, s, 256), F32), jax.ShapeDtypeStruct((4, s, 256), F32),
         jax.ShapeDtypeStruct((4, s, LANES), F32), jax.ShapeDtypeStruct((s, GROUP_W), BF16)],
        [], ("parallel", "arbitrary"), (q, k, v, z, o, lse, dy), rider)


def _adam(parts, w, m, v, name, tr):
    r, c = w.shape
    tr = min(tr, r)
    c1 = 1.0 - ADAM_B1 ** ADAM_STEP
    c2 = 1.0 - ADAM_B2 ** ADAM_STEP

    def body(p_ref, w_ref, m_ref, v_ref, g_ref, d_ref, nm_ref, nv_ref):
        g = p_ref[0].astype(F32)
        for i in range(1, N_DEV):
            g = g + p_ref[i].astype(F32)
        nm = ADAM_B1 * m_ref[...] + (1.0 - ADAM_B1) * g
        nv = ADAM_B2 * v_ref[...] + (1.0 - ADAM_B2) * (g * g)
        g_ref[...] = g
        nm_ref[...] = nm
        nv_ref[...] = nv
        d_ref[...] = -ADAM_LR * ((nm / c1) / (jnp.sqrt(nv / c2) + ADAM_EPS) + ADAM_WD * w_ref[...])

    blk = lambda: pl.BlockSpec((tr, c), lambda i: (i, 0))
    return pl.pallas_call(
        body, name=name, grid=(r // tr,),
        in_specs=[pl.BlockSpec((N_DEV, tr, c), lambda i: (0, i, 0)), blk(), blk(), blk()],
        out_specs=[blk(), blk(), blk(), blk()],
        out_shape=[jax.ShapeDtypeStruct((r, c), F32)] * 4,
        compiler_params=_params(("parallel",)),
    )(parts, w, m, v)


def _adam_layers(parts, w, m, v, name, tr, rider=None):
    nl, r, c = w.shape
    pieces = [p for layer in parts for p in layer]
    rp = pieces[0].shape[1]
    tr = min(tr, rp)
    nr, nrp = r // tr, rp // tr
    c1 = 1.0 - ADAM_B1 ** ADAM_STEP
    c2 = 1.0 - ADAM_B2 ** ADAM_STEP

    def body(*refs):
        p_refs, (w_ref, m_ref, v_ref, g_ref, d_ref, nm_ref, nv_ref) = refs[:len(pieces)], refs[len(pieces):]
        at = pl.program_id(0) * nr + pl.program_id(1)
        for j in range(len(pieces)):
            @pl.when(jnp.logical_and(at >= j * nrp, at < (j + 1) * nrp))
            def _(p_ref=p_refs[j]):
                g = p_ref[0].astype(F32)
                for i in range(1, N_DEV):
                    g = g + p_ref[i].astype(F32)
                nm = ADAM_B1 * m_ref[...] + (1.0 - ADAM_B1) * g
                nv = ADAM_B2 * v_ref[...] + (1.0 - ADAM_B2) * (g * g)
                g_ref[...] = g
                nm_ref[...] = nm
                nv_ref[...] = nv
                d_ref[...] = -ADAM_LR * ((nm / c1) / (jnp.sqrt(nv / c2) + ADAM_EPS) + ADAM_WD * w_ref[...])

    def part_spec(j):
        return pl.BlockSpec((N_DEV, tr, c), lambda ll, i: (0, jnp.clip(ll * nr + i - j * nrp, 0, nrp - 1), 0))

    blk = lambda: pl.BlockSpec((None, tr, c), lambda ll, i: (ll, i, 0))
    return _call(body, name, (nl, nr), [part_spec(j) for j in range(len(pieces))] + [blk(), blk(), blk()],
                 [blk(), blk(), blk(), blk()], [jax.ShapeDtypeStruct((nl, r, c), F32)] * 4, [],
                 ("arbitrary", "arbitrary"), (*pieces, w, m, v), rider)


REPLICATED = ("norm_g", "ret_norm_g", "gla_ba_f", "gla_ba_b", "gla_norm_g", "pool_w", "pool_scale",
              "mla_q_norm_g", "mla_kv_norm_g", "mla_qk_norm_q", "mla_qk_norm_k")
SMALL_SHARDED = ("mla_wq_b", "mla_wkv_b", "gla_wa2_f", "gla_wa2_b")
WEIGHTS = ("norm_g", "w_in", "ret_norm_g", "gla_wa2_f", "gla_ba_f", "gla_wa2_b", "gla_ba_b", "gla_norm_g", "pool_w",
           "pool_scale", "mla_q_norm_g", "mla_wq_b", "mla_kv_norm_g", "mla_wkv_b", "mla_qk_norm_q", "mla_qk_norm_k",
           "w_out")


def _pack(arrays, dtype):
    flat = jnp.concatenate([a.reshape(-1) for a in arrays]).astype(dtype)
    return flat.reshape(-1, LANES)


def _unpack(packed, like):
    flat = packed.reshape(-1)
    out, at = [], 0
    for a in like:
        out.append(flat[at:at + a.size].reshape(a.shape))
        at += a.size
    return out


def _columns_by_device(g):
    l, r, n = g.shape
    return g.reshape(l, r, N_DEV, n // N_DEV).transpose(2, 0, 1, 3)


def _gathered_columns(g, l, r, c):
    return g.reshape(N_DEV, l, r, c).transpose(1, 2, 0, 3).reshape(l, r, N_DEV * c)


def _layer_forward(x, wts, late_wts, tables, tm, tq, ride_inproj=None, ride_attn=None, target=None):
    cos_r, sin_r, cos_m, sin_m, tab, _ = tables
    z, h, *carried_in = _inproj(x, wts["norm_g"], wts["w_in"], min(x.shape[0], 2 * tm), rider=ride_inproj)
    wts.update(late_wts(carried_in))
    o_a, y_a = _ret_fwd(z, cos_r, sin_r, tab, wts["ret_norm_g"])
    o_b, y_b = _gla_fwd(z, wts["wa_f"], wts["wa_b"], wts["gla_ba_f"], wts["gla_ba_b"], wts["gla_norm_g"])
    y_c = _pool_fwd(z, wts["pool_w"], wts["pool_scale"])
    q, k, v = _mla_prep(z, cos_m, sin_m, wts["mla_q_norm_g"], wts["wq"], wts["mla_kv_norm_g"], wts["wkv"],
                        wts["qk_q"], wts["qk_k"], tm)
    o_d, y_d, lse, *carried_attn = _attn_fwd(q, k, v, z, tq, rider=ride_attn)
    y = jnp.concatenate([y_a, y_b, y_c, y_d], axis=1)
    w_out = wts["w_out"]
    if target is None:
        x_next = _mm(y, w_out, "nn", "outproj", tm, D_MODEL, 1024, add=x)
    else:
        x_next = _mm(y, w_out, "nn", "outproj_loss", tm, D_MODEL, 1024, tail=_loss_tail(x, target))
    saved = dict(x=x, z=z, h=h, o_a=o_a, o_b=o_b, o_d=o_d, lse=lse, q=q, k=k, v=v, y=y, w_out=w_out)
    return x_next, saved, carried_in, carried_attn


def _layer_backward(dx, sv, wts, tables, tm, tq, rides):
    cos_r, sin_r, cos_m, sin_m, tab, tab_sw = tables
    z = sv["z"]
    g = {}
    carried = {}

    def rider(name):
        return rides[name](g) if name in rides else None

    def landed(name, results, n_own):
        if name in rides:
            carried[name] = results[n_own]
        return results[:n_own]

    g["w_out"] = _mm(sv["y"], dx, "tn", "d_w_out", 2048, 1024, 1024, out_dtype=BF16)
    dy = _mm(dx, sv["w_out"], "nt", "d_y", tm, 2048, 1024)

    do_a, dg_a, g["ret_norm_g"] = _normgate_bwd(sv["o_a"], z, A_G, dy, 0, wts["ret_norm_g"], tm)
    dq_a, dk_a, dv_a = landed("ret", _ret_bwd(z, do_a, cos_r, sin_r, tab, tab_sw, rider=rider("ret")), 3)

    do_b, dg_b, g["gla_norm_g"] = _normgate_bwd(sv["o_b"], z, B_G, dy, 1, wts["gla_norm_g"], tm)
    dq_b, dk_b, dv_b, d_ga, d_waf, d_wab, g["gla_ba_f"], g["gla_ba_b"] = landed("gla", _gla_bwd(
        z, do_b, wts["wa_f"], wts["wa_b"], wts["gla_ba_f"], wts["gla_ba_b"], rider=rider("gla")), 8)
    g["gla_wa2_f"] = d_waf[0:GLA_RANK]
    g["gla_wa2_b"] = d_wab[GLA_RANK:2 * GLA_RANK]

    du_c, dg_c, g["pool_w"], g["pool_scale"] = _pool_bwd(z, dy, wts["pool_w"], wts["pool_scale"])

    d_q, d_k, d_v, dg_d = landed("attn", _attn_bwd(sv["q"], sv["k"], sv["v"], z, sv["o_d"], sv["lse"], dy, tq,
                                                   rider=rider("attn")), 4)
    (d_mq, d_mkv, d_mkr, d_wq, g["mla_wkv_b"], g["mla_q_norm_g"], g["mla_kv_norm_g"], d_qg, d_kg) = _mla_prep_bwd(
        z, cos_m, sin_m, wts["mla_q_norm_g"], wts["wq"], wts["mla_kv_norm_g"], wts["wkv"], wts["qk_q"], wts["qk_k"],
        d_q, d_k, d_v, tm)
    g["mla_wq_b"] = _unpad_wq(d_wq)
    g["mla_qk_norm_q"] = d_qg[:, _QK_INV]
    g["mla_qk_norm_k"] = d_kg[:, _QK_INV]

    dz = jnp.concatenate([dq_a, dk_a, dv_a, dg_a, dq_b, dk_b, dv_b, dg_b, d_mq, du_c, dg_c, dg_d, d_mkv,
                          d_ga.astype(BF16), d_mkr], axis=1)
    h, half = sv["h"], D_MODEL // 2
    g["w_in_a"] = _split_w_in(_mm(h[:, :half], dz, "tn", "d_w_in_a", half, 1024, 1024, out_dtype=BF16))
    res = _mm(h[:, half:], dz, "tn", "d_w_in_b", half, 1024, 1024, out_dtype=BF16, rider=rider("d_w_in"))
    (d_w_in_b,) = landed("d_w_in", res if "d_w_in" in rides else [res], 1)
    g["w_in_b"] = _split_w_in(d_w_in_b)
    dx_in, g["norm_g"] = landed("d_h", _mm(dz, wts["w_in"], "nt", "d_h", tm, D_MODEL, 1024, rider=rider("d_h"),
                                           tail=_norm_bwd_tail(sv["x"], wts["norm_g"], dx)), 2)
    return dx_in, g, carried


def kernel(x, norm_g, w_in, ret_norm_g, gla_wa2_f, gla_ba_f, gla_wa2_b, gla_ba_b, gla_norm_g, pool_w, pool_scale, mla_q_norm_g, mla_wq_b, mla_kv_norm_g, mla_wkv_b, mla_qk_norm_q, mla_qk_norm_k, w_out, loss_target, m_norm_g, m_w_in, m_ret_norm_g, m_gla_wa2_f, m_gla_ba_f, m_gla_wa2_b, m_gla_ba_b, m_gla_norm_g, m_pool_w, m_pool_scale, m_mla_q_norm_g, m_mla_wq_b, m_mla_kv_norm_g, m_mla_wkv_b, m_mla_qk_norm_q, m_mla_qk_norm_k, m_w_out, v_norm_g, v_w_in, v_ret_norm_g, v_gla_wa2_f, v_gla_ba_f, v_gla_wa2_b, v_gla_ba_b, v_gla_norm_g, v_pool_w, v_pool_scale, v_mla_q_norm_g, v_mla_wq_b, v_mla_kv_norm_g, v_mla_wkv_b, v_mla_qk_norm_q, v_mla_qk_norm_k, v_w_out):
    w = dict(norm_g=norm_g, w_in=w_in, ret_norm_g=ret_norm_g, gla_wa2_f=gla_wa2_f, gla_ba_f=gla_ba_f,
             gla_wa2_b=gla_wa2_b, gla_ba_b=gla_ba_b, gla_norm_g=gla_norm_g, pool_w=pool_w, pool_scale=pool_scale,
             mla_q_norm_g=mla_q_norm_g, mla_wq_b=mla_wq_b, mla_kv_norm_g=mla_kv_norm_g, mla_wkv_b=mla_wkv_b,
             mla_qk_norm_q=mla_qk_norm_q, mla_qk_norm_k=mla_qk_norm_k, w_out=w_out)
    m = dict(norm_g=m_norm_g, w_in=m_w_in, ret_norm_g=m_ret_norm_g, gla_wa2_f=m_gla_wa2_f, gla_ba_f=m_gla_ba_f,
             gla_wa2_b=m_gla_wa2_b, gla_ba_b=m_gla_ba_b, gla_norm_g=m_gla_norm_g, pool_w=m_pool_w,
             pool_scale=m_pool_scale, mla_q_norm_g=m_mla_q_norm_g, mla_wq_b=m_mla_wq_b, mla_kv_norm_g=m_mla_kv_norm_g,
             mla_wkv_b=m_mla_wkv_b, mla_qk_norm_q=m_mla_qk_norm_q, mla_qk_norm_k=m_mla_qk_norm_k, w_out=m_w_out)
    v = dict(norm_g=v_norm_g, w_in=v_w_in, ret_norm_g=v_ret_norm_g, gla_wa2_f=v_gla_wa2_f, gla_ba_f=v_gla_ba_f,
             gla_wa2_b=v_gla_wa2_b, gla_ba_b=v_gla_ba_b, gla_norm_g=v_gla_norm_g, pool_w=v_pool_w,
             pool_scale=v_pool_scale, mla_q_norm_g=v_mla_q_norm_g, mla_wq_b=v_mla_wq_b, mla_kv_norm_g=v_mla_kv_norm_g,
             mla_wkv_b=v_mla_wkv_b, mla_qk_norm_q=v_mla_qk_norm_q, mla_qk_norm_k=v_mla_qk_norm_k, w_out=v_w_out)
    xs, target = x[0], loss_target[0]
    s = xs.shape[0]
    tm, tq = min(s, 512), min(s, 256)
    c_in = w_in.shape[2]

    w_in_b = w_in.astype(BF16)
    w_out_b = w_out.astype(BF16).reshape(-1, D_MODEL)
    (w_in_g0,) = _exchange([("gather", w_in_b[0])], "gather_first")
    tables = _rope_tables(s) + _ret_tables()
    offs = np.cumsum([0] + [w[n].size for n in SMALL_SHARDED])

    def early_weights(l, w_in_g):
        return dict(
            norm_g=norm_g[l][None], w_in=_assemble_w_in(w_in_g), ret_norm_g=ret_norm_g[l][None],
            gla_ba_f=gla_ba_f[l][None], gla_ba_b=gla_ba_b[l][None],
            gla_norm_g=gla_norm_g[l][None], pool_w=pool_w[l], pool_scale=pool_scale[l][None],
            mla_q_norm_g=mla_q_norm_g[l][None], mla_kv_norm_g=mla_kv_norm_g[l][None],
            qk_q=_pad_qk_gain(mla_qk_norm_q[l]), qk_k=_pad_qk_gain(mla_qk_norm_k[l]))

    def late_weights(l, w_out_g, small_g):
        flat = small_g.reshape(N_DEV, -1)
        full = {n: _gathered_columns(flat[:, offs[i]:offs[i + 1]], *w[n].shape)[l]
                for i, n in enumerate(SMALL_SHARDED)}
        wa_f = jnp.zeros((LANES, 2 * LANES), BF16).at[0:GLA_RANK].set(full["gla_wa2_f"])
        wa_b = jnp.zeros((LANES, 2 * LANES), BF16).at[GLA_RANK:2 * GLA_RANK].set(full["gla_wa2_b"])
        return dict(w_out=w_out_g.reshape(N_DEV, DEPTH, -1, D_MODEL)[:, l].reshape(-1, D_MODEL),
                    wa_f=wa_f, wa_b=wa_b, wq=_pad_wq(full["mla_wq_b"]), wkv=full["mla_wkv_b"])

    by_owner = lambda g_w_out: g_w_out.reshape(N_DEV, -1, D_MODEL)

    layers = [early_weights(0, w_in_g0), None]
    x1, sv0, (w_out_g, small_g), (w_in_g1,) = _layer_forward(
        xs, layers[0], lambda got: late_weights(0, *got), tables, tm, tq,
        ride_inproj=[("gather", w_out_b), ("gather", _pack([w[n] for n in SMALL_SHARDED], BF16))],
        ride_attn=("gather", w_in_b[1]))
    layers[1] = early_weights(1, w_in_g1)
    (dx, loss_row), sv1, _, _ = _layer_forward(x1, layers[1], lambda got: late_weights(1, w_out_g, small_g), tables,
                                               tm, tq, target=target)
    loss = lax.psum(loss_row[0, 0], ("x", "y", "c"))

    dx, g1, got1 = _layer_backward(dx, sv1, layers[1], tables, tm, tq, {
        "attn": lambda g: ("scatter", by_owner(g["w_out"])),
        "d_h": lambda g: ("scatter", g["w_in_a"])})
    dx, g0, got0 = _layer_backward(dx, sv0, layers[0], tables, tm, tq, {
        "ret": lambda g: ("scatter", by_owner(g["w_out"])),
        "gla": lambda g: ("scatter", g1["w_in_b"]),
        "d_w_in": lambda g: ("scatter", g["w_in_a"]),
        "d_h": lambda g: ("scatter", g["w_in_b"])})
    in_parts = ((got0["d_w_in"], got0["d_h"]), (got1["d_h"], got0["gla"]))
    out_parts = ((got0["ret"],), (got1["attn"],))
    grads = (g0, g1)
    full = {n: jnp.stack([grads[l][n].reshape(w[n].shape[1:]) if n in REPLICATED else grads[l][n]
                          for l in range(DEPTH)]) for n in SMALL_SHARDED + REPLICATED}
    small_c = jnp.concatenate([_columns_by_device(full[n]).reshape(N_DEV, -1) for n in SMALL_SHARDED], axis=1)
    out = {}
    *out["w_in"], small_parts, rep_parts = _adam_layers(
        in_parts, w_in, m_w_in, v_w_in, "adam_w_in", 256,
        rider=[("scatter", small_c.reshape(N_DEV, -1, LANES)), ("gather", _pack([full[n] for n in REPLICATED], F32))])
    out["w_out"] = _adam_layers(out_parts, w_out, m_w_out, v_w_out, "adam_w_out", 128)
    for names, parts, label in ((SMALL_SHARDED, small_parts, "adam_small"), (REPLICATED, rep_parts, "adam_replicated")):
        res = _adam(parts, _pack([w[n] for n in names], F32), _pack([m[n] for n in names], F32),
                    _pack([v[n] for n in names], F32), label, 2048)
        for n, *vals in zip(names, *[_unpack(a, [w[n] for n in names]) for a in res]):
            out[n] = vals

    return (loss, dx[None], *[out[n][0] for n in WEIGHTS], *[out[n][1] for n in WEIGHTS],
            *[out[n][2] for n in WEIGHTS], *[out[n][3] for n in WEIGHTS])
```

```python
import functools
import math

import numpy as np
import jax
import jax.numpy as jnp
from jax import lax
from jax.experimental import pallas as pl
from jax.experimental.pallas import tpu as pltpu

F32 = jnp.float32
BF16 = jnp.bfloat16

N_DEV = 8
D_MODEL = 2048
DEPTH = 2
GROUP_W = 512
EPS = 1e-6
ROPE_THETA = 10000.0
LANES = 128

RET_HD = 128
RET_CHUNK = 256
RET_UNROLL = 2
GLA_CHUNK = 64
GLA_UNROLL = 4
GLA_CUM_ROWS = 256
GLA_DK = 64
GLA_TAU = 16.0
GLA_RANK = 16
POOL_WINDOWS = (2, 4, 8, 16)
MLA_QK = 192
MLA_ROPE = 64
ATTN_SCALE = MLA_QK ** -0.5
ATTN_Q_SCALE = ATTN_SCALE * math.log2(math.e)
IN_COLS = 5984

ADAM_LR = 0.001
ADAM_B1 = 0.9
ADAM_B2 = 0.999
ADAM_EPS = 1e-08
ADAM_WD = 0.01
ADAM_STEP = 10

A_Q, A_K, A_V, A_G = 0, 4, 8, 12
B_Q, B_K, B_V, B_G = 16, 18, 20, 24
M_Q, C_V, C_G, M_G = 28, 32, 36, 40
M_KV, GA, M_KR = 44, 46, 47
ZP_COLS = 48 * LANES

VMEM_LIMIT = 56 * 1024 * 1024


def _params(sem, vmem=VMEM_LIMIT):
    return pltpu.CompilerParams(dimension_semantics=sem, vmem_limit_bytes=vmem)


def _sigmoid(x):
    return 1.0 / (1.0 + jnp.exp(-x))


def _silu(x):
    return x * _sigmoid(x)


def _silu_grad(x):
    s = _sigmoid(x)
    return s * (1.0 + x * (1.0 - s))


def _dot(a, b, dims=(((1,), (0,)), ((), ()))):
    return lax.dot_general(a, b, dims, preferred_element_type=F32)


NT = (((1,), (1,)), ((), ()))
TN = (((0,), (0,)), ((), ()))


def _chunk_loop(n, body, init, unroll):
    unroll = math.gcd(n, unroll)

    def trip(t, carry):
        for u in range(unroll):
            carry = body(t * unroll + u, carry)
        return carry

    return lax.fori_loop(0, n // unroll, trip, init)


def _roll_lanes_half(x):
    return pltpu.roll(x, 64, 1)


def _wq_perm():
    idx = np.zeros((1024,), np.int32)
    ok = np.zeros((1024,), bool)
    for h in range(4):
        idx[128 * h:128 * h + 128] = 192 * h + np.arange(128)
        ok[128 * h:128 * h + 128] = True
        base = 512 + 128 * h
        idx[base:base + 32] = 192 * h + 128 + np.arange(32)
        ok[base:base + 32] = True
        idx[base + 64:base + 96] = 192 * h + 160 + np.arange(32)
        ok[base + 64:base + 96] = True
    inv = np.zeros((768,), np.int32)
    inv[idx[ok]] = np.nonzero(ok)[0]
    return idx, ok, inv


_WQ_IDX, _WQ_OK, _WQ_INV = _wq_perm()


def _pad_wq(wq):
    return jnp.where(jnp.asarray(_WQ_OK)[None, :], wq[:, _WQ_IDX], 0).astype(wq.dtype)


def _unpad_wq(wqp):
    return wqp[:, _WQ_INV]


def _qk_idx():
    idx = np.zeros((256,), np.int32)
    ok = np.zeros((256,), bool)
    idx[0:128] = np.arange(128)
    ok[0:128] = True
    idx[128:160] = 128 + np.arange(32)
    ok[128:160] = True
    idx[192:224] = 160 + np.arange(32)
    ok[192:224] = True
    inv = np.zeros((192,), np.int32)
    inv[idx[ok]] = np.nonzero(ok)[0]
    return idx, ok, inv


_QK_IDX, _QK_OK, _QK_INV = _qk_idx()


def _pad_qk_gain(g):
    return jnp.where(jnp.asarray(_QK_OK), g[_QK_IDX], 0.0).reshape(1, 256)


def _rope_tables(s):
    def tabs(dim):
        inv = 1.0 / (ROPE_THETA ** (jnp.arange(0, dim, 2, dtype=F32) / dim))
        ang = jnp.arange(s, dtype=F32)[:, None] * inv[None, :]
        return jnp.cos(ang), jnp.sin(ang)
    cr, sr = tabs(RET_HD)
    cos_r = jnp.concatenate([cr, cr], axis=1)
    sin_r = jnp.concatenate([-sr, sr], axis=1)
    cm, sm = tabs(MLA_ROPE)
    zz = jnp.zeros_like(cm)
    cos_m = jnp.concatenate([cm, zz, cm, zz], axis=1)
    sin_m = jnp.concatenate([-sm, zz, sm, zz], axis=1)
    return cos_r, sin_r, cos_m, sin_m


def _rope(x, cos, sin):
    return x * cos + _roll_lanes_half(x) * sin


def _rope_t(x, cos, sin):
    return x * cos + _roll_lanes_half(x * sin)


def _ret_tables():
    c = RET_CHUNK
    gamma_f = 1.0 - 2.0 ** (-5.0 - jnp.arange(4, dtype=F32))
    gamma_b = gamma_f[::-1]
    idx = jnp.arange(c, dtype=F32)
    diff = idx[:, None] - idx[None, :]

    def build(g1, g2):
        l1 = jnp.log(g1)[:, None, None]
        l2 = jnp.log(g2)[:, None, None]
        d1 = jnp.where(diff >= 0, jnp.exp(jnp.maximum(diff, 0.0)[None] * l1), 0.0)
        d2 = jnp.where(diff <= 0, jnp.exp(jnp.maximum(-diff, 0.0)[None] * l2), 0.0)
        ones = jnp.ones((1, c, LANES), F32)
        col = idx[None, :, None]
        qdf = jnp.exp((col + 1.0) * l1) * ones
        kdf = jnp.exp((c - 1.0 - col) * l1) * ones
        qdb = jnp.exp((c - col) * l2) * ones
        kdb = jnp.exp(col * l2) * ones
        cd1 = jnp.exp(c * l1) * ones
        cd2 = jnp.exp(c * l2) * ones
        return jnp.concatenate([d1 + d2, qdf, kdf, qdb, kdb, cd1, cd2], axis=2)

    return build(gamma_f, gamma_b), build(gamma_b, gamma_f)


MESH = pl.DeviceIdType.MESH
ANY = pl.BlockSpec(memory_space=pl.ANY)
_RELATIONS = ((0, 0, 1), (1, 0, 0), (0, 1, 0), (1, 1, 0), (1, 0, 1), (0, 1, 1), (1, 1, 1))


def _position():
    return lax.axis_index("x"), lax.axis_index("y"), lax.axis_index("c")


def _gather_copies(x_ref, out_ref, send_sems, recv_sems, local_sem, starting):
    x, y, cc = _position()
    me, sibling = (x, y, cc), (x, y, 1 - cc)
    chips = [(1 - x, y), (x, 1 - y), (1 - x, 1 - y)]

    def slab(px, py, pc):
        return out_ref.at[4 * px + 2 * py + pc]

    def copy(k, block, to, src=None):
        return pltpu.make_async_remote_copy(
            src_ref=slab(*block) if src is None else src, dst_ref=slab(*block),
            send_sem=send_sems.at[k], recv_sem=recv_sems.at[k], device_id=to, device_id_type=MESH)

    mine = pltpu.make_async_copy(x_ref, slab(*me), local_sem)
    first = [copy(0, me, sibling, src=x_ref)] + [copy(1 + j, me, (*chip, cc), src=x_ref) for j, chip in enumerate(chips)]
    if starting:
        return mine, first
    passed = [copy(4 + j, (*chip, cc), sibling) for j, chip in enumerate(chips)]
    arrivals = [copy(1 + j, (*chip, cc), me) for j, chip in enumerate(chips)]
    late = [copy(0, sibling, me)] + [copy(4 + j, (*chip, 1 - cc), me) for j, chip in enumerate(chips)]
    return mine, first, passed, arrivals, late


def _gather_start(*refs):
    mine, first = _gather_copies(*refs, starting=True)
    mine.start()
    for cp in first:
        cp.start()


def _gather_finish(*refs):
    mine, first, passed, arrivals, late = _gather_copies(*refs, starting=False)
    for arrived, onward in zip(arrivals, passed):
        arrived.wait_recv()
        onward.start()
    for cp in late:
        cp.wait_recv()
    for cp in first + passed:
        cp.wait_send()
    mine.wait()


def _scatter_copies(c_ref, out_ref, send_sems, recv_sems, local_sem):
    x, y, cc = _position()
    me = 4 * x + 2 * y + cc
    mine = pltpu.make_async_copy(c_ref.at[me], out_ref.at[me], local_sem)
    copies = []
    for k, (fx, fy, fc) in enumerate(_RELATIONS):
        px = 1 - x if fx else x
        py = 1 - y if fy else y
        pc = 1 - cc if fc else cc
        copies.append(pltpu.make_async_remote_copy(
            src_ref=c_ref.at[4 * px + 2 * py + pc], dst_ref=out_ref.at[me],
            send_sem=send_sems.at[k], recv_sem=recv_sems.at[k], device_id=(px, py, pc), device_id_type=MESH))
    return mine, copies


def _scatter_start(*refs):
    mine, copies = _scatter_copies(*refs)
    mine.start()
    for cp in copies:
        cp.start()


def _scatter_finish(*refs):
    mine, copies = _scatter_copies(*refs)
    for cp in copies:
        cp.wait()
    mine.wait()


_EXCHANGES = {"gather": (_gather_start, _gather_finish), "scatter": (_scatter_start, _scatter_finish)}


def _exchange_scratch():
    return [pltpu.SemaphoreType.DMA((7,)), pltpu.SemaphoreType.DMA((7,)), pltpu.SemaphoreType.DMA]


def _exchange_out(kind, src):
    return jax.ShapeDtypeStruct(((N_DEV,) + src.shape) if kind == "gather" else src.shape, src.dtype)


def _exchange(jobs, name):
    n = len(jobs)

    def body(*refs):
        srcs, outs, sems = refs[:n], refs[n:2 * n], refs[2 * n:]
        for half in (0, 1):
            for i, (kind, _) in enumerate(jobs):
                _EXCHANGES[kind][half](srcs[i], outs[i], *sems[3 * i:3 * i + 3])

    return pl.pallas_call(
        body, name=name, out_shape=[_exchange_out(kind, src) for kind, src in jobs],
        in_specs=[ANY] * n, out_specs=[ANY] * n,
        scratch_shapes=[sem for _ in jobs for sem in _exchange_scratch()])(*[src for _, src in jobs])


def _call(body, name, grid, in_specs, out_specs, out_shape, scratch, sem, args, rider=None):
    if rider is None:
        return pl.pallas_call(body, name=name, grid=grid, in_specs=in_specs, out_specs=out_specs, out_shape=out_shape,
                              scratch_shapes=scratch, compiler_params=_params(sem))(*args)
    jobs = rider if isinstance(rider, list) else [rider]
    ni, no, ns, nj = len(in_specs), len(out_specs), len(scratch), len(jobs)

    def carried(*refs):
        ins, rsrcs = refs[:ni], refs[ni:ni + nj]
        outs, routs = refs[ni + nj:ni + nj + no], refs[ni + nj + no:ni + 2 * nj + no]
        scr, sems = refs[ni + 2 * nj + no:ni + 2 * nj + no + ns], refs[ni + 2 * nj + no + ns:]
        ids = [pl.program_id(a) for a in range(len(grid))]
        is_first = functools.reduce(jnp.logical_and, [i == 0 for i in ids])
        is_last = functools.reduce(jnp.logical_and, [i == g - 1 for i, g in zip(ids, grid)])

        def half(which):
            for j, (kind, _) in enumerate(jobs):
                _EXCHANGES[kind][which](rsrcs[j], routs[j], *sems[3 * j:3 * j + 3])

        @pl.when(is_first)
        def _():
            half(0)

        body(*ins, *outs, *scr)

        @pl.when(is_last)
        def _():
            half(1)

    return pl.pallas_call(
        carried, name=name, grid=grid, in_specs=list(in_specs) + [ANY] * nj, out_specs=list(out_specs) + [ANY] * nj,
        out_shape=list(out_shape) + [_exchange_out(kind, src) for kind, src in jobs],
        scratch_shapes=list(scratch) + [sem for _ in jobs for sem in _exchange_scratch()],
        compiler_params=_params(("arbitrary",) * len(grid)))(*args, *[src for _, src in jobs])


def _inproj(x, g, wp, tm, tn=512, rider=None):
    s, d = x.shape
    n = wp.shape[1]

    def body(x_ref, g_ref, w_ref, z_ref, h_ref, hs):
        @pl.when(pl.program_id(1) == 0)
        def _():
            xv = x_ref[...]
            r = lax.rsqrt(jnp.mean(xv * xv, axis=-1, keepdims=True) + EPS)
            hv = (xv * r * g_ref[...]).astype(BF16)
            hs[...] = hv
            h_ref[...] = hv
        z_ref[...] = _dot(hs[...], w_ref[...])

    return _call(
        body, "inproj", (s // tm, n // tn),
        [pl.BlockSpec((tm, d), lambda i, j: (i, 0)),
         pl.BlockSpec((1, d), lambda i, j: (0, 0)),
         pl.BlockSpec((d, tn), lambda i, j: (0, j))],
        [pl.BlockSpec((tm, tn), lambda i, j: (i, j)), pl.BlockSpec((tm, d), lambda i, j: (i, 0))],
        [jax.ShapeDtypeStruct((s, n), F32), jax.ShapeDtypeStruct((s, d), BF16)],
        [pltpu.VMEM((tm, d), BF16)], ("parallel", "arbitrary"), (x, g, wp), rider)


def _relayout_plan():
    runs = ((0, 3584, 0), (3584, 3616, GA * LANES), (3616, 4640, C_V * LANES), (4640, 5152, M_Q * LANES),
            (5152, 5408, M_KV * LANES), (5408, 5440, M_KR * LANES), (5440, 5472, M_KR * LANES + 64),
            (5472, 5984, M_G * LANES))
    shard = IN_COLS // N_DEV
    plan = []
    for d in range(N_DEV):
        lo, hi = shard * d, shard * (d + 1)
        for a, b, p in runs:
            s, e = max(a, lo), min(b, hi)
            if s < e:
                plan.append((d, s - lo, p + (s - a), e - s))
    return plan


def _assemble_w_in(g, tr=256):
    _, r, c = g.shape

    def body(g_ref, o_ref):
        o_ref[...] = jnp.zeros_like(o_ref)
        for d, at, to, w in _relayout_plan():
            o_ref[:, to:to + w] = g_ref[d, :, at:at + w]

    return pl.pallas_call(
        body, name="assemble_w_in", grid=(r // tr,),
        in_specs=[pl.BlockSpec((N_DEV, tr, c), lambda i: (0, i, 0))],
        out_specs=pl.BlockSpec((tr, ZP_COLS), lambda i: (i, 0)),
        out_shape=jax.ShapeDtypeStruct((r, ZP_COLS), g.dtype),
        compiler_params=_params(("parallel",)),
    )(g)


def _split_w_in(wp, tr=256):
    r = wp.shape[0]
    c = IN_COLS // N_DEV

    def body(w_ref, o_ref):
        for d, at, to, w in _relayout_plan():
            o_ref[d, :, at:at + w] = w_ref[:, to:to + w]

    return pl.pallas_call(
        body, name="split_w_in", grid=(r // tr,),
        in_specs=[pl.BlockSpec((tr, ZP_COLS), lambda i: (i, 0))],
        out_specs=pl.BlockSpec((N_DEV, tr, c), lambda i: (0, i, 0)),
        out_shape=jax.ShapeDtypeStruct((N_DEV, r, c), wp.dtype),
        compiler_params=_params(("parallel",)),
    )(wp)


def _mm(a, b, mode, name, tm, tn, tk, add=None, out_dtype=F32, rider=None, tail=None):
    if mode == "tn":
        k, m = a.shape
    else:
        m, k = a.shape
    n = b.shape[0] if mode == "nt" else b.shape[1]
    tm, tn, tk = min(tm, m), min(tn, n), min(tk, k)
    nk = k // tk
    dims = {"nn": (((1,), (0,)), ((), ())), "nt": NT, "tn": TN}[mode]
    if tail is None:
        def plain(acc, i, extra_refs, out_refs):
            out_refs[0][...] = (acc + extra_refs[0][...] if extra_refs else acc).astype(out_dtype)
        tail = ([(add, "tile")] if add is not None else [], [(out_dtype, "tile")], plain)
    extra, outs, fn = tail
    spec = {"tile": pl.BlockSpec((tm, tn), lambda i, j, kk: (i, j)),
            "row": pl.BlockSpec((1, tn), lambda i, j, kk: (0, j)),
            "lanes": pl.BlockSpec((1, LANES), lambda i, j, kk: (0, 0))}
    shape = {"tile": (m, n), "row": (1, n), "lanes": (1, LANES)}
    ne, no = len(extra), len(outs)

    def body(*refs):
        a_ref, b_ref = refs[:2]
        extra_refs, out_refs, acc = refs[2:2 + ne], refs[2 + ne:2 + ne + no], refs[2 + ne + no]
        i, kk = pl.program_id(0), pl.program_id(2)

        @pl.when(kk == 0)
        def _():
            acc[...] = jnp.zeros_like(acc)

        acc[...] += _dot(a_ref[...].astype(BF16), b_ref[...].astype(BF16), dims)

        @pl.when(kk == nk - 1)
        def _():
            fn(acc[...], i, extra_refs, out_refs)

    a_spec = (pl.BlockSpec((tk, tm), lambda i, j, kk: (kk, i)) if mode == "tn"
              else pl.BlockSpec((tm, tk), lambda i, j, kk: (i, kk)))
    b_spec = (pl.BlockSpec((tn, tk), lambda i, j, kk: (j, kk)) if mode == "nt"
              else pl.BlockSpec((tk, tn), lambda i, j, kk: (kk, j)))
    summed = any(kind != "tile" for _, kind in outs)
    res = _call(body, name, (m // tm, n // tn, nk), [a_spec, b_spec] + [spec[kind] for _, kind in extra],
                [spec[kind] for _, kind in outs], [jax.ShapeDtypeStruct(shape[kind], dt) for dt, kind in outs],
                [pltpu.VMEM((tm, tn), F32)], ("arbitrary",) * 3 if summed else ("parallel", "parallel", "arbitrary"),
                [a, b] + [arr for arr, _ in extra], rider)
    return res[0] if (rider is None and no == 1) else res


def _norm_bwd_tail(x, g, dres):
    def fn(dh, i, extra_refs, out_refs):
        x_ref, g_ref, dres_ref = extra_refs
        dx_ref, dg_ref = out_refs

        @pl.when(i == 0)
        def _():
            dg_ref[...] = jnp.zeros_like(dg_ref)

        xv = x_ref[...]
        r = lax.rsqrt(jnp.mean(xv * xv, axis=-1, keepdims=True) + EPS)
        nv = xv * r
        dg_ref[...] += jnp.sum(dh * nv, axis=0, keepdims=True)
        u = dh * g_ref[...]
        dx_ref[...] = dres_ref[...] + r * (u - nv * jnp.mean(u * nv, axis=-1, keepdims=True))

    return [(x, "tile"), (g, "row"), (dres, "tile")], [(F32, "tile"), (F32, "row")], fn


def _loss_tail(x, target):
    d = x.shape[1]

    def fn(acc, i, extra_refs, out_refs):
        x_ref, t_ref = extra_refs
        dx_ref, loss_ref = out_refs

        @pl.when(i == 0)
        def _():
            loss_ref[...] = jnp.zeros_like(loss_ref)

        err = acc + x_ref[...] - t_ref[...]
        dx_ref[...] = err * (1.0 / d)
        per_tok = jnp.mean(err * err, axis=-1, keepdims=True)
        loss_ref[...] += 0.5 * jnp.sum(per_tok, axis=0, keepdims=True)

    return [(x, "tile"), (target, "tile")], [(F32, "tile"), (F32, "lanes")], fn


def _ret_core(q_ref, k_ref, v_ref, tab_ref, out_ref, back_ref, nchunk):
    c = RET_CHUNK

    def rows(n):
        return pl.ds(pl.multiple_of(n * c, c), c)

    zero = jnp.zeros((LANES, LANES), F32)

    def plane(i, n=c):
        return tab_ref[0:n, c + LANES * i:c + LANES * (i + 1)]

    def fwd(n, st):
        r = rows(n)
        q, k, vb = q_ref[r, :], k_ref[r, :], v_ref[r, :].astype(BF16)
        sc = _dot(q.astype(BF16), k.astype(BF16), NT) * tab_ref[:, 0:c]
        o = _dot(sc.astype(BF16), vb)
        o = o + _dot((q * plane(0)).astype(BF16), st.astype(BF16))
        out_ref[r, :] = o
        return st * plane(4, LANES) + _dot((k * plane(1)).astype(BF16), vb, TN)

    def bwd(i, st):
        r = rows(nchunk - 1 - i)
        q, k, vb = q_ref[r, :], k_ref[r, :], v_ref[r, :].astype(BF16)
        back_ref[r, :] = _dot((q * plane(2)).astype(BF16), st.astype(BF16))
        return st * plane(5, LANES) + _dot((k * plane(3)).astype(BF16), vb, TN)

    def both(i, states):
        return fwd(i, states[0]), bwd(i, states[1])

    _chunk_loop(nchunk, both, (zero, zero), RET_UNROLL)
    out_ref[...] += back_ref[...]


def _ret_fwd(z, cos_r, sin_r, tab, norm_g):
    s = z.shape[0]
    nchunk = s // RET_CHUNK
    scale = RET_HD ** -0.5
    col = lambda base: pl.BlockSpec((s, LANES), lambda h: (0, base + h), pipeline_mode=pl.Buffered(1))

    def body(q_ref, k_ref, v_ref, g_ref, cos_ref, sin_ref, tab_ref, ng_ref, o_ref, y_ref, qh, kh, back):
        qh[...] = _rope(q_ref[...], cos_ref[...], sin_ref[...])
        kh[...] = _rope(k_ref[...], cos_ref[...], sin_ref[...]) * scale
        _ret_core(qh, kh, v_ref, tab_ref, o_ref, back, nchunk)
        o = o_ref[...]
        r = lax.rsqrt(jnp.mean(o * o, axis=-1, keepdims=True) + EPS)
        y_ref[...] = (_silu(g_ref[...]) * (o * r * ng_ref[...])).astype(BF16)

    return pl.pallas_call(
        body, name="ret_fwd", grid=(4,),
        in_specs=[col(A_Q), col(A_K), col(A_V), col(A_G),
                  pl.BlockSpec((s, LANES), lambda h: (0, 0), pipeline_mode=pl.Buffered(1)),
                  pl.BlockSpec((s, LANES), lambda h: (0, 0), pipeline_mode=pl.Buffered(1)),
                  pl.BlockSpec((None, RET_CHUNK, RET_CHUNK + 6 * LANES), lambda h: (h, 0, 0)),
                  pl.BlockSpec((1, LANES), lambda h: (0, h))],
        out_specs=[pl.BlockSpec((s, LANES), lambda h: (0, h)), pl.BlockSpec((s, LANES), lambda h: (0, h))],
        out_shape=[jax.ShapeDtypeStruct((s, GROUP_W), F32), jax.ShapeDtypeStruct((s, GROUP_W), BF16)],
        scratch_shapes=[pltpu.VMEM((s, LANES), F32)] * 3,
        compiler_params=_params(("arbitrary",)),
    )(z, z, z, z, cos_r, sin_r, tab, norm_g)


def _ret_bwd(z, d_o, cos_r, sin_r, tab, tab_sw, rider=None):
    s = z.shape[0]
    nchunk = s // RET_CHUNK
    scale = RET_HD ** -0.5
    col = lambda base: pl.BlockSpec((s, LANES), lambda h: (0, base + h), pipeline_mode=pl.Buffered(1))
    whole = lambda: pl.BlockSpec((s, LANES), lambda h: (0, 0), pipeline_mode=pl.Buffered(1))
    tabspec = lambda: pl.BlockSpec((None, RET_CHUNK, RET_CHUNK + 6 * LANES), lambda h: (h, 0, 0))
    outspec = lambda: pl.BlockSpec((s, LANES), lambda h: (0, h))

    def body(q_ref, k_ref, v_ref, do_ref, cos_ref, sin_ref, tab_ref, tsw_ref, dq_ref, dk_ref, dv_ref,
             qh, kh, tmp, back):
        cos, sin = cos_ref[...], sin_ref[...]
        qh[...] = _rope(q_ref[...], cos, sin)
        kh[...] = _rope(k_ref[...], cos, sin) * scale
        _ret_core(kh, qh, do_ref, tsw_ref, tmp, back, nchunk)
        dv_ref[...] = tmp[...].astype(BF16)
        _ret_core(do_ref, v_ref, kh, tab_ref, tmp, back, nchunk)
        dq_ref[...] = _rope_t(tmp[...], cos, sin).astype(BF16)
        _ret_core(v_ref, do_ref, qh, tsw_ref, tmp, back, nchunk)
        dk_ref[...] = _rope_t(tmp[...] * scale, cos, sin).astype(BF16)

    return _call(
        body, "ret_bwd", (4,),
        [col(A_Q), col(A_K), col(A_V),
         pl.BlockSpec((s, LANES), lambda h: (0, h), pipeline_mode=pl.Buffered(1)),
         whole(), whole(), tabspec(), tabspec()],
        [outspec(), outspec(), outspec()],
        [jax.ShapeDtypeStruct((s, GROUP_W), BF16)] * 3,
        [pltpu.VMEM((s, LANES), F32)] * 4,
        ("arbitrary",), (z, z, z, d_o, cos_r, sin_r, tab, tab_sw), rider)


def _normgate_bwd(o, z, gate_blk, dy, dy_blk, norm_g, tm):
    s = o.shape[0]

    def body(o_ref, g_ref, dy_ref, ng_ref, do_ref, dg_ref, dng_ref):
        @pl.when(pl.program_id(0) == 0)
        def _():
            dng_ref[...] = jnp.zeros_like(dng_ref)

        for h in range(4):
            sl = slice(LANES * h, LANES * (h + 1))
            ov, gv, dyv, ng = o_ref[:, sl], g_ref[:, sl], dy_ref[:, sl], ng_ref[:, sl]
            r = lax.rsqrt(jnp.mean(ov * ov, axis=-1, keepdims=True) + EPS)
            on = ov * r
            dn = dyv * _silu(gv)
            u = dn * ng
            do_ref[:, sl] = r * (u - on * jnp.mean(u * on, axis=-1, keepdims=True))
            dg_ref[:, sl] = (dyv * (on * ng) * _silu_grad(gv)).astype(BF16)
            dng_ref[:, sl] += jnp.sum(dn * on, axis=0, keepdims=True)

    return pl.pallas_call(
        body, name="normgate_bwd", grid=(s // tm,),
        in_specs=[pl.BlockSpec((tm, GROUP_W), lambda i: (i, 0)),
                  pl.BlockSpec((tm, GROUP_W), lambda i: (i, gate_blk // 4)),
                  pl.BlockSpec((tm, GROUP_W), lambda i: (i, dy_blk)),
                  pl.BlockSpec((1, GROUP_W), lambda i: (0, 0))],
        out_specs=[pl.BlockSpec((tm, GROUP_W), lambda i: (i, 0)), pl.BlockSpec((tm, GROUP_W), lambda i: (i, 0)),
                   pl.BlockSpec((1, GROUP_W), lambda i: (0, 0))],
        out_shape=[jax.ShapeDtypeStruct((s, GROUP_W), F32), jax.ShapeDtypeStruct((s, GROUP_W), BF16),
                   jax.ShapeDtypeStruct((1, GROUP_W), F32)],
        compiler_params=_params(("arbitrary",)),
    )(o, z, dy, norm_g)


def _log_sigmoid(x):
    return jnp.minimum(x, 0.0) - jnp.log(1.0 + jnp.exp(-jnp.abs(x)))


def _gla_consts():
    c = GLA_CHUNK
    row = lax.broadcasted_iota(jnp.int32, (c, c), 0)
    colm = lax.broadcasted_iota(jnp.int32, (c, c), 1)
    lane = lax.broadcasted_iota(jnp.int32, (1, LANES), 1)
    low = row >= colm
    up = colm >= row
    heads = ((lane < GLA_DK).astype(F32), (lane >= GLA_DK).astype(F32))
    return low, up, heads


def _chunk_tri(upper):
    r = lax.broadcasted_iota(jnp.int32, (GLA_CUM_ROWS, GLA_CUM_ROWS), 0)
    c = lax.broadcasted_iota(jnp.int32, (GLA_CUM_ROWS, GLA_CUM_ROWS), 1)
    shift = GLA_CHUNK.bit_length() - 1
    same = jnp.right_shift(r, shift) == jnp.right_shift(c, shift)
    return jnp.where(jnp.logical_and(same, (c >= r) if upper else (r >= c)), 1.0, 0.0).astype(BF16)


def _exact_tri_matmul(tri, x):
    hi = x.astype(BF16)
    rest = x - hi.astype(F32)
    mid = rest.astype(BF16)
    lo = (rest - mid.astype(F32)).astype(BF16)
    return _dot(tri, hi) + _dot(tri, mid) + _dot(tri, lo)


def _gla_chunk(cum_ref, d, n):
    c = GLA_CHUNK
    cum = cum_ref[d, pl.ds(pl.multiple_of(n * c, c), c), :]
    last = cum_ref[d, pl.ds(n * c + (c - 1 if d == 0 else 0), 1), :]
    eq = jnp.exp(cum)
    ek = jnp.exp(-cum)
    el = jnp.exp(last - cum)
    dec = jnp.exp(last)
    return eq, ek, el, dec


def _gla_gates(ga_ref, wa_ref, ba_ref, cum_ref, s, upper):
    tri = _chunk_tri(upper)
    rows = min(s, GLA_CUM_ROWS)

    def step(i, carry):
        r = pl.ds(pl.multiple_of(i * rows, rows), rows)
        pre = _dot(ga_ref[r, :].astype(BF16), wa_ref[...].astype(BF16)) + ba_ref[...]
        cum_ref[r, :] = _exact_tri_matmul(tri[0:rows, 0:rows], _log_sigmoid(pre) * (1.0 / GLA_TAU))
        return carry
    lax.fori_loop(0, s // rows, step, 0)


def _gla_fwd(z, wa_f, wa_b, ba_f, ba_b, norm_g):
    s = z.shape[0]
    c = GLA_CHUNK
    nchunk = s // c
    scale = GLA_DK ** -0.5
    tm = min(s, 512)
    one = pl.Buffered(1)

    def body(q_ref, k_ref, v_ref, ga_ref, g_ref, waf_ref, wab_ref, baf_ref, bab_ref, ng_ref, o_ref, y_ref,
             la_s):
        low, up, heads = _gla_consts()
        _gla_gates(ga_ref, waf_ref, baf_ref, la_s.at[0], s, False)
        _gla_gates(ga_ref, wab_ref, bab_ref, la_s.at[1], s, True)
        for d in range(2):
            tri = (low, up)[d]

            def step(i, states):
                n = i if d == 0 else nchunk - 1 - i
                r = pl.ds(pl.multiple_of(n * c, c), c)
                q = q_ref[r, :] * scale
                k = k_ref[r, :]
                eq, ek, el, dec = _gla_chunk(la_s, d, n)
                qt = q * eq
                ktb = (k * ek).astype(BF16)
                kl = k * el
                new_states = []
                for hh in range(2):
                    cols = slice(LANES * hh, LANES * (hh + 1))
                    vb = v_ref[r, cols].astype(BF16)
                    qm = (qt * heads[hh]).astype(BF16)
                    a = jnp.where(tri, _dot(qm, ktb, NT), 0.0)
                    o = _dot(a.astype(BF16), vb) + _dot(qm, states[hh].astype(BF16), NT)
                    if d == 0:
                        o_ref[r, cols] = o
                    else:
                        o_ref[r, cols] += o
                    new_states.append(states[hh] * dec + _dot(vb, (kl * heads[hh]).astype(BF16), TN))
                return tuple(new_states)

            zero = jnp.zeros((LANES, LANES), F32)
            _chunk_loop(nchunk, step, (zero, zero), GLA_UNROLL)

        def epi(i, carry):
            r = pl.ds(pl.multiple_of(i * tm, tm), tm)
            for hh in range(2):
                cols = slice(LANES * hh, LANES * (hh + 1))
                o = o_ref[r, cols]
                rr = lax.rsqrt(jnp.mean(o * o, axis=-1, keepdims=True) + EPS)
                y_ref[r, cols] = (_silu(g_ref[r, cols]) * (o * rr * ng_ref[:, cols])).astype(BF16)
            return carry

        lax.fori_loop(0, s // tm, epi, 0)

    w2 = 2 * LANES
    return pl.pallas_call(
        body, name="gla_fwd", grid=(2,),
        in_specs=[pl.BlockSpec((s, LANES), lambda p: (0, B_Q + p), pipeline_mode=one),
                  pl.BlockSpec((s, LANES), lambda p: (0, B_K + p), pipeline_mode=one),
                  pl.BlockSpec((s, w2), lambda p: (0, B_V // 2 + p), pipeline_mode=one),
                  pl.BlockSpec((s, LANES), lambda p: (0, GA), pipeline_mode=one),
                  pl.BlockSpec((s, w2), lambda p: (0, B_G // 2 + p), pipeline_mode=one),
                  pl.BlockSpec((LANES, LANES), lambda p: (0, p)),
                  pl.BlockSpec((LANES, LANES), lambda p: (0, p)),
                  pl.BlockSpec((1, LANES), lambda p: (0, p)),
                  pl.BlockSpec((1, LANES), lambda p: (0, p)),
                  pl.BlockSpec((1, w2), lambda p: (0, p))],
        out_specs=[pl.BlockSpec((s, w2), lambda p: (0, p)), pl.BlockSpec((s, w2), lambda p: (0, p))],
        out_shape=[jax.ShapeDtypeStruct((s, GROUP_W), F32), jax.ShapeDtypeStruct((s, GROUP_W), BF16)],
        scratch_shapes=[pltpu.VMEM((2, s, LANES), F32)],
        compiler_params=_params(("arbitrary",)),
    )(z, z, z, z, z, wa_f, wa_b, ba_f, ba_b, norm_g)


def _gla_bwd(z, d_o, wa_f, wa_b, ba_f, ba_b, rider=None):
    s = z.shape[0]
    c = GLA_CHUNK
    nchunk = s // c
    scale = GLA_DK ** -0.5
    tm = min(s, GLA_CUM_ROWS)
    one = pl.Buffered(1)

    def body(q_ref, k_ref, v_ref, ga_ref, do_ref, waf_ref, wab_ref, baf_ref, bab_ref,
             dq_ref, dk_ref, dv_ref, dga_ref, dwaf_ref, dwab_ref, dbaf_ref, dbab_ref,
             la_s, dla_s, stash, dq_s, dk_s, dv_s):
        low, up, heads = _gla_consts()
        rowi = lax.broadcasted_iota(jnp.int32, (c, 1), 0)
        _gla_gates(ga_ref, waf_ref, baf_ref, la_s.at[0], s, False)
        _gla_gates(ga_ref, wab_ref, bab_ref, la_s.at[1], s, True)
        for d in range(2):
            tri = (low, up)[d]
            last_row = (rowi == (c - 1 if d == 0 else 0)).astype(F32)
            order = (lambda i: i) if d == 0 else (lambda i: nchunk - 1 - i)
            zero = jnp.zeros((LANES, LANES), F32)

            def states(i, sts):
                n = order(i)
                r = pl.ds(pl.multiple_of(n * c, c), c)
                k = k_ref[r, :]
                _, _, el, dec = _gla_chunk(la_s, d, n)
                kl = k * el
                new = []
                for hh in range(2):
                    cols = slice(LANES * hh, LANES * (hh + 1))
                    stash[hh, n] = sts[hh]
                    new.append(sts[hh] * dec + _dot(v_ref[r, cols].astype(BF16), (kl * heads[hh]).astype(BF16), TN))
                return tuple(new)

            _chunk_loop(nchunk, states, (zero, zero), GLA_UNROLL)

            def step(i, dsts):
                n = order(nchunk - 1 - i)
                r = pl.ds(pl.multiple_of(n * c, c), c)
                q = q_ref[r, :] * scale
                k = k_ref[r, :]
                eq, ek, el, dec = _gla_chunk(la_s, d, n)
                qt = q * eq
                kt = k * ek
                kl = k * el
                ktb = kt.astype(BF16)
                dqt = jnp.zeros((c, LANES), F32)
                dkt = jnp.zeros((c, LANES), F32)
                dkl = jnp.zeros((c, LANES), F32)
                ddec = jnp.zeros((1, LANES), F32)
                new = []
                for hh in range(2):
                    cols = slice(LANES * hh, LANES * (hh + 1))
                    vb = v_ref[r, cols].astype(BF16)
                    dob = do_ref[r, cols].astype(BF16)
                    qm = (qt * heads[hh]).astype(BF16)
                    a = jnp.where(tri, _dot(qm, ktb, NT), 0.0).astype(BF16)
                    da = jnp.where(tri, _dot(dob, vb, NT), 0.0).astype(BF16)
                    sn = stash[hh, n]
                    dst = dsts[hh]
                    dstb = dst.astype(BF16)
                    dqt = dqt + (_dot(da, ktb) + _dot(dob, sn.astype(BF16))) * heads[hh]
                    dkt = dkt + _dot(da, qm, TN)
                    dv = _dot(a, dob, TN) + _dot((kl * heads[hh]).astype(BF16), dstb, NT)
                    dkl = dkl + _dot(vb, dstb)
                    ddec = ddec + jnp.sum(dst * sn, axis=0, keepdims=True)
                    new.append(dst * dec + _dot(dob, qm, TN))
                    if d == 0:
                        dv_s[r, cols] = dv
                    else:
                        dv_ref[r, cols] = (dv_s[r, cols] + dv).astype(BF16)
                dlast = ddec * dec + jnp.sum(dkl * kl, axis=0, keepdims=True)
                dq = dqt * eq * scale
                dk = dkt * ek + dkl * el
                dcum = dqt * qt - dkt * kt - dkl * kl + last_row * dlast
                dla_s[d, r, :] = dcum
                if d == 0:
                    dq_s[r, :] = dq
                    dk_s[r, :] = dk
                else:
                    dq_ref[r, :] = (dq_s[r, :] + dq).astype(BF16)
                    dk_ref[r, :] = (dk_s[r, :] + dk).astype(BF16)
                return tuple(new)

            _chunk_loop(nchunk, step, (zero, zero), GLA_UNROLL)

        first = pl.program_id(0) == 0
        for d, (wa_ref, ba_ref, dwa_ref, dba_ref) in enumerate(
                ((waf_ref, baf_ref, dwaf_ref, dbaf_ref), (wab_ref, bab_ref, dwab_ref, dbab_ref))):
            dwa_ref[...] = jnp.zeros_like(dwa_ref)
            dba_ref[...] = jnp.zeros_like(dba_ref)
            tri_t = _chunk_tri(d == 0)[0:tm, 0:tm]

            def gates(i, carry):
                r = pl.ds(pl.multiple_of(i * tm, tm), tm)
                gab = ga_ref[r, :].astype(BF16)
                wab16 = wa_ref[...].astype(BF16)
                pre = _dot(gab, wab16) + ba_ref[...]
                dpre = _exact_tri_matmul(tri_t, dla_s[d, r, :]) * (1.0 / GLA_TAU) * _sigmoid(-pre)
                dpb = dpre.astype(BF16)
                dwa_ref[...] += _dot(gab, dpb, TN)
                dba_ref[...] += jnp.sum(dpre, axis=0, keepdims=True)
                dga = _dot(dpb, wab16, NT)
                if d == 0:
                    @pl.when(first)
                    def _():
                        dga_ref[r, :] = dga

                    @pl.when(jnp.logical_not(first))
                    def _():
                        dga_ref[r, :] += dga
                else:
                    dga_ref[r, :] += dga
                return carry

            lax.fori_loop(0, s // tm, gates, 0)

    w2 = 2 * LANES
    return _call(
        body, "gla_bwd", (2,),
        [pl.BlockSpec((s, LANES), lambda p: (0, B_Q + p), pipeline_mode=one),
         pl.BlockSpec((s, LANES), lambda p: (0, B_K + p), pipeline_mode=one),
         pl.BlockSpec((s, w2), lambda p: (0, B_V // 2 + p), pipeline_mode=one),
         pl.BlockSpec((s, LANES), lambda p: (0, GA), pipeline_mode=one),
         pl.BlockSpec((s, w2), lambda p: (0, p), pipeline_mode=one),
         pl.BlockSpec((LANES, LANES), lambda p: (0, p)),
         pl.BlockSpec((LANES, LANES), lambda p: (0, p)),
         pl.BlockSpec((1, LANES), lambda p: (0, p)),
         pl.BlockSpec((1, LANES), lambda p: (0, p))],
        [pl.BlockSpec((s, LANES), lambda p: (0, p), pipeline_mode=one),
         pl.BlockSpec((s, LANES), lambda p: (0, p), pipeline_mode=one),
         pl.BlockSpec((s, w2), lambda p: (0, p), pipeline_mode=one),
         pl.BlockSpec((s, LANES), lambda p: (0, 0), pipeline_mode=one),
         pl.BlockSpec((LANES, LANES), lambda p: (0, p)),
         pl.BlockSpec((LANES, LANES), lambda p: (0, p)),
         pl.BlockSpec((1, LANES), lambda p: (0, p)),
         pl.BlockSpec((1, LANES), lambda p: (0, p))],
        [jax.ShapeDtypeStruct((s, w2), BF16), jax.ShapeDtypeStruct((s, w2), BF16),
         jax.ShapeDtypeStruct((s, GROUP_W), BF16), jax.ShapeDtypeStruct((s, LANES), F32),
         jax.ShapeDtypeStruct((LANES, w2), F32), jax.ShapeDtypeStruct((LANES, w2), F32),
         jax.ShapeDtypeStruct((1, w2), F32), jax.ShapeDtypeStruct((1, w2), F32)],
        [pltpu.VMEM((2, s, LANES), F32), pltpu.VMEM((2, s, LANES), F32),
         pltpu.VMEM((2, nchunk, LANES, LANES), F32),
         pltpu.VMEM((s, LANES), F32), pltpu.VMEM((s, LANES), F32), pltpu.VMEM((s, w2), F32)],
        ("arbitrary",), (z, z, z, z, d_o, wa_f, wa_b, ba_f, ba_b), rider)


def _shift_rows(x, d, rowi):
    s = x.shape[0]
    if d == 0:
        return x
    y = pltpu.roll(x, d % s, 0)
    keep = (rowi >= d) if d > 0 else (rowi < s + d)
    return jnp.where(keep, y, 0.0)


def _run_sum(x, m, step, rowi):
    acc, n = x, 1
    while n < m:
        acc = acc + _shift_rows(acc, step * n, rowi)
        n *= 2
    return acc


def _pool_counts(s, w, rowi):
    hi = jnp.minimum(rowi + w // 2, s)
    lo = jnp.maximum(rowi - w // 2, 0)
    return (hi - lo).astype(F32)


def _pooled(u, w, rowi):
    s = u.shape[0]
    win = _shift_rows(_run_sum(u, w // 2, 1, rowi), 1, rowi) + _run_sum(u, w // 2, -1, rowi)
    return win / _pool_counts(s, w, rowi) - u


def _pool_fwd(z, pool_w, pool_scale):
    s = z.shape[0]
    one = pl.Buffered(1)

    def body(u_ref, g_ref, w_ref, sc_ref, y_ref):
        rowi = lax.broadcasted_iota(jnp.int32, (s, 1), 0)
        for g, w in enumerate(POOL_WINDOWS):
            cols = slice(LANES * g, LANES * (g + 1))
            pooled = _pooled(u_ref[:, cols], w, rowi)
            mixed = _dot(pooled.astype(BF16), w_ref[g].astype(BF16))
            y_ref[:, cols] = (_silu(g_ref[:, cols]) * (mixed * sc_ref[:, cols])).astype(BF16)

    return pl.pallas_call(
        body, name="pool_fwd", grid=(1,),
        in_specs=[pl.BlockSpec((s, GROUP_W), lambda i: (0, C_V // 4), pipeline_mode=one),
                  pl.BlockSpec((s, GROUP_W), lambda i: (0, C_G // 4), pipeline_mode=one),
                  pl.BlockSpec((4, LANES, LANES), lambda i: (0, 0, 0)),
                  pl.BlockSpec((1, GROUP_W), lambda i: (0, 0))],
        out_specs=pl.BlockSpec((s, GROUP_W), lambda i: (0, 0), pipeline_mode=one),
        out_shape=jax.ShapeDtypeStruct((s, GROUP_W), BF16),
        compiler_params=_params(("arbitrary",)),
    )(z, z, pool_w, pool_scale)


def _pool_bwd(z, dy, pool_w, pool_scale):
    s = z.shape[0]
    one = pl.Buffered(1)

    def body(u_ref, g_ref, dy_ref, w_ref, sc_ref, du_ref, dg_ref, dw_ref, dsc_ref):
        rowi = lax.broadcasted_iota(jnp.int32, (s, 1), 0)
        for g, w in enumerate(POOL_WINDOWS):
            cols = slice(LANES * g, LANES * (g + 1))
            gate, dyv, sc = g_ref[:, cols], dy_ref[:, cols], sc_ref[:, cols]
            wb = w_ref[g].astype(BF16)
            pooled = _pooled(u_ref[:, cols], w, rowi)
            pb = pooled.astype(BF16)
            mixed = _dot(pb, wb)
            dg_ref[:, cols] = (dyv * (mixed * sc) * _silu_grad(gate)).astype(BF16)
            dt = dyv * _silu(gate)
            dsc_ref[:, cols] = jnp.sum(dt * mixed, axis=0, keepdims=True)
            dmb = (dt * sc).astype(BF16)
            dw_ref[g] = _dot(pb, dmb, TN)
            dpool = _dot(dmb, wb, NT)
            e = dpool / _pool_counts(s, w, rowi)
            du_ref[:, cols] = (_run_sum(e, w // 2, 1, rowi) + _shift_rows(_run_sum(e, w // 2, -1, rowi), -1, rowi)
                               - dpool).astype(BF16)

    return pl.pallas_call(
        body, name="pool_bwd", grid=(1,),
        in_specs=[pl.BlockSpec((s, GROUP_W), lambda i: (0, C_V // 4), pipeline_mode=one),
                  pl.BlockSpec((s, GROUP_W), lambda i: (0, C_G // 4), pipeline_mode=one),
                  pl.BlockSpec((s, GROUP_W), lambda i: (0, 2), pipeline_mode=one),
                  pl.BlockSpec((4, LANES, LANES), lambda i: (0, 0, 0)),
                  pl.BlockSpec((1, GROUP_W), lambda i: (0, 0))],
        out_specs=[pl.BlockSpec((s, GROUP_W), lambda i: (0, 0), pipeline_mode=one),
                   pl.BlockSpec((s, GROUP_W), lambda i: (0, 0), pipeline_mode=one),
                   pl.BlockSpec((4, LANES, LANES), lambda i: (0, 0, 0)),
                   pl.BlockSpec((1, GROUP_W), lambda i: (0, 0))],
        out_shape=[jax.ShapeDtypeStruct((s, GROUP_W), BF16), jax.ShapeDtypeStruct((s, GROUP_W), BF16),
                   jax.ShapeDtypeStruct((4, LANES, LANES), F32), jax.ShapeDtypeStruct((1, GROUP_W), F32)],
        compiler_params=_params(("arbitrary",)),
    )(z, z, dy, pool_w, pool_scale)


def _mla_heads(qf, kv, kpe, qg, kg, cos, sin):
    out = []
    for h in range(4):
        qa = qf[:, LANES * h:LANES * (h + 1)]
        qb = qf[:, 512 + LANES * h:512 + LANES * (h + 1)]
        ka = kv[:, 256 * h:256 * h + LANES]
        rq = lax.rsqrt((jnp.sum(qa * qa, axis=-1, keepdims=True) + jnp.sum(qb * qb, axis=-1, keepdims=True))
                       * (1.0 / MLA_QK) + EPS)
        rk = lax.rsqrt((jnp.sum(ka * ka, axis=-1, keepdims=True) + jnp.sum(kpe * kpe, axis=-1, keepdims=True))
                       * (1.0 / MLA_QK) + EPS)
        out.append((qa, qb, rq, ka, rk))
    return out


def _mla_latents(mq_ref, mkv_ref, gq_ref, gkv_ref, wq_ref, wkv_ref):
    mq = mq_ref[...]
    rq = lax.rsqrt(jnp.mean(mq * mq, axis=-1, keepdims=True) + EPS)
    qn = mq * rq
    qnb = (qn * gq_ref[...]).astype(BF16)
    mkv = mkv_ref[...]
    rk = lax.rsqrt(jnp.mean(mkv * mkv, axis=-1, keepdims=True) + EPS)
    kvn = mkv * rk
    kvnb = (kvn * gkv_ref[...]).astype(BF16)
    qf = _dot(qnb, wq_ref[...])
    kv = _dot(kvnb, wkv_ref[...])
    return qn, rq, qnb, kvn, rk, kvnb, qf, kv


def _mla_prep(z, cos_m, sin_m, gq, wq, gkv, wkv, qg, kg, tm):
    s = z.shape[0]

    def body(mq_ref, mkv_ref, mkr_ref, cos_ref, sin_ref, gq_ref, wq_ref, gkv_ref, wkv_ref, qg_ref, kg_ref,
             q_ref, k_ref, v_ref):
        _, _, _, _, _, _, qf, kv = _mla_latents(mq_ref, mkv_ref, gq_ref, gkv_ref, wq_ref, wkv_ref)
        kpe = mkr_ref[...]
        cos, sin = cos_ref[...], sin_ref[...]
        qg, kg = qg_ref[...], kg_ref[...]
        for h, (qa, qb, rq, ka, rk) in enumerate(_mla_heads(qf, kv, kpe, qg, kg, cos, sin)):
            q_ref[h, :, 0:LANES] = (qa * rq * qg[:, 0:LANES] * ATTN_Q_SCALE).astype(BF16)
            q_ref[h, :, LANES:] = (_rope(qb * rq * qg[:, LANES:], cos, sin) * ATTN_Q_SCALE).astype(BF16)
            k_ref[h, :, 0:LANES] = (ka * rk * kg[:, 0:LANES]).astype(BF16)
            k_ref[h, :, LANES:] = _rope(kpe * rk * kg[:, LANES:], cos, sin).astype(BF16)
            v_ref[h] = kv[:, 256 * h + LANES:256 * (h + 1)].astype(BF16)

    full = lambda shape: pl.BlockSpec(shape, lambda i: (0,) * len(shape))
    return pl.pallas_call(
        body, name="mla_prep", grid=(s // tm,),
        in_specs=[pl.BlockSpec((tm, 512), lambda i: (i, M_Q // 4)),
                  pl.BlockSpec((tm, 256), lambda i: (i, M_KV // 2)),
                  pl.BlockSpec((tm, LANES), lambda i: (i, M_KR)),
                  pl.BlockSpec((tm, LANES), lambda i: (i, 0)),
                  pl.BlockSpec((tm, LANES), lambda i: (i, 0)),
                  full((1, 512)), full((512, 1024)), full((1, 256)), full((256, 1024)), full((1, 256)), full((1, 256))],
        out_specs=[pl.BlockSpec((4, tm, 256), lambda i: (0, i, 0)), pl.BlockSpec((4, tm, 256), lambda i: (0, i, 0)),
                   pl.BlockSpec((4, tm, LANES), lambda i: (0, i, 0))],
        out_shape=[jax.ShapeDtypeStruct((4, s, 256), BF16), jax.ShapeDtypeStruct((4, s, 256), BF16),
                   jax.ShapeDtypeStruct((4, s, LANES), BF16)],
        compiler_params=_params(("parallel",)),
    )(z, z, z, cos_m, sin_m, gq, wq, gkv, wkv, qg, kg)


def _mla_prep_bwd(z, cos_m, sin_m, gq, wq, gkv, wkv, qg, kg, dq, dk, dv, tm):
    s = z.shape[0]

    def body(mq_ref, mkv_ref, mkr_ref, cos_ref, sin_ref, gq_ref, wq_ref, gkv_ref, wkv_ref, qg_ref, kg_ref,
             dq_ref, dk_ref, dv_ref,
             dmq_ref, dmkv_ref, dmkr_ref, dwq_ref, dwkv_ref, dgq_ref, dgkv_ref, dqg_ref, dkg_ref, dqf, dkv):
        @pl.when(pl.program_id(0) == 0)
        def _():
            for r in (dwq_ref, dwkv_ref, dgq_ref, dgkv_ref, dqg_ref, dkg_ref):
                r[...] = jnp.zeros_like(r)

        qn, rq0, qnb, kvn, rk0, kvnb, qf, kv = _mla_latents(mq_ref, mkv_ref, gq_ref, gkv_ref, wq_ref, wkv_ref)
        kpe = mkr_ref[...]
        cos, sin = cos_ref[...], sin_ref[...]
        qg, kg = qg_ref[...], kg_ref[...]
        dkpe = jnp.zeros_like(kpe)
        inv = 1.0 / MLA_QK

        def norm_bwd(a, b, r, da_n, db_n, g):
            ga, gb = g[:, 0:LANES], g[:, LANES:]
            dg_a = jnp.sum(da_n * a * r, axis=0, keepdims=True)
            dg_b = jnp.sum(db_n * b * r, axis=0, keepdims=True)
            ua, ub = da_n * ga, db_n * gb
            dt = (jnp.sum(ua * a, axis=-1, keepdims=True) + jnp.sum(ub * b, axis=-1, keepdims=True)) * inv
            r3 = r * r * r
            return r * ua - a * (r3 * dt), r * ub - b * (r3 * dt), dg_a, dg_b

        for h, (qa, qb, rq, ka, rk) in enumerate(_mla_heads(qf, kv, kpe, qg, kg, cos, sin)):
            dqa, dqb, dga, dgb = norm_bwd(qa, qb, rq, dq_ref[h, :, 0:LANES] * ATTN_SCALE,
                                          _rope_t(dq_ref[h, :, LANES:] * ATTN_SCALE, cos, sin), qg)
            dqf[:, LANES * h:LANES * (h + 1)] = dqa
            dqf[:, 512 + LANES * h:512 + LANES * (h + 1)] = dqb
            dqg_ref[:, 0:LANES] += dga
            dqg_ref[:, LANES:] += dgb
            ln2 = math.log(2.0)
            dka, dkb, dga, dgb = norm_bwd(ka, kpe, rk, dk_ref[h, :, 0:LANES] * ln2,
                                          _rope_t(dk_ref[h, :, LANES:] * ln2, cos, sin), kg)
            dkv[:, 256 * h:256 * h + LANES] = dka
            dkv[:, 256 * h + LANES:256 * (h + 1)] = dv_ref[h]
            dkpe = dkpe + dkb
            dkg_ref[:, 0:LANES] += dga
            dkg_ref[:, LANES:] += dgb
        dmkr_ref[...] = dkpe.astype(BF16)

        def latent_bwd(dfull, w_ref, nb, n, r, g_ref, dw_ref, dg_ref, dlat_ref):
            db = dfull.astype(BF16)
            dn = _dot(db, w_ref[...], NT)
            dw_ref[...] += _dot(nb, db, TN)
            dg_ref[...] += jnp.sum(dn * n, axis=0, keepdims=True)
            u = dn * g_ref[...]
            dlat_ref[...] = (r * (u - n * jnp.mean(u * n, axis=-1, keepdims=True))).astype(BF16)

        latent_bwd(dqf[...], wq_ref, qnb, qn, rq0, gq_ref, dwq_ref, dgq_ref, dmq_ref)
        latent_bwd(dkv[...], wkv_ref, kvnb, kvn, rk0, gkv_ref, dwkv_ref, dgkv_ref, dmkv_ref)

    full = lambda shape: pl.BlockSpec(shape, lambda i: (0,) * len(shape))
    return pl.pallas_call(
        body, name="mla_prep_bwd", grid=(s // tm,),
        in_specs=[pl.BlockSpec((tm, 512), lambda i: (i, M_Q // 4)),
                  pl.BlockSpec((tm, 256), lambda i: (i, M_KV // 2)),
                  pl.BlockSpec((tm, LANES), lambda i: (i, M_KR)),
                  pl.BlockSpec((tm, LANES), lambda i: (i, 0)),
                  pl.BlockSpec((tm, LANES), lambda i: (i, 0)),
                  full((1, 512)), full((512, 1024)), full((1, 256)), full((256, 1024)), full((1, 256)), full((1, 256)),
                  pl.BlockSpec((4, tm, 256), lambda i: (0, i, 0)), pl.BlockSpec((4, tm, 256), lambda i: (0, i, 0)),
                  pl.BlockSpec((4, tm, LANES), lambda i: (0, i, 0))],
        out_specs=[pl.BlockSpec((tm, 512), lambda i: (i, 0)), pl.BlockSpec((tm, 256), lambda i: (i, 0)),
                   pl.BlockSpec((tm, LANES), lambda i: (i, 0)),
                   full((512, 1024)), full((256, 1024)), full((1, 512)), full((1, 256)), full((1, 256)), full((1, 256))],
        out_shape=[jax.ShapeDtypeStruct((s, 512), BF16), jax.ShapeDtypeStruct((s, 256), BF16),
                   jax.ShapeDtypeStruct((s, LANES), BF16),
                   jax.ShapeDtypeStruct((512, 1024), F32), jax.ShapeDtypeStruct((256, 1024), F32),
                   jax.ShapeDtypeStruct((1, 512), F32), jax.ShapeDtypeStruct((1, 256), F32),
                   jax.ShapeDtypeStruct((1, 256), F32), jax.ShapeDtypeStruct((1, 256), F32)],
        scratch_shapes=[pltpu.VMEM((tm, 1024), F32), pltpu.VMEM((tm, 1024), F32)],
        compiler_params=_params(("arbitrary",)),
    )(z, z, z, cos_m, sin_m, gq, wq, gkv, wkv, qg, kg, dq, dk, dv)


def _attn_fwd(q, k, v, z, tq, rider=None):
    s = q.shape[1]

    def body(q_ref, k_ref, v_ref, g_ref, o_ref, y_ref, lse_ref):
        sc = _dot(q_ref[...], k_ref[...], NT)
        m = jnp.max(sc, axis=-1, keepdims=True)
        p = jnp.exp2(sc - m)
        l = jnp.sum(p, axis=-1, keepdims=True)
        o = _dot(p.astype(BF16), v_ref[...]) / l
        o_ref[...] = o
        y_ref[...] = (_silu(g_ref[...]) * o).astype(BF16)
        lse_ref[...] = m + jnp.log2(l)

    return _call(
        body, "attn_fwd", (4, s // tq),
        [pl.BlockSpec((None, tq, 256), lambda h, i: (h, i, 0)),
         pl.BlockSpec((None, s, 256), lambda h, i: (h, 0, 0)),
         pl.BlockSpec((None, s, LANES), lambda h, i: (h, 0, 0)),
         pl.BlockSpec((tq, LANES), lambda h, i: (i, M_G + h))],
        [pl.BlockSpec((tq, LANES), lambda h, i: (i, h)), pl.BlockSpec((tq, LANES), lambda h, i: (i, h)),
         pl.BlockSpec((None, tq, 1), lambda h, i: (h, i, 0))],
        [jax.ShapeDtypeStruct((s, GROUP_W), F32), jax.ShapeDtypeStruct((s, GROUP_W), BF16),
         jax.ShapeDtypeStruct((4, s, 1), F32)],
        [], ("parallel", "parallel"), (q, k, v, z), rider)


def _attn_bwd(q, k, v, z, o, lse, dy, tq, rider=None):
    s = q.shape[1]

    def body(q_ref, k_ref, v_ref, g_ref, o_ref, lse_ref, dy_ref, dq_ref, dk_ref, dv_ref, dg_ref):
        @pl.when(pl.program_id(1) == 0)
        def _():
            dk_ref[...] = jnp.zeros_like(dk_ref)
            dv_ref[...] = jnp.zeros_like(dv_ref)

        gate, ov, dyv = g_ref[...], o_ref[...], dy_ref[...]
        do = dyv * _silu(gate)
        dg_ref[...] = (dyv * ov * _silu_grad(gate)).astype(BF16)
        delta = jnp.sum(do * ov, axis=-1, keepdims=True)
        dob = do.astype(BF16)
        qb, kb = q_ref[...], k_ref[...]
        p = jnp.exp2(_dot(qb, kb, NT) - lse_ref[...])
        dp = _dot(dob, v_ref[...], NT)
        ds = (p * (dp - delta)).astype(BF16)
        dq_ref[...] = _dot(ds, kb)
        dk_ref[...] += _dot(ds, qb, TN)
        dv_ref[...] += _dot(p.astype(BF16), dob, TN)

    return _call(
        body, "attn_bwd", (4, s // tq),
        [pl.BlockSpec((None, tq, 256), lambda h, i: (h, i, 0)),
         pl.BlockSpec((None, s, 256), lambda h, i: (h, 0, 0)),
         pl.BlockSpec((None, s, LANES), lambda h, i: (h, 0, 0)),
         pl.BlockSpec((tq, LANES), lambda h, i: (i, M_G + h)),
         pl.BlockSpec((tq, LANES), lambda h, i: (i, h)),
         pl.BlockSpec((None, tq, 1), lambda h, i: (h, i, 0)),
         pl.BlockSpec((tq, LANES), lambda h, i: (i, 12 + h))],
        [pl.BlockSpec((None, tq, 256), lambda h, i: (h, i, 0)),
         pl.BlockSpec((None, s, 256), lambda h, i: (h, 0, 0)),
         pl.BlockSpec((None, s, LANES), lambda h, i: (h, 0, 0)),
         pl.BlockSpec((tq, LANES), lambda h, i: (i, h))],
        [jax.ShapeDtypeStruct((4, s, 256), F32), jax.ShapeDtypeStruct((4, s, 256), F32),
         jax.ShapeDtypeStruct((4, s, LANES), F32), jax.ShapeDtypeStruct((s, GROUP_W), BF16)],
        [], ("parallel", "arbitrary"), (q, k, v, z, o, lse, dy), rider)


def _adam(parts, w, m, v, name, tr):
    r, c = w.shape
    tr = min(tr, r)
    c1 = 1.0 - ADAM_B1 ** ADAM_STEP
    c2 = 1.0 - ADAM_B2 ** ADAM_STEP

    def body(p_ref, w_ref, m_ref, v_ref, g_ref, d_ref, nm_ref, nv_ref):
        g = p_ref[0].astype(F32)
        for i in range(1, N_DEV):
            g = g + p_ref[i].astype(F32)
        nm = ADAM_B1 * m_ref[...] + (1.0 - ADAM_B1) * g
        nv = ADAM_B2 * v_ref[...] + (1.0 - ADAM_B2) * (g * g)
        g_ref[...] = g
        nm_ref[...] = nm
        nv_ref[...] = nv
        d_ref[...] = -ADAM_LR * ((nm / c1) / (jnp.sqrt(nv / c2) + ADAM_EPS) + ADAM_WD * w_ref[...])

    blk = lambda: pl.BlockSpec((tr, c), lambda i: (i, 0))
    return pl.pallas_call(
        body, name=name, grid=(r // tr,),
        in_specs=[pl.BlockSpec((N_DEV, tr, c), lambda i: (0, i, 0)), blk(), blk(), blk()],
        out_specs=[blk(), blk(), blk(), blk()],
        out_shape=[jax.ShapeDtypeStruct((r, c), F32)] * 4,
        compiler_params=_params(("parallel",)),
    )(parts, w, m, v)


def _adam_columns(parts, w, m, v, name, tc, rider=None):
    nl, r, c = w.shape
    pieces = [p for layer in parts for p in layer]
    nh = len(parts[0])
    rp = pieces[0].shape[2]
    tc = min(tc, rp)
    ncb = rp // tc
    c1 = 1.0 - ADAM_B1 ** ADAM_STEP
    c2 = 1.0 - ADAM_B2 ** ADAM_STEP

    def body(*refs):
        p_refs, (w_ref, m_ref, v_ref, g_ref, d_ref, nm_ref, nv_ref) = refs[:len(pieces)], refs[len(pieces):]
        for h in range(nh):
            @pl.when(pl.program_id(0) == h)
            def _(h=h):
                for l in range(nl):
                    p_ref = p_refs[l * nh + h]
                    g = p_ref[0].astype(F32)
                    for i in range(1, N_DEV):
                        g = g + p_ref[i].astype(F32)
                    nm = ADAM_B1 * m_ref[:, l, :] + (1.0 - ADAM_B1) * g
                    nv = ADAM_B2 * v_ref[:, l, :] + (1.0 - ADAM_B2) * (g * g)
                    g_ref[:, l, :] = g
                    nm_ref[:, l, :] = nm
                    nv_ref[:, l, :] = nv
                    d_ref[:, l, :] = -ADAM_LR * ((nm / c1) / (jnp.sqrt(nv / c2) + ADAM_EPS) + ADAM_WD * w_ref[:, l, :])

    def part_spec(j):
        return pl.BlockSpec((N_DEV, c, tc), lambda h, i: (0, 0, jnp.clip((h - j % nh) * ncb + i, 0, ncb - 1)))

    blk = lambda: pl.BlockSpec((c, nl, tc), lambda h, i: (0, 0, h * ncb + i))
    t = lambda a: jnp.transpose(a, (2, 0, 1))
    *res, = _call(body, name, (nh, ncb), [part_spec(j) for j in range(len(pieces))] + [blk(), blk(), blk()],
                  [blk(), blk(), blk(), blk()], [jax.ShapeDtypeStruct((c, nl, r), F32)] * 4, [],
                  ("arbitrary",) * 2, (*pieces, t(w), t(m), t(v)), rider)
    return [jnp.transpose(a, (1, 2, 0)) for a in res[:4]] + res[4:]


def _adam_layers(parts, w, m, v, name, tr, rider=None):
    nl, r, c = w.shape
    pieces = [p for layer in parts for p in layer]
    rp = pieces[0].shape[1]
    tr = min(tr, rp)
    nr, nrp = r // tr, rp // tr
    c1 = 1.0 - ADAM_B1 ** ADAM_STEP
    c2 = 1.0 - ADAM_B2 ** ADAM_STEP

    def body(*refs):
        p_refs, (w_ref, m_ref, v_ref, g_ref, d_ref, nm_ref, nv_ref) = refs[:len(pieces)], refs[len(pieces):]
        at = pl.program_id(0) * nr + pl.program_id(1)
        for j in range(len(pieces)):
            @pl.when(jnp.logical_and(at >= j * nrp, at < (j + 1) * nrp))
            def _(p_ref=p_refs[j]):
                g = p_ref[0].astype(F32)
                for i in range(1, N_DEV):
                    g = g + p_ref[i].astype(F32)
                nm = ADAM_B1 * m_ref[...] + (1.0 - ADAM_B1) * g
                nv = ADAM_B2 * v_ref[...] + (1.0 - ADAM_B2) * (g * g)
                g_ref[...] = g
                nm_ref[...] = nm
                nv_ref[...] = nv
                d_ref[...] = -ADAM_LR * ((nm / c1) / (jnp.sqrt(nv / c2) + ADAM_EPS) + ADAM_WD * w_ref[...])

    def part_spec(j):
        return pl.BlockSpec((N_DEV, tr, c), lambda ll, i: (0, jnp.clip(ll * nr + i - j * nrp, 0, nrp - 1), 0))

    blk = lambda: pl.BlockSpec((None, tr, c), lambda ll, i: (ll, i, 0))
    return _call(body, name, (nl, nr), [part_spec(j) for j in range(len(pieces))] + [blk(), blk(), blk()],
                 [blk(), blk(), blk(), blk()], [jax.ShapeDtypeStruct((nl, r, c), F32)] * 4, [],
                 ("arbitrary", "arbitrary"), (*pieces, w, m, v), rider)


REPLICATED = ("norm_g", "ret_norm_g", "gla_ba_f", "gla_ba_b", "gla_norm_g", "pool_w", "pool_scale",
              "mla_q_norm_g", "mla_kv_norm_g", "mla_qk_norm_q", "mla_qk_norm_k")
SMALL_SHARDED = ("mla_wq_b", "mla_wkv_b", "gla_wa2_f", "gla_wa2_b")
WEIGHTS = ("norm_g", "w_in", "ret_norm_g", "gla_wa2_f", "gla_ba_f", "gla_wa2_b", "gla_ba_b", "gla_norm_g", "pool_w",
           "pool_scale", "mla_q_norm_g", "mla_wq_b", "mla_kv_norm_g", "mla_wkv_b", "mla_qk_norm_q", "mla_qk_norm_k",
           "w_out")


def _pack(arrays, dtype):
    flat = jnp.concatenate([a.reshape(-1) for a in arrays]).astype(dtype)
    return flat.reshape(-1, LANES)


def _unpack(packed, like):
    flat = packed.reshape(-1)
    out, at = [], 0
    for a in like:
        out.append(flat[at:at + a.size].reshape(a.shape))
        at += a.size
    return out


def _columns_by_device(g):
    l, r, n = g.shape
    return g.reshape(l, r, N_DEV, n // N_DEV).transpose(2, 0, 1, 3)


def _gathered_columns(g, l, r, c):
    return g.reshape(N_DEV, l, r, c).transpose(1, 2, 0, 3).reshape(l, r, N_DEV * c)


def _layer_forward(x, wts, late_wts, tables, tm, tq, ride_inproj=None, ride_attn=None, target=None):
    cos_r, sin_r, cos_m, sin_m, tab, _ = tables
    z, h, *carried_in = _inproj(x, wts["norm_g"], wts["w_in"], min(x.shape[0], 2 * tm), rider=ride_inproj)
    wts.update(late_wts(carried_in))
    o_a, y_a = _ret_fwd(z, cos_r, sin_r, tab, wts["ret_norm_g"])
    o_b, y_b = _gla_fwd(z, wts["wa_f"], wts["wa_b"], wts["gla_ba_f"], wts["gla_ba_b"], wts["gla_norm_g"])
    y_c = _pool_fwd(z, wts["pool_w"], wts["pool_scale"])
    q, k, v = _mla_prep(z, cos_m, sin_m, wts["mla_q_norm_g"], wts["wq"], wts["mla_kv_norm_g"], wts["wkv"],
                        wts["qk_q"], wts["qk_k"], tm)
    o_d, y_d, lse, *carried_attn = _attn_fwd(q, k, v, z, tq, rider=ride_attn)
    y = jnp.concatenate([y_a, y_b, y_c, y_d], axis=1)
    w_out = wts["w_out"]
    if target is None:
        x_next = _mm(y, w_out, "nn", "outproj", tm, D_MODEL, 1024, add=x)
    else:
        x_next = _mm(y, w_out, "nn", "outproj_loss", tm, D_MODEL, 1024, tail=_loss_tail(x, target))
    saved = dict(x=x, z=z, h=h, o_a=o_a, o_b=o_b, o_d=o_d, lse=lse, q=q, k=k, v=v, y=y, w_out=w_out)
    return x_next, saved, carried_in, carried_attn


def _layer_backward(dx, sv, wts, tables, tm, tq, rides):
    cos_r, sin_r, cos_m, sin_m, tab, tab_sw = tables
    z = sv["z"]
    g = {}
    carried = {}

    def rider(name):
        return rides[name](g) if name in rides else None

    def landed(name, results, n_own):
        if name in rides:
            carried[name] = results[n_own]
        return results[:n_own]

    g["w_out"] = _mm(sv["y"], dx, "tn", "d_w_out", 2048, 1024, 1024, out_dtype=BF16)
    dy = _mm(dx, sv["w_out"], "nt", "d_y", tm, 2048, 1024)

    do_a, dg_a, g["ret_norm_g"] = _normgate_bwd(sv["o_a"], z, A_G, dy, 0, wts["ret_norm_g"], tm)
    dq_a, dk_a, dv_a = landed("ret", _ret_bwd(z, do_a, cos_r, sin_r, tab, tab_sw, rider=rider("ret")), 3)

    do_b, dg_b, g["gla_norm_g"] = _normgate_bwd(sv["o_b"], z, B_G, dy, 1, wts["gla_norm_g"], tm)
    dq_b, dk_b, dv_b, d_ga, d_waf, d_wab, g["gla_ba_f"], g["gla_ba_b"] = landed("gla", _gla_bwd(
        z, do_b, wts["wa_f"], wts["wa_b"], wts["gla_ba_f"], wts["gla_ba_b"], rider=rider("gla")), 8)
    g["gla_wa2_f"] = d_waf[0:GLA_RANK]
    g["gla_wa2_b"] = d_wab[GLA_RANK:2 * GLA_RANK]

    du_c, dg_c, g["pool_w"], g["pool_scale"] = _pool_bwd(z, dy, wts["pool_w"], wts["pool_scale"])

    d_q, d_k, d_v, dg_d = landed("attn", _attn_bwd(sv["q"], sv["k"], sv["v"], z, sv["o_d"], sv["lse"], dy, tq,
                                                   rider=rider("attn")), 4)
    (d_mq, d_mkv, d_mkr, d_wq, g["mla_wkv_b"], g["mla_q_norm_g"], g["mla_kv_norm_g"], d_qg, d_kg) = _mla_prep_bwd(
        z, cos_m, sin_m, wts["mla_q_norm_g"], wts["wq"], wts["mla_kv_norm_g"], wts["wkv"], wts["qk_q"], wts["qk_k"],
        d_q, d_k, d_v, tm)
    g["mla_wq_b"] = _unpad_wq(d_wq)
    g["mla_qk_norm_q"] = d_qg[:, _QK_INV]
    g["mla_qk_norm_k"] = d_kg[:, _QK_INV]

    dz = jnp.concatenate([dq_a, dk_a, dv_a, dg_a, dq_b, dk_b, dv_b, dg_b, d_mq, du_c, dg_c, dg_d, d_mkv,
                          d_ga.astype(BF16), d_mkr], axis=1)
    h, half = sv["h"], D_MODEL // 2
    g["w_in_a"] = _split_w_in(_mm(h[:, :half], dz, "tn", "d_w_in_a", half, 1024, 1024, out_dtype=BF16))
    res = _mm(h[:, half:], dz, "tn", "d_w_in_b", half, 1024, 1024, out_dtype=BF16, rider=rider("d_w_in"))
    (d_w_in_b,) = landed("d_w_in", res if "d_w_in" in rides else [res], 1)
    g["w_in_b"] = _split_w_in(d_w_in_b)
    dx_in, g["norm_g"] = landed("d_h", _mm(dz, wts["w_in"], "nt", "d_h", tm, D_MODEL, 1024, rider=rider("d_h"),
                                           tail=_norm_bwd_tail(sv["x"], wts["norm_g"], dx)), 2)
    return dx_in, g, carried


def kernel(x, norm_g, w_in, ret_norm_g, gla_wa2_f, gla_ba_f, gla_wa2_b, gla_ba_b, gla_norm_g, pool_w, pool_scale, mla_q_norm_g, mla_wq_b, mla_kv_norm_g, mla_wkv_b, mla_qk_norm_q, mla_qk_norm_k, w_out, loss_target, m_norm_g, m_w_in, m_ret_norm_g, m_gla_wa2_f, m_gla_ba_f, m_gla_wa2_b, m_gla_ba_b, m_gla_norm_g, m_pool_w, m_pool_scale, m_mla_q_norm_g, m_mla_wq_b, m_mla_kv_norm_g, m_mla_wkv_b, m_mla_qk_norm_q, m_mla_qk_norm_k, m_w_out, v_norm_g, v_w_in, v_ret_norm_g, v_gla_wa2_f, v_gla_ba_f, v_gla_wa2_b, v_gla_ba_b, v_gla_norm_g, v_pool_w, v_pool_scale, v_mla_q_norm_g, v_mla_wq_b, v_mla_kv_norm_g, v_mla_wkv_b, v_mla_qk_norm_q, v_mla_qk_norm_k, v_w_out):
    w = dict(norm_g=norm_g, w_in=w_in, ret_norm_g=ret_norm_g, gla_wa2_f=gla_wa2_f, gla_ba_f=gla_ba_f,
             gla_wa2_b=gla_wa2_b, gla_ba_b=gla_ba_b, gla_norm_g=gla_norm_g, pool_w=pool_w, pool_scale=pool_scale,
             mla_q_norm_g=mla_q_norm_g, mla_wq_b=mla_wq_b, mla_kv_norm_g=mla_kv_norm_g, mla_wkv_b=mla_wkv_b,
             mla_qk_norm_q=mla_qk_norm_q, mla_qk_norm_k=mla_qk_norm_k, w_out=w_out)
    m = dict(norm_g=m_norm_g, w_in=m_w_in, ret_norm_g=m_ret_norm_g, gla_wa2_f=m_gla_wa2_f, gla_ba_f=m_gla_ba_f,
             gla_wa2_b=m_gla_wa2_b, gla_ba_b=m_gla_ba_b, gla_norm_g=m_gla_norm_g, pool_w=m_pool_w,
             pool_scale=m_pool_scale, mla_q_norm_g=m_mla_q_norm_g, mla_wq_b=m_mla_wq_b, mla_kv_norm_g=m_mla_kv_norm_g,
             mla_wkv_b=m_mla_wkv_b, mla_qk_norm_q=m_mla_qk_norm_q, mla_qk_norm_k=m_mla_qk_norm_k, w_out=m_w_out)
    v = dict(norm_g=v_norm_g, w_in=v_w_in, ret_norm_g=v_ret_norm_g, gla_wa2_f=v_gla_wa2_f, gla_ba_f=v_gla_ba_f,
             gla_wa2_b=v_gla_wa2_b, gla_ba_b=v_gla_ba_b, gla_norm_g=v_gla_norm_g, pool_w=v_pool_w,
             pool_scale=v_pool_scale, mla_q_norm_g=v_mla_q_norm_g, mla_wq_b=v_mla_wq_b, mla_kv_norm_g=v_mla_kv_norm_g,
             mla_wkv_b=v_mla_wkv_b, mla_qk_norm_q=v_mla_qk_norm_q, mla_qk_norm_k=v_mla_qk_norm_k, w_out=v_w_out)
    xs, target = x[0], loss_target[0]
    s = xs.shape[0]
    tm, tq = min(s, 512), min(s, 256)
    c_in = w_in.shape[2]

    w_in_b = w_in.astype(BF16)
    w_out_b = w_out.astype(BF16).reshape(-1, D_MODEL)
    (w_in_g0,) = _exchange([("gather", w_in_b[0])], "gather_first")
    tables = _rope_tables(s) + _ret_tables()
    offs = np.cumsum([0] + [w[n].size for n in SMALL_SHARDED])

    def early_weights(l, w_in_g):
        return dict(
            norm_g=norm_g[l][None], w_in=_assemble_w_in(w_in_g), ret_norm_g=ret_norm_g[l][None],
            gla_ba_f=gla_ba_f[l][None], gla_ba_b=gla_ba_b[l][None],
            gla_norm_g=gla_norm_g[l][None], pool_w=pool_w[l], pool_scale=pool_scale[l][None],
            mla_q_norm_g=mla_q_norm_g[l][None], mla_kv_norm_g=mla_kv_norm_g[l][None],
            qk_q=_pad_qk_gain(mla_qk_norm_q[l]), qk_k=_pad_qk_gain(mla_qk_norm_k[l]))

    def late_weights(l, w_out_g, small_g):
        flat = small_g.reshape(N_DEV, -1)
        full = {n: _gathered_columns(flat[:, offs[i]:offs[i + 1]], *w[n].shape)[l]
                for i, n in enumerate(SMALL_SHARDED)}
        wa_f = jnp.zeros((LANES, 2 * LANES), BF16).at[0:GLA_RANK].set(full["gla_wa2_f"])
        wa_b = jnp.zeros((LANES, 2 * LANES), BF16).at[GLA_RANK:2 * GLA_RANK].set(full["gla_wa2_b"])
        return dict(w_out=w_out_g.reshape(N_DEV, DEPTH, -1, D_MODEL)[:, l].reshape(-1, D_MODEL),
                    wa_f=wa_f, wa_b=wa_b, wq=_pad_wq(full["mla_wq_b"]), wkv=full["mla_wkv_b"])

    by_owner = lambda g_w_out: g_w_out.reshape(N_DEV, -1, D_MODEL)

    layers = [early_weights(0, w_in_g0), None]
    x1, sv0, (w_out_g, small_g), (w_in_g1,) = _layer_forward(
        xs, layers[0], lambda got: late_weights(0, *got), tables, tm, tq,
        ride_inproj=[("gather", w_out_b), ("gather", _pack([w[n] for n in SMALL_SHARDED], BF16))],
        ride_attn=("gather", w_in_b[1]))
    layers[1] = early_weights(1, w_in_g1)
    (dx, loss_row), sv1, _, _ = _layer_forward(x1, layers[1], lambda got: late_weights(1, w_out_g, small_g), tables,
                                               tm, tq, target=target)
    loss = lax.psum(loss_row[0, 0], ("x", "y", "c"))

    dx, g1, got1 = _layer_backward(dx, sv1, layers[1], tables, tm, tq, {
        "attn": lambda g: ("scatter", by_owner(g["w_out"])),
        "d_h": lambda g: ("scatter", g["w_in_a"])})
    dx, g0, got0 = _layer_backward(dx, sv0, layers[0], tables, tm, tq, {
        "ret": lambda g: ("scatter", by_owner(g["w_out"])),
        "gla": lambda g: ("scatter", g1["w_in_b"]),
        "d_w_in": lambda g: ("scatter", g["w_in_a"]),
        "d_h": lambda g: ("scatter", g["w_in_b"])})
    in_parts = ((got0["d_w_in"], got0["d_h"]), (got1["d_h"], got0["gla"]))
    out_parts = ((got0["ret"],), (got1["attn"],))
    grads = (g0, g1)
    full = {n: jnp.stack([grads[l][n].reshape(w[n].shape[1:]) if n in REPLICATED else grads[l][n]
                          for l in range(DEPTH)]) for n in SMALL_SHARDED + REPLICATED}
    small_c = jnp.concatenate([_columns_by_device(full[n]).reshape(N_DEV, -1) for n in SMALL_SHARDED], axis=1)
    out = {}
    in_parts_t = [[jnp.transpose(p, (0, 2, 1)) for p in layer] for layer in in_parts]
    *out["w_in"], small_parts, rep_parts = _adam_columns(
        in_parts_t, w_in, m_w_in, v_w_in, "adam_w_in", 256,
        rider=[("scatter", small_c.reshape(N_DEV, -1, LANES)), ("gather", _pack([full[n] for n in REPLICATED], F32))])
    out["w_out"] = _adam_layers(out_parts, w_out, m_w_out, v_w_out, "adam_w_out", 128)
    for names, parts, label in ((SMALL_SHARDED, small_parts, "adam_small"), (REPLICATED, rep_parts, "adam_replicated")):
        res = _adam(parts, _pack([w[n] for n in names], F32), _pack([m[n] for n in names], F32),
                    _pack([v[n] for n in names], F32), label, 2048)
        for n, *vals in zip(names, *[_unpack(a, [w[n] for n in names]) for a in res]):
            out[n] = vals

    return (loss, dx[None], *[out[n][0] for n in WEIGHTS], *[out[n][1] for n in WEIGHTS],
            *[out[n][2] for n in WEIGHTS], *[out[n][3] for n in WEIGHTS])
```

```python
import functools
import math

import numpy as np
import jax
import jax.numpy as jnp
from jax import lax
from jax.experimental import pallas as pl
from jax.experimental.pallas import tpu as pltpu

F32 = jnp.float32
BF16 = jnp.bfloat16

N_DEV = 8
D_MODEL = 2048
DEPTH = 2
GROUP_W = 512
EPS = 1e-6
ROPE_THETA = 10000.0
LANES = 128

RET_HD = 128
RET_CHUNK = 256
RET_UNROLL = 2
GLA_CHUNK = 64
GLA_UNROLL = 4
GLA_CUM_ROWS = 256
GLA_DK = 64
GLA_TAU = 16.0
GLA_RANK = 16
POOL_WINDOWS = (2, 4, 8, 16)
MLA_QK = 192
MLA_ROPE = 64
ATTN_SCALE = MLA_QK ** -0.5
ATTN_Q_SCALE = ATTN_SCALE * math.log2(math.e)
IN_COLS = 5984

ADAM_LR = 0.001
ADAM_B1 = 0.9
ADAM_B2 = 0.999
ADAM_EPS = 1e-08
ADAM_WD = 0.01
ADAM_STEP = 10

A_Q, A_K, A_V, A_G = 0, 4, 8, 12
B_Q, B_K, B_V, B_G = 16, 18, 20, 24
M_Q, C_V, C_G, M_G = 28, 32, 36, 40
M_KV, GA, M_KR = 44, 46, 47
ZP_COLS = 48 * LANES

VMEM_LIMIT = 56 * 1024 * 1024


def _params(sem, vmem=VMEM_LIMIT):
    return pltpu.CompilerParams(dimension_semantics=sem, vmem_limit_bytes=vmem)


def _sigmoid(x):
    return 1.0 / (1.0 + jnp.exp(-x))


def _silu(x):
    return x * _sigmoid(x)


def _silu_grad(x):
    s = _sigmoid(x)
    return s * (1.0 + x * (1.0 - s))


def _dot(a, b, dims=(((1,), (0,)), ((), ()))):
    return lax.dot_general(a, b, dims, preferred_element_type=F32)


NT = (((1,), (1,)), ((), ()))
TN = (((0,), (0,)), ((), ()))


def _chunk_loop(n, body, init, unroll):
    unroll = math.gcd(n, unroll)

    def trip(t, carry):
        for u in range(unroll):
            carry = body(t * unroll + u, carry)
        return carry

    return lax.fori_loop(0, n // unroll, trip, init)


def _roll_lanes_half(x):
    return pltpu.roll(x, 64, 1)


def _wq_perm():
    idx = np.zeros((1024,), np.int32)
    ok = np.zeros((1024,), bool)
    for h in range(4):
        idx[128 * h:128 * h + 128] = 192 * h + np.arange(128)
        ok[128 * h:128 * h + 128] = True
        base = 512 + 128 * h
        idx[base:base + 32] = 192 * h + 128 + np.arange(32)
        ok[base:base + 32] = True
        idx[base + 64:base + 96] = 192 * h + 160 + np.arange(32)
        ok[base + 64:base + 96] = True
    inv = np.zeros((768,), np.int32)
    inv[idx[ok]] = np.nonzero(ok)[0]
    return idx, ok, inv


_WQ_IDX, _WQ_OK, _WQ_INV = _wq_perm()


def _pad_wq(wq):
    return jnp.where(jnp.asarray(_WQ_OK)[None, :], wq[:, _WQ_IDX], 0).astype(wq.dtype)


def _unpad_wq(wqp):
    return wqp[:, _WQ_INV]


def _qk_idx():
    idx = np.zeros((256,), np.int32)
    ok = np.zeros((256,), bool)
    idx[0:128] = np.arange(128)
    ok[0:128] = True
    idx[128:160] = 128 + np.arange(32)
    ok[128:160] = True
    idx[192:224] = 160 + np.arange(32)
    ok[192:224] = True
    inv = np.zeros((192,), np.int32)
    inv[idx[ok]] = np.nonzero(ok)[0]
    return idx, ok, inv


_QK_IDX, _QK_OK, _QK_INV = _qk_idx()


def _pad_qk_gain(g):
    return jnp.where(jnp.asarray(_QK_OK), g[_QK_IDX], 0.0).reshape(1, 256)


def _rope_tables(s):
    def tabs(dim):
        inv = 1.0 / (ROPE_THETA ** (jnp.arange(0, dim, 2, dtype=F32) / dim))
        ang = jnp.arange(s, dtype=F32)[:, None] * inv[None, :]
        return jnp.cos(ang), jnp.sin(ang)
    cr, sr = tabs(RET_HD)
    cos_r = jnp.concatenate([cr, cr], axis=1)
    sin_r = jnp.concatenate([-sr, sr], axis=1)
    cm, sm = tabs(MLA_ROPE)
    zz = jnp.zeros_like(cm)
    cos_m = jnp.concatenate([cm, zz, cm, zz], axis=1)
    sin_m = jnp.concatenate([-sm, zz, sm, zz], axis=1)
    return cos_r, sin_r, cos_m, sin_m


def _rope(x, cos, sin):
    return x * cos + _roll_lanes_half(x) * sin


def _rope_t(x, cos, sin):
    return x * cos + _roll_lanes_half(x * sin)


def _ret_tables():
    c = RET_CHUNK
    gamma_f = 1.0 - 2.0 ** (-5.0 - jnp.arange(4, dtype=F32))
    gamma_b = gamma_f[::-1]
    idx = jnp.arange(c, dtype=F32)
    diff = idx[:, None] - idx[None, :]

    def build(g1, g2):
        l1 = jnp.log(g1)[:, None, None]
        l2 = jnp.log(g2)[:, None, None]
        d1 = jnp.where(diff >= 0, jnp.exp(jnp.maximum(diff, 0.0)[None] * l1), 0.0)
        d2 = jnp.where(diff <= 0, jnp.exp(jnp.maximum(-diff, 0.0)[None] * l2), 0.0)
        ones = jnp.ones((1, c, LANES), F32)
        col = idx[None, :, None]
        qdf = jnp.exp((col + 1.0) * l1) * ones
        kdf = jnp.exp((c - 1.0 - col) * l1) * ones
        qdb = jnp.exp((c - col) * l2) * ones
        kdb = jnp.exp(col * l2) * ones
        cd1 = jnp.exp(c * l1) * ones
        cd2 = jnp.exp(c * l2) * ones
        return jnp.concatenate([d1 + d2, qdf, kdf, qdb, kdb, cd1, cd2], axis=2)

    return build(gamma_f, gamma_b), build(gamma_b, gamma_f)


MESH = pl.DeviceIdType.MESH
ANY = pl.BlockSpec(memory_space=pl.ANY)
_RELATIONS = ((0, 0, 1), (1, 0, 0), (0, 1, 0), (1, 1, 0), (1, 0, 1), (0, 1, 1), (1, 1, 1))


def _position():
    return lax.axis_index("x"), lax.axis_index("y"), lax.axis_index("c")


def _gather_copies(x_ref, out_ref, send_sems, recv_sems, local_sem, starting):
    x, y, cc = _position()
    me, sibling = (x, y, cc), (x, y, 1 - cc)
    chips = [(1 - x, y), (x, 1 - y), (1 - x, 1 - y)]

    def slab(px, py, pc):
        return out_ref.at[4 * px + 2 * py + pc]

    def copy(k, block, to, src=None):
        return pltpu.make_async_remote_copy(
            src_ref=slab(*block) if src is None else src, dst_ref=slab(*block),
            send_sem=send_sems.at[k], recv_sem=recv_sems.at[k], device_id=to, device_id_type=MESH)

    mine = pltpu.make_async_copy(x_ref, slab(*me), local_sem)
    first = [copy(0, me, sibling, src=x_ref)] + [copy(1 + j, me, (*chip, cc), src=x_ref) for j, chip in enumerate(chips)]
    if starting:
        return mine, first
    passed = [copy(4 + j, (*chip, cc), sibling) for j, chip in enumerate(chips)]
    arrivals = [copy(1 + j, (*chip, cc), me) for j, chip in enumerate(chips)]
    late = [copy(0, sibling, me)] + [copy(4 + j, (*chip, 1 - cc), me) for j, chip in enumerate(chips)]
    return mine, first, passed, arrivals, late


def _gather_start(*refs):
    mine, first = _gather_copies(*refs, starting=True)
    mine.start()
    for cp in first:
        cp.start()


def _gather_finish(*refs):
    mine, first, passed, arrivals, late = _gather_copies(*refs, starting=False)
    for arrived, onward in zip(arrivals, passed):
        arrived.wait_recv()
        onward.start()
    for cp in late:
        cp.wait_recv()
    for cp in first + passed:
        cp.wait_send()
    mine.wait()


def _scatter_copies(c_ref, out_ref, send_sems, recv_sems, local_sem):
    x, y, cc = _position()
    me = 4 * x + 2 * y + cc
    mine = pltpu.make_async_copy(c_ref.at[me], out_ref.at[me], local_sem)
    copies = []
    for k, (fx, fy, fc) in enumerate(_RELATIONS):
        px = 1 - x if fx else x
        py = 1 - y if fy else y
        pc = 1 - cc if fc else cc
        copies.append(pltpu.make_async_remote_copy(
            src_ref=c_ref.at[4 * px + 2 * py + pc], dst_ref=out_ref.at[me],
            send_sem=send_sems.at[k], recv_sem=recv_sems.at[k], device_id=(px, py, pc), device_id_type=MESH))
    return mine, copies


def _scatter_start(*refs):
    mine, copies = _scatter_copies(*refs)
    mine.start()
    for cp in copies:
        cp.start()


def _scatter_finish(*refs):
    mine, copies = _scatter_copies(*refs)
    for cp in copies:
        cp.wait()
    mine.wait()


_EXCHANGES = {"gather": (_gather_start, _gather_finish), "scatter": (_scatter_start, _scatter_finish)}


def _exchange_scratch():
    return [pltpu.SemaphoreType.DMA((7,)), pltpu.SemaphoreType.DMA((7,)), pltpu.SemaphoreType.DMA]


def _exchange_out(kind, src):
    return jax.ShapeDtypeStruct(((N_DEV,) + src.shape) if kind == "gather" else src.shape, src.dtype)


def _exchange(jobs, name):
    n = len(jobs)

    def body(*refs):
        srcs, outs, sems = refs[:n], refs[n:2 * n], refs[2 * n:]
        for half in (0, 1):
            for i, (kind, _) in enumerate(jobs):
                _EXCHANGES[kind][half](srcs[i], outs[i], *sems[3 * i:3 * i + 3])

    return pl.pallas_call(
        body, name=name, out_shape=[_exchange_out(kind, src) for kind, src in jobs],
        in_specs=[ANY] * n, out_specs=[ANY] * n,
        scratch_shapes=[sem for _ in jobs for sem in _exchange_scratch()])(*[src for _, src in jobs])


def _call(body, name, grid, in_specs, out_specs, out_shape, scratch, sem, args, rider=None):
    if rider is None:
        return pl.pallas_call(body, name=name, grid=grid, in_specs=in_specs, out_specs=out_specs, out_shape=out_shape,
                              scratch_shapes=scratch, compiler_params=_params(sem))(*args)
    jobs = rider if isinstance(rider, list) else [rider]
    ni, no, ns, nj = len(in_specs), len(out_specs), len(scratch), len(jobs)

    def carried(*refs):
        ins, rsrcs = refs[:ni], refs[ni:ni + nj]
        outs, routs = refs[ni + nj:ni + nj + no], refs[ni + nj + no:ni + 2 * nj + no]
        scr, sems = refs[ni + 2 * nj + no:ni + 2 * nj + no + ns], refs[ni + 2 * nj + no + ns:]
        ids = [pl.program_id(a) for a in range(len(grid))]
        is_first = functools.reduce(jnp.logical_and, [i == 0 for i in ids])
        is_last = functools.reduce(jnp.logical_and, [i == g - 1 for i, g in zip(ids, grid)])

        def half(which):
            for j, (kind, _) in enumerate(jobs):
                _EXCHANGES[kind][which](rsrcs[j], routs[j], *sems[3 * j:3 * j + 3])

        @pl.when(is_first)
        def _():
            half(0)

        body(*ins, *outs, *scr)

        @pl.when(is_last)
        def _():
            half(1)

    return pl.pallas_call(
        carried, name=name, grid=grid, in_specs=list(in_specs) + [ANY] * nj, out_specs=list(out_specs) + [ANY] * nj,
        out_shape=list(out_shape) + [_exchange_out(kind, src) for kind, src in jobs],
        scratch_shapes=list(scratch) + [sem for _ in jobs for sem in _exchange_scratch()],
        compiler_params=_params(("arbitrary",) * len(grid)))(*args, *[src for _, src in jobs])


def _inproj(x, g, wt, tm, tn=512, rider=None):
    s, d = x.shape
    n = wt.shape[0]

    def body(x_ref, g_ref, w_ref, z_ref, h_ref, hs):
        @pl.when(pl.program_id(1) == 0)
        def _():
            xv = x_ref[...]
            r = lax.rsqrt(jnp.mean(xv * xv, axis=-1, keepdims=True) + EPS)
            hv = (xv * r * g_ref[...]).astype(BF16)
            hs[...] = hv
            h_ref[...] = hv
        z_ref[...] = _dot(hs[...], w_ref[...], NT)

    return _call(
        body, "inproj", (s // tm, n // tn),
        [pl.BlockSpec((tm, d), lambda i, j: (i, 0)),
         pl.BlockSpec((1, d), lambda i, j: (0, 0)),
         pl.BlockSpec((tn, d), lambda i, j: (j, 0))],
        [pl.BlockSpec((tm, tn), lambda i, j: (i, j)), pl.BlockSpec((tm, d), lambda i, j: (i, 0))],
        [jax.ShapeDtypeStruct((s, n), F32), jax.ShapeDtypeStruct((s, d), BF16)],
        [pltpu.VMEM((tm, d), BF16)], ("parallel", "arbitrary"), (x, g, wt), rider)


def _relayout_plan():
    runs = ((0, 3584, 0), (3584, 3616, GA * LANES), (3616, 4640, C_V * LANES), (4640, 5152, M_Q * LANES),
            (5152, 5408, M_KV * LANES), (5408, 5440, M_KR * LANES), (5440, 5472, M_KR * LANES + 64),
            (5472, 5984, M_G * LANES))
    shard = IN_COLS // N_DEV
    plan = []
    for d in range(N_DEV):
        lo, hi = shard * d, shard * (d + 1)
        for a, b, p in runs:
            s, e = max(a, lo), min(b, hi)
            if s < e:
                plan.append((d, s - lo, p + (s - a), e - s))
    return plan


def _assemble_w_in(g, tc=512):
    _, c, r = g.shape
    tc = min(tc, r)

    def body(g_ref, o_ref):
        o_ref[...] = jnp.zeros_like(o_ref)
        for d, at, to, w in _relayout_plan():
            o_ref[to:to + w, :] = g_ref[d, at:at + w, :]

    return pl.pallas_call(
        body, name="assemble_w_in", grid=(r // tc,),
        in_specs=[pl.BlockSpec((N_DEV, c, tc), lambda i: (0, 0, i))],
        out_specs=pl.BlockSpec((ZP_COLS, tc), lambda i: (0, i)),
        out_shape=jax.ShapeDtypeStruct((ZP_COLS, r), g.dtype),
        compiler_params=_params(("parallel",)),
    )(g)


def _split_w_in(wt, tc=512):
    r = wt.shape[1]
    c = IN_COLS // N_DEV
    tc = min(tc, r)

    def body(w_ref, o_ref):
        for d, at, to, w in _relayout_plan():
            o_ref[d, at:at + w, :] = w_ref[to:to + w, :]

    return pl.pallas_call(
        body, name="split_w_in", grid=(r // tc,),
        in_specs=[pl.BlockSpec((ZP_COLS, tc), lambda i: (0, i))],
        out_specs=pl.BlockSpec((N_DEV, c, tc), lambda i: (0, 0, i)),
        out_shape=jax.ShapeDtypeStruct((N_DEV, c, r), wt.dtype),
        compiler_params=_params(("parallel",)),
    )(wt)


def _mm(a, b, mode, name, tm, tn, tk, add=None, out_dtype=F32, rider=None, tail=None):
    if mode == "tn":
        k, m = a.shape
    else:
        m, k = a.shape
    n = b.shape[0] if mode == "nt" else b.shape[1]
    tm, tn, tk = min(tm, m), min(tn, n), min(tk, k)
    nk = k // tk
    dims = {"nn": (((1,), (0,)), ((), ())), "nt": NT, "tn": TN}[mode]
    if tail is None:
        def plain(acc, i, extra_refs, out_refs):
            out_refs[0][...] = (acc + extra_refs[0][...] if extra_refs else acc).astype(out_dtype)
        tail = ([(add, "tile")] if add is not None else [], [(out_dtype, "tile")], plain)
    extra, outs, fn = tail
    spec = {"tile": pl.BlockSpec((tm, tn), lambda i, j, kk: (i, j)),
            "row": pl.BlockSpec((1, tn), lambda i, j, kk: (0, j)),
            "lanes": pl.BlockSpec((1, LANES), lambda i, j, kk: (0, 0))}
    shape = {"tile": (m, n), "row": (1, n), "lanes": (1, LANES)}
    ne, no = len(extra), len(outs)

    def body(*refs):
        a_ref, b_ref = refs[:2]
        extra_refs, out_refs, acc = refs[2:2 + ne], refs[2 + ne:2 + ne + no], refs[2 + ne + no]
        i, kk = pl.program_id(0), pl.program_id(2)

        @pl.when(kk == 0)
        def _():
            acc[...] = jnp.zeros_like(acc)

        acc[...] += _dot(a_ref[...].astype(BF16), b_ref[...].astype(BF16), dims)

        @pl.when(kk == nk - 1)
        def _():
            fn(acc[...], i, extra_refs, out_refs)

    a_spec = (pl.BlockSpec((tk, tm), lambda i, j, kk: (kk, i)) if mode == "tn"
              else pl.BlockSpec((tm, tk), lambda i, j, kk: (i, kk)))
    b_spec = (pl.BlockSpec((tn, tk), lambda i, j, kk: (j, kk)) if mode == "nt"
              else pl.BlockSpec((tk, tn), lambda i, j, kk: (kk, j)))
    summed = any(kind != "tile" for _, kind in outs)
    res = _call(body, name, (m // tm, n // tn, nk), [a_spec, b_spec] + [spec[kind] for _, kind in extra],
                [spec[kind] for _, kind in outs], [jax.ShapeDtypeStruct(shape[kind], dt) for dt, kind in outs],
                [pltpu.VMEM((tm, tn), F32)], ("arbitrary",) * 3 if summed else ("parallel", "parallel", "arbitrary"),
                [a, b] + [arr for arr, _ in extra], rider)
    return res[0] if (rider is None and no == 1) else res


def _norm_bwd_tail(x, g, dres):
    def fn(dh, i, extra_refs, out_refs):
        x_ref, g_ref, dres_ref = extra_refs
        dx_ref, dg_ref = out_refs

        @pl.when(i == 0)
        def _():
            dg_ref[...] = jnp.zeros_like(dg_ref)

        xv = x_ref[...]
        r = lax.rsqrt(jnp.mean(xv * xv, axis=-1, keepdims=True) + EPS)
        nv = xv * r
        dg_ref[...] += jnp.sum(dh * nv, axis=0, keepdims=True)
        u = dh * g_ref[...]
        dx_ref[...] = dres_ref[...] + r * (u - nv * jnp.mean(u * nv, axis=-1, keepdims=True))

    return [(x, "tile"), (g, "row"), (dres, "tile")], [(F32, "tile"), (F32, "row")], fn


def _loss_tail(x, target):
    d = x.shape[1]

    def fn(acc, i, extra_refs, out_refs):
        x_ref, t_ref = extra_refs
        dx_ref, loss_ref = out_refs

        @pl.when(i == 0)
        def _():
            loss_ref[...] = jnp.zeros_like(loss_ref)

        err = acc + x_ref[...] - t_ref[...]
        dx_ref[...] = err * (1.0 / d)
        per_tok = jnp.mean(err * err, axis=-1, keepdims=True)
        loss_ref[...] += 0.5 * jnp.sum(per_tok, axis=0, keepdims=True)

    return [(x, "tile"), (target, "tile")], [(F32, "tile"), (F32, "lanes")], fn


def _ret_core(q_ref, k_ref, v_ref, tab_ref, out_ref, back_ref, nchunk):
    c = RET_CHUNK

    def rows(n):
        return pl.ds(pl.multiple_of(n * c, c), c)

    zero = jnp.zeros((LANES, LANES), F32)

    def plane(i, n=c):
        return tab_ref[0:n, c + LANES * i:c + LANES * (i + 1)]

    def fwd(n, st):
        r = rows(n)
        q, k, vb = q_ref[r, :], k_ref[r, :], v_ref[r, :].astype(BF16)
        sc = _dot(q.astype(BF16), k.astype(BF16), NT) * tab_ref[:, 0:c]
        o = _dot(sc.astype(BF16), vb)
        o = o + _dot((q * plane(0)).astype(BF16), st.astype(BF16))
        out_ref[r, :] = o
        return st * plane(4, LANES) + _dot((k * plane(1)).astype(BF16), vb, TN)

    def bwd(i, st):
        r = rows(nchunk - 1 - i)
        q, k, vb = q_ref[r, :], k_ref[r, :], v_ref[r, :].astype(BF16)
        back_ref[r, :] = _dot((q * plane(2)).astype(BF16), st.astype(BF16))
        return st * plane(5, LANES) + _dot((k * plane(3)).astype(BF16), vb, TN)

    def both(i, states):
        return fwd(i, states[0]), bwd(i, states[1])

    _chunk_loop(nchunk, both, (zero, zero), RET_UNROLL)
    out_ref[...] += back_ref[...]


def _ret_fwd(z, cos_r, sin_r, tab, norm_g):
    s = z.shape[0]
    nchunk = s // RET_CHUNK
    scale = RET_HD ** -0.5
    col = lambda base: pl.BlockSpec((s, LANES), lambda h: (0, base + h), pipeline_mode=pl.Buffered(1))

    def body(q_ref, k_ref, v_ref, g_ref, cos_ref, sin_ref, tab_ref, ng_ref, o_ref, y_ref, qh, kh, back):
        qh[...] = _rope(q_ref[...], cos_ref[...], sin_ref[...])
        kh[...] = _rope(k_ref[...], cos_ref[...], sin_ref[...]) * scale
        _ret_core(qh, kh, v_ref, tab_ref, o_ref, back, nchunk)
        o = o_ref[...]
        r = lax.rsqrt(jnp.mean(o * o, axis=-1, keepdims=True) + EPS)
        y_ref[...] = (_silu(g_ref[...]) * (o * r * ng_ref[...])).astype(BF16)

    return pl.pallas_call(
        body, name="ret_fwd", grid=(4,),
        in_specs=[col(A_Q), col(A_K), col(A_V), col(A_G),
                  pl.BlockSpec((s, LANES), lambda h: (0, 0), pipeline_mode=pl.Buffered(1)),
                  pl.BlockSpec((s, LANES), lambda h: (0, 0), pipeline_mode=pl.Buffered(1)),
                  pl.BlockSpec((None, RET_CHUNK, RET_CHUNK + 6 * LANES), lambda h: (h, 0, 0)),
                  pl.BlockSpec((1, LANES), lambda h: (0, h))],
        out_specs=[pl.BlockSpec((s, LANES), lambda h: (0, h)), pl.BlockSpec((s, LANES), lambda h: (0, h))],
        out_shape=[jax.ShapeDtypeStruct((s, GROUP_W), F32), jax.ShapeDtypeStruct((s, GROUP_W), BF16)],
        scratch_shapes=[pltpu.VMEM((s, LANES), F32)] * 3,
        compiler_params=_params(("arbitrary",)),
    )(z, z, z, z, cos_r, sin_r, tab, norm_g)


def _ret_bwd(z, d_o, cos_r, sin_r, tab, tab_sw, rider=None):
    s = z.shape[0]
    nchunk = s // RET_CHUNK
    scale = RET_HD ** -0.5
    col = lambda base: pl.BlockSpec((s, LANES), lambda h: (0, base + h), pipeline_mode=pl.Buffered(1))
    whole = lambda: pl.BlockSpec((s, LANES), lambda h: (0, 0), pipeline_mode=pl.Buffered(1))
    tabspec = lambda: pl.BlockSpec((None, RET_CHUNK, RET_CHUNK + 6 * LANES), lambda h: (h, 0, 0))
    outspec = lambda: pl.BlockSpec((s, LANES), lambda h: (0, h))

    def body(q_ref, k_ref, v_ref, do_ref, cos_ref, sin_ref, tab_ref, tsw_ref, dq_ref, dk_ref, dv_ref,
             qh, kh, tmp, back):
        cos, sin = cos_ref[...], sin_ref[...]
        qh[...] = _rope(q_ref[...], cos, sin)
        kh[...] = _rope(k_ref[...], cos, sin) * scale
        _ret_core(kh, qh, do_ref, tsw_ref, tmp, back, nchunk)
        dv_ref[...] = tmp[...].astype(BF16)
        _ret_core(do_ref, v_ref, kh, tab_ref, tmp, back, nchunk)
        dq_ref[...] = _rope_t(tmp[...], cos, sin).astype(BF16)
        _ret_core(v_ref, do_ref, qh, tsw_ref, tmp, back, nchunk)
        dk_ref[...] = _rope_t(tmp[...] * scale, cos, sin).astype(BF16)

    return _call(
        body, "ret_bwd", (4,),
        [col(A_Q), col(A_K), col(A_V),
         pl.BlockSpec((s, LANES), lambda h: (0, h), pipeline_mode=pl.Buffered(1)),
         whole(), whole(), tabspec(), tabspec()],
        [outspec(), outspec(), outspec()],
        [jax.ShapeDtypeStruct((s, GROUP_W), BF16)] * 3,
        [pltpu.VMEM((s, LANES), F32)] * 4,
        ("arbitrary",), (z, z, z, d_o, cos_r, sin_r, tab, tab_sw), rider)


def _normgate_bwd(o, z, gate_blk, dy, dy_blk, norm_g, tm):
    s = o.shape[0]

    def body(o_ref, g_ref, dy_ref, ng_ref, do_ref, dg_ref, dng_ref):
        @pl.when(pl.program_id(0) == 0)
        def _():
            dng_ref[...] = jnp.zeros_like(dng_ref)

        for h in range(4):
            sl = slice(LANES * h, LANES * (h + 1))
            ov, gv, dyv, ng = o_ref[:, sl], g_ref[:, sl], dy_ref[:, sl], ng_ref[:, sl]
            r = lax.rsqrt(jnp.mean(ov * ov, axis=-1, keepdims=True) + EPS)
            on = ov * r
            dn = dyv * _silu(gv)
            u = dn * ng
            do_ref[:, sl] = r * (u - on * jnp.mean(u * on, axis=-1, keepdims=True))
            dg_ref[:, sl] = (dyv * (on * ng) * _silu_grad(gv)).astype(BF16)
            dng_ref[:, sl] += jnp.sum(dn * on, axis=0, keepdims=True)

    return pl.pallas_call(
        body, name="normgate_bwd", grid=(s // tm,),
        in_specs=[pl.BlockSpec((tm, GROUP_W), lambda i: (i, 0)),
                  pl.BlockSpec((tm, GROUP_W), lambda i: (i, gate_blk // 4)),
                  pl.BlockSpec((tm, GROUP_W), lambda i: (i, dy_blk)),
                  pl.BlockSpec((1, GROUP_W), lambda i: (0, 0))],
        out_specs=[pl.BlockSpec((tm, GROUP_W), lambda i: (i, 0)), pl.BlockSpec((tm, GROUP_W), lambda i: (i, 0)),
                   pl.BlockSpec((1, GROUP_W), lambda i: (0, 0))],
        out_shape=[jax.ShapeDtypeStruct((s, GROUP_W), F32), jax.ShapeDtypeStruct((s, GROUP_W), BF16),
                   jax.ShapeDtypeStruct((1, GROUP_W), F32)],
        compiler_params=_params(("arbitrary",)),
    )(o, z, dy, norm_g)


def _log_sigmoid(x):
    return jnp.minimum(x, 0.0) - jnp.log(1.0 + jnp.exp(-jnp.abs(x)))


def _gla_consts():
    c = GLA_CHUNK
    row = lax.broadcasted_iota(jnp.int32, (c, c), 0)
    colm = lax.broadcasted_iota(jnp.int32, (c, c), 1)
    lane = lax.broadcasted_iota(jnp.int32, (1, LANES), 1)
    low = row >= colm
    up = colm >= row
    heads = ((lane < GLA_DK).astype(F32), (lane >= GLA_DK).astype(F32))
    return low, up, heads


def _chunk_tri(upper):
    r = lax.broadcasted_iota(jnp.int32, (GLA_CUM_ROWS, GLA_CUM_ROWS), 0)
    c = lax.broadcasted_iota(jnp.int32, (GLA_CUM_ROWS, GLA_CUM_ROWS), 1)
    shift = GLA_CHUNK.bit_length() - 1
    same = jnp.right_shift(r, shift) == jnp.right_shift(c, shift)
    return jnp.where(jnp.logical_and(same, (c >= r) if upper else (r >= c)), 1.0, 0.0).astype(BF16)


def _exact_tri_matmul(tri, x):
    hi = x.astype(BF16)
    rest = x - hi.astype(F32)
    mid = rest.astype(BF16)
    lo = (rest - mid.astype(F32)).astype(BF16)
    return _dot(tri, hi) + _dot(tri, mid) + _dot(tri, lo)


def _gla_chunk(cum_ref, d, n):
    c = GLA_CHUNK
    cum = cum_ref[d, pl.ds(pl.multiple_of(n * c, c), c), :]
    last = cum_ref[d, pl.ds(n * c + (c - 1 if d == 0 else 0), 1), :]
    eq = jnp.exp(cum)
    ek = jnp.exp(-cum)
    el = jnp.exp(last - cum)
    dec = jnp.exp(last)
    return eq, ek, el, dec


def _gla_gates(ga_ref, wa_ref, ba_ref, cum_ref, s, upper):
    tri = _chunk_tri(upper)
    rows = min(s, GLA_CUM_ROWS)

    def step(i, carry):
        r = pl.ds(pl.multiple_of(i * rows, rows), rows)
        pre = _dot(ga_ref[r, :].astype(BF16), wa_ref[...].astype(BF16)) + ba_ref[...]
        cum_ref[r, :] = _exact_tri_matmul(tri[0:rows, 0:rows], _log_sigmoid(pre) * (1.0 / GLA_TAU))
        return carry
    lax.fori_loop(0, s // rows, step, 0)


def _gla_fwd(z, wa_f, wa_b, ba_f, ba_b, norm_g):
    s = z.shape[0]
    c = GLA_CHUNK
    nchunk = s // c
    scale = GLA_DK ** -0.5
    tm = min(s, 512)
    one = pl.Buffered(1)

    def body(q_ref, k_ref, v_ref, ga_ref, g_ref, waf_ref, wab_ref, baf_ref, bab_ref, ng_ref, o_ref, y_ref,
             la_s):
        low, up, heads = _gla_consts()
        _gla_gates(ga_ref, waf_ref, baf_ref, la_s.at[0], s, False)
        _gla_gates(ga_ref, wab_ref, bab_ref, la_s.at[1], s, True)
        for d in range(2):
            tri = (low, up)[d]

            def step(i, states):
                n = i if d == 0 else nchunk - 1 - i
                r = pl.ds(pl.multiple_of(n * c, c), c)
                q = q_ref[r, :] * scale
                k = k_ref[r, :]
                eq, ek, el, dec = _gla_chunk(la_s, d, n)
                qt = q * eq
                ktb = (k * ek).astype(BF16)
                kl = k * el
                new_states = []
                for hh in range(2):
                    cols = slice(LANES * hh, LANES * (hh + 1))
                    vb = v_ref[r, cols].astype(BF16)
                    qm = (qt * heads[hh]).astype(BF16)
                    a = jnp.where(tri, _dot(qm, ktb, NT), 0.0)
                    o = _dot(a.astype(BF16), vb) + _dot(qm, states[hh].astype(BF16), NT)
                    if d == 0:
                        o_ref[r, cols] = o
                    else:
                        o_ref[r, cols] += o
                    new_states.append(states[hh] * dec + _dot(vb, (kl * heads[hh]).astype(BF16), TN))
                return tuple(new_states)

            zero = jnp.zeros((LANES, LANES), F32)
            _chunk_loop(nchunk, step, (zero, zero), GLA_UNROLL)

        def epi(i, carry):
            r = pl.ds(pl.multiple_of(i * tm, tm), tm)
            for hh in range(2):
                cols = slice(LANES * hh, LANES * (hh + 1))
                o = o_ref[r, cols]
                rr = lax.rsqrt(jnp.mean(o * o, axis=-1, keepdims=True) + EPS)
                y_ref[r, cols] = (_silu(g_ref[r, cols]) * (o * rr * ng_ref[:, cols])).astype(BF16)
            return carry

        lax.fori_loop(0, s // tm, epi, 0)

    w2 = 2 * LANES
    return pl.pallas_call(
        body, name="gla_fwd", grid=(2,),
        in_specs=[pl.BlockSpec((s, LANES), lambda p: (0, B_Q + p), pipeline_mode=one),
                  pl.BlockSpec((s, LANES), lambda p: (0, B_K + p), pipeline_mode=one),
                  pl.BlockSpec((s, w2), lambda p: (0, B_V // 2 + p), pipeline_mode=one),
                  pl.BlockSpec((s, LANES), lambda p: (0, GA), pipeline_mode=one),
                  pl.BlockSpec((s, w2), lambda p: (0, B_G // 2 + p), pipeline_mode=one),
                  pl.BlockSpec((LANES, LANES), lambda p: (0, p)),
                  pl.BlockSpec((LANES, LANES), lambda p: (0, p)),
                  pl.BlockSpec((1, LANES), lambda p: (0, p)),
                  pl.BlockSpec((1, LANES), lambda p: (0, p)),
                  pl.BlockSpec((1, w2), lambda p: (0, p))],
        out_specs=[pl.BlockSpec((s, w2), lambda p: (0, p)), pl.BlockSpec((s, w2), lambda p: (0, p))],
        out_shape=[jax.ShapeDtypeStruct((s, GROUP_W), F32), jax.ShapeDtypeStruct((s, GROUP_W), BF16)],
        scratch_shapes=[pltpu.VMEM((2, s, LANES), F32)],
        compiler_params=_params(("arbitrary",)),
    )(z, z, z, z, z, wa_f, wa_b, ba_f, ba_b, norm_g)


def _gla_bwd(z, d_o, wa_f, wa_b, ba_f, ba_b, rider=None):
    s = z.shape[0]
    c = GLA_CHUNK
    nchunk = s // c
    scale = GLA_DK ** -0.5
    tm = min(s, GLA_CUM_ROWS)
    one = pl.Buffered(1)

    def body(q_ref, k_ref, v_ref, ga_ref, do_ref, waf_ref, wab_ref, baf_ref, bab_ref,
             dq_ref, dk_ref, dv_ref, dga_ref, dwaf_ref, dwab_ref, dbaf_ref, dbab_ref,
             la_s, dla_s, stash, dq_s, dk_s, dv_s):
        low, up, heads = _gla_consts()
        rowi = lax.broadcasted_iota(jnp.int32, (c, 1), 0)
        _gla_gates(ga_ref, waf_ref, baf_ref, la_s.at[0], s, False)
        _gla_gates(ga_ref, wab_ref, bab_ref, la_s.at[1], s, True)
        for d in range(2):
            tri = (low, up)[d]
            last_row = (rowi == (c - 1 if d == 0 else 0)).astype(F32)
            order = (lambda i: i) if d == 0 else (lambda i: nchunk - 1 - i)
            zero = jnp.zeros((LANES, LANES), F32)

            def states(i, sts):
                n = order(i)
                r = pl.ds(pl.multiple_of(n * c, c), c)
                k = k_ref[r, :]
                _, _, el, dec = _gla_chunk(la_s, d, n)
                kl = k * el
                new = []
                for hh in range(2):
                    cols = slice(LANES * hh, LANES * (hh + 1))
                    stash[hh, n] = sts[hh]
                    new.append(sts[hh] * dec + _dot(v_ref[r, cols].astype(BF16), (kl * heads[hh]).astype(BF16), TN))
                return tuple(new)

            _chunk_loop(nchunk, states, (zero, zero), GLA_UNROLL)

            def step(i, dsts):
                n = order(nchunk - 1 - i)
                r = pl.ds(pl.multiple_of(n * c, c), c)
                q = q_ref[r, :] * scale
                k = k_ref[r, :]
                eq, ek, el, dec = _gla_chunk(la_s, d, n)
                qt = q * eq
                kt = k * ek
                kl = k * el
                ktb = kt.astype(BF16)
                dqt = jnp.zeros((c, LANES), F32)
                dkt = jnp.zeros((c, LANES), F32)
                dkl = jnp.zeros((c, LANES), F32)
                ddec = jnp.zeros((1, LANES), F32)
                new = []
                for hh in range(2):
                    cols = slice(LANES * hh, LANES * (hh + 1))
                    vb = v_ref[r, cols].astype(BF16)
                    dob = do_ref[r, cols].astype(BF16)
                    qm = (qt * heads[hh]).astype(BF16)
                    a = jnp.where(tri, _dot(qm, ktb, NT), 0.0).astype(BF16)
                    da = jnp.where(tri, _dot(dob, vb, NT), 0.0).astype(BF16)
                    sn = stash[hh, n]
                    dst = dsts[hh]
                    dstb = dst.astype(BF16)
                    dqt = dqt + (_dot(da, ktb) + _dot(dob, sn.astype(BF16))) * heads[hh]
                    dkt = dkt + _dot(da, qm, TN)
                    dv = _dot(a, dob, TN) + _dot((kl * heads[hh]).astype(BF16), dstb, NT)
                    dkl = dkl + _dot(vb, dstb)
                    ddec = ddec + jnp.sum(dst * sn, axis=0, keepdims=True)
                    new.append(dst * dec + _dot(dob, qm, TN))
                    if d == 0:
                        dv_s[r, cols] = dv
                    else:
                        dv_ref[r, cols] = (dv_s[r, cols] + dv).astype(BF16)
                dlast = ddec * dec + jnp.sum(dkl * kl, axis=0, keepdims=True)
                dq = dqt * eq * scale
                dk = dkt * ek + dkl * el
                dcum = dqt * qt - dkt * kt - dkl * kl + last_row * dlast
                dla_s[d, r, :] = dcum
                if d == 0:
                    dq_s[r, :] = dq
                    dk_s[r, :] = dk
                else:
                    dq_ref[r, :] = (dq_s[r, :] + dq).astype(BF16)
                    dk_ref[r, :] = (dk_s[r, :] + dk).astype(BF16)
                return tuple(new)

            _chunk_loop(nchunk, step, (zero, zero), GLA_UNROLL)

        first = pl.program_id(0) == 0
        for d, (wa_ref, ba_ref, dwa_ref, dba_ref) in enumerate(
                ((waf_ref, baf_ref, dwaf_ref, dbaf_ref), (wab_ref, bab_ref, dwab_ref, dbab_ref))):
            dwa_ref[...] = jnp.zeros_like(dwa_ref)
            dba_ref[...] = jnp.zeros_like(dba_ref)
            tri_t = _chunk_tri(d == 0)[0:tm, 0:tm]

            def gates(i, carry):
                r = pl.ds(pl.multiple_of(i * tm, tm), tm)
                gab = ga_ref[r, :].astype(BF16)
                wab16 = wa_ref[...].astype(BF16)
                pre = _dot(gab, wab16) + ba_ref[...]
                dpre = _exact_tri_matmul(tri_t, dla_s[d, r, :]) * (1.0 / GLA_TAU) * _sigmoid(-pre)
                dpb = dpre.astype(BF16)
                dwa_ref[...] += _dot(gab, dpb, TN)
                dba_ref[...] += jnp.sum(dpre, axis=0, keepdims=True)
                dga = _dot(dpb, wab16, NT)
                if d == 0:
                    @pl.when(first)
                    def _():
                        dga_ref[r, :] = dga

                    @pl.when(jnp.logical_not(first))
                    def _():
                        dga_ref[r, :] += dga
                else:
                    dga_ref[r, :] += dga
                return carry

            lax.fori_loop(0, s // tm, gates, 0)

    w2 = 2 * LANES
    return _call(
        body, "gla_bwd", (2,),
        [pl.BlockSpec((s, LANES), lambda p: (0, B_Q + p), pipeline_mode=one),
         pl.BlockSpec((s, LANES), lambda p: (0, B_K + p), pipeline_mode=one),
         pl.BlockSpec((s, w2), lambda p: (0, B_V // 2 + p), pipeline_mode=one),
         pl.BlockSpec((s, LANES), lambda p: (0, GA), pipeline_mode=one),
         pl.BlockSpec((s, w2), lambda p: (0, p), pipeline_mode=one),
         pl.BlockSpec((LANES, LANES), lambda p: (0, p)),
         pl.BlockSpec((LANES, LANES), lambda p: (0, p)),
         pl.BlockSpec((1, LANES), lambda p: (0, p)),
         pl.BlockSpec((1, LANES), lambda p: (0, p))],
        [pl.BlockSpec((s, LANES), lambda p: (0, p), pipeline_mode=one),
         pl.BlockSpec((s, LANES), lambda p: (0, p), pipeline_mode=one),
         pl.BlockSpec((s, w2), lambda p: (0, p), pipeline_mode=one),
         pl.BlockSpec((s, LANES), lambda p: (0, 0), pipeline_mode=one),
         pl.BlockSpec((LANES, LANES), lambda p: (0, p)),
         pl.BlockSpec((LANES, LANES), lambda p: (0, p)),
         pl.BlockSpec((1, LANES), lambda p: (0, p)),
         pl.BlockSpec((1, LANES), lambda p: (0, p))],
        [jax.ShapeDtypeStruct((s, w2), BF16), jax.ShapeDtypeStruct((s, w2), BF16),
         jax.ShapeDtypeStruct((s, GROUP_W), BF16), jax.ShapeDtypeStruct((s, LANES), F32),
         jax.ShapeDtypeStruct((LANES, w2), F32), jax.ShapeDtypeStruct((LANES, w2), F32),
         jax.ShapeDtypeStruct((1, w2), F32), jax.ShapeDtypeStruct((1, w2), F32)],
        [pltpu.VMEM((2, s, LANES), F32), pltpu.VMEM((2, s, LANES), F32),
         pltpu.VMEM((2, nchunk, LANES, LANES), F32),
         pltpu.VMEM((s, LANES), F32), pltpu.VMEM((s, LANES), F32), pltpu.VMEM((s, w2), F32)],
        ("arbitrary",), (z, z, z, z, d_o, wa_f, wa_b, ba_f, ba_b), rider)


def _shift_rows(x, d, rowi):
    s = x.shape[0]
    if d == 0:
        return x
    y = pltpu.roll(x, d % s, 0)
    keep = (rowi >= d) if d > 0 else (rowi < s + d)
    return jnp.where(keep, y, 0.0)


def _run_sum(x, m, step, rowi):
    acc, n = x, 1
    while n < m:
        acc = acc + _shift_rows(acc, step * n, rowi)
        n *= 2
    return acc


def _pool_counts(s, w, rowi):
    hi = jnp.minimum(rowi + w // 2, s)
    lo = jnp.maximum(rowi - w // 2, 0)
    return (hi - lo).astype(F32)


def _pooled(u, w, rowi):
    s = u.shape[0]
    win = _shift_rows(_run_sum(u, w // 2, 1, rowi), 1, rowi) + _run_sum(u, w // 2, -1, rowi)
    return win / _pool_counts(s, w, rowi) - u


def _pool_fwd(z, pool_w, pool_scale):
    s = z.shape[0]
    one = pl.Buffered(1)

    def body(u_ref, g_ref, w_ref, sc_ref, y_ref):
        rowi = lax.broadcasted_iota(jnp.int32, (s, 1), 0)
        for g, w in enumerate(POOL_WINDOWS):
            cols = slice(LANES * g, LANES * (g + 1))
            pooled = _pooled(u_ref[:, cols], w, rowi)
            mixed = _dot(pooled.astype(BF16), w_ref[g].astype(BF16))
            y_ref[:, cols] = (_silu(g_ref[:, cols]) * (mixed * sc_ref[:, cols])).astype(BF16)

    return pl.pallas_call(
        body, name="pool_fwd", grid=(1,),
        in_specs=[pl.BlockSpec((s, GROUP_W), lambda i: (0, C_V // 4), pipeline_mode=one),
                  pl.BlockSpec((s, GROUP_W), lambda i: (0, C_G // 4), pipeline_mode=one),
                  pl.BlockSpec((4, LANES, LANES), lambda i: (0, 0, 0)),
                  pl.BlockSpec((1, GROUP_W), lambda i: (0, 0))],
        out_specs=pl.BlockSpec((s, GROUP_W), lambda i: (0, 0), pipeline_mode=one),
        out_shape=jax.ShapeDtypeStruct((s, GROUP_W), BF16),
        compiler_params=_params(("arbitrary",)),
    )(z, z, pool_w, pool_scale)


def _pool_bwd(z, dy, pool_w, pool_scale):
    s = z.shape[0]
    one = pl.Buffered(1)

    def body(u_ref, g_ref, dy_ref, w_ref, sc_ref, du_ref, dg_ref, dw_ref, dsc_ref):
        rowi = lax.broadcasted_iota(jnp.int32, (s, 1), 0)
        for g, w in enumerate(POOL_WINDOWS):
            cols = slice(LANES * g, LANES * (g + 1))
            gate, dyv, sc = g_ref[:, cols], dy_ref[:, cols], sc_ref[:, cols]
            wb = w_ref[g].astype(BF16)
            pooled = _pooled(u_ref[:, cols], w, rowi)
            pb = pooled.astype(BF16)
            mixed = _dot(pb, wb)
            dg_ref[:, cols] = (dyv * (mixed * sc) * _silu_grad(gate)).astype(BF16)
            dt = dyv * _silu(gate)
            dsc_ref[:, cols] = jnp.sum(dt * mixed, axis=0, keepdims=True)
            dmb = (dt * sc).astype(BF16)
            dw_ref[g] = _dot(pb, dmb, TN)
            dpool = _dot(dmb, wb, NT)
            e = dpool / _pool_counts(s, w, rowi)
            du_ref[:, cols] = (_run_sum(e, w // 2, 1, rowi) + _shift_rows(_run_sum(e, w // 2, -1, rowi), -1, rowi)
                               - dpool).astype(BF16)

    return pl.pallas_call(
        body, name="pool_bwd", grid=(1,),
        in_specs=[pl.BlockSpec((s, GROUP_W), lambda i: (0, C_V // 4), pipeline_mode=one),
                  pl.BlockSpec((s, GROUP_W), lambda i: (0, C_G // 4), pipeline_mode=one),
                  pl.BlockSpec((s, GROUP_W), lambda i: (0, 2), pipeline_mode=one),
                  pl.BlockSpec((4, LANES, LANES), lambda i: (0, 0, 0)),
                  pl.BlockSpec((1, GROUP_W), lambda i: (0, 0))],
        out_specs=[pl.BlockSpec((s, GROUP_W), lambda i: (0, 0), pipeline_mode=one),
                   pl.BlockSpec((s, GROUP_W), lambda i: (0, 0), pipeline_mode=one),
                   pl.BlockSpec((4, LANES, LANES), lambda i: (0, 0, 0)),
                   pl.BlockSpec((1, GROUP_W), lambda i: (0, 0))],
        out_shape=[jax.ShapeDtypeStruct((s, GROUP_W), BF16), jax.ShapeDtypeStruct((s, GROUP_W), BF16),
                   jax.ShapeDtypeStruct((4, LANES, LANES), F32), jax.ShapeDtypeStruct((1, GROUP_W), F32)],
        compiler_params=_params(("arbitrary",)),
    )(z, z, dy, pool_w, pool_scale)


def _mla_heads(qf, kv, kpe, qg, kg, cos, sin):
    out = []
    for h in range(4):
        qa = qf[:, LANES * h:LANES * (h + 1)]
        qb = qf[:, 512 + LANES * h:512 + LANES * (h + 1)]
        ka = kv[:, 256 * h:256 * h + LANES]
        rq = lax.rsqrt((jnp.sum(qa * qa, axis=-1, keepdims=True) + jnp.sum(qb * qb, axis=-1, keepdims=True))
                       * (1.0 / MLA_QK) + EPS)
        rk = lax.rsqrt((jnp.sum(ka * ka, axis=-1, keepdims=True) + jnp.sum(kpe * kpe, axis=-1, keepdims=True))
                       * (1.0 / MLA_QK) + EPS)
        out.append((qa, qb, rq, ka, rk))
    return out


def _mla_latents(mq_ref, mkv_ref, gq_ref, gkv_ref, wq_ref, wkv_ref):
    mq = mq_ref[...]
    rq = lax.rsqrt(jnp.mean(mq * mq, axis=-1, keepdims=True) + EPS)
    qn = mq * rq
    qnb = (qn * gq_ref[...]).astype(BF16)
    mkv = mkv_ref[...]
    rk = lax.rsqrt(jnp.mean(mkv * mkv, axis=-1, keepdims=True) + EPS)
    kvn = mkv * rk
    kvnb = (kvn * gkv_ref[...]).astype(BF16)
    qf = _dot(qnb, wq_ref[...])
    kv = _dot(kvnb, wkv_ref[...])
    return qn, rq, qnb, kvn, rk, kvnb, qf, kv


def _mla_prep(z, cos_m, sin_m, gq, wq, gkv, wkv, qg, kg, tm):
    s = z.shape[0]

    def body(mq_ref, mkv_ref, mkr_ref, cos_ref, sin_ref, gq_ref, wq_ref, gkv_ref, wkv_ref, qg_ref, kg_ref,
             q_ref, k_ref, v_ref):
        _, _, _, _, _, _, qf, kv = _mla_latents(mq_ref, mkv_ref, gq_ref, gkv_ref, wq_ref, wkv_ref)
        kpe = mkr_ref[...]
        cos, sin = cos_ref[...], sin_ref[...]
        qg, kg = qg_ref[...], kg_ref[...]
        for h, (qa, qb, rq, ka, rk) in enumerate(_mla_heads(qf, kv, kpe, qg, kg, cos, sin)):
            q_ref[h, :, 0:LANES] = (qa * rq * qg[:, 0:LANES] * ATTN_Q_SCALE).astype(BF16)
            q_ref[h, :, LANES:] = (_rope(qb * rq * qg[:, LANES:], cos, sin) * ATTN_Q_SCALE).astype(BF16)
            k_ref[h, :, 0:LANES] = (ka * rk * kg[:, 0:LANES]).astype(BF16)
            k_ref[h, :, LANES:] = _rope(kpe * rk * kg[:, LANES:], cos, sin).astype(BF16)
            v_ref[h] = kv[:, 256 * h + LANES:256 * (h + 1)].astype(BF16)

    full = lambda shape: pl.BlockSpec(shape, lambda i: (0,) * len(shape))
    return pl.pallas_call(
        body, name="mla_prep", grid=(s // tm,),
        in_specs=[pl.BlockSpec((tm, 512), lambda i: (i, M_Q // 4)),
                  pl.BlockSpec((tm, 256), lambda i: (i, M_KV // 2)),
                  pl.BlockSpec((tm, LANES), lambda i: (i, M_KR)),
                  pl.BlockSpec((tm, LANES), lambda i: (i, 0)),
                  pl.BlockSpec((tm, LANES), lambda i: (i, 0)),
                  full((1, 512)), full((512, 1024)), full((1, 256)), full((256, 1024)), full((1, 256)), full((1, 256))],
        out_specs=[pl.BlockSpec((4, tm, 256), lambda i: (0, i, 0)), pl.BlockSpec((4, tm, 256), lambda i: (0, i, 0)),
                   pl.BlockSpec((4, tm, LANES), lambda i: (0, i, 0))],
        out_shape=[jax.ShapeDtypeStruct((4, s, 256), BF16), jax.ShapeDtypeStruct((4, s, 256), BF16),
                   jax.ShapeDtypeStruct((4, s, LANES), BF16)],
        compiler_params=_params(("parallel",)),
    )(z, z, z, cos_m, sin_m, gq, wq, gkv, wkv, qg, kg)


def _mla_prep_bwd(z, cos_m, sin_m, gq, wq, gkv, wkv, qg, kg, dq, dk, dv, tm):
    s = z.shape[0]

    def body(mq_ref, mkv_ref, mkr_ref, cos_ref, sin_ref, gq_ref, wq_ref, gkv_ref, wkv_ref, qg_ref, kg_ref,
             dq_ref, dk_ref, dv_ref,
             dmq_ref, dmkv_ref, dmkr_ref, dwq_ref, dwkv_ref, dgq_ref, dgkv_ref, dqg_ref, dkg_ref, dqf, dkv):
        @pl.when(pl.program_id(0) == 0)
        def _():
            for r in (dwq_ref, dwkv_ref, dgq_ref, dgkv_ref, dqg_ref, dkg_ref):
                r[...] = jnp.zeros_like(r)

        qn, rq0, qnb, kvn, rk0, kvnb, qf, kv = _mla_latents(mq_ref, mkv_ref, gq_ref, gkv_ref, wq_ref, wkv_ref)
        kpe = mkr_ref[...]
        cos, sin = cos_ref[...], sin_ref[...]
        qg, kg = qg_ref[...], kg_ref[...]
        dkpe = jnp.zeros_like(kpe)
        inv = 1.0 / MLA_QK

        def norm_bwd(a, b, r, da_n, db_n, g):
            ga, gb = g[:, 0:LANES], g[:, LANES:]
            dg_a = jnp.sum(da_n * a * r, axis=0, keepdims=True)
            dg_b = jnp.sum(db_n * b * r, axis=0, keepdims=True)
            ua, ub = da_n * ga, db_n * gb
            dt = (jnp.sum(ua * a, axis=-1, keepdims=True) + jnp.sum(ub * b, axis=-1, keepdims=True)) * inv
            r3 = r * r * r
            return r * ua - a * (r3 * dt), r * ub - b * (r3 * dt), dg_a, dg_b

        for h, (qa, qb, rq, ka, rk) in enumerate(_mla_heads(qf, kv, kpe, qg, kg, cos, sin)):
            dqa, dqb, dga, dgb = norm_bwd(qa, qb, rq, dq_ref[h, :, 0:LANES] * ATTN_SCALE,
                                          _rope_t(dq_ref[h, :, LANES:] * ATTN_SCALE, cos, sin), qg)
            dqf[:, LANES * h:LANES * (h + 1)] = dqa
            dqf[:, 512 + LANES * h:512 + LANES * (h + 1)] = dqb
            dqg_ref[:, 0:LANES] += dga
            dqg_ref[:, LANES:] += dgb
            ln2 = math.log(2.0)
            dka, dkb, dga, dgb = norm_bwd(ka, kpe, rk, dk_ref[h, :, 0:LANES] * ln2,
                                          _rope_t(dk_ref[h, :, LANES:] * ln2, cos, sin), kg)
            dkv[:, 256 * h:256 * h + LANES] = dka
            dkv[:, 256 * h + LANES:256 * (h + 1)] = dv_ref[h]
            dkpe = dkpe + dkb
            dkg_ref[:, 0:LANES] += dga
            dkg_ref[:, LANES:] += dgb
        dmkr_ref[...] = dkpe.astype(BF16)

        def latent_bwd(dfull, w_ref, nb, n, r, g_ref, dw_ref, dg_ref, dlat_ref):
            db = dfull.astype(BF16)
            dn = _dot(db, w_ref[...], NT)
            dw_ref[...] += _dot(nb, db, TN)
            dg_ref[...] += jnp.sum(dn * n, axis=0, keepdims=True)
            u = dn * g_ref[...]
            dlat_ref[...] = (r * (u - n * jnp.mean(u * n, axis=-1, keepdims=True))).astype(BF16)

        latent_bwd(dqf[...], wq_ref, qnb, qn, rq0, gq_ref, dwq_ref, dgq_ref, dmq_ref)
        latent_bwd(dkv[...], wkv_ref, kvnb, kvn, rk0, gkv_ref, dwkv_ref, dgkv_ref, dmkv_ref)

    full = lambda shape: pl.BlockSpec(shape, lambda i: (0,) * len(shape))
    return pl.pallas_call(
        body, name="mla_prep_bwd", grid=(s // tm,),
        in_specs=[pl.BlockSpec((tm, 512), lambda i: (i, M_Q // 4)),
                  pl.BlockSpec((tm, 256), lambda i: (i, M_KV // 2)),
                  pl.BlockSpec((tm, LANES), lambda i: (i, M_KR)),
                  pl.BlockSpec((tm, LANES), lambda i: (i, 0)),
                  pl.BlockSpec((tm, LANES), lambda i: (i, 0)),
                  full((1, 512)), full((512, 1024)), full((1, 256)), full((256, 1024)), full((1, 256)), full((1, 256)),
                  pl.BlockSpec((4, tm, 256), lambda i: (0, i, 0)), pl.BlockSpec((4, tm, 256), lambda i: (0, i, 0)),
                  pl.BlockSpec((4, tm, LANES), lambda i: (0, i, 0))],
        out_specs=[pl.BlockSpec((tm, 512), lambda i: (i, 0)), pl.BlockSpec((tm, 256), lambda i: (i, 0)),
                   pl.BlockSpec((tm, LANES), lambda i: (i, 0)),
                   full((512, 1024)), full((256, 1024)), full((1, 512)), full((1, 256)), full((1, 256)), full((1, 256))],
        out_shape=[jax.ShapeDtypeStruct((s, 512), BF16), jax.ShapeDtypeStruct((s, 256), BF16),
                   jax.ShapeDtypeStruct((s, LANES), BF16),
                   jax.ShapeDtypeStruct((512, 1024), F32), jax.ShapeDtypeStruct((256, 1024), F32),
                   jax.ShapeDtypeStruct((1, 512), F32), jax.ShapeDtypeStruct((1, 256), F32),
                   jax.ShapeDtypeStruct((1, 256), F32), jax.ShapeDtypeStruct((1, 256), F32)],
        scratch_shapes=[pltpu.VMEM((tm, 1024), F32), pltpu.VMEM((tm, 1024), F32)],
        compiler_params=_params(("arbitrary",)),
    )(z, z, z, cos_m, sin_m, gq, wq, gkv, wkv, qg, kg, dq, dk, dv)


def _attn_fwd(q, k, v, z, tq, rider=None):
    s = q.shape[1]

    def body(q_ref, k_ref, v_ref, g_ref, o_ref, y_ref, lse_ref):
        sc = _dot(q_ref[...], k_ref[...], NT)
        m = jnp.max(sc, axis=-1, keepdims=True)
        p = jnp.exp2(sc - m)
        l = jnp.sum(p, axis=-1, keepdims=True)
        o = _dot(p.astype(BF16), v_ref[...]) / l
        o_ref[...] = o
        y_ref[...] = (_silu(g_ref[...]) * o).astype(BF16)
        lse_ref[...] = m + jnp.log2(l)

    return _call(
        body, "attn_fwd", (4, s // tq),
        [pl.BlockSpec((None, tq, 256), lambda h, i: (h, i, 0)),
         pl.BlockSpec((None, s, 256), lambda h, i: (h, 0, 0)),
         pl.BlockSpec((None, s, LANES), lambda h, i: (h, 0, 0)),
         pl.BlockSpec((tq, LANES), lambda h, i: (i, M_G + h))],
        [pl.BlockSpec((tq, LANES), lambda h, i: (i, h)), pl.BlockSpec((tq, LANES), lambda h, i: (i, h)),
         pl.BlockSpec((None, tq, 1), lambda h, i: (h, i, 0))],
        [jax.ShapeDtypeStruct((s, GROUP_W), F32), jax.ShapeDtypeStruct((s, GROUP_W), BF16),
         jax.ShapeDtypeStruct((4, s, 1), F32)],
        [], ("parallel", "parallel"), (q, k, v, z), rider)


def _attn_bwd(q, k, v, z, o, lse, dy, tq, rider=None):
    s = q.shape[1]

    def body(q_ref, k_ref, v_ref, g_ref, o_ref, lse_ref, dy_ref, dq_ref, dk_ref, dv_ref, dg_ref):
        @pl.when(pl.program_id(1) == 0)
        def _():
            dk_ref[...] = jnp.zeros_like(dk_ref)
            dv_ref[...] = jnp.zeros_like(dv_ref)

        gate, ov, dyv = g_ref[...], o_ref[...], dy_ref[...]
        do = dyv * _silu(gate)
        dg_ref[...] = (dyv * ov * _silu_grad(gate)).astype(BF16)
        delta = jnp.sum(do * ov, axis=-1, keepdims=True)
        dob = do.astype(BF16)
        qb, kb = q_ref[...], k_ref[...]
        p = jnp.exp2(_dot(qb, kb, NT) - lse_ref[...])
        dp = _dot(dob, v_ref[...], NT)
        ds = (p * (dp - delta)).astype(BF16)
        dq_ref[...] = _dot(ds, kb)
        dk_ref[...] += _dot(ds, qb, TN)
        dv_ref[...] += _dot(p.astype(BF16), dob, TN)

    return _call(
        body, "attn_bwd", (4, s // tq),
        [pl.BlockSpec((None, tq, 256), lambda h, i: (h, i, 0)),
         pl.BlockSpec((None, s, 256), lambda h, i: (h, 0, 0)),
         pl.BlockSpec((None, s, LANES), lambda h, i: (h, 0, 0)),
         pl.BlockSpec((tq, LANES), lambda h, i: (i, M_G + h)),
         pl.BlockSpec((tq, LANES), lambda h, i: (i, h)),
         pl.BlockSpec((None, tq, 1), lambda h, i: (h, i, 0)),
         pl.BlockSpec((tq, LANES), lambda h, i: (i, 12 + h))],
        [pl.BlockSpec((None, tq, 256), lambda h, i: (h, i, 0)),
         pl.BlockSpec((None, s, 256), lambda h, i: (h, 0, 0)),
         pl.BlockSpec((None, s, LANES), lambda h, i: (h, 0, 0)),
         pl.BlockSpec((tq, LANES), lambda h, i: (i, h))],
        [jax.ShapeDtypeStruct((4, s, 256), F32), jax.ShapeDtypeStruct((4, s, 256), F32),
         jax.ShapeDtypeStruct((4, s, LANES), F32), jax.ShapeDtypeStruct((s, GROUP_W), BF16)],
        [], ("parallel", "arbitrary"), (q, k, v, z, o, lse, dy), rider)


def _adam(parts, w, m, v, name, tr):
    r, c = w.shape
    tr = min(tr, r)
    c1 = 1.0 - ADAM_B1 ** ADAM_STEP
    c2 = 1.0 - ADAM_B2 ** ADAM_STEP

    def body(p_ref, w_ref, m_ref, v_ref, g_ref, d_ref, nm_ref, nv_ref):
        g = p_ref[0].astype(F32)
        for i in range(1, N_DEV):
            g = g + p_ref[i].astype(F32)
        nm = ADAM_B1 * m_ref[...] + (1.0 - ADAM_B1) * g
        nv = ADAM_B2 * v_ref[...] + (1.0 - ADAM_B2) * (g * g)
        g_ref[...] = g
        nm_ref[...] = nm
        nv_ref[...] = nv
        d_ref[...] = -ADAM_LR * ((nm / c1) / (jnp.sqrt(nv / c2) + ADAM_EPS) + ADAM_WD * w_ref[...])

    blk = lambda: pl.BlockSpec((tr, c), lambda i: (i, 0))
    return pl.pallas_call(
        body, name=name, grid=(r // tr,),
        in_specs=[pl.BlockSpec((N_DEV, tr, c), lambda i: (0, i, 0)), blk(), blk(), blk()],
        out_specs=[blk(), blk(), blk(), blk()],
        out_shape=[jax.ShapeDtypeStruct((r, c), F32)] * 4,
        compiler_params=_params(("parallel",)),
    )(parts, w, m, v)


def _adam_columns(parts, w, m, v, name, tc, rider=None):
    nl, r, c = w.shape
    pieces = [p for layer in parts for p in layer]
    nh = len(parts[0])
    rp = pieces[0].shape[2]
    tc = min(tc, rp)
    ncb = rp // tc
    c1 = 1.0 - ADAM_B1 ** ADAM_STEP
    c2 = 1.0 - ADAM_B2 ** ADAM_STEP

    def body(*refs):
        p_refs, (w_ref, m_ref, v_ref, g_ref, d_ref, nm_ref, nv_ref) = refs[:len(pieces)], refs[len(pieces):]
        for h in range(nh):
            @pl.when(pl.program_id(0) == h)
            def _(h=h):
                for l in range(nl):
                    p_ref = p_refs[l * nh + h]
                    g = p_ref[0].astype(F32)
                    for i in range(1, N_DEV):
                        g = g + p_ref[i].astype(F32)
                    nm = ADAM_B1 * m_ref[:, l, :] + (1.0 - ADAM_B1) * g
                    nv = ADAM_B2 * v_ref[:, l, :] + (1.0 - ADAM_B2) * (g * g)
                    g_ref[:, l, :] = g
                    nm_ref[:, l, :] = nm
                    nv_ref[:, l, :] = nv
                    d_ref[:, l, :] = -ADAM_LR * ((nm / c1) / (jnp.sqrt(nv / c2) + ADAM_EPS) + ADAM_WD * w_ref[:, l, :])

    def part_spec(j):
        return pl.BlockSpec((N_DEV, c, tc), lambda h, i: (0, 0, jnp.clip((h - j % nh) * ncb + i, 0, ncb - 1)))

    blk = lambda: pl.BlockSpec((c, nl, tc), lambda h, i: (0, 0, h * ncb + i))
    t = lambda a: jnp.transpose(a, (2, 0, 1))
    *res, = _call(body, name, (nh, ncb), [part_spec(j) for j in range(len(pieces))] + [blk(), blk(), blk()],
                  [blk(), blk(), blk(), blk()], [jax.ShapeDtypeStruct((c, nl, r), F32)] * 4, [],
                  ("arbitrary",) * 2, (*pieces, t(w), t(m), t(v)), rider)
    return [jnp.transpose(a, (1, 2, 0)) for a in res[:4]] + res[4:]


def _adam_layers(parts, w, m, v, name, tr, rider=None):
    nl, r, c = w.shape
    pieces = [p for layer in parts for p in layer]
    rp = pieces[0].shape[1]
    tr = min(tr, rp)
    nr, nrp = r // tr, rp // tr
    c1 = 1.0 - ADAM_B1 ** ADAM_STEP
    c2 = 1.0 - ADAM_B2 ** ADAM_STEP

    def body(*refs):
        p_refs, (w_ref, m_ref, v_ref, g_ref, d_ref, nm_ref, nv_ref) = refs[:len(pieces)], refs[len(pieces):]
        at = pl.program_id(0) * nr + pl.program_id(1)
        for j in range(len(pieces)):
            @pl.when(jnp.logical_and(at >= j * nrp, at < (j + 1) * nrp))
            def _(p_ref=p_refs[j]):
                g = p_ref[0].astype(F32)
                for i in range(1, N_DEV):
                    g = g + p_ref[i].astype(F32)
                nm = ADAM_B1 * m_ref[...] + (1.0 - ADAM_B1) * g
                nv = ADAM_B2 * v_ref[...] + (1.0 - ADAM_B2) * (g * g)
                g_ref[...] = g
                nm_ref[...] = nm
                nv_ref[...] = nv
                d_ref[...] = -ADAM_LR * ((nm / c1) / (jnp.sqrt(nv / c2) + ADAM_EPS) + ADAM_WD * w_ref[...])

    def part_spec(j):
        return pl.BlockSpec((N_DEV, tr, c), lambda ll, i: (0, jnp.clip(ll * nr + i - j * nrp, 0, nrp - 1), 0))

    blk = lambda: pl.BlockSpec((None, tr, c), lambda ll, i: (ll, i, 0))
    return _call(body, name, (nl, nr), [part_spec(j) for j in range(len(pieces))] + [blk(), blk(), blk()],
                 [blk(), blk(), blk(), blk()], [jax.ShapeDtypeStruct((nl, r, c), F32)] * 4, [],
                 ("arbitrary", "arbitrary"), (*pieces, w, m, v), rider)


REPLICATED = ("norm_g", "ret_norm_g", "gla_ba_f", "gla_ba_b", "gla_norm_g", "pool_w", "pool_scale",
              "mla_q_norm_g", "mla_kv_norm_g", "mla_qk_norm_q", "mla_qk_norm_k")
REPLICATED_EARLY = REPLICATED[1:]
SMALL_SHARDED = ("mla_wq_b", "mla_wkv_b", "gla_wa2_f", "gla_wa2_b")
WEIGHTS = ("norm_g", "w_in", "ret_norm_g", "gla_wa2_f", "gla_ba_f", "gla_wa2_b", "gla_ba_b", "gla_norm_g", "pool_w",
           "pool_scale", "mla_q_norm_g", "mla_wq_b", "mla_kv_norm_g", "mla_wkv_b", "mla_qk_norm_q", "mla_qk_norm_k",
           "w_out")


def _pack(arrays, dtype):
    flat = jnp.concatenate([a.reshape(-1) for a in arrays]).astype(dtype)
    return flat.reshape(-1, LANES)


def _unpack(packed, like):
    flat = packed.reshape(-1)
    out, at = [], 0
    for a in like:
        out.append(flat[at:at + a.size].reshape(a.shape))
        at += a.size
    return out


def _columns_by_device(g):
    l, r, n = g.shape
    return g.reshape(l, r, N_DEV, n // N_DEV).transpose(2, 0, 1, 3)


def _gathered_columns(g, l, r, c):
    return g.reshape(N_DEV, l, r, c).transpose(1, 2, 0, 3).reshape(l, r, N_DEV * c)


def _layer_forward(x, wts, late_wts, tables, tm, tq, ride_inproj=None, ride_attn=None, target=None):
    cos_r, sin_r, cos_m, sin_m, tab, _ = tables
    z, h, *carried_in = _inproj(x, wts["norm_g"], wts["w_in"], min(x.shape[0], 2 * tm), rider=ride_inproj)
    wts.update(late_wts(carried_in))
    o_a, y_a = _ret_fwd(z, cos_r, sin_r, tab, wts["ret_norm_g"])
    o_b, y_b = _gla_fwd(z, wts["wa_f"], wts["wa_b"], wts["gla_ba_f"], wts["gla_ba_b"], wts["gla_norm_g"])
    y_c = _pool_fwd(z, wts["pool_w"], wts["pool_scale"])
    q, k, v = _mla_prep(z, cos_m, sin_m, wts["mla_q_norm_g"], wts["wq"], wts["mla_kv_norm_g"], wts["wkv"],
                        wts["qk_q"], wts["qk_k"], tm)
    o_d, y_d, lse, *carried_attn = _attn_fwd(q, k, v, z, tq, rider=ride_attn)
    y = jnp.concatenate([y_a, y_b, y_c, y_d], axis=1)
    w_out = wts["w_out"]
    if target is None:
        x_next = _mm(y, w_out, "nn", "outproj", tm, D_MODEL, 1024, add=x)
    else:
        x_next = _mm(y, w_out, "nn", "outproj_loss", tm, D_MODEL, 1024, tail=_loss_tail(x, target))
    saved = dict(x=x, z=z, h=h, o_a=o_a, o_b=o_b, o_d=o_d, lse=lse, q=q, k=k, v=v, y=y, w_out=w_out)
    return x_next, saved, carried_in, carried_attn


def _layer_backward(dx, sv, wts, tables, tm, tq, rides):
    cos_r, sin_r, cos_m, sin_m, tab, tab_sw = tables
    z = sv["z"]
    g = {}
    carried = {}

    def rider(name):
        return rides[name](g) if name in rides else None

    def landed(name, results, n_own):
        if name in rides:
            carried[name] = list(results[n_own:])
        return results[:n_own]

    g["w_out"] = _mm(sv["y"], dx, "tn", "d_w_out", 2048, 1024, 1024, out_dtype=BF16)
    dy = _mm(dx, sv["w_out"], "nt", "d_y", tm, 2048, 1024)

    do_a, dg_a, g["ret_norm_g"] = _normgate_bwd(sv["o_a"], z, A_G, dy, 0, wts["ret_norm_g"], tm)
    dq_a, dk_a, dv_a = landed("ret", _ret_bwd(z, do_a, cos_r, sin_r, tab, tab_sw, rider=rider("ret")), 3)

    do_b, dg_b, g["gla_norm_g"] = _normgate_bwd(sv["o_b"], z, B_G, dy, 1, wts["gla_norm_g"], tm)
    dq_b, dk_b, dv_b, d_ga, d_waf, d_wab, g["gla_ba_f"], g["gla_ba_b"] = landed("gla", _gla_bwd(
        z, do_b, wts["wa_f"], wts["wa_b"], wts["gla_ba_f"], wts["gla_ba_b"], rider=rider("gla")), 8)
    g["gla_wa2_f"] = d_waf[0:GLA_RANK]
    g["gla_wa2_b"] = d_wab[GLA_RANK:2 * GLA_RANK]

    du_c, dg_c, g["pool_w"], g["pool_scale"] = _pool_bwd(z, dy, wts["pool_w"], wts["pool_scale"])

    d_q, d_k, d_v, dg_d = landed("attn", _attn_bwd(sv["q"], sv["k"], sv["v"], z, sv["o_d"], sv["lse"], dy, tq,
                                                   rider=rider("attn")), 4)
    (d_mq, d_mkv, d_mkr, d_wq, g["mla_wkv_b"], g["mla_q_norm_g"], g["mla_kv_norm_g"], d_qg, d_kg) = _mla_prep_bwd(
        z, cos_m, sin_m, wts["mla_q_norm_g"], wts["wq"], wts["mla_kv_norm_g"], wts["wkv"], wts["qk_q"], wts["qk_k"],
        d_q, d_k, d_v, tm)
    g["mla_wq_b"] = _unpad_wq(d_wq)
    g["mla_qk_norm_q"] = d_qg[:, _QK_INV]
    g["mla_qk_norm_k"] = d_kg[:, _QK_INV]

    dz = jnp.concatenate([dq_a, dk_a, dv_a, dg_a, dq_b, dk_b, dv_b, dg_b, d_mq, du_c, dg_c, dg_d, d_mkv,
                          d_ga.astype(BF16), d_mkr], axis=1)
    h, half = sv["h"], D_MODEL // 2
    for name, cols in (("d_w_in_a", h[:, :half]), ("d_w_in_b", h[:, half:])):
        res = _mm(dz, cols, "tn", name, 2048, 1024, 1024, out_dtype=BF16, rider=rider(name))
        (d_wt,) = landed(name, res if name in rides else [res], 1)
        g["w_in" + name[-2:]] = _split_w_in(d_wt)
    dx_in, g["norm_g"] = landed("d_h", _mm(dz, wts["w_in"], "nn", "d_h", tm, D_MODEL, 1024, rider=rider("d_h"),
                                           tail=_norm_bwd_tail(sv["x"], wts["norm_g"], dx)), 2)
    return dx_in, g, carried


def kernel(x, norm_g, w_in, ret_norm_g, gla_wa2_f, gla_ba_f, gla_wa2_b, gla_ba_b, gla_norm_g, pool_w, pool_scale, mla_q_norm_g, mla_wq_b, mla_kv_norm_g, mla_wkv_b, mla_qk_norm_q, mla_qk_norm_k, w_out, loss_target, m_norm_g, m_w_in, m_ret_norm_g, m_gla_wa2_f, m_gla_ba_f, m_gla_wa2_b, m_gla_ba_b, m_gla_norm_g, m_pool_w, m_pool_scale, m_mla_q_norm_g, m_mla_wq_b, m_mla_kv_norm_g, m_mla_wkv_b, m_mla_qk_norm_q, m_mla_qk_norm_k, m_w_out, v_norm_g, v_w_in, v_ret_norm_g, v_gla_wa2_f, v_gla_ba_f, v_gla_wa2_b, v_gla_ba_b, v_gla_norm_g, v_pool_w, v_pool_scale, v_mla_q_norm_g, v_mla_wq_b, v_mla_kv_norm_g, v_mla_wkv_b, v_mla_qk_norm_q, v_mla_qk_norm_k, v_w_out):
    w = dict(norm_g=norm_g, w_in=w_in, ret_norm_g=ret_norm_g, gla_wa2_f=gla_wa2_f, gla_ba_f=gla_ba_f,
             gla_wa2_b=gla_wa2_b, gla_ba_b=gla_ba_b, gla_norm_g=gla_norm_g, pool_w=pool_w, pool_scale=pool_scale,
             mla_q_norm_g=mla_q_norm_g, mla_wq_b=mla_wq_b, mla_kv_norm_g=mla_kv_norm_g, mla_wkv_b=mla_wkv_b,
             mla_qk_norm_q=mla_qk_norm_q, mla_qk_norm_k=mla_qk_norm_k, w_out=w_out)
    m = dict(norm_g=m_norm_g, w_in=m_w_in, ret_norm_g=m_ret_norm_g, gla_wa2_f=m_gla_wa2_f, gla_ba_f=m_gla_ba_f,
             gla_wa2_b=m_gla_wa2_b, gla_ba_b=m_gla_ba_b, gla_norm_g=m_gla_norm_g, pool_w=m_pool_w,
             pool_scale=m_pool_scale, mla_q_norm_g=m_mla_q_norm_g, mla_wq_b=m_mla_wq_b, mla_kv_norm_g=m_mla_kv_norm_g,
             mla_wkv_b=m_mla_wkv_b, mla_qk_norm_q=m_mla_qk_norm_q, mla_qk_norm_k=m_mla_qk_norm_k, w_out=m_w_out)
    v = dict(norm_g=v_norm_g, w_in=v_w_in, ret_norm_g=v_ret_norm_g, gla_wa2_f=v_gla_wa2_f, gla_ba_f=v_gla_ba_f,
             gla_wa2_b=v_gla_wa2_b, gla_ba_b=v_gla_ba_b, gla_norm_g=v_gla_norm_g, pool_w=v_pool_w,
             pool_scale=v_pool_scale, mla_q_norm_g=v_mla_q_norm_g, mla_wq_b=v_mla_wq_b, mla_kv_norm_g=v_mla_kv_norm_g,
             mla_wkv_b=v_mla_wkv_b, mla_qk_norm_q=v_mla_qk_norm_q, mla_qk_norm_k=v_mla_qk_norm_k, w_out=v_w_out)
    xs, target = x[0], loss_target[0]
    s = xs.shape[0]
    tm, tq = min(s, 512), min(s, 256)
    c_in = w_in.shape[2]

    w_in_b = jnp.transpose(w_in, (2, 0, 1)).astype(BF16)
    w_out_b = w_out.astype(BF16).reshape(-1, D_MODEL)
    (w_in_g0,) = _exchange([("gather", w_in_b[:, 0])], "gather_first")
    tables = _rope_tables(s) + _ret_tables()
    offs = np.cumsum([0] + [w[n].size for n in SMALL_SHARDED])

    def early_weights(l, w_in_g):
        return dict(
            norm_g=norm_g[l][None], w_in=_assemble_w_in(w_in_g), ret_norm_g=ret_norm_g[l][None],
            gla_ba_f=gla_ba_f[l][None], gla_ba_b=gla_ba_b[l][None],
            gla_norm_g=gla_norm_g[l][None], pool_w=pool_w[l], pool_scale=pool_scale[l][None],
            mla_q_norm_g=mla_q_norm_g[l][None], mla_kv_norm_g=mla_kv_norm_g[l][None],
            qk_q=_pad_qk_gain(mla_qk_norm_q[l]), qk_k=_pad_qk_gain(mla_qk_norm_k[l]))

    def late_weights(l, w_out_g, small_g):
        flat = small_g.reshape(N_DEV, -1)
        full = {n: _gathered_columns(flat[:, offs[i]:offs[i + 1]], *w[n].shape)[l]
                for i, n in enumerate(SMALL_SHARDED)}
        wa_f = jnp.zeros((LANES, 2 * LANES), BF16).at[0:GLA_RANK].set(full["gla_wa2_f"])
        wa_b = jnp.zeros((LANES, 2 * LANES), BF16).at[GLA_RANK:2 * GLA_RANK].set(full["gla_wa2_b"])
        return dict(w_out=w_out_g.reshape(N_DEV, DEPTH, -1, D_MODEL)[:, l].reshape(-1, D_MODEL),
                    wa_f=wa_f, wa_b=wa_b, wq=_pad_wq(full["mla_wq_b"]), wkv=full["mla_wkv_b"])

    by_owner = lambda g_w_out: g_w_out.reshape(N_DEV, -1, D_MODEL)

    layers = [early_weights(0, w_in_g0), None]
    x1, sv0, (w_out_g, small_g), (w_in_g1,) = _layer_forward(
        xs, layers[0], lambda got: late_weights(0, *got), tables, tm, tq,
        ride_inproj=[("gather", w_out_b), ("gather", _pack([w[n] for n in SMALL_SHARDED], BF16))],
        ride_attn=("gather", w_in_b[:, 1]))
    layers[1] = early_weights(1, w_in_g1)
    (dx, loss_row), sv1, _, _ = _layer_forward(x1, layers[1], lambda got: late_weights(1, w_out_g, small_g), tables,
                                               tm, tq, target=target)
    loss = lax.psum(loss_row[0, 0], ("x", "y", "c"))

    def small_jobs(g):
        grads = (g, g1)
        full = {n: jnp.stack([grads[l][n].reshape(w[n].shape[1:]) if n in REPLICATED else grads[l][n]
                              for l in range(DEPTH)]) for n in SMALL_SHARDED + REPLICATED_EARLY}
        small_c = jnp.concatenate([_columns_by_device(full[n]).reshape(N_DEV, -1) for n in SMALL_SHARDED], axis=1)
        return [("scatter", small_c.reshape(N_DEV, -1, LANES)),
                ("gather", _pack([full[n] for n in REPLICATED_EARLY], F32))]

    dx, g1, got1 = _layer_backward(dx, sv1, layers[1], tables, tm, tq, {
        "attn": lambda g: ("scatter", by_owner(g["w_out"])),
        "d_h": lambda g: ("scatter", g["w_in_a"])})
    dx, g0, got0 = _layer_backward(dx, sv0, layers[0], tables, tm, tq, {
        "ret": lambda g: ("scatter", by_owner(g["w_out"])),
        "gla": lambda g: ("scatter", g1["w_in_b"]),
        "d_w_in_a": small_jobs,
        "d_w_in_b": lambda g: ("scatter", g["w_in_a"]),
        "d_h": lambda g: ("scatter", g["w_in_b"])})
    in_parts = ((got0["d_w_in_b"][0], got0["d_h"][0]), (got1["d_h"][0], got0["gla"][0]))
    out_parts = ((got0["ret"][0],), (got1["attn"][0],))
    small_parts, rep_parts = got0["d_w_in_a"]
    norm_pack = _pack([jnp.stack([g0["norm_g"][0], g1["norm_g"][0]])], F32)

    out = {}
    out["w_in"] = _adam_columns(in_parts, w_in, m_w_in, v_w_in, "adam_w_in", 256)
    *out["w_out"], norm_parts = _adam_layers(out_parts, w_out, m_w_out, v_w_out, "adam_w_out", 128,
                                             rider=("gather", norm_pack))
    for names, parts, label in ((SMALL_SHARDED, small_parts, "adam_small"),
                                (REPLICATED_EARLY, rep_parts, "adam_replicated"), (("norm_g",), norm_parts, "adam_norm")):
        res = _adam(parts, _pack([w[n] for n in names], F32), _pack([m[n] for n in names], F32),
                    _pack([v[n] for n in names], F32), label, 2048)
        for n, *vals in zip(names, *[_unpack(a, [w[n] for n in names]) for a in res]):
            out[n] = vals

    return (loss, dx[None], *[out[n][0] for n in WEIGHTS], *[out[n][1] for n in WEIGHTS],
            *[out[n][2] for n in WEIGHTS], *[out[n][3] for n in WEIGHTS])
```

```python
import functools
import math

import numpy as np
import jax
import jax.numpy as jnp
from jax import lax
from jax.experimental import pallas as pl
from jax.experimental.pallas import tpu as pltpu

F32 = jnp.float32
BF16 = jnp.bfloat16

N_DEV = 8
D_MODEL = 2048
DEPTH = 2
GROUP_W = 512
EPS = 1e-6
ROPE_THETA = 10000.0
LANES = 128

RET_HD = 128
RET_CHUNK = 256
RET_UNROLL = 2
GLA_CHUNK = 64
GLA_UNROLL = 4
GLA_CUM_ROWS = 256
GLA_DK = 64
GLA_TAU = 16.0
GLA_RANK = 16
POOL_WINDOWS = (2, 4, 8, 16)
MLA_QK = 192
MLA_ROPE = 64
ATTN_SCALE = MLA_QK ** -0.5
ATTN_Q_SCALE = ATTN_SCALE * math.log2(math.e)
IN_COLS = 5984

ADAM_LR = 0.001
ADAM_B1 = 0.9
ADAM_B2 = 0.999
ADAM_EPS = 1e-08
ADAM_WD = 0.01
ADAM_STEP = 10

A_Q, A_K, A_V, A_G = 0, 4, 8, 12
B_Q, B_K, B_V, B_G = 16, 18, 20, 24
M_Q, C_V, C_G, M_G = 28, 32, 36, 40
M_KV, GA, M_KR = 44, 46, 47
ZP_COLS = 48 * LANES

VMEM_LIMIT = 56 * 1024 * 1024


def _params(sem, vmem=VMEM_LIMIT):
    return pltpu.CompilerParams(dimension_semantics=sem, vmem_limit_bytes=vmem)


def _sigmoid(x):
    return 1.0 / (1.0 + jnp.exp(-x))


def _silu(x):
    return x * _sigmoid(x)


def _silu_grad(x):
    s = _sigmoid(x)
    return s * (1.0 + x * (1.0 - s))


def _dot(a, b, dims=(((1,), (0,)), ((), ()))):
    return lax.dot_general(a, b, dims, preferred_element_type=F32)


NT = (((1,), (1,)), ((), ()))
TN = (((0,), (0,)), ((), ()))


def _chunk_loop(n, body, init, unroll):
    unroll = math.gcd(n, unroll)

    def trip(t, carry):
        for u in range(unroll):
            carry = body(t * unroll + u, carry)
        return carry

    return lax.fori_loop(0, n // unroll, trip, init)


def _roll_lanes_half(x):
    return pltpu.roll(x, 64, 1)


def _wq_perm():
    idx = np.zeros((1024,), np.int32)
    ok = np.zeros((1024,), bool)
    for h in range(4):
        idx[128 * h:128 * h + 128] = 192 * h + np.arange(128)
        ok[128 * h:128 * h + 128] = True
        base = 512 + 128 * h
        idx[base:base + 32] = 192 * h + 128 + np.arange(32)
        ok[base:base + 32] = True
        idx[base + 64:base + 96] = 192 * h + 160 + np.arange(32)
        ok[base + 64:base + 96] = True
    inv = np.zeros((768,), np.int32)
    inv[idx[ok]] = np.nonzero(ok)[0]
    return idx, ok, inv


_WQ_IDX, _WQ_OK, _WQ_INV = _wq_perm()


def _pad_wq(wq):
    return jnp.where(jnp.asarray(_WQ_OK)[None, :], wq[:, _WQ_IDX], 0).astype(wq.dtype)


def _unpad_wq(wqp):
    return wqp[:, _WQ_INV]


def _qk_idx():
    idx = np.zeros((256,), np.int32)
    ok = np.zeros((256,), bool)
    idx[0:128] = np.arange(128)
    ok[0:128] = True
    idx[128:160] = 128 + np.arange(32)
    ok[128:160] = True
    idx[192:224] = 160 + np.arange(32)
    ok[192:224] = True
    inv = np.zeros((192,), np.int32)
    inv[idx[ok]] = np.nonzero(ok)[0]
    return idx, ok, inv


_QK_IDX, _QK_OK, _QK_INV = _qk_idx()


def _pad_qk_gain(g):
    return jnp.where(jnp.asarray(_QK_OK), g[_QK_IDX], 0.0).reshape(1, 256)


def _rope_tables(s):
    def tabs(dim):
        inv = 1.0 / (ROPE_THETA ** (jnp.arange(0, dim, 2, dtype=F32) / dim))
        ang = jnp.arange(s, dtype=F32)[:, None] * inv[None, :]
        return jnp.cos(ang), jnp.sin(ang)
    cr, sr = tabs(RET_HD)
    cos_r = jnp.concatenate([cr, cr], axis=1)
    sin_r = jnp.concatenate([-sr, sr], axis=1)
    cm, sm = tabs(MLA_ROPE)
    zz = jnp.zeros_like(cm)
    cos_m = jnp.concatenate([cm, zz, cm, zz], axis=1)
    sin_m = jnp.concatenate([-sm, zz, sm, zz], axis=1)
    return cos_r, sin_r, cos_m, sin_m


def _rope(x, cos, sin):
    return x * cos + _roll_lanes_half(x) * sin


def _rope_t(x, cos, sin):
    return x * cos + _roll_lanes_half(x * sin)


def _ret_tables():
    c = RET_CHUNK
    gamma_f = 1.0 - 2.0 ** (-5.0 - jnp.arange(4, dtype=F32))
    gamma_b = gamma_f[::-1]
    idx = jnp.arange(c, dtype=F32)
    diff = idx[:, None] - idx[None, :]

    def build(g1, g2):
        l1 = jnp.log(g1)[:, None, None]
        l2 = jnp.log(g2)[:, None, None]
        d1 = jnp.where(diff >= 0, jnp.exp(jnp.maximum(diff, 0.0)[None] * l1), 0.0)
        d2 = jnp.where(diff <= 0, jnp.exp(jnp.maximum(-diff, 0.0)[None] * l2), 0.0)
        ones = jnp.ones((1, c, LANES), F32)
        col = idx[None, :, None]
        qdf = jnp.exp((col + 1.0) * l1) * ones
        kdf = jnp.exp((c - 1.0 - col) * l1) * ones
        qdb = jnp.exp((c - col) * l2) * ones
        kdb = jnp.exp(col * l2) * ones
        cd1 = jnp.exp(c * l1) * ones
        cd2 = jnp.exp(c * l2) * ones
        return jnp.concatenate([d1 + d2, qdf, kdf, qdb, kdb, cd1, cd2], axis=2)

    return build(gamma_f, gamma_b), build(gamma_b, gamma_f)


MESH = pl.DeviceIdType.MESH
ANY = pl.BlockSpec(memory_space=pl.ANY)
_RELATIONS = ((0, 0, 1), (1, 0, 0), (0, 1, 0), (1, 1, 0), (1, 0, 1), (0, 1, 1), (1, 1, 1))


def _position():
    return lax.axis_index("x"), lax.axis_index("y"), lax.axis_index("c")


def _gather_copies(x_ref, out_ref, send_sems, recv_sems, local_sem, starting):
    x, y, cc = _position()
    me, sibling = (x, y, cc), (x, y, 1 - cc)
    chips = [(1 - x, y), (x, 1 - y), (1 - x, 1 - y)]

    def slab(px, py, pc):
        return out_ref.at[4 * px + 2 * py + pc]

    def copy(k, block, to, src=None):
        return pltpu.make_async_remote_copy(
            src_ref=slab(*block) if src is None else src, dst_ref=slab(*block),
            send_sem=send_sems.at[k], recv_sem=recv_sems.at[k], device_id=to, device_id_type=MESH)

    mine = pltpu.make_async_copy(x_ref, slab(*me), local_sem)
    first = [copy(0, me, sibling, src=x_ref)] + [copy(1 + j, me, (*chip, cc), src=x_ref) for j, chip in enumerate(chips)]
    if starting:
        return mine, first
    passed = [copy(4 + j, (*chip, cc), sibling) for j, chip in enumerate(chips)]
    arrivals = [copy(1 + j, (*chip, cc), me) for j, chip in enumerate(chips)]
    late = [copy(0, sibling, me)] + [copy(4 + j, (*chip, 1 - cc), me) for j, chip in enumerate(chips)]
    return mine, first, passed, arrivals, late


def _gather_start(*refs):
    mine, first = _gather_copies(*refs, starting=True)
    mine.start()
    for cp in first:
        cp.start()


def _gather_finish(*refs):
    mine, first, passed, arrivals, late = _gather_copies(*refs, starting=False)
    for arrived, onward in zip(arrivals, passed):
        arrived.wait_recv()
        onward.start()
    for cp in late:
        cp.wait_recv()
    for cp in first + passed:
        cp.wait_send()
    mine.wait()


def _scatter_copies(c_ref, out_ref, send_sems, recv_sems, local_sem):
    x, y, cc = _position()
    me = 4 * x + 2 * y + cc
    mine = pltpu.make_async_copy(c_ref.at[me], out_ref.at[me], local_sem)
    copies = []
    for k, (fx, fy, fc) in enumerate(_RELATIONS):
        px = 1 - x if fx else x
        py = 1 - y if fy else y
        pc = 1 - cc if fc else cc
        copies.append(pltpu.make_async_remote_copy(
            src_ref=c_ref.at[4 * px + 2 * py + pc], dst_ref=out_ref.at[me],
            send_sem=send_sems.at[k], recv_sem=recv_sems.at[k], device_id=(px, py, pc), device_id_type=MESH))
    return mine, copies


def _scatter_start(*refs):
    mine, copies = _scatter_copies(*refs)
    mine.start()
    for cp in copies:
        cp.start()


def _scatter_finish(*refs):
    mine, copies = _scatter_copies(*refs)
    for cp in copies:
        cp.wait()
    mine.wait()


_EXCHANGES = {"gather": (_gather_start, _gather_finish), "scatter": (_scatter_start, _scatter_finish)}


def _exchange_scratch():
    return [pltpu.SemaphoreType.DMA((7,)), pltpu.SemaphoreType.DMA((7,)), pltpu.SemaphoreType.DMA]


def _exchange_out(kind, src):
    return jax.ShapeDtypeStruct(((N_DEV,) + src.shape) if kind == "gather" else src.shape, src.dtype)


def _exchange(jobs, name):
    n = len(jobs)

    def body(*refs):
        srcs, outs, sems = refs[:n], refs[n:2 * n], refs[2 * n:]
        for half in (0, 1):
            for i, (kind, _) in enumerate(jobs):
                _EXCHANGES[kind][half](srcs[i], outs[i], *sems[3 * i:3 * i + 3])

    return pl.pallas_call(
        body, name=name, out_shape=[_exchange_out(kind, src) for kind, src in jobs],
        in_specs=[ANY] * n, out_specs=[ANY] * n,
        scratch_shapes=[sem for _ in jobs for sem in _exchange_scratch()])(*[src for _, src in jobs])


def _call(body, name, grid, in_specs, out_specs, out_shape, scratch, sem, args, rider=None):
    if rider is None:
        return pl.pallas_call(body, name=name, grid=grid, in_specs=in_specs, out_specs=out_specs, out_shape=out_shape,
                              scratch_shapes=scratch, compiler_params=_params(sem))(*args)
    jobs = rider if isinstance(rider, list) else [rider]
    ni, no, ns, nj = len(in_specs), len(out_specs), len(scratch), len(jobs)

    def carried(*refs):
        ins, rsrcs = refs[:ni], refs[ni:ni + nj]
        outs, routs = refs[ni + nj:ni + nj + no], refs[ni + nj + no:ni + 2 * nj + no]
        scr, sems = refs[ni + 2 * nj + no:ni + 2 * nj + no + ns], refs[ni + 2 * nj + no + ns:]
        ids = [pl.program_id(a) for a in range(len(grid))]
        is_first = functools.reduce(jnp.logical_and, [i == 0 for i in ids])
        is_last = functools.reduce(jnp.logical_and, [i == g - 1 for i, g in zip(ids, grid)])

        def half(which):
            for j, (kind, _) in enumerate(jobs):
                _EXCHANGES[kind][which](rsrcs[j], routs[j], *sems[3 * j:3 * j + 3])

        @pl.when(is_first)
        def _():
            half(0)

        body(*ins, *outs, *scr)

        @pl.when(is_last)
        def _():
            half(1)

    return pl.pallas_call(
        carried, name=name, grid=grid, in_specs=list(in_specs) + [ANY] * nj, out_specs=list(out_specs) + [ANY] * nj,
        out_shape=list(out_shape) + [_exchange_out(kind, src) for kind, src in jobs],
        scratch_shapes=list(scratch) + [sem for _ in jobs for sem in _exchange_scratch()],
        compiler_params=_params(("arbitrary",) * len(grid)))(*args, *[src for _, src in jobs])


def _inproj(x, g, wt, tm, tn=512, rider=None):
    s, d = x.shape
    n = wt.shape[0]

    def body(x_ref, g_ref, w_ref, z_ref, h_ref, hs):
        @pl.when(pl.program_id(1) == 0)
        def _():
            xv = x_ref[...]
            r = lax.rsqrt(jnp.mean(xv * xv, axis=-1, keepdims=True) + EPS)
            hv = (xv * r * g_ref[...]).astype(BF16)
            hs[...] = hv
            h_ref[...] = hv
        z_ref[...] = _dot(hs[...], w_ref[...], NT)

    return _call(
        body, "inproj", (s // tm, n // tn),
        [pl.BlockSpec((tm, d), lambda i, j: (i, 0)),
         pl.BlockSpec((1, d), lambda i, j: (0, 0)),
         pl.BlockSpec((tn, d), lambda i, j: (j, 0))],
        [pl.BlockSpec((tm, tn), lambda i, j: (i, j)), pl.BlockSpec((tm, d), lambda i, j: (i, 0))],
        [jax.ShapeDtypeStruct((s, n), F32), jax.ShapeDtypeStruct((s, d), BF16)],
        [pltpu.VMEM((tm, d), BF16)], ("parallel", "arbitrary"), (x, g, wt), rider)


def _relayout_plan():
    runs = ((0, 3584, 0), (3584, 3616, GA * LANES), (3616, 4640, C_V * LANES), (4640, 5152, M_Q * LANES),
            (5152, 5408, M_KV * LANES), (5408, 5440, M_KR * LANES), (5440, 5472, M_KR * LANES + 64),
            (5472, 5984, M_G * LANES))
    shard = IN_COLS // N_DEV
    plan = []
    for d in range(N_DEV):
        lo, hi = shard * d, shard * (d + 1)
        for a, b, p in runs:
            s, e = max(a, lo), min(b, hi)
            if s < e:
                plan.append((d, s - lo, p + (s - a), e - s))
    return plan


def _assemble_w_in(g, tc=512):
    _, c, r = g.shape
    tc = min(tc, r)

    def body(g_ref, o_ref):
        o_ref[...] = jnp.zeros_like(o_ref)
        for d, at, to, w in _relayout_plan():
            o_ref[to:to + w, :] = g_ref[d, at:at + w, :]

    return pl.pallas_call(
        body, name="assemble_w_in", grid=(r // tc,),
        in_specs=[pl.BlockSpec((N_DEV, c, tc), lambda i: (0, 0, i))],
        out_specs=pl.BlockSpec((ZP_COLS, tc), lambda i: (0, i)),
        out_shape=jax.ShapeDtypeStruct((ZP_COLS, r), g.dtype),
        compiler_params=_params(("parallel",)),
    )(g)


def _split_w_in(wt, tc=512):
    r = wt.shape[1]
    c = IN_COLS // N_DEV
    tc = min(tc, r)

    def body(w_ref, o_ref):
        for d, at, to, w in _relayout_plan():
            o_ref[d, at:at + w, :] = w_ref[to:to + w, :]

    return pl.pallas_call(
        body, name="split_w_in", grid=(r // tc,),
        in_specs=[pl.BlockSpec((ZP_COLS, tc), lambda i: (0, i))],
        out_specs=pl.BlockSpec((N_DEV, c, tc), lambda i: (0, 0, i)),
        out_shape=jax.ShapeDtypeStruct((N_DEV, c, r), wt.dtype),
        compiler_params=_params(("parallel",)),
    )(wt)


def _mm(a, b, mode, name, tm, tn, tk, add=None, out_dtype=F32, rider=None, tail=None):
    if mode == "tn":
        k, m = a.shape
    else:
        m, k = a.shape
    n = b.shape[0] if mode == "nt" else b.shape[1]
    tm, tn, tk = min(tm, m), min(tn, n), min(tk, k)
    nk = k // tk
    dims = {"nn": (((1,), (0,)), ((), ())), "nt": NT, "tn": TN}[mode]
    if tail is None:
        def plain(acc, i, extra_refs, out_refs):
            out_refs[0][...] = (acc + extra_refs[0][...] if extra_refs else acc).astype(out_dtype)
        tail = ([(add, "tile")] if add is not None else [], [(out_dtype, "tile")], plain)
    extra, outs, fn = tail
    spec = {"tile": pl.BlockSpec((tm, tn), lambda i, j, kk: (i, j)),
            "row": pl.BlockSpec((1, tn), lambda i, j, kk: (0, j)),
            "lanes": pl.BlockSpec((1, LANES), lambda i, j, kk: (0, 0))}
    shape = {"tile": (m, n), "row": (1, n), "lanes": (1, LANES)}
    ne, no = len(extra), len(outs)

    def body(*refs):
        a_ref, b_ref = refs[:2]
        extra_refs, out_refs, acc = refs[2:2 + ne], refs[2 + ne:2 + ne + no], refs[2 + ne + no]
        i, kk = pl.program_id(0), pl.program_id(2)

        @pl.when(kk == 0)
        def _():
            acc[...] = jnp.zeros_like(acc)

        acc[...] += _dot(a_ref[...].astype(BF16), b_ref[...].astype(BF16), dims)

        @pl.when(kk == nk - 1)
        def _():
            fn(acc[...], i, extra_refs, out_refs)

    a_spec = (pl.BlockSpec((tk, tm), lambda i, j, kk: (kk, i)) if mode == "tn"
              else pl.BlockSpec((tm, tk), lambda i, j, kk: (i, kk)))
    b_spec = (pl.BlockSpec((tn, tk), lambda i, j, kk: (j, kk)) if mode == "nt"
              else pl.BlockSpec((tk, tn), lambda i, j, kk: (kk, j)))
    summed = any(kind != "tile" for _, kind in outs)
    res = _call(body, name, (m // tm, n // tn, nk), [a_spec, b_spec] + [spec[kind] for _, kind in extra],
                [spec[kind] for _, kind in outs], [jax.ShapeDtypeStruct(shape[kind], dt) for dt, kind in outs],
                [pltpu.VMEM((tm, tn), F32)], ("arbitrary",) * 3 if summed else ("parallel", "parallel", "arbitrary"),
                [a, b] + [arr for arr, _ in extra], rider)
    return res[0] if (rider is None and no == 1) else res


def _norm_bwd_tail(x, g, dres):
    def fn(dh, i, extra_refs, out_refs):
        x_ref, g_ref, dres_ref = extra_refs
        dx_ref, dg_ref = out_refs

        @pl.when(i == 0)
        def _():
            dg_ref[...] = jnp.zeros_like(dg_ref)

        xv = x_ref[...]
        r = lax.rsqrt(jnp.mean(xv * xv, axis=-1, keepdims=True) + EPS)
        nv = xv * r
        dg_ref[...] += jnp.sum(dh * nv, axis=0, keepdims=True)
        u = dh * g_ref[...]
        dx_ref[...] = dres_ref[...] + r * (u - nv * jnp.mean(u * nv, axis=-1, keepdims=True))

    return [(x, "tile"), (g, "row"), (dres, "tile")], [(F32, "tile"), (F32, "row")], fn


def _loss_tail(x, target):
    d = x.shape[1]

    def fn(acc, i, extra_refs, out_refs):
        x_ref, t_ref = extra_refs
        dx_ref, loss_ref = out_refs

        @pl.when(i == 0)
        def _():
            loss_ref[...] = jnp.zeros_like(loss_ref)

        err = acc + x_ref[...] - t_ref[...]
        dx_ref[...] = err * (1.0 / d)
        per_tok = jnp.mean(err * err, axis=-1, keepdims=True)
        loss_ref[...] += 0.5 * jnp.sum(per_tok, axis=0, keepdims=True)

    return [(x, "tile"), (target, "tile")], [(F32, "tile"), (F32, "lanes")], fn


def _ret_core(q_ref, k_ref, v_ref, tab_ref, out_ref, back_ref, nchunk):
    c = RET_CHUNK

    def rows(n):
        return pl.ds(pl.multiple_of(n * c, c), c)

    zero = jnp.zeros((LANES, LANES), F32)

    def plane(i, n=c):
        return tab_ref[0:n, c + LANES * i:c + LANES * (i + 1)]

    def fwd(n, st):
        r = rows(n)
        q, k, vb = q_ref[r, :], k_ref[r, :], v_ref[r, :].astype(BF16)
        sc = _dot(q.astype(BF16), k.astype(BF16), NT) * tab_ref[:, 0:c]
        o = _dot(sc.astype(BF16), vb)
        o = o + _dot((q * plane(0)).astype(BF16), st.astype(BF16))
        out_ref[r, :] = o
        return st * plane(4, LANES) + _dot((k * plane(1)).astype(BF16), vb, TN)

    def bwd(i, st):
        r = rows(nchunk - 1 - i)
        q, k, vb = q_ref[r, :], k_ref[r, :], v_ref[r, :].astype(BF16)
        back_ref[r, :] = _dot((q * plane(2)).astype(BF16), st.astype(BF16))
        return st * plane(5, LANES) + _dot((k * plane(3)).astype(BF16), vb, TN)

    def both(i, states):
        return fwd(i, states[0]), bwd(i, states[1])

    _chunk_loop(nchunk, both, (zero, zero), RET_UNROLL)
    out_ref[...] += back_ref[...]


def _ret_fwd(z, cos_r, sin_r, tab, norm_g):
    s = z.shape[0]
    nchunk = s // RET_CHUNK
    scale = RET_HD ** -0.5
    col = lambda base: pl.BlockSpec((s, LANES), lambda h: (0, base + h), pipeline_mode=pl.Buffered(1))

    def body(q_ref, k_ref, v_ref, g_ref, cos_ref, sin_ref, tab_ref, ng_ref, o_ref, y_ref, qh, kh, back):
        qh[...] = _rope(q_ref[...], cos_ref[...], sin_ref[...])
        kh[...] = _rope(k_ref[...], cos_ref[...], sin_ref[...]) * scale
        _ret_core(qh, kh, v_ref, tab_ref, o_ref, back, nchunk)
        o = o_ref[...]
        r = lax.rsqrt(jnp.mean(o * o, axis=-1, keepdims=True) + EPS)
        y_ref[...] = (_silu(g_ref[...]) * (o * r * ng_ref[...])).astype(BF16)

    return pl.pallas_call(
        body, name="ret_fwd", grid=(4,),
        in_specs=[col(A_Q), col(A_K), col(A_V), col(A_G),
                  pl.BlockSpec((s, LANES), lambda h: (0, 0), pipeline_mode=pl.Buffered(1)),
                  pl.BlockSpec((s, LANES), lambda h: (0, 0), pipeline_mode=pl.Buffered(1)),
                  pl.BlockSpec((None, RET_CHUNK, RET_CHUNK + 6 * LANES), lambda h: (h, 0, 0)),
                  pl.BlockSpec((1, LANES), lambda h: (0, h))],
        out_specs=[pl.BlockSpec((s, LANES), lambda h: (0, h)), pl.BlockSpec((s, LANES), lambda h: (0, h))],
        out_shape=[jax.ShapeDtypeStruct((s, GROUP_W), F32), jax.ShapeDtypeStruct((s, GROUP_W), BF16)],
        scratch_shapes=[pltpu.VMEM((s, LANES), F32)] * 3,
        compiler_params=_params(("arbitrary",)),
    )(z, z, z, z, cos_r, sin_r, tab, norm_g)


def _ret_bwd(z, d_o, cos_r, sin_r, tab, tab_sw, rider=None):
    s = z.shape[0]
    nchunk = s // RET_CHUNK
    scale = RET_HD ** -0.5
    col = lambda base: pl.BlockSpec((s, LANES), lambda h: (0, base + h), pipeline_mode=pl.Buffered(1))
    whole = lambda: pl.BlockSpec((s, LANES), lambda h: (0, 0), pipeline_mode=pl.Buffered(1))
    tabspec = lambda: pl.BlockSpec((None, RET_CHUNK, RET_CHUNK + 6 * LANES), lambda h: (h, 0, 0))
    outspec = lambda: pl.BlockSpec((s, LANES), lambda h: (0, h))

    def body(q_ref, k_ref, v_ref, do_ref, cos_ref, sin_ref, tab_ref, tsw_ref, dq_ref, dk_ref, dv_ref,
             qh, kh, tmp, back):
        cos, sin = cos_ref[...], sin_ref[...]
        qh[...] = _rope(q_ref[...], cos, sin)
        kh[...] = _rope(k_ref[...], cos, sin) * scale
        _ret_core(kh, qh, do_ref, tsw_ref, tmp, back, nchunk)
        dv_ref[...] = tmp[...].astype(BF16)
        _ret_core(do_ref, v_ref, kh, tab_ref, tmp, back, nchunk)
        dq_ref[...] = _rope_t(tmp[...], cos, sin).astype(BF16)
        _ret_core(v_ref, do_ref, qh, tsw_ref, tmp, back, nchunk)
        dk_ref[...] = _rope_t(tmp[...] * scale, cos, sin).astype(BF16)

    return _call(
        body, "ret_bwd", (4,),
        [col(A_Q), col(A_K), col(A_V),
         pl.BlockSpec((s, LANES), lambda h: (0, h), pipeline_mode=pl.Buffered(1)),
         whole(), whole(), tabspec(), tabspec()],
        [outspec(), outspec(), outspec()],
        [jax.ShapeDtypeStruct((s, GROUP_W), BF16)] * 3,
        [pltpu.VMEM((s, LANES), F32)] * 4,
        ("arbitrary",), (z, z, z, d_o, cos_r, sin_r, tab, tab_sw), rider)


def _normgate_bwd(o, z, gate_blk, dy, dy_blk, norm_g, tm):
    s = o.shape[0]

    def body(o_ref, g_ref, dy_ref, ng_ref, do_ref, dg_ref, dng_ref):
        @pl.when(pl.program_id(0) == 0)
        def _():
            dng_ref[...] = jnp.zeros_like(dng_ref)

        for h in range(4):
            sl = slice(LANES * h, LANES * (h + 1))
            ov, gv, dyv, ng = o_ref[:, sl], g_ref[:, sl], dy_ref[:, sl], ng_ref[:, sl]
            r = lax.rsqrt(jnp.mean(ov * ov, axis=-1, keepdims=True) + EPS)
            on = ov * r
            dn = dyv * _silu(gv)
            u = dn * ng
            do_ref[:, sl] = r * (u - on * jnp.mean(u * on, axis=-1, keepdims=True))
            dg_ref[:, sl] = (dyv * (on * ng) * _silu_grad(gv)).astype(BF16)
            dng_ref[:, sl] += jnp.sum(dn * on, axis=0, keepdims=True)

    return pl.pallas_call(
        body, name="normgate_bwd", grid=(s // tm,),
        in_specs=[pl.BlockSpec((tm, GROUP_W), lambda i: (i, 0)),
                  pl.BlockSpec((tm, GROUP_W), lambda i: (i, gate_blk // 4)),
                  pl.BlockSpec((tm, GROUP_W), lambda i: (i, dy_blk)),
                  pl.BlockSpec((1, GROUP_W), lambda i: (0, 0))],
        out_specs=[pl.BlockSpec((tm, GROUP_W), lambda i: (i, 0)), pl.BlockSpec((tm, GROUP_W), lambda i: (i, 0)),
                   pl.BlockSpec((1, GROUP_W), lambda i: (0, 0))],
        out_shape=[jax.ShapeDtypeStruct((s, GROUP_W), F32), jax.ShapeDtypeStruct((s, GROUP_W), BF16),
                   jax.ShapeDtypeStruct((1, GROUP_W), F32)],
        compiler_params=_params(("arbitrary",)),
    )(o, z, dy, norm_g)


def _log_sigmoid(x):
    return jnp.minimum(x, 0.0) - jnp.log(1.0 + jnp.exp(-jnp.abs(x)))


def _gla_consts():
    c = GLA_CHUNK
    row = lax.broadcasted_iota(jnp.int32, (c, c), 0)
    colm = lax.broadcasted_iota(jnp.int32, (c, c), 1)
    lane = lax.broadcasted_iota(jnp.int32, (1, LANES), 1)
    low = row >= colm
    up = colm >= row
    heads = ((lane < GLA_DK).astype(F32), (lane >= GLA_DK).astype(F32))
    return low, up, heads


def _chunk_running_sum(x, suffix):
    rows = x.shape[0]
    pos = jnp.bitwise_and(lax.broadcasted_iota(jnp.int32, (rows, 1), 0), GLA_CHUNK - 1)
    k = 1
    while k < GLA_CHUNK:
        if suffix:
            x = x + jnp.where(pos < GLA_CHUNK - k, pltpu.roll(x, rows - k, 0), 0.0)
        else:
            x = x + jnp.where(pos >= k, pltpu.roll(x, k, 0), 0.0)
        k *= 2
    return x


def _gla_chunk(cum_ref, d, n):
    c = GLA_CHUNK
    cum = cum_ref[d, pl.ds(pl.multiple_of(n * c, c), c), :]
    last = cum_ref[d, pl.ds(n * c + (c - 1 if d == 0 else 0), 1), :]
    eq = jnp.exp(cum)
    ek = jnp.exp(-cum)
    el = jnp.exp(last - cum)
    dec = jnp.exp(last)
    return eq, ek, el, dec


def _gla_gates(ga_ref, wa_ref, ba_ref, cum_ref, s, suffix):
    rows = min(s, GLA_CUM_ROWS)

    def step(i, carry):
        r = pl.ds(pl.multiple_of(i * rows, rows), rows)
        pre = _dot(ga_ref[r, :].astype(BF16), wa_ref[...].astype(BF16)) + ba_ref[...]
        cum_ref[r, :] = _chunk_running_sum(_log_sigmoid(pre) * (1.0 / GLA_TAU), suffix)
        return carry
    lax.fori_loop(0, s // rows, step, 0)


def _gla_fwd(z, wa_f, wa_b, ba_f, ba_b, norm_g):
    s = z.shape[0]
    c = GLA_CHUNK
    nchunk = s // c
    scale = GLA_DK ** -0.5
    tm = min(s, 512)
    one = pl.Buffered(1)

    def body(q_ref, k_ref, v_ref, ga_ref, g_ref, waf_ref, wab_ref, baf_ref, bab_ref, ng_ref, o_ref, y_ref,
             la_s):
        low, up, heads = _gla_consts()
        _gla_gates(ga_ref, waf_ref, baf_ref, la_s.at[0], s, False)
        _gla_gates(ga_ref, wab_ref, bab_ref, la_s.at[1], s, True)
        for d in range(2):
            tri = (low, up)[d]

            def step(i, states):
                n = i if d == 0 else nchunk - 1 - i
                r = pl.ds(pl.multiple_of(n * c, c), c)
                q = q_ref[r, :] * scale
                k = k_ref[r, :]
                eq, ek, el, dec = _gla_chunk(la_s, d, n)
                qt = q * eq
                ktb = (k * ek).astype(BF16)
                kl = k * el
                new_states = []
                for hh in range(2):
                    cols = slice(LANES * hh, LANES * (hh + 1))
                    vb = v_ref[r, cols].astype(BF16)
                    qm = (qt * heads[hh]).astype(BF16)
                    a = jnp.where(tri, _dot(qm, ktb, NT), 0.0)
                    o = _dot(a.astype(BF16), vb) + _dot(qm, states[hh].astype(BF16), NT)
                    if d == 0:
                        o_ref[r, cols] = o
                    else:
                        o_ref[r, cols] += o
                    new_states.append(states[hh] * dec + _dot(vb, (kl * heads[hh]).astype(BF16), TN))
                return tuple(new_states)

            zero = jnp.zeros((LANES, LANES), F32)
            _chunk_loop(nchunk, step, (zero, zero), GLA_UNROLL)

        def epi(i, carry):
            r = pl.ds(pl.multiple_of(i * tm, tm), tm)
            for hh in range(2):
                cols = slice(LANES * hh, LANES * (hh + 1))
                o = o_ref[r, cols]
                rr = lax.rsqrt(jnp.mean(o * o, axis=-1, keepdims=True) + EPS)
                y_ref[r, cols] = (_silu(g_ref[r, cols]) * (o * rr * ng_ref[:, cols])).astype(BF16)
            return carry

        lax.fori_loop(0, s // tm, epi, 0)

    w2 = 2 * LANES
    return pl.pallas_call(
        body, name="gla_fwd", grid=(2,),
        in_specs=[pl.BlockSpec((s, LANES), lambda p: (0, B_Q + p), pipeline_mode=one),
                  pl.BlockSpec((s, LANES), lambda p: (0, B_K + p), pipeline_mode=one),
                  pl.BlockSpec((s, w2), lambda p: (0, B_V // 2 + p), pipeline_mode=one),
                  pl.BlockSpec((s, LANES), lambda p: (0, GA), pipeline_mode=one),
                  pl.BlockSpec((s, w2), lambda p: (0, B_G // 2 + p), pipeline_mode=one),
                  pl.BlockSpec((LANES, LANES), lambda p: (0, p)),
                  pl.BlockSpec((LANES, LANES), lambda p: (0, p)),
                  pl.BlockSpec((1, LANES), lambda p: (0, p)),
                  pl.BlockSpec((1, LANES), lambda p: (0, p)),
                  pl.BlockSpec((1, w2), lambda p: (0, p))],
        out_specs=[pl.BlockSpec((s, w2), lambda p: (0, p)), pl.BlockSpec((s, w2), lambda p: (0, p))],
        out_shape=[jax.ShapeDtypeStruct((s, GROUP_W), F32), jax.ShapeDtypeStruct((s, GROUP_W), BF16)],
        scratch_shapes=[pltpu.VMEM((2, s, LANES), F32)],
        compiler_params=_params(("arbitrary",)),
    )(z, z, z, z, z, wa_f, wa_b, ba_f, ba_b, norm_g)


def _gla_bwd(z, d_o, wa_f, wa_b, ba_f, ba_b, rider=None):
    s = z.shape[0]
    c = GLA_CHUNK
    nchunk = s // c
    scale = GLA_DK ** -0.5
    tm = min(s, GLA_CUM_ROWS)
    one = pl.Buffered(1)

    def body(q_ref, k_ref, v_ref, ga_ref, do_ref, waf_ref, wab_ref, baf_ref, bab_ref,
             dq_ref, dk_ref, dv_ref, dga_ref, dwaf_ref, dwab_ref, dbaf_ref, dbab_ref,
             la_s, dla_s, stash, dq_s, dk_s, dv_s):
        low, up, heads = _gla_consts()
        rowi = lax.broadcasted_iota(jnp.int32, (c, 1), 0)
        _gla_gates(ga_ref, waf_ref, baf_ref, la_s.at[0], s, False)
        _gla_gates(ga_ref, wab_ref, bab_ref, la_s.at[1], s, True)
        for d in range(2):
            tri = (low, up)[d]
            last_row = (rowi == (c - 1 if d == 0 else 0)).astype(F32)
            order = (lambda i: i) if d == 0 else (lambda i: nchunk - 1 - i)
            zero = jnp.zeros((LANES, LANES), F32)

            def states(i, sts):
                n = order(i)
                r = pl.ds(pl.multiple_of(n * c, c), c)
                k = k_ref[r, :]
                _, _, el, dec = _gla_chunk(la_s, d, n)
                kl = k * el
                new = []
                for hh in range(2):
                    cols = slice(LANES * hh, LANES * (hh + 1))
                    stash[hh, n] = sts[hh]
                    new.append(sts[hh] * dec + _dot(v_ref[r, cols].astype(BF16), (kl * heads[hh]).astype(BF16), TN))
                return tuple(new)

            _chunk_loop(nchunk, states, (zero, zero), GLA_UNROLL)

            def step(i, dsts):
                n = order(nchunk - 1 - i)
                r = pl.ds(pl.multiple_of(n * c, c), c)
                q = q_ref[r, :] * scale
                k = k_ref[r, :]
                eq, ek, el, dec = _gla_chunk(la_s, d, n)
                qt = q * eq
                kt = k * ek
                kl = k * el
                ktb = kt.astype(BF16)
                dqt = jnp.zeros((c, LANES), F32)
                dkt = jnp.zeros((c, LANES), F32)
                dkl = jnp.zeros((c, LANES), F32)
                ddec = jnp.zeros((1, LANES), F32)
                new = []
                for hh in range(2):
                    cols = slice(LANES * hh, LANES * (hh + 1))
                    vb = v_ref[r, cols].astype(BF16)
                    dob = do_ref[r, cols].astype(BF16)
                    qm = (qt * heads[hh]).astype(BF16)
                    a = jnp.where(tri, _dot(qm, ktb, NT), 0.0).astype(BF16)
                    da = jnp.where(tri, _dot(dob, vb, NT), 0.0).astype(BF16)
                    sn = stash[hh, n]
                    dst = dsts[hh]
                    dstb = dst.astype(BF16)
                    dqt = dqt + (_dot(da, ktb) + _dot(dob, sn.astype(BF16))) * heads[hh]
                    dkt = dkt + _dot(da, qm, TN)
                    dv = _dot(a, dob, TN) + _dot((kl * heads[hh]).astype(BF16), dstb, NT)
                    dkl = dkl + _dot(vb, dstb)
                    ddec = ddec + jnp.sum(dst * sn, axis=0, keepdims=True)
                    new.append(dst * dec + _dot(dob, qm, TN))
                    if d == 0:
                        dv_s[r, cols] = dv
                    else:
                        dv_ref[r, cols] = (dv_s[r, cols] + dv).astype(BF16)
                dlast = ddec * dec + jnp.sum(dkl * kl, axis=0, keepdims=True)
                dq = dqt * eq * scale
                dk = dkt * ek + dkl * el
                dcum = dqt * qt - dkt * kt - dkl * kl + last_row * dlast
                dla_s[d, r, :] = dcum
                if d == 0:
                    dq_s[r, :] = dq
                    dk_s[r, :] = dk
                else:
                    dq_ref[r, :] = (dq_s[r, :] + dq).astype(BF16)
                    dk_ref[r, :] = (dk_s[r, :] + dk).astype(BF16)
                return tuple(new)

            _chunk_loop(nchunk, step, (zero, zero), GLA_UNROLL)

        first = pl.program_id(0) == 0
        for d, (wa_ref, ba_ref, dwa_ref, dba_ref) in enumerate(
                ((waf_ref, baf_ref, dwaf_ref, dbaf_ref), (wab_ref, bab_ref, dwab_ref, dbab_ref))):
            dwa_ref[...] = jnp.zeros_like(dwa_ref)
            dba_ref[...] = jnp.zeros_like(dba_ref)

            def gates(i, carry):
                r = pl.ds(pl.multiple_of(i * tm, tm), tm)
                gab = ga_ref[r, :].astype(BF16)
                wab16 = wa_ref[...].astype(BF16)
                pre = _dot(gab, wab16) + ba_ref[...]
                dla = _chunk_running_sum(dla_s[d, r, :], suffix=(d == 0))
                dpre = dla * (1.0 / GLA_TAU) * _sigmoid(-pre)
                dpb = dpre.astype(BF16)
                dwa_ref[...] += _dot(gab, dpb, TN)
                dba_ref[...] += jnp.sum(dpre, axis=0, keepdims=True)
                dga = _dot(dpb, wab16, NT)
                if d == 0:
                    @pl.when(first)
                    def _():
                        dga_ref[r, :] = dga

                    @pl.when(jnp.logical_not(first))
                    def _():
                        dga_ref[r, :] += dga
                else:
                    dga_ref[r, :] += dga
                return carry

            lax.fori_loop(0, s // tm, gates, 0)

    w2 = 2 * LANES
    return _call(
        body, "gla_bwd", (2,),
        [pl.BlockSpec((s, LANES), lambda p: (0, B_Q + p), pipeline_mode=one),
         pl.BlockSpec((s, LANES), lambda p: (0, B_K + p), pipeline_mode=one),
         pl.BlockSpec((s, w2), lambda p: (0, B_V // 2 + p), pipeline_mode=one),
         pl.BlockSpec((s, LANES), lambda p: (0, GA), pipeline_mode=one),
         pl.BlockSpec((s, w2), lambda p: (0, p), pipeline_mode=one),
         pl.BlockSpec((LANES, LANES), lambda p: (0, p)),
         pl.BlockSpec((LANES, LANES), lambda p: (0, p)),
         pl.BlockSpec((1, LANES), lambda p: (0, p)),
         pl.BlockSpec((1, LANES), lambda p: (0, p))],
        [pl.BlockSpec((s, LANES), lambda p: (0, p), pipeline_mode=one),
         pl.BlockSpec((s, LANES), lambda p: (0, p), pipeline_mode=one),
         pl.BlockSpec((s, w2), lambda p: (0, p), pipeline_mode=one),
         pl.BlockSpec((s, LANES), lambda p: (0, 0), pipeline_mode=one),
         pl.BlockSpec((LANES, LANES), lambda p: (0, p)),
         pl.BlockSpec((LANES, LANES), lambda p: (0, p)),
         pl.BlockSpec((1, LANES), lambda p: (0, p)),
         pl.BlockSpec((1, LANES), lambda p: (0, p))],
        [jax.ShapeDtypeStruct((s, w2), BF16), jax.ShapeDtypeStruct((s, w2), BF16),
         jax.ShapeDtypeStruct((s, GROUP_W), BF16), jax.ShapeDtypeStruct((s, LANES), F32),
         jax.ShapeDtypeStruct((LANES, w2), F32), jax.ShapeDtypeStruct((LANES, w2), F32),
         jax.ShapeDtypeStruct((1, w2), F32), jax.ShapeDtypeStruct((1, w2), F32)],
        [pltpu.VMEM((2, s, LANES), F32), pltpu.VMEM((2, s, LANES), F32),
         pltpu.VMEM((2, nchunk, LANES, LANES), F32),
         pltpu.VMEM((s, LANES), F32), pltpu.VMEM((s, LANES), F32), pltpu.VMEM((s, w2), F32)],
        ("arbitrary",), (z, z, z, z, d_o, wa_f, wa_b, ba_f, ba_b), rider)


def _shift_rows(x, d, rowi):
    s = x.shape[0]
    if d == 0:
        return x
    y = pltpu.roll(x, d % s, 0)
    keep = (rowi >= d) if d > 0 else (rowi < s + d)
    return jnp.where(keep, y, 0.0)


def _run_sum(x, m, step, rowi):
    acc, n = x, 1
    while n < m:
        acc = acc + _shift_rows(acc, step * n, rowi)
        n *= 2
    return acc


def _pool_counts(s, w, rowi):
    hi = jnp.minimum(rowi + w // 2, s)
    lo = jnp.maximum(rowi - w // 2, 0)
    return (hi - lo).astype(F32)


def _pooled(u, w, rowi):
    s = u.shape[0]
    win = _shift_rows(_run_sum(u, w // 2, 1, rowi), 1, rowi) + _run_sum(u, w // 2, -1, rowi)
    return win / _pool_counts(s, w, rowi) - u


def _pool_fwd(z, pool_w, pool_scale):
    s = z.shape[0]
    one = pl.Buffered(1)

    def body(u_ref, g_ref, w_ref, sc_ref, y_ref):
        rowi = lax.broadcasted_iota(jnp.int32, (s, 1), 0)
        for g, w in enumerate(POOL_WINDOWS):
            cols = slice(LANES * g, LANES * (g + 1))
            pooled = _pooled(u_ref[:, cols], w, rowi)
            mixed = _dot(pooled.astype(BF16), w_ref[g].astype(BF16))
            y_ref[:, cols] = (_silu(g_ref[:, cols]) * (mixed * sc_ref[:, cols])).astype(BF16)

    return pl.pallas_call(
        body, name="pool_fwd", grid=(1,),
        in_specs=[pl.BlockSpec((s, GROUP_W), lambda i: (0, C_V // 4), pipeline_mode=one),
                  pl.BlockSpec((s, GROUP_W), lambda i: (0, C_G // 4), pipeline_mode=one),
                  pl.BlockSpec((4, LANES, LANES), lambda i: (0, 0, 0)),
                  pl.BlockSpec((1, GROUP_W), lambda i: (0, 0))],
        out_specs=pl.BlockSpec((s, GROUP_W), lambda i: (0, 0), pipeline_mode=one),
        out_shape=jax.ShapeDtypeStruct((s, GROUP_W), BF16),
        compiler_params=_params(("arbitrary",)),
    )(z, z, pool_w, pool_scale)


def _pool_bwd(z, dy, pool_w, pool_scale):
    s = z.shape[0]
    one = pl.Buffered(1)

    def body(u_ref, g_ref, dy_ref, w_ref, sc_ref, du_ref, dg_ref, dw_ref, dsc_ref):
        rowi = lax.broadcasted_iota(jnp.int32, (s, 1), 0)
        for g, w in enumerate(POOL_WINDOWS):
            cols = slice(LANES * g, LANES * (g + 1))
            gate, dyv, sc = g_ref[:, cols], dy_ref[:, cols], sc_ref[:, cols]
            wb = w_ref[g].astype(BF16)
            pooled = _pooled(u_ref[:, cols], w, rowi)
            pb = pooled.astype(BF16)
            mixed = _dot(pb, wb)
            dg_ref[:, cols] = (dyv * (mixed * sc) * _silu_grad(gate)).astype(BF16)
            dt = dyv * _silu(gate)
            dsc_ref[:, cols] = jnp.sum(dt * mixed, axis=0, keepdims=True)
            dmb = (dt * sc).astype(BF16)
            dw_ref[g] = _dot(pb, dmb, TN)
            dpool = _dot(dmb, wb, NT)
            e = dpool / _pool_counts(s, w, rowi)
            du_ref[:, cols] = (_run_sum(e, w // 2, 1, rowi) + _shift_rows(_run_sum(e, w // 2, -1, rowi), -1, rowi)
                               - dpool).astype(BF16)

    return pl.pallas_call(
        body, name="pool_bwd", grid=(1,),
        in_specs=[pl.BlockSpec((s, GROUP_W), lambda i: (0, C_V // 4), pipeline_mode=one),
                  pl.BlockSpec((s, GROUP_W), lambda i: (0, C_G // 4), pipeline_mode=one),
                  pl.BlockSpec((s, GROUP_W), lambda i: (0, 2), pipeline_mode=one),
                  pl.BlockSpec((4, LANES, LANES), lambda i: (0, 0, 0)),
                  pl.BlockSpec((1, GROUP_W), lambda i: (0, 0))],
        out_specs=[pl.BlockSpec((s, GROUP_W), lambda i: (0, 0), pipeline_mode=one),
                   pl.BlockSpec((s, GROUP_W), lambda i: (0, 0), pipeline_mode=one),
                   pl.BlockSpec((4, LANES, LANES), lambda i: (0, 0, 0)),
                   pl.BlockSpec((1, GROUP_W), lambda i: (0, 0))],
        out_shape=[jax.ShapeDtypeStruct((s, GROUP_W), BF16), jax.ShapeDtypeStruct((s, GROUP_W), BF16),
                   jax.ShapeDtypeStruct((4, LANES, LANES), F32), jax.ShapeDtypeStruct((1, GROUP_W), F32)],
        compiler_params=_params(("arbitrary",)),
    )(z, z, dy, pool_w, pool_scale)


def _mla_heads(qf, kv, kpe, qg, kg, cos, sin):
    out = []
    for h in range(4):
        qa = qf[:, LANES * h:LANES * (h + 1)]
        qb = qf[:, 512 + LANES * h:512 + LANES * (h + 1)]
        ka = kv[:, 256 * h:256 * h + LANES]
        rq = lax.rsqrt((jnp.sum(qa * qa, axis=-1, keepdims=True) + jnp.sum(qb * qb, axis=-1, keepdims=True))
                       * (1.0 / MLA_QK) + EPS)
        rk = lax.rsqrt((jnp.sum(ka * ka, axis=-1, keepdims=True) + jnp.sum(kpe * kpe, axis=-1, keepdims=True))
                       * (1.0 / MLA_QK) + EPS)
        out.append((qa, qb, rq, ka, rk))
    return out


def _mla_latents(mq_ref, mkv_ref, gq_ref, gkv_ref, wq_ref, wkv_ref):
    mq = mq_ref[...]
    rq = lax.rsqrt(jnp.mean(mq * mq, axis=-1, keepdims=True) + EPS)
    qn = mq * rq
    qnb = (qn * gq_ref[...]).astype(BF16)
    mkv = mkv_ref[...]
    rk = lax.rsqrt(jnp.mean(mkv * mkv, axis=-1, keepdims=True) + EPS)
    kvn = mkv * rk
    kvnb = (kvn * gkv_ref[...]).astype(BF16)
    qf = _dot(qnb, wq_ref[...])
    kv = _dot(kvnb, wkv_ref[...])
    return qn, rq, qnb, kvn, rk, kvnb, qf, kv


def _mla_prep(z, cos_m, sin_m, gq, wq, gkv, wkv, qg, kg, tm):
    s = z.shape[0]

    def body(mq_ref, mkv_ref, mkr_ref, cos_ref, sin_ref, gq_ref, wq_ref, gkv_ref, wkv_ref, qg_ref, kg_ref,
             q_ref, k_ref, v_ref):
        _, _, _, _, _, _, qf, kv = _mla_latents(mq_ref, mkv_ref, gq_ref, gkv_ref, wq_ref, wkv_ref)
        kpe = mkr_ref[...]
        cos, sin = cos_ref[...], sin_ref[...]
        qg, kg = qg_ref[...], kg_ref[...]
        for h, (qa, qb, rq, ka, rk) in enumerate(_mla_heads(qf, kv, kpe, qg, kg, cos, sin)):
            q_ref[h, :, 0:LANES] = (qa * rq * qg[:, 0:LANES] * ATTN_Q_SCALE).astype(BF16)
            q_ref[h, :, LANES:] = (_rope(qb * rq * qg[:, LANES:], cos, sin) * ATTN_Q_SCALE).astype(BF16)
            k_ref[h, :, 0:LANES] = (ka * rk * kg[:, 0:LANES]).astype(BF16)
            k_ref[h, :, LANES:] = _rope(kpe * rk * kg[:, LANES:], cos, sin).astype(BF16)
            v_ref[h] = kv[:, 256 * h + LANES:256 * (h + 1)].astype(BF16)

    full = lambda shape: pl.BlockSpec(shape, lambda i: (0,) * len(shape))
    return pl.pallas_call(
        body, name="mla_prep", grid=(s // tm,),
        in_specs=[pl.BlockSpec((tm, 512), lambda i: (i, M_Q // 4)),
                  pl.BlockSpec((tm, 256), lambda i: (i, M_KV // 2)),
                  pl.BlockSpec((tm, LANES), lambda i: (i, M_KR)),
                  pl.BlockSpec((tm, LANES), lambda i: (i, 0)),
                  pl.BlockSpec((tm, LANES), lambda i: (i, 0)),
                  full((1, 512)), full((512, 1024)), full((1, 256)), full((256, 1024)), full((1, 256)), full((1, 256))],
        out_specs=[pl.BlockSpec((4, tm, 256), lambda i: (0, i, 0)), pl.BlockSpec((4, tm, 256), lambda i: (0, i, 0)),
                   pl.BlockSpec((4, tm, LANES), lambda i: (0, i, 0))],
        out_shape=[jax.ShapeDtypeStruct((4, s, 256), BF16), jax.ShapeDtypeStruct((4, s, 256), BF16),
                   jax.ShapeDtypeStruct((4, s, LANES), BF16)],
        compiler_params=_params(("parallel",)),
    )(z, z, z, cos_m, sin_m, gq, wq, gkv, wkv, qg, kg)


def _mla_prep_bwd(z, cos_m, sin_m, gq, wq, gkv, wkv, qg, kg, dq, dk, dv, tm):
    s = z.shape[0]

    def body(mq_ref, mkv_ref, mkr_ref, cos_ref, sin_ref, gq_ref, wq_ref, gkv_ref, wkv_ref, qg_ref, kg_ref,
             dq_ref, dk_ref, dv_ref,
             dmq_ref, dmkv_ref, dmkr_ref, dwq_ref, dwkv_ref, dgq_ref, dgkv_ref, dqg_ref, dkg_ref, dqf, dkv):
        @pl.when(pl.program_id(0) == 0)
        def _():
            for r in (dwq_ref, dwkv_ref, dgq_ref, dgkv_ref, dqg_ref, dkg_ref):
                r[...] = jnp.zeros_like(r)

        qn, rq0, qnb, kvn, rk0, kvnb, qf, kv = _mla_latents(mq_ref, mkv_ref, gq_ref, gkv_ref, wq_ref, wkv_ref)
        kpe = mkr_ref[...]
        cos, sin = cos_ref[...], sin_ref[...]
        qg, kg = qg_ref[...], kg_ref[...]
        dkpe = jnp.zeros_like(kpe)
        inv = 1.0 / MLA_QK

        def norm_bwd(a, b, r, da_n, db_n, g):
            ga, gb = g[:, 0:LANES], g[:, LANES:]
            dg_a = jnp.sum(da_n * a * r, axis=0, keepdims=True)
            dg_b = jnp.sum(db_n * b * r, axis=0, keepdims=True)
            ua, ub = da_n * ga, db_n * gb
            dt = (jnp.sum(ua * a, axis=-1, keepdims=True) + jnp.sum(ub * b, axis=-1, keepdims=True)) * inv
            r3 = r * r * r
            return r * ua - a * (r3 * dt), r * ub - b * (r3 * dt), dg_a, dg_b

        for h, (qa, qb, rq, ka, rk) in enumerate(_mla_heads(qf, kv, kpe, qg, kg, cos, sin)):
            dqa, dqb, dga, dgb = norm_bwd(qa, qb, rq, dq_ref[h, :, 0:LANES] * ATTN_SCALE,
                                          _rope_t(dq_ref[h, :, LANES:] * ATTN_SCALE, cos, sin), qg)
            dqf[:, LANES * h:LANES * (h + 1)] = dqa
            dqf[:, 512 + LANES * h:512 + LANES * (h + 1)] = dqb
            dqg_ref[:, 0:LANES] += dga
            dqg_ref[:, LANES:] += dgb
            ln2 = math.log(2.0)
            dka, dkb, dga, dgb = norm_bwd(ka, kpe, rk, dk_ref[h, :, 0:LANES] * ln2,
                                          _rope_t(dk_ref[h, :, LANES:] * ln2, cos, sin), kg)
            dkv[:, 256 * h:256 * h + LANES] = dka
            dkv[:, 256 * h + LANES:256 * (h + 1)] = dv_ref[h]
            dkpe = dkpe + dkb
            dkg_ref[:, 0:LANES] += dga
            dkg_ref[:, LANES:] += dgb
        dmkr_ref[...] = dkpe.astype(BF16)

        def latent_bwd(dfull, w_ref, nb, n, r, g_ref, dw_ref, dg_ref, dlat_ref):
            db = dfull.astype(BF16)
            dn = _dot(db, w_ref[...], NT)
            dw_ref[...] += _dot(nb, db, TN)
            dg_ref[...] += jnp.sum(dn * n, axis=0, keepdims=True)
            u = dn * g_ref[...]
            dlat_ref[...] = (r * (u - n * jnp.mean(u * n, axis=-1, keepdims=True))).astype(BF16)

        latent_bwd(dqf[...], wq_ref, qnb, qn, rq0, gq_ref, dwq_ref, dgq_ref, dmq_ref)
        latent_bwd(dkv[...], wkv_ref, kvnb, kvn, rk0, gkv_ref, dwkv_ref, dgkv_ref, dmkv_ref)

    full = lambda shape: pl.BlockSpec(shape, lambda i: (0,) * len(shape))
    return pl.pallas_call(
        body, name="mla_prep_bwd", grid=(s // tm,),
        in_specs=[pl.BlockSpec((tm, 512), lambda i: (i, M_Q // 4)),
                  pl.BlockSpec((tm, 256), lambda i: (i, M_KV // 2)),
                  pl.BlockSpec((tm, LANES), lambda i: (i, M_KR)),
                  pl.BlockSpec((tm, LANES), lambda i: (i, 0)),
                  pl.BlockSpec((tm, LANES), lambda i: (i, 0)),
                  full((1, 512)), full((512, 1024)), full((1, 256)), full((256, 1024)), full((1, 256)), full((1, 256)),
                  pl.BlockSpec((4, tm, 256), lambda i: (0, i, 0)), pl.BlockSpec((4, tm, 256), lambda i: (0, i, 0)),
                  pl.BlockSpec((4, tm, LANES), lambda i: (0, i, 0))],
        out_specs=[pl.BlockSpec((tm, 512), lambda i: (i, 0)), pl.BlockSpec((tm, 256), lambda i: (i, 0)),
                   pl.BlockSpec((tm, LANES), lambda i: (i, 0)),
                   full((512, 1024)), full((256, 1024)), full((1, 512)), full((1, 256)), full((1, 256)), full((1, 256))],
        out_shape=[jax.ShapeDtypeStruct((s, 512), BF16), jax.ShapeDtypeStruct((s, 256), BF16),
                   jax.ShapeDtypeStruct((s, LANES), BF16),
                   jax.ShapeDtypeStruct((512, 1024), F32), jax.ShapeDtypeStruct((256, 1024), F32),
                   jax.ShapeDtypeStruct((1, 512), F32), jax.ShapeDtypeStruct((1, 256), F32),
                   jax.ShapeDtypeStruct((1, 256), F32), jax.ShapeDtypeStruct((1, 256), F32)],
        scratch_shapes=[pltpu.VMEM((tm, 1024), F32), pltpu.VMEM((tm, 1024), F32)],
        compiler_params=_params(("arbitrary",)),
    )(z, z, z, cos_m, sin_m, gq, wq, gkv, wkv, qg, kg, dq, dk, dv)


def _attn_fwd(q, k, v, z, tq, rider=None):
    s = q.shape[1]

    def body(q_ref, k_ref, v_ref, g_ref, o_ref, y_ref, lse_ref):
        sc = _dot(q_ref[...], k_ref[...], NT)
        m = jnp.max(sc, axis=-1, keepdims=True)
        p = jnp.exp2(sc - m)
        l = jnp.sum(p, axis=-1, keepdims=True)
        o = _dot(p.astype(BF16), v_ref[...]) / l
        o_ref[...] = o
        y_ref[...] = (_silu(g_ref[...]) * o).astype(BF16)
        lse_ref[...] = m + jnp.log2(l)

    return _call(
        body, "attn_fwd", (4, s // tq),
        [pl.BlockSpec((None, tq, 256), lambda h, i: (h, i, 0)),
         pl.BlockSpec((None, s, 256), lambda h, i: (h, 0, 0)),
         pl.BlockSpec((None, s, LANES), lambda h, i: (h, 0, 0)),
         pl.BlockSpec((tq, LANES), lambda h, i: (i, M_G + h))],
        [pl.BlockSpec((tq, LANES), lambda h, i: (i, h)), pl.BlockSpec((tq, LANES), lambda h, i: (i, h)),
         pl.BlockSpec((None, tq, 1), lambda h, i: (h, i, 0))],
        [jax.ShapeDtypeStruct((s, GROUP_W), F32), jax.ShapeDtypeStruct((s, GROUP_W), BF16),
         jax.ShapeDtypeStruct((4, s, 1), F32)],
        [], ("parallel", "parallel"), (q, k, v, z), rider)


def _attn_bwd(q, k, v, z, o, lse, dy, tq, rider=None):
    s = q.shape[1]

    def body(q_ref, k_ref, v_ref, g_ref, o_ref, lse_ref, dy_ref, dq_ref, dk_ref, dv_ref, dg_ref):
        @pl.when(pl.program_id(1) == 0)
        def _():
            dk_ref[...] = jnp.zeros_like(dk_ref)
            dv_ref[...] = jnp.zeros_like(dv_ref)

        gate, ov, dyv = g_ref[...], o_ref[...], dy_ref[...]
        do = dyv * _silu(gate)
        dg_ref[...] = (dyv * ov * _silu_grad(gate)).astype(BF16)
        delta = jnp.sum(do * ov, axis=-1, keepdims=True)
        dob = do.astype(BF16)
        qb, kb = q_ref[...], k_ref[...]
        p = jnp.exp2(_dot(qb, kb, NT) - lse_ref[...])
        dp = _dot(dob, v_ref[...], NT)
        ds = (p * (dp - delta)).astype(BF16)
        dq_ref[...] = _dot(ds, kb)
        dk_ref[...] += _dot(ds, qb, TN)
        dv_ref[...] += _dot(p.astype(BF16), dob, TN)

    return _call(
        body, "attn_bwd", (4, s // tq),
        [pl.BlockSpec((None, tq, 256), lambda h, i: (h, i, 0)),
         pl.BlockSpec((None, s, 256), lambda h, i: (h, 0, 0)),
         pl.BlockSpec((None, s, LANES), lambda h, i: (h, 0, 0)),
         pl.BlockSpec((tq, LANES), lambda h, i: (i, M_G + h)),
         pl.BlockSpec((tq, LANES), lambda h, i: (i, h)),
         pl.BlockSpec((None, tq, 1), lambda h, i: (h, i, 0)),
         pl.BlockSpec((tq, LANES), lambda h, i: (i, 12 + h))],
        [pl.BlockSpec((None, tq, 256), lambda h, i: (h, i, 0)),
         pl.BlockSpec((None, s, 256), lambda h, i: (h, 0, 0)),
         pl.BlockSpec((None, s, LANES), lambda h, i: (h, 0, 0)),
         pl.BlockSpec((tq, LANES), lambda h, i: (i, h))],
        [jax.ShapeDtypeStruct((4, s, 256), F32), jax.ShapeDtypeStruct((4, s, 256), F32),
         jax.ShapeDtypeStruct((4, s, LANES), F32), jax.ShapeDtypeStruct((s, GROUP_W), BF16)],
        [], ("parallel", "arbitrary"), (q, k, v, z, o, lse, dy), rider)


def _adam(parts, w, m, v, name, tr):
    r, c = w.shape
    tr = min(tr, r)
    c1 = 1.0 - ADAM_B1 ** ADAM_STEP
    c2 = 1.0 - ADAM_B2 ** ADAM_STEP

    def body(p_ref, w_ref, m_ref, v_ref, g_ref, d_ref, nm_ref, nv_ref):
        g = p_ref[0].astype(F32)
        for i in range(1, N_DEV):
            g = g + p_ref[i].astype(F32)
        nm = ADAM_B1 * m_ref[...] + (1.0 - ADAM_B1) * g
        nv = ADAM_B2 * v_ref[...] + (1.0 - ADAM_B2) * (g * g)
        g_ref[...] = g
        nm_ref[...] = nm
        nv_ref[...] = nv
        d_ref[...] = -ADAM_LR * ((nm / c1) / (jnp.sqrt(nv / c2) + ADAM_EPS) + ADAM_WD * w_ref[...])

    blk = lambda: pl.BlockSpec((tr, c), lambda i: (i, 0))
    return pl.pallas_call(
        body, name=name, grid=(r // tr,),
        in_specs=[pl.BlockSpec((N_DEV, tr, c), lambda i: (0, i, 0)), blk(), blk(), blk()],
        out_specs=[blk(), blk(), blk(), blk()],
        out_shape=[jax.ShapeDtypeStruct((r, c), F32)] * 4,
        compiler_params=_params(("parallel",)),
    )(parts, w, m, v)


def _adam_columns(parts, w, m, v, name, tc, rider=None):
    nl, r, c = w.shape
    pieces = [p for layer in parts for p in layer]
    nh = len(parts[0])
    rp = pieces[0].shape[2]
    tc = min(tc, rp)
    ncb = rp // tc
    c1 = 1.0 - ADAM_B1 ** ADAM_STEP
    c2 = 1.0 - ADAM_B2 ** ADAM_STEP

    def body(*refs):
        p_refs, (w_ref, m_ref, v_ref, g_ref, d_ref, nm_ref, nv_ref) = refs[:len(pieces)], refs[len(pieces):]
        for h in range(nh):
            @pl.when(pl.program_id(0) == h)
            def _(h=h):
                for l in range(nl):
                    p_ref = p_refs[l * nh + h]
                    g = p_ref[0].astype(F32)
                    for i in range(1, N_DEV):
                        g = g + p_ref[i].astype(F32)
                    nm = ADAM_B1 * m_ref[:, l, :] + (1.0 - ADAM_B1) * g
                    nv = ADAM_B2 * v_ref[:, l, :] + (1.0 - ADAM_B2) * (g * g)
                    g_ref[:, l, :] = g
                    nm_ref[:, l, :] = nm
                    nv_ref[:, l, :] = nv
                    d_ref[:, l, :] = -ADAM_LR * ((nm / c1) / (jnp.sqrt(nv / c2) + ADAM_EPS) + ADAM_WD * w_ref[:, l, :])

    def part_spec(j):
        return pl.BlockSpec((N_DEV, c, tc), lambda h, i: (0, 0, jnp.clip((h - j % nh) * ncb + i, 0, ncb - 1)))

    blk = lambda: pl.BlockSpec((c, nl, tc), lambda h, i: (0, 0, h * ncb + i))
    t = lambda a: jnp.transpose(a, (2, 0, 1))
    *res, = _call(body, name, (nh, ncb), [part_spec(j) for j in range(len(pieces))] + [blk(), blk(), blk()],
                  [blk(), blk(), blk(), blk()], [jax.ShapeDtypeStruct((c, nl, r), F32)] * 4, [],
                  ("arbitrary",) * 2, (*pieces, t(w), t(m), t(v)), rider)
    return [jnp.transpose(a, (1, 2, 0)) for a in res[:4]] + res[4:]


def _adam_layers(parts, w, m, v, name, tr, rider=None):
    nl, r, c = w.shape
    pieces = [p for layer in parts for p in layer]
    rp = pieces[0].shape[1]
    tr = min(tr, rp)
    nr, nrp = r // tr, rp // tr
    c1 = 1.0 - ADAM_B1 ** ADAM_STEP
    c2 = 1.0 - ADAM_B2 ** ADAM_STEP

    def body(*refs):
        p_refs, (w_ref, m_ref, v_ref, g_ref, d_ref, nm_ref, nv_ref) = refs[:len(pieces)], refs[len(pieces):]
        at = pl.program_id(0) * nr + pl.program_id(1)
        for j in range(len(pieces)):
            @pl.when(jnp.logical_and(at >= j * nrp, at < (j + 1) * nrp))
            def _(p_ref=p_refs[j]):
                g = p_ref[0].astype(F32)
                for i in range(1, N_DEV):
                    g = g + p_ref[i].astype(F32)
                nm = ADAM_B1 * m_ref[...] + (1.0 - ADAM_B1) * g
                nv = ADAM_B2 * v_ref[...] + (1.0 - ADAM_B2) * (g * g)
                g_ref[...] = g
                nm_ref[...] = nm
                nv_ref[...] = nv
                d_ref[...] = -ADAM_LR * ((nm / c1) / (jnp.sqrt(nv / c2) + ADAM_EPS) + ADAM_WD * w_ref[...])

    def part_spec(j):
        return pl.BlockSpec((N_DEV, tr, c), lambda ll, i: (0, jnp.clip(ll * nr + i - j * nrp, 0, nrp - 1), 0))

    blk = lambda: pl.BlockSpec((None, tr, c), lambda ll, i: (ll, i, 0))
    return _call(body, name, (nl, nr), [part_spec(j) for j in range(len(pieces))] + [blk(), blk(), blk()],
                 [blk(), blk(), blk(), blk()], [jax.ShapeDtypeStruct((nl, r, c), F32)] * 4, [],
                 ("arbitrary", "arbitrary"), (*pieces, w, m, v), rider)


REPLICATED = ("norm_g", "ret_norm_g", "gla_ba_f", "gla_ba_b", "gla_norm_g", "pool_w", "pool_scale",
              "mla_q_norm_g", "mla_kv_norm_g", "mla_qk_norm_q", "mla_qk_norm_k")
REPLICATED_EARLY = REPLICATED[1:]
SMALL_SHARDED = ("mla_wq_b", "mla_wkv_b", "gla_wa2_f", "gla_wa2_b")
WEIGHTS = ("norm_g", "w_in", "ret_norm_g", "gla_wa2_f", "gla_ba_f", "gla_wa2_b", "gla_ba_b", "gla_norm_g", "pool_w",
           "pool_scale", "mla_q_norm_g", "mla_wq_b", "mla_kv_norm_g", "mla_wkv_b", "mla_qk_norm_q", "mla_qk_norm_k",
           "w_out")


def _pack(arrays, dtype):
    flat = jnp.concatenate([a.reshape(-1) for a in arrays]).astype(dtype)
    return flat.reshape(-1, LANES)


def _unpack(packed, like):
    flat = packed.reshape(-1)
    out, at = [], 0
    for a in like:
        out.append(flat[at:at + a.size].reshape(a.shape))
        at += a.size
    return out


def _columns_by_device(g):
    l, r, n = g.shape
    return g.reshape(l, r, N_DEV, n // N_DEV).transpose(2, 0, 1, 3)


def _gathered_columns(g, l, r, c):
    return g.reshape(N_DEV, l, r, c).transpose(1, 2, 0, 3).reshape(l, r, N_DEV * c)


def _layer_forward(x, wts, late_wts, tables, tm, tq, ride_inproj=None, ride_attn=None, target=None):
    cos_r, sin_r, cos_m, sin_m, tab, _ = tables
    z, h, *carried_in = _inproj(x, wts["norm_g"], wts["w_in"], min(x.shape[0], 2 * tm), rider=ride_inproj)
    wts.update(late_wts(carried_in))
    o_a, y_a = _ret_fwd(z, cos_r, sin_r, tab, wts["ret_norm_g"])
    o_b, y_b = _gla_fwd(z, wts["wa_f"], wts["wa_b"], wts["gla_ba_f"], wts["gla_ba_b"], wts["gla_norm_g"])
    y_c = _pool_fwd(z, wts["pool_w"], wts["pool_scale"])
    q, k, v = _mla_prep(z, cos_m, sin_m, wts["mla_q_norm_g"], wts["wq"], wts["mla_kv_norm_g"], wts["wkv"],
                        wts["qk_q"], wts["qk_k"], tm)
    o_d, y_d, lse, *carried_attn = _attn_fwd(q, k, v, z, tq, rider=ride_attn)
    y = jnp.concatenate([y_a, y_b, y_c, y_d], axis=1)
    w_out = wts["w_out"]
    if target is None:
        x_next = _mm(y, w_out, "nn", "outproj", tm, D_MODEL, 1024, add=x)
    else:
        x_next = _mm(y, w_out, "nn", "outproj_loss", tm, D_MODEL, 1024, tail=_loss_tail(x, target))
    saved = dict(x=x, z=z, h=h, o_a=o_a, o_b=o_b, o_d=o_d, lse=lse, q=q, k=k, v=v, y=y, w_out=w_out)
    return x_next, saved, carried_in, carried_attn


def _layer_backward(dx, sv, wts, tables, tm, tq, rides):
    cos_r, sin_r, cos_m, sin_m, tab, tab_sw = tables
    z = sv["z"]
    g = {}
    carried = {}

    def rider(name):
        return rides[name](g) if name in rides else None

    def landed(name, results, n_own):
        if name in rides:
            carried[name] = list(results[n_own:])
        return results[:n_own]

    g["w_out"] = _mm(sv["y"], dx, "tn", "d_w_out", 2048, 1024, 1024, out_dtype=BF16)
    dy = _mm(dx, sv["w_out"], "nt", "d_y", tm, 2048, 1024)

    do_a, dg_a, g["ret_norm_g"] = _normgate_bwd(sv["o_a"], z, A_G, dy, 0, wts["ret_norm_g"], tm)
    dq_a, dk_a, dv_a = landed("ret", _ret_bwd(z, do_a, cos_r, sin_r, tab, tab_sw, rider=rider("ret")), 3)

    do_b, dg_b, g["gla_norm_g"] = _normgate_bwd(sv["o_b"], z, B_G, dy, 1, wts["gla_norm_g"], tm)
    dq_b, dk_b, dv_b, d_ga, d_waf, d_wab, g["gla_ba_f"], g["gla_ba_b"] = landed("gla", _gla_bwd(
        z, do_b, wts["wa_f"], wts["wa_b"], wts["gla_ba_f"], wts["gla_ba_b"], rider=rider("gla")), 8)
    g["gla_wa2_f"] = d_waf[0:GLA_RANK]
    g["gla_wa2_b"] = d_wab[GLA_RANK:2 * GLA_RANK]

    du_c, dg_c, g["pool_w"], g["pool_scale"] = _pool_bwd(z, dy, wts["pool_w"], wts["pool_scale"])

    d_q, d_k, d_v, dg_d = landed("attn", _attn_bwd(sv["q"], sv["k"], sv["v"], z, sv["o_d"], sv["lse"], dy, tq,
                                                   rider=rider("attn")), 4)
    (d_mq, d_mkv, d_mkr, d_wq, g["mla_wkv_b"], g["mla_q_norm_g"], g["mla_kv_norm_g"], d_qg, d_kg) = _mla_prep_bwd(
        z, cos_m, sin_m, wts["mla_q_norm_g"], wts["wq"], wts["mla_kv_norm_g"], wts["wkv"], wts["qk_q"], wts["qk_k"],
        d_q, d_k, d_v, tm)
    g["mla_wq_b"] = _unpad_wq(d_wq)
    g["mla_qk_norm_q"] = d_qg[:, _QK_INV]
    g["mla_qk_norm_k"] = d_kg[:, _QK_INV]

    dz = jnp.concatenate([dq_a, dk_a, dv_a, dg_a, dq_b, dk_b, dv_b, dg_b, d_mq, du_c, dg_c, dg_d, d_mkv,
                          d_ga.astype(BF16), d_mkr], axis=1)
    h, half = sv["h"], D_MODEL // 2
    for name, cols in (("d_w_in_a", h[:, :half]), ("d_w_in_b", h[:, half:])):
        res = _mm(dz, cols, "tn", name, 2048, 1024, 1024, out_dtype=BF16, rider=rider(name))
        (d_wt,) = landed(name, res if name in rides else [res], 1)
        g["w_in" + name[-2:]] = _split_w_in(d_wt)
    dx_in, g["norm_g"] = landed("d_h", _mm(dz, wts["w_in"], "nn", "d_h", tm, D_MODEL, 1024, rider=rider("d_h"),
                                           tail=_norm_bwd_tail(sv["x"], wts["norm_g"], dx)), 2)
    return dx_in, g, carried


def kernel(x, norm_g, w_in, ret_norm_g, gla_wa2_f, gla_ba_f, gla_wa2_b, gla_ba_b, gla_norm_g, pool_w, pool_scale, mla_q_norm_g, mla_wq_b, mla_kv_norm_g, mla_wkv_b, mla_qk_norm_q, mla_qk_norm_k, w_out, loss_target, m_norm_g, m_w_in, m_ret_norm_g, m_gla_wa2_f, m_gla_ba_f, m_gla_wa2_b, m_gla_ba_b, m_gla_norm_g, m_pool_w, m_pool_scale, m_mla_q_norm_g, m_mla_wq_b, m_mla_kv_norm_g, m_mla_wkv_b, m_mla_qk_norm_q, m_mla_qk_norm_k, m_w_out, v_norm_g, v_w_in, v_ret_norm_g, v_gla_wa2_f, v_gla_ba_f, v_gla_wa2_b, v_gla_ba_b, v_gla_norm_g, v_pool_w, v_pool_scale, v_mla_q_norm_g, v_mla_wq_b, v_mla_kv_norm_g, v_mla_wkv_b, v_mla_qk_norm_q, v_mla_qk_norm_k, v_w_out):
    w = dict(norm_g=norm_g, w_in=w_in, ret_norm_g=ret_norm_g, gla_wa2_f=gla_wa2_f, gla_ba_f=gla_ba_f,
             gla_wa2_b=gla_wa2_b, gla_ba_b=gla_ba_b, gla_norm_g=gla_norm_g, pool_w=pool_w, pool_scale=pool_scale,
             mla_q_norm_g=mla_q_norm_g, mla_wq_b=mla_wq_b, mla_kv_norm_g=mla_kv_norm_g, mla_wkv_b=mla_wkv_b,
             mla_qk_norm_q=mla_qk_norm_q, mla_qk_norm_k=mla_qk_norm_k, w_out=w_out)
    m = dict(norm_g=m_norm_g, w_in=m_w_in, ret_norm_g=m_ret_norm_g, gla_wa2_f=m_gla_wa2_f, gla_ba_f=m_gla_ba_f,
             gla_wa2_b=m_gla_wa2_b, gla_ba_b=m_gla_ba_b, gla_norm_g=m_gla_norm_g, pool_w=m_pool_w,
             pool_scale=m_pool_scale, mla_q_norm_g=m_mla_q_norm_g, mla_wq_b=m_mla_wq_b, mla_kv_norm_g=m_mla_kv_norm_g,
             mla_wkv_b=m_mla_wkv_b, mla_qk_norm_q=m_mla_qk_norm_q, mla_qk_norm_k=m_mla_qk_norm_k, w_out=m_w_out)
    v = dict(norm_g=v_norm_g, w_in=v_w_in, ret_norm_g=v_ret_norm_g, gla_wa2_f=v_gla_wa2_f, gla_ba_f=v_gla_ba_f,
             gla_wa2_b=v_gla_wa2_b, gla_ba_b=v_gla_ba_b, gla_norm_g=v_gla_norm_g, pool_w=v_pool_w,
             pool_scale=v_pool_scale, mla_q_norm_g=v_mla_q_norm_g, mla_wq_b=v_mla_wq_b, mla_kv_norm_g=v_mla_kv_norm_g,
             mla_wkv_b=v_mla_wkv_b, mla_qk_norm_q=v_mla_qk_norm_q, mla_qk_norm_k=v_mla_qk_norm_k, w_out=v_w_out)
    xs, target = x[0], loss_target[0]
    s = xs.shape[0]
    tm, tq = min(s, 512), min(s, 256)
    c_in = w_in.shape[2]

    w_in_b = jnp.transpose(w_in, (2, 0, 1)).astype(BF16)
    w_out_b = w_out.astype(BF16).reshape(-1, D_MODEL)
    (w_in_g0,) = _exchange([("gather", w_in_b[:, 0])], "gather_first")
    tables = _rope_tables(s) + _ret_tables()
    offs = np.cumsum([0] + [w[n].size for n in SMALL_SHARDED])

    def early_weights(l, w_in_g):
        return dict(
            norm_g=norm_g[l][None], w_in=_assemble_w_in(w_in_g), ret_norm_g=ret_norm_g[l][None],
            gla_ba_f=gla_ba_f[l][None], gla_ba_b=gla_ba_b[l][None],
            gla_norm_g=gla_norm_g[l][None], pool_w=pool_w[l], pool_scale=pool_scale[l][None],
            mla_q_norm_g=mla_q_norm_g[l][None], mla_kv_norm_g=mla_kv_norm_g[l][None],
            qk_q=_pad_qk_gain(mla_qk_norm_q[l]), qk_k=_pad_qk_gain(mla_qk_norm_k[l]))

    def late_weights(l, w_out_g, small_g):
        flat = small_g.reshape(N_DEV, -1)
        full = {n: _gathered_columns(flat[:, offs[i]:offs[i + 1]], *w[n].shape)[l]
                for i, n in enumerate(SMALL_SHARDED)}
        wa_f = jnp.zeros((LANES, 2 * LANES), BF16).at[0:GLA_RANK].set(full["gla_wa2_f"])
        wa_b = jnp.zeros((LANES, 2 * LANES), BF16).at[GLA_RANK:2 * GLA_RANK].set(full["gla_wa2_b"])
        return dict(w_out=w_out_g.reshape(N_DEV, DEPTH, -1, D_MODEL)[:, l].reshape(-1, D_MODEL),
                    wa_f=wa_f, wa_b=wa_b, wq=_pad_wq(full["mla_wq_b"]), wkv=full["mla_wkv_b"])

    by_owner = lambda g_w_out: g_w_out.reshape(N_DEV, -1, D_MODEL)

    layers = [early_weights(0, w_in_g0), None]
    x1, sv0, (w_out_g, small_g), (w_in_g1,) = _layer_forward(
        xs, layers[0], lambda got: late_weights(0, *got), tables, tm, tq,
        ride_inproj=[("gather", w_out_b), ("gather", _pack([w[n] for n in SMALL_SHARDED], BF16))],
        ride_attn=("gather", w_in_b[:, 1]))
    layers[1] = early_weights(1, w_in_g1)
    (dx, loss_row), sv1, _, _ = _layer_forward(x1, layers[1], lambda got: late_weights(1, w_out_g, small_g), tables,
                                               tm, tq, target=target)
    loss = lax.psum(loss_row[0, 0], ("x", "y", "c"))

    def small_jobs(g):
        grads = (g, g1)
        full = {n: jnp.stack([grads[l][n].reshape(w[n].shape[1:]) if n in REPLICATED else grads[l][n]
                              for l in range(DEPTH)]) for n in SMALL_SHARDED + REPLICATED_EARLY}
        small_c = jnp.concatenate([_columns_by_device(full[n]).reshape(N_DEV, -1) for n in SMALL_SHARDED], axis=1)
        return [("scatter", small_c.reshape(N_DEV, -1, LANES)),
                ("gather", _pack([full[n] for n in REPLICATED_EARLY], F32))]

    dx, g1, got1 = _layer_backward(dx, sv1, layers[1], tables, tm, tq, {
        "attn": lambda g: ("scatter", by_owner(g["w_out"]))})
    dx, g0, got0 = _layer_backward(dx, sv0, layers[0], tables, tm, tq, {
        "ret": lambda g: ("scatter", by_owner(g["w_out"])),
        "gla": lambda g: ("scatter", g1["w_in_b"]),
        "attn": lambda g: ("scatter", g1["w_in_a"]),
        "d_w_in_a": small_jobs,
        "d_w_in_b": lambda g: ("scatter", g["w_in_a"]),
        "d_h": lambda g: ("scatter", g["w_in_b"])})
    in_parts = ((got0["d_w_in_b"][0], got0["d_h"][0]), (got0["attn"][0], got0["gla"][0]))
    out_parts = ((got0["ret"][0],), (got1["attn"][0],))
    small_parts, rep_parts = got0["d_w_in_a"]
    norm_pack = _pack([jnp.stack([g0["norm_g"][0], g1["norm_g"][0]])], F32)

    out = {}
    out["w_in"] = _adam_columns(in_parts, w_in, m_w_in, v_w_in, "adam_w_in", 256)
    *out["w_out"], norm_parts = _adam_layers(out_parts, w_out, m_w_out, v_w_out, "adam_w_out", 128,
                                             rider=("gather", norm_pack))
    for names, parts, label in ((SMALL_SHARDED, small_parts, "adam_small"),
                                (REPLICATED_EARLY, rep_parts, "adam_replicated"), (("norm_g",), norm_parts, "adam_norm")):
        res = _adam(parts, _pack([w[n] for n in names], F32), _pack([m[n] for n in names], F32),
                    _pack([v[n] for n in names], F32), label, 2048)
        for n, *vals in zip(names, *[_unpack(a, [w[n] for n in names]) for a in res]):
            out[n] = vals

    return (loss, dx[None], *[out[n][0] for n in WEIGHTS], *[out[n][1] for n in WEIGHTS],
            *[out[n][2] for n in WEIGHTS], *[out[n][3] for n in WEIGHTS])
```

```python
import functools
import math

import numpy as np
import jax
import jax.numpy as jnp
from jax import lax
from jax.experimental import pallas as pl
from jax.experimental.pallas import tpu as pltpu

F32 = jnp.float32
BF16 = jnp.bfloat16

N_DEV = 8
D_MODEL = 2048
DEPTH = 2
GROUP_W = 512
EPS = 1e-6
ROPE_THETA = 10000.0
LANES = 128

RET_HD = 128
RET_CHUNK = 256
RET_UNROLL = 4
GLA_CHUNK = 64
GLA_UNROLL = 8
GLA_CUM_ROWS = 256
GLA_DK = 64
GLA_TAU = 16.0
GLA_RANK = 16
POOL_WINDOWS = (2, 4, 8, 16)
MLA_QK = 192
MLA_ROPE = 64
ATTN_SCALE = MLA_QK ** -0.5
ATTN_Q_SCALE = ATTN_SCALE * math.log2(math.e)
IN_COLS = 5984

ADAM_LR = 0.001
ADAM_B1 = 0.9
ADAM_B2 = 0.999
ADAM_EPS = 1e-08
ADAM_WD = 0.01
ADAM_STEP = 10

A_Q, A_K, A_V, A_G = 0, 4, 8, 12
B_Q, B_K, B_V, B_G = 16, 18, 20, 24
M_Q, C_V, C_G, M_G = 28, 32, 36, 40
M_KV, GA, M_KR = 44, 46, 47
ZP_COLS = 48 * LANES

VMEM_LIMIT = 56 * 1024 * 1024


def _params(sem, vmem=VMEM_LIMIT):
    return pltpu.CompilerParams(dimension_semantics=sem, vmem_limit_bytes=vmem)


def _sigmoid(x):
    return 1.0 / (1.0 + jnp.exp(-x))


def _silu(x):
    return x * _sigmoid(x)


def _silu_grad(x):
    s = _sigmoid(x)
    return s * (1.0 + x * (1.0 - s))


def _dot(a, b, dims=(((1,), (0,)), ((), ()))):
    return lax.dot_general(a, b, dims, preferred_element_type=F32)


NT = (((1,), (1,)), ((), ()))
TN = (((0,), (0,)), ((), ()))


def _chunk_loop(n, body, init, unroll):
    unroll = math.gcd(n, unroll)

    def trip(t, carry):
        for u in range(unroll):
            carry = body(t * unroll + u, carry)
        return carry

    return lax.fori_loop(0, n // unroll, trip, init)


def _roll_lanes_half(x):
    return pltpu.roll(x, 64, 1)


def _wq_perm():
    idx = np.zeros((1024,), np.int32)
    ok = np.zeros((1024,), bool)
    for h in range(4):
        idx[128 * h:128 * h + 128] = 192 * h + np.arange(128)
        ok[128 * h:128 * h + 128] = True
        base = 512 + 128 * h
        idx[base:base + 32] = 192 * h + 128 + np.arange(32)
        ok[base:base + 32] = True
        idx[base + 64:base + 96] = 192 * h + 160 + np.arange(32)
        ok[base + 64:base + 96] = True
    inv = np.zeros((768,), np.int32)
    inv[idx[ok]] = np.nonzero(ok)[0]
    return idx, ok, inv


_WQ_IDX, _WQ_OK, _WQ_INV = _wq_perm()


def _pad_wq(wq):
    return jnp.where(jnp.asarray(_WQ_OK)[None, :], wq[:, _WQ_IDX], 0).astype(wq.dtype)


def _unpad_wq(wqp):
    return wqp[:, _WQ_INV]


def _qk_idx():
    idx = np.zeros((256,), np.int32)
    ok = np.zeros((256,), bool)
    idx[0:128] = np.arange(128)
    ok[0:128] = True
    idx[128:160] = 128 + np.arange(32)
    ok[128:160] = True
    idx[192:224] = 160 + np.arange(32)
    ok[192:224] = True
    inv = np.zeros((192,), np.int32)
    inv[idx[ok]] = np.nonzero(ok)[0]
    return idx, ok, inv


_QK_IDX, _QK_OK, _QK_INV = _qk_idx()


def _pad_qk_gain(g):
    return jnp.where(jnp.asarray(_QK_OK), g[_QK_IDX], 0.0).reshape(1, 256)


def _rope_tables(s):
    def tabs(dim):
        inv = 1.0 / (ROPE_THETA ** (jnp.arange(0, dim, 2, dtype=F32) / dim))
        ang = jnp.arange(s, dtype=F32)[:, None] * inv[None, :]
        return jnp.cos(ang), jnp.sin(ang)
    cr, sr = tabs(RET_HD)
    cos_r = jnp.concatenate([cr, cr], axis=1)
    sin_r = jnp.concatenate([-sr, sr], axis=1)
    cm, sm = tabs(MLA_ROPE)
    zz = jnp.zeros_like(cm)
    cos_m = jnp.concatenate([cm, zz, cm, zz], axis=1)
    sin_m = jnp.concatenate([-sm, zz, sm, zz], axis=1)
    return cos_r, sin_r, cos_m, sin_m


def _rope(x, cos, sin):
    return x * cos + _roll_lanes_half(x) * sin


def _rope_t(x, cos, sin):
    return x * cos + _roll_lanes_half(x * sin)


def _ret_tables():
    c = RET_CHUNK
    gamma_f = 1.0 - 2.0 ** (-5.0 - jnp.arange(4, dtype=F32))
    gamma_b = gamma_f[::-1]
    idx = jnp.arange(c, dtype=F32)
    diff = idx[:, None] - idx[None, :]

    def build(g1, g2):
        l1 = jnp.log(g1)[:, None, None]
        l2 = jnp.log(g2)[:, None, None]
        d1 = jnp.where(diff >= 0, jnp.exp(jnp.maximum(diff, 0.0)[None] * l1), 0.0)
        d2 = jnp.where(diff <= 0, jnp.exp(jnp.maximum(-diff, 0.0)[None] * l2), 0.0)
        ones = jnp.ones((1, c, LANES), F32)
        col = idx[None, :, None]
        qdf = jnp.exp((col + 1.0) * l1) * ones
        kdf = jnp.exp((c - 1.0 - col) * l1) * ones
        qdb = jnp.exp((c - col) * l2) * ones
        kdb = jnp.exp(col * l2) * ones
        cd1 = jnp.exp(c * l1) * ones
        cd2 = jnp.exp(c * l2) * ones
        return jnp.concatenate([d1 + d2, qdf, kdf, qdb, kdb, cd1, cd2], axis=2)

    return build(gamma_f, gamma_b), build(gamma_b, gamma_f)


MESH = pl.DeviceIdType.MESH
ANY = pl.BlockSpec(memory_space=pl.ANY)
_RELATIONS = ((0, 0, 1), (1, 0, 0), (0, 1, 0), (1, 1, 0), (1, 0, 1), (0, 1, 1), (1, 1, 1))


def _position():
    return lax.axis_index("x"), lax.axis_index("y"), lax.axis_index("c")


def _gather_copies(x_ref, out_ref, send_sems, recv_sems, local_sem, starting):
    x, y, cc = _position()
    me, sibling = (x, y, cc), (x, y, 1 - cc)
    chips = [(1 - x, y), (x, 1 - y), (1 - x, 1 - y)]

    def slab(px, py, pc):
        return out_ref.at[4 * px + 2 * py + pc]

    def copy(k, block, to, src=None):
        return pltpu.make_async_remote_copy(
            src_ref=slab(*block) if src is None else src, dst_ref=slab(*block),
            send_sem=send_sems.at[k], recv_sem=recv_sems.at[k], device_id=to, device_id_type=MESH)

    mine = pltpu.make_async_copy(x_ref, slab(*me), local_sem)
    first = [copy(0, me, sibling, src=x_ref)] + [copy(1 + j, me, (*chip, cc), src=x_ref) for j, chip in enumerate(chips)]
    if starting:
        return mine, first
    passed = [copy(4 + j, (*chip, cc), sibling) for j, chip in enumerate(chips)]
    arrivals = [copy(1 + j, (*chip, cc), me) for j, chip in enumerate(chips)]
    late = [copy(0, sibling, me)] + [copy(4 + j, (*chip, 1 - cc), me) for j, chip in enumerate(chips)]
    return mine, first, passed, arrivals, late


def _gather_start(*refs):
    mine, first = _gather_copies(*refs, starting=True)
    mine.start()
    for cp in first:
        cp.start()


def _gather_finish(*refs):
    mine, first, passed, arrivals, late = _gather_copies(*refs, starting=False)
    for arrived, onward in zip(arrivals, passed):
        arrived.wait_recv()
        onward.start()
    for cp in late:
        cp.wait_recv()
    for cp in first + passed:
        cp.wait_send()
    mine.wait()


def _scatter_copies(c_ref, out_ref, send_sems, recv_sems, local_sem):
    x, y, cc = _position()
    me = 4 * x + 2 * y + cc
    mine = pltpu.make_async_copy(c_ref.at[me], out_ref.at[me], local_sem)
    copies = []
    for k, (fx, fy, fc) in enumerate(_RELATIONS):
        px = 1 - x if fx else x
        py = 1 - y if fy else y
        pc = 1 - cc if fc else cc
        copies.append(pltpu.make_async_remote_copy(
            src_ref=c_ref.at[4 * px + 2 * py + pc], dst_ref=out_ref.at[me],
            send_sem=send_sems.at[k], recv_sem=recv_sems.at[k], device_id=(px, py, pc), device_id_type=MESH))
    return mine, copies


def _scatter_start(*refs):
    mine, copies = _scatter_copies(*refs)
    mine.start()
    for cp in copies:
        cp.start()


def _scatter_finish(*refs):
    mine, copies = _scatter_copies(*refs)
    for cp in copies:
        cp.wait()
    mine.wait()


_EXCHANGES = {"gather": (_gather_start, _gather_finish), "scatter": (_scatter_start, _scatter_finish)}


def _exchange_scratch():
    return [pltpu.SemaphoreType.DMA((7,)), pltpu.SemaphoreType.DMA((7,)), pltpu.SemaphoreType.DMA]


def _exchange_out(kind, src):
    return jax.ShapeDtypeStruct(((N_DEV,) + src.shape) if kind == "gather" else src.shape, src.dtype)


def _exchange(jobs, name):
    n = len(jobs)

    def body(*refs):
        srcs, outs, sems = refs[:n], refs[n:2 * n], refs[2 * n:]
        for half in (0, 1):
            for i, (kind, _) in enumerate(jobs):
                _EXCHANGES[kind][half](srcs[i], outs[i], *sems[3 * i:3 * i + 3])

    return pl.pallas_call(
        body, name=name, out_shape=[_exchange_out(kind, src) for kind, src in jobs],
        in_specs=[ANY] * n, out_specs=[ANY] * n,
        scratch_shapes=[sem for _ in jobs for sem in _exchange_scratch()])(*[src for _, src in jobs])


def _call(body, name, grid, in_specs, out_specs, out_shape, scratch, sem, args, rider=None):
    if rider is None:
        return pl.pallas_call(body, name=name, grid=grid, in_specs=in_specs, out_specs=out_specs, out_shape=out_shape,
                              scratch_shapes=scratch, compiler_params=_params(sem))(*args)
    jobs = rider if isinstance(rider, list) else [rider]
    ni, no, ns, nj = len(in_specs), len(out_specs), len(scratch), len(jobs)

    def carried(*refs):
        ins, rsrcs = refs[:ni], refs[ni:ni + nj]
        outs, routs = refs[ni + nj:ni + nj + no], refs[ni + nj + no:ni + 2 * nj + no]
        scr, sems = refs[ni + 2 * nj + no:ni + 2 * nj + no + ns], refs[ni + 2 * nj + no + ns:]
        ids = [pl.program_id(a) for a in range(len(grid))]
        is_first = functools.reduce(jnp.logical_and, [i == 0 for i in ids])
        is_last = functools.reduce(jnp.logical_and, [i == g - 1 for i, g in zip(ids, grid)])

        def half(which):
            for j, (kind, _) in enumerate(jobs):
                _EXCHANGES[kind][which](rsrcs[j], routs[j], *sems[3 * j:3 * j + 3])

        @pl.when(is_first)
        def _():
            half(0)

        body(*ins, *outs, *scr)

        @pl.when(is_last)
        def _():
            half(1)

    return pl.pallas_call(
        carried, name=name, grid=grid, in_specs=list(in_specs) + [ANY] * nj, out_specs=list(out_specs) + [ANY] * nj,
        out_shape=list(out_shape) + [_exchange_out(kind, src) for kind, src in jobs],
        scratch_shapes=list(scratch) + [sem for _ in jobs for sem in _exchange_scratch()],
        compiler_params=_params(("arbitrary",) * len(grid)))(*args, *[src for _, src in jobs])


def _inproj(x, g, wt, tm, tn=512, rider=None):
    s, d = x.shape
    n = wt.shape[0]

    def body(x_ref, g_ref, w_ref, z_ref, h_ref, hs):
        @pl.when(pl.program_id(1) == 0)
        def _():
            xv = x_ref[...]
            r = lax.rsqrt(jnp.mean(xv * xv, axis=-1, keepdims=True) + EPS)
            hv = (xv * r * g_ref[...]).astype(BF16)
            hs[...] = hv
            h_ref[...] = hv
        z_ref[...] = _dot(hs[...], w_ref[...], NT)

    return _call(
        body, "inproj", (s // tm, n // tn),
        [pl.BlockSpec((tm, d), lambda i, j: (i, 0)),
         pl.BlockSpec((1, d), lambda i, j: (0, 0)),
         pl.BlockSpec((tn, d), lambda i, j: (j, 0))],
        [pl.BlockSpec((tm, tn), lambda i, j: (i, j)), pl.BlockSpec((tm, d), lambda i, j: (i, 0))],
        [jax.ShapeDtypeStruct((s, n), F32), jax.ShapeDtypeStruct((s, d), BF16)],
        [pltpu.VMEM((tm, d), BF16)], ("parallel", "arbitrary"), (x, g, wt), rider)


def _relayout_plan():
    runs = ((0, 3584, 0), (3584, 3616, GA * LANES), (3616, 4640, C_V * LANES), (4640, 5152, M_Q * LANES),
            (5152, 5408, M_KV * LANES), (5408, 5440, M_KR * LANES), (5440, 5472, M_KR * LANES + 64),
            (5472, 5984, M_G * LANES))
    shard = IN_COLS // N_DEV
    plan = []
    for d in range(N_DEV):
        lo, hi = shard * d, shard * (d + 1)
        for a, b, p in runs:
            s, e = max(a, lo), min(b, hi)
            if s < e:
                plan.append((d, s - lo, p + (s - a), e - s))
    return plan


def _assemble_w_in(g, tc=512):
    _, c, r = g.shape
    tc = min(tc, r)

    def body(g_ref, o_ref):
        o_ref[...] = jnp.zeros_like(o_ref)
        for d, at, to, w in _relayout_plan():
            o_ref[to:to + w, :] = g_ref[d, at:at + w, :]

    return pl.pallas_call(
        body, name="assemble_w_in", grid=(r // tc,),
        in_specs=[pl.BlockSpec((N_DEV, c, tc), lambda i: (0, 0, i))],
        out_specs=pl.BlockSpec((ZP_COLS, tc), lambda i: (0, i)),
        out_shape=jax.ShapeDtypeStruct((ZP_COLS, r), g.dtype),
        compiler_params=_params(("parallel",)),
    )(g)


def _split_w_in(wt, tc=512):
    r = wt.shape[1]
    c = IN_COLS // N_DEV
    tc = min(tc, r)

    def body(w_ref, o_ref):
        for d, at, to, w in _relayout_plan():
            o_ref[d, at:at + w, :] = w_ref[to:to + w, :]

    return pl.pallas_call(
        body, name="split_w_in", grid=(r // tc,),
        in_specs=[pl.BlockSpec((ZP_COLS, tc), lambda i: (0, i))],
        out_specs=pl.BlockSpec((N_DEV, c, tc), lambda i: (0, 0, i)),
        out_shape=jax.ShapeDtypeStruct((N_DEV, c, r), wt.dtype),
        compiler_params=_params(("parallel",)),
    )(wt)


def _mm(a, b, mode, name, tm, tn, tk, add=None, out_dtype=F32, rider=None, tail=None):
    if mode == "tn":
        k, m = a.shape
    else:
        m, k = a.shape
    n = b.shape[0] if mode == "nt" else b.shape[1]
    tm, tn, tk = min(tm, m), min(tn, n), min(tk, k)
    nk = k // tk
    dims = {"nn": (((1,), (0,)), ((), ())), "nt": NT, "tn": TN}[mode]
    if tail is None:
        def plain(acc, i, extra_refs, out_refs):
            out_refs[0][...] = (acc + extra_refs[0][...] if extra_refs else acc).astype(out_dtype)
        tail = ([(add, "tile")] if add is not None else [], [(out_dtype, "tile")], plain)
    extra, outs, fn = tail
    spec = {"tile": pl.BlockSpec((tm, tn), lambda i, j, kk: (i, j)),
            "row": pl.BlockSpec((1, tn), lambda i, j, kk: (0, j)),
            "lanes": pl.BlockSpec((1, LANES), lambda i, j, kk: (0, 0))}
    shape = {"tile": (m, n), "row": (1, n), "lanes": (1, LANES)}
    ne, no = len(extra), len(outs)

    def body(*refs):
        a_ref, b_ref = refs[:2]
        extra_refs, out_refs, acc = refs[2:2 + ne], refs[2 + ne:2 + ne + no], refs[2 + ne + no]
        i, kk = pl.program_id(0), pl.program_id(2)

        @pl.when(kk == 0)
        def _():
            acc[...] = jnp.zeros_like(acc)

        acc[...] += _dot(a_ref[...].astype(BF16), b_ref[...].astype(BF16), dims)

        @pl.when(kk == nk - 1)
        def _():
            fn(acc[...], i, extra_refs, out_refs)

    a_spec = (pl.BlockSpec((tk, tm), lambda i, j, kk: (kk, i)) if mode == "tn"
              else pl.BlockSpec((tm, tk), lambda i, j, kk: (i, kk)))
    b_spec = (pl.BlockSpec((tn, tk), lambda i, j, kk: (j, kk)) if mode == "nt"
              else pl.BlockSpec((tk, tn), lambda i, j, kk: (kk, j)))
    summed = any(kind != "tile" for _, kind in outs)
    res = _call(body, name, (m // tm, n // tn, nk), [a_spec, b_spec] + [spec[kind] for _, kind in extra],
                [spec[kind] for _, kind in outs], [jax.ShapeDtypeStruct(shape[kind], dt) for dt, kind in outs],
                [pltpu.VMEM((tm, tn), F32)], ("arbitrary",) * 3 if summed else ("parallel", "parallel", "arbitrary"),
                [a, b] + [arr for arr, _ in extra], rider)
    return res[0] if (rider is None and no == 1) else res


def _norm_bwd_tail(x, g, dres):
    def fn(dh, i, extra_refs, out_refs):
        x_ref, g_ref, dres_ref = extra_refs
        dx_ref, dg_ref = out_refs

        @pl.when(i == 0)
        def _():
            dg_ref[...] = jnp.zeros_like(dg_ref)

        xv = x_ref[...]
        r = lax.rsqrt(jnp.mean(xv * xv, axis=-1, keepdims=True) + EPS)
        nv = xv * r
        dg_ref[...] += jnp.sum(dh * nv, axis=0, keepdims=True)
        u = dh * g_ref[...]
        dx_ref[...] = dres_ref[...] + r * (u - nv * jnp.mean(u * nv, axis=-1, keepdims=True))

    return [(x, "tile"), (g, "row"), (dres, "tile")], [(F32, "tile"), (F32, "row")], fn


def _loss_tail(x, target):
    d = x.shape[1]

    def fn(acc, i, extra_refs, out_refs):
        x_ref, t_ref = extra_refs
        dx_ref, loss_ref = out_refs

        @pl.when(i == 0)
        def _():
            loss_ref[...] = jnp.zeros_like(loss_ref)

        err = acc + x_ref[...] - t_ref[...]
        dx_ref[...] = err * (1.0 / d)
        per_tok = jnp.mean(err * err, axis=-1, keepdims=True)
        loss_ref[...] += 0.5 * jnp.sum(per_tok, axis=0, keepdims=True)

    return [(x, "tile"), (target, "tile")], [(F32, "tile"), (F32, "lanes")], fn


def _ret_core(q_ref, k_ref, v_ref, tab_ref, out_ref, back_ref, nchunk):
    c = RET_CHUNK

    def rows(n):
        return pl.ds(pl.multiple_of(n * c, c), c)

    zero = jnp.zeros((LANES, LANES), F32)

    def plane(i, n=c):
        return tab_ref[0:n, c + LANES * i:c + LANES * (i + 1)]

    def fwd(n, st):
        r = rows(n)
        q, k, vb = q_ref[r, :], k_ref[r, :], v_ref[r, :].astype(BF16)
        sc = _dot(q.astype(BF16), k.astype(BF16), NT) * tab_ref[:, 0:c]
        o = _dot(sc.astype(BF16), vb)
        o = o + _dot((q * plane(0)).astype(BF16), st.astype(BF16))
        out_ref[r, :] = o
        return st * plane(4, LANES) + _dot((k * plane(1)).astype(BF16), vb, TN)

    def bwd(i, st):
        r = rows(nchunk - 1 - i)
        q, k, vb = q_ref[r, :], k_ref[r, :], v_ref[r, :].astype(BF16)
        back_ref[r, :] = _dot((q * plane(2)).astype(BF16), st.astype(BF16))
        return st * plane(5, LANES) + _dot((k * plane(3)).astype(BF16), vb, TN)

    def both(i, states):
        return fwd(i, states[0]), bwd(i, states[1])

    _chunk_loop(nchunk, both, (zero, zero), RET_UNROLL)
    out_ref[...] += back_ref[...]


def _ret_fwd(z, cos_r, sin_r, tab, norm_g):
    s = z.shape[0]
    nchunk = s // RET_CHUNK
    scale = RET_HD ** -0.5
    col = lambda base: pl.BlockSpec((s, LANES), lambda h: (0, base + h), pipeline_mode=pl.Buffered(1))

    def body(q_ref, k_ref, v_ref, g_ref, cos_ref, sin_ref, tab_ref, ng_ref, o_ref, y_ref, qh, kh, back):
        qh[...] = _rope(q_ref[...], cos_ref[...], sin_ref[...])
        kh[...] = _rope(k_ref[...], cos_ref[...], sin_ref[...]) * scale
        _ret_core(qh, kh, v_ref, tab_ref, o_ref, back, nchunk)
        o = o_ref[...]
        r = lax.rsqrt(jnp.mean(o * o, axis=-1, keepdims=True) + EPS)
        y_ref[...] = (_silu(g_ref[...]) * (o * r * ng_ref[...])).astype(BF16)

    return pl.pallas_call(
        body, name="ret_fwd", grid=(4,),
        in_specs=[col(A_Q), col(A_K), col(A_V), col(A_G),
                  pl.BlockSpec((s, LANES), lambda h: (0, 0), pipeline_mode=pl.Buffered(1)),
                  pl.BlockSpec((s, LANES), lambda h: (0, 0), pipeline_mode=pl.Buffered(1)),
                  pl.BlockSpec((None, RET_CHUNK, RET_CHUNK + 6 * LANES), lambda h: (h, 0, 0)),
                  pl.BlockSpec((1, LANES), lambda h: (0, h))],
        out_specs=[pl.BlockSpec((s, LANES), lambda h: (0, h)), pl.BlockSpec((s, LANES), lambda h: (0, h))],
        out_shape=[jax.ShapeDtypeStruct((s, GROUP_W), F32), jax.ShapeDtypeStruct((s, GROUP_W), BF16)],
        scratch_shapes=[pltpu.VMEM((s, LANES), F32)] * 3,
        compiler_params=_params(("arbitrary",)),
    )(z, z, z, z, cos_r, sin_r, tab, norm_g)


def _ret_bwd(z, d_o, cos_r, sin_r, tab, tab_sw, rider=None):
    s = z.shape[0]
    nchunk = s // RET_CHUNK
    scale = RET_HD ** -0.5
    col = lambda base: pl.BlockSpec((s, LANES), lambda h: (0, base + h), pipeline_mode=pl.Buffered(1))
    whole = lambda: pl.BlockSpec((s, LANES), lambda h: (0, 0), pipeline_mode=pl.Buffered(1))
    tabspec = lambda: pl.BlockSpec((None, RET_CHUNK, RET_CHUNK + 6 * LANES), lambda h: (h, 0, 0))
    outspec = lambda: pl.BlockSpec((s, LANES), lambda h: (0, h))

    def body(q_ref, k_ref, v_ref, do_ref, cos_ref, sin_ref, tab_ref, tsw_ref, dq_ref, dk_ref, dv_ref,
             qh, kh, tmp, back):
        cos, sin = cos_ref[...], sin_ref[...]
        qh[...] = _rope(q_ref[...], cos, sin)
        kh[...] = _rope(k_ref[...], cos, sin) * scale
        _ret_core(kh, qh, do_ref, tsw_ref, tmp, back, nchunk)
        dv_ref[...] = tmp[...].astype(BF16)
        _ret_core(do_ref, v_ref, kh, tab_ref, tmp, back, nchunk)
        dq_ref[...] = _rope_t(tmp[...], cos, sin).astype(BF16)
        _ret_core(v_ref, do_ref, qh, tsw_ref, tmp, back, nchunk)
        dk_ref[...] = _rope_t(tmp[...] * scale, cos, sin).astype(BF16)

    return _call(
        body, "ret_bwd", (4,),
        [col(A_Q), col(A_K), col(A_V),
         pl.BlockSpec((s, LANES), lambda h: (0, h), pipeline_mode=pl.Buffered(1)),
         whole(), whole(), tabspec(), tabspec()],
        [outspec(), outspec(), outspec()],
        [jax.ShapeDtypeStruct((s, GROUP_W), BF16)] * 3,
        [pltpu.VMEM((s, LANES), F32)] * 4,
        ("arbitrary",), (z, z, z, d_o, cos_r, sin_r, tab, tab_sw), rider)


def _normgate_bwd(o, z, gate_blk, dy, dy_blk, norm_g, tm):
    s = o.shape[0]

    def body(o_ref, g_ref, dy_ref, ng_ref, do_ref, dg_ref, dng_ref):
        @pl.when(pl.program_id(0) == 0)
        def _():
            dng_ref[...] = jnp.zeros_like(dng_ref)

        for h in range(4):
            sl = slice(LANES * h, LANES * (h + 1))
            ov, gv, dyv, ng = o_ref[:, sl], g_ref[:, sl], dy_ref[:, sl], ng_ref[:, sl]
            r = lax.rsqrt(jnp.mean(ov * ov, axis=-1, keepdims=True) + EPS)
            on = ov * r
            dn = dyv * _silu(gv)
            u = dn * ng
            do_ref[:, sl] = r * (u - on * jnp.mean(u * on, axis=-1, keepdims=True))
            dg_ref[:, sl] = (dyv * (on * ng) * _silu_grad(gv)).astype(BF16)
            dng_ref[:, sl] += jnp.sum(dn * on, axis=0, keepdims=True)

    return pl.pallas_call(
        body, name="normgate_bwd", grid=(s // tm,),
        in_specs=[pl.BlockSpec((tm, GROUP_W), lambda i: (i, 0)),
                  pl.BlockSpec((tm, GROUP_W), lambda i: (i, gate_blk // 4)),
                  pl.BlockSpec((tm, GROUP_W), lambda i: (i, dy_blk)),
                  pl.BlockSpec((1, GROUP_W), lambda i: (0, 0))],
        out_specs=[pl.BlockSpec((tm, GROUP_W), lambda i: (i, 0)), pl.BlockSpec((tm, GROUP_W), lambda i: (i, 0)),
                   pl.BlockSpec((1, GROUP_W), lambda i: (0, 0))],
        out_shape=[jax.ShapeDtypeStruct((s, GROUP_W), F32), jax.ShapeDtypeStruct((s, GROUP_W), BF16),
                   jax.ShapeDtypeStruct((1, GROUP_W), F32)],
        compiler_params=_params(("arbitrary",)),
    )(o, z, dy, norm_g)


def _log_sigmoid(x):
    return jnp.minimum(x, 0.0) - jnp.log(1.0 + jnp.exp(-jnp.abs(x)))


def _gla_consts():
    c = GLA_CHUNK
    row = lax.broadcasted_iota(jnp.int32, (c, c), 0)
    colm = lax.broadcasted_iota(jnp.int32, (c, c), 1)
    lane = lax.broadcasted_iota(jnp.int32, (1, LANES), 1)
    low = row >= colm
    up = colm >= row
    heads = ((lane < GLA_DK).astype(F32), (lane >= GLA_DK).astype(F32))
    return low, up, heads


def _chunk_running_sum(x, suffix):
    rows = x.shape[0]
    pos = jnp.bitwise_and(lax.broadcasted_iota(jnp.int32, (rows, 1), 0), GLA_CHUNK - 1)
    k = 1
    while k < GLA_CHUNK:
        if suffix:
            x = x + jnp.where(pos < GLA_CHUNK - k, pltpu.roll(x, rows - k, 0), 0.0)
        else:
            x = x + jnp.where(pos >= k, pltpu.roll(x, k, 0), 0.0)
        k *= 2
    return x


def _gla_chunk(cum_ref, d, n):
    c = GLA_CHUNK
    cum = cum_ref[d, pl.ds(pl.multiple_of(n * c, c), c), :]
    last = cum_ref[d, pl.ds(n * c + (c - 1 if d == 0 else 0), 1), :]
    eq = jnp.exp(cum)
    ek = jnp.exp(-cum)
    el = jnp.exp(last - cum)
    dec = jnp.exp(last)
    return eq, ek, el, dec


def _gla_gates(ga_ref, wa_ref, ba_ref, cum_ref, s, suffix):
    rows = min(s, GLA_CUM_ROWS)

    def step(i, carry):
        r = pl.ds(pl.multiple_of(i * rows, rows), rows)
        pre = _dot(ga_ref[r, :].astype(BF16), wa_ref[...].astype(BF16)) + ba_ref[...]
        cum_ref[r, :] = _chunk_running_sum(_log_sigmoid(pre) * (1.0 / GLA_TAU), suffix)
        return carry
    lax.fori_loop(0, s // rows, step, 0)


def _gla_fwd(z, wa_f, wa_b, ba_f, ba_b, norm_g):
    s = z.shape[0]
    c = GLA_CHUNK
    nchunk = s // c
    scale = GLA_DK ** -0.5
    tm = min(s, 512)
    one = pl.Buffered(1)

    def body(q_ref, k_ref, v_ref, ga_ref, g_ref, waf_ref, wab_ref, baf_ref, bab_ref, ng_ref, o_ref, y_ref,
             la_s):
        low, up, heads = _gla_consts()
        _gla_gates(ga_ref, waf_ref, baf_ref, la_s.at[0], s, False)
        _gla_gates(ga_ref, wab_ref, bab_ref, la_s.at[1], s, True)
        for d in range(2):
            tri = (low, up)[d]

            def step(i, states):
                n = i if d == 0 else nchunk - 1 - i
                r = pl.ds(pl.multiple_of(n * c, c), c)
                q = q_ref[r, :] * scale
                k = k_ref[r, :]
                eq, ek, el, dec = _gla_chunk(la_s, d, n)
                qt = q * eq
                ktb = (k * ek).astype(BF16)
                kl = k * el
                new_states = []
                for hh in range(2):
                    cols = slice(LANES * hh, LANES * (hh + 1))
                    vb = v_ref[r, cols].astype(BF16)
                    qm = (qt * heads[hh]).astype(BF16)
                    a = jnp.where(tri, _dot(qm, ktb, NT), 0.0)
                    o = _dot(a.astype(BF16), vb) + _dot(qm, states[hh].astype(BF16), NT)
                    if d == 0:
                        o_ref[r, cols] = o
                    else:
                        o_ref[r, cols] += o
                    new_states.append(states[hh] * dec + _dot(vb, (kl * heads[hh]).astype(BF16), TN))
                return tuple(new_states)

            zero = jnp.zeros((LANES, LANES), F32)
            _chunk_loop(nchunk, step, (zero, zero), GLA_UNROLL)

        def epi(i, carry):
            r = pl.ds(pl.multiple_of(i * tm, tm), tm)
            for hh in range(2):
                cols = slice(LANES * hh, LANES * (hh + 1))
                o = o_ref[r, cols]
                rr = lax.rsqrt(jnp.mean(o * o, axis=-1, keepdims=True) + EPS)
                y_ref[r, cols] = (_silu(g_ref[r, cols]) * (o * rr * ng_ref[:, cols])).astype(BF16)
            return carry

        lax.fori_loop(0, s // tm, epi, 0)

    w2 = 2 * LANES
    return pl.pallas_call(
        body, name="gla_fwd", grid=(2,),
        in_specs=[pl.BlockSpec((s, LANES), lambda p: (0, B_Q + p), pipeline_mode=one),
                  pl.BlockSpec((s, LANES), lambda p: (0, B_K + p), pipeline_mode=one),
                  pl.BlockSpec((s, w2), lambda p: (0, B_V // 2 + p), pipeline_mode=one),
                  pl.BlockSpec((s, LANES), lambda p: (0, GA), pipeline_mode=one),
                  pl.BlockSpec((s, w2), lambda p: (0, B_G // 2 + p), pipeline_mode=one),
                  pl.BlockSpec((LANES, LANES), lambda p: (0, p)),
                  pl.BlockSpec((LANES, LANES), lambda p: (0, p)),
                  pl.BlockSpec((1, LANES), lambda p: (0, p)),
                  pl.BlockSpec((1, LANES), lambda p: (0, p)),
                  pl.BlockSpec((1, w2), lambda p: (0, p))],
        out_specs=[pl.BlockSpec((s, w2), lambda p: (0, p)), pl.BlockSpec((s, w2), lambda p: (0, p))],
        out_shape=[jax.ShapeDtypeStruct((s, GROUP_W), F32), jax.ShapeDtypeStruct((s, GROUP_W), BF16)],
        scratch_shapes=[pltpu.VMEM((2, s, LANES), F32)],
        compiler_params=_params(("arbitrary",)),
    )(z, z, z, z, z, wa_f, wa_b, ba_f, ba_b, norm_g)


def _gla_bwd(z, d_o, wa_f, wa_b, ba_f, ba_b, rider=None):
    s = z.shape[0]
    c = GLA_CHUNK
    nchunk = s // c
    scale = GLA_DK ** -0.5
    tm = min(s, GLA_CUM_ROWS)
    one = pl.Buffered(1)

    def body(q_ref, k_ref, v_ref, ga_ref, do_ref, waf_ref, wab_ref, baf_ref, bab_ref,
             dq_ref, dk_ref, dv_ref, dga_ref, dwaf_ref, dwab_ref, dbaf_ref, dbab_ref,
             la_s, dla_s, stash, dq_s, dk_s, dv_s):
        low, up, heads = _gla_consts()
        rowi = lax.broadcasted_iota(jnp.int32, (c, 1), 0)
        _gla_gates(ga_ref, waf_ref, baf_ref, la_s.at[0], s, False)
        _gla_gates(ga_ref, wab_ref, bab_ref, la_s.at[1], s, True)
        for d in range(2):
            tri = (low, up)[d]
            last_row = (rowi == (c - 1 if d == 0 else 0)).astype(F32)
            order = (lambda i: i) if d == 0 else (lambda i: nchunk - 1 - i)
            zero = jnp.zeros((LANES, LANES), F32)

            def states(i, sts):
                n = order(i)
                r = pl.ds(pl.multiple_of(n * c, c), c)
                k = k_ref[r, :]
                _, _, el, dec = _gla_chunk(la_s, d, n)
                kl = k * el
                new = []
                for hh in range(2):
                    cols = slice(LANES * hh, LANES * (hh + 1))
                    stash[hh, n] = sts[hh]
                    new.append(sts[hh] * dec + _dot(v_ref[r, cols].astype(BF16), (kl * heads[hh]).astype(BF16), TN))
                return tuple(new)

            _chunk_loop(nchunk, states, (zero, zero), GLA_UNROLL)

            def step(i, dsts):
                n = order(nchunk - 1 - i)
                r = pl.ds(pl.multiple_of(n * c, c), c)
                q = q_ref[r, :] * scale
                k = k_ref[r, :]
                eq, ek, el, dec = _gla_chunk(la_s, d, n)
                qt = q * eq
                kt = k * ek
                kl = k * el
                ktb = kt.astype(BF16)
                dqt = jnp.zeros((c, LANES), F32)
                dkt = jnp.zeros((c, LANES), F32)
                dkl = jnp.zeros((c, LANES), F32)
                ddec = jnp.zeros((1, LANES), F32)
                new = []
                for hh in range(2):
                    cols = slice(LANES * hh, LANES * (hh + 1))
                    vb = v_ref[r, cols].astype(BF16)
                    dob = do_ref[r, cols].astype(BF16)
                    qm = (qt * heads[hh]).astype(BF16)
                    a = jnp.where(tri, _dot(qm, ktb, NT), 0.0).astype(BF16)
                    da = jnp.where(tri, _dot(dob, vb, NT), 0.0).astype(BF16)
                    sn = stash[hh, n]
                    dst = dsts[hh]
                    dstb = dst.astype(BF16)
                    dqt = dqt + (_dot(da, ktb) + _dot(dob, sn.astype(BF16))) * heads[hh]
                    dkt = dkt + _dot(da, qm, TN)
                    dv = _dot(a, dob, TN) + _dot((kl * heads[hh]).astype(BF16), dstb, NT)
                    dkl = dkl + _dot(vb, dstb)
                    ddec = ddec + jnp.sum(dst * sn, axis=0, keepdims=True)
                    new.append(dst * dec + _dot(dob, qm, TN))
                    if d == 0:
                        dv_s[r, cols] = dv
                    else:
                        dv_ref[r, cols] = (dv_s[r, cols] + dv).astype(BF16)
                dlast = ddec * dec + jnp.sum(dkl * kl, axis=0, keepdims=True)
                dq = dqt * eq * scale
                dk = dkt * ek + dkl * el
                dcum = dqt * qt - dkt * kt - dkl * kl + last_row * dlast
                dla_s[d, r, :] = dcum
                if d == 0:
                    dq_s[r, :] = dq
                    dk_s[r, :] = dk
                else:
                    dq_ref[r, :] = (dq_s[r, :] + dq).astype(BF16)
                    dk_ref[r, :] = (dk_s[r, :] + dk).astype(BF16)
                return tuple(new)

            _chunk_loop(nchunk, step, (zero, zero), GLA_UNROLL)

        first = pl.program_id(0) == 0
        for d, (wa_ref, ba_ref, dwa_ref, dba_ref) in enumerate(
                ((waf_ref, baf_ref, dwaf_ref, dbaf_ref), (wab_ref, bab_ref, dwab_ref, dbab_ref))):
            dwa_ref[...] = jnp.zeros_like(dwa_ref)
            dba_ref[...] = jnp.zeros_like(dba_ref)

            def gates(i, carry):
                r = pl.ds(pl.multiple_of(i * tm, tm), tm)
                gab = ga_ref[r, :].astype(BF16)
                wab16 = wa_ref[...].astype(BF16)
                pre = _dot(gab, wab16) + ba_ref[...]
                dla = _chunk_running_sum(dla_s[d, r, :], suffix=(d == 0))
                dpre = dla * (1.0 / GLA_TAU) * _sigmoid(-pre)
                dpb = dpre.astype(BF16)
                dwa_ref[...] += _dot(gab, dpb, TN)
                dba_ref[...] += jnp.sum(dpre, axis=0, keepdims=True)
                dga = _dot(dpb, wab16, NT)
                if d == 0:
                    @pl.when(first)
                    def _():
                        dga_ref[r, :] = dga

                    @pl.when(jnp.logical_not(first))
                    def _():
                        dga_ref[r, :] += dga
                else:
                    dga_ref[r, :] += dga
                return carry

            lax.fori_loop(0, s // tm, gates, 0)

    w2 = 2 * LANES
    return _call(
        body, "gla_bwd", (2,),
        [pl.BlockSpec((s, LANES), lambda p: (0, B_Q + p), pipeline_mode=one),
         pl.BlockSpec((s, LANES), lambda p: (0, B_K + p), pipeline_mode=one),
         pl.BlockSpec((s, w2), lambda p: (0, B_V // 2 + p), pipeline_mode=one),
         pl.BlockSpec((s, LANES), lambda p: (0, GA), pipeline_mode=one),
         pl.BlockSpec((s, w2), lambda p: (0, p), pipeline_mode=one),
         pl.BlockSpec((LANES, LANES), lambda p: (0, p)),
         pl.BlockSpec((LANES, LANES), lambda p: (0, p)),
         pl.BlockSpec((1, LANES), lambda p: (0, p)),
         pl.BlockSpec((1, LANES), lambda p: (0, p))],
        [pl.BlockSpec((s, LANES), lambda p: (0, p), pipeline_mode=one),
         pl.BlockSpec((s, LANES), lambda p: (0, p), pipeline_mode=one),
         pl.BlockSpec((s, w2), lambda p: (0, p), pipeline_mode=one),
         pl.BlockSpec((s, LANES), lambda p: (0, 0), pipeline_mode=one),
         pl.BlockSpec((LANES, LANES), lambda p: (0, p)),
         pl.BlockSpec((LANES, LANES), lambda p: (0, p)),
         pl.BlockSpec((1, LANES), lambda p: (0, p)),
         pl.BlockSpec((1, LANES), lambda p: (0, p))],
        [jax.ShapeDtypeStruct((s, w2), BF16), jax.ShapeDtypeStruct((s, w2), BF16),
         jax.ShapeDtypeStruct((s, GROUP_W), BF16), jax.ShapeDtypeStruct((s, LANES), F32),
         jax.ShapeDtypeStruct((LANES, w2), F32), jax.ShapeDtypeStruct((LANES, w2), F32),
         jax.ShapeDtypeStruct((1, w2), F32), jax.ShapeDtypeStruct((1, w2), F32)],
        [pltpu.VMEM((2, s, LANES), F32), pltpu.VMEM((2, s, LANES), F32),
         pltpu.VMEM((2, nchunk, LANES, LANES), F32),
         pltpu.VMEM((s, LANES), F32), pltpu.VMEM((s, LANES), F32), pltpu.VMEM((s, w2), F32)],
        ("arbitrary",), (z, z, z, z, d_o, wa_f, wa_b, ba_f, ba_b), rider)


def _shift_rows(x, d, rowi):
    s = x.shape[0]
    if d == 0:
        return x
    y = pltpu.roll(x, d % s, 0)
    keep = (rowi >= d) if d > 0 else (rowi < s + d)
    return jnp.where(keep, y, 0.0)


def _run_sum(x, m, step, rowi):
    acc, n = x, 1
    while n < m:
        acc = acc + _shift_rows(acc, step * n, rowi)
        n *= 2
    return acc


def _pool_counts(s, w, rowi):
    hi = jnp.minimum(rowi + w // 2, s)
    lo = jnp.maximum(rowi - w // 2, 0)
    return (hi - lo).astype(F32)


def _pooled(u, w, rowi):
    s = u.shape[0]
    win = _shift_rows(_run_sum(u, w // 2, 1, rowi), 1, rowi) + _run_sum(u, w // 2, -1, rowi)
    return win / _pool_counts(s, w, rowi) - u


def _pool_fwd(z, pool_w, pool_scale):
    s = z.shape[0]
    one = pl.Buffered(1)

    def body(u_ref, g_ref, w_ref, sc_ref, y_ref):
        rowi = lax.broadcasted_iota(jnp.int32, (s, 1), 0)
        for g, w in enumerate(POOL_WINDOWS):
            cols = slice(LANES * g, LANES * (g + 1))
            pooled = _pooled(u_ref[:, cols], w, rowi)
            mixed = _dot(pooled.astype(BF16), w_ref[g].astype(BF16))
            y_ref[:, cols] = (_silu(g_ref[:, cols]) * (mixed * sc_ref[:, cols])).astype(BF16)

    return pl.pallas_call(
        body, name="pool_fwd", grid=(1,),
        in_specs=[pl.BlockSpec((s, GROUP_W), lambda i: (0, C_V // 4), pipeline_mode=one),
                  pl.BlockSpec((s, GROUP_W), lambda i: (0, C_G // 4), pipeline_mode=one),
                  pl.BlockSpec((4, LANES, LANES), lambda i: (0, 0, 0)),
                  pl.BlockSpec((1, GROUP_W), lambda i: (0, 0))],
        out_specs=pl.BlockSpec((s, GROUP_W), lambda i: (0, 0), pipeline_mode=one),
        out_shape=jax.ShapeDtypeStruct((s, GROUP_W), BF16),
        compiler_params=_params(("arbitrary",)),
    )(z, z, pool_w, pool_scale)


def _pool_bwd(z, dy, pool_w, pool_scale):
    s = z.shape[0]
    one = pl.Buffered(1)

    def body(u_ref, g_ref, dy_ref, w_ref, sc_ref, du_ref, dg_ref, dw_ref, dsc_ref):
        rowi = lax.broadcasted_iota(jnp.int32, (s, 1), 0)
        for g, w in enumerate(POOL_WINDOWS):
            cols = slice(LANES * g, LANES * (g + 1))
            gate, dyv, sc = g_ref[:, cols], dy_ref[:, cols], sc_ref[:, cols]
            wb = w_ref[g].astype(BF16)
            pooled = _pooled(u_ref[:, cols], w, rowi)
            pb = pooled.astype(BF16)
            mixed = _dot(pb, wb)
            dg_ref[:, cols] = (dyv * (mixed * sc) * _silu_grad(gate)).astype(BF16)
            dt = dyv * _silu(gate)
            dsc_ref[:, cols] = jnp.sum(dt * mixed, axis=0, keepdims=True)
            dmb = (dt * sc).astype(BF16)
            dw_ref[g] = _dot(pb, dmb, TN)
            dpool = _dot(dmb, wb, NT)
            e = dpool / _pool_counts(s, w, rowi)
            du_ref[:, cols] = (_run_sum(e, w // 2, 1, rowi) + _shift_rows(_run_sum(e, w // 2, -1, rowi), -1, rowi)
                               - dpool).astype(BF16)

    return pl.pallas_call(
        body, name="pool_bwd", grid=(1,),
        in_specs=[pl.BlockSpec((s, GROUP_W), lambda i: (0, C_V // 4), pipeline_mode=one),
                  pl.BlockSpec((s, GROUP_W), lambda i: (0, C_G // 4), pipeline_mode=one),
                  pl.BlockSpec((s, GROUP_W), lambda i: (0, 2), pipeline_mode=one),
                  pl.BlockSpec((4, LANES, LANES), lambda i: (0, 0, 0)),
                  pl.BlockSpec((1, GROUP_W), lambda i: (0, 0))],
        out_specs=[pl.BlockSpec((s, GROUP_W), lambda i: (0, 0), pipeline_mode=one),
                   pl.BlockSpec((s, GROUP_W), lambda i: (0, 0), pipeline_mode=one),
                   pl.BlockSpec((4, LANES, LANES), lambda i: (0, 0, 0)),
                   pl.BlockSpec((1, GROUP_W), lambda i: (0, 0))],
        out_shape=[jax.ShapeDtypeStruct((s, GROUP_W), BF16), jax.ShapeDtypeStruct((s, GROUP_W), BF16),
                   jax.ShapeDtypeStruct((4, LANES, LANES), F32), jax.ShapeDtypeStruct((1, GROUP_W), F32)],
        compiler_params=_params(("arbitrary",)),
    )(z, z, dy, pool_w, pool_scale)


def _mla_heads(qf, kv, kpe, qg, kg, cos, sin):
    out = []
    for h in range(4):
        qa = qf[:, LANES * h:LANES * (h + 1)]
        qb = qf[:, 512 + LANES * h:512 + LANES * (h + 1)]
        ka = kv[:, 256 * h:256 * h + LANES]
        rq = lax.rsqrt((jnp.sum(qa * qa, axis=-1, keepdims=True) + jnp.sum(qb * qb, axis=-1, keepdims=True))
                       * (1.0 / MLA_QK) + EPS)
        rk = lax.rsqrt((jnp.sum(ka * ka, axis=-1, keepdims=True) + jnp.sum(kpe * kpe, axis=-1, keepdims=True))
                       * (1.0 / MLA_QK) + EPS)
        out.append((qa, qb, rq, ka, rk))
    return out


def _mla_latents(mq_ref, mkv_ref, gq_ref, gkv_ref, wq_ref, wkv_ref):
    mq = mq_ref[...]
    rq = lax.rsqrt(jnp.mean(mq * mq, axis=-1, keepdims=True) + EPS)
    qn = mq * rq
    qnb = (qn * gq_ref[...]).astype(BF16)
    mkv = mkv_ref[...]
    rk = lax.rsqrt(jnp.mean(mkv * mkv, axis=-1, keepdims=True) + EPS)
    kvn = mkv * rk
    kvnb = (kvn * gkv_ref[...]).astype(BF16)
    qf = _dot(qnb, wq_ref[...])
    kv = _dot(kvnb, wkv_ref[...])
    return qn, rq, qnb, kvn, rk, kvnb, qf, kv


def _mla_prep(z, cos_m, sin_m, gq, wq, gkv, wkv, qg, kg, tm):
    s = z.shape[0]

    def body(mq_ref, mkv_ref, mkr_ref, cos_ref, sin_ref, gq_ref, wq_ref, gkv_ref, wkv_ref, qg_ref, kg_ref,
             q_ref, k_ref, v_ref):
        _, _, _, _, _, _, qf, kv = _mla_latents(mq_ref, mkv_ref, gq_ref, gkv_ref, wq_ref, wkv_ref)
        kpe = mkr_ref[...]
        cos, sin = cos_ref[...], sin_ref[...]
        qg, kg = qg_ref[...], kg_ref[...]
        for h, (qa, qb, rq, ka, rk) in enumerate(_mla_heads(qf, kv, kpe, qg, kg, cos, sin)):
            q_ref[h, :, 0:LANES] = (qa * rq * qg[:, 0:LANES] * ATTN_Q_SCALE).astype(BF16)
            q_ref[h, :, LANES:] = (_rope(qb * rq * qg[:, LANES:], cos, sin) * ATTN_Q_SCALE).astype(BF16)
            k_ref[h, :, 0:LANES] = (ka * rk * kg[:, 0:LANES]).astype(BF16)
            k_ref[h, :, LANES:] = _rope(kpe * rk * kg[:, LANES:], cos, sin).astype(BF16)
            v_ref[h] = kv[:, 256 * h + LANES:256 * (h + 1)].astype(BF16)

    full = lambda shape: pl.BlockSpec(shape, lambda i: (0,) * len(shape))
    return pl.pallas_call(
        body, name="mla_prep", grid=(s // tm,),
        in_specs=[pl.BlockSpec((tm, 512), lambda i: (i, M_Q // 4)),
                  pl.BlockSpec((tm, 256), lambda i: (i, M_KV // 2)),
                  pl.BlockSpec((tm, LANES), lambda i: (i, M_KR)),
                  pl.BlockSpec((tm, LANES), lambda i: (i, 0)),
                  pl.BlockSpec((tm, LANES), lambda i: (i, 0)),
                  full((1, 512)), full((512, 1024)), full((1, 256)), full((256, 1024)), full((1, 256)), full((1, 256))],
        out_specs=[pl.BlockSpec((4, tm, 256), lambda i: (0, i, 0)), pl.BlockSpec((4, tm, 256), lambda i: (0, i, 0)),
                   pl.BlockSpec((4, tm, LANES), lambda i: (0, i, 0))],
        out_shape=[jax.ShapeDtypeStruct((4, s, 256), BF16), jax.ShapeDtypeStruct((4, s, 256), BF16),
                   jax.ShapeDtypeStruct((4, s, LANES), BF16)],
        compiler_params=_params(("parallel",)),
    )(z, z, z, cos_m, sin_m, gq, wq, gkv, wkv, qg, kg)


def _mla_prep_bwd(z, cos_m, sin_m, gq, wq, gkv, wkv, qg, kg, dq, dk, dv, tm):
    s = z.shape[0]

    def body(mq_ref, mkv_ref, mkr_ref, cos_ref, sin_ref, gq_ref, wq_ref, gkv_ref, wkv_ref, qg_ref, kg_ref,
             dq_ref, dk_ref, dv_ref,
             dmq_ref, dmkv_ref, dmkr_ref, dwq_ref, dwkv_ref, dgq_ref, dgkv_ref, dqg_ref, dkg_ref, dqf, dkv):
        @pl.when(pl.program_id(0) == 0)
        def _():
            for r in (dwq_ref, dwkv_ref, dgq_ref, dgkv_ref, dqg_ref, dkg_ref):
                r[...] = jnp.zeros_like(r)

        qn, rq0, qnb, kvn, rk0, kvnb, qf, kv = _mla_latents(mq_ref, mkv_ref, gq_ref, gkv_ref, wq_ref, wkv_ref)
        kpe = mkr_ref[...]
        cos, sin = cos_ref[...], sin_ref[...]
        qg, kg = qg_ref[...], kg_ref[...]
        dkpe = jnp.zeros_like(kpe)
        inv = 1.0 / MLA_QK

        def norm_bwd(a, b, r, da_n, db_n, g):
            ga, gb = g[:, 0:LANES], g[:, LANES:]
            dg_a = jnp.sum(da_n * a * r, axis=0, keepdims=True)
            dg_b = jnp.sum(db_n * b * r, axis=0, keepdims=True)
            ua, ub = da_n * ga, db_n * gb
            dt = (jnp.sum(ua * a, axis=-1, keepdims=True) + jnp.sum(ub * b, axis=-1, keepdims=True)) * inv
            r3 = r * r * r
            return r * ua - a * (r3 * dt), r * ub - b * (r3 * dt), dg_a, dg_b

        for h, (qa, qb, rq, ka, rk) in enumerate(_mla_heads(qf, kv, kpe, qg, kg, cos, sin)):
            dqa, dqb, dga, dgb = norm_bwd(qa, qb, rq, dq_ref[h, :, 0:LANES] * ATTN_SCALE,
                                          _rope_t(dq_ref[h, :, LANES:] * ATTN_SCALE, cos, sin), qg)
            dqf[:, LANES * h:LANES * (h + 1)] = dqa
            dqf[:, 512 + LANES * h:512 + LANES * (h + 1)] = dqb
            dqg_ref[:, 0:LANES] += dga
            dqg_ref[:, LANES:] += dgb
            ln2 = math.log(2.0)
            dka, dkb, dga, dgb = norm_bwd(ka, kpe, rk, dk_ref[h, :, 0:LANES] * ln2,
                                          _rope_t(dk_ref[h, :, LANES:] * ln2, cos, sin), kg)
            dkv[:, 256 * h:256 * h + LANES] = dka
            dkv[:, 256 * h + LANES:256 * (h + 1)] = dv_ref[h]
            dkpe = dkpe + dkb
            dkg_ref[:, 0:LANES] += dga
            dkg_ref[:, LANES:] += dgb
        dmkr_ref[...] = dkpe.astype(BF16)

        def latent_bwd(dfull, w_ref, nb, n, r, g_ref, dw_ref, dg_ref, dlat_ref):
            db = dfull.astype(BF16)
            dn = _dot(db, w_ref[...], NT)
            dw_ref[...] += _dot(nb, db, TN)
            dg_ref[...] += jnp.sum(dn * n, axis=0, keepdims=True)
            u = dn * g_ref[...]
            dlat_ref[...] = (r * (u - n * jnp.mean(u * n, axis=-1, keepdims=True))).astype(BF16)

        latent_bwd(dqf[...], wq_ref, qnb, qn, rq0, gq_ref, dwq_ref, dgq_ref, dmq_ref)
        latent_bwd(dkv[...], wkv_ref, kvnb, kvn, rk0, gkv_ref, dwkv_ref, dgkv_ref, dmkv_ref)

    full = lambda shape: pl.BlockSpec(shape, lambda i: (0,) * len(shape))
    return pl.pallas_call(
        body, name="mla_prep_bwd", grid=(s // tm,),
        in_specs=[pl.BlockSpec((tm, 512), lambda i: (i, M_Q // 4)),
                  pl.BlockSpec((tm, 256), lambda i: (i, M_KV // 2)),
                  pl.BlockSpec((tm, LANES), lambda i: (i, M_KR)),
                  pl.BlockSpec((tm, LANES), lambda i: (i, 0)),
                  pl.BlockSpec((tm, LANES), lambda i: (i, 0)),
                  full((1, 512)), full((512, 1024)), full((1, 256)), full((256, 1024)), full((1, 256)), full((1, 256)),
                  pl.BlockSpec((4, tm, 256), lambda i: (0, i, 0)), pl.BlockSpec((4, tm, 256), lambda i: (0, i, 0)),
                  pl.BlockSpec((4, tm, LANES), lambda i: (0, i, 0))],
        out_specs=[pl.BlockSpec((tm, 512), lambda i: (i, 0)), pl.BlockSpec((tm, 256), lambda i: (i, 0)),
                   pl.BlockSpec((tm, LANES), lambda i: (i, 0)),
                   full((512, 1024)), full((256, 1024)), full((1, 512)), full((1, 256)), full((1, 256)), full((1, 256))],
        out_shape=[jax.ShapeDtypeStruct((s, 512), BF16), jax.ShapeDtypeStruct((s, 256), BF16),
                   jax.ShapeDtypeStruct((s, LANES), BF16),
                   jax.ShapeDtypeStruct((512, 1024), F32), jax.ShapeDtypeStruct((256, 1024), F32),
                   jax.ShapeDtypeStruct((1, 512), F32), jax.ShapeDtypeStruct((1, 256), F32),
                   jax.ShapeDtypeStruct((1, 256), F32), jax.ShapeDtypeStruct((1, 256), F32)],
        scratch_shapes=[pltpu.VMEM((tm, 1024), F32), pltpu.VMEM((tm, 1024), F32)],
        compiler_params=_params(("arbitrary",)),
    )(z, z, z, cos_m, sin_m, gq, wq, gkv, wkv, qg, kg, dq, dk, dv)


def _attn_fwd(q, k, v, z, tq, rider=None):
    s = q.shape[1]

    def body(q_ref, k_ref, v_ref, g_ref, o_ref, y_ref, lse_ref):
        sc = _dot(q_ref[...], k_ref[...], NT)
        m = jnp.max(sc, axis=-1, keepdims=True)
        p = jnp.exp2(sc - m)
        l = jnp.sum(p, axis=-1, keepdims=True)
        o = _dot(p.astype(BF16), v_ref[...]) / l
        o_ref[...] = o
        y_ref[...] = (_silu(g_ref[...]) * o).astype(BF16)
        lse_ref[...] = m + jnp.log2(l)

    return _call(
        body, "attn_fwd", (4, s // tq),
        [pl.BlockSpec((None, tq, 256), lambda h, i: (h, i, 0)),
         pl.BlockSpec((None, s, 256), lambda h, i: (h, 0, 0)),
         pl.BlockSpec((None, s, LANES), lambda h, i: (h, 0, 0)),
         pl.BlockSpec((tq, LANES), lambda h, i: (i, M_G + h))],
        [pl.BlockSpec((tq, LANES), lambda h, i: (i, h)), pl.BlockSpec((tq, LANES), lambda h, i: (i, h)),
         pl.BlockSpec((None, tq, 1), lambda h, i: (h, i, 0))],
        [jax.ShapeDtypeStruct((s, GROUP_W), F32), jax.ShapeDtypeStruct((s, GROUP_W), BF16),
         jax.ShapeDtypeStruct((4, s, 1), F32)],
        [], ("parallel", "parallel"), (q, k, v, z), rider)


def _attn_bwd(q, k, v, z, o, lse, dy, tq, rider=None):
    s = q.shape[1]

    def body(q_ref, k_ref, v_ref, g_ref, o_ref, lse_ref, dy_ref, dq_ref, dk_ref, dv_ref, dg_ref):
        @pl.when(pl.program_id(1) == 0)
        def _():
            dk_ref[...] = jnp.zeros_like(dk_ref)
            dv_ref[...] = jnp.zeros_like(dv_ref)

        gate, ov, dyv = g_ref[...], o_ref[...], dy_ref[...]
        do = dyv * _silu(gate)
        dg_ref[...] = (dyv * ov * _silu_grad(gate)).astype(BF16)
        delta = jnp.sum(do * ov, axis=-1, keepdims=True)
        dob = do.astype(BF16)
        qb, kb = q_ref[...], k_ref[...]
        p = jnp.exp2(_dot(qb, kb, NT) - lse_ref[...])
        dp = _dot(dob, v_ref[...], NT)
        ds = (p * (dp - delta)).astype(BF16)
        dq_ref[...] = _dot(ds, kb)
        dk_ref[...] += _dot(ds, qb, TN)
        dv_ref[...] += _dot(p.astype(BF16), dob, TN)

    return _call(
        body, "attn_bwd", (4, s // tq),
        [pl.BlockSpec((None, tq, 256), lambda h, i: (h, i, 0)),
         pl.BlockSpec((None, s, 256), lambda h, i: (h, 0, 0)),
         pl.BlockSpec((None, s, LANES), lambda h, i: (h, 0, 0)),
         pl.BlockSpec((tq, LANES), lambda h, i: (i, M_G + h)),
         pl.BlockSpec((tq, LANES), lambda h, i: (i, h)),
         pl.BlockSpec((None, tq, 1), lambda h, i: (h, i, 0)),
         pl.BlockSpec((tq, LANES), lambda h, i: (i, 12 + h))],
        [pl.BlockSpec((None, tq, 256), lambda h, i: (h, i, 0)),
         pl.BlockSpec((None, s, 256), lambda h, i: (h, 0, 0)),
         pl.BlockSpec((None, s, LANES), lambda h, i: (h, 0, 0)),
         pl.BlockSpec((tq, LANES), lambda h, i: (i, h))],
        [jax.ShapeDtypeStruct((4, s, 256), F32), jax.ShapeDtypeStruct((4, s, 256), F32),
         jax.ShapeDtypeStruct((4, s, LANES), F32), jax.ShapeDtypeStruct((s, GROUP_W), BF16)],
        [], ("parallel", "arbitrary"), (q, k, v, z, o, lse, dy), rider)


def _adam(parts, w, m, v, name, tr):
    r, c = w.shape
    tr = min(tr, r)
    c1 = 1.0 - ADAM_B1 ** ADAM_STEP
    c2 = 1.0 - ADAM_B2 ** ADAM_STEP

    def body(p_ref, w_ref, m_ref, v_ref, g_ref, d_ref, nm_ref, nv_ref):
        g = p_ref[0].astype(F32)
        for i in range(1, N_DEV):
            g = g + p_ref[i].astype(F32)
        nm = ADAM_B1 * m_ref[...] + (1.0 - ADAM_B1) * g
        nv = ADAM_B2 * v_ref[...] + (1.0 - ADAM_B2) * (g * g)
        g_ref[...] = g
        nm_ref[...] = nm
        nv_ref[...] = nv
        d_ref[...] = -ADAM_LR * ((nm / c1) / (jnp.sqrt(nv / c2) + ADAM_EPS) + ADAM_WD * w_ref[...])

    blk = lambda: pl.BlockSpec((tr, c), lambda i: (i, 0))
    return pl.pallas_call(
        body, name=name, grid=(r // tr,),
        in_specs=[pl.BlockSpec((N_DEV, tr, c), lambda i: (0, i, 0)), blk(), blk(), blk()],
        out_specs=[blk(), blk(), blk(), blk()],
        out_shape=[jax.ShapeDtypeStruct((r, c), F32)] * 4,
        compiler_params=_params(("parallel",)),
    )(parts, w, m, v)


def _adam_columns(parts, w, m, v, name, tc, rider=None):
    nl, r, c = w.shape
    pieces = [p for layer in parts for p in layer]
    nh = len(parts[0])
    rp = pieces[0].shape[2]
    tc = min(tc, rp)
    ncb = rp // tc
    c1 = 1.0 - ADAM_B1 ** ADAM_STEP
    c2 = 1.0 - ADAM_B2 ** ADAM_STEP

    def body(*refs):
        p_refs, (w_ref, m_ref, v_ref, g_ref, d_ref, nm_ref, nv_ref) = refs[:len(pieces)], refs[len(pieces):]
        for h in range(nh):
            @pl.when(pl.program_id(0) == h)
            def _(h=h):
                for l in range(nl):
                    p_ref = p_refs[l * nh + h]
                    g = p_ref[0].astype(F32)
                    for i in range(1, N_DEV):
                        g = g + p_ref[i].astype(F32)
                    nm = ADAM_B1 * m_ref[:, l, :] + (1.0 - ADAM_B1) * g
                    nv = ADAM_B2 * v_ref[:, l, :] + (1.0 - ADAM_B2) * (g * g)
                    g_ref[:, l, :] = g
                    nm_ref[:, l, :] = nm
                    nv_ref[:, l, :] = nv
                    d_ref[:, l, :] = -ADAM_LR * ((nm / c1) / (jnp.sqrt(nv / c2) + ADAM_EPS) + ADAM_WD * w_ref[:, l, :])

    def part_spec(j):
        return pl.BlockSpec((N_DEV, c, tc), lambda h, i: (0, 0, jnp.clip((h - j % nh) * ncb + i, 0, ncb - 1)))

    blk = lambda: pl.BlockSpec((c, nl, tc), lambda h, i: (0, 0, h * ncb + i))
    t = lambda a: jnp.transpose(a, (2, 0, 1))
    *res, = _call(body, name, (nh, ncb), [part_spec(j) for j in range(len(pieces))] + [blk(), blk(), blk()],
                  [blk(), blk(), blk(), blk()], [jax.ShapeDtypeStruct((c, nl, r), F32)] * 4, [],
                  ("arbitrary",) * 2, (*pieces, t(w), t(m), t(v)), rider)
    return [jnp.transpose(a, (1, 2, 0)) for a in res[:4]] + res[4:]


def _adam_layers(parts, w, m, v, name, tr, rider=None):
    nl, r, c = w.shape
    pieces = [p for layer in parts for p in layer]
    rp = pieces[0].shape[1]
    tr = min(tr, rp)
    nr, nrp = r // tr, rp // tr
    c1 = 1.0 - ADAM_B1 ** ADAM_STEP
    c2 = 1.0 - ADAM_B2 ** ADAM_STEP

    def body(*refs):
        p_refs, (w_ref, m_ref, v_ref, g_ref, d_ref, nm_ref, nv_ref) = refs[:len(pieces)], refs[len(pieces):]
        at = pl.program_id(0) * nr + pl.program_id(1)
        for j in range(len(pieces)):
            @pl.when(jnp.logical_and(at >= j * nrp, at < (j + 1) * nrp))
            def _(p_ref=p_refs[j]):
                g = p_ref[0].astype(F32)
                for i in range(1, N_DEV):
                    g = g + p_ref[i].astype(F32)
                nm = ADAM_B1 * m_ref[...] + (1.0 - ADAM_B1) * g
                nv = ADAM_B2 * v_ref[...] + (1.0 - ADAM_B2) * (g * g)
                g_ref[...] = g
                nm_ref[...] = nm
                nv_ref[...] = nv
                d_ref[...] = -ADAM_LR * ((nm / c1) / (jnp.sqrt(nv / c2) + ADAM_EPS) + ADAM_WD * w_ref[...])

    def part_spec(j):
        return pl.BlockSpec((N_DEV, tr, c), lambda ll, i: (0, jnp.clip(ll * nr + i - j * nrp, 0, nrp - 1), 0))

    blk = lambda: pl.BlockSpec((None, tr, c), lambda ll, i: (ll, i, 0))
    return _call(body, name, (nl, nr), [part_spec(j) for j in range(len(pieces))] + [blk(), blk(), blk()],
                 [blk(), blk(), blk(), blk()], [jax.ShapeDtypeStruct((nl, r, c), F32)] * 4, [],
                 ("arbitrary", "arbitrary"), (*pieces, w, m, v), rider)


REPLICATED = ("norm_g", "ret_norm_g", "gla_ba_f", "gla_ba_b", "gla_norm_g", "pool_w", "pool_scale",
              "mla_q_norm_g", "mla_kv_norm_g", "mla_qk_norm_q", "mla_qk_norm_k")
REPLICATED_EARLY = REPLICATED[1:]
SMALL_SHARDED = ("mla_wq_b", "mla_wkv_b", "gla_wa2_f", "gla_wa2_b")
WEIGHTS = ("norm_g", "w_in", "ret_norm_g", "gla_wa2_f", "gla_ba_f", "gla_wa2_b", "gla_ba_b", "gla_norm_g", "pool_w",
           "pool_scale", "mla_q_norm_g", "mla_wq_b", "mla_kv_norm_g", "mla_wkv_b", "mla_qk_norm_q", "mla_qk_norm_k",
           "w_out")


PACK_ROWS = 16


def _packed_rows(a):
    rows = a.size // LANES
    return rows, -(-rows // PACK_ROWS) * PACK_ROWS


def _pack(arrays, dtype):
    parts = []
    for a in arrays:
        rows, padded = _packed_rows(a)
        parts.append(jnp.pad(a.reshape(rows, LANES).astype(dtype), ((0, padded - rows), (0, 0))))
    return jnp.concatenate(parts, axis=0)


def _unpack(packed, like):
    out, at = [], 0
    for a in like:
        rows, padded = _packed_rows(a)
        out.append(packed[..., at:at + rows, :].reshape(packed.shape[:-2] + a.shape))
        at += padded
    return out


def _columns_by_device(g):
    l, r, n = g.shape
    return g.reshape(l, r, N_DEV, n // N_DEV).transpose(2, 0, 1, 3)


def _gathered_columns(g, l, r, c):
    return g.reshape(N_DEV, l, r, c).transpose(1, 2, 0, 3).reshape(l, r, N_DEV * c)


def _layer_forward(x, wts, late_wts, tables, tm, tq, ride_inproj=None, ride_attn=None, target=None):
    cos_r, sin_r, cos_m, sin_m, tab, _ = tables
    z, h, *carried_in = _inproj(x, wts["norm_g"], wts["w_in"], min(x.shape[0], 2 * tm), rider=ride_inproj)
    wts.update(late_wts(carried_in))
    o_a, y_a = _ret_fwd(z, cos_r, sin_r, tab, wts["ret_norm_g"])
    o_b, y_b = _gla_fwd(z, wts["wa_f"], wts["wa_b"], wts["gla_ba_f"], wts["gla_ba_b"], wts["gla_norm_g"])
    y_c = _pool_fwd(z, wts["pool_w"], wts["pool_scale"])
    q, k, v = _mla_prep(z, cos_m, sin_m, wts["mla_q_norm_g"], wts["wq"], wts["mla_kv_norm_g"], wts["wkv"],
                        wts["qk_q"], wts["qk_k"], tm)
    o_d, y_d, lse, *carried_attn = _attn_fwd(q, k, v, z, tq, rider=ride_attn)
    y = jnp.concatenate([y_a, y_b, y_c, y_d], axis=1)
    w_out = wts["w_out"]
    if target is None:
        x_next = _mm(y, w_out, "nn", "outproj", tm, D_MODEL, 1024, add=x)
    else:
        x_next = _mm(y, w_out, "nn", "outproj_loss", tm, D_MODEL, 1024, tail=_loss_tail(x, target))
    saved = dict(x=x, z=z, h=h, o_a=o_a, o_b=o_b, o_d=o_d, lse=lse, q=q, k=k, v=v, y=y, w_out=w_out)
    return x_next, saved, carried_in, carried_attn


def _layer_backward(dx, sv, wts, tables, tm, tq, rides):
    cos_r, sin_r, cos_m, sin_m, tab, tab_sw = tables
    z = sv["z"]
    g = {}
    carried = {}

    def rider(name):
        return rides[name](g) if name in rides else None

    def landed(name, results, n_own):
        if name in rides:
            carried[name] = list(results[n_own:])
        return results[:n_own]

    g["w_out"] = _mm(sv["y"], dx, "tn", "d_w_out", 2048, 1024, 1024, out_dtype=BF16)
    dy = _mm(dx, sv["w_out"], "nt", "d_y", tm, 2048, 1024)

    do_a, dg_a, g["ret_norm_g"] = _normgate_bwd(sv["o_a"], z, A_G, dy, 0, wts["ret_norm_g"], tm)
    dq_a, dk_a, dv_a = landed("ret", _ret_bwd(z, do_a, cos_r, sin_r, tab, tab_sw, rider=rider("ret")), 3)

    do_b, dg_b, g["gla_norm_g"] = _normgate_bwd(sv["o_b"], z, B_G, dy, 1, wts["gla_norm_g"], tm)
    dq_b, dk_b, dv_b, d_ga, d_waf, d_wab, g["gla_ba_f"], g["gla_ba_b"] = landed("gla", _gla_bwd(
        z, do_b, wts["wa_f"], wts["wa_b"], wts["gla_ba_f"], wts["gla_ba_b"], rider=rider("gla")), 8)
    g["gla_wa2_f"] = d_waf[0:GLA_RANK]
    g["gla_wa2_b"] = d_wab[GLA_RANK:2 * GLA_RANK]

    du_c, dg_c, g["pool_w"], g["pool_scale"] = _pool_bwd(z, dy, wts["pool_w"], wts["pool_scale"])

    d_q, d_k, d_v, dg_d = landed("attn", _attn_bwd(sv["q"], sv["k"], sv["v"], z, sv["o_d"], sv["lse"], dy, tq,
                                                   rider=rider("attn")), 4)
    (d_mq, d_mkv, d_mkr, d_wq, g["mla_wkv_b"], g["mla_q_norm_g"], g["mla_kv_norm_g"], d_qg, d_kg) = _mla_prep_bwd(
        z, cos_m, sin_m, wts["mla_q_norm_g"], wts["wq"], wts["mla_kv_norm_g"], wts["wkv"], wts["qk_q"], wts["qk_k"],
        d_q, d_k, d_v, tm)
    g["mla_wq_b"] = _unpad_wq(d_wq)
    g["mla_qk_norm_q"] = d_qg[:, _QK_INV]
    g["mla_qk_norm_k"] = d_kg[:, _QK_INV]

    dz = jnp.concatenate([dq_a, dk_a, dv_a, dg_a, dq_b, dk_b, dv_b, dg_b, d_mq, du_c, dg_c, dg_d, d_mkv,
                          d_ga.astype(BF16), d_mkr], axis=1)
    h, half = sv["h"], D_MODEL // 2
    for name, cols in (("d_w_in_a", h[:, :half]), ("d_w_in_b", h[:, half:])):
        res = _mm(dz, cols, "tn", name, 2048, 1024, 1024, out_dtype=BF16, rider=rider(name))
        (d_wt,) = landed(name, res if name in rides else [res], 1)
        g["w_in" + name[-2:]] = _split_w_in(d_wt)
    dx_in, g["norm_g"] = landed("d_h", _mm(dz, wts["w_in"], "nn", "d_h", tm, D_MODEL, 1024, rider=rider("d_h"),
                                           tail=_norm_bwd_tail(sv["x"], wts["norm_g"], dx)), 2)
    return dx_in, g, carried


def kernel(x, norm_g, w_in, ret_norm_g, gla_wa2_f, gla_ba_f, gla_wa2_b, gla_ba_b, gla_norm_g, pool_w, pool_scale, mla_q_norm_g, mla_wq_b, mla_kv_norm_g, mla_wkv_b, mla_qk_norm_q, mla_qk_norm_k, w_out, loss_target, m_norm_g, m_w_in, m_ret_norm_g, m_gla_wa2_f, m_gla_ba_f, m_gla_wa2_b, m_gla_ba_b, m_gla_norm_g, m_pool_w, m_pool_scale, m_mla_q_norm_g, m_mla_wq_b, m_mla_kv_norm_g, m_mla_wkv_b, m_mla_qk_norm_q, m_mla_qk_norm_k, m_w_out, v_norm_g, v_w_in, v_ret_norm_g, v_gla_wa2_f, v_gla_ba_f, v_gla_wa2_b, v_gla_ba_b, v_gla_norm_g, v_pool_w, v_pool_scale, v_mla_q_norm_g, v_mla_wq_b, v_mla_kv_norm_g, v_mla_wkv_b, v_mla_qk_norm_q, v_mla_qk_norm_k, v_w_out):
    w = dict(norm_g=norm_g, w_in=w_in, ret_norm_g=ret_norm_g, gla_wa2_f=gla_wa2_f, gla_ba_f=gla_ba_f,
             gla_wa2_b=gla_wa2_b, gla_ba_b=gla_ba_b, gla_norm_g=gla_norm_g, pool_w=pool_w, pool_scale=pool_scale,
             mla_q_norm_g=mla_q_norm_g, mla_wq_b=mla_wq_b, mla_kv_norm_g=mla_kv_norm_g, mla_wkv_b=mla_wkv_b,
             mla_qk_norm_q=mla_qk_norm_q, mla_qk_norm_k=mla_qk_norm_k, w_out=w_out)
    m = dict(norm_g=m_norm_g, w_in=m_w_in, ret_norm_g=m_ret_norm_g, gla_wa2_f=m_gla_wa2_f, gla_ba_f=m_gla_ba_f,
             gla_wa2_b=m_gla_wa2_b, gla_ba_b=m_gla_ba_b, gla_norm_g=m_gla_norm_g, pool_w=m_pool_w,
             pool_scale=m_pool_scale, mla_q_norm_g=m_mla_q_norm_g, mla_wq_b=m_mla_wq_b, mla_kv_norm_g=m_mla_kv_norm_g,
             mla_wkv_b=m_mla_wkv_b, mla_qk_norm_q=m_mla_qk_norm_q, mla_qk_norm_k=m_mla_qk_norm_k, w_out=m_w_out)
    v = dict(norm_g=v_norm_g, w_in=v_w_in, ret_norm_g=v_ret_norm_g, gla_wa2_f=v_gla_wa2_f, gla_ba_f=v_gla_ba_f,
             gla_wa2_b=v_gla_wa2_b, gla_ba_b=v_gla_ba_b, gla_norm_g=v_gla_norm_g, pool_w=v_pool_w,
             pool_scale=v_pool_scale, mla_q_norm_g=v_mla_q_norm_g, mla_wq_b=v_mla_wq_b, mla_kv_norm_g=v_mla_kv_norm_g,
             mla_wkv_b=v_mla_wkv_b, mla_qk_norm_q=v_mla_qk_norm_q, mla_qk_norm_k=v_mla_qk_norm_k, w_out=v_w_out)
    xs, target = x[0], loss_target[0]
    s = xs.shape[0]
    tm, tq = min(s, 512), min(s, 256)
    c_in = w_in.shape[2]

    w_in_b = jnp.transpose(w_in, (2, 0, 1)).astype(BF16)
    w_out_b = w_out.astype(BF16).reshape(-1, D_MODEL)
    (w_in_g0,) = _exchange([("gather", w_in_b[:, 0])], "gather_first")
    tables = _rope_tables(s) + _ret_tables()

    def early_weights(l, w_in_g):
        return dict(
            norm_g=norm_g[l][None], w_in=_assemble_w_in(w_in_g), ret_norm_g=ret_norm_g[l][None],
            gla_ba_f=gla_ba_f[l][None], gla_ba_b=gla_ba_b[l][None],
            gla_norm_g=gla_norm_g[l][None], pool_w=pool_w[l], pool_scale=pool_scale[l][None],
            mla_q_norm_g=mla_q_norm_g[l][None], mla_kv_norm_g=mla_kv_norm_g[l][None],
            qk_q=_pad_qk_gain(mla_qk_norm_q[l]), qk_k=_pad_qk_gain(mla_qk_norm_k[l]))

    def late_weights(l, w_out_g, small_g):
        shards = _unpack(small_g, [w[n] for n in SMALL_SHARDED])
        full = {n: _gathered_columns(shards[i], *w[n].shape)[l] for i, n in enumerate(SMALL_SHARDED)}
        wa_f = jnp.zeros((LANES, 2 * LANES), BF16).at[0:GLA_RANK].set(full["gla_wa2_f"])
        wa_b = jnp.zeros((LANES, 2 * LANES), BF16).at[GLA_RANK:2 * GLA_RANK].set(full["gla_wa2_b"])
        return dict(w_out=w_out_g.reshape(N_DEV, DEPTH, -1, D_MODEL)[:, l].reshape(-1, D_MODEL),
                    wa_f=wa_f, wa_b=wa_b, wq=_pad_wq(full["mla_wq_b"]), wkv=full["mla_wkv_b"])

    by_owner = lambda g_w_out: g_w_out.reshape(N_DEV, -1, D_MODEL)

    layers = [early_weights(0, w_in_g0), None]
    x1, sv0, (w_out_g, small_g), (w_in_g1,) = _layer_forward(
        xs, layers[0], lambda got: late_weights(0, *got), tables, tm, tq,
        ride_inproj=[("gather", w_out_b), ("gather", _pack([w[n] for n in SMALL_SHARDED], BF16))],
        ride_attn=("gather", w_in_b[:, 1]))
    layers[1] = early_weights(1, w_in_g1)
    (dx, loss_row), sv1, _, _ = _layer_forward(x1, layers[1], lambda got: late_weights(1, w_out_g, small_g), tables,
                                               tm, tq, target=target)
    loss = lax.psum(loss_row[0, 0], ("x", "y", "c"))

    def small_jobs(g):
        grads = (g, g1)
        full = {n: jnp.stack([grads[l][n].reshape(w[n].shape[1:]) if n in REPLICATED else grads[l][n]
                              for l in range(DEPTH)]) for n in SMALL_SHARDED + REPLICATED_EARLY}
        small_c = jax.vmap(lambda *shards: _pack(shards, F32))(*[_columns_by_device(full[n]) for n in SMALL_SHARDED])
        return [("scatter", small_c),
                ("gather", _pack([full[n] for n in REPLICATED_EARLY], F32))]

    dx, g1, got1 = _layer_backward(dx, sv1, layers[1], tables, tm, tq, {
        "attn": lambda g: ("scatter", by_owner(g["w_out"]))})
    dx, g0, got0 = _layer_backward(dx, sv0, layers[0], tables, tm, tq, {
        "ret": lambda g: ("scatter", by_owner(g["w_out"])),
        "gla": lambda g: ("scatter", g1["w_in_b"]),
        "attn": lambda g: ("scatter", g1["w_in_a"]),
        "d_w_in_a": small_jobs,
        "d_w_in_b": lambda g: ("scatter", g["w_in_a"]),
        "d_h": lambda g: ("scatter", g["w_in_b"])})
    in_parts = ((got0["d_w_in_b"][0], got0["d_h"][0]), (got0["attn"][0], got0["gla"][0]))
    out_parts = ((got0["ret"][0],), (got1["attn"][0],))
    small_parts, rep_parts = got0["d_w_in_a"]
    norm_pack = _pack([jnp.stack([g0["norm_g"][0], g1["norm_g"][0]])], F32)

    out = {}
    out["w_in"] = _adam_columns(in_parts, w_in, m_w_in, v_w_in, "adam_w_in", 256)
    *out["w_out"], norm_parts = _adam_layers(out_parts, w_out, m_w_out, v_w_out, "adam_w_out", 128,
                                             rider=("gather", norm_pack))
    for names, parts, label in ((SMALL_SHARDED, small_parts, "adam_small"),
                                (REPLICATED_EARLY, rep_parts, "adam_replicated"), (("norm_g",), norm_parts, "adam_norm")):
        res = _adam(parts, _pack([w[n] for n in names], F32), _pack([m[n] for n in names], F32),
                    _pack([v[n] for n in names], F32), label, 2048)
        for n, *vals in zip(names, *[_unpack(a, [w[n] for n in names]) for a in res]):
            out[n] = vals

    return (loss, dx[None], *[out[n][0] for n in WEIGHTS], *[out[n][1] for n in WEIGHTS],
            *[out[n][2] for n in WEIGHTS], *[out[n][3] for n in WEIGHTS])
```

```python
import functools
import math

import numpy as np
import jax
import jax.numpy as jnp
from jax import lax
from jax.experimental import pallas as pl
from jax.experimental.pallas import tpu as pltpu

F32 = jnp.float32
BF16 = jnp.bfloat16

N_DEV = 8
D_MODEL = 2048
DEPTH = 2
GROUP_W = 512
EPS = 1e-6
ROPE_THETA = 10000.0
LANES = 128

RET_HD = 128
RET_CHUNK = 256
RET_UNROLL = 4
GLA_CHUNK = 64
GLA_UNROLL = 8
GLA_CUM_ROWS = 256
GLA_DK = 64
GLA_TAU = 16.0
GLA_RANK = 16
POOL_WINDOWS = (2, 4, 8, 16)
MLA_QK = 192
MLA_ROPE = 64
ATTN_SCALE = MLA_QK ** -0.5
ATTN_Q_SCALE = ATTN_SCALE * math.log2(math.e)
IN_COLS = 5984

ADAM_LR = 0.001
ADAM_B1 = 0.9
ADAM_B2 = 0.999
ADAM_EPS = 1e-08
ADAM_WD = 0.01
ADAM_STEP = 10

A_Q, A_K, A_V, A_G = 0, 4, 8, 12
B_Q, B_K, B_V, B_G = 16, 18, 20, 24
M_Q, C_V, C_G, M_G = 28, 32, 36, 40
M_KV, GA, M_KR = 44, 46, 47
ZP_COLS = 48 * LANES

VMEM_LIMIT = 56 * 1024 * 1024


def _params(sem, vmem=VMEM_LIMIT):
    return pltpu.CompilerParams(dimension_semantics=sem, vmem_limit_bytes=vmem)


def _sigmoid(x):
    return 1.0 / (1.0 + jnp.exp(-x))


def _silu(x):
    return x * _sigmoid(x)


def _silu_grad(x):
    s = _sigmoid(x)
    return s * (1.0 + x * (1.0 - s))


def _dot(a, b, dims=(((1,), (0,)), ((), ()))):
    return lax.dot_general(a, b, dims, preferred_element_type=F32)


NT = (((1,), (1,)), ((), ()))
TN = (((0,), (0,)), ((), ()))


def _chunk_loop(n, body, init, unroll):
    unroll = math.gcd(n, unroll)

    def trip(t, carry):
        for u in range(unroll):
            carry = body(t * unroll + u, carry)
        return carry

    return lax.fori_loop(0, n // unroll, trip, init)


def _roll_lanes_half(x):
    return pltpu.roll(x, 64, 1)


def _wq_perm():
    idx = np.zeros((1024,), np.int32)
    ok = np.zeros((1024,), bool)
    for h in range(4):
        idx[128 * h:128 * h + 128] = 192 * h + np.arange(128)
        ok[128 * h:128 * h + 128] = True
        base = 512 + 128 * h
        idx[base:base + 32] = 192 * h + 128 + np.arange(32)
        ok[base:base + 32] = True
        idx[base + 64:base + 96] = 192 * h + 160 + np.arange(32)
        ok[base + 64:base + 96] = True
    inv = np.zeros((768,), np.int32)
    inv[idx[ok]] = np.nonzero(ok)[0]
    return idx, ok, inv


_WQ_IDX, _WQ_OK, _WQ_INV = _wq_perm()


def _pad_wq(wq):
    return jnp.where(jnp.asarray(_WQ_OK)[None, :], wq[:, _WQ_IDX], 0).astype(wq.dtype)


def _unpad_wq(wqp):
    return wqp[:, _WQ_INV]


def _qk_idx():
    idx = np.zeros((256,), np.int32)
    ok = np.zeros((256,), bool)
    idx[0:128] = np.arange(128)
    ok[0:128] = True
    idx[128:160] = 128 + np.arange(32)
    ok[128:160] = True
    idx[192:224] = 160 + np.arange(32)
    ok[192:224] = True
    inv = np.zeros((192,), np.int32)
    inv[idx[ok]] = np.nonzero(ok)[0]
    return idx, ok, inv


_QK_IDX, _QK_OK, _QK_INV = _qk_idx()


def _pad_qk_gain(g):
    return jnp.where(jnp.asarray(_QK_OK), g[_QK_IDX], 0.0).reshape(1, 256)


def _rope_tables(s):
    def tabs(dim):
        inv = 1.0 / (ROPE_THETA ** (jnp.arange(0, dim, 2, dtype=F32) / dim))
        ang = jnp.arange(s, dtype=F32)[:, None] * inv[None, :]
        return jnp.cos(ang), jnp.sin(ang)
    cr, sr = tabs(RET_HD)
    cos_r = jnp.concatenate([cr, cr], axis=1)
    sin_r = jnp.concatenate([-sr, sr], axis=1)
    cm, sm = tabs(MLA_ROPE)
    zz = jnp.zeros_like(cm)
    cos_m = jnp.concatenate([cm, zz, cm, zz], axis=1)
    sin_m = jnp.concatenate([-sm, zz, sm, zz], axis=1)
    return cos_r, sin_r, cos_m, sin_m


def _rope(x, cos, sin):
    return x * cos + _roll_lanes_half(x) * sin


def _rope_t(x, cos, sin):
    return x * cos + _roll_lanes_half(x * sin)


def _ret_tables():
    c = RET_CHUNK
    gamma_f = 1.0 - 2.0 ** (-5.0 - jnp.arange(4, dtype=F32))
    gamma_b = gamma_f[::-1]
    idx = jnp.arange(c, dtype=F32)
    diff = idx[:, None] - idx[None, :]

    def build(g1, g2):
        l1 = jnp.log(g1)[:, None, None]
        l2 = jnp.log(g2)[:, None, None]
        d1 = jnp.where(diff >= 0, jnp.exp(jnp.maximum(diff, 0.0)[None] * l1), 0.0)
        d2 = jnp.where(diff <= 0, jnp.exp(jnp.maximum(-diff, 0.0)[None] * l2), 0.0)
        ones = jnp.ones((1, c, LANES), F32)
        col = idx[None, :, None]
        qdf = jnp.exp((col + 1.0) * l1) * ones
        kdf = jnp.exp((c - 1.0 - col) * l1) * ones
        qdb = jnp.exp((c - col) * l2) * ones
        kdb = jnp.exp(col * l2) * ones
        cd1 = jnp.exp(c * l1) * ones
        cd2 = jnp.exp(c * l2) * ones
        return jnp.concatenate([d1 + d2, qdf, kdf, qdb, kdb, cd1, cd2], axis=2)

    return build(gamma_f, gamma_b), build(gamma_b, gamma_f)


MESH = pl.DeviceIdType.MESH
ANY = pl.BlockSpec(memory_space=pl.ANY)
_RELATIONS = ((0, 0, 1), (1, 0, 0), (0, 1, 0), (1, 1, 0), (1, 0, 1), (0, 1, 1), (1, 1, 1))


def _position():
    return lax.axis_index("x"), lax.axis_index("y"), lax.axis_index("c")


def _gather_copies(x_ref, out_ref, send_sems, recv_sems, local_sem, starting):
    x, y, cc = _position()
    me, sibling = (x, y, cc), (x, y, 1 - cc)
    chips = [(1 - x, y), (x, 1 - y), (1 - x, 1 - y)]

    def slab(px, py, pc):
        return out_ref.at[4 * px + 2 * py + pc]

    def copy(k, block, to, src=None):
        return pltpu.make_async_remote_copy(
            src_ref=slab(*block) if src is None else src, dst_ref=slab(*block),
            send_sem=send_sems.at[k], recv_sem=recv_sems.at[k], device_id=to, device_id_type=MESH)

    mine = pltpu.make_async_copy(x_ref, slab(*me), local_sem)
    first = [copy(0, me, sibling, src=x_ref)] + [copy(1 + j, me, (*chip, cc), src=x_ref) for j, chip in enumerate(chips)]
    if starting:
        return mine, first
    passed = [copy(4 + j, (*chip, cc), sibling) for j, chip in enumerate(chips)]
    arrivals = [copy(1 + j, (*chip, cc), me) for j, chip in enumerate(chips)]
    late = [copy(0, sibling, me)] + [copy(4 + j, (*chip, 1 - cc), me) for j, chip in enumerate(chips)]
    return mine, first, passed, arrivals, late


def _gather_start(*refs):
    mine, first = _gather_copies(*refs, starting=True)
    mine.start()
    for cp in first:
        cp.start()


def _gather_finish(*refs):
    mine, first, passed, arrivals, late = _gather_copies(*refs, starting=False)
    for arrived, onward in zip(arrivals, passed):
        arrived.wait_recv()
        onward.start()
    for cp in late:
        cp.wait_recv()
    for cp in first + passed:
        cp.wait_send()
    mine.wait()


def _scatter_copies(c_ref, out_ref, send_sems, recv_sems, local_sem):
    x, y, cc = _position()
    me = 4 * x + 2 * y + cc
    mine = pltpu.make_async_copy(c_ref.at[me], out_ref.at[me], local_sem)
    copies = []
    for k, (fx, fy, fc) in enumerate(_RELATIONS):
        px = 1 - x if fx else x
        py = 1 - y if fy else y
        pc = 1 - cc if fc else cc
        copies.append(pltpu.make_async_remote_copy(
            src_ref=c_ref.at[4 * px + 2 * py + pc], dst_ref=out_ref.at[me],
            send_sem=send_sems.at[k], recv_sem=recv_sems.at[k], device_id=(px, py, pc), device_id_type=MESH))
    return mine, copies


def _scatter_start(*refs):
    mine, copies = _scatter_copies(*refs)
    mine.start()
    for cp in copies:
        cp.start()


def _scatter_finish(*refs):
    mine, copies = _scatter_copies(*refs)
    for cp in copies:
        cp.wait()
    mine.wait()


_EXCHANGES = {"gather": (_gather_start, _gather_finish), "scatter": (_scatter_start, _scatter_finish)}


def _exchange_scratch():
    return [pltpu.SemaphoreType.DMA((7,)), pltpu.SemaphoreType.DMA((7,)), pltpu.SemaphoreType.DMA]


def _exchange_out(kind, src):
    return jax.ShapeDtypeStruct(((N_DEV,) + src.shape) if kind == "gather" else src.shape, src.dtype)


def _exchange(jobs, name):
    n = len(jobs)

    def body(*refs):
        srcs, outs, sems = refs[:n], refs[n:2 * n], refs[2 * n:]
        for half in (0, 1):
            for i, (kind, _) in enumerate(jobs):
                _EXCHANGES[kind][half](srcs[i], outs[i], *sems[3 * i:3 * i + 3])

    return pl.pallas_call(
        body, name=name, out_shape=[_exchange_out(kind, src) for kind, src in jobs],
        in_specs=[ANY] * n, out_specs=[ANY] * n,
        scratch_shapes=[sem for _ in jobs for sem in _exchange_scratch()])(*[src for _, src in jobs])


def _call(body, name, grid, in_specs, out_specs, out_shape, scratch, sem, args, rider=None):
    if rider is None:
        return pl.pallas_call(body, name=name, grid=grid, in_specs=in_specs, out_specs=out_specs, out_shape=out_shape,
                              scratch_shapes=scratch, compiler_params=_params(sem))(*args)
    jobs = rider if isinstance(rider, list) else [rider]
    ni, no, ns, nj = len(in_specs), len(out_specs), len(scratch), len(jobs)

    def carried(*refs):
        ins, rsrcs = refs[:ni], refs[ni:ni + nj]
        outs, routs = refs[ni + nj:ni + nj + no], refs[ni + nj + no:ni + 2 * nj + no]
        scr, sems = refs[ni + 2 * nj + no:ni + 2 * nj + no + ns], refs[ni + 2 * nj + no + ns:]
        ids = [pl.program_id(a) for a in range(len(grid))]
        is_first = functools.reduce(jnp.logical_and, [i == 0 for i in ids])
        is_last = functools.reduce(jnp.logical_and, [i == g - 1 for i, g in zip(ids, grid)])

        def half(which):
            for j, (kind, _) in enumerate(jobs):
                _EXCHANGES[kind][which](rsrcs[j], routs[j], *sems[3 * j:3 * j + 3])

        @pl.when(is_first)
        def _():
            half(0)

        body(*ins, *outs, *scr)

        @pl.when(is_last)
        def _():
            half(1)

    return pl.pallas_call(
        carried, name=name, grid=grid, in_specs=list(in_specs) + [ANY] * nj, out_specs=list(out_specs) + [ANY] * nj,
        out_shape=list(out_shape) + [_exchange_out(kind, src) for kind, src in jobs],
        scratch_shapes=list(scratch) + [sem for _ in jobs for sem in _exchange_scratch()],
        compiler_params=_params(("arbitrary",) * len(grid)))(*args, *[src for _, src in jobs])


def _inproj(x, g, wt, tm, tn=512, rider=None):
    s, d = x.shape
    n = wt.shape[0]

    def body(x_ref, g_ref, w_ref, z_ref, h_ref, hs):
        @pl.when(pl.program_id(1) == 0)
        def _():
            xv = x_ref[...]
            r = lax.rsqrt(jnp.mean(xv * xv, axis=-1, keepdims=True) + EPS)
            hv = (xv * r * g_ref[...]).astype(BF16)
            hs[...] = hv
            h_ref[...] = hv
        z_ref[...] = _dot(hs[...], w_ref[...], NT)

    return _call(
        body, "inproj", (s // tm, n // tn),
        [pl.BlockSpec((tm, d), lambda i, j: (i, 0)),
         pl.BlockSpec((1, d), lambda i, j: (0, 0)),
         pl.BlockSpec((tn, d), lambda i, j: (j, 0))],
        [pl.BlockSpec((tm, tn), lambda i, j: (i, j)), pl.BlockSpec((tm, d), lambda i, j: (i, 0))],
        [jax.ShapeDtypeStruct((s, n), F32), jax.ShapeDtypeStruct((s, d), BF16)],
        [pltpu.VMEM((tm, d), BF16)], ("parallel", "arbitrary"), (x, g, wt), rider)


def _relayout_plan():
    runs = ((0, 3584, 0), (3584, 3616, GA * LANES), (3616, 4640, C_V * LANES), (4640, 5152, M_Q * LANES),
            (5152, 5408, M_KV * LANES), (5408, 5440, M_KR * LANES), (5440, 5472, M_KR * LANES + 64),
            (5472, 5984, M_G * LANES))
    shard = IN_COLS // N_DEV
    plan = []
    for d in range(N_DEV):
        lo, hi = shard * d, shard * (d + 1)
        for a, b, p in runs:
            s, e = max(a, lo), min(b, hi)
            if s < e:
                plan.append((d, s - lo, p + (s - a), e - s))
    return plan


def _assemble_w_in(g, tc=512):
    _, c, r = g.shape
    tc = min(tc, r)

    def body(g_ref, o_ref):
        o_ref[...] = jnp.zeros_like(o_ref)
        for d, at, to, w in _relayout_plan():
            o_ref[to:to + w, :] = g_ref[d, at:at + w, :]

    return pl.pallas_call(
        body, name="assemble_w_in", grid=(r // tc,),
        in_specs=[pl.BlockSpec((N_DEV, c, tc), lambda i: (0, 0, i))],
        out_specs=pl.BlockSpec((ZP_COLS, tc), lambda i: (0, i)),
        out_shape=jax.ShapeDtypeStruct((ZP_COLS, r), g.dtype),
        compiler_params=_params(("parallel",)),
    )(g)


def _split_w_in(wt, tc=512):
    r = wt.shape[1]
    c = IN_COLS // N_DEV
    tc = min(tc, r)

    def body(w_ref, o_ref):
        for d, at, to, w in _relayout_plan():
            o_ref[d, at:at + w, :] = w_ref[to:to + w, :]

    return pl.pallas_call(
        body, name="split_w_in", grid=(r // tc,),
        in_specs=[pl.BlockSpec((ZP_COLS, tc), lambda i: (0, i))],
        out_specs=pl.BlockSpec((N_DEV, c, tc), lambda i: (0, 0, i)),
        out_shape=jax.ShapeDtypeStruct((N_DEV, c, r), wt.dtype),
        compiler_params=_params(("parallel",)),
    )(wt)


def _mm(a, b, mode, name, tm, tn, tk, add=None, out_dtype=F32, rider=None, tail=None):
    if mode == "tn":
        k, m = a.shape
    else:
        m, k = a.shape
    n = b.shape[0] if mode == "nt" else b.shape[1]
    tm, tn, tk = min(tm, m), min(tn, n), min(tk, k)
    nk = k // tk
    dims = {"nn": (((1,), (0,)), ((), ())), "nt": NT, "tn": TN}[mode]
    if tail is None:
        def plain(acc, i, extra_refs, out_refs):
            out_refs[0][...] = (acc + extra_refs[0][...] if extra_refs else acc).astype(out_dtype)
        tail = ([(add, "tile")] if add is not None else [], [(out_dtype, "tile")], plain)
    extra, outs, fn = tail
    spec = {"tile": pl.BlockSpec((tm, tn), lambda i, j, kk: (i, j)),
            "row": pl.BlockSpec((1, tn), lambda i, j, kk: (0, j)),
            "lanes": pl.BlockSpec((1, LANES), lambda i, j, kk: (0, 0))}
    shape = {"tile": (m, n), "row": (1, n), "lanes": (1, LANES)}
    ne, no = len(extra), len(outs)

    def body(*refs):
        a_ref, b_ref = refs[:2]
        extra_refs, out_refs, acc = refs[2:2 + ne], refs[2 + ne:2 + ne + no], refs[2 + ne + no]
        i, kk = pl.program_id(0), pl.program_id(2)

        @pl.when(kk == 0)
        def _():
            acc[...] = jnp.zeros_like(acc)

        acc[...] += _dot(a_ref[...].astype(BF16), b_ref[...].astype(BF16), dims)

        @pl.when(kk == nk - 1)
        def _():
            fn(acc[...], i, extra_refs, out_refs)

    a_spec = (pl.BlockSpec((tk, tm), lambda i, j, kk: (kk, i)) if mode == "tn"
              else pl.BlockSpec((tm, tk), lambda i, j, kk: (i, kk)))
    b_spec = (pl.BlockSpec((tn, tk), lambda i, j, kk: (j, kk)) if mode == "nt"
              else pl.BlockSpec((tk, tn), lambda i, j, kk: (kk, j)))
    summed = any(kind != "tile" for _, kind in outs)
    res = _call(body, name, (m // tm, n // tn, nk), [a_spec, b_spec] + [spec[kind] for _, kind in extra],
                [spec[kind] for _, kind in outs], [jax.ShapeDtypeStruct(shape[kind], dt) for dt, kind in outs],
                [pltpu.VMEM((tm, tn), F32)], ("arbitrary",) * 3 if summed else ("parallel", "parallel", "arbitrary"),
                [a, b] + [arr for arr, _ in extra], rider)
    return res[0] if (rider is None and no == 1) else res


def _norm_bwd_tail(x, g, dres):
    def fn(dh, i, extra_refs, out_refs):
        x_ref, g_ref, dres_ref = extra_refs
        dx_ref, dg_ref = out_refs

        @pl.when(i == 0)
        def _():
            dg_ref[...] = jnp.zeros_like(dg_ref)

        xv = x_ref[...]
        r = lax.rsqrt(jnp.mean(xv * xv, axis=-1, keepdims=True) + EPS)
        nv = xv * r
        dg_ref[...] += jnp.sum(dh * nv, axis=0, keepdims=True)
        u = dh * g_ref[...]
        dx_ref[...] = dres_ref[...] + r * (u - nv * jnp.mean(u * nv, axis=-1, keepdims=True))

    return [(x, "tile"), (g, "row"), (dres, "tile")], [(F32, "tile"), (F32, "row")], fn


def _loss_tail(x, target):
    d = x.shape[1]

    def fn(acc, i, extra_refs, out_refs):
        x_ref, t_ref = extra_refs
        dx_ref, loss_ref = out_refs

        @pl.when(i == 0)
        def _():
            loss_ref[...] = jnp.zeros_like(loss_ref)

        err = acc + x_ref[...] - t_ref[...]
        dx_ref[...] = err * (1.0 / d)
        per_tok = jnp.mean(err * err, axis=-1, keepdims=True)
        loss_ref[...] += 0.5 * jnp.sum(per_tok, axis=0, keepdims=True)

    return [(x, "tile"), (target, "tile")], [(F32, "tile"), (F32, "lanes")], fn


def _ret_core(q_ref, k_ref, v_ref, tab_ref, out_ref, back_ref, nchunk):
    c = RET_CHUNK

    def rows(n):
        return pl.ds(pl.multiple_of(n * c, c), c)

    zero = jnp.zeros((LANES, LANES), F32)

    def plane(i, n=c):
        return tab_ref[0:n, c + LANES * i:c + LANES * (i + 1)]

    def fwd(n, st):
        r = rows(n)
        q, k, vb = q_ref[r, :], k_ref[r, :], v_ref[r, :].astype(BF16)
        sc = _dot(q.astype(BF16), k.astype(BF16), NT) * tab_ref[:, 0:c]
        o = _dot(sc.astype(BF16), vb)
        o = o + _dot((q * plane(0)).astype(BF16), st.astype(BF16))
        out_ref[r, :] = o
        return st * plane(4, LANES) + _dot((k * plane(1)).astype(BF16), vb, TN)

    def bwd(i, st):
        r = rows(nchunk - 1 - i)
        q, k, vb = q_ref[r, :], k_ref[r, :], v_ref[r, :].astype(BF16)
        back_ref[r, :] = _dot((q * plane(2)).astype(BF16), st.astype(BF16))
        return st * plane(5, LANES) + _dot((k * plane(3)).astype(BF16), vb, TN)

    def both(i, states):
        return fwd(i, states[0]), bwd(i, states[1])

    _chunk_loop(nchunk, both, (zero, zero), RET_UNROLL)
    out_ref[...] += back_ref[...]


def _ret_fwd(z, cos_r, sin_r, tab, norm_g):
    s = z.shape[0]
    nchunk = s // RET_CHUNK
    scale = RET_HD ** -0.5
    col = lambda base: pl.BlockSpec((s, LANES), lambda h: (0, base + h), pipeline_mode=pl.Buffered(1))

    def body(q_ref, k_ref, v_ref, g_ref, cos_ref, sin_ref, tab_ref, ng_ref, o_ref, y_ref, qh, kh, back):
        qh[...] = _rope(q_ref[...], cos_ref[...], sin_ref[...])
        kh[...] = _rope(k_ref[...], cos_ref[...], sin_ref[...]) * scale
        _ret_core(qh, kh, v_ref, tab_ref, o_ref, back, nchunk)
        o = o_ref[...]
        r = lax.rsqrt(jnp.mean(o * o, axis=-1, keepdims=True) + EPS)
        y_ref[...] = (_silu(g_ref[...]) * (o * r * ng_ref[...])).astype(BF16)

    return pl.pallas_call(
        body, name="ret_fwd", grid=(4,),
        in_specs=[col(A_Q), col(A_K), col(A_V), col(A_G),
                  pl.BlockSpec((s, LANES), lambda h: (0, 0), pipeline_mode=pl.Buffered(1)),
                  pl.BlockSpec((s, LANES), lambda h: (0, 0), pipeline_mode=pl.Buffered(1)),
                  pl.BlockSpec((None, RET_CHUNK, RET_CHUNK + 6 * LANES), lambda h: (h, 0, 0)),
                  pl.BlockSpec((1, LANES), lambda h: (0, h))],
        out_specs=[pl.BlockSpec((s, LANES), lambda h: (0, h)), pl.BlockSpec((s, LANES), lambda h: (0, h))],
        out_shape=[jax.ShapeDtypeStruct((s, GROUP_W), F32), jax.ShapeDtypeStruct((s, GROUP_W), BF16)],
        scratch_shapes=[pltpu.VMEM((s, LANES), F32)] * 3,
        compiler_params=_params(("arbitrary",)),
    )(z, z, z, z, cos_r, sin_r, tab, norm_g)


def _ret_bwd(z, d_o, cos_r, sin_r, tab, tab_sw, rider=None):
    s = z.shape[0]
    nchunk = s // RET_CHUNK
    scale = RET_HD ** -0.5
    col = lambda base: pl.BlockSpec((s, LANES), lambda h: (0, base + h), pipeline_mode=pl.Buffered(1))
    whole = lambda: pl.BlockSpec((s, LANES), lambda h: (0, 0), pipeline_mode=pl.Buffered(1))
    tabspec = lambda: pl.BlockSpec((None, RET_CHUNK, RET_CHUNK + 6 * LANES), lambda h: (h, 0, 0))
    outspec = lambda: pl.BlockSpec((s, LANES), lambda h: (0, h))

    def body(q_ref, k_ref, v_ref, do_ref, cos_ref, sin_ref, tab_ref, tsw_ref, dq_ref, dk_ref, dv_ref,
             qh, kh, tmp, back):
        cos, sin = cos_ref[...], sin_ref[...]
        qh[...] = _rope(q_ref[...], cos, sin)
        kh[...] = _rope(k_ref[...], cos, sin) * scale
        _ret_core(kh, qh, do_ref, tsw_ref, tmp, back, nchunk)
        dv_ref[...] = tmp[...].astype(BF16)
        _ret_core(do_ref, v_ref, kh, tab_ref, tmp, back, nchunk)
        dq_ref[...] = _rope_t(tmp[...], cos, sin).astype(BF16)
        _ret_core(v_ref, do_ref, qh, tsw_ref, tmp, back, nchunk)
        dk_ref[...] = _rope_t(tmp[...] * scale, cos, sin).astype(BF16)

    return _call(
        body, "ret_bwd", (4,),
        [col(A_Q), col(A_K), col(A_V),
         pl.BlockSpec((s, LANES), lambda h: (0, h), pipeline_mode=pl.Buffered(1)),
         whole(), whole(), tabspec(), tabspec()],
        [outspec(), outspec(), outspec()],
        [jax.ShapeDtypeStruct((s, GROUP_W), BF16)] * 3,
        [pltpu.VMEM((s, LANES), F32)] * 4,
        ("arbitrary",), (z, z, z, d_o, cos_r, sin_r, tab, tab_sw), rider)


def _normgate_bwd(o, z, gate_blk, dy, dy_blk, norm_g, tm):
    s = o.shape[0]

    def body(o_ref, g_ref, dy_ref, ng_ref, do_ref, dg_ref, dng_ref):
        @pl.when(pl.program_id(0) == 0)
        def _():
            dng_ref[...] = jnp.zeros_like(dng_ref)

        for h in range(4):
            sl = slice(LANES * h, LANES * (h + 1))
            ov, gv, dyv, ng = o_ref[:, sl], g_ref[:, sl], dy_ref[:, sl], ng_ref[:, sl]
            r = lax.rsqrt(jnp.mean(ov * ov, axis=-1, keepdims=True) + EPS)
            on = ov * r
            dn = dyv * _silu(gv)
            u = dn * ng
            do_ref[:, sl] = r * (u - on * jnp.mean(u * on, axis=-1, keepdims=True))
            dg_ref[:, sl] = (dyv * (on * ng) * _silu_grad(gv)).astype(BF16)
            dng_ref[:, sl] += jnp.sum(dn * on, axis=0, keepdims=True)

    return pl.pallas_call(
        body, name="normgate_bwd", grid=(s // tm,),
        in_specs=[pl.BlockSpec((tm, GROUP_W), lambda i: (i, 0)),
                  pl.BlockSpec((tm, GROUP_W), lambda i: (i, gate_blk // 4)),
                  pl.BlockSpec((tm, GROUP_W), lambda i: (i, dy_blk)),
                  pl.BlockSpec((1, GROUP_W), lambda i: (0, 0))],
        out_specs=[pl.BlockSpec((tm, GROUP_W), lambda i: (i, 0)), pl.BlockSpec((tm, GROUP_W), lambda i: (i, 0)),
                   pl.BlockSpec((1, GROUP_W), lambda i: (0, 0))],
        out_shape=[jax.ShapeDtypeStruct((s, GROUP_W), F32), jax.ShapeDtypeStruct((s, GROUP_W), BF16),
                   jax.ShapeDtypeStruct((1, GROUP_W), F32)],
        compiler_params=_params(("arbitrary",)),
    )(o, z, dy, norm_g)


def _log_sigmoid(x):
    return jnp.minimum(x, 0.0) - jnp.log(1.0 + jnp.exp(-jnp.abs(x)))


def _gla_consts():
    c = GLA_CHUNK
    row = lax.broadcasted_iota(jnp.int32, (c, c), 0)
    colm = lax.broadcasted_iota(jnp.int32, (c, c), 1)
    lane = lax.broadcasted_iota(jnp.int32, (1, LANES), 1)
    low = row >= colm
    up = colm >= row
    heads = ((lane < GLA_DK).astype(F32), (lane >= GLA_DK).astype(F32))
    return low, up, heads


def _chunk_running_sum(x, suffix):
    rows = x.shape[0]
    pos = jnp.bitwise_and(lax.broadcasted_iota(jnp.int32, (rows, 1), 0), GLA_CHUNK - 1)
    k = 1
    while k < GLA_CHUNK:
        if suffix:
            x = x + jnp.where(pos < GLA_CHUNK - k, pltpu.roll(x, rows - k, 0), 0.0)
        else:
            x = x + jnp.where(pos >= k, pltpu.roll(x, k, 0), 0.0)
        k *= 2
    return x


def _gla_chunk(cum_ref, d, n):
    c = GLA_CHUNK
    cum = cum_ref[d, pl.ds(pl.multiple_of(n * c, c), c), :]
    last = cum_ref[d, pl.ds(n * c + (c - 1 if d == 0 else 0), 1), :]
    eq = jnp.exp(cum)
    ek = jnp.exp(-cum)
    el = jnp.exp(last - cum)
    dec = jnp.exp(last)
    return eq, ek, el, dec


def _gla_gates(ga_ref, wa_ref, ba_ref, cum_ref, s, suffix):
    rows = min(s, GLA_CUM_ROWS)

    def step(i, carry):
        r = pl.ds(pl.multiple_of(i * rows, rows), rows)
        pre = _dot(ga_ref[r, :].astype(BF16), wa_ref[...].astype(BF16)) + ba_ref[...]
        cum_ref[r, :] = _chunk_running_sum(_log_sigmoid(pre) * (1.0 / GLA_TAU), suffix)
        return carry
    lax.fori_loop(0, s // rows, step, 0)


def _gla_fwd(z, wa_f, wa_b, ba_f, ba_b, norm_g):
    s = z.shape[0]
    c = GLA_CHUNK
    nchunk = s // c
    scale = GLA_DK ** -0.5
    tm = min(s, 512)
    one = pl.Buffered(1)

    def body(q_ref, k_ref, v_ref, ga_ref, g_ref, waf_ref, wab_ref, baf_ref, bab_ref, ng_ref, o_ref, y_ref,
             la_s):
        low, up, heads = _gla_consts()
        _gla_gates(ga_ref, waf_ref, baf_ref, la_s.at[0], s, False)
        _gla_gates(ga_ref, wab_ref, bab_ref, la_s.at[1], s, True)
        for d in range(2):
            tri = (low, up)[d]

            def step(i, states):
                n = i if d == 0 else nchunk - 1 - i
                r = pl.ds(pl.multiple_of(n * c, c), c)
                q = q_ref[r, :] * scale
                k = k_ref[r, :]
                eq, ek, el, dec = _gla_chunk(la_s, d, n)
                qt = q * eq
                ktb = (k * ek).astype(BF16)
                kl = k * el
                new_states = []
                for hh in range(2):
                    cols = slice(LANES * hh, LANES * (hh + 1))
                    vb = v_ref[r, cols].astype(BF16)
                    qm = (qt * heads[hh]).astype(BF16)
                    a = jnp.where(tri, _dot(qm, ktb, NT), 0.0)
                    o = _dot(a.astype(BF16), vb) + _dot(qm, states[hh].astype(BF16), NT)
                    if d == 0:
                        o_ref[r, cols] = o
                    else:
                        o_ref[r, cols] += o
                    new_states.append(states[hh] * dec + _dot(vb, (kl * heads[hh]).astype(BF16), TN))
                return tuple(new_states)

            zero = jnp.zeros((LANES, LANES), F32)
            _chunk_loop(nchunk, step, (zero, zero), GLA_UNROLL)

        def epi(i, carry):
            r = pl.ds(pl.multiple_of(i * tm, tm), tm)
            for hh in range(2):
                cols = slice(LANES * hh, LANES * (hh + 1))
                o = o_ref[r, cols]
                rr = lax.rsqrt(jnp.mean(o * o, axis=-1, keepdims=True) + EPS)
                y_ref[r, cols] = (_silu(g_ref[r, cols]) * (o * rr * ng_ref[:, cols])).astype(BF16)
            return carry

        lax.fori_loop(0, s // tm, epi, 0)

    w2 = 2 * LANES
    return pl.pallas_call(
        body, name="gla_fwd", grid=(2,),
        in_specs=[pl.BlockSpec((s, LANES), lambda p: (0, B_Q + p), pipeline_mode=one),
                  pl.BlockSpec((s, LANES), lambda p: (0, B_K + p), pipeline_mode=one),
                  pl.BlockSpec((s, w2), lambda p: (0, B_V // 2 + p), pipeline_mode=one),
                  pl.BlockSpec((s, LANES), lambda p: (0, GA), pipeline_mode=one),
                  pl.BlockSpec((s, w2), lambda p: (0, B_G // 2 + p), pipeline_mode=one),
                  pl.BlockSpec((LANES, LANES), lambda p: (0, p)),
                  pl.BlockSpec((LANES, LANES), lambda p: (0, p)),
                  pl.BlockSpec((1, LANES), lambda p: (0, p)),
                  pl.BlockSpec((1, LANES), lambda p: (0, p)),
                  pl.BlockSpec((1, w2), lambda p: (0, p))],
        out_specs=[pl.BlockSpec((s, w2), lambda p: (0, p)), pl.BlockSpec((s, w2), lambda p: (0, p))],
        out_shape=[jax.ShapeDtypeStruct((s, GROUP_W), F32), jax.ShapeDtypeStruct((s, GROUP_W), BF16)],
        scratch_shapes=[pltpu.VMEM((2, s, LANES), F32)],
        compiler_params=_params(("arbitrary",)),
    )(z, z, z, z, z, wa_f, wa_b, ba_f, ba_b, norm_g)


def _gla_bwd(z, d_o, wa_f, wa_b, ba_f, ba_b, rider=None):
    s = z.shape[0]
    c = GLA_CHUNK
    nchunk = s // c
    scale = GLA_DK ** -0.5
    tm = min(s, GLA_CUM_ROWS)
    one = pl.Buffered(1)

    def body(q_ref, k_ref, v_ref, ga_ref, do_ref, waf_ref, wab_ref, baf_ref, bab_ref,
             dq_ref, dk_ref, dv_ref, dga_ref, dwaf_ref, dwab_ref, dbaf_ref, dbab_ref,
             la_s, dla_s, stash, dq_s, dk_s, dv_s):
        low, up, heads = _gla_consts()
        rowi = lax.broadcasted_iota(jnp.int32, (c, 1), 0)
        _gla_gates(ga_ref, waf_ref, baf_ref, la_s.at[0], s, False)
        _gla_gates(ga_ref, wab_ref, bab_ref, la_s.at[1], s, True)
        for d in range(2):
            tri = (low, up)[d]
            last_row = (rowi == (c - 1 if d == 0 else 0)).astype(F32)
            order = (lambda i: i) if d == 0 else (lambda i: nchunk - 1 - i)
            zero = jnp.zeros((LANES, LANES), F32)

            def states(i, sts):
                n = order(i)
                r = pl.ds(pl.multiple_of(n * c, c), c)
                k = k_ref[r, :]
                _, _, el, dec = _gla_chunk(la_s, d, n)
                kl = k * el
                new = []
                for hh in range(2):
                    cols = slice(LANES * hh, LANES * (hh + 1))
                    stash[hh, n] = sts[hh]
                    new.append(sts[hh] * dec + _dot(v_ref[r, cols].astype(BF16), (kl * heads[hh]).astype(BF16), TN))
                return tuple(new)

            _chunk_loop(nchunk, states, (zero, zero), GLA_UNROLL)

            def step(i, dsts):
                n = order(nchunk - 1 - i)
                r = pl.ds(pl.multiple_of(n * c, c), c)
                q = q_ref[r, :] * scale
                k = k_ref[r, :]
                eq, ek, el, dec = _gla_chunk(la_s, d, n)
                qt = q * eq
                kt = k * ek
                kl = k * el
                ktb = kt.astype(BF16)
                dqt = jnp.zeros((c, LANES), F32)
                dkt = jnp.zeros((c, LANES), F32)
                dkl = jnp.zeros((c, LANES), F32)
                ddec = jnp.zeros((1, LANES), F32)
                new = []
                for hh in range(2):
                    cols = slice(LANES * hh, LANES * (hh + 1))
                    vb = v_ref[r, cols].astype(BF16)
                    dob = do_ref[r, cols].astype(BF16)
                    qm = (qt * heads[hh]).astype(BF16)
                    a = jnp.where(tri, _dot(qm, ktb, NT), 0.0).astype(BF16)
                    da = jnp.where(tri, _dot(dob, vb, NT), 0.0).astype(BF16)
                    sn = stash[hh, n]
                    dst = dsts[hh]
                    dstb = dst.astype(BF16)
                    dqt = dqt + (_dot(da, ktb) + _dot(dob, sn.astype(BF16))) * heads[hh]
                    dkt = dkt + _dot(da, qm, TN)
                    dv = _dot(a, dob, TN) + _dot((kl * heads[hh]).astype(BF16), dstb, NT)
                    dkl = dkl + _dot(vb, dstb)
                    ddec = ddec + jnp.sum(dst * sn, axis=0, keepdims=True)
                    new.append(dst * dec + _dot(dob, qm, TN))
                    if d == 0:
                        dv_s[r, cols] = dv
                    else:
                        dv_ref[r, cols] = (dv_s[r, cols] + dv).astype(BF16)
                dlast = ddec * dec + jnp.sum(dkl * kl, axis=0, keepdims=True)
                dq = dqt * eq * scale
                dk = dkt * ek + dkl * el
                dcum = dqt * qt - dkt * kt - dkl * kl + last_row * dlast
                dla_s[d, r, :] = dcum
                if d == 0:
                    dq_s[r, :] = dq
                    dk_s[r, :] = dk
                else:
                    dq_ref[r, :] = (dq_s[r, :] + dq).astype(BF16)
                    dk_ref[r, :] = (dk_s[r, :] + dk).astype(BF16)
                return tuple(new)

            _chunk_loop(nchunk, step, (zero, zero), GLA_UNROLL)

        first = pl.program_id(0) == 0
        for d, (wa_ref, ba_ref, dwa_ref, dba_ref) in enumerate(
                ((waf_ref, baf_ref, dwaf_ref, dbaf_ref), (wab_ref, bab_ref, dwab_ref, dbab_ref))):
            dwa_ref[...] = jnp.zeros_like(dwa_ref)
            dba_ref[...] = jnp.zeros_like(dba_ref)

            def gates(i, carry):
                r = pl.ds(pl.multiple_of(i * tm, tm), tm)
                gab = ga_ref[r, :].astype(BF16)
                wab16 = wa_ref[...].astype(BF16)
                pre = _dot(gab, wab16) + ba_ref[...]
                dla = _chunk_running_sum(dla_s[d, r, :], suffix=(d == 0))
                dpre = dla * (1.0 / GLA_TAU) * _sigmoid(-pre)
                dpb = dpre.astype(BF16)
                dwa_ref[...] += _dot(gab, dpb, TN)
                dba_ref[...] += jnp.sum(dpre, axis=0, keepdims=True)
                dga = _dot(dpb, wab16, NT)
                if d == 0:
                    @pl.when(first)
                    def _():
                        dga_ref[r, :] = dga

                    @pl.when(jnp.logical_not(first))
                    def _():
                        dga_ref[r, :] += dga
                else:
                    dga_ref[r, :] += dga
                return carry

            lax.fori_loop(0, s // tm, gates, 0)

    w2 = 2 * LANES
    return _call(
        body, "gla_bwd", (2,),
        [pl.BlockSpec((s, LANES), lambda p: (0, B_Q + p), pipeline_mode=one),
         pl.BlockSpec((s, LANES), lambda p: (0, B_K + p), pipeline_mode=one),
         pl.BlockSpec((s, w2), lambda p: (0, B_V // 2 + p), pipeline_mode=one),
         pl.BlockSpec((s, LANES), lambda p: (0, GA), pipeline_mode=one),
         pl.BlockSpec((s, w2), lambda p: (0, p), pipeline_mode=one),
         pl.BlockSpec((LANES, LANES), lambda p: (0, p)),
         pl.BlockSpec((LANES, LANES), lambda p: (0, p)),
         pl.BlockSpec((1, LANES), lambda p: (0, p)),
         pl.BlockSpec((1, LANES), lambda p: (0, p))],
        [pl.BlockSpec((s, LANES), lambda p: (0, p), pipeline_mode=one),
         pl.BlockSpec((s, LANES), lambda p: (0, p), pipeline_mode=one),
         pl.BlockSpec((s, w2), lambda p: (0, p), pipeline_mode=one),
         pl.BlockSpec((s, LANES), lambda p: (0, 0), pipeline_mode=one),
         pl.BlockSpec((LANES, LANES), lambda p: (0, p)),
         pl.BlockSpec((LANES, LANES), lambda p: (0, p)),
         pl.BlockSpec((1, LANES), lambda p: (0, p)),
         pl.BlockSpec((1, LANES), lambda p: (0, p))],
        [jax.ShapeDtypeStruct((s, w2), BF16), jax.ShapeDtypeStruct((s, w2), BF16),
         jax.ShapeDtypeStruct((s, GROUP_W), BF16), jax.ShapeDtypeStruct((s, LANES), F32),
         jax.ShapeDtypeStruct((LANES, w2), F32), jax.ShapeDtypeStruct((LANES, w2), F32),
         jax.ShapeDtypeStruct((1, w2), F32), jax.ShapeDtypeStruct((1, w2), F32)],
        [pltpu.VMEM((2, s, LANES), F32), pltpu.VMEM((2, s, LANES), F32),
         pltpu.VMEM((2, nchunk, LANES, LANES), F32),
         pltpu.VMEM((s, LANES), F32), pltpu.VMEM((s, LANES), F32), pltpu.VMEM((s, w2), F32)],
        ("arbitrary",), (z, z, z, z, d_o, wa_f, wa_b, ba_f, ba_b), rider)


def _shift_rows(x, d, rowi):
    s = x.shape[0]
    if d == 0:
        return x
    y = pltpu.roll(x, d % s, 0)
    keep = (rowi >= d) if d > 0 else (rowi < s + d)
    return jnp.where(keep, y, 0.0)


def _run_sum(x, m, step, rowi):
    acc, n = x, 1
    while n < m:
        acc = acc + _shift_rows(acc, step * n, rowi)
        n *= 2
    return acc


def _pool_counts(s, w, rowi):
    hi = jnp.minimum(rowi + w // 2, s)
    lo = jnp.maximum(rowi - w // 2, 0)
    return (hi - lo).astype(F32)


def _pooled(u, w, rowi):
    s = u.shape[0]
    win = _shift_rows(_run_sum(u, w // 2, 1, rowi), 1, rowi) + _run_sum(u, w // 2, -1, rowi)
    return win / _pool_counts(s, w, rowi) - u


def _pool_fwd(z, pool_w, pool_scale):
    s = z.shape[0]
    one = pl.Buffered(1)

    def body(u_ref, g_ref, w_ref, sc_ref, y_ref):
        rowi = lax.broadcasted_iota(jnp.int32, (s, 1), 0)
        for g, w in enumerate(POOL_WINDOWS):
            cols = slice(LANES * g, LANES * (g + 1))
            pooled = _pooled(u_ref[:, cols], w, rowi)
            mixed = _dot(pooled.astype(BF16), w_ref[g].astype(BF16))
            y_ref[:, cols] = (_silu(g_ref[:, cols]) * (mixed * sc_ref[:, cols])).astype(BF16)

    return pl.pallas_call(
        body, name="pool_fwd", grid=(1,),
        in_specs=[pl.BlockSpec((s, GROUP_W), lambda i: (0, C_V // 4), pipeline_mode=one),
                  pl.BlockSpec((s, GROUP_W), lambda i: (0, C_G // 4), pipeline_mode=one),
                  pl.BlockSpec((4, LANES, LANES), lambda i: (0, 0, 0)),
                  pl.BlockSpec((1, GROUP_W), lambda i: (0, 0))],
        out_specs=pl.BlockSpec((s, GROUP_W), lambda i: (0, 0), pipeline_mode=one),
        out_shape=jax.ShapeDtypeStruct((s, GROUP_W), BF16),
        compiler_params=_params(("arbitrary",)),
    )(z, z, pool_w, pool_scale)


def _pool_bwd(z, dy, pool_w, pool_scale):
    s = z.shape[0]
    one = pl.Buffered(1)

    def body(u_ref, g_ref, dy_ref, w_ref, sc_ref, du_ref, dg_ref, dw_ref, dsc_ref):
        rowi = lax.broadcasted_iota(jnp.int32, (s, 1), 0)
        for g, w in enumerate(POOL_WINDOWS):
            cols = slice(LANES * g, LANES * (g + 1))
            gate, dyv, sc = g_ref[:, cols], dy_ref[:, cols], sc_ref[:, cols]
            wb = w_ref[g].astype(BF16)
            pooled = _pooled(u_ref[:, cols], w, rowi)
            pb = pooled.astype(BF16)
            mixed = _dot(pb, wb)
            dg_ref[:, cols] = (dyv * (mixed * sc) * _silu_grad(gate)).astype(BF16)
            dt = dyv * _silu(gate)
            dsc_ref[:, cols] = jnp.sum(dt * mixed, axis=0, keepdims=True)
            dmb = (dt * sc).astype(BF16)
            dw_ref[g] = _dot(pb, dmb, TN)
            dpool = _dot(dmb, wb, NT)
            e = dpool / _pool_counts(s, w, rowi)
            du_ref[:, cols] = (_run_sum(e, w // 2, 1, rowi) + _shift_rows(_run_sum(e, w // 2, -1, rowi), -1, rowi)
                               - dpool).astype(BF16)

    return pl.pallas_call(
        body, name="pool_bwd", grid=(1,),
        in_specs=[pl.BlockSpec((s, GROUP_W), lambda i: (0, C_V // 4), pipeline_mode=one),
                  pl.BlockSpec((s, GROUP_W), lambda i: (0, C_G // 4), pipeline_mode=one),
                  pl.BlockSpec((s, GROUP_W), lambda i: (0, 2), pipeline_mode=one),
                  pl.BlockSpec((4, LANES, LANES), lambda i: (0, 0, 0)),
                  pl.BlockSpec((1, GROUP_W), lambda i: (0, 0))],
        out_specs=[pl.BlockSpec((s, GROUP_W), lambda i: (0, 0), pipeline_mode=one),
                   pl.BlockSpec((s, GROUP_W), lambda i: (0, 0), pipeline_mode=one),
                   pl.BlockSpec((4, LANES, LANES), lambda i: (0, 0, 0)),
                   pl.BlockSpec((1, GROUP_W), lambda i: (0, 0))],
        out_shape=[jax.ShapeDtypeStruct((s, GROUP_W), BF16), jax.ShapeDtypeStruct((s, GROUP_W), BF16),
                   jax.ShapeDtypeStruct((4, LANES, LANES), F32), jax.ShapeDtypeStruct((1, GROUP_W), F32)],
        compiler_params=_params(("arbitrary",)),
    )(z, z, dy, pool_w, pool_scale)


def _mla_heads(qf, kv, kpe, qg, kg, cos, sin):
    out = []
    for h in range(4):
        qa = qf[:, LANES * h:LANES * (h + 1)]
        qb = qf[:, 512 + LANES * h:512 + LANES * (h + 1)]
        ka = kv[:, 256 * h:256 * h + LANES]
        rq = lax.rsqrt((jnp.sum(qa * qa, axis=-1, keepdims=True) + jnp.sum(qb * qb, axis=-1, keepdims=True))
                       * (1.0 / MLA_QK) + EPS)
        rk = lax.rsqrt((jnp.sum(ka * ka, axis=-1, keepdims=True) + jnp.sum(kpe * kpe, axis=-1, keepdims=True))
                       * (1.0 / MLA_QK) + EPS)
        out.append((qa, qb, rq, ka, rk))
    return out


def _mla_latents(mq_ref, mkv_ref, gq_ref, gkv_ref, wq_ref, wkv_ref):
    mq = mq_ref[...]
    rq = lax.rsqrt(jnp.mean(mq * mq, axis=-1, keepdims=True) + EPS)
    qn = mq * rq
    qnb = (qn * gq_ref[...]).astype(BF16)
    mkv = mkv_ref[...]
    rk = lax.rsqrt(jnp.mean(mkv * mkv, axis=-1, keepdims=True) + EPS)
    kvn = mkv * rk
    kvnb = (kvn * gkv_ref[...]).astype(BF16)
    qf = _dot(qnb, wq_ref[...])
    kv = _dot(kvnb, wkv_ref[...])
    return qn, rq, qnb, kvn, rk, kvnb, qf, kv


def _mla_prep(z, cos_m, sin_m, gq, wq, gkv, wkv, qg, kg, tm):
    s = z.shape[0]

    def body(mq_ref, mkv_ref, mkr_ref, cos_ref, sin_ref, gq_ref, wq_ref, gkv_ref, wkv_ref, qg_ref, kg_ref,
             q_ref, k_ref, v_ref):
        _, _, _, _, _, _, qf, kv = _mla_latents(mq_ref, mkv_ref, gq_ref, gkv_ref, wq_ref, wkv_ref)
        kpe = mkr_ref[...]
        cos, sin = cos_ref[...], sin_ref[...]
        qg, kg = qg_ref[...], kg_ref[...]
        for h, (qa, qb, rq, ka, rk) in enumerate(_mla_heads(qf, kv, kpe, qg, kg, cos, sin)):
            q_ref[h, :, 0:LANES] = (qa * rq * qg[:, 0:LANES] * ATTN_Q_SCALE).astype(BF16)
            q_ref[h, :, LANES:] = (_rope(qb * rq * qg[:, LANES:], cos, sin) * ATTN_Q_SCALE).astype(BF16)
            k_ref[h, :, 0:LANES] = (ka * rk * kg[:, 0:LANES]).astype(BF16)
            k_ref[h, :, LANES:] = _rope(kpe * rk * kg[:, LANES:], cos, sin).astype(BF16)
            v_ref[h] = kv[:, 256 * h + LANES:256 * (h + 1)].astype(BF16)

    full = lambda shape: pl.BlockSpec(shape, lambda i: (0,) * len(shape))
    return pl.pallas_call(
        body, name="mla_prep", grid=(s // tm,),
        in_specs=[pl.BlockSpec((tm, 512), lambda i: (i, M_Q // 4)),
                  pl.BlockSpec((tm, 256), lambda i: (i, M_KV // 2)),
                  pl.BlockSpec((tm, LANES), lambda i: (i, M_KR)),
                  pl.BlockSpec((tm, LANES), lambda i: (i, 0)),
                  pl.BlockSpec((tm, LANES), lambda i: (i, 0)),
                  full((1, 512)), full((512, 1024)), full((1, 256)), full((256, 1024)), full((1, 256)), full((1, 256))],
        out_specs=[pl.BlockSpec((4, tm, 256), lambda i: (0, i, 0)), pl.BlockSpec((4, tm, 256), lambda i: (0, i, 0)),
                   pl.BlockSpec((4, tm, LANES), lambda i: (0, i, 0))],
        out_shape=[jax.ShapeDtypeStruct((4, s, 256), BF16), jax.ShapeDtypeStruct((4, s, 256), BF16),
                   jax.ShapeDtypeStruct((4, s, LANES), BF16)],
        compiler_params=_params(("parallel",)),
    )(z, z, z, cos_m, sin_m, gq, wq, gkv, wkv, qg, kg)


def _mla_prep_bwd(z, cos_m, sin_m, gq, wq, gkv, wkv, qg, kg, dq, dk, dv, tm):
    s = z.shape[0]

    def body(mq_ref, mkv_ref, mkr_ref, cos_ref, sin_ref, gq_ref, wq_ref, gkv_ref, wkv_ref, qg_ref, kg_ref,
             dq_ref, dk_ref, dv_ref,
             dmq_ref, dmkv_ref, dmkr_ref, dwq_ref, dwkv_ref, dgq_ref, dgkv_ref, dqg_ref, dkg_ref, dqf, dkv):
        @pl.when(pl.program_id(0) == 0)
        def _():
            for r in (dwq_ref, dwkv_ref, dgq_ref, dgkv_ref, dqg_ref, dkg_ref):
                r[...] = jnp.zeros_like(r)

        qn, rq0, qnb, kvn, rk0, kvnb, qf, kv = _mla_latents(mq_ref, mkv_ref, gq_ref, gkv_ref, wq_ref, wkv_ref)
        kpe = mkr_ref[...]
        cos, sin = cos_ref[...], sin_ref[...]
        qg, kg = qg_ref[...], kg_ref[...]
        dkpe = jnp.zeros_like(kpe)
        inv = 1.0 / MLA_QK

        def norm_bwd(a, b, r, da_n, db_n, g):
            ga, gb = g[:, 0:LANES], g[:, LANES:]
            dg_a = jnp.sum(da_n * a * r, axis=0, keepdims=True)
            dg_b = jnp.sum(db_n * b * r, axis=0, keepdims=True)
            ua, ub = da_n * ga, db_n * gb
            dt = (jnp.sum(ua * a, axis=-1, keepdims=True) + jnp.sum(ub * b, axis=-1, keepdims=True)) * inv
            r3 = r * r * r
            return r * ua - a * (r3 * dt), r * ub - b * (r3 * dt), dg_a, dg_b

        for h, (qa, qb, rq, ka, rk) in enumerate(_mla_heads(qf, kv, kpe, qg, kg, cos, sin)):
            dqa, dqb, dga, dgb = norm_bwd(qa, qb, rq, dq_ref[h, :, 0:LANES] * ATTN_SCALE,
                                          _rope_t(dq_ref[h, :, LANES:] * ATTN_SCALE, cos, sin), qg)
            dqf[:, LANES * h:LANES * (h + 1)] = dqa
            dqf[:, 512 + LANES * h:512 + LANES * (h + 1)] = dqb
            dqg_ref[:, 0:LANES] += dga
            dqg_ref[:, LANES:] += dgb
            ln2 = math.log(2.0)
            dka, dkb, dga, dgb = norm_bwd(ka, kpe, rk, dk_ref[h, :, 0:LANES] * ln2,
                                          _rope_t(dk_ref[h, :, LANES:] * ln2, cos, sin), kg)
            dkv[:, 256 * h:256 * h + LANES] = dka
            dkv[:, 256 * h + LANES:256 * (h + 1)] = dv_ref[h]
            dkpe = dkpe + dkb
            dkg_ref[:, 0:LANES] += dga
            dkg_ref[:, LANES:] += dgb
        dmkr_ref[...] = dkpe.astype(BF16)

        def latent_bwd(dfull, w_ref, nb, n, r, g_ref, dw_ref, dg_ref, dlat_ref):
            db = dfull.astype(BF16)
            dn = _dot(db, w_ref[...], NT)
            dw_ref[...] += _dot(nb, db, TN)
            dg_ref[...] += jnp.sum(dn * n, axis=0, keepdims=True)
            u = dn * g_ref[...]
            dlat_ref[...] = (r * (u - n * jnp.mean(u * n, axis=-1, keepdims=True))).astype(BF16)

        latent_bwd(dqf[...], wq_ref, qnb, qn, rq0, gq_ref, dwq_ref, dgq_ref, dmq_ref)
        latent_bwd(dkv[...], wkv_ref, kvnb, kvn, rk0, gkv_ref, dwkv_ref, dgkv_ref, dmkv_ref)

    full = lambda shape: pl.BlockSpec(shape, lambda i: (0,) * len(shape))
    return pl.pallas_call(
        body, name="mla_prep_bwd", grid=(s // tm,),
        in_specs=[pl.BlockSpec((tm, 512), lambda i: (i, M_Q // 4)),
                  pl.BlockSpec((tm, 256), lambda i: (i, M_KV // 2)),
                  pl.BlockSpec((tm, LANES), lambda i: (i, M_KR)),
                  pl.BlockSpec((tm, LANES), lambda i: (i, 0)),
                  pl.BlockSpec((tm, LANES), lambda i: (i, 0)),
                  full((1, 512)), full((512, 1024)), full((1, 256)), full((256, 1024)), full((1, 256)), full((1, 256)),
                  pl.BlockSpec((4, tm, 256), lambda i: (0, i, 0)), pl.BlockSpec((4, tm, 256), lambda i: (0, i, 0)),
                  pl.BlockSpec((4, tm, LANES), lambda i: (0, i, 0))],
        out_specs=[pl.BlockSpec((tm, 512), lambda i: (i, 0)), pl.BlockSpec((tm, 256), lambda i: (i, 0)),
                   pl.BlockSpec((tm, LANES), lambda i: (i, 0)),
                   full((512, 1024)), full((256, 1024)), full((1, 512)), full((1, 256)), full((1, 256)), full((1, 256))],
        out_shape=[jax.ShapeDtypeStruct((s, 512), BF16), jax.ShapeDtypeStruct((s, 256), BF16),
                   jax.ShapeDtypeStruct((s, LANES), BF16),
                   jax.ShapeDtypeStruct((512, 1024), F32), jax.ShapeDtypeStruct((256, 1024), F32),
                   jax.ShapeDtypeStruct((1, 512), F32), jax.ShapeDtypeStruct((1, 256), F32),
                   jax.ShapeDtypeStruct((1, 256), F32), jax.ShapeDtypeStruct((1, 256), F32)],
        scratch_shapes=[pltpu.VMEM((tm, 1024), F32), pltpu.VMEM((tm, 1024), F32)],
        compiler_params=_params(("arbitrary",)),
    )(z, z, z, cos_m, sin_m, gq, wq, gkv, wkv, qg, kg, dq, dk, dv)


def _attn_fwd(q, k, v, z, tq, rider=None):
    s = q.shape[1]

    def body(q_ref, k_ref, v_ref, g_ref, o_ref, y_ref, lse_ref):
        sc = _dot(q_ref[...], k_ref[...], NT)
        m = jnp.max(sc, axis=-1, keepdims=True)
        p = jnp.exp2(sc - m)
        l = jnp.sum(p, axis=-1, keepdims=True)
        o = _dot(p.astype(BF16), v_ref[...]) / l
        o_ref[...] = o
        y_ref[...] = (_silu(g_ref[...]) * o).astype(BF16)
        lse_ref[...] = m + jnp.log2(l)

    return _call(
        body, "attn_fwd", (4, s // tq),
        [pl.BlockSpec((None, tq, 256), lambda h, i: (h, i, 0)),
         pl.BlockSpec((None, s, 256), lambda h, i: (h, 0, 0)),
         pl.BlockSpec((None, s, LANES), lambda h, i: (h, 0, 0)),
         pl.BlockSpec((tq, LANES), lambda h, i: (i, M_G + h))],
        [pl.BlockSpec((tq, LANES), lambda h, i: (i, h)), pl.BlockSpec((tq, LANES), lambda h, i: (i, h)),
         pl.BlockSpec((None, tq, 1), lambda h, i: (h, i, 0))],
        [jax.ShapeDtypeStruct((s, GROUP_W), F32), jax.ShapeDtypeStruct((s, GROUP_W), BF16),
         jax.ShapeDtypeStruct((4, s, 1), F32)],
        [], ("parallel", "parallel"), (q, k, v, z), rider)


def _attn_bwd(q, k, v, z, o, lse, dy, tq, rider=None):
    s = q.shape[1]

    def body(q_ref, k_ref, v_ref, g_ref, o_ref, lse_ref, dy_ref, dq_ref, dk_ref, dv_ref, dg_ref):
        @pl.when(pl.program_id(1) == 0)
        def _():
            dk_ref[...] = jnp.zeros_like(dk_ref)
            dv_ref[...] = jnp.zeros_like(dv_ref)

        gate, ov, dyv = g_ref[...], o_ref[...], dy_ref[...]
        do = dyv * _silu(gate)
        dg_ref[...] = (dyv * ov * _silu_grad(gate)).astype(BF16)
        delta = jnp.sum(do * ov, axis=-1, keepdims=True)
        dob = do.astype(BF16)
        qb, kb = q_ref[...], k_ref[...]
        p = jnp.exp2(_dot(qb, kb, NT) - lse_ref[...])
        dp = _dot(dob, v_ref[...], NT)
        ds = (p * (dp - delta)).astype(BF16)
        dq_ref[...] = _dot(ds, kb)
        dk_ref[...] += _dot(ds, qb, TN)
        dv_ref[...] += _dot(p.astype(BF16), dob, TN)

    return _call(
        body, "attn_bwd", (4, s // tq),
        [pl.BlockSpec((None, tq, 256), lambda h, i: (h, i, 0)),
         pl.BlockSpec((None, s, 256), lambda h, i: (h, 0, 0)),
         pl.BlockSpec((None, s, LANES), lambda h, i: (h, 0, 0)),
         pl.BlockSpec((tq, LANES), lambda h, i: (i, M_G + h)),
         pl.BlockSpec((tq, LANES), lambda h, i: (i, h)),
         pl.BlockSpec((None, tq, 1), lambda h, i: (h, i, 0)),
         pl.BlockSpec((tq, LANES), lambda h, i: (i, 12 + h))],
        [pl.BlockSpec((None, tq, 256), lambda h, i: (h, i, 0)),
         pl.BlockSpec((None, s, 256), lambda h, i: (h, 0, 0)),
         pl.BlockSpec((None, s, LANES), lambda h, i: (h, 0, 0)),
         pl.BlockSpec((tq, LANES), lambda h, i: (i, h))],
        [jax.ShapeDtypeStruct((4, s, 256), F32), jax.ShapeDtypeStruct((4, s, 256), F32),
         jax.ShapeDtypeStruct((4, s, LANES), F32), jax.ShapeDtypeStruct((s, GROUP_W), BF16)],
        [], ("parallel", "arbitrary"), (q, k, v, z, o, lse, dy), rider)


def _adam(parts, w, m, v, name, tr):
    r, c = w.shape
    tr = min(tr, r)
    c1 = 1.0 - ADAM_B1 ** ADAM_STEP
    c2 = 1.0 - ADAM_B2 ** ADAM_STEP

    def body(p_ref, w_ref, m_ref, v_ref, g_ref, d_ref, nm_ref, nv_ref):
        g = p_ref[0].astype(F32)
        for i in range(1, N_DEV):
            g = g + p_ref[i].astype(F32)
        nm = ADAM_B1 * m_ref[...] + (1.0 - ADAM_B1) * g
        nv = ADAM_B2 * v_ref[...] + (1.0 - ADAM_B2) * (g * g)
        g_ref[...] = g
        nm_ref[...] = nm
        nv_ref[...] = nv
        d_ref[...] = -ADAM_LR * ((nm / c1) / (jnp.sqrt(nv / c2) + ADAM_EPS) + ADAM_WD * w_ref[...])

    blk = lambda: pl.BlockSpec((tr, c), lambda i: (i, 0))
    return pl.pallas_call(
        body, name=name, grid=(r // tr,),
        in_specs=[pl.BlockSpec((N_DEV, tr, c), lambda i: (0, i, 0)), blk(), blk(), blk()],
        out_specs=[blk(), blk(), blk(), blk()],
        out_shape=[jax.ShapeDtypeStruct((r, c), F32)] * 4,
        compiler_params=_params(("parallel",)),
    )(parts, w, m, v)


def _adam_columns(parts, w, m, v, name, tc, rider=None):
    nl, r, c = w.shape
    pieces = [p for layer in parts for p in layer]
    nh = len(parts[0])
    rp = pieces[0].shape[2]
    tc = min(tc, rp)
    ncb = rp // tc
    c1 = 1.0 - ADAM_B1 ** ADAM_STEP
    c2 = 1.0 - ADAM_B2 ** ADAM_STEP

    def body(*refs):
        p_refs, (w_ref, m_ref, v_ref, g_ref, d_ref, nm_ref, nv_ref) = refs[:len(pieces)], refs[len(pieces):]
        for h in range(nh):
            @pl.when(pl.program_id(0) == h)
            def _(h=h):
                for l in range(nl):
                    p_ref = p_refs[l * nh + h]
                    g = p_ref[0].astype(F32)
                    for i in range(1, N_DEV):
                        g = g + p_ref[i].astype(F32)
                    nm = ADAM_B1 * m_ref[:, l, :] + (1.0 - ADAM_B1) * g
                    nv = ADAM_B2 * v_ref[:, l, :] + (1.0 - ADAM_B2) * (g * g)
                    g_ref[:, l, :] = g
                    nm_ref[:, l, :] = nm
                    nv_ref[:, l, :] = nv
                    d_ref[:, l, :] = -ADAM_LR * ((nm / c1) / (jnp.sqrt(nv / c2) + ADAM_EPS) + ADAM_WD * w_ref[:, l, :])

    def part_spec(j):
        return pl.BlockSpec((N_DEV, c, tc), lambda h, i: (0, 0, jnp.clip((h - j % nh) * ncb + i, 0, ncb - 1)))

    blk = lambda: pl.BlockSpec((c, nl, tc), lambda h, i: (0, 0, h * ncb + i))
    t = lambda a: jnp.transpose(a, (2, 0, 1))
    *res, = _call(body, name, (nh, ncb), [part_spec(j) for j in range(len(pieces))] + [blk(), blk(), blk()],
                  [blk(), blk(), blk(), blk()], [jax.ShapeDtypeStruct((c, nl, r), F32)] * 4, [],
                  ("arbitrary",) * 2, (*pieces, t(w), t(m), t(v)), rider)
    return [jnp.transpose(a, (1, 2, 0)) for a in res[:4]] + res[4:]


def _adam_layers(parts, w, m, v, name, tr, rider=None):
    nl, r, c = w.shape
    pieces = [p for layer in parts for p in layer]
    rp = pieces[0].shape[1]
    tr = min(tr, rp)
    nr, nrp = r // tr, rp // tr
    c1 = 1.0 - ADAM_B1 ** ADAM_STEP
    c2 = 1.0 - ADAM_B2 ** ADAM_STEP

    def body(*refs):
        p_refs, (w_ref, m_ref, v_ref, g_ref, d_ref, nm_ref, nv_ref) = refs[:len(pieces)], refs[len(pieces):]
        at = pl.program_id(0) * nr + pl.program_id(1)
        for j in range(len(pieces)):
            @pl.when(jnp.logical_and(at >= j * nrp, at < (j + 1) * nrp))
            def _(p_ref=p_refs[j]):
                g = p_ref[0].astype(F32)
                for i in range(1, N_DEV):
                    g = g + p_ref[i].astype(F32)
                nm = ADAM_B1 * m_ref[...] + (1.0 - ADAM_B1) * g
                nv = ADAM_B2 * v_ref[...] + (1.0 - ADAM_B2) * (g * g)
                g_ref[...] = g
                nm_ref[...] = nm
                nv_ref[...] = nv
                d_ref[...] = -ADAM_LR * ((nm / c1) / (jnp.sqrt(nv / c2) + ADAM_EPS) + ADAM_WD * w_ref[...])

    def part_spec(j):
        return pl.BlockSpec((N_DEV, tr, c), lambda ll, i: (0, jnp.clip(ll * nr + i - j * nrp, 0, nrp - 1), 0))

    blk = lambda: pl.BlockSpec((None, tr, c), lambda ll, i: (ll, i, 0))
    return _call(body, name, (nl, nr), [part_spec(j) for j in range(len(pieces))] + [blk(), blk(), blk()],
                 [blk(), blk(), blk(), blk()], [jax.ShapeDtypeStruct((nl, r, c), F32)] * 4, [],
                 ("arbitrary", "arbitrary"), (*pieces, w, m, v), rider)


REPLICATED = ("norm_g", "ret_norm_g", "gla_ba_f", "gla_ba_b", "gla_norm_g", "pool_w", "pool_scale",
              "mla_q_norm_g", "mla_kv_norm_g", "mla_qk_norm_q", "mla_qk_norm_k")
REPLICATED_EARLY = REPLICATED[1:]
SMALL_SHARDED = ("mla_wq_b", "mla_wkv_b", "gla_wa2_f", "gla_wa2_b")
WEIGHTS = ("norm_g", "w_in", "ret_norm_g", "gla_wa2_f", "gla_ba_f", "gla_wa2_b", "gla_ba_b", "gla_norm_g", "pool_w",
           "pool_scale", "mla_q_norm_g", "mla_wq_b", "mla_kv_norm_g", "mla_wkv_b", "mla_qk_norm_q", "mla_qk_norm_k",
           "w_out")


PACK_ROWS = 16


def _packed_rows(a):
    rows = a.size // LANES
    return rows, -(-rows // PACK_ROWS) * PACK_ROWS


def _pack(arrays, dtype):
    parts = []
    for a in arrays:
        rows, padded = _packed_rows(a)
        parts.append(jnp.pad(a.reshape(rows, LANES).astype(dtype), ((0, padded - rows), (0, 0))))
    return jnp.concatenate(parts, axis=0)


def _unpack(packed, like):
    out, at = [], 0
    for a in like:
        rows, padded = _packed_rows(a)
        out.append(packed[..., at:at + rows, :].reshape(packed.shape[:-2] + a.shape))
        at += padded
    return out


def _columns_by_device(g):
    l, r, n = g.shape
    return g.reshape(l, r, N_DEV, n // N_DEV).transpose(2, 0, 1, 3)


def _gathered_columns(g, l, r, c):
    return g.reshape(N_DEV, l, r, c).transpose(1, 2, 0, 3).reshape(l, r, N_DEV * c)


def _layer_forward(x, wts, late_wts, tables, tm, tq, ride_inproj=None, ride_attn=None, target=None):
    cos_r, sin_r, cos_m, sin_m, tab, _ = tables
    z, h, *carried_in = _inproj(x, wts["norm_g"], wts["w_in"], min(x.shape[0], 2 * tm), rider=ride_inproj)
    wts.update(late_wts(carried_in))
    o_a, y_a = _ret_fwd(z, cos_r, sin_r, tab, wts["ret_norm_g"])
    o_b, y_b = _gla_fwd(z, wts["wa_f"], wts["wa_b"], wts["gla_ba_f"], wts["gla_ba_b"], wts["gla_norm_g"])
    y_c = _pool_fwd(z, wts["pool_w"], wts["pool_scale"])
    q, k, v = _mla_prep(z, cos_m, sin_m, wts["mla_q_norm_g"], wts["wq"], wts["mla_kv_norm_g"], wts["wkv"],
                        wts["qk_q"], wts["qk_k"], tm)
    o_d, y_d, lse, *carried_attn = _attn_fwd(q, k, v, z, tq, rider=ride_attn)
    y = jnp.concatenate([y_a, y_b, y_c, y_d], axis=1)
    w_out = wts["w_out"]
    if target is None:
        x_next = _mm(y, w_out, "nn", "outproj", tm, D_MODEL, 1024, add=x)
    else:
        x_next = _mm(y, w_out, "nn", "outproj_loss", tm, D_MODEL, 1024, tail=_loss_tail(x, target))
    saved = dict(x=x, z=z, h=h, o_a=o_a, o_b=o_b, o_d=o_d, lse=lse, q=q, k=k, v=v, y=y, w_out=w_out)
    return x_next, saved, carried_in, carried_attn


def _layer_backward(dx, sv, wts, tables, tm, tq, rides):
    cos_r, sin_r, cos_m, sin_m, tab, tab_sw = tables
    z = sv["z"]
    g = {}
    carried = {}

    def rider(name):
        return rides[name](g) if name in rides else None

    def landed(name, results, n_own):
        if name in rides:
            carried[name] = list(results[n_own:])
        return results[:n_own]

    g["w_out"] = _mm(sv["y"], dx, "tn", "d_w_out", 2048, 1024, 1024, out_dtype=BF16)
    dy = _mm(dx, sv["w_out"], "nt", "d_y", tm, 2048, 1024)

    do_a, dg_a, g["ret_norm_g"] = _normgate_bwd(sv["o_a"], z, A_G, dy, 0, wts["ret_norm_g"], tm)
    dq_a, dk_a, dv_a = landed("ret", _ret_bwd(z, do_a, cos_r, sin_r, tab, tab_sw, rider=rider("ret")), 3)

    do_b, dg_b, g["gla_norm_g"] = _normgate_bwd(sv["o_b"], z, B_G, dy, 1, wts["gla_norm_g"], tm)
    dq_b, dk_b, dv_b, d_ga, d_waf, d_wab, g["gla_ba_f"], g["gla_ba_b"] = landed("gla", _gla_bwd(
        z, do_b, wts["wa_f"], wts["wa_b"], wts["gla_ba_f"], wts["gla_ba_b"], rider=rider("gla")), 8)
    g["gla_wa2_f"] = d_waf[0:GLA_RANK]
    g["gla_wa2_b"] = d_wab[GLA_RANK:2 * GLA_RANK]

    du_c, dg_c, g["pool_w"], g["pool_scale"] = _pool_bwd(z, dy, wts["pool_w"], wts["pool_scale"])

    d_q, d_k, d_v, dg_d = landed("attn", _attn_bwd(sv["q"], sv["k"], sv["v"], z, sv["o_d"], sv["lse"], dy, tq,
                                                   rider=rider("attn")), 4)
    (d_mq, d_mkv, d_mkr, d_wq, g["mla_wkv_b"], g["mla_q_norm_g"], g["mla_kv_norm_g"], d_qg, d_kg) = _mla_prep_bwd(
        z, cos_m, sin_m, wts["mla_q_norm_g"], wts["wq"], wts["mla_kv_norm_g"], wts["wkv"], wts["qk_q"], wts["qk_k"],
        d_q, d_k, d_v, tm)
    g["mla_wq_b"] = _unpad_wq(d_wq)
    g["mla_qk_norm_q"] = d_qg[:, _QK_INV]
    g["mla_qk_norm_k"] = d_kg[:, _QK_INV]

    dz = jnp.concatenate([dq_a, dk_a, dv_a, dg_a, dq_b, dk_b, dv_b, dg_b, d_mq, du_c, dg_c, dg_d, d_mkv,
                          d_ga.astype(BF16), d_mkr], axis=1)
    h, half = sv["h"], D_MODEL // 2
    for name, cols in (("d_w_in_a", h[:, :half]), ("d_w_in_b", h[:, half:])):
        res = _mm(dz, cols, "tn", name, 2048, 1024, 1024, out_dtype=BF16, rider=rider(name))
        (d_wt,) = landed(name, res if name in rides else [res], 1)
        g["w_in" + name[-2:]] = _split_w_in(d_wt)
    dx_in, g["norm_g"] = landed("d_h", _mm(dz, wts["w_in"], "nn", "d_h", tm, D_MODEL, 1024, rider=rider("d_h"),
                                           tail=_norm_bwd_tail(sv["x"], wts["norm_g"], dx)), 2)
    return dx_in, g, carried


def kernel(x, norm_g, w_in, ret_norm_g, gla_wa2_f, gla_ba_f, gla_wa2_b, gla_ba_b, gla_norm_g, pool_w, pool_scale, mla_q_norm_g, mla_wq_b, mla_kv_norm_g, mla_wkv_b, mla_qk_norm_q, mla_qk_norm_k, w_out, loss_target, m_norm_g, m_w_in, m_ret_norm_g, m_gla_wa2_f, m_gla_ba_f, m_gla_wa2_b, m_gla_ba_b, m_gla_norm_g, m_pool_w, m_pool_scale, m_mla_q_norm_g, m_mla_wq_b, m_mla_kv_norm_g, m_mla_wkv_b, m_mla_qk_norm_q, m_mla_qk_norm_k, m_w_out, v_norm_g, v_w_in, v_ret_norm_g, v_gla_wa2_f, v_gla_ba_f, v_gla_wa2_b, v_gla_ba_b, v_gla_norm_g, v_pool_w, v_pool_scale, v_mla_q_norm_g, v_mla_wq_b, v_mla_kv_norm_g, v_mla_wkv_b, v_mla_qk_norm_q, v_mla_qk_norm_k, v_w_out):
    w = dict(norm_g=norm_g, w_in=w_in, ret_norm_g=ret_norm_g, gla_wa2_f=gla_wa2_f, gla_ba_f=gla_ba_f,
             gla_wa2_b=gla_wa2_b, gla_ba_b=gla_ba_b, gla_norm_g=gla_norm_g, pool_w=pool_w, pool_scale=pool_scale,
             mla_q_norm_g=mla_q_norm_g, mla_wq_b=mla_wq_b, mla_kv_norm_g=mla_kv_norm_g, mla_wkv_b=mla_wkv_b,
             mla_qk_norm_q=mla_qk_norm_q, mla_qk_norm_k=mla_qk_norm_k, w_out=w_out)
    m = dict(norm_g=m_norm_g, w_in=m_w_in, ret_norm_g=m_ret_norm_g, gla_wa2_f=m_gla_wa2_f, gla_ba_f=m_gla_ba_f,
             gla_wa2_b=m_gla_wa2_b, gla_ba_b=m_gla_ba_b, gla_norm_g=m_gla_norm_g, pool_w=m_pool_w,
             pool_scale=m_pool_scale, mla_q_norm_g=m_mla_q_norm_g, mla_wq_b=m_mla_wq_b, mla_kv_norm_g=m_mla_kv_norm_g,
             mla_wkv_b=m_mla_wkv_b, mla_qk_norm_q=m_mla_qk_norm_q, mla_qk_norm_k=m_mla_qk_norm_k, w_out=m_w_out)
    v = dict(norm_g=v_norm_g, w_in=v_w_in, ret_norm_g=v_ret_norm_g, gla_wa2_f=v_gla_wa2_f, gla_ba_f=v_gla_ba_f,
             gla_wa2_b=v_gla_wa2_b, gla_ba_b=v_gla_ba_b, gla_norm_g=v_gla_norm_g, pool_w=v_pool_w,
             pool_scale=v_pool_scale, mla_q_norm_g=v_mla_q_norm_g, mla_wq_b=v_mla_wq_b, mla_kv_norm_g=v_mla_kv_norm_g,
             mla_wkv_b=v_mla_wkv_b, mla_qk_norm_q=v_mla_qk_norm_q, mla_qk_norm_k=v_mla_qk_norm_k, w_out=v_w_out)
    xs, target = x[0], loss_target[0]
    s = xs.shape[0]
    tm, tq = min(s, 512), min(s, 256)
    c_in = w_in.shape[2]

    w_in_b = jnp.transpose(w_in, (2, 0, 1)).astype(BF16)
    w_out_b = w_out.astype(BF16).reshape(-1, D_MODEL)
    (w_in_g0,) = _exchange([("gather", w_in_b[:, 0])], "gather_first")
    tables = _rope_tables(s) + _ret_tables()

    def early_weights(l, w_in_g):
        return dict(
            norm_g=norm_g[l][None], w_in=_assemble_w_in(w_in_g), ret_norm_g=ret_norm_g[l][None],
            gla_ba_f=gla_ba_f[l][None], gla_ba_b=gla_ba_b[l][None],
            gla_norm_g=gla_norm_g[l][None], pool_w=pool_w[l], pool_scale=pool_scale[l][None],
            mla_q_norm_g=mla_q_norm_g[l][None], mla_kv_norm_g=mla_kv_norm_g[l][None],
            qk_q=_pad_qk_gain(mla_qk_norm_q[l]), qk_k=_pad_qk_gain(mla_qk_norm_k[l]))

    def late_weights(l, w_out_g, small_g):
        shards = _unpack(small_g, [w[n] for n in SMALL_SHARDED])
        full = {n: _gathered_columns(shards[i], *w[n].shape)[l] for i, n in enumerate(SMALL_SHARDED)}
        wa_f = jnp.zeros((LANES, 2 * LANES), BF16).at[0:GLA_RANK].set(full["gla_wa2_f"])
        wa_b = jnp.zeros((LANES, 2 * LANES), BF16).at[GLA_RANK:2 * GLA_RANK].set(full["gla_wa2_b"])
        return dict(w_out=w_out_g.reshape(N_DEV, DEPTH, -1, D_MODEL)[:, l].reshape(-1, D_MODEL),
                    wa_f=wa_f, wa_b=wa_b, wq=_pad_wq(full["mla_wq_b"]), wkv=full["mla_wkv_b"])

    by_owner = lambda g_w_out: g_w_out.reshape(N_DEV, -1, D_MODEL)

    layers = [early_weights(0, w_in_g0), None]
    x1, sv0, (w_out_g, small_g), (w_in_g1,) = _layer_forward(
        xs, layers[0], lambda got: late_weights(0, *got), tables, tm, tq,
        ride_inproj=[("gather", w_out_b), ("gather", _pack([w[n] for n in SMALL_SHARDED], BF16))],
        ride_attn=("gather", w_in_b[:, 1]))
    layers[1] = early_weights(1, w_in_g1)
    (dx, loss_row), sv1, _, _ = _layer_forward(x1, layers[1], lambda got: late_weights(1, w_out_g, small_g), tables,
                                               tm, tq, target=target)
    loss = lax.psum(loss_row[0, 0], ("x", "y", "c"))

    def small_jobs(g):
        grads = (g, g1)
        full = {n: jnp.stack([grads[l][n].reshape(w[n].shape[1:]) if n in REPLICATED else grads[l][n]
                              for l in range(DEPTH)]) for n in SMALL_SHARDED + REPLICATED_EARLY}
        small_c = jax.vmap(lambda *shards: _pack(shards, F32))(*[_columns_by_device(full[n]) for n in SMALL_SHARDED])
        return [("scatter", small_c),
                ("gather", _pack([full[n] for n in REPLICATED_EARLY], F32))]

    dx, g1, got1 = _layer_backward(dx, sv1, layers[1], tables, tm, tq, {
        "attn": lambda g: ("scatter", by_owner(g["w_out"]))})
    dx, g0, got0 = _layer_backward(dx, sv0, layers[0], tables, tm, tq, {
        "gla": lambda g: ("scatter", g1["w_in_b"]),
        "attn": lambda g: [("scatter", g1["w_in_a"]), ("scatter", by_owner(g["w_out"]))],
        "d_w_in_a": small_jobs,
        "d_w_in_b": lambda g: ("scatter", g["w_in_a"]),
        "d_h": lambda g: ("scatter", g["w_in_b"])})
    in_parts = ((got0["d_w_in_b"][0], got0["d_h"][0]), (got0["attn"][0], got0["gla"][0]))
    out_parts = ((got0["attn"][1],), (got1["attn"][0],))
    small_parts, rep_parts = got0["d_w_in_a"]
    norm_pack = _pack([jnp.stack([g0["norm_g"][0], g1["norm_g"][0]])], F32)

    out = {}
    out["w_in"] = _adam_columns(in_parts, w_in, m_w_in, v_w_in, "adam_w_in", 256)
    *out["w_out"], norm_parts = _adam_layers(out_parts, w_out, m_w_out, v_w_out, "adam_w_out", 128,
                                             rider=("gather", norm_pack))
    for names, parts, label in ((SMALL_SHARDED, small_parts, "adam_small"),
                                (REPLICATED_EARLY, rep_parts, "adam_replicated"), (("norm_g",), norm_parts, "adam_norm")):
        res = _adam(parts, _pack([w[n] for n in names], F32), _pack([m[n] for n in names], F32),
                    _pack([v[n] for n in names], F32), label, 2048)
        for n, *vals in zip(names, *[_unpack(a, [w[n] for n in names]) for a in res]):
            out[n] = vals

    return (loss, dx[None], *[out[n][0] for n in WEIGHTS], *[out[n][1] for n in WEIGHTS],
            *[out[n][2] for n in WEIGHTS], *[out[n][3] for n in WEIGHTS])
```

```python
import functools
import math

import numpy as np
import jax
import jax.numpy as jnp
from jax import lax
from jax.experimental import pallas as pl
from jax.experimental.pallas import tpu as pltpu

F32 = jnp.float32
BF16 = jnp.bfloat16

N_DEV = 8
D_MODEL = 2048
DEPTH = 2
GROUP_W = 512
EPS = 1e-6
ROPE_THETA = 10000.0
LANES = 128

RET_HD = 128
RET_CHUNK = 256
RET_UNROLL = 4
GLA_CHUNK = 64
GLA_UNROLL = 8
GLA_CUM_ROWS = 256
GLA_DK = 64
GLA_TAU = 16.0
GLA_RANK = 16
POOL_WINDOWS = (2, 4, 8, 16)
MLA_QK = 192
MLA_ROPE = 64
ATTN_SCALE = MLA_QK ** -0.5
ATTN_Q_SCALE = ATTN_SCALE * math.log2(math.e)
IN_COLS = 5984

ADAM_LR = 0.001
ADAM_B1 = 0.9
ADAM_B2 = 0.999
ADAM_EPS = 1e-08
ADAM_WD = 0.01
ADAM_STEP = 10

A_Q, A_K, A_V, A_G = 0, 4, 8, 12
B_Q, B_K, B_V, B_G = 16, 18, 20, 24
M_Q, C_V, C_G, M_G = 28, 32, 36, 40
M_KV, GA, M_KR = 44, 46, 47
ZP_COLS = 48 * LANES

VMEM_LIMIT = 56 * 1024 * 1024


def _params(sem, vmem=VMEM_LIMIT):
    return pltpu.CompilerParams(dimension_semantics=sem, vmem_limit_bytes=vmem)


def _sigmoid(x):
    return 1.0 / (1.0 + jnp.exp(-x))


def _silu(x):
    return x * _sigmoid(x)


def _silu_grad(x):
    s = _sigmoid(x)
    return s * (1.0 + x * (1.0 - s))


def _dot(a, b, dims=(((1,), (0,)), ((), ()))):
    return lax.dot_general(a, b, dims, preferred_element_type=F32)


NT = (((1,), (1,)), ((), ()))
TN = (((0,), (0,)), ((), ()))


def _chunk_loop(n, body, init, unroll):
    unroll = math.gcd(n, unroll)

    def trip(t, carry):
        for u in range(unroll):
            carry = body(t * unroll + u, carry)
        return carry

    return lax.fori_loop(0, n // unroll, trip, init)


def _roll_lanes_half(x):
    return pltpu.roll(x, 64, 1)


def _wq_perm():
    idx = np.zeros((1024,), np.int32)
    ok = np.zeros((1024,), bool)
    for h in range(4):
        idx[128 * h:128 * h + 128] = 192 * h + np.arange(128)
        ok[128 * h:128 * h + 128] = True
        base = 512 + 128 * h
        idx[base:base + 32] = 192 * h + 128 + np.arange(32)
        ok[base:base + 32] = True
        idx[base + 64:base + 96] = 192 * h + 160 + np.arange(32)
        ok[base + 64:base + 96] = True
    inv = np.zeros((768,), np.int32)
    inv[idx[ok]] = np.nonzero(ok)[0]
    return idx, ok, inv


_WQ_IDX, _WQ_OK, _WQ_INV = _wq_perm()


def _pad_wq(wq):
    return jnp.where(jnp.asarray(_WQ_OK)[None, :], wq[:, _WQ_IDX], 0).astype(wq.dtype)


def _unpad_wq(wqp):
    return wqp[:, _WQ_INV]


def _qk_idx():
    idx = np.zeros((256,), np.int32)
    ok = np.zeros((256,), bool)
    idx[0:128] = np.arange(128)
    ok[0:128] = True
    idx[128:160] = 128 + np.arange(32)
    ok[128:160] = True
    idx[192:224] = 160 + np.arange(32)
    ok[192:224] = True
    inv = np.zeros((192,), np.int32)
    inv[idx[ok]] = np.nonzero(ok)[0]
    return idx, ok, inv


_QK_IDX, _QK_OK, _QK_INV = _qk_idx()


def _pad_qk_gain(g):
    return jnp.where(jnp.asarray(_QK_OK), g[_QK_IDX], 0.0).reshape(1, 256)


def _rope_tables(s):
    def tabs(dim):
        inv = 1.0 / (ROPE_THETA ** (jnp.arange(0, dim, 2, dtype=F32) / dim))
        ang = jnp.arange(s, dtype=F32)[:, None] * inv[None, :]
        return jnp.cos(ang), jnp.sin(ang)
    cr, sr = tabs(RET_HD)
    cos_r = jnp.concatenate([cr, cr], axis=1)
    sin_r = jnp.concatenate([-sr, sr], axis=1)
    cm, sm = tabs(MLA_ROPE)
    zz = jnp.zeros_like(cm)
    cos_m = jnp.concatenate([cm, zz, cm, zz], axis=1)
    sin_m = jnp.concatenate([-sm, zz, sm, zz], axis=1)
    return cos_r, sin_r, cos_m, sin_m


def _rope(x, cos, sin):
    return x * cos + _roll_lanes_half(x) * sin


def _rope_t(x, cos, sin):
    return x * cos + _roll_lanes_half(x * sin)


def _ret_tables():
    c = RET_CHUNK
    gamma_f = 1.0 - 2.0 ** (-5.0 - jnp.arange(4, dtype=F32))
    gamma_b = gamma_f[::-1]
    idx = jnp.arange(c, dtype=F32)
    diff = idx[:, None] - idx[None, :]

    def build(g1, g2):
        l1 = jnp.log(g1)[:, None, None]
        l2 = jnp.log(g2)[:, None, None]
        d1 = jnp.where(diff >= 0, jnp.exp(jnp.maximum(diff, 0.0)[None] * l1), 0.0)
        d2 = jnp.where(diff <= 0, jnp.exp(jnp.maximum(-diff, 0.0)[None] * l2), 0.0)
        ones = jnp.ones((1, c, LANES), F32)
        col = idx[None, :, None]
        qdf = jnp.exp((col + 1.0) * l1) * ones
        kdf = jnp.exp((c - 1.0 - col) * l1) * ones
        qdb = jnp.exp((c - col) * l2) * ones
        kdb = jnp.exp(col * l2) * ones
        cd1 = jnp.exp(c * l1) * ones
        cd2 = jnp.exp(c * l2) * ones
        return jnp.concatenate([d1 + d2, qdf, kdf, qdb, kdb, cd1, cd2], axis=2)

    return build(gamma_f, gamma_b), build(gamma_b, gamma_f)


MESH = pl.DeviceIdType.MESH
ANY = pl.BlockSpec(memory_space=pl.ANY)
_RELATIONS = ((0, 0, 1), (1, 0, 0), (0, 1, 0), (1, 1, 0), (1, 0, 1), (0, 1, 1), (1, 1, 1))


def _position():
    return lax.axis_index("x"), lax.axis_index("y"), lax.axis_index("c")


def _gather_copies(x_ref, out_ref, send_sems, recv_sems, local_sem, starting):
    x, y, cc = _position()
    me, sibling = (x, y, cc), (x, y, 1 - cc)
    chips = [(1 - x, y), (x, 1 - y), (1 - x, 1 - y)]

    def slab(px, py, pc):
        return out_ref.at[4 * px + 2 * py + pc]

    def copy(k, block, to, src=None):
        return pltpu.make_async_remote_copy(
            src_ref=slab(*block) if src is None else src, dst_ref=slab(*block),
            send_sem=send_sems.at[k], recv_sem=recv_sems.at[k], device_id=to, device_id_type=MESH)

    mine = pltpu.make_async_copy(x_ref, slab(*me), local_sem)
    first = [copy(0, me, sibling, src=x_ref)] + [copy(1 + j, me, (*chip, cc), src=x_ref) for j, chip in enumerate(chips)]
    if starting:
        return mine, first
    passed = [copy(4 + j, (*chip, cc), sibling) for j, chip in enumerate(chips)]
    arrivals = [copy(1 + j, (*chip, cc), me) for j, chip in enumerate(chips)]
    late = [copy(0, sibling, me)] + [copy(4 + j, (*chip, 1 - cc), me) for j, chip in enumerate(chips)]
    return mine, first, passed, arrivals, late


def _gather_start(*refs):
    mine, first = _gather_copies(*refs, starting=True)
    mine.start()
    for cp in first:
        cp.start()


def _gather_finish(*refs):
    mine, first, passed, arrivals, late = _gather_copies(*refs, starting=False)
    for arrived, onward in zip(arrivals, passed):
        arrived.wait_recv()
        onward.start()
    for cp in late:
        cp.wait_recv()
    for cp in first + passed:
        cp.wait_send()
    mine.wait()


def _scatter_copies(c_ref, out_ref, send_sems, recv_sems, local_sem):
    x, y, cc = _position()
    me = 4 * x + 2 * y + cc
    mine = pltpu.make_async_copy(c_ref.at[me], out_ref.at[me], local_sem)
    copies = []
    for k, (fx, fy, fc) in enumerate(_RELATIONS):
        px = 1 - x if fx else x
        py = 1 - y if fy else y
        pc = 1 - cc if fc else cc
        copies.append(pltpu.make_async_remote_copy(
            src_ref=c_ref.at[4 * px + 2 * py + pc], dst_ref=out_ref.at[me],
            send_sem=send_sems.at[k], recv_sem=recv_sems.at[k], device_id=(px, py, pc), device_id_type=MESH))
    return mine, copies


def _scatter_start(*refs):
    mine, copies = _scatter_copies(*refs)
    mine.start()
    for cp in copies:
        cp.start()


def _scatter_finish(*refs):
    mine, copies = _scatter_copies(*refs)
    for cp in copies:
        cp.wait()
    mine.wait()


_EXCHANGES = {"gather": (_gather_start, _gather_finish), "scatter": (_scatter_start, _scatter_finish)}


def _exchange_scratch():
    return [pltpu.SemaphoreType.DMA((7,)), pltpu.SemaphoreType.DMA((7,)), pltpu.SemaphoreType.DMA]


def _exchange_out(kind, src):
    return jax.ShapeDtypeStruct(((N_DEV,) + src.shape) if kind == "gather" else src.shape, src.dtype)


def _exchange(jobs, name):
    n = len(jobs)

    def body(*refs):
        srcs, outs, sems = refs[:n], refs[n:2 * n], refs[2 * n:]
        for half in (0, 1):
            for i, (kind, _) in enumerate(jobs):
                _EXCHANGES[kind][half](srcs[i], outs[i], *sems[3 * i:3 * i + 3])

    return pl.pallas_call(
        body, name=name, out_shape=[_exchange_out(kind, src) for kind, src in jobs],
        in_specs=[ANY] * n, out_specs=[ANY] * n,
        scratch_shapes=[sem for _ in jobs for sem in _exchange_scratch()])(*[src for _, src in jobs])


def _call(body, name, grid, in_specs, out_specs, out_shape, scratch, sem, args, rider=None):
    if rider is None:
        return pl.pallas_call(body, name=name, grid=grid, in_specs=in_specs, out_specs=out_specs, out_shape=out_shape,
                              scratch_shapes=scratch, compiler_params=_params(sem))(*args)
    jobs = rider if isinstance(rider, list) else [rider]
    ni, no, ns, nj = len(in_specs), len(out_specs), len(scratch), len(jobs)

    def carried(*refs):
        ins, rsrcs = refs[:ni], refs[ni:ni + nj]
        outs, routs = refs[ni + nj:ni + nj + no], refs[ni + nj + no:ni + 2 * nj + no]
        scr, sems = refs[ni + 2 * nj + no:ni + 2 * nj + no + ns], refs[ni + 2 * nj + no + ns:]
        ids = [pl.program_id(a) for a in range(len(grid))]
        is_first = functools.reduce(jnp.logical_and, [i == 0 for i in ids])
        is_last = functools.reduce(jnp.logical_and, [i == g - 1 for i, g in zip(ids, grid)])

        def half(which):
            for j, (kind, _) in enumerate(jobs):
                _EXCHANGES[kind][which](rsrcs[j], routs[j], *sems[3 * j:3 * j + 3])

        @pl.when(is_first)
        def _():
            half(0)

        body(*ins, *outs, *scr)

        @pl.when(is_last)
        def _():
            half(1)

    return pl.pallas_call(
        carried, name=name, grid=grid, in_specs=list(in_specs) + [ANY] * nj, out_specs=list(out_specs) + [ANY] * nj,
        out_shape=list(out_shape) + [_exchange_out(kind, src) for kind, src in jobs],
        scratch_shapes=list(scratch) + [sem for _ in jobs for sem in _exchange_scratch()],
        compiler_params=_params(("arbitrary",) * len(grid)))(*args, *[src for _, src in jobs])


def _inproj(x, g, wt, tm, tn=512, rider=None):
    s, d = x.shape
    n = wt.shape[0]

    def body(x_ref, g_ref, w_ref, z_ref, h_ref, hs):
        @pl.when(pl.program_id(1) == 0)
        def _():
            xv = x_ref[...]
            r = lax.rsqrt(jnp.mean(xv * xv, axis=-1, keepdims=True) + EPS)
            hv = (xv * r * g_ref[...]).astype(BF16)
            hs[...] = hv
            h_ref[...] = hv
        z_ref[...] = _dot(hs[...], w_ref[...], NT)

    return _call(
        body, "inproj", (s // tm, n // tn),
        [pl.BlockSpec((tm, d), lambda i, j: (i, 0)),
         pl.BlockSpec((1, d), lambda i, j: (0, 0)),
         pl.BlockSpec((tn, d), lambda i, j: (j, 0))],
        [pl.BlockSpec((tm, tn), lambda i, j: (i, j)), pl.BlockSpec((tm, d), lambda i, j: (i, 0))],
        [jax.ShapeDtypeStruct((s, n), F32), jax.ShapeDtypeStruct((s, d), BF16)],
        [pltpu.VMEM((tm, d), BF16)], ("parallel", "arbitrary"), (x, g, wt), rider)


def _relayout_plan():
    runs = ((0, 3584, 0), (3584, 3616, GA * LANES), (3616, 4640, C_V * LANES), (4640, 5152, M_Q * LANES),
            (5152, 5408, M_KV * LANES), (5408, 5440, M_KR * LANES), (5440, 5472, M_KR * LANES + 64),
            (5472, 5984, M_G * LANES))
    shard = IN_COLS // N_DEV
    plan = []
    for d in range(N_DEV):
        lo, hi = shard * d, shard * (d + 1)
        for a, b, p in runs:
            s, e = max(a, lo), min(b, hi)
            if s < e:
                plan.append((d, s - lo, p + (s - a), e - s))
    return plan


def _assemble_w_in(g, tc=512):
    _, c, r = g.shape
    tc = min(tc, r)

    def body(g_ref, o_ref):
        o_ref[...] = jnp.zeros_like(o_ref)
        for d, at, to, w in _relayout_plan():
            o_ref[to:to + w, :] = g_ref[d, at:at + w, :]

    return pl.pallas_call(
        body, name="assemble_w_in", grid=(r // tc,),
        in_specs=[pl.BlockSpec((N_DEV, c, tc), lambda i: (0, 0, i))],
        out_specs=pl.BlockSpec((ZP_COLS, tc), lambda i: (0, i)),
        out_shape=jax.ShapeDtypeStruct((ZP_COLS, r), g.dtype),
        compiler_params=_params(("parallel",)),
    )(g)


def _split_w_in(wt, tc=256):
    r = wt.shape[1]
    c = IN_COLS // N_DEV
    tc = min(tc, r)

    def body(w_ref, o_ref):
        for d, at, to, w in _relayout_plan():
            o_ref[d, at:at + w, :] = w_ref[to:to + w, :]

    return pl.pallas_call(
        body, name="split_w_in", grid=(r // tc,),
        in_specs=[pl.BlockSpec((ZP_COLS, tc), lambda i: (0, i))],
        out_specs=pl.BlockSpec((N_DEV, c, tc), lambda i: (0, 0, i)),
        out_shape=jax.ShapeDtypeStruct((N_DEV, c, r), wt.dtype),
        compiler_params=_params(("parallel",)),
    )(wt)


def _mm(a, b, mode, name, tm, tn, tk, add=None, out_dtype=F32, rider=None, tail=None):
    if mode == "tn":
        k, m = a.shape
    else:
        m, k = a.shape
    n = b.shape[0] if mode == "nt" else b.shape[1]
    tm, tn, tk = min(tm, m), min(tn, n), min(tk, k)
    nk = k // tk
    dims = {"nn": (((1,), (0,)), ((), ())), "nt": NT, "tn": TN}[mode]
    if tail is None:
        def plain(acc, i, extra_refs, out_refs):
            out_refs[0][...] = (acc + extra_refs[0][...] if extra_refs else acc).astype(out_dtype)
        tail = ([(add, "tile")] if add is not None else [], [(out_dtype, "tile")], plain)
    extra, outs, fn = tail
    spec = {"tile": pl.BlockSpec((tm, tn), lambda i, j, kk: (i, j)),
            "row": pl.BlockSpec((1, tn), lambda i, j, kk: (0, j)),
            "lanes": pl.BlockSpec((1, LANES), lambda i, j, kk: (0, 0))}
    shape = {"tile": (m, n), "row": (1, n), "lanes": (1, LANES)}
    ne, no = len(extra), len(outs)

    def body(*refs):
        a_ref, b_ref = refs[:2]
        extra_refs, out_refs, acc = refs[2:2 + ne], refs[2 + ne:2 + ne + no], refs[2 + ne + no]
        i, kk = pl.program_id(0), pl.program_id(2)

        @pl.when(kk == 0)
        def _():
            acc[...] = jnp.zeros_like(acc)

        acc[...] += _dot(a_ref[...].astype(BF16), b_ref[...].astype(BF16), dims)

        @pl.when(kk == nk - 1)
        def _():
            fn(acc[...], i, extra_refs, out_refs)

    a_spec = (pl.BlockSpec((tk, tm), lambda i, j, kk: (kk, i)) if mode == "tn"
              else pl.BlockSpec((tm, tk), lambda i, j, kk: (i, kk)))
    b_spec = (pl.BlockSpec((tn, tk), lambda i, j, kk: (j, kk)) if mode == "nt"
              else pl.BlockSpec((tk, tn), lambda i, j, kk: (kk, j)))
    summed = any(kind != "tile" for _, kind in outs)
    res = _call(body, name, (m // tm, n // tn, nk), [a_spec, b_spec] + [spec[kind] for _, kind in extra],
                [spec[kind] for _, kind in outs], [jax.ShapeDtypeStruct(shape[kind], dt) for dt, kind in outs],
                [pltpu.VMEM((tm, tn), F32)], ("arbitrary",) * 3 if summed else ("parallel", "parallel", "arbitrary"),
                [a, b] + [arr for arr, _ in extra], rider)
    return res[0] if (rider is None and no == 1) else res


def _norm_bwd_tail(x, g, dres):
    def fn(dh, i, extra_refs, out_refs):
        x_ref, g_ref, dres_ref = extra_refs
        dx_ref, dg_ref = out_refs

        @pl.when(i == 0)
        def _():
            dg_ref[...] = jnp.zeros_like(dg_ref)

        xv = x_ref[...]
        r = lax.rsqrt(jnp.mean(xv * xv, axis=-1, keepdims=True) + EPS)
        nv = xv * r
        dg_ref[...] += jnp.sum(dh * nv, axis=0, keepdims=True)
        u = dh * g_ref[...]
        dx_ref[...] = dres_ref[...] + r * (u - nv * jnp.mean(u * nv, axis=-1, keepdims=True))

    return [(x, "tile"), (g, "row"), (dres, "tile")], [(F32, "tile"), (F32, "row")], fn


def _loss_tail(x, target):
    d = x.shape[1]

    def fn(acc, i, extra_refs, out_refs):
        x_ref, t_ref = extra_refs
        dx_ref, loss_ref = out_refs

        @pl.when(i == 0)
        def _():
            loss_ref[...] = jnp.zeros_like(loss_ref)

        err = acc + x_ref[...] - t_ref[...]
        dx_ref[...] = err * (1.0 / d)
        per_tok = jnp.mean(err * err, axis=-1, keepdims=True)
        loss_ref[...] += 0.5 * jnp.sum(per_tok, axis=0, keepdims=True)

    return [(x, "tile"), (target, "tile")], [(F32, "tile"), (F32, "lanes")], fn


def _ret_core(q_ref, k_ref, v_ref, tab_ref, out_ref, back_ref, nchunk):
    c = RET_CHUNK

    def rows(n):
        return pl.ds(pl.multiple_of(n * c, c), c)

    zero = jnp.zeros((LANES, LANES), F32)

    def plane(i, n=c):
        return tab_ref[0:n, c + LANES * i:c + LANES * (i + 1)]

    def fwd(n, st):
        r = rows(n)
        q, k, vb = q_ref[r, :], k_ref[r, :], v_ref[r, :].astype(BF16)
        sc = _dot(q.astype(BF16), k.astype(BF16), NT) * tab_ref[:, 0:c]
        o = _dot(sc.astype(BF16), vb)
        o = o + _dot((q * plane(0)).astype(BF16), st.astype(BF16))
        out_ref[r, :] = o
        return st * plane(4, LANES) + _dot((k * plane(1)).astype(BF16), vb, TN)

    def bwd(i, st):
        r = rows(nchunk - 1 - i)
        q, k, vb = q_ref[r, :], k_ref[r, :], v_ref[r, :].astype(BF16)
        back_ref[r, :] = _dot((q * plane(2)).astype(BF16), st.astype(BF16))
        return st * plane(5, LANES) + _dot((k * plane(3)).astype(BF16), vb, TN)

    def both(i, states):
        return fwd(i, states[0]), bwd(i, states[1])

    _chunk_loop(nchunk, both, (zero, zero), RET_UNROLL)
    out_ref[...] += back_ref[...]


def _ret_fwd(z, cos_r, sin_r, tab, norm_g):
    s = z.shape[0]
    nchunk = s // RET_CHUNK
    scale = RET_HD ** -0.5
    col = lambda base: pl.BlockSpec((s, LANES), lambda h: (0, base + h), pipeline_mode=pl.Buffered(1))

    def body(q_ref, k_ref, v_ref, g_ref, cos_ref, sin_ref, tab_ref, ng_ref, o_ref, y_ref, qh, kh, back):
        qh[...] = _rope(q_ref[...], cos_ref[...], sin_ref[...])
        kh[...] = _rope(k_ref[...], cos_ref[...], sin_ref[...]) * scale
        _ret_core(qh, kh, v_ref, tab_ref, o_ref, back, nchunk)
        o = o_ref[...]
        r = lax.rsqrt(jnp.mean(o * o, axis=-1, keepdims=True) + EPS)
        y_ref[...] = (_silu(g_ref[...]) * (o * r * ng_ref[...])).astype(BF16)

    return pl.pallas_call(
        body, name="ret_fwd", grid=(4,),
        in_specs=[col(A_Q), col(A_K), col(A_V), col(A_G),
                  pl.BlockSpec((s, LANES), lambda h: (0, 0), pipeline_mode=pl.Buffered(1)),
                  pl.BlockSpec((s, LANES), lambda h: (0, 0), pipeline_mode=pl.Buffered(1)),
                  pl.BlockSpec((None, RET_CHUNK, RET_CHUNK + 6 * LANES), lambda h: (h, 0, 0)),
                  pl.BlockSpec((1, LANES), lambda h: (0, h))],
        out_specs=[pl.BlockSpec((s, LANES), lambda h: (0, h)), pl.BlockSpec((s, LANES), lambda h: (0, h))],
        out_shape=[jax.ShapeDtypeStruct((s, GROUP_W), F32), jax.ShapeDtypeStruct((s, GROUP_W), BF16)],
        scratch_shapes=[pltpu.VMEM((s, LANES), F32)] * 3,
        compiler_params=_params(("arbitrary",)),
    )(z, z, z, z, cos_r, sin_r, tab, norm_g)


def _ret_bwd(z, d_o, cos_r, sin_r, tab, tab_sw, rider=None):
    s = z.shape[0]
    nchunk = s // RET_CHUNK
    scale = RET_HD ** -0.5
    col = lambda base: pl.BlockSpec((s, LANES), lambda h: (0, base + h), pipeline_mode=pl.Buffered(1))
    whole = lambda: pl.BlockSpec((s, LANES), lambda h: (0, 0), pipeline_mode=pl.Buffered(1))
    tabspec = lambda: pl.BlockSpec((None, RET_CHUNK, RET_CHUNK + 6 * LANES), lambda h: (h, 0, 0))
    outspec = lambda: pl.BlockSpec((s, LANES), lambda h: (0, h))

    def body(q_ref, k_ref, v_ref, do_ref, cos_ref, sin_ref, tab_ref, tsw_ref, dq_ref, dk_ref, dv_ref,
             qh, kh, tmp, back):
        cos, sin = cos_ref[...], sin_ref[...]
        qh[...] = _rope(q_ref[...], cos, sin)
        kh[...] = _rope(k_ref[...], cos, sin) * scale
        _ret_core(kh, qh, do_ref, tsw_ref, tmp, back, nchunk)
        dv_ref[...] = tmp[...].astype(BF16)
        _ret_core(do_ref, v_ref, kh, tab_ref, tmp, back, nchunk)
        dq_ref[...] = _rope_t(tmp[...], cos, sin).astype(BF16)
        _ret_core(v_ref, do_ref, qh, tsw_ref, tmp, back, nchunk)
        dk_ref[...] = _rope_t(tmp[...] * scale, cos, sin).astype(BF16)

    return _call(
        body, "ret_bwd", (4,),
        [col(A_Q), col(A_K), col(A_V),
         pl.BlockSpec((s, LANES), lambda h: (0, h), pipeline_mode=pl.Buffered(1)),
         whole(), whole(), tabspec(), tabspec()],
        [outspec(), outspec(), outspec()],
        [jax.ShapeDtypeStruct((s, GROUP_W), BF16)] * 3,
        [pltpu.VMEM((s, LANES), F32)] * 4,
        ("arbitrary",), (z, z, z, d_o, cos_r, sin_r, tab, tab_sw), rider)


def _normgate_bwd(o, z, gate_blk, dy, dy_blk, norm_g, tm):
    s = o.shape[0]

    def body(o_ref, g_ref, dy_ref, ng_ref, do_ref, dg_ref, dng_ref):
        @pl.when(pl.program_id(0) == 0)
        def _():
            dng_ref[...] = jnp.zeros_like(dng_ref)

        for h in range(4):
            sl = slice(LANES * h, LANES * (h + 1))
            ov, gv, dyv, ng = o_ref[:, sl], g_ref[:, sl], dy_ref[:, sl], ng_ref[:, sl]
            r = lax.rsqrt(jnp.mean(ov * ov, axis=-1, keepdims=True) + EPS)
            on = ov * r
            dn = dyv * _silu(gv)
            u = dn * ng
            do_ref[:, sl] = r * (u - on * jnp.mean(u * on, axis=-1, keepdims=True))
            dg_ref[:, sl] = (dyv * (on * ng) * _silu_grad(gv)).astype(BF16)
            dng_ref[:, sl] += jnp.sum(dn * on, axis=0, keepdims=True)

    return pl.pallas_call(
        body, name="normgate_bwd", grid=(s // tm,),
        in_specs=[pl.BlockSpec((tm, GROUP_W), lambda i: (i, 0)),
                  pl.BlockSpec((tm, GROUP_W), lambda i: (i, gate_blk // 4)),
                  pl.BlockSpec((tm, GROUP_W), lambda i: (i, dy_blk)),
                  pl.BlockSpec((1, GROUP_W), lambda i: (0, 0))],
        out_specs=[pl.BlockSpec((tm, GROUP_W), lambda i: (i, 0)), pl.BlockSpec((tm, GROUP_W), lambda i: (i, 0)),
                   pl.BlockSpec((1, GROUP_W), lambda i: (0, 0))],
        out_shape=[jax.ShapeDtypeStruct((s, GROUP_W), F32), jax.ShapeDtypeStruct((s, GROUP_W), BF16),
                   jax.ShapeDtypeStruct((1, GROUP_W), F32)],
        compiler_params=_params(("arbitrary",)),
    )(o, z, dy, norm_g)


def _log_sigmoid(x):
    return jnp.minimum(x, 0.0) - jnp.log(1.0 + jnp.exp(-jnp.abs(x)))


def _gla_consts():
    c = GLA_CHUNK
    row = lax.broadcasted_iota(jnp.int32, (c, c), 0)
    colm = lax.broadcasted_iota(jnp.int32, (c, c), 1)
    lane = lax.broadcasted_iota(jnp.int32, (1, LANES), 1)
    low = row >= colm
    up = colm >= row
    heads = ((lane < GLA_DK).astype(F32), (lane >= GLA_DK).astype(F32))
    return low, up, heads


def _chunk_running_sum(x, suffix):
    rows = x.shape[0]
    pos = jnp.bitwise_and(lax.broadcasted_iota(jnp.int32, (rows, 1), 0), GLA_CHUNK - 1)
    k = 1
    while k < GLA_CHUNK:
        if suffix:
            x = x + jnp.where(pos < GLA_CHUNK - k, pltpu.roll(x, rows - k, 0), 0.0)
        else:
            x = x + jnp.where(pos >= k, pltpu.roll(x, k, 0), 0.0)
        k *= 2
    return x


def _gla_chunk(cum_ref, d, n):
    c = GLA_CHUNK
    cum = cum_ref[d, pl.ds(pl.multiple_of(n * c, c), c), :]
    last = cum_ref[d, pl.ds(n * c + (c - 1 if d == 0 else 0), 1), :]
    eq = jnp.exp(cum)
    ek = jnp.exp(-cum)
    el = jnp.exp(last - cum)
    dec = jnp.exp(last)
    return eq, ek, el, dec


def _gla_gates(ga_ref, wa_ref, ba_ref, cum_ref, s, suffix):
    rows = min(s, GLA_CUM_ROWS)

    def step(i, carry):
        r = pl.ds(pl.multiple_of(i * rows, rows), rows)
        pre = _dot(ga_ref[r, :].astype(BF16), wa_ref[...].astype(BF16)) + ba_ref[...]
        cum_ref[r, :] = _chunk_running_sum(_log_sigmoid(pre) * (1.0 / GLA_TAU), suffix)
        return carry
    lax.fori_loop(0, s // rows, step, 0)


def _gla_fwd(z, wa_f, wa_b, ba_f, ba_b, norm_g):
    s = z.shape[0]
    c = GLA_CHUNK
    nchunk = s // c
    scale = GLA_DK ** -0.5
    tm = min(s, 512)
    one = pl.Buffered(1)

    def body(q_ref, k_ref, v_ref, ga_ref, g_ref, waf_ref, wab_ref, baf_ref, bab_ref, ng_ref, o_ref, y_ref,
             la_s):
        low, up, heads = _gla_consts()
        _gla_gates(ga_ref, waf_ref, baf_ref, la_s.at[0], s, False)
        _gla_gates(ga_ref, wab_ref, bab_ref, la_s.at[1], s, True)
        for d in range(2):
            tri = (low, up)[d]

            def step(i, states):
                n = i if d == 0 else nchunk - 1 - i
                r = pl.ds(pl.multiple_of(n * c, c), c)
                q = q_ref[r, :] * scale
                k = k_ref[r, :]
                eq, ek, el, dec = _gla_chunk(la_s, d, n)
                qt = q * eq
                ktb = (k * ek).astype(BF16)
                kl = k * el
                new_states = []
                for hh in range(2):
                    cols = slice(LANES * hh, LANES * (hh + 1))
                    vb = v_ref[r, cols].astype(BF16)
                    qm = (qt * heads[hh]).astype(BF16)
                    a = jnp.where(tri, _dot(qm, ktb, NT), 0.0)
                    o = _dot(a.astype(BF16), vb) + _dot(qm, states[hh].astype(BF16), NT)
                    if d == 0:
                        o_ref[r, cols] = o
                    else:
                        o_ref[r, cols] += o
                    new_states.append(states[hh] * dec + _dot(vb, (kl * heads[hh]).astype(BF16), TN))
                return tuple(new_states)

            zero = jnp.zeros((LANES, LANES), F32)
            _chunk_loop(nchunk, step, (zero, zero), GLA_UNROLL)

        def epi(i, carry):
            r = pl.ds(pl.multiple_of(i * tm, tm), tm)
            for hh in range(2):
                cols = slice(LANES * hh, LANES * (hh + 1))
                o = o_ref[r, cols]
                rr = lax.rsqrt(jnp.mean(o * o, axis=-1, keepdims=True) + EPS)
                y_ref[r, cols] = (_silu(g_ref[r, cols]) * (o * rr * ng_ref[:, cols])).astype(BF16)
            return carry

        lax.fori_loop(0, s // tm, epi, 0)

    w2 = 2 * LANES
    return pl.pallas_call(
        body, name="gla_fwd", grid=(2,),
        in_specs=[pl.BlockSpec((s, LANES), lambda p: (0, B_Q + p), pipeline_mode=one),
                  pl.BlockSpec((s, LANES), lambda p: (0, B_K + p), pipeline_mode=one),
                  pl.BlockSpec((s, w2), lambda p: (0, B_V // 2 + p), pipeline_mode=one),
                  pl.BlockSpec((s, LANES), lambda p: (0, GA), pipeline_mode=one),
                  pl.BlockSpec((s, w2), lambda p: (0, B_G // 2 + p), pipeline_mode=one),
                  pl.BlockSpec((LANES, LANES), lambda p: (0, p)),
                  pl.BlockSpec((LANES, LANES), lambda p: (0, p)),
                  pl.BlockSpec((1, LANES), lambda p: (0, p)),
                  pl.BlockSpec((1, LANES), lambda p: (0, p)),
                  pl.BlockSpec((1, w2), lambda p: (0, p))],
        out_specs=[pl.BlockSpec((s, w2), lambda p: (0, p)), pl.BlockSpec((s, w2), lambda p: (0, p))],
        out_shape=[jax.ShapeDtypeStruct((s, GROUP_W), F32), jax.ShapeDtypeStruct((s, GROUP_W), BF16)],
        scratch_shapes=[pltpu.VMEM((2, s, LANES), F32)],
        compiler_params=_params(("arbitrary",)),
    )(z, z, z, z, z, wa_f, wa_b, ba_f, ba_b, norm_g)


def _gla_bwd(z, d_o, wa_f, wa_b, ba_f, ba_b, rider=None):
    s = z.shape[0]
    c = GLA_CHUNK
    nchunk = s // c
    scale = GLA_DK ** -0.5
    tm = min(s, GLA_CUM_ROWS)
    one = pl.Buffered(1)

    def body(q_ref, k_ref, v_ref, ga_ref, do_ref, waf_ref, wab_ref, baf_ref, bab_ref,
             dq_ref, dk_ref, dv_ref, dga_ref, dwaf_ref, dwab_ref, dbaf_ref, dbab_ref,
             la_s, dla_s, stash, dq_s, dk_s, dv_s):
        low, up, heads = _gla_consts()
        rowi = lax.broadcasted_iota(jnp.int32, (c, 1), 0)
        _gla_gates(ga_ref, waf_ref, baf_ref, la_s.at[0], s, False)
        _gla_gates(ga_ref, wab_ref, bab_ref, la_s.at[1], s, True)
        for d in range(2):
            tri = (low, up)[d]
            last_row = (rowi == (c - 1 if d == 0 else 0)).astype(F32)
            order = (lambda i: i) if d == 0 else (lambda i: nchunk - 1 - i)
            zero = jnp.zeros((LANES, LANES), F32)

            def states(i, sts):
                n = order(i)
                r = pl.ds(pl.multiple_of(n * c, c), c)
                k = k_ref[r, :]
                _, _, el, dec = _gla_chunk(la_s, d, n)
                kl = k * el
                new = []
                for hh in range(2):
                    cols = slice(LANES * hh, LANES * (hh + 1))
                    stash[hh, n] = sts[hh]
                    new.append(sts[hh] * dec + _dot(v_ref[r, cols].astype(BF16), (kl * heads[hh]).astype(BF16), TN))
                return tuple(new)

            _chunk_loop(nchunk, states, (zero, zero), GLA_UNROLL)

            def step(i, dsts):
                n = order(nchunk - 1 - i)
                r = pl.ds(pl.multiple_of(n * c, c), c)
                q = q_ref[r, :] * scale
                k = k_ref[r, :]
                eq, ek, el, dec = _gla_chunk(la_s, d, n)
                qt = q * eq
                kt = k * ek
                kl = k * el
                ktb = kt.astype(BF16)
                dqt = jnp.zeros((c, LANES), F32)
                dkt = jnp.zeros((c, LANES), F32)
                dkl = jnp.zeros((c, LANES), F32)
                ddec = jnp.zeros((1, LANES), F32)
                new = []
                for hh in range(2):
                    cols = slice(LANES * hh, LANES * (hh + 1))
                    vb = v_ref[r, cols].astype(BF16)
                    dob = do_ref[r, cols].astype(BF16)
                    qm = (qt * heads[hh]).astype(BF16)
                    a = jnp.where(tri, _dot(qm, ktb, NT), 0.0).astype(BF16)
                    da = jnp.where(tri, _dot(dob, vb, NT), 0.0).astype(BF16)
                    sn = stash[hh, n]
                    dst = dsts[hh]
                    dstb = dst.astype(BF16)
                    dqt = dqt + (_dot(da, ktb) + _dot(dob, sn.astype(BF16))) * heads[hh]
                    dkt = dkt + _dot(da, qm, TN)
                    dv = _dot(a, dob, TN) + _dot((kl * heads[hh]).astype(BF16), dstb, NT)
                    dkl = dkl + _dot(vb, dstb)
                    ddec = ddec + jnp.sum(dst * sn, axis=0, keepdims=True)
                    new.append(dst * dec + _dot(dob, qm, TN))
                    if d == 0:
                        dv_s[r, cols] = dv
                    else:
                        dv_ref[r, cols] = (dv_s[r, cols] + dv).astype(BF16)
                dlast = ddec * dec + jnp.sum(dkl * kl, axis=0, keepdims=True)
                dq = dqt * eq * scale
                dk = dkt * ek + dkl * el
                dcum = dqt * qt - dkt * kt - dkl * kl + last_row * dlast
                dla_s[d, r, :] = dcum
                if d == 0:
                    dq_s[r, :] = dq
                    dk_s[r, :] = dk
                else:
                    dq_ref[r, :] = (dq_s[r, :] + dq).astype(BF16)
                    dk_ref[r, :] = (dk_s[r, :] + dk).astype(BF16)
                return tuple(new)

            _chunk_loop(nchunk, step, (zero, zero), GLA_UNROLL)

        first = pl.program_id(0) == 0
        for d, (wa_ref, ba_ref, dwa_ref, dba_ref) in enumerate(
                ((waf_ref, baf_ref, dwaf_ref, dbaf_ref), (wab_ref, bab_ref, dwab_ref, dbab_ref))):
            dwa_ref[...] = jnp.zeros_like(dwa_ref)
            dba_ref[...] = jnp.zeros_like(dba_ref)

            def gates(i, carry):
                r = pl.ds(pl.multiple_of(i * tm, tm), tm)
                gab = ga_ref[r, :].astype(BF16)
                wab16 = wa_ref[...].astype(BF16)
                pre = _dot(gab, wab16) + ba_ref[...]
                dla = _chunk_running_sum(dla_s[d, r, :], suffix=(d == 0))
                dpre = dla * (1.0 / GLA_TAU) * _sigmoid(-pre)
                dpb = dpre.astype(BF16)
                dwa_ref[...] += _dot(gab, dpb, TN)
                dba_ref[...] += jnp.sum(dpre, axis=0, keepdims=True)
                dga = _dot(dpb, wab16, NT)
                if d == 0:
                    @pl.when(first)
                    def _():
                        dga_ref[r, :] = dga

                    @pl.when(jnp.logical_not(first))
                    def _():
                        dga_ref[r, :] += dga
                else:
                    dga_ref[r, :] += dga
                return carry

            lax.fori_loop(0, s // tm, gates, 0)

    w2 = 2 * LANES
    return _call(
        body, "gla_bwd", (2,),
        [pl.BlockSpec((s, LANES), lambda p: (0, B_Q + p), pipeline_mode=one),
         pl.BlockSpec((s, LANES), lambda p: (0, B_K + p), pipeline_mode=one),
         pl.BlockSpec((s, w2), lambda p: (0, B_V // 2 + p), pipeline_mode=one),
         pl.BlockSpec((s, LANES), lambda p: (0, GA), pipeline_mode=one),
         pl.BlockSpec((s, w2), lambda p: (0, p), pipeline_mode=one),
         pl.BlockSpec((LANES, LANES), lambda p: (0, p)),
         pl.BlockSpec((LANES, LANES), lambda p: (0, p)),
         pl.BlockSpec((1, LANES), lambda p: (0, p)),
         pl.BlockSpec((1, LANES), lambda p: (0, p))],
        [pl.BlockSpec((s, LANES), lambda p: (0, p), pipeline_mode=one),
         pl.BlockSpec((s, LANES), lambda p: (0, p), pipeline_mode=one),
         pl.BlockSpec((s, w2), lambda p: (0, p), pipeline_mode=one),
         pl.BlockSpec((s, LANES), lambda p: (0, 0), pipeline_mode=one),
         pl.BlockSpec((LANES, LANES), lambda p: (0, p)),
         pl.BlockSpec((LANES, LANES), lambda p: (0, p)),
         pl.BlockSpec((1, LANES), lambda p: (0, p)),
         pl.BlockSpec((1, LANES), lambda p: (0, p))],
        [jax.ShapeDtypeStruct((s, w2), BF16), jax.ShapeDtypeStruct((s, w2), BF16),
         jax.ShapeDtypeStruct((s, GROUP_W), BF16), jax.ShapeDtypeStruct((s, LANES), F32),
         jax.ShapeDtypeStruct((LANES, w2), F32), jax.ShapeDtypeStruct((LANES, w2), F32),
         jax.ShapeDtypeStruct((1, w2), F32), jax.ShapeDtypeStruct((1, w2), F32)],
        [pltpu.VMEM((2, s, LANES), F32), pltpu.VMEM((2, s, LANES), F32),
         pltpu.VMEM((2, nchunk, LANES, LANES), F32),
         pltpu.VMEM((s, LANES), F32), pltpu.VMEM((s, LANES), F32), pltpu.VMEM((s, w2), F32)],
        ("arbitrary",), (z, z, z, z, d_o, wa_f, wa_b, ba_f, ba_b), rider)


def _shift_rows(x, d, rowi):
    s = x.shape[0]
    if d == 0:
        return x
    y = pltpu.roll(x, d % s, 0)
    keep = (rowi >= d) if d > 0 else (rowi < s + d)
    return jnp.where(keep, y, 0.0)


def _run_sum(x, m, step, rowi):
    acc, n = x, 1
    while n < m:
        acc = acc + _shift_rows(acc, step * n, rowi)
        n *= 2
    return acc


def _pool_counts(s, w, rowi):
    hi = jnp.minimum(rowi + w // 2, s)
    lo = jnp.maximum(rowi - w // 2, 0)
    return (hi - lo).astype(F32)


def _pooled(u, w, rowi):
    s = u.shape[0]
    win = _shift_rows(_run_sum(u, w // 2, 1, rowi), 1, rowi) + _run_sum(u, w // 2, -1, rowi)
    return win / _pool_counts(s, w, rowi) - u


def _pool_fwd(z, pool_w, pool_scale):
    s = z.shape[0]
    one = pl.Buffered(1)

    def body(u_ref, g_ref, w_ref, sc_ref, y_ref):
        rowi = lax.broadcasted_iota(jnp.int32, (s, 1), 0)
        for g, w in enumerate(POOL_WINDOWS):
            cols = slice(LANES * g, LANES * (g + 1))
            pooled = _pooled(u_ref[:, cols], w, rowi)
            mixed = _dot(pooled.astype(BF16), w_ref[g].astype(BF16))
            y_ref[:, cols] = (_silu(g_ref[:, cols]) * (mixed * sc_ref[:, cols])).astype(BF16)

    return pl.pallas_call(
        body, name="pool_fwd", grid=(1,),
        in_specs=[pl.BlockSpec((s, GROUP_W), lambda i: (0, C_V // 4), pipeline_mode=one),
                  pl.BlockSpec((s, GROUP_W), lambda i: (0, C_G // 4), pipeline_mode=one),
                  pl.BlockSpec((4, LANES, LANES), lambda i: (0, 0, 0)),
                  pl.BlockSpec((1, GROUP_W), lambda i: (0, 0))],
        out_specs=pl.BlockSpec((s, GROUP_W), lambda i: (0, 0), pipeline_mode=one),
        out_shape=jax.ShapeDtypeStruct((s, GROUP_W), BF16),
        compiler_params=_params(("arbitrary",)),
    )(z, z, pool_w, pool_scale)


def _pool_bwd(z, dy, pool_w, pool_scale):
    s = z.shape[0]
    one = pl.Buffered(1)

    def body(u_ref, g_ref, dy_ref, w_ref, sc_ref, du_ref, dg_ref, dw_ref, dsc_ref):
        rowi = lax.broadcasted_iota(jnp.int32, (s, 1), 0)
        for g, w in enumerate(POOL_WINDOWS):
            cols = slice(LANES * g, LANES * (g + 1))
            gate, dyv, sc = g_ref[:, cols], dy_ref[:, cols], sc_ref[:, cols]
            wb = w_ref[g].astype(BF16)
            pooled = _pooled(u_ref[:, cols], w, rowi)
            pb = pooled.astype(BF16)
            mixed = _dot(pb, wb)
            dg_ref[:, cols] = (dyv * (mixed * sc) * _silu_grad(gate)).astype(BF16)
            dt = dyv * _silu(gate)
            dsc_ref[:, cols] = jnp.sum(dt * mixed, axis=0, keepdims=True)
            dmb = (dt * sc).astype(BF16)
            dw_ref[g] = _dot(pb, dmb, TN)
            dpool = _dot(dmb, wb, NT)
            e = dpool / _pool_counts(s, w, rowi)
            du_ref[:, cols] = (_run_sum(e, w // 2, 1, rowi) + _shift_rows(_run_sum(e, w // 2, -1, rowi), -1, rowi)
                               - dpool).astype(BF16)

    return pl.pallas_call(
        body, name="pool_bwd", grid=(1,),
        in_specs=[pl.BlockSpec((s, GROUP_W), lambda i: (0, C_V // 4), pipeline_mode=one),
                  pl.BlockSpec((s, GROUP_W), lambda i: (0, C_G // 4), pipeline_mode=one),
                  pl.BlockSpec((s, GROUP_W), lambda i: (0, 2), pipeline_mode=one),
                  pl.BlockSpec((4, LANES, LANES), lambda i: (0, 0, 0)),
                  pl.BlockSpec((1, GROUP_W), lambda i: (0, 0))],
        out_specs=[pl.BlockSpec((s, GROUP_W), lambda i: (0, 0), pipeline_mode=one),
                   pl.BlockSpec((s, GROUP_W), lambda i: (0, 0), pipeline_mode=one),
                   pl.BlockSpec((4, LANES, LANES), lambda i: (0, 0, 0)),
                   pl.BlockSpec((1, GROUP_W), lambda i: (0, 0))],
        out_shape=[jax.ShapeDtypeStruct((s, GROUP_W), BF16), jax.ShapeDtypeStruct((s, GROUP_W), BF16),
                   jax.ShapeDtypeStruct((4, LANES, LANES), F32), jax.ShapeDtypeStruct((1, GROUP_W), F32)],
        compiler_params=_params(("arbitrary",)),
    )(z, z, dy, pool_w, pool_scale)


def _mla_heads(qf, kv, kpe, qg, kg, cos, sin):
    out = []
    for h in range(4):
        qa = qf[:, LANES * h:LANES * (h + 1)]
        qb = qf[:, 512 + LANES * h:512 + LANES * (h + 1)]
        ka = kv[:, 256 * h:256 * h + LANES]
        rq = lax.rsqrt((jnp.sum(qa * qa, axis=-1, keepdims=True) + jnp.sum(qb * qb, axis=-1, keepdims=True))
                       * (1.0 / MLA_QK) + EPS)
        rk = lax.rsqrt((jnp.sum(ka * ka, axis=-1, keepdims=True) + jnp.sum(kpe * kpe, axis=-1, keepdims=True))
                       * (1.0 / MLA_QK) + EPS)
        out.append((qa, qb, rq, ka, rk))
    return out


def _mla_latents(mq_ref, mkv_ref, gq_ref, gkv_ref, wq_ref, wkv_ref):
    mq = mq_ref[...]
    rq = lax.rsqrt(jnp.mean(mq * mq, axis=-1, keepdims=True) + EPS)
    qn = mq * rq
    qnb = (qn * gq_ref[...]).astype(BF16)
    mkv = mkv_ref[...]
    rk = lax.rsqrt(jnp.mean(mkv * mkv, axis=-1, keepdims=True) + EPS)
    kvn = mkv * rk
    kvnb = (kvn * gkv_ref[...]).astype(BF16)
    qf = _dot(qnb, wq_ref[...])
    kv = _dot(kvnb, wkv_ref[...])
    return qn, rq, qnb, kvn, rk, kvnb, qf, kv


def _mla_prep(z, cos_m, sin_m, gq, wq, gkv, wkv, qg, kg, tm):
    s = z.shape[0]

    def body(mq_ref, mkv_ref, mkr_ref, cos_ref, sin_ref, gq_ref, wq_ref, gkv_ref, wkv_ref, qg_ref, kg_ref,
             q_ref, k_ref, v_ref):
        _, _, _, _, _, _, qf, kv = _mla_latents(mq_ref, mkv_ref, gq_ref, gkv_ref, wq_ref, wkv_ref)
        kpe = mkr_ref[...]
        cos, sin = cos_ref[...], sin_ref[...]
        qg, kg = qg_ref[...], kg_ref[...]
        for h, (qa, qb, rq, ka, rk) in enumerate(_mla_heads(qf, kv, kpe, qg, kg, cos, sin)):
            q_ref[h, :, 0:LANES] = (qa * rq * qg[:, 0:LANES] * ATTN_Q_SCALE).astype(BF16)
            q_ref[h, :, LANES:] = (_rope(qb * rq * qg[:, LANES:], cos, sin) * ATTN_Q_SCALE).astype(BF16)
            k_ref[h, :, 0:LANES] = (ka * rk * kg[:, 0:LANES]).astype(BF16)
            k_ref[h, :, LANES:] = _rope(kpe * rk * kg[:, LANES:], cos, sin).astype(BF16)
            v_ref[h] = kv[:, 256 * h + LANES:256 * (h + 1)].astype(BF16)

    full = lambda shape: pl.BlockSpec(shape, lambda i: (0,) * len(shape))
    return pl.pallas_call(
        body, name="mla_prep", grid=(s // tm,),
        in_specs=[pl.BlockSpec((tm, 512), lambda i: (i, M_Q // 4)),
                  pl.BlockSpec((tm, 256), lambda i: (i, M_KV // 2)),
                  pl.BlockSpec((tm, LANES), lambda i: (i, M_KR)),
                  pl.BlockSpec((tm, LANES), lambda i: (i, 0)),
                  pl.BlockSpec((tm, LANES), lambda i: (i, 0)),
                  full((1, 512)), full((512, 1024)), full((1, 256)), full((256, 1024)), full((1, 256)), full((1, 256))],
        out_specs=[pl.BlockSpec((4, tm, 256), lambda i: (0, i, 0)), pl.BlockSpec((4, tm, 256), lambda i: (0, i, 0)),
                   pl.BlockSpec((4, tm, LANES), lambda i: (0, i, 0))],
        out_shape=[jax.ShapeDtypeStruct((4, s, 256), BF16), jax.ShapeDtypeStruct((4, s, 256), BF16),
                   jax.ShapeDtypeStruct((4, s, LANES), BF16)],
        compiler_params=_params(("parallel",)),
    )(z, z, z, cos_m, sin_m, gq, wq, gkv, wkv, qg, kg)


def _mla_prep_bwd(z, cos_m, sin_m, gq, wq, gkv, wkv, qg, kg, dq, dk, dv, tm):
    s = z.shape[0]

    def body(mq_ref, mkv_ref, mkr_ref, cos_ref, sin_ref, gq_ref, wq_ref, gkv_ref, wkv_ref, qg_ref, kg_ref,
             dq_ref, dk_ref, dv_ref,
             dmq_ref, dmkv_ref, dmkr_ref, dwq_ref, dwkv_ref, dgq_ref, dgkv_ref, dqg_ref, dkg_ref, dqf, dkv):
        @pl.when(pl.program_id(0) == 0)
        def _():
            for r in (dwq_ref, dwkv_ref, dgq_ref, dgkv_ref, dqg_ref, dkg_ref):
                r[...] = jnp.zeros_like(r)

        qn, rq0, qnb, kvn, rk0, kvnb, qf, kv = _mla_latents(mq_ref, mkv_ref, gq_ref, gkv_ref, wq_ref, wkv_ref)
        kpe = mkr_ref[...]
        cos, sin = cos_ref[...], sin_ref[...]
        qg, kg = qg_ref[...], kg_ref[...]
        dkpe = jnp.zeros_like(kpe)
        inv = 1.0 / MLA_QK

        def norm_bwd(a, b, r, da_n, db_n, g):
            ga, gb = g[:, 0:LANES], g[:, LANES:]
            dg_a = jnp.sum(da_n * a * r, axis=0, keepdims=True)
            dg_b = jnp.sum(db_n * b * r, axis=0, keepdims=True)
            ua, ub = da_n * ga, db_n * gb
            dt = (jnp.sum(ua * a, axis=-1, keepdims=True) + jnp.sum(ub * b, axis=-1, keepdims=True)) * inv
            r3 = r * r * r
            return r * ua - a * (r3 * dt), r * ub - b * (r3 * dt), dg_a, dg_b

        for h, (qa, qb, rq, ka, rk) in enumerate(_mla_heads(qf, kv, kpe, qg, kg, cos, sin)):
            dqa, dqb, dga, dgb = norm_bwd(qa, qb, rq, dq_ref[h, :, 0:LANES] * ATTN_SCALE,
                                          _rope_t(dq_ref[h, :, LANES:] * ATTN_SCALE, cos, sin), qg)
            dqf[:, LANES * h:LANES * (h + 1)] = dqa
            dqf[:, 512 + LANES * h:512 + LANES * (h + 1)] = dqb
            dqg_ref[:, 0:LANES] += dga
            dqg_ref[:, LANES:] += dgb
            ln2 = math.log(2.0)
            dka, dkb, dga, dgb = norm_bwd(ka, kpe, rk, dk_ref[h, :, 0:LANES] * ln2,
                                          _rope_t(dk_ref[h, :, LANES:] * ln2, cos, sin), kg)
            dkv[:, 256 * h:256 * h + LANES] = dka
            dkv[:, 256 * h + LANES:256 * (h + 1)] = dv_ref[h]
            dkpe = dkpe + dkb
            dkg_ref[:, 0:LANES] += dga
            dkg_ref[:, LANES:] += dgb
        dmkr_ref[...] = dkpe.astype(BF16)

        def latent_bwd(dfull, w_ref, nb, n, r, g_ref, dw_ref, dg_ref, dlat_ref):
            db = dfull.astype(BF16)
            dn = _dot(db, w_ref[...], NT)
            dw_ref[...] += _dot(nb, db, TN)
            dg_ref[...] += jnp.sum(dn * n, axis=0, keepdims=True)
            u = dn * g_ref[...]
            dlat_ref[...] = (r * (u - n * jnp.mean(u * n, axis=-1, keepdims=True))).astype(BF16)

        latent_bwd(dqf[...], wq_ref, qnb, qn, rq0, gq_ref, dwq_ref, dgq_ref, dmq_ref)
        latent_bwd(dkv[...], wkv_ref, kvnb, kvn, rk0, gkv_ref, dwkv_ref, dgkv_ref, dmkv_ref)

    full = lambda shape: pl.BlockSpec(shape, lambda i: (0,) * len(shape))
    return pl.pallas_call(
        body, name="mla_prep_bwd", grid=(s // tm,),
        in_specs=[pl.BlockSpec((tm, 512), lambda i: (i, M_Q // 4)),
                  pl.BlockSpec((tm, 256), lambda i: (i, M_KV // 2)),
                  pl.BlockSpec((tm, LANES), lambda i: (i, M_KR)),
                  pl.BlockSpec((tm, LANES), lambda i: (i, 0)),
                  pl.BlockSpec((tm, LANES), lambda i: (i, 0)),
                  full((1, 512)), full((512, 1024)), full((1, 256)), full((256, 1024)), full((1, 256)), full((1, 256)),
                  pl.BlockSpec((4, tm, 256), lambda i: (0, i, 0)), pl.BlockSpec((4, tm, 256), lambda i: (0, i, 0)),
                  pl.BlockSpec((4, tm, LANES), lambda i: (0, i, 0))],
        out_specs=[pl.BlockSpec((tm, 512), lambda i: (i, 0)), pl.BlockSpec((tm, 256), lambda i: (i, 0)),
                   pl.BlockSpec((tm, LANES), lambda i: (i, 0)),
                   full((512, 1024)), full((256, 1024)), full((1, 512)), full((1, 256)), full((1, 256)), full((1, 256))],
        out_shape=[jax.ShapeDtypeStruct((s, 512), BF16), jax.ShapeDtypeStruct((s, 256), BF16),
                   jax.ShapeDtypeStruct((s, LANES), BF16),
                   jax.ShapeDtypeStruct((512, 1024), F32), jax.ShapeDtypeStruct((256, 1024), F32),
                   jax.ShapeDtypeStruct((1, 512), F32), jax.ShapeDtypeStruct((1, 256), F32),
                   jax.ShapeDtypeStruct((1, 256), F32), jax.ShapeDtypeStruct((1, 256), F32)],
        scratch_shapes=[pltpu.VMEM((tm, 1024), F32), pltpu.VMEM((tm, 1024), F32)],
        compiler_params=_params(("arbitrary",)),
    )(z, z, z, cos_m, sin_m, gq, wq, gkv, wkv, qg, kg, dq, dk, dv)


def _attn_fwd(q, k, v, z, tq, rider=None):
    s = q.shape[1]

    def body(q_ref, k_ref, v_ref, g_ref, o_ref, y_ref, lse_ref):
        sc = _dot(q_ref[...], k_ref[...], NT)
        m = jnp.max(sc, axis=-1, keepdims=True)
        p = jnp.exp2(sc - m)
        l = jnp.sum(p, axis=-1, keepdims=True)
        o = _dot(p.astype(BF16), v_ref[...]) / l
        o_ref[...] = o
        y_ref[...] = (_silu(g_ref[...]) * o).astype(BF16)
        lse_ref[...] = m + jnp.log2(l)

    return _call(
        body, "attn_fwd", (4, s // tq),
        [pl.BlockSpec((None, tq, 256), lambda h, i: (h, i, 0)),
         pl.BlockSpec((None, s, 256), lambda h, i: (h, 0, 0)),
         pl.BlockSpec((None, s, LANES), lambda h, i: (h, 0, 0)),
         pl.BlockSpec((tq, LANES), lambda h, i: (i, M_G + h))],
        [pl.BlockSpec((tq, LANES), lambda h, i: (i, h)), pl.BlockSpec((tq, LANES), lambda h, i: (i, h)),
         pl.BlockSpec((None, tq, 1), lambda h, i: (h, i, 0))],
        [jax.ShapeDtypeStruct((s, GROUP_W), F32), jax.ShapeDtypeStruct((s, GROUP_W), BF16),
         jax.ShapeDtypeStruct((4, s, 1), F32)],
        [], ("parallel", "parallel"), (q, k, v, z), rider)


def _attn_bwd(q, k, v, z, o, lse, dy, tq, rider=None):
    s = q.shape[1]

    def body(q_ref, k_ref, v_ref, g_ref, o_ref, lse_ref, dy_ref, dq_ref, dk_ref, dv_ref, dg_ref):
        @pl.when(pl.program_id(1) == 0)
        def _():
            dk_ref[...] = jnp.zeros_like(dk_ref)
            dv_ref[...] = jnp.zeros_like(dv_ref)

        gate, ov, dyv = g_ref[...], o_ref[...], dy_ref[...]
        do = dyv * _silu(gate)
        dg_ref[...] = (dyv * ov * _silu_grad(gate)).astype(BF16)
        delta = jnp.sum(do * ov, axis=-1, keepdims=True)
        dob = do.astype(BF16)
        qb, kb = q_ref[...], k_ref[...]
        p = jnp.exp2(_dot(qb, kb, NT) - lse_ref[...])
        dp = _dot(dob, v_ref[...], NT)
        ds = (p * (dp - delta)).astype(BF16)
        dq_ref[...] = _dot(ds, kb)
        dk_ref[...] += _dot(ds, qb, TN)
        dv_ref[...] += _dot(p.astype(BF16), dob, TN)

    return _call(
        body, "attn_bwd", (4, s // tq),
        [pl.BlockSpec((None, tq, 256), lambda h, i: (h, i, 0)),
         pl.BlockSpec((None, s, 256), lambda h, i: (h, 0, 0)),
         pl.BlockSpec((None, s, LANES), lambda h, i: (h, 0, 0)),
         pl.BlockSpec((tq, LANES), lambda h, i: (i, M_G + h)),
         pl.BlockSpec((tq, LANES), lambda h, i: (i, h)),
         pl.BlockSpec((None, tq, 1), lambda h, i: (h, i, 0)),
         pl.BlockSpec((tq, LANES), lambda h, i: (i, 12 + h))],
        [pl.BlockSpec((None, tq, 256), lambda h, i: (h, i, 0)),
         pl.BlockSpec((None, s, 256), lambda h, i: (h, 0, 0)),
         pl.BlockSpec((None, s, LANES), lambda h, i: (h, 0, 0)),
         pl.BlockSpec((tq, LANES), lambda h, i: (i, h))],
        [jax.ShapeDtypeStruct((4, s, 256), F32), jax.ShapeDtypeStruct((4, s, 256), F32),
         jax.ShapeDtypeStruct((4, s, LANES), F32), jax.ShapeDtypeStruct((s, GROUP_W), BF16)],
        [], ("parallel", "arbitrary"), (q, k, v, z, o, lse, dy), rider)


def _adam(parts, w, m, v, name, tr):
    r, c = w.shape
    tr = min(tr, r)
    c1 = 1.0 - ADAM_B1 ** ADAM_STEP
    c2 = 1.0 - ADAM_B2 ** ADAM_STEP

    def body(p_ref, w_ref, m_ref, v_ref, g_ref, d_ref, nm_ref, nv_ref):
        g = p_ref[0].astype(F32)
        for i in range(1, N_DEV):
            g = g + p_ref[i].astype(F32)
        nm = ADAM_B1 * m_ref[...] + (1.0 - ADAM_B1) * g
        nv = ADAM_B2 * v_ref[...] + (1.0 - ADAM_B2) * (g * g)
        g_ref[...] = g
        nm_ref[...] = nm
        nv_ref[...] = nv
        d_ref[...] = -ADAM_LR * ((nm / c1) / (jnp.sqrt(nv / c2) + ADAM_EPS) + ADAM_WD * w_ref[...])

    blk = lambda: pl.BlockSpec((tr, c), lambda i: (i, 0))
    return pl.pallas_call(
        body, name=name, grid=(r // tr,),
        in_specs=[pl.BlockSpec((N_DEV, tr, c), lambda i: (0, i, 0)), blk(), blk(), blk()],
        out_specs=[blk(), blk(), blk(), blk()],
        out_shape=[jax.ShapeDtypeStruct((r, c), F32)] * 4,
        compiler_params=_params(("parallel",)),
    )(parts, w, m, v)


def _adam_columns(parts, w, m, v, name, tc, rider=None):
    nl, r, c = w.shape
    pieces, spans = [], []
    for l, layer in enumerate(parts):
        at = 0
        for p in layer:
            pieces.append(p)
            spans.append((l, at // tc, p.shape[2] // tc))
            at += p.shape[2]
    c1 = 1.0 - ADAM_B1 ** ADAM_STEP
    c2 = 1.0 - ADAM_B2 ** ADAM_STEP

    def body(*refs):
        p_refs, (w_ref, m_ref, v_ref, g_ref, d_ref, nm_ref, nv_ref) = refs[:len(pieces)], refs[len(pieces):]
        i = pl.program_id(0)
        for (l, first, count), p_ref in zip(spans, p_refs):
            @pl.when(jnp.logical_and(i >= first, i < first + count))
            def _(l=l, p_ref=p_ref):
                g = p_ref[0].astype(F32)
                for k in range(1, N_DEV):
                    g = g + p_ref[k].astype(F32)
                nm = ADAM_B1 * m_ref[:, l, :] + (1.0 - ADAM_B1) * g
                nv = ADAM_B2 * v_ref[:, l, :] + (1.0 - ADAM_B2) * (g * g)
                g_ref[:, l, :] = g
                nm_ref[:, l, :] = nm
                nv_ref[:, l, :] = nv
                d_ref[:, l, :] = -ADAM_LR * ((nm / c1) / (jnp.sqrt(nv / c2) + ADAM_EPS) + ADAM_WD * w_ref[:, l, :])

    def part_spec(first, count):
        return pl.BlockSpec((N_DEV, c, tc), lambda i: (0, 0, jnp.clip(i - first, 0, count - 1)))

    blk = lambda: pl.BlockSpec((c, nl, tc), lambda i: (0, 0, i))
    t = lambda a: jnp.transpose(a, (2, 0, 1))
    *res, = _call(body, name, (r // tc,), [part_spec(first, count) for _, first, count in spans] + [blk(), blk(), blk()],
                  [blk(), blk(), blk(), blk()], [jax.ShapeDtypeStruct((c, nl, r), F32)] * 4, [],
                  ("arbitrary",), (*pieces, t(w), t(m), t(v)), rider)
    return [jnp.transpose(a, (1, 2, 0)) for a in res[:4]] + res[4:]


def _adam_layers(parts, w, m, v, name, tr, rider=None):
    nl, r, c = w.shape
    pieces = [p for layer in parts for p in layer]
    rp = pieces[0].shape[1]
    tr = min(tr, rp)
    nr, nrp = r // tr, rp // tr
    c1 = 1.0 - ADAM_B1 ** ADAM_STEP
    c2 = 1.0 - ADAM_B2 ** ADAM_STEP

    def body(*refs):
        p_refs, (w_ref, m_ref, v_ref, g_ref, d_ref, nm_ref, nv_ref) = refs[:len(pieces)], refs[len(pieces):]
        at = pl.program_id(0) * nr + pl.program_id(1)
        for j in range(len(pieces)):
            @pl.when(jnp.logical_and(at >= j * nrp, at < (j + 1) * nrp))
            def _(p_ref=p_refs[j]):
                g = p_ref[0].astype(F32)
                for i in range(1, N_DEV):
                    g = g + p_ref[i].astype(F32)
                nm = ADAM_B1 * m_ref[...] + (1.0 - ADAM_B1) * g
                nv = ADAM_B2 * v_ref[...] + (1.0 - ADAM_B2) * (g * g)
                g_ref[...] = g
                nm_ref[...] = nm
                nv_ref[...] = nv
                d_ref[...] = -ADAM_LR * ((nm / c1) / (jnp.sqrt(nv / c2) + ADAM_EPS) + ADAM_WD * w_ref[...])

    def part_spec(j):
        return pl.BlockSpec((N_DEV, tr, c), lambda ll, i: (0, jnp.clip(ll * nr + i - j * nrp, 0, nrp - 1), 0))

    blk = lambda: pl.BlockSpec((None, tr, c), lambda ll, i: (ll, i, 0))
    return _call(body, name, (nl, nr), [part_spec(j) for j in range(len(pieces))] + [blk(), blk(), blk()],
                 [blk(), blk(), blk(), blk()], [jax.ShapeDtypeStruct((nl, r, c), F32)] * 4, [],
                 ("arbitrary", "arbitrary"), (*pieces, w, m, v), rider)


REPLICATED = ("norm_g", "ret_norm_g", "gla_ba_f", "gla_ba_b", "gla_norm_g", "pool_w", "pool_scale",
              "mla_q_norm_g", "mla_kv_norm_g", "mla_qk_norm_q", "mla_qk_norm_k")
REPLICATED_EARLY = REPLICATED[1:]
SMALL_SHARDED = ("mla_wq_b", "mla_wkv_b", "gla_wa2_f", "gla_wa2_b")
WEIGHTS = ("norm_g", "w_in", "ret_norm_g", "gla_wa2_f", "gla_ba_f", "gla_wa2_b", "gla_ba_b", "gla_norm_g", "pool_w",
           "pool_scale", "mla_q_norm_g", "mla_wq_b", "mla_kv_norm_g", "mla_wkv_b", "mla_qk_norm_q", "mla_qk_norm_k",
           "w_out")


PACK_ROWS = 16
LAST_FIRST_COLS = 768


def _packed_rows(a):
    rows = a.size // LANES
    return rows, -(-rows // PACK_ROWS) * PACK_ROWS


def _pack(arrays, dtype):
    parts = []
    for a in arrays:
        rows, padded = _packed_rows(a)
        parts.append(jnp.pad(a.reshape(rows, LANES).astype(dtype), ((0, padded - rows), (0, 0))))
    return jnp.concatenate(parts, axis=0)


def _unpack(packed, like):
    out, at = [], 0
    for a in like:
        rows, padded = _packed_rows(a)
        out.append(packed[..., at:at + rows, :].reshape(packed.shape[:-2] + a.shape))
        at += padded
    return out


def _columns_by_device(g):
    l, r, n = g.shape
    return g.reshape(l, r, N_DEV, n // N_DEV).transpose(2, 0, 1, 3)


def _gathered_columns(g, l, r, c):
    return g.reshape(N_DEV, l, r, c).transpose(1, 2, 0, 3).reshape(l, r, N_DEV * c)


def _layer_forward(x, wts, late_wts, tables, tm, tq, ride_inproj=None, ride_attn=None, target=None):
    cos_r, sin_r, cos_m, sin_m, tab, _ = tables
    z, h, *carried_in = _inproj(x, wts["norm_g"], wts["w_in"], min(x.shape[0], 2 * tm), rider=ride_inproj)
    wts.update(late_wts(carried_in))
    o_a, y_a = _ret_fwd(z, cos_r, sin_r, tab, wts["ret_norm_g"])
    o_b, y_b = _gla_fwd(z, wts["wa_f"], wts["wa_b"], wts["gla_ba_f"], wts["gla_ba_b"], wts["gla_norm_g"])
    y_c = _pool_fwd(z, wts["pool_w"], wts["pool_scale"])
    q, k, v = _mla_prep(z, cos_m, sin_m, wts["mla_q_norm_g"], wts["wq"], wts["mla_kv_norm_g"], wts["wkv"],
                        wts["qk_q"], wts["qk_k"], tm)
    o_d, y_d, lse, *carried_attn = _attn_fwd(q, k, v, z, tq, rider=ride_attn)
    y = jnp.concatenate([y_a, y_b, y_c, y_d], axis=1)
    w_out = wts["w_out"]
    if target is None:
        x_next = _mm(y, w_out, "nn", "outproj", tm, D_MODEL, 1024, add=x)
    else:
        x_next = _mm(y, w_out, "nn", "outproj_loss", tm, D_MODEL, 1024, tail=_loss_tail(x, target))
    saved = dict(x=x, z=z, h=h, o_a=o_a, o_b=o_b, o_d=o_d, lse=lse, q=q, k=k, v=v, y=y, w_out=w_out)
    return x_next, saved, carried_in, carried_attn


def _layer_backward(dx, sv, wts, tables, tm, tq, rides, first_cols=D_MODEL // 2):
    cos_r, sin_r, cos_m, sin_m, tab, tab_sw = tables
    z = sv["z"]
    g = {}
    carried = {}

    def rider(name):
        return rides[name](g) if name in rides else None

    def landed(name, results, n_own):
        if name in rides:
            carried[name] = list(results[n_own:])
        return results[:n_own]

    g["w_out"] = _mm(sv["y"], dx, "tn", "d_w_out", 2048, 1024, 1024, out_dtype=BF16)
    dy = _mm(dx, sv["w_out"], "nt", "d_y", tm, 2048, 1024)

    do_a, dg_a, g["ret_norm_g"] = _normgate_bwd(sv["o_a"], z, A_G, dy, 0, wts["ret_norm_g"], tm)
    dq_a, dk_a, dv_a = landed("ret", _ret_bwd(z, do_a, cos_r, sin_r, tab, tab_sw, rider=rider("ret")), 3)

    do_b, dg_b, g["gla_norm_g"] = _normgate_bwd(sv["o_b"], z, B_G, dy, 1, wts["gla_norm_g"], tm)
    dq_b, dk_b, dv_b, d_ga, d_waf, d_wab, g["gla_ba_f"], g["gla_ba_b"] = landed("gla", _gla_bwd(
        z, do_b, wts["wa_f"], wts["wa_b"], wts["gla_ba_f"], wts["gla_ba_b"], rider=rider("gla")), 8)
    g["gla_wa2_f"] = d_waf[0:GLA_RANK]
    g["gla_wa2_b"] = d_wab[GLA_RANK:2 * GLA_RANK]

    du_c, dg_c, g["pool_w"], g["pool_scale"] = _pool_bwd(z, dy, wts["pool_w"], wts["pool_scale"])

    d_q, d_k, d_v, dg_d = landed("attn", _attn_bwd(sv["q"], sv["k"], sv["v"], z, sv["o_d"], sv["lse"], dy, tq,
                                                   rider=rider("attn")), 4)
    (d_mq, d_mkv, d_mkr, d_wq, g["mla_wkv_b"], g["mla_q_norm_g"], g["mla_kv_norm_g"], d_qg, d_kg) = _mla_prep_bwd(
        z, cos_m, sin_m, wts["mla_q_norm_g"], wts["wq"], wts["mla_kv_norm_g"], wts["wkv"], wts["qk_q"], wts["qk_k"],
        d_q, d_k, d_v, tm)
    g["mla_wq_b"] = _unpad_wq(d_wq)
    g["mla_qk_norm_q"] = d_qg[:, _QK_INV]
    g["mla_qk_norm_k"] = d_kg[:, _QK_INV]

    dz = jnp.concatenate([dq_a, dk_a, dv_a, dg_a, dq_b, dk_b, dv_b, dg_b, d_mq, du_c, dg_c, dg_d, d_mkv,
                          d_ga.astype(BF16), d_mkr], axis=1)
    h = sv["h"]
    for name, cols in (("d_w_in_a", h[:, :first_cols]), ("d_w_in_b", h[:, first_cols:])):
        res = _mm(dz, cols, "tn", name, 2048, cols.shape[1], 1024, out_dtype=BF16, rider=rider(name))
        (d_wt,) = landed(name, res if name in rides else [res], 1)
        g["w_in" + name[-2:]] = _split_w_in(d_wt)
    dx_in, g["norm_g"] = landed("d_h", _mm(dz, wts["w_in"], "nn", "d_h", tm, D_MODEL, 1024, rider=rider("d_h"),
                                           tail=_norm_bwd_tail(sv["x"], wts["norm_g"], dx)), 2)
    return dx_in, g, carried


def kernel(x, norm_g, w_in, ret_norm_g, gla_wa2_f, gla_ba_f, gla_wa2_b, gla_ba_b, gla_norm_g, pool_w, pool_scale, mla_q_norm_g, mla_wq_b, mla_kv_norm_g, mla_wkv_b, mla_qk_norm_q, mla_qk_norm_k, w_out, loss_target, m_norm_g, m_w_in, m_ret_norm_g, m_gla_wa2_f, m_gla_ba_f, m_gla_wa2_b, m_gla_ba_b, m_gla_norm_g, m_pool_w, m_pool_scale, m_mla_q_norm_g, m_mla_wq_b, m_mla_kv_norm_g, m_mla_wkv_b, m_mla_qk_norm_q, m_mla_qk_norm_k, m_w_out, v_norm_g, v_w_in, v_ret_norm_g, v_gla_wa2_f, v_gla_ba_f, v_gla_wa2_b, v_gla_ba_b, v_gla_norm_g, v_pool_w, v_pool_scale, v_mla_q_norm_g, v_mla_wq_b, v_mla_kv_norm_g, v_mla_wkv_b, v_mla_qk_norm_q, v_mla_qk_norm_k, v_w_out):
    w = dict(norm_g=norm_g, w_in=w_in, ret_norm_g=ret_norm_g, gla_wa2_f=gla_wa2_f, gla_ba_f=gla_ba_f,
             gla_wa2_b=gla_wa2_b, gla_ba_b=gla_ba_b, gla_norm_g=gla_norm_g, pool_w=pool_w, pool_scale=pool_scale,
             mla_q_norm_g=mla_q_norm_g, mla_wq_b=mla_wq_b, mla_kv_norm_g=mla_kv_norm_g, mla_wkv_b=mla_wkv_b,
             mla_qk_norm_q=mla_qk_norm_q, mla_qk_norm_k=mla_qk_norm_k, w_out=w_out)
    m = dict(norm_g=m_norm_g, w_in=m_w_in, ret_norm_g=m_ret_norm_g, gla_wa2_f=m_gla_wa2_f, gla_ba_f=m_gla_ba_f,
             gla_wa2_b=m_gla_wa2_b, gla_ba_b=m_gla_ba_b, gla_norm_g=m_gla_norm_g, pool_w=m_pool_w,
             pool_scale=m_pool_scale, mla_q_norm_g=m_mla_q_norm_g, mla_wq_b=m_mla_wq_b, mla_kv_norm_g=m_mla_kv_norm_g,
             mla_wkv_b=m_mla_wkv_b, mla_qk_norm_q=m_mla_qk_norm_q, mla_qk_norm_k=m_mla_qk_norm_k, w_out=m_w_out)
    v = dict(norm_g=v_norm_g, w_in=v_w_in, ret_norm_g=v_ret_norm_g, gla_wa2_f=v_gla_wa2_f, gla_ba_f=v_gla_ba_f,
             gla_wa2_b=v_gla_wa2_b, gla_ba_b=v_gla_ba_b, gla_norm_g=v_gla_norm_g, pool_w=v_pool_w,
             pool_scale=v_pool_scale, mla_q_norm_g=v_mla_q_norm_g, mla_wq_b=v_mla_wq_b, mla_kv_norm_g=v_mla_kv_norm_g,
             mla_wkv_b=v_mla_wkv_b, mla_qk_norm_q=v_mla_qk_norm_q, mla_qk_norm_k=v_mla_qk_norm_k, w_out=v_w_out)
    xs, target = x[0], loss_target[0]
    s = xs.shape[0]
    tm, tq = min(s, 512), min(s, 256)
    c_in = w_in.shape[2]

    w_in_b = jnp.transpose(w_in, (2, 0, 1)).astype(BF16)
    w_out_b = w_out.astype(BF16).reshape(-1, D_MODEL)
    (w_in_g0,) = _exchange([("gather", w_in_b[:, 0])], "gather_first")
    tables = _rope_tables(s) + _ret_tables()

    def early_weights(l, w_in_g):
        return dict(
            norm_g=norm_g[l][None], w_in=_assemble_w_in(w_in_g), ret_norm_g=ret_norm_g[l][None],
            gla_ba_f=gla_ba_f[l][None], gla_ba_b=gla_ba_b[l][None],
            gla_norm_g=gla_norm_g[l][None], pool_w=pool_w[l], pool_scale=pool_scale[l][None],
            mla_q_norm_g=mla_q_norm_g[l][None], mla_kv_norm_g=mla_kv_norm_g[l][None],
            qk_q=_pad_qk_gain(mla_qk_norm_q[l]), qk_k=_pad_qk_gain(mla_qk_norm_k[l]))

    def late_weights(l, w_out_g, small_g):
        shards = _unpack(small_g, [w[n] for n in SMALL_SHARDED])
        full = {n: _gathered_columns(shards[i], *w[n].shape)[l] for i, n in enumerate(SMALL_SHARDED)}
        wa_f = jnp.zeros((LANES, 2 * LANES), BF16).at[0:GLA_RANK].set(full["gla_wa2_f"])
        wa_b = jnp.zeros((LANES, 2 * LANES), BF16).at[GLA_RANK:2 * GLA_RANK].set(full["gla_wa2_b"])
        return dict(w_out=w_out_g.reshape(N_DEV, DEPTH, -1, D_MODEL)[:, l].reshape(-1, D_MODEL),
                    wa_f=wa_f, wa_b=wa_b, wq=_pad_wq(full["mla_wq_b"]), wkv=full["mla_wkv_b"])

    by_owner = lambda g_w_out: g_w_out.reshape(N_DEV, -1, D_MODEL)

    layers = [early_weights(0, w_in_g0), None]
    x1, sv0, (w_out_g, small_g), (w_in_g1,) = _layer_forward(
        xs, layers[0], lambda got: late_weights(0, *got), tables, tm, tq,
        ride_inproj=[("gather", w_out_b), ("gather", _pack([w[n] for n in SMALL_SHARDED], BF16))],
        ride_attn=("gather", w_in_b[:, 1]))
    layers[1] = early_weights(1, w_in_g1)
    (dx, loss_row), sv1, _, _ = _layer_forward(x1, layers[1], lambda got: late_weights(1, w_out_g, small_g), tables,
                                               tm, tq, target=target)
    loss = lax.psum(loss_row[0, 0], ("x", "y", "c"))

    def small_jobs(g):
        grads = (g, g1)
        full = {n: jnp.stack([grads[l][n].reshape(w[n].shape[1:]) if n in REPLICATED else grads[l][n]
                              for l in range(DEPTH)]) for n in SMALL_SHARDED + REPLICATED_EARLY}
        small_c = jax.vmap(lambda *shards: _pack(shards, F32))(*[_columns_by_device(full[n]) for n in SMALL_SHARDED])
        return [("scatter", small_c),
                ("gather", _pack([full[n] for n in REPLICATED_EARLY], F32))]

    dx, g1, got1 = _layer_backward(dx, sv1, layers[1], tables, tm, tq, {
        "attn": lambda g: ("scatter", by_owner(g["w_out"]))})
    dx, g0, got0 = _layer_backward(dx, sv0, layers[0], tables, tm, tq, {
        "gla": lambda g: ("scatter", g1["w_in_b"]),
        "attn": lambda g: [("scatter", g1["w_in_a"]), ("scatter", by_owner(g["w_out"]))],
        "d_w_in_b": lambda g: ("scatter", g["w_in_a"]),
        "d_h": lambda g: [("scatter", g["w_in_b"])] + small_jobs(g)}, first_cols=LAST_FIRST_COLS)
    in_parts = ((got0["d_w_in_b"][0], got0["d_h"][0]), (got0["attn"][0], got0["gla"][0]))
    out_parts = ((got0["attn"][1],), (got1["attn"][0],))
    small_parts, rep_parts = got0["d_h"][1:]
    norm_pack = _pack([jnp.stack([g0["norm_g"][0], g1["norm_g"][0]])], F32)

    out = {}
    out["w_in"] = _adam_columns(in_parts, w_in, m_w_in, v_w_in, "adam_w_in", 256)
    *out["w_out"], norm_parts = _adam_layers(out_parts, w_out, m_w_out, v_w_out, "adam_w_out", 128,
                                             rider=("gather", norm_pack))
    for names, parts, label in ((SMALL_SHARDED, small_parts, "adam_small"),
                                (REPLICATED_EARLY, rep_parts, "adam_replicated"), (("norm_g",), norm_parts, "adam_norm")):
        res = _adam(parts, _pack([w[n] for n in names], F32), _pack([m[n] for n in names], F32),
                    _pack([v[n] for n in names], F32), label, 2048)
        for n, *vals in zip(names, *[_unpack(a, [w[n] for n in names]) for a in res]):
            out[n] = vals

    return (loss, dx[None], *[out[n][0] for n in WEIGHTS], *[out[n][1] for n in WEIGHTS],
            *[out[n][2] for n in WEIGHTS], *[out[n][3] for n in WEIGHTS])
```

```python
import functools
import math

import numpy as np
import jax
import jax.numpy as jnp
from jax import lax
from jax.experimental import pallas as pl
from jax.experimental.pallas import tpu as pltpu

F32 = jnp.float32
BF16 = jnp.bfloat16

N_DEV = 8
D_MODEL = 2048
DEPTH = 2
GROUP_W = 512
EPS = 1e-6
ROPE_THETA = 10000.0
LANES = 128

RET_HD = 128
RET_CHUNK = 256
RET_UNROLL = 4
GLA_CHUNK = 64
GLA_UNROLL = 8
GLA_CUM_ROWS = 256
GLA_DK = 64
GLA_TAU = 16.0
GLA_RANK = 16
POOL_WINDOWS = (2, 4, 8, 16)
MLA_QK = 192
MLA_ROPE = 64
ATTN_SCALE = MLA_QK ** -0.5
ATTN_Q_SCALE = ATTN_SCALE * math.log2(math.e)
IN_COLS = 5984

ADAM_LR = 0.001
ADAM_B1 = 0.9
ADAM_B2 = 0.999
ADAM_EPS = 1e-08
ADAM_WD = 0.01
ADAM_STEP = 10

A_Q, A_K, A_V, A_G = 0, 4, 8, 12
B_Q, B_K, B_V, B_G = 16, 18, 20, 24
M_Q, C_V, C_G, M_G = 28, 32, 36, 40
M_KV, GA, M_KR = 44, 46, 47
ZP_COLS = 48 * LANES

VMEM_LIMIT = 56 * 1024 * 1024


def _params(sem, vmem=VMEM_LIMIT):
    return pltpu.CompilerParams(dimension_semantics=sem, vmem_limit_bytes=vmem)


def _sigmoid(x):
    return 1.0 / (1.0 + jnp.exp(-x))


def _silu(x):
    return x * _sigmoid(x)


def _silu_grad(x):
    s = _sigmoid(x)
    return s * (1.0 + x * (1.0 - s))


def _dot(a, b, dims=(((1,), (0,)), ((), ()))):
    return lax.dot_general(a, b, dims, preferred_element_type=F32)


NT = (((1,), (1,)), ((), ()))
TN = (((0,), (0,)), ((), ()))


def _chunk_loop(n, body, init, unroll):
    unroll = math.gcd(n, unroll)

    def trip(t, carry):
        for u in range(unroll):
            carry = body(t * unroll + u, carry)
        return carry

    return lax.fori_loop(0, n // unroll, trip, init)


def _roll_lanes_half(x):
    return pltpu.roll(x, 64, 1)


def _wq_perm():
    idx = np.zeros((1024,), np.int32)
    ok = np.zeros((1024,), bool)
    for h in range(4):
        idx[128 * h:128 * h + 128] = 192 * h + np.arange(128)
        ok[128 * h:128 * h + 128] = True
        base = 512 + 128 * h
        idx[base:base + 32] = 192 * h + 128 + np.arange(32)
        ok[base:base + 32] = True
        idx[base + 64:base + 96] = 192 * h + 160 + np.arange(32)
        ok[base + 64:base + 96] = True
    inv = np.zeros((768,), np.int32)
    inv[idx[ok]] = np.nonzero(ok)[0]
    return idx, ok, inv


_WQ_IDX, _WQ_OK, _WQ_INV = _wq_perm()


def _pad_wq(wq):
    return jnp.where(jnp.asarray(_WQ_OK)[None, :], wq[:, _WQ_IDX], 0).astype(wq.dtype)


def _unpad_wq(wqp):
    return wqp[:, _WQ_INV]


def _qk_idx():
    idx = np.zeros((256,), np.int32)
    ok = np.zeros((256,), bool)
    idx[0:128] = np.arange(128)
    ok[0:128] = True
    idx[128:160] = 128 + np.arange(32)
    ok[128:160] = True
    idx[192:224] = 160 + np.arange(32)
    ok[192:224] = True
    inv = np.zeros((192,), np.int32)
    inv[idx[ok]] = np.nonzero(ok)[0]
    return idx, ok, inv


_QK_IDX, _QK_OK, _QK_INV = _qk_idx()


def _pad_qk_gain(g):
    return jnp.where(jnp.asarray(_QK_OK), g[_QK_IDX], 0.0).reshape(1, 256)


def _rope_tables(s):
    def tabs(dim):
        inv = 1.0 / (ROPE_THETA ** (jnp.arange(0, dim, 2, dtype=F32) / dim))
        ang = jnp.arange(s, dtype=F32)[:, None] * inv[None, :]
        return jnp.cos(ang), jnp.sin(ang)
    cr, sr = tabs(RET_HD)
    cos_r = jnp.concatenate([cr, cr], axis=1)
    sin_r = jnp.concatenate([-sr, sr], axis=1)
    cm, sm = tabs(MLA_ROPE)
    zz = jnp.zeros_like(cm)
    cos_m = jnp.concatenate([cm, zz, cm, zz], axis=1)
    sin_m = jnp.concatenate([-sm, zz, sm, zz], axis=1)
    return cos_r, sin_r, cos_m, sin_m


def _rope(x, cos, sin):
    return x * cos + _roll_lanes_half(x) * sin


def _rope_t(x, cos, sin):
    return x * cos + _roll_lanes_half(x * sin)


def _ret_tables():
    c = RET_CHUNK
    gamma_f = 1.0 - 2.0 ** (-5.0 - jnp.arange(4, dtype=F32))
    gamma_b = gamma_f[::-1]
    idx = jnp.arange(c, dtype=F32)
    diff = idx[:, None] - idx[None, :]

    def build(g1, g2):
        l1 = jnp.log(g1)[:, None, None]
        l2 = jnp.log(g2)[:, None, None]
        d1 = jnp.where(diff >= 0, jnp.exp(jnp.maximum(diff, 0.0)[None] * l1), 0.0)
        d2 = jnp.where(diff <= 0, jnp.exp(jnp.maximum(-diff, 0.0)[None] * l2), 0.0)
        ones = jnp.ones((1, c, LANES), F32)
        col = idx[None, :, None]
        qdf = jnp.exp((col + 1.0) * l1) * ones
        kdf = jnp.exp((c - 1.0 - col) * l1) * ones
        qdb = jnp.exp((c - col) * l2) * ones
        kdb = jnp.exp(col * l2) * ones
        cd1 = jnp.exp(c * l1) * ones
        cd2 = jnp.exp(c * l2) * ones
        return jnp.concatenate([d1 + d2, qdf, kdf, qdb, kdb, cd1, cd2], axis=2)

    return build(gamma_f, gamma_b), build(gamma_b, gamma_f)


MESH = pl.DeviceIdType.MESH
ANY = pl.BlockSpec(memory_space=pl.ANY)
_RELATIONS = ((0, 0, 1), (1, 0, 0), (0, 1, 0), (1, 1, 0), (1, 0, 1), (0, 1, 1), (1, 1, 1))


def _position():
    return lax.axis_index("x"), lax.axis_index("y"), lax.axis_index("c")


def _gather_copies(x_ref, out_ref, send_sems, recv_sems, local_sem, stage):
    x, y, cc = _position()
    me, sibling = (x, y, cc), (x, y, 1 - cc)
    chips = [(1 - x, y), (x, 1 - y), (1 - x, 1 - y)]

    def slab(px, py, pc):
        return out_ref.at[4 * px + 2 * py + pc]

    def copy(k, block, to, src=None):
        return pltpu.make_async_remote_copy(
            src_ref=slab(*block) if src is None else src, dst_ref=slab(*block),
            send_sem=send_sems.at[k], recv_sem=recv_sems.at[k], device_id=to, device_id_type=MESH)

    passed = lambda: [copy(4 + j, (*chip, cc), sibling) for j, chip in enumerate(chips)]
    if stage == "forward":
        return [copy(1 + j, (*chip, cc), me) for j, chip in enumerate(chips)], passed()
    mine = pltpu.make_async_copy(x_ref, slab(*me), local_sem)
    first = [copy(0, me, sibling, src=x_ref)] + [copy(1 + j, me, (*chip, cc), src=x_ref) for j, chip in enumerate(chips)]
    if stage == "start":
        return mine, first
    late = [copy(0, sibling, me)] + [copy(4 + j, (*chip, 1 - cc), me) for j, chip in enumerate(chips)]
    return mine, first, passed(), late


def _gather_start(*refs):
    mine, first = _gather_copies(*refs, stage="start")
    mine.start()
    for cp in first:
        cp.start()


def _gather_forward(*refs):
    arrivals, passed = _gather_copies(*refs, stage="forward")
    for arrived, onward in zip(arrivals, passed):
        arrived.wait_recv()
        onward.start()


def _gather_finish(*refs):
    mine, first, passed, late = _gather_copies(*refs, stage="finish")
    for cp in late:
        cp.wait_recv()
    for cp in first + passed:
        cp.wait_send()
    mine.wait()


def _scatter_copies(c_ref, out_ref, send_sems, recv_sems, local_sem):
    x, y, cc = _position()
    me = 4 * x + 2 * y + cc
    mine = pltpu.make_async_copy(c_ref.at[me], out_ref.at[me], local_sem)
    copies = []
    for k, (fx, fy, fc) in enumerate(_RELATIONS):
        px = 1 - x if fx else x
        py = 1 - y if fy else y
        pc = 1 - cc if fc else cc
        copies.append(pltpu.make_async_remote_copy(
            src_ref=c_ref.at[4 * px + 2 * py + pc], dst_ref=out_ref.at[me],
            send_sem=send_sems.at[k], recv_sem=recv_sems.at[k], device_id=(px, py, pc), device_id_type=MESH))
    return mine, copies


def _scatter_start(*refs):
    mine, copies = _scatter_copies(*refs)
    mine.start()
    for cp in copies:
        cp.start()


def _scatter_finish(*refs):
    mine, copies = _scatter_copies(*refs)
    for cp in copies:
        cp.wait()
    mine.wait()


def _nothing(*refs):
    pass


_EXCHANGES = {"gather": (_gather_start, _gather_forward, _gather_finish),
              "scatter": (_scatter_start, _nothing, _scatter_finish)}


def _exchange_scratch():
    return [pltpu.SemaphoreType.DMA((7,)), pltpu.SemaphoreType.DMA((7,)), pltpu.SemaphoreType.DMA]


def _exchange_out(kind, src):
    return jax.ShapeDtypeStruct(((N_DEV,) + src.shape) if kind == "gather" else src.shape, src.dtype)


def _exchange(jobs, name):
    n = len(jobs)

    def body(*refs):
        srcs, outs, sems = refs[:n], refs[n:2 * n], refs[2 * n:]
        for stage in (0, 1, 2):
            for i, (kind, _) in enumerate(jobs):
                _EXCHANGES[kind][stage](srcs[i], outs[i], *sems[3 * i:3 * i + 3])

    return pl.pallas_call(
        body, name=name, out_shape=[_exchange_out(kind, src) for kind, src in jobs],
        in_specs=[ANY] * n, out_specs=[ANY] * n,
        scratch_shapes=[sem for _ in jobs for sem in _exchange_scratch()])(*[src for _, src in jobs])


def _call(body, name, grid, in_specs, out_specs, out_shape, scratch, sem, args, rider=None):
    if rider is None:
        return pl.pallas_call(body, name=name, grid=grid, in_specs=in_specs, out_specs=out_specs, out_shape=out_shape,
                              scratch_shapes=scratch, compiler_params=_params(sem))(*args)
    jobs = rider if isinstance(rider, list) else [rider]
    ni, no, ns, nj = len(in_specs), len(out_specs), len(scratch), len(jobs)

    def carried(*refs):
        ins, rsrcs = refs[:ni], refs[ni:ni + nj]
        outs, routs = refs[ni + nj:ni + nj + no], refs[ni + nj + no:ni + 2 * nj + no]
        scr, sems = refs[ni + 2 * nj + no:ni + 2 * nj + no + ns], refs[ni + 2 * nj + no + ns:]
        step = functools.reduce(lambda acc, ig: acc * ig[1] + ig[0],
                                [(pl.program_id(a), g) for a, g in enumerate(grid)], 0)
        steps = math.prod(grid)

        def stage(which):
            for j, (kind, _) in enumerate(jobs):
                _EXCHANGES[kind][which](rsrcs[j], routs[j], *sems[3 * j:3 * j + 3])

        @pl.when(step == 0)
        def _():
            stage(0)

        body(*ins, *outs, *scr)

        @pl.when(step == (3 * steps) // 4)
        def _():
            stage(1)

        @pl.when(step == steps - 1)
        def _():
            stage(2)

    return pl.pallas_call(
        carried, name=name, grid=grid, in_specs=list(in_specs) + [ANY] * nj, out_specs=list(out_specs) + [ANY] * nj,
        out_shape=list(out_shape) + [_exchange_out(kind, src) for kind, src in jobs],
        scratch_shapes=list(scratch) + [sem for _ in jobs for sem in _exchange_scratch()],
        compiler_params=_params(("arbitrary",) * len(grid)))(*args, *[src for _, src in jobs])


def _inproj(x, g, wt, tm, tn=512, rider=None):
    s, d = x.shape
    n = wt.shape[0]

    def body(x_ref, g_ref, w_ref, z_ref, h_ref, hs):
        @pl.when(pl.program_id(1) == 0)
        def _():
            xv = x_ref[...]
            r = lax.rsqrt(jnp.mean(xv * xv, axis=-1, keepdims=True) + EPS)
            hv = (xv * r * g_ref[...]).astype(BF16)
            hs[...] = hv
            h_ref[...] = hv
        z_ref[...] = _dot(hs[...], w_ref[...], NT)

    return _call(
        body, "inproj", (s // tm, n // tn),
        [pl.BlockSpec((tm, d), lambda i, j: (i, 0)),
         pl.BlockSpec((1, d), lambda i, j: (0, 0)),
         pl.BlockSpec((tn, d), lambda i, j: (j, 0))],
        [pl.BlockSpec((tm, tn), lambda i, j: (i, j)), pl.BlockSpec((tm, d), lambda i, j: (i, 0))],
        [jax.ShapeDtypeStruct((s, n), F32), jax.ShapeDtypeStruct((s, d), BF16)],
        [pltpu.VMEM((tm, d), BF16)], ("parallel", "arbitrary"), (x, g, wt), rider)


def _relayout_plan():
    runs = ((0, 3584, 0), (3584, 3616, GA * LANES), (3616, 4640, C_V * LANES), (4640, 5152, M_Q * LANES),
            (5152, 5408, M_KV * LANES), (5408, 5440, M_KR * LANES), (5440, 5472, M_KR * LANES + 64),
            (5472, 5984, M_G * LANES))
    shard = IN_COLS // N_DEV
    plan = []
    for d in range(N_DEV):
        lo, hi = shard * d, shard * (d + 1)
        for a, b, p in runs:
            s, e = max(a, lo), min(b, hi)
            if s < e:
                plan.append((d, s - lo, p + (s - a), e - s))
    return plan


def _assemble_w_in(g, tc=512):
    _, c, r = g.shape
    tc = min(tc, r)

    def body(g_ref, o_ref):
        o_ref[...] = jnp.zeros_like(o_ref)
        for d, at, to, w in _relayout_plan():
            o_ref[to:to + w, :] = g_ref[d, at:at + w, :]

    return pl.pallas_call(
        body, name="assemble_w_in", grid=(r // tc,),
        in_specs=[pl.BlockSpec((N_DEV, c, tc), lambda i: (0, 0, i))],
        out_specs=pl.BlockSpec((ZP_COLS, tc), lambda i: (0, i)),
        out_shape=jax.ShapeDtypeStruct((ZP_COLS, r), g.dtype),
        compiler_params=_params(("parallel",)),
    )(g)


def _split_w_in(wt, tc=512):
    r = wt.shape[1]
    c = IN_COLS // N_DEV
    tc = min(tc, r)

    def body(w_ref, o_ref):
        for d, at, to, w in _relayout_plan():
            o_ref[d, at:at + w, :] = w_ref[to:to + w, :]

    return pl.pallas_call(
        body, name="split_w_in", grid=(r // tc,),
        in_specs=[pl.BlockSpec((ZP_COLS, tc), lambda i: (0, i))],
        out_specs=pl.BlockSpec((N_DEV, c, tc), lambda i: (0, 0, i)),
        out_shape=jax.ShapeDtypeStruct((N_DEV, c, r), wt.dtype),
        compiler_params=_params(("parallel",)),
    )(wt)


def _mm(a, b, mode, name, tm, tn, tk, add=None, out_dtype=F32, rider=None, tail=None):
    if mode == "tn":
        k, m = a.shape
    else:
        m, k = a.shape
    n = b.shape[0] if mode == "nt" else b.shape[1]
    tm, tn, tk = min(tm, m), min(tn, n), min(tk, k)
    nk = k // tk
    dims = {"nn": (((1,), (0,)), ((), ())), "nt": NT, "tn": TN}[mode]
    if tail is None:
        def plain(acc, i, extra_refs, out_refs):
            out_refs[0][...] = (acc + extra_refs[0][...] if extra_refs else acc).astype(out_dtype)
        tail = ([(add, "tile")] if add is not None else [], [(out_dtype, "tile")], plain)
    extra, outs, fn = tail
    spec = {"tile": pl.BlockSpec((tm, tn), lambda i, j, kk: (i, j)),
            "row": pl.BlockSpec((1, tn), lambda i, j, kk: (0, j)),
            "lanes": pl.BlockSpec((1, LANES), lambda i, j, kk: (0, 0))}
    shape = {"tile": (m, n), "row": (1, n), "lanes": (1, LANES)}
    ne, no = len(extra), len(outs)

    def body(*refs):
        a_ref, b_ref = refs[:2]
        extra_refs, out_refs, acc = refs[2:2 + ne], refs[2 + ne:2 + ne + no], refs[2 + ne + no]
        i, kk = pl.program_id(0), pl.program_id(2)

        @pl.when(kk == 0)
        def _():
            acc[...] = jnp.zeros_like(acc)

        acc[...] += _dot(a_ref[...].astype(BF16), b_ref[...].astype(BF16), dims)

        @pl.when(kk == nk - 1)
        def _():
            fn(acc[...], i, extra_refs, out_refs)

    a_spec = (pl.BlockSpec((tk, tm), lambda i, j, kk: (kk, i)) if mode == "tn"
              else pl.BlockSpec((tm, tk), lambda i, j, kk: (i, kk)))
    b_spec = (pl.BlockSpec((tn, tk), lambda i, j, kk: (j, kk)) if mode == "nt"
              else pl.BlockSpec((tk, tn), lambda i, j, kk: (kk, j)))
    summed = any(kind != "tile" for _, kind in outs)
    res = _call(body, name, (m // tm, n // tn, nk), [a_spec, b_spec] + [spec[kind] for _, kind in extra],
                [spec[kind] for _, kind in outs], [jax.ShapeDtypeStruct(shape[kind], dt) for dt, kind in outs],
                [pltpu.VMEM((tm, tn), F32)], ("arbitrary",) * 3 if summed else ("parallel", "parallel", "arbitrary"),
                [a, b] + [arr for arr, _ in extra], rider)
    return res[0] if (rider is None and no == 1) else res


def _norm_bwd_tail(x, g, dres):
    def fn(dh, i, extra_refs, out_refs):
        x_ref, g_ref, dres_ref = extra_refs
        dx_ref, dg_ref = out_refs

        @pl.when(i == 0)
        def _():
            dg_ref[...] = jnp.zeros_like(dg_ref)

        xv = x_ref[...]
        r = lax.rsqrt(jnp.mean(xv * xv, axis=-1, keepdims=True) + EPS)
        nv = xv * r
        dg_ref[...] += jnp.sum(dh * nv, axis=0, keepdims=True)
        u = dh * g_ref[...]
        dx_ref[...] = dres_ref[...] + r * (u - nv * jnp.mean(u * nv, axis=-1, keepdims=True))

    return [(x, "tile"), (g, "row"), (dres, "tile")], [(F32, "tile"), (F32, "row")], fn


def _loss_tail(x, target):
    d = x.shape[1]

    def fn(acc, i, extra_refs, out_refs):
        x_ref, t_ref = extra_refs
        dx_ref, loss_ref = out_refs

        @pl.when(i == 0)
        def _():
            loss_ref[...] = jnp.zeros_like(loss_ref)

        err = acc + x_ref[...] - t_ref[...]
        dx_ref[...] = err * (1.0 / d)
        per_tok = jnp.mean(err * err, axis=-1, keepdims=True)
        loss_ref[...] += 0.5 * jnp.sum(per_tok, axis=0, keepdims=True)

    return [(x, "tile"), (target, "tile")], [(F32, "tile"), (F32, "lanes")], fn


def _ret_core(q_ref, k_ref, v_ref, tab_ref, out_ref, back_ref, nchunk):
    c = RET_CHUNK

    def rows(n):
        return pl.ds(pl.multiple_of(n * c, c), c)

    zero = jnp.zeros((LANES, LANES), F32)

    def plane(i, n=c):
        return tab_ref[0:n, c + LANES * i:c + LANES * (i + 1)]

    def fwd(n, st):
        r = rows(n)
        q, k, vb = q_ref[r, :], k_ref[r, :], v_ref[r, :].astype(BF16)
        sc = _dot(q.astype(BF16), k.astype(BF16), NT) * tab_ref[:, 0:c]
        o = _dot(sc.astype(BF16), vb)
        o = o + _dot((q * plane(0)).astype(BF16), st.astype(BF16))
        out_ref[r, :] = o
        return st * plane(4, LANES) + _dot((k * plane(1)).astype(BF16), vb, TN)

    def bwd(i, st):
        r = rows(nchunk - 1 - i)
        q, k, vb = q_ref[r, :], k_ref[r, :], v_ref[r, :].astype(BF16)
        back_ref[r, :] = _dot((q * plane(2)).astype(BF16), st.astype(BF16))
        return st * plane(5, LANES) + _dot((k * plane(3)).astype(BF16), vb, TN)

    def both(i, states):
        return fwd(i, states[0]), bwd(i, states[1])

    _chunk_loop(nchunk, both, (zero, zero), RET_UNROLL)
    out_ref[...] += back_ref[...]


def _ret_fwd(z, cos_r, sin_r, tab, norm_g):
    s = z.shape[0]
    nchunk = s // RET_CHUNK
    scale = RET_HD ** -0.5
    col = lambda base: pl.BlockSpec((s, LANES), lambda h: (0, base + h), pipeline_mode=pl.Buffered(1))

    def body(q_ref, k_ref, v_ref, g_ref, cos_ref, sin_ref, tab_ref, ng_ref, o_ref, y_ref, qh, kh, back):
        qh[...] = _rope(q_ref[...], cos_ref[...], sin_ref[...])
        kh[...] = _rope(k_ref[...], cos_ref[...], sin_ref[...]) * scale
        _ret_core(qh, kh, v_ref, tab_ref, o_ref, back, nchunk)
        o = o_ref[...]
        r = lax.rsqrt(jnp.mean(o * o, axis=-1, keepdims=True) + EPS)
        y_ref[...] = (_silu(g_ref[...]) * (o * r * ng_ref[...])).astype(BF16)

    return pl.pallas_call(
        body, name="ret_fwd", grid=(4,),
        in_specs=[col(A_Q), col(A_K), col(A_V), col(A_G),
                  pl.BlockSpec((s, LANES), lambda h: (0, 0), pipeline_mode=pl.Buffered(1)),
                  pl.BlockSpec((s, LANES), lambda h: (0, 0), pipeline_mode=pl.Buffered(1)),
                  pl.BlockSpec((None, RET_CHUNK, RET_CHUNK + 6 * LANES), lambda h: (h, 0, 0)),
                  pl.BlockSpec((1, LANES), lambda h: (0, h))],
        out_specs=[pl.BlockSpec((s, LANES), lambda h: (0, h)), pl.BlockSpec((s, LANES), lambda h: (0, h))],
        out_shape=[jax.ShapeDtypeStruct((s, GROUP_W), F32), jax.ShapeDtypeStruct((s, GROUP_W), BF16)],
        scratch_shapes=[pltpu.VMEM((s, LANES), F32)] * 3,
        compiler_params=_params(("arbitrary",)),
    )(z, z, z, z, cos_r, sin_r, tab, norm_g)


def _ret_bwd(z, d_o, cos_r, sin_r, tab, tab_sw, rider=None):
    s = z.shape[0]
    nchunk = s // RET_CHUNK
    scale = RET_HD ** -0.5
    col = lambda base: pl.BlockSpec((s, LANES), lambda h: (0, base + h), pipeline_mode=pl.Buffered(1))
    whole = lambda: pl.BlockSpec((s, LANES), lambda h: (0, 0), pipeline_mode=pl.Buffered(1))
    tabspec = lambda: pl.BlockSpec((None, RET_CHUNK, RET_CHUNK + 6 * LANES), lambda h: (h, 0, 0))
    outspec = lambda: pl.BlockSpec((s, LANES), lambda h: (0, h))

    def body(q_ref, k_ref, v_ref, do_ref, cos_ref, sin_ref, tab_ref, tsw_ref, dq_ref, dk_ref, dv_ref,
             qh, kh, tmp, back):
        cos, sin = cos_ref[...], sin_ref[...]
        qh[...] = _rope(q_ref[...], cos, sin)
        kh[...] = _rope(k_ref[...], cos, sin) * scale
        _ret_core(kh, qh, do_ref, tsw_ref, tmp, back, nchunk)
        dv_ref[...] = tmp[...].astype(BF16)
        _ret_core(do_ref, v_ref, kh, tab_ref, tmp, back, nchunk)
        dq_ref[...] = _rope_t(tmp[...], cos, sin).astype(BF16)
        _ret_core(v_ref, do_ref, qh, tsw_ref, tmp, back, nchunk)
        dk_ref[...] = _rope_t(tmp[...] * scale, cos, sin).astype(BF16)

    return _call(
        body, "ret_bwd", (4,),
        [col(A_Q), col(A_K), col(A_V),
         pl.BlockSpec((s, LANES), lambda h: (0, h), pipeline_mode=pl.Buffered(1)),
         whole(), whole(), tabspec(), tabspec()],
        [outspec(), outspec(), outspec()],
        [jax.ShapeDtypeStruct((s, GROUP_W), BF16)] * 3,
        [pltpu.VMEM((s, LANES), F32)] * 4,
        ("arbitrary",), (z, z, z, d_o, cos_r, sin_r, tab, tab_sw), rider)


def _normgate_bwd(o, z, gate_blk, dy, dy_blk, norm_g, tm):
    s = o.shape[0]

    def body(o_ref, g_ref, dy_ref, ng_ref, do_ref, dg_ref, dng_ref):
        @pl.when(pl.program_id(0) == 0)
        def _():
            dng_ref[...] = jnp.zeros_like(dng_ref)

        for h in range(4):
            sl = slice(LANES * h, LANES * (h + 1))
            ov, gv, dyv, ng = o_ref[:, sl], g_ref[:, sl], dy_ref[:, sl], ng_ref[:, sl]
            r = lax.rsqrt(jnp.mean(ov * ov, axis=-1, keepdims=True) + EPS)
            on = ov * r
            dn = dyv * _silu(gv)
            u = dn * ng
            do_ref[:, sl] = r * (u - on * jnp.mean(u * on, axis=-1, keepdims=True))
            dg_ref[:, sl] = (dyv * (on * ng) * _silu_grad(gv)).astype(BF16)
            dng_ref[:, sl] += jnp.sum(dn * on, axis=0, keepdims=True)

    return pl.pallas_call(
        body, name="normgate_bwd", grid=(s // tm,),
        in_specs=[pl.BlockSpec((tm, GROUP_W), lambda i: (i, 0)),
                  pl.BlockSpec((tm, GROUP_W), lambda i: (i, gate_blk // 4)),
                  pl.BlockSpec((tm, GROUP_W), lambda i: (i, dy_blk)),
                  pl.BlockSpec((1, GROUP_W), lambda i: (0, 0))],
        out_specs=[pl.BlockSpec((tm, GROUP_W), lambda i: (i, 0)), pl.BlockSpec((tm, GROUP_W), lambda i: (i, 0)),
                   pl.BlockSpec((1, GROUP_W), lambda i: (0, 0))],
        out_shape=[jax.ShapeDtypeStruct((s, GROUP_W), F32), jax.ShapeDtypeStruct((s, GROUP_W), BF16),
                   jax.ShapeDtypeStruct((1, GROUP_W), F32)],
        compiler_params=_params(("arbitrary",)),
    )(o, z, dy, norm_g)


def _log_sigmoid(x):
    return jnp.minimum(x, 0.0) - jnp.log(1.0 + jnp.exp(-jnp.abs(x)))


def _gla_consts():
    c = GLA_CHUNK
    row = lax.broadcasted_iota(jnp.int32, (c, c), 0)
    colm = lax.broadcasted_iota(jnp.int32, (c, c), 1)
    lane = lax.broadcasted_iota(jnp.int32, (1, LANES), 1)
    low = row >= colm
    up = colm >= row
    heads = ((lane < GLA_DK).astype(F32), (lane >= GLA_DK).astype(F32))
    return low, up, heads


def _chunk_running_sum(x, suffix):
    rows = x.shape[0]
    pos = jnp.bitwise_and(lax.broadcasted_iota(jnp.int32, (rows, 1), 0), GLA_CHUNK - 1)
    k = 1
    while k < GLA_CHUNK:
        if suffix:
            x = x + jnp.where(pos < GLA_CHUNK - k, pltpu.roll(x, rows - k, 0), 0.0)
        else:
            x = x + jnp.where(pos >= k, pltpu.roll(x, k, 0), 0.0)
        k *= 2
    return x


def _gla_chunk(cum_ref, d, n):
    c = GLA_CHUNK
    cum = cum_ref[d, pl.ds(pl.multiple_of(n * c, c), c), :]
    last = cum_ref[d, pl.ds(n * c + (c - 1 if d == 0 else 0), 1), :]
    eq = jnp.exp(cum)
    ek = jnp.exp(-cum)
    el = jnp.exp(last - cum)
    dec = jnp.exp(last)
    return eq, ek, el, dec


def _gla_gates(ga_ref, wa_ref, ba_ref, cum_ref, s, suffix):
    rows = min(s, GLA_CUM_ROWS)

    def step(i, carry):
        r = pl.ds(pl.multiple_of(i * rows, rows), rows)
        pre = _dot(ga_ref[r, :].astype(BF16), wa_ref[...].astype(BF16)) + ba_ref[...]
        cum_ref[r, :] = _chunk_running_sum(_log_sigmoid(pre) * (1.0 / GLA_TAU), suffix)
        return carry
    lax.fori_loop(0, s // rows, step, 0)


def _gla_fwd(z, wa_f, wa_b, ba_f, ba_b, norm_g):
    s = z.shape[0]
    c = GLA_CHUNK
    nchunk = s // c
    scale = GLA_DK ** -0.5
    tm = min(s, 512)
    one = pl.Buffered(1)

    def body(q_ref, k_ref, v_ref, ga_ref, g_ref, waf_ref, wab_ref, baf_ref, bab_ref, ng_ref, o_ref, y_ref,
             la_s):
        low, up, heads = _gla_consts()
        _gla_gates(ga_ref, waf_ref, baf_ref, la_s.at[0], s, False)
        _gla_gates(ga_ref, wab_ref, bab_ref, la_s.at[1], s, True)
        for d in range(2):
            tri = (low, up)[d]

            def step(i, states):
                n = i if d == 0 else nchunk - 1 - i
                r = pl.ds(pl.multiple_of(n * c, c), c)
                q = q_ref[r, :] * scale
                k = k_ref[r, :]
                eq, ek, el, dec = _gla_chunk(la_s, d, n)
                qt = q * eq
                ktb = (k * ek).astype(BF16)
                kl = k * el
                new_states = []
                for hh in range(2):
                    cols = slice(LANES * hh, LANES * (hh + 1))
                    vb = v_ref[r, cols].astype(BF16)
                    qm = (qt * heads[hh]).astype(BF16)
                    a = jnp.where(tri, _dot(qm, ktb, NT), 0.0)
                    o = _dot(a.astype(BF16), vb) + _dot(qm, states[hh].astype(BF16), NT)
                    if d == 0:
                        o_ref[r, cols] = o
                    else:
                        o_ref[r, cols] += o
                    new_states.append(states[hh] * dec + _dot(vb, (kl * heads[hh]).astype(BF16), TN))
                return tuple(new_states)

            zero = jnp.zeros((LANES, LANES), F32)
            _chunk_loop(nchunk, step, (zero, zero), GLA_UNROLL)

        def epi(i, carry):
            r = pl.ds(pl.multiple_of(i * tm, tm), tm)
            for hh in range(2):
                cols = slice(LANES * hh, LANES * (hh + 1))
                o = o_ref[r, cols]
                rr = lax.rsqrt(jnp.mean(o * o, axis=-1, keepdims=True) + EPS)
                y_ref[r, cols] = (_silu(g_ref[r, cols]) * (o * rr * ng_ref[:, cols])).astype(BF16)
            return carry

        lax.fori_loop(0, s // tm, epi, 0)

    w2 = 2 * LANES
    return pl.pallas_call(
        body, name="gla_fwd", grid=(2,),
        in_specs=[pl.BlockSpec((s, LANES), lambda p: (0, B_Q + p), pipeline_mode=one),
                  pl.BlockSpec((s, LANES), lambda p: (0, B_K + p), pipeline_mode=one),
                  pl.BlockSpec((s, w2), lambda p: (0, B_V // 2 + p), pipeline_mode=one),
                  pl.BlockSpec((s, LANES), lambda p: (0, GA), pipeline_mode=one),
                  pl.BlockSpec((s, w2), lambda p: (0, B_G // 2 + p), pipeline_mode=one),
                  pl.BlockSpec((LANES, LANES), lambda p: (0, p)),
                  pl.BlockSpec((LANES, LANES), lambda p: (0, p)),
                  pl.BlockSpec((1, LANES), lambda p: (0, p)),
                  pl.BlockSpec((1, LANES), lambda p: (0, p)),
                  pl.BlockSpec((1, w2), lambda p: (0, p))],
        out_specs=[pl.BlockSpec((s, w2), lambda p: (0, p)), pl.BlockSpec((s, w2), lambda p: (0, p))],
        out_shape=[jax.ShapeDtypeStruct((s, GROUP_W), F32), jax.ShapeDtypeStruct((s, GROUP_W), BF16)],
        scratch_shapes=[pltpu.VMEM((2, s, LANES), F32)],
        compiler_params=_params(("arbitrary",)),
    )(z, z, z, z, z, wa_f, wa_b, ba_f, ba_b, norm_g)


def _gla_bwd(z, d_o, wa_f, wa_b, ba_f, ba_b, rider=None):
    s = z.shape[0]
    c = GLA_CHUNK
    nchunk = s // c
    scale = GLA_DK ** -0.5
    tm = min(s, GLA_CUM_ROWS)
    one = pl.Buffered(1)

    def body(q_ref, k_ref, v_ref, ga_ref, do_ref, waf_ref, wab_ref, baf_ref, bab_ref,
             dq_ref, dk_ref, dv_ref, dga_ref, dwaf_ref, dwab_ref, dbaf_ref, dbab_ref,
             la_s, dla_s, stash, dq_s, dk_s, dv_s):
        low, up, heads = _gla_consts()
        rowi = lax.broadcasted_iota(jnp.int32, (c, 1), 0)
        _gla_gates(ga_ref, waf_ref, baf_ref, la_s.at[0], s, False)
        _gla_gates(ga_ref, wab_ref, bab_ref, la_s.at[1], s, True)
        for d in range(2):
            tri = (low, up)[d]
            last_row = (rowi == (c - 1 if d == 0 else 0)).astype(F32)
            order = (lambda i: i) if d == 0 else (lambda i: nchunk - 1 - i)
            zero = jnp.zeros((LANES, LANES), F32)

            def states(i, sts):
                n = order(i)
                r = pl.ds(pl.multiple_of(n * c, c), c)
                k = k_ref[r, :]
                _, _, el, dec = _gla_chunk(la_s, d, n)
                kl = k * el
                new = []
                for hh in range(2):
                    cols = slice(LANES * hh, LANES * (hh + 1))
                    stash[hh, n] = sts[hh]
                    new.append(sts[hh] * dec + _dot(v_ref[r, cols].astype(BF16), (kl * heads[hh]).astype(BF16), TN))
                return tuple(new)

            _chunk_loop(nchunk, states, (zero, zero), GLA_UNROLL)

            def step(i, dsts):
                n = order(nchunk - 1 - i)
                r = pl.ds(pl.multiple_of(n * c, c), c)
                q = q_ref[r, :] * scale
                k = k_ref[r, :]
                eq, ek, el, dec = _gla_chunk(la_s, d, n)
                qt = q * eq
                kt = k * ek
                kl = k * el
                ktb = kt.astype(BF16)
                dqt = jnp.zeros((c, LANES), F32)
                dkt = jnp.zeros((c, LANES), F32)
                dkl = jnp.zeros((c, LANES), F32)
                ddec = jnp.zeros((1, LANES), F32)
                new = []
                for hh in range(2):
                    cols = slice(LANES * hh, LANES * (hh + 1))
                    vb = v_ref[r, cols].astype(BF16)
                    dob = do_ref[r, cols].astype(BF16)
                    qm = (qt * heads[hh]).astype(BF16)
                    a = jnp.where(tri, _dot(qm, ktb, NT), 0.0).astype(BF16)
                    da = jnp.where(tri, _dot(dob, vb, NT), 0.0).astype(BF16)
                    sn = stash[hh, n]
                    dst = dsts[hh]
                    dstb = dst.astype(BF16)
                    dqt = dqt + (_dot(da, ktb) + _dot(dob, sn.astype(BF16))) * heads[hh]
                    dkt = dkt + _dot(da, qm, TN)
                    dv = _dot(a, dob, TN) + _dot((kl * heads[hh]).astype(BF16), dstb, NT)
                    dkl = dkl + _dot(vb, dstb)
                    ddec = ddec + jnp.sum(dst * sn, axis=0, keepdims=True)
                    new.append(dst * dec + _dot(dob, qm, TN))
                    if d == 0:
                        dv_s[r, cols] = dv
                    else:
                        dv_ref[r, cols] = (dv_s[r, cols] + dv).astype(BF16)
                dlast = ddec * dec + jnp.sum(dkl * kl, axis=0, keepdims=True)
                dq = dqt * eq * scale
                dk = dkt * ek + dkl * el
                dcum = dqt * qt - dkt * kt - dkl * kl + last_row * dlast
                dla_s[d, r, :] = dcum
                if d == 0:
                    dq_s[r, :] = dq
                    dk_s[r, :] = dk
                else:
                    dq_ref[r, :] = (dq_s[r, :] + dq).astype(BF16)
                    dk_ref[r, :] = (dk_s[r, :] + dk).astype(BF16)
                return tuple(new)

            _chunk_loop(nchunk, step, (zero, zero), GLA_UNROLL)

        first = pl.program_id(0) == 0
        for d, (wa_ref, ba_ref, dwa_ref, dba_ref) in enumerate(
                ((waf_ref, baf_ref, dwaf_ref, dbaf_ref), (wab_ref, bab_ref, dwab_ref, dbab_ref))):
            dwa_ref[...] = jnp.zeros_like(dwa_ref)
            dba_ref[...] = jnp.zeros_like(dba_ref)

            def gates(i, carry):
                r = pl.ds(pl.multiple_of(i * tm, tm), tm)
                gab = ga_ref[r, :].astype(BF16)
                wab16 = wa_ref[...].astype(BF16)
                pre = _dot(gab, wab16) + ba_ref[...]
                dla = _chunk_running_sum(dla_s[d, r, :], suffix=(d == 0))
                dpre = dla * (1.0 / GLA_TAU) * _sigmoid(-pre)
                dpb = dpre.astype(BF16)
                dwa_ref[...] += _dot(gab, dpb, TN)
                dba_ref[...] += jnp.sum(dpre, axis=0, keepdims=True)
                dga = _dot(dpb, wab16, NT)
                if d == 0:
                    @pl.when(first)
                    def _():
                        dga_ref[r, :] = dga

                    @pl.when(jnp.logical_not(first))
                    def _():
                        dga_ref[r, :] += dga
                else:
                    dga_ref[r, :] += dga
                return carry

            lax.fori_loop(0, s // tm, gates, 0)

    w2 = 2 * LANES
    return _call(
        body, "gla_bwd", (2,),
        [pl.BlockSpec((s, LANES), lambda p: (0, B_Q + p), pipeline_mode=one),
         pl.BlockSpec((s, LANES), lambda p: (0, B_K + p), pipeline_mode=one),
         pl.BlockSpec((s, w2), lambda p: (0, B_V // 2 + p), pipeline_mode=one),
         pl.BlockSpec((s, LANES), lambda p: (0, GA), pipeline_mode=one),
         pl.BlockSpec((s, w2), lambda p: (0, p), pipeline_mode=one),
         pl.BlockSpec((LANES, LANES), lambda p: (0, p)),
         pl.BlockSpec((LANES, LANES), lambda p: (0, p)),
         pl.BlockSpec((1, LANES), lambda p: (0, p)),
         pl.BlockSpec((1, LANES), lambda p: (0, p))],
        [pl.BlockSpec((s, LANES), lambda p: (0, p), pipeline_mode=one),
         pl.BlockSpec((s, LANES), lambda p: (0, p), pipeline_mode=one),
         pl.BlockSpec((s, w2), lambda p: (0, p), pipeline_mode=one),
         pl.BlockSpec((s, LANES), lambda p: (0, 0), pipeline_mode=one),
         pl.BlockSpec((LANES, LANES), lambda p: (0, p)),
         pl.BlockSpec((LANES, LANES), lambda p: (0, p)),
         pl.BlockSpec((1, LANES), lambda p: (0, p)),
         pl.BlockSpec((1, LANES), lambda p: (0, p))],
        [jax.ShapeDtypeStruct((s, w2), BF16), jax.ShapeDtypeStruct((s, w2), BF16),
         jax.ShapeDtypeStruct((s, GROUP_W), BF16), jax.ShapeDtypeStruct((s, LANES), F32),
         jax.ShapeDtypeStruct((LANES, w2), F32), jax.ShapeDtypeStruct((LANES, w2), F32),
         jax.ShapeDtypeStruct((1, w2), F32), jax.ShapeDtypeStruct((1, w2), F32)],
        [pltpu.VMEM((2, s, LANES), F32), pltpu.VMEM((2, s, LANES), F32),
         pltpu.VMEM((2, nchunk, LANES, LANES), F32),
         pltpu.VMEM((s, LANES), F32), pltpu.VMEM((s, LANES), F32), pltpu.VMEM((s, w2), F32)],
        ("arbitrary",), (z, z, z, z, d_o, wa_f, wa_b, ba_f, ba_b), rider)


def _shift_rows(x, d, rowi):
    s = x.shape[0]
    if d == 0:
        return x
    y = pltpu.roll(x, d % s, 0)
    keep = (rowi >= d) if d > 0 else (rowi < s + d)
    return jnp.where(keep, y, 0.0)


def _run_sum(x, m, step, rowi):
    acc, n = x, 1
    while n < m:
        acc = acc + _shift_rows(acc, step * n, rowi)
        n *= 2
    return acc


def _pool_counts(s, w, rowi):
    hi = jnp.minimum(rowi + w // 2, s)
    lo = jnp.maximum(rowi - w // 2, 0)
    return (hi - lo).astype(F32)


def _pooled(u, w, rowi):
    s = u.shape[0]
    win = _shift_rows(_run_sum(u, w // 2, 1, rowi), 1, rowi) + _run_sum(u, w // 2, -1, rowi)
    return win / _pool_counts(s, w, rowi) - u


def _pool_fwd(z, pool_w, pool_scale):
    s = z.shape[0]
    one = pl.Buffered(1)

    def body(u_ref, g_ref, w_ref, sc_ref, y_ref):
        rowi = lax.broadcasted_iota(jnp.int32, (s, 1), 0)
        for g, w in enumerate(POOL_WINDOWS):
            cols = slice(LANES * g, LANES * (g + 1))
            pooled = _pooled(u_ref[:, cols], w, rowi)
            mixed = _dot(pooled.astype(BF16), w_ref[g].astype(BF16))
            y_ref[:, cols] = (_silu(g_ref[:, cols]) * (mixed * sc_ref[:, cols])).astype(BF16)

    return pl.pallas_call(
        body, name="pool_fwd", grid=(1,),
        in_specs=[pl.BlockSpec((s, GROUP_W), lambda i: (0, C_V // 4), pipeline_mode=one),
                  pl.BlockSpec((s, GROUP_W), lambda i: (0, C_G // 4), pipeline_mode=one),
                  pl.BlockSpec((4, LANES, LANES), lambda i: (0, 0, 0)),
                  pl.BlockSpec((1, GROUP_W), lambda i: (0, 0))],
        out_specs=pl.BlockSpec((s, GROUP_W), lambda i: (0, 0), pipeline_mode=one),
        out_shape=jax.ShapeDtypeStruct((s, GROUP_W), BF16),
        compiler_params=_params(("arbitrary",)),
    )(z, z, pool_w, pool_scale)


def _pool_bwd(z, dy, pool_w, pool_scale):
    s = z.shape[0]
    one = pl.Buffered(1)

    def body(u_ref, g_ref, dy_ref, w_ref, sc_ref, du_ref, dg_ref, dw_ref, dsc_ref):
        rowi = lax.broadcasted_iota(jnp.int32, (s, 1), 0)
        for g, w in enumerate(POOL_WINDOWS):
            cols = slice(LANES * g, LANES * (g + 1))
            gate, dyv, sc = g_ref[:, cols], dy_ref[:, cols], sc_ref[:, cols]
            wb = w_ref[g].astype(BF16)
            pooled = _pooled(u_ref[:, cols], w, rowi)
            pb = pooled.astype(BF16)
            mixed = _dot(pb, wb)
            dg_ref[:, cols] = (dyv * (mixed * sc) * _silu_grad(gate)).astype(BF16)
            dt = dyv * _silu(gate)
            dsc_ref[:, cols] = jnp.sum(dt * mixed, axis=0, keepdims=True)
            dmb = (dt * sc).astype(BF16)
            dw_ref[g] = _dot(pb, dmb, TN)
            dpool = _dot(dmb, wb, NT)
            e = dpool / _pool_counts(s, w, rowi)
            du_ref[:, cols] = (_run_sum(e, w // 2, 1, rowi) + _shift_rows(_run_sum(e, w // 2, -1, rowi), -1, rowi)
                               - dpool).astype(BF16)

    return pl.pallas_call(
        body, name="pool_bwd", grid=(1,),
        in_specs=[pl.BlockSpec((s, GROUP_W), lambda i: (0, C_V // 4), pipeline_mode=one),
                  pl.BlockSpec((s, GROUP_W), lambda i: (0, C_G // 4), pipeline_mode=one),
                  pl.BlockSpec((s, GROUP_W), lambda i: (0, 2), pipeline_mode=one),
                  pl.BlockSpec((4, LANES, LANES), lambda i: (0, 0, 0)),
                  pl.BlockSpec((1, GROUP_W), lambda i: (0, 0))],
        out_specs=[pl.BlockSpec((s, GROUP_W), lambda i: (0, 0), pipeline_mode=one),
                   pl.BlockSpec((s, GROUP_W), lambda i: (0, 0), pipeline_mode=one),
                   pl.BlockSpec((4, LANES, LANES), lambda i: (0, 0, 0)),
                   pl.BlockSpec((1, GROUP_W), lambda i: (0, 0))],
        out_shape=[jax.ShapeDtypeStruct((s, GROUP_W), BF16), jax.ShapeDtypeStruct((s, GROUP_W), BF16),
                   jax.ShapeDtypeStruct((4, LANES, LANES), F32), jax.ShapeDtypeStruct((1, GROUP_W), F32)],
        compiler_params=_params(("arbitrary",)),
    )(z, z, dy, pool_w, pool_scale)


def _mla_heads(qf, kv, kpe, qg, kg, cos, sin):
    out = []
    for h in range(4):
        qa = qf[:, LANES * h:LANES * (h + 1)]
        qb = qf[:, 512 + LANES * h:512 + LANES * (h + 1)]
        ka = kv[:, 256 * h:256 * h + LANES]
        rq = lax.rsqrt((jnp.sum(qa * qa, axis=-1, keepdims=True) + jnp.sum(qb * qb, axis=-1, keepdims=True))
                       * (1.0 / MLA_QK) + EPS)
        rk = lax.rsqrt((jnp.sum(ka * ka, axis=-1, keepdims=True) + jnp.sum(kpe * kpe, axis=-1, keepdims=True))
                       * (1.0 / MLA_QK) + EPS)
        out.append((qa, qb, rq, ka, rk))
    return out


def _mla_latents(mq_ref, mkv_ref, gq_ref, gkv_ref, wq_ref, wkv_ref):
    mq = mq_ref[...]
    rq = lax.rsqrt(jnp.mean(mq * mq, axis=-1, keepdims=True) + EPS)
    qn = mq * rq
    qnb = (qn * gq_ref[...]).astype(BF16)
    mkv = mkv_ref[...]
    rk = lax.rsqrt(jnp.mean(mkv * mkv, axis=-1, keepdims=True) + EPS)
    kvn = mkv * rk
    kvnb = (kvn * gkv_ref[...]).astype(BF16)
    qf = _dot(qnb, wq_ref[...])
    kv = _dot(kvnb, wkv_ref[...])
    return qn, rq, qnb, kvn, rk, kvnb, qf, kv


def _mla_prep(z, cos_m, sin_m, gq, wq, gkv, wkv, qg, kg, tm):
    s = z.shape[0]

    def body(mq_ref, mkv_ref, mkr_ref, cos_ref, sin_ref, gq_ref, wq_ref, gkv_ref, wkv_ref, qg_ref, kg_ref,
             q_ref, k_ref, v_ref):
        _, _, _, _, _, _, qf, kv = _mla_latents(mq_ref, mkv_ref, gq_ref, gkv_ref, wq_ref, wkv_ref)
        kpe = mkr_ref[...]
        cos, sin = cos_ref[...], sin_ref[...]
        qg, kg = qg_ref[...], kg_ref[...]
        for h, (qa, qb, rq, ka, rk) in enumerate(_mla_heads(qf, kv, kpe, qg, kg, cos, sin)):
            q_ref[h, :, 0:LANES] = (qa * rq * qg[:, 0:LANES] * ATTN_Q_SCALE).astype(BF16)
            q_ref[h, :, LANES:] = (_rope(qb * rq * qg[:, LANES:], cos, sin) * ATTN_Q_SCALE).astype(BF16)
            k_ref[h, :, 0:LANES] = (ka * rk * kg[:, 0:LANES]).astype(BF16)
            k_ref[h, :, LANES:] = _rope(kpe * rk * kg[:, LANES:], cos, sin).astype(BF16)
            v_ref[h] = kv[:, 256 * h + LANES:256 * (h + 1)].astype(BF16)

    full = lambda shape: pl.BlockSpec(shape, lambda i: (0,) * len(shape))
    return pl.pallas_call(
        body, name="mla_prep", grid=(s // tm,),
        in_specs=[pl.BlockSpec((tm, 512), lambda i: (i, M_Q // 4)),
                  pl.BlockSpec((tm, 256), lambda i: (i, M_KV // 2)),
                  pl.BlockSpec((tm, LANES), lambda i: (i, M_KR)),
                  pl.BlockSpec((tm, LANES), lambda i: (i, 0)),
                  pl.BlockSpec((tm, LANES), lambda i: (i, 0)),
                  full((1, 512)), full((512, 1024)), full((1, 256)), full((256, 1024)), full((1, 256)), full((1, 256))],
        out_specs=[pl.BlockSpec((4, tm, 256), lambda i: (0, i, 0)), pl.BlockSpec((4, tm, 256), lambda i: (0, i, 0)),
                   pl.BlockSpec((4, tm, LANES), lambda i: (0, i, 0))],
        out_shape=[jax.ShapeDtypeStruct((4, s, 256), BF16), jax.ShapeDtypeStruct((4, s, 256), BF16),
                   jax.ShapeDtypeStruct((4, s, LANES), BF16)],
        compiler_params=_params(("parallel",)),
    )(z, z, z, cos_m, sin_m, gq, wq, gkv, wkv, qg, kg)


def _mla_prep_bwd(z, cos_m, sin_m, gq, wq, gkv, wkv, qg, kg, dq, dk, dv, tm):
    s = z.shape[0]

    def body(mq_ref, mkv_ref, mkr_ref, cos_ref, sin_ref, gq_ref, wq_ref, gkv_ref, wkv_ref, qg_ref, kg_ref,
             dq_ref, dk_ref, dv_ref,
             dmq_ref, dmkv_ref, dmkr_ref, dwq_ref, dwkv_ref, dgq_ref, dgkv_ref, dqg_ref, dkg_ref, dqf, dkv):
        @pl.when(pl.program_id(0) == 0)
        def _():
            for r in (dwq_ref, dwkv_ref, dgq_ref, dgkv_ref, dqg_ref, dkg_ref):
                r[...] = jnp.zeros_like(r)

        qn, rq0, qnb, kvn, rk0, kvnb, qf, kv = _mla_latents(mq_ref, mkv_ref, gq_ref, gkv_ref, wq_ref, wkv_ref)
        kpe = mkr_ref[...]
        cos, sin = cos_ref[...], sin_ref[...]
        qg, kg = qg_ref[...], kg_ref[...]
        dkpe = jnp.zeros_like(kpe)
        inv = 1.0 / MLA_QK

        def norm_bwd(a, b, r, da_n, db_n, g):
            ga, gb = g[:, 0:LANES], g[:, LANES:]
            dg_a = jnp.sum(da_n * a * r, axis=0, keepdims=True)
            dg_b = jnp.sum(db_n * b * r, axis=0, keepdims=True)
            ua, ub = da_n * ga, db_n * gb
            dt = (jnp.sum(ua * a, axis=-1, keepdims=True) + jnp.sum(ub * b, axis=-1, keepdims=True)) * inv
            r3 = r * r * r
            return r * ua - a * (r3 * dt), r * ub - b * (r3 * dt), dg_a, dg_b

        for h, (qa, qb, rq, ka, rk) in enumerate(_mla_heads(qf, kv, kpe, qg, kg, cos, sin)):
            dqa, dqb, dga, dgb = norm_bwd(qa, qb, rq, dq_ref[h, :, 0:LANES] * ATTN_SCALE,
                                          _rope_t(dq_ref[h, :, LANES:] * ATTN_SCALE, cos, sin), qg)
            dqf[:, LANES * h:LANES * (h + 1)] = dqa
            dqf[:, 512 + LANES * h:512 + LANES * (h + 1)] = dqb
            dqg_ref[:, 0:LANES] += dga
            dqg_ref[:, LANES:] += dgb
            ln2 = math.log(2.0)
            dka, dkb, dga, dgb = norm_bwd(ka, kpe, rk, dk_ref[h, :, 0:LANES] * ln2,
                                          _rope_t(dk_ref[h, :, LANES:] * ln2, cos, sin), kg)
            dkv[:, 256 * h:256 * h + LANES] = dka
            dkv[:, 256 * h + LANES:256 * (h + 1)] = dv_ref[h]
            dkpe = dkpe + dkb
            dkg_ref[:, 0:LANES] += dga
            dkg_ref[:, LANES:] += dgb
        dmkr_ref[...] = dkpe.astype(BF16)

        def latent_bwd(dfull, w_ref, nb, n, r, g_ref, dw_ref, dg_ref, dlat_ref):
            db = dfull.astype(BF16)
            dn = _dot(db, w_ref[...], NT)
            dw_ref[...] += _dot(nb, db, TN)
            dg_ref[...] += jnp.sum(dn * n, axis=0, keepdims=True)
            u = dn * g_ref[...]
            dlat_ref[...] = (r * (u - n * jnp.mean(u * n, axis=-1, keepdims=True))).astype(BF16)

        latent_bwd(dqf[...], wq_ref, qnb, qn, rq0, gq_ref, dwq_ref, dgq_ref, dmq_ref)
        latent_bwd(dkv[...], wkv_ref, kvnb, kvn, rk0, gkv_ref, dwkv_ref, dgkv_ref, dmkv_ref)

    full = lambda shape: pl.BlockSpec(shape, lambda i: (0,) * len(shape))
    return pl.pallas_call(
        body, name="mla_prep_bwd", grid=(s // tm,),
        in_specs=[pl.BlockSpec((tm, 512), lambda i: (i, M_Q // 4)),
                  pl.BlockSpec((tm, 256), lambda i: (i, M_KV // 2)),
                  pl.BlockSpec((tm, LANES), lambda i: (i, M_KR)),
                  pl.BlockSpec((tm, LANES), lambda i: (i, 0)),
                  pl.BlockSpec((tm, LANES), lambda i: (i, 0)),
                  full((1, 512)), full((512, 1024)), full((1, 256)), full((256, 1024)), full((1, 256)), full((1, 256)),
                  pl.BlockSpec((4, tm, 256), lambda i: (0, i, 0)), pl.BlockSpec((4, tm, 256), lambda i: (0, i, 0)),
                  pl.BlockSpec((4, tm, LANES), lambda i: (0, i, 0))],
        out_specs=[pl.BlockSpec((tm, 512), lambda i: (i, 0)), pl.BlockSpec((tm, 256), lambda i: (i, 0)),
                   pl.BlockSpec((tm, LANES), lambda i: (i, 0)),
                   full((512, 1024)), full((256, 1024)), full((1, 512)), full((1, 256)), full((1, 256)), full((1, 256))],
        out_shape=[jax.ShapeDtypeStruct((s, 512), BF16), jax.ShapeDtypeStruct((s, 256), BF16),
                   jax.ShapeDtypeStruct((s, LANES), BF16),
                   jax.ShapeDtypeStruct((512, 1024), F32), jax.ShapeDtypeStruct((256, 1024), F32),
                   jax.ShapeDtypeStruct((1, 512), F32), jax.ShapeDtypeStruct((1, 256), F32),
                   jax.ShapeDtypeStruct((1, 256), F32), jax.ShapeDtypeStruct((1, 256), F32)],
        scratch_shapes=[pltpu.VMEM((tm, 1024), F32), pltpu.VMEM((tm, 1024), F32)],
        compiler_params=_params(("arbitrary",)),
    )(z, z, z, cos_m, sin_m, gq, wq, gkv, wkv, qg, kg, dq, dk, dv)


def _attn_fwd(q, k, v, z, tq, rider=None):
    s = q.shape[1]

    def body(q_ref, k_ref, v_ref, g_ref, o_ref, y_ref, lse_ref):
        sc = _dot(q_ref[...], k_ref[...], NT)
        m = jnp.max(sc, axis=-1, keepdims=True)
        p = jnp.exp2(sc - m)
        l = jnp.sum(p, axis=-1, keepdims=True)
        o = _dot(p.astype(BF16), v_ref[...]) / l
        o_ref[...] = o
        y_ref[...] = (_silu(g_ref[...]) * o).astype(BF16)
        lse_ref[...] = m + jnp.log2(l)

    return _call(
        body, "attn_fwd", (4, s // tq),
        [pl.BlockSpec((None, tq, 256), lambda h, i: (h, i, 0)),
         pl.BlockSpec((None, s, 256), lambda h, i: (h, 0, 0)),
         pl.BlockSpec((None, s, LANES), lambda h, i: (h, 0, 0)),
         pl.BlockSpec((tq, LANES), lambda h, i: (i, M_G + h))],
        [pl.BlockSpec((tq, LANES), lambda h, i: (i, h)), pl.BlockSpec((tq, LANES), lambda h, i: (i, h)),
         pl.BlockSpec((None, tq, 1), lambda h, i: (h, i, 0))],
        [jax.ShapeDtypeStruct((s, GROUP_W), F32), jax.ShapeDtypeStruct((s, GROUP_W), BF16),
         jax.ShapeDtypeStruct((4, s, 1), F32)],
        [], ("parallel", "parallel"), (q, k, v, z), rider)


def _attn_bwd(q, k, v, z, o, lse, dy, tq, rider=None):
    s = q.shape[1]

    def body(q_ref, k_ref, v_ref, g_ref, o_ref, lse_ref, dy_ref, dq_ref, dk_ref, dv_ref, dg_ref):
        @pl.when(pl.program_id(1) == 0)
        def _():
            dk_ref[...] = jnp.zeros_like(dk_ref)
            dv_ref[...] = jnp.zeros_like(dv_ref)

        gate, ov, dyv = g_ref[...], o_ref[...], dy_ref[...]
        do = dyv * _silu(gate)
        dg_ref[...] = (dyv * ov * _silu_grad(gate)).astype(BF16)
        delta = jnp.sum(do * ov, axis=-1, keepdims=True)
        dob = do.astype(BF16)
        qb, kb = q_ref[...], k_ref[...]
        p = jnp.exp2(_dot(qb, kb, NT) - lse_ref[...])
        dp = _dot(dob, v_ref[...], NT)
        ds = (p * (dp - delta)).astype(BF16)
        dq_ref[...] = _dot(ds, kb)
        dk_ref[...] += _dot(ds, qb, TN)
        dv_ref[...] += _dot(p.astype(BF16), dob, TN)

    return _call(
        body, "attn_bwd", (4, s // tq),
        [pl.BlockSpec((None, tq, 256), lambda h, i: (h, i, 0)),
         pl.BlockSpec((None, s, 256), lambda h, i: (h, 0, 0)),
         pl.BlockSpec((None, s, LANES), lambda h, i: (h, 0, 0)),
         pl.BlockSpec((tq, LANES), lambda h, i: (i, M_G + h)),
         pl.BlockSpec((tq, LANES), lambda h, i: (i, h)),
         pl.BlockSpec((None, tq, 1), lambda h, i: (h, i, 0)),
         pl.BlockSpec((tq, LANES), lambda h, i: (i, 12 + h))],
        [pl.BlockSpec((None, tq, 256), lambda h, i: (h, i, 0)),
         pl.BlockSpec((None, s, 256), lambda h, i: (h, 0, 0)),
         pl.BlockSpec((None, s, LANES), lambda h, i: (h, 0, 0)),
         pl.BlockSpec((tq, LANES), lambda h, i: (i, h))],
        [jax.ShapeDtypeStruct((4, s, 256), F32), jax.ShapeDtypeStruct((4, s, 256), F32),
         jax.ShapeDtypeStruct((4, s, LANES), F32), jax.ShapeDtypeStruct((s, GROUP_W), BF16)],
        [], ("parallel", "arbitrary"), (q, k, v, z, o, lse, dy), rider)


def _adam(parts, w, m, v, name, tr):
    r, c = w.shape
    tr = min(tr, r)
    c1 = 1.0 - ADAM_B1 ** ADAM_STEP
    c2 = 1.0 - ADAM_B2 ** ADAM_STEP

    def body(p_ref, w_ref, m_ref, v_ref, g_ref, d_ref, nm_ref, nv_ref):
        g = p_ref[0].astype(F32)
        for i in range(1, N_DEV):
            g = g + p_ref[i].astype(F32)
        nm = ADAM_B1 * m_ref[...] + (1.0 - ADAM_B1) * g
        nv = ADAM_B2 * v_ref[...] + (1.0 - ADAM_B2) * (g * g)
        g_ref[...] = g
        nm_ref[...] = nm
        nv_ref[...] = nv
        d_ref[...] = -ADAM_LR * ((nm / c1) / (jnp.sqrt(nv / c2) + ADAM_EPS) + ADAM_WD * w_ref[...])

    blk = lambda: pl.BlockSpec((tr, c), lambda i: (i, 0))
    return pl.pallas_call(
        body, name=name, grid=(r // tr,),
        in_specs=[pl.BlockSpec((N_DEV, tr, c), lambda i: (0, i, 0)), blk(), blk(), blk()],
        out_specs=[blk(), blk(), blk(), blk()],
        out_shape=[jax.ShapeDtypeStruct((r, c), F32)] * 4,
        compiler_params=_params(("parallel",)),
    )(parts, w, m, v)


def _adam_columns(parts, w, m, v, name, tc, rider=None):
    nl, r, c = w.shape
    pieces = [p for layer in parts for p in layer]
    nh = len(parts[0])
    rp = pieces[0].shape[2]
    tc = min(tc, rp)
    ncb = rp // tc
    c1 = 1.0 - ADAM_B1 ** ADAM_STEP
    c2 = 1.0 - ADAM_B2 ** ADAM_STEP

    def body(*refs):
        p_refs, (w_ref, m_ref, v_ref, g_ref, d_ref, nm_ref, nv_ref) = refs[:len(pieces)], refs[len(pieces):]
        for h in range(nh):
            @pl.when(pl.program_id(0) == h)
            def _(h=h):
                for l in range(nl):
                    p_ref = p_refs[l * nh + h]
                    g = p_ref[0].astype(F32)
                    for i in range(1, N_DEV):
                        g = g + p_ref[i].astype(F32)
                    nm = ADAM_B1 * m_ref[:, l, :] + (1.0 - ADAM_B1) * g
                    nv = ADAM_B2 * v_ref[:, l, :] + (1.0 - ADAM_B2) * (g * g)
                    g_ref[:, l, :] = g
                    nm_ref[:, l, :] = nm
                    nv_ref[:, l, :] = nv
                    d_ref[:, l, :] = -ADAM_LR * ((nm / c1) / (jnp.sqrt(nv / c2) + ADAM_EPS) + ADAM_WD * w_ref[:, l, :])

    def part_spec(j):
        return pl.BlockSpec((N_DEV, c, tc), lambda h, i: (0, 0, jnp.clip((h - j % nh) * ncb + i, 0, ncb - 1)))

    blk = lambda: pl.BlockSpec((c, nl, tc), lambda h, i: (0, 0, h * ncb + i))
    t = lambda a: jnp.transpose(a, (2, 0, 1))
    *res, = _call(body, name, (nh, ncb), [part_spec(j) for j in range(len(pieces))] + [blk(), blk(), blk()],
                  [blk(), blk(), blk(), blk()], [jax.ShapeDtypeStruct((c, nl, r), F32)] * 4, [],
                  ("arbitrary",) * 2, (*pieces, t(w), t(m), t(v)), rider)
    return [jnp.transpose(a, (1, 2, 0)) for a in res[:4]] + res[4:]


def _adam_layers(parts, w, m, v, name, tr, rider=None):
    nl, r, c = w.shape
    pieces = [p for layer in parts for p in layer]
    rp = pieces[0].shape[1]
    tr = min(tr, rp)
    nr, nrp = r // tr, rp // tr
    c1 = 1.0 - ADAM_B1 ** ADAM_STEP
    c2 = 1.0 - ADAM_B2 ** ADAM_STEP

    def body(*refs):
        p_refs, (w_ref, m_ref, v_ref, g_ref, d_ref, nm_ref, nv_ref) = refs[:len(pieces)], refs[len(pieces):]
        at = pl.program_id(0) * nr + pl.program_id(1)
        for j in range(len(pieces)):
            @pl.when(jnp.logical_and(at >= j * nrp, at < (j + 1) * nrp))
            def _(p_ref=p_refs[j]):
                g = p_ref[0].astype(F32)
                for i in range(1, N_DEV):
                    g = g + p_ref[i].astype(F32)
                nm = ADAM_B1 * m_ref[...] + (1.0 - ADAM_B1) * g
                nv = ADAM_B2 * v_ref[...] + (1.0 - ADAM_B2) * (g * g)
                g_ref[...] = g
                nm_ref[...] = nm
                nv_ref[...] = nv
                d_ref[...] = -ADAM_LR * ((nm / c1) / (jnp.sqrt(nv / c2) + ADAM_EPS) + ADAM_WD * w_ref[...])

    def part_spec(j):
        return pl.BlockSpec((N_DEV, tr, c), lambda ll, i: (0, jnp.clip(ll * nr + i - j * nrp, 0, nrp - 1), 0))

    blk = lambda: pl.BlockSpec((None, tr, c), lambda ll, i: (ll, i, 0))
    return _call(body, name, (nl, nr), [part_spec(j) for j in range(len(pieces))] + [blk(), blk(), blk()],
                 [blk(), blk(), blk(), blk()], [jax.ShapeDtypeStruct((nl, r, c), F32)] * 4, [],
                 ("arbitrary", "arbitrary"), (*pieces, w, m, v), rider)


REPLICATED = ("norm_g", "ret_norm_g", "gla_ba_f", "gla_ba_b", "gla_norm_g", "pool_w", "pool_scale",
              "mla_q_norm_g", "mla_kv_norm_g", "mla_qk_norm_q", "mla_qk_norm_k")
REPLICATED_EARLY = REPLICATED[1:]
SMALL_SHARDED = ("mla_wq_b", "mla_wkv_b", "gla_wa2_f", "gla_wa2_b")
WEIGHTS = ("norm_g", "w_in", "ret_norm_g", "gla_wa2_f", "gla_ba_f", "gla_wa2_b", "gla_ba_b", "gla_norm_g", "pool_w",
           "pool_scale", "mla_q_norm_g", "mla_wq_b", "mla_kv_norm_g", "mla_wkv_b", "mla_qk_norm_q", "mla_qk_norm_k",
           "w_out")


PACK_ROWS = 16


def _packed_rows(a):
    rows = a.size // LANES
    return rows, -(-rows // PACK_ROWS) * PACK_ROWS


def _pack(arrays, dtype):
    parts = []
    for a in arrays:
        rows, padded = _packed_rows(a)
        parts.append(jnp.pad(a.reshape(rows, LANES).astype(dtype), ((0, padded - rows), (0, 0))))
    return jnp.concatenate(parts, axis=0)


def _unpack(packed, like):
    out, at = [], 0
    for a in like:
        rows, padded = _packed_rows(a)
        out.append(packed[..., at:at + rows, :].reshape(packed.shape[:-2] + a.shape))
        at += padded
    return out


def _columns_by_device(g):
    l, r, n = g.shape
    return g.reshape(l, r, N_DEV, n // N_DEV).transpose(2, 0, 1, 3)


def _gathered_columns(g, l, r, c):
    return g.reshape(N_DEV, l, r, c).transpose(1, 2, 0, 3).reshape(l, r, N_DEV * c)


def _layer_forward(x, wts, late_wts, tables, tm, tq, ride_inproj=None, ride_attn=None, target=None):
    cos_r, sin_r, cos_m, sin_m, tab, _ = tables
    z, h, *carried_in = _inproj(x, wts["norm_g"], wts["w_in"], min(x.shape[0], 2 * tm), rider=ride_inproj)
    wts.update(late_wts(carried_in))
    o_a, y_a = _ret_fwd(z, cos_r, sin_r, tab, wts["ret_norm_g"])
    o_b, y_b = _gla_fwd(z, wts["wa_f"], wts["wa_b"], wts["gla_ba_f"], wts["gla_ba_b"], wts["gla_norm_g"])
    y_c = _pool_fwd(z, wts["pool_w"], wts["pool_scale"])
    q, k, v = _mla_prep(z, cos_m, sin_m, wts["mla_q_norm_g"], wts["wq"], wts["mla_kv_norm_g"], wts["wkv"],
                        wts["qk_q"], wts["qk_k"], tm)
    o_d, y_d, lse, *carried_attn = _attn_fwd(q, k, v, z, tq, rider=ride_attn)
    y = jnp.concatenate([y_a, y_b, y_c, y_d], axis=1)
    w_out = wts["w_out"]
    if target is None:
        x_next = _mm(y, w_out, "nn", "outproj", tm, D_MODEL, 1024, add=x)
    else:
        x_next = _mm(y, w_out, "nn", "outproj_loss", tm, D_MODEL, 1024, tail=_loss_tail(x, target))
    saved = dict(x=x, z=z, h=h, o_a=o_a, o_b=o_b, o_d=o_d, lse=lse, q=q, k=k, v=v, y=y, w_out=w_out)
    return x_next, saved, carried_in, carried_attn


def _layer_backward(dx, sv, wts, tables, tm, tq, rides):
    cos_r, sin_r, cos_m, sin_m, tab, tab_sw = tables
    z = sv["z"]
    g = {}
    carried = {}

    def rider(name):
        return rides[name](g) if name in rides else None

    def landed(name, results, n_own):
        if name in rides:
            carried[name] = list(results[n_own:])
        return results[:n_own]

    g["w_out"] = _mm(sv["y"], dx, "tn", "d_w_out", 2048, 1024, 1024, out_dtype=BF16)
    dy = _mm(dx, sv["w_out"], "nt", "d_y", tm, 2048, 1024)

    do_a, dg_a, g["ret_norm_g"] = _normgate_bwd(sv["o_a"], z, A_G, dy, 0, wts["ret_norm_g"], tm)
    dq_a, dk_a, dv_a = landed("ret", _ret_bwd(z, do_a, cos_r, sin_r, tab, tab_sw, rider=rider("ret")), 3)

    do_b, dg_b, g["gla_norm_g"] = _normgate_bwd(sv["o_b"], z, B_G, dy, 1, wts["gla_norm_g"], tm)
    dq_b, dk_b, dv_b, d_ga, d_waf, d_wab, g["gla_ba_f"], g["gla_ba_b"] = landed("gla", _gla_bwd(
        z, do_b, wts["wa_f"], wts["wa_b"], wts["gla_ba_f"], wts["gla_ba_b"], rider=rider("gla")), 8)
    g["gla_wa2_f"] = d_waf[0:GLA_RANK]
    g["gla_wa2_b"] = d_wab[GLA_RANK:2 * GLA_RANK]

    du_c, dg_c, g["pool_w"], g["pool_scale"] = _pool_bwd(z, dy, wts["pool_w"], wts["pool_scale"])

    d_q, d_k, d_v, dg_d = landed("attn", _attn_bwd(sv["q"], sv["k"], sv["v"], z, sv["o_d"], sv["lse"], dy, tq,
                                                   rider=rider("attn")), 4)
    (d_mq, d_mkv, d_mkr, d_wq, g["mla_wkv_b"], g["mla_q_norm_g"], g["mla_kv_norm_g"], d_qg, d_kg) = _mla_prep_bwd(
        z, cos_m, sin_m, wts["mla_q_norm_g"], wts["wq"], wts["mla_kv_norm_g"], wts["wkv"], wts["qk_q"], wts["qk_k"],
        d_q, d_k, d_v, tm)
    g["mla_wq_b"] = _unpad_wq(d_wq)
    g["mla_qk_norm_q"] = d_qg[:, _QK_INV]
    g["mla_qk_norm_k"] = d_kg[:, _QK_INV]

    dz = jnp.concatenate([dq_a, dk_a, dv_a, dg_a, dq_b, dk_b, dv_b, dg_b, d_mq, du_c, dg_c, dg_d, d_mkv,
                          d_ga.astype(BF16), d_mkr], axis=1)
    h, half = sv["h"], D_MODEL // 2
    for name, cols in (("d_w_in_a", h[:, :half]), ("d_w_in_b", h[:, half:])):
        res = _mm(dz, cols, "tn", name, 2048, 1024, 1024, out_dtype=BF16, rider=rider(name))
        (d_wt,) = landed(name, res if name in rides else [res], 1)
        g["w_in" + name[-2:]] = _split_w_in(d_wt)
    dx_in, g["norm_g"] = landed("d_h", _mm(dz, wts["w_in"], "nn", "d_h", tm, D_MODEL, 1024, rider=rider("d_h"),
                                           tail=_norm_bwd_tail(sv["x"], wts["norm_g"], dx)), 2)
    return dx_in, g, carried


def kernel(x, norm_g, w_in, ret_norm_g, gla_wa2_f, gla_ba_f, gla_wa2_b, gla_ba_b, gla_norm_g, pool_w, pool_scale, mla_q_norm_g, mla_wq_b, mla_kv_norm_g, mla_wkv_b, mla_qk_norm_q, mla_qk_norm_k, w_out, loss_target, m_norm_g, m_w_in, m_ret_norm_g, m_gla_wa2_f, m_gla_ba_f, m_gla_wa2_b, m_gla_ba_b, m_gla_norm_g, m_pool_w, m_pool_scale, m_mla_q_norm_g, m_mla_wq_b, m_mla_kv_norm_g, m_mla_wkv_b, m_mla_qk_norm_q, m_mla_qk_norm_k, m_w_out, v_norm_g, v_w_in, v_ret_norm_g, v_gla_wa2_f, v_gla_ba_f, v_gla_wa2_b, v_gla_ba_b, v_gla_norm_g, v_pool_w, v_pool_scale, v_mla_q_norm_g, v_mla_wq_b, v_mla_kv_norm_g, v_mla_wkv_b, v_mla_qk_norm_q, v_mla_qk_norm_k, v_w_out):
    w = dict(norm_g=norm_g, w_in=w_in, ret_norm_g=ret_norm_g, gla_wa2_f=gla_wa2_f, gla_ba_f=gla_ba_f,
             gla_wa2_b=gla_wa2_b, gla_ba_b=gla_ba_b, gla_norm_g=gla_norm_g, pool_w=pool_w, pool_scale=pool_scale,
             mla_q_norm_g=mla_q_norm_g, mla_wq_b=mla_wq_b, mla_kv_norm_g=mla_kv_norm_g, mla_wkv_b=mla_wkv_b,
             mla_qk_norm_q=mla_qk_norm_q, mla_qk_norm_k=mla_qk_norm_k, w_out=w_out)
    m = dict(norm_g=m_norm_g, w_in=m_w_in, ret_norm_g=m_ret_norm_g, gla_wa2_f=m_gla_wa2_f, gla_ba_f=m_gla_ba_f,
             gla_wa2_b=m_gla_wa2_b, gla_ba_b=m_gla_ba_b, gla_norm_g=m_gla_norm_g, pool_w=m_pool_w,
             pool_scale=m_pool_scale, mla_q_norm_g=m_mla_q_norm_g, mla_wq_b=m_mla_wq_b, mla_kv_norm_g=m_mla_kv_norm_g,
             mla_wkv_b=m_mla_wkv_b, mla_qk_norm_q=m_mla_qk_norm_q, mla_qk_norm_k=m_mla_qk_norm_k, w_out=m_w_out)
    v = dict(norm_g=v_norm_g, w_in=v_w_in, ret_norm_g=v_ret_norm_g, gla_wa2_f=v_gla_wa2_f, gla_ba_f=v_gla_ba_f,
             gla_wa2_b=v_gla_wa2_b, gla_ba_b=v_gla_ba_b, gla_norm_g=v_gla_norm_g, pool_w=v_pool_w,
             pool_scale=v_pool_scale, mla_q_norm_g=v_mla_q_norm_g, mla_wq_b=v_mla_wq_b, mla_kv_norm_g=v_mla_kv_norm_g,
             mla_wkv_b=v_mla_wkv_b, mla_qk_norm_q=v_mla_qk_norm_q, mla_qk_norm_k=v_mla_qk_norm_k, w_out=v_w_out)
    xs, target = x[0], loss_target[0]
    s = xs.shape[0]
    tm, tq = min(s, 512), min(s, 256)
    c_in = w_in.shape[2]

    w_in_b = jnp.transpose(w_in, (2, 0, 1)).astype(BF16)
    w_out_b = w_out.astype(BF16).reshape(-1, D_MODEL)
    (w_in_g0,) = _exchange([("gather", w_in_b[:, 0])], "gather_first")
    tables = _rope_tables(s) + _ret_tables()

    def early_weights(l, w_in_g):
        return dict(
            norm_g=norm_g[l][None], w_in=_assemble_w_in(w_in_g), ret_norm_g=ret_norm_g[l][None],
            gla_ba_f=gla_ba_f[l][None], gla_ba_b=gla_ba_b[l][None],
            gla_norm_g=gla_norm_g[l][None], pool_w=pool_w[l], pool_scale=pool_scale[l][None],
            mla_q_norm_g=mla_q_norm_g[l][None], mla_kv_norm_g=mla_kv_norm_g[l][None],
            qk_q=_pad_qk_gain(mla_qk_norm_q[l]), qk_k=_pad_qk_gain(mla_qk_norm_k[l]))

    def late_weights(l, w_out_g, small_g):
        shards = _unpack(small_g, [w[n] for n in SMALL_SHARDED])
        full = {n: _gathered_columns(shards[i], *w[n].shape)[l] for i, n in enumerate(SMALL_SHARDED)}
        wa_f = jnp.zeros((LANES, 2 * LANES), BF16).at[0:GLA_RANK].set(full["gla_wa2_f"])
        wa_b = jnp.zeros((LANES, 2 * LANES), BF16).at[GLA_RANK:2 * GLA_RANK].set(full["gla_wa2_b"])
        return dict(w_out=w_out_g.reshape(N_DEV, DEPTH, -1, D_MODEL)[:, l].reshape(-1, D_MODEL),
                    wa_f=wa_f, wa_b=wa_b, wq=_pad_wq(full["mla_wq_b"]), wkv=full["mla_wkv_b"])

    by_owner = lambda g_w_out: g_w_out.reshape(N_DEV, -1, D_MODEL)

    layers = [early_weights(0, w_in_g0), None]
    x1, sv0, (w_out_g, small_g), (w_in_g1,) = _layer_forward(
        xs, layers[0], lambda got: late_weights(0, *got), tables, tm, tq,
        ride_inproj=[("gather", w_out_b), ("gather", _pack([w[n] for n in SMALL_SHARDED], BF16))],
        ride_attn=("gather", w_in_b[:, 1]))
    layers[1] = early_weights(1, w_in_g1)
    (dx, loss_row), sv1, _, _ = _layer_forward(x1, layers[1], lambda got: late_weights(1, w_out_g, small_g), tables,
                                               tm, tq, target=target)
    loss = lax.psum(loss_row[0, 0], ("x", "y", "c"))

    def small_jobs(g):
        grads = (g, g1)
        full = {n: jnp.stack([grads[l][n].reshape(w[n].shape[1:]) if n in REPLICATED else grads[l][n]
                              for l in range(DEPTH)]) for n in SMALL_SHARDED + REPLICATED_EARLY}
        small_c = jax.vmap(lambda *shards: _pack(shards, F32))(*[_columns_by_device(full[n]) for n in SMALL_SHARDED])
        return [("scatter", small_c),
                ("gather", _pack([full[n] for n in REPLICATED_EARLY], F32))]

    dx, g1, got1 = _layer_backward(dx, sv1, layers[1], tables, tm, tq, {
        "attn": lambda g: ("scatter", by_owner(g["w_out"]))})
    dx, g0, got0 = _layer_backward(dx, sv0, layers[0], tables, tm, tq, {
        "gla": lambda g: ("scatter", g1["w_in_b"]),
        "attn": lambda g: [("scatter", g1["w_in_a"]), ("scatter", by_owner(g["w_out"]))],
        "d_w_in_a": small_jobs,
        "d_w_in_b": lambda g: ("scatter", g["w_in_a"]),
        "d_h": lambda g: ("scatter", g["w_in_b"])})
    in_parts = ((got0["d_w_in_b"][0], got0["d_h"][0]), (got0["attn"][0], got0["gla"][0]))
    out_parts = ((got0["attn"][1],), (got1["attn"][0],))
    small_parts, rep_parts = got0["d_w_in_a"]
    norm_pack = _pack([jnp.stack([g0["norm_g"][0], g1["norm_g"][0]])], F32)

    out = {}
    out["w_in"] = _adam_columns(in_parts, w_in, m_w_in, v_w_in, "adam_w_in", 256)
    *out["w_out"], norm_parts = _adam_layers(out_parts, w_out, m_w_out, v_w_out, "adam_w_out", 128,
                                             rider=("gather", norm_pack))
    for names, parts, label in ((SMALL_SHARDED, small_parts, "adam_small"),
                                (REPLICATED_EARLY, rep_parts, "adam_replicated"), (("norm_g",), norm_parts, "adam_norm")):
        res = _adam(parts, _pack([w[n] for n in names], F32), _pack([m[n] for n in names], F32),
                    _pack([v[n] for n in names], F32), label, 2048)
        for n, *vals in zip(names, *[_unpack(a, [w[n] for n in names]) for a in res]):
            out[n] = vals

    return (loss, dx[None], *[out[n][0] for n in WEIGHTS], *[out[n][1] for n in WEIGHTS],
            *[out[n][2] for n in WEIGHTS], *[out[n][3] for n in WEIGHTS])
```

```python
import functools
import math

import numpy as np
import jax
import jax.numpy as jnp
from jax import lax
from jax.experimental import pallas as pl
from jax.experimental.pallas import tpu as pltpu

F32 = jnp.float32
BF16 = jnp.bfloat16

N_DEV = 8
D_MODEL = 2048
DEPTH = 2
GROUP_W = 512
EPS = 1e-6
ROPE_THETA = 10000.0
LANES = 128

RET_HD = 128
RET_CHUNK = 256
RET_UNROLL = 4
GLA_CHUNK = 64
GLA_UNROLL = 8
GLA_CUM_ROWS = 256
GLA_DK = 64
GLA_TAU = 16.0
GLA_RANK = 16
POOL_WINDOWS = (2, 4, 8, 16)
MLA_QK = 192
MLA_ROPE = 64
ATTN_SCALE = MLA_QK ** -0.5
ATTN_Q_SCALE = ATTN_SCALE * math.log2(math.e)
IN_COLS = 5984

ADAM_LR = 0.001
ADAM_B1 = 0.9
ADAM_B2 = 0.999
ADAM_EPS = 1e-08
ADAM_WD = 0.01
ADAM_STEP = 10

A_Q, A_K, A_V, A_G = 0, 4, 8, 12
B_Q, B_K, B_V, B_G = 16, 18, 20, 24
M_Q, C_V, C_G, M_G = 28, 32, 36, 40
M_KV, GA, M_KR = 44, 46, 47
ZP_COLS = 48 * LANES

VMEM_LIMIT = 56 * 1024 * 1024


def _params(sem, vmem=VMEM_LIMIT):
    return pltpu.CompilerParams(dimension_semantics=sem, vmem_limit_bytes=vmem)


def _sigmoid(x):
    return 1.0 / (1.0 + jnp.exp(-x))


def _silu(x):
    return x * _sigmoid(x)


def _silu_grad(x):
    s = _sigmoid(x)
    return s * (1.0 + x * (1.0 - s))


def _dot(a, b, dims=(((1,), (0,)), ((), ()))):
    return lax.dot_general(a, b, dims, preferred_element_type=F32)


NT = (((1,), (1,)), ((), ()))
TN = (((0,), (0,)), ((), ()))


def _chunk_loop(n, body, init, unroll):
    unroll = math.gcd(n, unroll)

    def trip(t, carry):
        for u in range(unroll):
            carry = body(t * unroll + u, carry)
        return carry

    return lax.fori_loop(0, n // unroll, trip, init)


def _roll_lanes_half(x):
    return pltpu.roll(x, 64, 1)


def _wq_perm():
    idx = np.zeros((1024,), np.int32)
    ok = np.zeros((1024,), bool)
    for h in range(4):
        idx[128 * h:128 * h + 128] = 192 * h + np.arange(128)
        ok[128 * h:128 * h + 128] = True
        base = 512 + 128 * h
        idx[base:base + 32] = 192 * h + 128 + np.arange(32)
        ok[base:base + 32] = True
        idx[base + 64:base + 96] = 192 * h + 160 + np.arange(32)
        ok[base + 64:base + 96] = True
    inv = np.zeros((768,), np.int32)
    inv[idx[ok]] = np.nonzero(ok)[0]
    return idx, ok, inv


_WQ_IDX, _WQ_OK, _WQ_INV = _wq_perm()


def _pad_wq(wq):
    return jnp.where(jnp.asarray(_WQ_OK)[None, :], wq[:, _WQ_IDX], 0).astype(wq.dtype)


def _unpad_wq(wqp):
    return wqp[:, _WQ_INV]


def _qk_idx():
    idx = np.zeros((256,), np.int32)
    ok = np.zeros((256,), bool)
    idx[0:128] = np.arange(128)
    ok[0:128] = True
    idx[128:160] = 128 + np.arange(32)
    ok[128:160] = True
    idx[192:224] = 160 + np.arange(32)
    ok[192:224] = True
    inv = np.zeros((192,), np.int32)
    inv[idx[ok]] = np.nonzero(ok)[0]
    return idx, ok, inv


_QK_IDX, _QK_OK, _QK_INV = _qk_idx()


def _pad_qk_gain(g):
    return jnp.where(jnp.asarray(_QK_OK), g[_QK_IDX], 0.0).reshape(1, 256)


def _rope_tables(s):
    def tabs(dim):
        inv = 1.0 / (ROPE_THETA ** (jnp.arange(0, dim, 2, dtype=F32) / dim))
        ang = jnp.arange(s, dtype=F32)[:, None] * inv[None, :]
        return jnp.cos(ang), jnp.sin(ang)
    cr, sr = tabs(RET_HD)
    cos_r = jnp.concatenate([cr, cr], axis=1)
    sin_r = jnp.concatenate([-sr, sr], axis=1)
    cm, sm = tabs(MLA_ROPE)
    zz = jnp.zeros_like(cm)
    cos_m = jnp.concatenate([cm, zz, cm, zz], axis=1)
    sin_m = jnp.concatenate([-sm, zz, sm, zz], axis=1)
    return cos_r, sin_r, cos_m, sin_m


def _rope(x, cos, sin):
    return x * cos + _roll_lanes_half(x) * sin


def _rope_t(x, cos, sin):
    return x * cos + _roll_lanes_half(x * sin)


def _ret_tables():
    c = RET_CHUNK
    gamma_f = 1.0 - 2.0 ** (-5.0 - jnp.arange(4, dtype=F32))
    gamma_b = gamma_f[::-1]
    idx = jnp.arange(c, dtype=F32)
    diff = idx[:, None] - idx[None, :]

    def build(g1, g2):
        l1 = jnp.log(g1)[:, None, None]
        l2 = jnp.log(g2)[:, None, None]
        d1 = jnp.where(diff >= 0, jnp.exp(jnp.maximum(diff, 0.0)[None] * l1), 0.0)
        d2 = jnp.where(diff <= 0, jnp.exp(jnp.maximum(-diff, 0.0)[None] * l2), 0.0)
        ones = jnp.ones((1, c, LANES), F32)
        col = idx[None, :, None]
        qdf = jnp.exp((col + 1.0) * l1) * ones
        kdf = jnp.exp((c - 1.0 - col) * l1) * ones
        qdb = jnp.exp((c - col) * l2) * ones
        kdb = jnp.exp(col * l2) * ones
        cd1 = jnp.exp(c * l1) * ones
        cd2 = jnp.exp(c * l2) * ones
        return jnp.concatenate([d1 + d2, qdf, kdf, qdb, kdb, cd1, cd2], axis=2)

    return build(gamma_f, gamma_b), build(gamma_b, gamma_f)


MESH = pl.DeviceIdType.MESH
ANY = pl.BlockSpec(memory_space=pl.ANY)
_RELATIONS = ((0, 0, 1), (1, 0, 0), (0, 1, 0), (1, 1, 0), (1, 0, 1), (0, 1, 1), (1, 1, 1))


def _position():
    return lax.axis_index("x"), lax.axis_index("y"), lax.axis_index("c")


def _gather_copies(x_ref, out_ref, send_sems, recv_sems, local_sem, starting):
    x, y, cc = _position()
    me, sibling = (x, y, cc), (x, y, 1 - cc)
    chips = [(1 - x, y), (x, 1 - y), (1 - x, 1 - y)]

    def slab(px, py, pc):
        return out_ref.at[4 * px + 2 * py + pc]

    def copy(k, block, to, src=None):
        return pltpu.make_async_remote_copy(
            src_ref=slab(*block) if src is None else src, dst_ref=slab(*block),
            send_sem=send_sems.at[k], recv_sem=recv_sems.at[k], device_id=to, device_id_type=MESH)

    mine = pltpu.make_async_copy(x_ref, slab(*me), local_sem)
    first = [copy(0, me, sibling, src=x_ref)] + [copy(1 + j, me, (*chip, cc), src=x_ref) for j, chip in enumerate(chips)]
    if starting:
        return mine, first
    passed = [copy(4 + j, (*chip, cc), sibling) for j, chip in enumerate(chips)]
    arrivals = [copy(1 + j, (*chip, cc), me) for j, chip in enumerate(chips)]
    late = [copy(0, sibling, me)] + [copy(4 + j, (*chip, 1 - cc), me) for j, chip in enumerate(chips)]
    return mine, first, passed, arrivals, late


def _gather_start(*refs):
    mine, first = _gather_copies(*refs, starting=True)
    mine.start()
    for cp in first:
        cp.start()


def _gather_finish(*refs):
    mine, first, passed, arrivals, late = _gather_copies(*refs, starting=False)
    for arrived, onward in zip(arrivals, passed):
        arrived.wait_recv()
        onward.start()
    for cp in late:
        cp.wait_recv()
    for cp in first + passed:
        cp.wait_send()
    mine.wait()


def _scatter_copies(c_ref, out_ref, send_sems, recv_sems, local_sem):
    x, y, cc = _position()
    me = 4 * x + 2 * y + cc
    mine = pltpu.make_async_copy(c_ref.at[me], out_ref.at[me], local_sem)
    copies = []
    for k, (fx, fy, fc) in enumerate(_RELATIONS):
        px = 1 - x if fx else x
        py = 1 - y if fy else y
        pc = 1 - cc if fc else cc
        copies.append(pltpu.make_async_remote_copy(
            src_ref=c_ref.at[4 * px + 2 * py + pc], dst_ref=out_ref.at[me],
            send_sem=send_sems.at[k], recv_sem=recv_sems.at[k], device_id=(px, py, pc), device_id_type=MESH))
    return mine, copies


def _scatter_start(*refs):
    mine, copies = _scatter_copies(*refs)
    mine.start()
    for cp in copies:
        cp.start()


def _scatter_finish(*refs):
    mine, copies = _scatter_copies(*refs)
    for cp in copies:
        cp.wait()
    mine.wait()


_EXCHANGES = {"gather": (_gather_start, _gather_finish), "scatter": (_scatter_start, _scatter_finish)}


def _exchange_scratch():
    return [pltpu.SemaphoreType.DMA((7,)), pltpu.SemaphoreType.DMA((7,)), pltpu.SemaphoreType.DMA]


def _exchange_out(kind, src):
    return jax.ShapeDtypeStruct(((N_DEV,) + src.shape) if kind == "gather" else src.shape, src.dtype)


def _exchange(jobs, name):
    n = len(jobs)

    def body(*refs):
        srcs, outs, sems = refs[:n], refs[n:2 * n], refs[2 * n:]
        for half in (0, 1):
            for i, (kind, _) in enumerate(jobs):
                _EXCHANGES[kind][half](srcs[i], outs[i], *sems[3 * i:3 * i + 3])

    return pl.pallas_call(
        body, name=name, out_shape=[_exchange_out(kind, src) for kind, src in jobs],
        in_specs=[ANY] * n, out_specs=[ANY] * n,
        scratch_shapes=[sem for _ in jobs for sem in _exchange_scratch()])(*[src for _, src in jobs])


def _call(body, name, grid, in_specs, out_specs, out_shape, scratch, sem, args, rider=None):
    if rider is None:
        return pl.pallas_call(body, name=name, grid=grid, in_specs=in_specs, out_specs=out_specs, out_shape=out_shape,
                              scratch_shapes=scratch, compiler_params=_params(sem))(*args)
    jobs = rider if isinstance(rider, list) else [rider]
    ni, no, ns, nj = len(in_specs), len(out_specs), len(scratch), len(jobs)

    def carried(*refs):
        ins, rsrcs = refs[:ni], refs[ni:ni + nj]
        outs, routs = refs[ni + nj:ni + nj + no], refs[ni + nj + no:ni + 2 * nj + no]
        scr, sems = refs[ni + 2 * nj + no:ni + 2 * nj + no + ns], refs[ni + 2 * nj + no + ns:]
        ids = [pl.program_id(a) for a in range(len(grid))]
        is_first = functools.reduce(jnp.logical_and, [i == 0 for i in ids])
        is_last = functools.reduce(jnp.logical_and, [i == g - 1 for i, g in zip(ids, grid)])

        def half(which):
            for j, (kind, _) in enumerate(jobs):
                _EXCHANGES[kind][which](rsrcs[j], routs[j], *sems[3 * j:3 * j + 3])

        @pl.when(is_first)
        def _():
            half(0)

        body(*ins, *outs, *scr)

        @pl.when(is_last)
        def _():
            half(1)

    return pl.pallas_call(
        carried, name=name, grid=grid, in_specs=list(in_specs) + [ANY] * nj, out_specs=list(out_specs) + [ANY] * nj,
        out_shape=list(out_shape) + [_exchange_out(kind, src) for kind, src in jobs],
        scratch_shapes=list(scratch) + [sem for _ in jobs for sem in _exchange_scratch()],
        compiler_params=_params(("arbitrary",) * len(grid)))(*args, *[src for _, src in jobs])


def _inproj(x, g, wt, tm, tn=512, rider=None):
    s, d = x.shape
    n = wt.shape[0]

    def body(x_ref, g_ref, w_ref, z_ref, h_ref, hs):
        @pl.when(pl.program_id(1) == 0)
        def _():
            xv = x_ref[...]
            r = lax.rsqrt(jnp.mean(xv * xv, axis=-1, keepdims=True) + EPS)
            hv = (xv * r * g_ref[...]).astype(BF16)
            hs[...] = hv
            h_ref[...] = hv
        z_ref[...] = _dot(hs[...], w_ref[...], NT)

    return _call(
        body, "inproj", (s // tm, n // tn),
        [pl.BlockSpec((tm, d), lambda i, j: (i, 0)),
         pl.BlockSpec((1, d), lambda i, j: (0, 0)),
         pl.BlockSpec((tn, d), lambda i, j: (j, 0))],
        [pl.BlockSpec((tm, tn), lambda i, j: (i, j)), pl.BlockSpec((tm, d), lambda i, j: (i, 0))],
        [jax.ShapeDtypeStruct((s, n), F32), jax.ShapeDtypeStruct((s, d), BF16)],
        [pltpu.VMEM((tm, d), BF16)], ("parallel", "arbitrary"), (x, g, wt), rider)


def _relayout_plan():
    runs = ((0, 3584, 0), (3584, 3616, GA * LANES), (3616, 4640, C_V * LANES), (4640, 5152, M_Q * LANES),
            (5152, 5408, M_KV * LANES), (5408, 5440, M_KR * LANES), (5440, 5472, M_KR * LANES + 64),
            (5472, 5984, M_G * LANES))
    shard = IN_COLS // N_DEV
    plan = []
    for d in range(N_DEV):
        lo, hi = shard * d, shard * (d + 1)
        for a, b, p in runs:
            s, e = max(a, lo), min(b, hi)
            if s < e:
                plan.append((d, s - lo, p + (s - a), e - s))
    return plan


def _assemble_w_in(g, tc=512):
    _, c, r = g.shape
    tc = min(tc, r)

    def body(g_ref, o_ref):
        o_ref[...] = jnp.zeros_like(o_ref)
        for d, at, to, w in _relayout_plan():
            o_ref[to:to + w, :] = g_ref[d, at:at + w, :]

    return pl.pallas_call(
        body, name="assemble_w_in", grid=(r // tc,),
        in_specs=[pl.BlockSpec((N_DEV, c, tc), lambda i: (0, 0, i))],
        out_specs=pl.BlockSpec((ZP_COLS, tc), lambda i: (0, i)),
        out_shape=jax.ShapeDtypeStruct((ZP_COLS, r), g.dtype),
        compiler_params=_params(("parallel",)),
    )(g)


def _split_w_in(wt, tc=512):
    r = wt.shape[1]
    c = IN_COLS // N_DEV
    tc = min(tc, r)

    def body(w_ref, o_ref):
        for d, at, to, w in _relayout_plan():
            o_ref[d, at:at + w, :] = w_ref[to:to + w, :]

    return pl.pallas_call(
        body, name="split_w_in", grid=(r // tc,),
        in_specs=[pl.BlockSpec((ZP_COLS, tc), lambda i: (0, i))],
        out_specs=pl.BlockSpec((N_DEV, c, tc), lambda i: (0, 0, i)),
        out_shape=jax.ShapeDtypeStruct((N_DEV, c, r), wt.dtype),
        compiler_params=_params(("parallel",)),
    )(wt)


def _mm(a, b, mode, name, tm, tn, tk, add=None, out_dtype=F32, rider=None, tail=None):
    if mode == "tn":
        k, m = a.shape
    else:
        m, k = a.shape
    n = b.shape[0] if mode == "nt" else b.shape[1]
    tm, tn, tk = min(tm, m), min(tn, n), min(tk, k)
    nk = k // tk
    dims = {"nn": (((1,), (0,)), ((), ())), "nt": NT, "tn": TN}[mode]
    if tail is None:
        def plain(acc, i, extra_refs, out_refs):
            out_refs[0][...] = (acc + extra_refs[0][...] if extra_refs else acc).astype(out_dtype)
        tail = ([(add, "tile")] if add is not None else [], [(out_dtype, "tile")], plain)
    extra, outs, fn = tail
    spec = {"tile": pl.BlockSpec((tm, tn), lambda i, j, kk: (i, j)),
            "row": pl.BlockSpec((1, tn), lambda i, j, kk: (0, j)),
            "lanes": pl.BlockSpec((1, LANES), lambda i, j, kk: (0, 0))}
    shape = {"tile": (m, n), "row": (1, n), "lanes": (1, LANES)}
    ne, no = len(extra), len(outs)

    def body(*refs):
        a_ref, b_ref = refs[:2]
        extra_refs, out_refs, acc = refs[2:2 + ne], refs[2 + ne:2 + ne + no], refs[2 + ne + no]
        i, kk = pl.program_id(0), pl.program_id(2)

        @pl.when(kk == 0)
        def _():
            acc[...] = jnp.zeros_like(acc)

        acc[...] += _dot(a_ref[...].astype(BF16), b_ref[...].astype(BF16), dims)

        @pl.when(kk == nk - 1)
        def _():
            fn(acc[...], i, extra_refs, out_refs)

    a_spec = (pl.BlockSpec((tk, tm), lambda i, j, kk: (kk, i)) if mode == "tn"
              else pl.BlockSpec((tm, tk), lambda i, j, kk: (i, kk)))
    b_spec = (pl.BlockSpec((tn, tk), lambda i, j, kk: (j, kk)) if mode == "nt"
              else pl.BlockSpec((tk, tn), lambda i, j, kk: (kk, j)))
    summed = any(kind != "tile" for _, kind in outs)
    res = _call(body, name, (m // tm, n // tn, nk), [a_spec, b_spec] + [spec[kind] for _, kind in extra],
                [spec[kind] for _, kind in outs], [jax.ShapeDtypeStruct(shape[kind], dt) for dt, kind in outs],
                [pltpu.VMEM((tm, tn), F32)], ("arbitrary",) * 3 if summed else ("parallel", "parallel", "arbitrary"),
                [a, b] + [arr for arr, _ in extra], rider)
    return res[0] if (rider is None and no == 1) else res


def _norm_bwd_tail(x, g, dres):
    def fn(dh, i, extra_refs, out_refs):
        x_ref, g_ref, dres_ref = extra_refs
        dx_ref, dg_ref = out_refs

        @pl.when(i == 0)
        def _():
            dg_ref[...] = jnp.zeros_like(dg_ref)

        xv = x_ref[...]
        r = lax.rsqrt(jnp.mean(xv * xv, axis=-1, keepdims=True) + EPS)
        nv = xv * r
        dg_ref[...] += jnp.sum(dh * nv, axis=0, keepdims=True)
        u = dh * g_ref[...]
        dx_ref[...] = dres_ref[...] + r * (u - nv * jnp.mean(u * nv, axis=-1, keepdims=True))

    return [(x, "tile"), (g, "row"), (dres, "tile")], [(F32, "tile"), (F32, "row")], fn


def _loss_tail(x, target):
    d = x.shape[1]

    def fn(acc, i, extra_refs, out_refs):
        x_ref, t_ref = extra_refs
        dx_ref, loss_ref = out_refs

        @pl.when(i == 0)
        def _():
            loss_ref[...] = jnp.zeros_like(loss_ref)

        err = acc + x_ref[...] - t_ref[...]
        dx_ref[...] = err * (1.0 / d)
        per_tok = jnp.mean(err * err, axis=-1, keepdims=True)
        loss_ref[...] += 0.5 * jnp.sum(per_tok, axis=0, keepdims=True)

    return [(x, "tile"), (target, "tile")], [(F32, "tile"), (F32, "lanes")], fn


def _ret_core(q_ref, k_ref, v_ref, tab_ref, out_ref, back_ref, nchunk):
    c = RET_CHUNK

    def rows(n):
        return pl.ds(pl.multiple_of(n * c, c), c)

    zero = jnp.zeros((LANES, LANES), F32)

    def plane(i, n=c):
        return tab_ref[0:n, c + LANES * i:c + LANES * (i + 1)]

    def fwd(n, st):
        r = rows(n)
        q, k, vb = q_ref[r, :], k_ref[r, :], v_ref[r, :].astype(BF16)
        sc = _dot(q.astype(BF16), k.astype(BF16), NT) * tab_ref[:, 0:c]
        o = _dot(sc.astype(BF16), vb)
        o = o + _dot((q * plane(0)).astype(BF16), st.astype(BF16))
        out_ref[r, :] = o
        return st * plane(4, LANES) + _dot((k * plane(1)).astype(BF16), vb, TN)

    def bwd(i, st):
        r = rows(nchunk - 1 - i)
        q, k, vb = q_ref[r, :], k_ref[r, :], v_ref[r, :].astype(BF16)
        back_ref[r, :] = _dot((q * plane(2)).astype(BF16), st.astype(BF16))
        return st * plane(5, LANES) + _dot((k * plane(3)).astype(BF16), vb, TN)

    def both(i, states):
        return fwd(i, states[0]), bwd(i, states[1])

    _chunk_loop(nchunk, both, (zero, zero), RET_UNROLL)
    out_ref[...] += back_ref[...]


def _ret_fwd(z, cos_r, sin_r, tab, norm_g):
    s = z.shape[0]
    nchunk = s // RET_CHUNK
    scale = RET_HD ** -0.5
    col = lambda base: pl.BlockSpec((s, LANES), lambda h: (0, base + h), pipeline_mode=pl.Buffered(1))

    def body(q_ref, k_ref, v_ref, g_ref, cos_ref, sin_ref, tab_ref, ng_ref, o_ref, y_ref, qh, kh, back):
        qh[...] = _rope(q_ref[...], cos_ref[...], sin_ref[...])
        kh[...] = _rope(k_ref[...], cos_ref[...], sin_ref[...]) * scale
        _ret_core(qh, kh, v_ref, tab_ref, o_ref, back, nchunk)
        o = o_ref[...]
        r = lax.rsqrt(jnp.mean(o * o, axis=-1, keepdims=True) + EPS)
        y_ref[...] = (_silu(g_ref[...]) * (o * r * ng_ref[...])).astype(BF16)

    return pl.pallas_call(
        body, name="ret_fwd", grid=(4,),
        in_specs=[col(A_Q), col(A_K), col(A_V), col(A_G),
                  pl.BlockSpec((s, LANES), lambda h: (0, 0), pipeline_mode=pl.Buffered(1)),
                  pl.BlockSpec((s, LANES), lambda h: (0, 0), pipeline_mode=pl.Buffered(1)),
                  pl.BlockSpec((None, RET_CHUNK, RET_CHUNK + 6 * LANES), lambda h: (h, 0, 0)),
                  pl.BlockSpec((1, LANES), lambda h: (0, h))],
        out_specs=[pl.BlockSpec((s, LANES), lambda h: (0, h)), pl.BlockSpec((s, LANES), lambda h: (0, h))],
        out_shape=[jax.ShapeDtypeStruct((s, GROUP_W), F32), jax.ShapeDtypeStruct((s, GROUP_W), BF16)],
        scratch_shapes=[pltpu.VMEM((s, LANES), F32)] * 3,
        compiler_params=_params(("arbitrary",)),
    )(z, z, z, z, cos_r, sin_r, tab, norm_g)


def _ret_bwd(z, d_o, cos_r, sin_r, tab, tab_sw, rider=None):
    s = z.shape[0]
    nchunk = s // RET_CHUNK
    scale = RET_HD ** -0.5
    col = lambda base: pl.BlockSpec((s, LANES), lambda h: (0, base + h), pipeline_mode=pl.Buffered(1))
    whole = lambda: pl.BlockSpec((s, LANES), lambda h: (0, 0), pipeline_mode=pl.Buffered(1))
    tabspec = lambda: pl.BlockSpec((None, RET_CHUNK, RET_CHUNK + 6 * LANES), lambda h: (h, 0, 0))
    outspec = lambda: pl.BlockSpec((s, LANES), lambda h: (0, h))

    def body(q_ref, k_ref, v_ref, do_ref, cos_ref, sin_ref, tab_ref, tsw_ref, dq_ref, dk_ref, dv_ref,
             qh, kh, tmp, back):
        cos, sin = cos_ref[...], sin_ref[...]
        qh[...] = _rope(q_ref[...], cos, sin)
        kh[...] = _rope(k_ref[...], cos, sin) * scale
        _ret_core(kh, qh, do_ref, tsw_ref, tmp, back, nchunk)
        dv_ref[...] = tmp[...].astype(BF16)
        _ret_core(do_ref, v_ref, kh, tab_ref, tmp, back, nchunk)
        dq_ref[...] = _rope_t(tmp[...], cos, sin).astype(BF16)
        _ret_core(v_ref, do_ref, qh, tsw_ref, tmp, back, nchunk)
        dk_ref[...] = _rope_t(tmp[...] * scale, cos, sin).astype(BF16)

    return _call(
        body, "ret_bwd", (4,),
        [col(A_Q), col(A_K), col(A_V),
         pl.BlockSpec((s, LANES), lambda h: (0, h), pipeline_mode=pl.Buffered(1)),
         whole(), whole(), tabspec(), tabspec()],
        [outspec(), outspec(), outspec()],
        [jax.ShapeDtypeStruct((s, GROUP_W), BF16)] * 3,
        [pltpu.VMEM((s, LANES), F32)] * 4,
        ("arbitrary",), (z, z, z, d_o, cos_r, sin_r, tab, tab_sw), rider)


def _normgate_bwd(o, z, gate_blk, dy, dy_blk, norm_g, tm):
    s = o.shape[0]

    def body(o_ref, g_ref, dy_ref, ng_ref, do_ref, dg_ref, dng_ref):
        @pl.when(pl.program_id(0) == 0)
        def _():
            dng_ref[...] = jnp.zeros_like(dng_ref)

        for h in range(4):
            sl = slice(LANES * h, LANES * (h + 1))
            ov, gv, dyv, ng = o_ref[:, sl], g_ref[:, sl], dy_ref[:, sl], ng_ref[:, sl]
            r = lax.rsqrt(jnp.mean(ov * ov, axis=-1, keepdims=True) + EPS)
            on = ov * r
            dn = dyv * _silu(gv)
            u = dn * ng
            do_ref[:, sl] = r * (u - on * jnp.mean(u * on, axis=-1, keepdims=True))
            dg_ref[:, sl] = (dyv * (on * ng) * _silu_grad(gv)).astype(BF16)
            dng_ref[:, sl] += jnp.sum(dn * on, axis=0, keepdims=True)

    return pl.pallas_call(
        body, name="normgate_bwd", grid=(s // tm,),
        in_specs=[pl.BlockSpec((tm, GROUP_W), lambda i: (i, 0)),
                  pl.BlockSpec((tm, GROUP_W), lambda i: (i, gate_blk // 4)),
                  pl.BlockSpec((tm, GROUP_W), lambda i: (i, dy_blk)),
                  pl.BlockSpec((1, GROUP_W), lambda i: (0, 0))],
        out_specs=[pl.BlockSpec((tm, GROUP_W), lambda i: (i, 0)), pl.BlockSpec((tm, GROUP_W), lambda i: (i, 0)),
                   pl.BlockSpec((1, GROUP_W), lambda i: (0, 0))],
        out_shape=[jax.ShapeDtypeStruct((s, GROUP_W), F32), jax.ShapeDtypeStruct((s, GROUP_W), BF16),
                   jax.ShapeDtypeStruct((1, GROUP_W), F32)],
        compiler_params=_params(("arbitrary",)),
    )(o, z, dy, norm_g)


def _log_sigmoid(x):
    return jnp.minimum(x, 0.0) - jnp.log(1.0 + jnp.exp(-jnp.abs(x)))


def _gla_consts():
    c = GLA_CHUNK
    row = lax.broadcasted_iota(jnp.int32, (c, c), 0)
    colm = lax.broadcasted_iota(jnp.int32, (c, c), 1)
    lane = lax.broadcasted_iota(jnp.int32, (1, LANES), 1)
    low = row >= colm
    up = colm >= row
    heads = ((lane < GLA_DK).astype(F32), (lane >= GLA_DK).astype(F32))
    return low, up, heads


def _chunk_running_sum(x, suffix):
    rows = x.shape[0]
    pos = jnp.bitwise_and(lax.broadcasted_iota(jnp.int32, (rows, 1), 0), GLA_CHUNK - 1)
    k = 1
    while k < GLA_CHUNK:
        if suffix:
            x = x + jnp.where(pos < GLA_CHUNK - k, pltpu.roll(x, rows - k, 0), 0.0)
        else:
            x = x + jnp.where(pos >= k, pltpu.roll(x, k, 0), 0.0)
        k *= 2
    return x


def _gla_chunk(cum_ref, d, n):
    c = GLA_CHUNK
    cum = cum_ref[d, pl.ds(pl.multiple_of(n * c, c), c), :]
    last = cum_ref[d, pl.ds(n * c + (c - 1 if d == 0 else 0), 1), :]
    eq = jnp.exp(cum)
    ek = jnp.exp(-cum)
    el = jnp.exp(last - cum)
    dec = jnp.exp(last)
    return eq, ek, el, dec


def _gla_gates(ga_ref, wa_ref, ba_ref, cum_ref, s, suffix):
    rows = min(s, GLA_CUM_ROWS)

    def step(i, carry):
        r = pl.ds(pl.multiple_of(i * rows, rows), rows)
        pre = _dot(ga_ref[r, :].astype(BF16), wa_ref[...].astype(BF16)) + ba_ref[...]
        cum_ref[r, :] = _chunk_running_sum(_log_sigmoid(pre) * (1.0 / GLA_TAU), suffix)
        return carry
    lax.fori_loop(0, s // rows, step, 0)


def _gla_fwd(z, wa_f, wa_b, ba_f, ba_b, norm_g):
    s = z.shape[0]
    c = GLA_CHUNK
    nchunk = s // c
    scale = GLA_DK ** -0.5
    tm = min(s, 512)
    one = pl.Buffered(1)

    def body(q_ref, k_ref, v_ref, ga_ref, g_ref, waf_ref, wab_ref, baf_ref, bab_ref, ng_ref, o_ref, y_ref,
             la_s):
        low, up, heads = _gla_consts()
        _gla_gates(ga_ref, waf_ref, baf_ref, la_s.at[0], s, False)
        _gla_gates(ga_ref, wab_ref, bab_ref, la_s.at[1], s, True)
        for d in range(2):
            tri = (low, up)[d]

            def step(i, states):
                n = i if d == 0 else nchunk - 1 - i
                r = pl.ds(pl.multiple_of(n * c, c), c)
                q = q_ref[r, :] * scale
                k = k_ref[r, :]
                eq, ek, el, dec = _gla_chunk(la_s, d, n)
                qt = q * eq
                ktb = (k * ek).astype(BF16)
                kl = k * el
                new_states = []
                for hh in range(2):
                    cols = slice(LANES * hh, LANES * (hh + 1))
                    vb = v_ref[r, cols].astype(BF16)
                    qm = (qt * heads[hh]).astype(BF16)
                    a = jnp.where(tri, _dot(qm, ktb, NT), 0.0)
                    o = _dot(a.astype(BF16), vb) + _dot(qm, states[hh].astype(BF16), NT)
                    if d == 0:
                        o_ref[r, cols] = o
                    else:
                        o_ref[r, cols] += o
                    new_states.append(states[hh] * dec + _dot(vb, (kl * heads[hh]).astype(BF16), TN))
                return tuple(new_states)

            zero = jnp.zeros((LANES, LANES), F32)
            _chunk_loop(nchunk, step, (zero, zero), GLA_UNROLL)

        def epi(i, carry):
            r = pl.ds(pl.multiple_of(i * tm, tm), tm)
            for hh in range(2):
                cols = slice(LANES * hh, LANES * (hh + 1))
                o = o_ref[r, cols]
                rr = lax.rsqrt(jnp.mean(o * o, axis=-1, keepdims=True) + EPS)
                y_ref[r, cols] = (_silu(g_ref[r, cols]) * (o * rr * ng_ref[:, cols])).astype(BF16)
            return carry

        lax.fori_loop(0, s // tm, epi, 0)

    w2 = 2 * LANES
    return pl.pallas_call(
        body, name="gla_fwd", grid=(2,),
        in_specs=[pl.BlockSpec((s, LANES), lambda p: (0, B_Q + p), pipeline_mode=one),
                  pl.BlockSpec((s, LANES), lambda p: (0, B_K + p), pipeline_mode=one),
                  pl.BlockSpec((s, w2), lambda p: (0, B_V // 2 + p), pipeline_mode=one),
                  pl.BlockSpec((s, LANES), lambda p: (0, GA), pipeline_mode=one),
                  pl.BlockSpec((s, w2), lambda p: (0, B_G // 2 + p), pipeline_mode=one),
                  pl.BlockSpec((LANES, LANES), lambda p: (0, p)),
                  pl.BlockSpec((LANES, LANES), lambda p: (0, p)),
                  pl.BlockSpec((1, LANES), lambda p: (0, p)),
                  pl.BlockSpec((1, LANES), lambda p: (0, p)),
                  pl.BlockSpec((1, w2), lambda p: (0, p))],
        out_specs=[pl.BlockSpec((s, w2), lambda p: (0, p)), pl.BlockSpec((s, w2), lambda p: (0, p))],
        out_shape=[jax.ShapeDtypeStruct((s, GROUP_W), F32), jax.ShapeDtypeStruct((s, GROUP_W), BF16)],
        scratch_shapes=[pltpu.VMEM((2, s, LANES), F32)],
        compiler_params=_params(("arbitrary",)),
    )(z, z, z, z, z, wa_f, wa_b, ba_f, ba_b, norm_g)


def _gla_bwd(z, d_o, wa_f, wa_b, ba_f, ba_b, rider=None):
    s = z.shape[0]
    c = GLA_CHUNK
    nchunk = s // c
    scale = GLA_DK ** -0.5
    tm = min(s, GLA_CUM_ROWS)
    one = pl.Buffered(1)

    def body(q_ref, k_ref, v_ref, ga_ref, do_ref, waf_ref, wab_ref, baf_ref, bab_ref,
             dq_ref, dk_ref, dv_ref, dga_ref, dwaf_ref, dwab_ref, dbaf_ref, dbab_ref,
             la_s, dla_s, stash, dq_s, dk_s, dv_s):
        low, up, heads = _gla_consts()
        rowi = lax.broadcasted_iota(jnp.int32, (c, 1), 0)
        _gla_gates(ga_ref, waf_ref, baf_ref, la_s.at[0], s, False)
        _gla_gates(ga_ref, wab_ref, bab_ref, la_s.at[1], s, True)
        for d in range(2):
            tri = (low, up)[d]
            last_row = (rowi == (c - 1 if d == 0 else 0)).astype(F32)
            order = (lambda i: i) if d == 0 else (lambda i: nchunk - 1 - i)
            zero = jnp.zeros((LANES, LANES), F32)

            def states(i, sts):
                n = order(i)
                r = pl.ds(pl.multiple_of(n * c, c), c)
                k = k_ref[r, :]
                _, _, el, dec = _gla_chunk(la_s, d, n)
                kl = k * el
                new = []
                for hh in range(2):
                    cols = slice(LANES * hh, LANES * (hh + 1))
                    stash[hh, n] = sts[hh]
                    new.append(sts[hh] * dec + _dot(v_ref[r, cols].astype(BF16), (kl * heads[hh]).astype(BF16), TN))
                return tuple(new)

            _chunk_loop(nchunk, states, (zero, zero), GLA_UNROLL)

            def step(i, dsts):
                n = order(nchunk - 1 - i)
                r = pl.ds(pl.multiple_of(n * c, c), c)
                q = q_ref[r, :] * scale
                k = k_ref[r, :]
                eq, ek, el, dec = _gla_chunk(la_s, d, n)
                qt = q * eq
                kt = k * ek
                kl = k * el
                ktb = kt.astype(BF16)
                dqt = jnp.zeros((c, LANES), F32)
                dkt = jnp.zeros((c, LANES), F32)
                dkl = jnp.zeros((c, LANES), F32)
                ddec = jnp.zeros((1, LANES), F32)
                new = []
                for hh in range(2):
                    cols = slice(LANES * hh, LANES * (hh + 1))
                    vb = v_ref[r, cols].astype(BF16)
                    dob = do_ref[r, cols].astype(BF16)
                    qm = (qt * heads[hh]).astype(BF16)
                    a = jnp.where(tri, _dot(qm, ktb, NT), 0.0).astype(BF16)
                    da = jnp.where(tri, _dot(dob, vb, NT), 0.0).astype(BF16)
                    sn = stash[hh, n]
                    dst = dsts[hh]
                    dstb = dst.astype(BF16)
                    dqt = dqt + (_dot(da, ktb) + _dot(dob, sn.astype(BF16))) * heads[hh]
                    dkt = dkt + _dot(da, qm, TN)
                    dv = _dot(a, dob, TN) + _dot((kl * heads[hh]).astype(BF16), dstb, NT)
                    dkl = dkl + _dot(vb, dstb)
                    ddec = ddec + jnp.sum(dst * sn, axis=0, keepdims=True)
                    new.append(dst * dec + _dot(dob, qm, TN))
                    if d == 0:
                        dv_s[r, cols] = dv
                    else:
                        dv_ref[r, cols] = (dv_s[r, cols] + dv).astype(BF16)
                dlast = ddec * dec + jnp.sum(dkl * kl, axis=0, keepdims=True)
                dq = dqt * eq * scale
                dk = dkt * ek + dkl * el
                dcum = dqt * qt - dkt * kt - dkl * kl + last_row * dlast
                dla_s[d, r, :] = dcum
                if d == 0:
                    dq_s[r, :] = dq
                    dk_s[r, :] = dk
                else:
                    dq_ref[r, :] = (dq_s[r, :] + dq).astype(BF16)
                    dk_ref[r, :] = (dk_s[r, :] + dk).astype(BF16)
                return tuple(new)

            _chunk_loop(nchunk, step, (zero, zero), GLA_UNROLL)

        first = pl.program_id(0) == 0
        for d, (wa_ref, ba_ref, dwa_ref, dba_ref) in enumerate(
                ((waf_ref, baf_ref, dwaf_ref, dbaf_ref), (wab_ref, bab_ref, dwab_ref, dbab_ref))):
            dwa_ref[...] = jnp.zeros_like(dwa_ref)
            dba_ref[...] = jnp.zeros_like(dba_ref)

            def gates(i, carry):
                r = pl.ds(pl.multiple_of(i * tm, tm), tm)
                gab = ga_ref[r, :].astype(BF16)
                wab16 = wa_ref[...].astype(BF16)
                pre = _dot(gab, wab16) + ba_ref[...]
                dla = _chunk_running_sum(dla_s[d, r, :], suffix=(d == 0))
                dpre = dla * (1.0 / GLA_TAU) * _sigmoid(-pre)
                dpb = dpre.astype(BF16)
                dwa_ref[...] += _dot(gab, dpb, TN)
                dba_ref[...] += jnp.sum(dpre, axis=0, keepdims=True)
                dga = _dot(dpb, wab16, NT)
                if d == 0:
                    @pl.when(first)
                    def _():
                        dga_ref[r, :] = dga

                    @pl.when(jnp.logical_not(first))
                    def _():
                        dga_ref[r, :] += dga
                else:
                    dga_ref[r, :] += dga
                return carry

            lax.fori_loop(0, s // tm, gates, 0)

    w2 = 2 * LANES
    return _call(
        body, "gla_bwd", (2,),
        [pl.BlockSpec((s, LANES), lambda p: (0, B_Q + p), pipeline_mode=one),
         pl.BlockSpec((s, LANES), lambda p: (0, B_K + p), pipeline_mode=one),
         pl.BlockSpec((s, w2), lambda p: (0, B_V // 2 + p), pipeline_mode=one),
         pl.BlockSpec((s, LANES), lambda p: (0, GA), pipeline_mode=one),
         pl.BlockSpec((s, w2), lambda p: (0, p), pipeline_mode=one),
         pl.BlockSpec((LANES, LANES), lambda p: (0, p)),
         pl.BlockSpec((LANES, LANES), lambda p: (0, p)),
         pl.BlockSpec((1, LANES), lambda p: (0, p)),
         pl.BlockSpec((1, LANES), lambda p: (0, p))],
        [pl.BlockSpec((s, LANES), lambda p: (0, p), pipeline_mode=one),
         pl.BlockSpec((s, LANES), lambda p: (0, p), pipeline_mode=one),
         pl.BlockSpec((s, w2), lambda p: (0, p), pipeline_mode=one),
         pl.BlockSpec((s, LANES), lambda p: (0, 0), pipeline_mode=one),
         pl.BlockSpec((LANES, LANES), lambda p: (0, p)),
         pl.BlockSpec((LANES, LANES), lambda p: (0, p)),
         pl.BlockSpec((1, LANES), lambda p: (0, p)),
         pl.BlockSpec((1, LANES), lambda p: (0, p))],
        [jax.ShapeDtypeStruct((s, w2), BF16), jax.ShapeDtypeStruct((s, w2), BF16),
         jax.ShapeDtypeStruct((s, GROUP_W), BF16), jax.ShapeDtypeStruct((s, LANES), F32),
         jax.ShapeDtypeStruct((LANES, w2), F32), jax.ShapeDtypeStruct((LANES, w2), F32),
         jax.ShapeDtypeStruct((1, w2), F32), jax.ShapeDtypeStruct((1, w2), F32)],
        [pltpu.VMEM((2, s, LANES), F32), pltpu.VMEM((2, s, LANES), F32),
         pltpu.VMEM((2, nchunk, LANES, LANES), F32),
         pltpu.VMEM((s, LANES), F32), pltpu.VMEM((s, LANES), F32), pltpu.VMEM((s, w2), F32)],
        ("arbitrary",), (z, z, z, z, d_o, wa_f, wa_b, ba_f, ba_b), rider)


def _shift_rows(x, d, rowi):
    s = x.shape[0]
    if d == 0:
        return x
    y = pltpu.roll(x, d % s, 0)
    keep = (rowi >= d) if d > 0 else (rowi < s + d)
    return jnp.where(keep, y, 0.0)


def _run_sum(x, m, step, rowi):
    acc, n = x, 1
    while n < m:
        acc = acc + _shift_rows(acc, step * n, rowi)
        n *= 2
    return acc


def _pool_counts(s, w, rowi):
    hi = jnp.minimum(rowi + w // 2, s)
    lo = jnp.maximum(rowi - w // 2, 0)
    return (hi - lo).astype(F32)


def _pooled(u, w, rowi):
    s = u.shape[0]
    win = _shift_rows(_run_sum(u, w // 2, 1, rowi), 1, rowi) + _run_sum(u, w // 2, -1, rowi)
    return win / _pool_counts(s, w, rowi) - u


def _pool_fwd(z, pool_w, pool_scale):
    s = z.shape[0]
    one = pl.Buffered(1)

    def body(u_ref, g_ref, w_ref, sc_ref, y_ref):
        rowi = lax.broadcasted_iota(jnp.int32, (s, 1), 0)
        for g, w in enumerate(POOL_WINDOWS):
            cols = slice(LANES * g, LANES * (g + 1))
            pooled = _pooled(u_ref[:, cols], w, rowi)
            mixed = _dot(pooled.astype(BF16), w_ref[g].astype(BF16))
            y_ref[:, cols] = (_silu(g_ref[:, cols]) * (mixed * sc_ref[:, cols])).astype(BF16)

    return pl.pallas_call(
        body, name="pool_fwd", grid=(1,),
        in_specs=[pl.BlockSpec((s, GROUP_W), lambda i: (0, C_V // 4), pipeline_mode=one),
                  pl.BlockSpec((s, GROUP_W), lambda i: (0, C_G // 4), pipeline_mode=one),
                  pl.BlockSpec((4, LANES, LANES), lambda i: (0, 0, 0)),
                  pl.BlockSpec((1, GROUP_W), lambda i: (0, 0))],
        out_specs=pl.BlockSpec((s, GROUP_W), lambda i: (0, 0), pipeline_mode=one),
        out_shape=jax.ShapeDtypeStruct((s, GROUP_W), BF16),
        compiler_params=_params(("arbitrary",)),
    )(z, z, pool_w, pool_scale)


def _pool_bwd(z, dy, pool_w, pool_scale):
    s = z.shape[0]
    one = pl.Buffered(1)

    def body(u_ref, g_ref, dy_ref, w_ref, sc_ref, du_ref, dg_ref, dw_ref, dsc_ref):
        rowi = lax.broadcasted_iota(jnp.int32, (s, 1), 0)
        for g, w in enumerate(POOL_WINDOWS):
            cols = slice(LANES * g, LANES * (g + 1))
            gate, dyv, sc = g_ref[:, cols], dy_ref[:, cols], sc_ref[:, cols]
            wb = w_ref[g].astype(BF16)
            pooled = _pooled(u_ref[:, cols], w, rowi)
            pb = pooled.astype(BF16)
            mixed = _dot(pb, wb)
            dg_ref[:, cols] = (dyv * (mixed * sc) * _silu_grad(gate)).astype(BF16)
            dt = dyv * _silu(gate)
            dsc_ref[:, cols] = jnp.sum(dt * mixed, axis=0, keepdims=True)
            dmb = (dt * sc).astype(BF16)
            dw_ref[g] = _dot(pb, dmb, TN)
            dpool = _dot(dmb, wb, NT)
            e = dpool / _pool_counts(s, w, rowi)
            du_ref[:, cols] = (_run_sum(e, w // 2, 1, rowi) + _shift_rows(_run_sum(e, w // 2, -1, rowi), -1, rowi)
                               - dpool).astype(BF16)

    return pl.pallas_call(
        body, name="pool_bwd", grid=(1,),
        in_specs=[pl.BlockSpec((s, GROUP_W), lambda i: (0, C_V // 4), pipeline_mode=one),
                  pl.BlockSpec((s, GROUP_W), lambda i: (0, C_G // 4), pipeline_mode=one),
                  pl.BlockSpec((s, GROUP_W), lambda i: (0, 2), pipeline_mode=one),
                  pl.BlockSpec((4, LANES, LANES), lambda i: (0, 0, 0)),
                  pl.BlockSpec((1, GROUP_W), lambda i: (0, 0))],
        out_specs=[pl.BlockSpec((s, GROUP_W), lambda i: (0, 0), pipeline_mode=one),
                   pl.BlockSpec((s, GROUP_W), lambda i: (0, 0), pipeline_mode=one),
                   pl.BlockSpec((4, LANES, LANES), lambda i: (0, 0, 0)),
                   pl.BlockSpec((1, GROUP_W), lambda i: (0, 0))],
        out_shape=[jax.ShapeDtypeStruct((s, GROUP_W), BF16), jax.ShapeDtypeStruct((s, GROUP_W), BF16),
                   jax.ShapeDtypeStruct((4, LANES, LANES), F32), jax.ShapeDtypeStruct((1, GROUP_W), F32)],
        compiler_params=_params(("arbitrary",)),
    )(z, z, dy, pool_w, pool_scale)


def _mla_heads(qf, kv, kpe, qg, kg, cos, sin):
    out = []
    for h in range(4):
        qa = qf[:, LANES * h:LANES * (h + 1)]
        qb = qf[:, 512 + LANES * h:512 + LANES * (h + 1)]
        ka = kv[:, 256 * h:256 * h + LANES]
        rq = lax.rsqrt((jnp.sum(qa * qa, axis=-1, keepdims=True) + jnp.sum(qb * qb, axis=-1, keepdims=True))
                       * (1.0 / MLA_QK) + EPS)
        rk = lax.rsqrt((jnp.sum(ka * ka, axis=-1, keepdims=True) + jnp.sum(kpe * kpe, axis=-1, keepdims=True))
                       * (1.0 / MLA_QK) + EPS)
        out.append((qa, qb, rq, ka, rk))
    return out


def _mla_latents(mq_ref, mkv_ref, gq_ref, gkv_ref, wq_ref, wkv_ref):
    mq = mq_ref[...]
    rq = lax.rsqrt(jnp.mean(mq * mq, axis=-1, keepdims=True) + EPS)
    qn = mq * rq
    qnb = (qn * gq_ref[...]).astype(BF16)
    mkv = mkv_ref[...]
    rk = lax.rsqrt(jnp.mean(mkv * mkv, axis=-1, keepdims=True) + EPS)
    kvn = mkv * rk
    kvnb = (kvn * gkv_ref[...]).astype(BF16)
    qf = _dot(qnb, wq_ref[...])
    kv = _dot(kvnb, wkv_ref[...])
    return qn, rq, qnb, kvn, rk, kvnb, qf, kv


def _mla_prep(z, cos_m, sin_m, gq, wq, gkv, wkv, qg, kg, tm):
    s = z.shape[0]

    def body(mq_ref, mkv_ref, mkr_ref, cos_ref, sin_ref, gq_ref, wq_ref, gkv_ref, wkv_ref, qg_ref, kg_ref,
             q_ref, k_ref, v_ref):
        _, _, _, _, _, _, qf, kv = _mla_latents(mq_ref, mkv_ref, gq_ref, gkv_ref, wq_ref, wkv_ref)
        kpe = mkr_ref[...]
        cos, sin = cos_ref[...], sin_ref[...]
        qg, kg = qg_ref[...], kg_ref[...]
        for h, (qa, qb, rq, ka, rk) in enumerate(_mla_heads(qf, kv, kpe, qg, kg, cos, sin)):
            q_ref[h, :, 0:LANES] = (qa * rq * qg[:, 0:LANES] * ATTN_Q_SCALE).astype(BF16)
            q_ref[h, :, LANES:] = (_rope(qb * rq * qg[:, LANES:], cos, sin) * ATTN_Q_SCALE).astype(BF16)
            k_ref[h, :, 0:LANES] = (ka * rk * kg[:, 0:LANES]).astype(BF16)
            k_ref[h, :, LANES:] = _rope(kpe * rk * kg[:, LANES:], cos, sin).astype(BF16)
            v_ref[h] = kv[:, 256 * h + LANES:256 * (h + 1)].astype(BF16)

    full = lambda shape: pl.BlockSpec(shape, lambda i: (0,) * len(shape))
    return pl.pallas_call(
        body, name="mla_prep", grid=(s // tm,),
        in_specs=[pl.BlockSpec((tm, 512), lambda i: (i, M_Q // 4)),
                  pl.BlockSpec((tm, 256), lambda i: (i, M_KV // 2)),
                  pl.BlockSpec((tm, LANES), lambda i: (i, M_KR)),
                  pl.BlockSpec((tm, LANES), lambda i: (i, 0)),
                  pl.BlockSpec((tm, LANES), lambda i: (i, 0)),
                  full((1, 512)), full((512, 1024)), full((1, 256)), full((256, 1024)), full((1, 256)), full((1, 256))],
        out_specs=[pl.BlockSpec((4, tm, 256), lambda i: (0, i, 0)), pl.BlockSpec((4, tm, 256), lambda i: (0, i, 0)),
                   pl.BlockSpec((4, tm, LANES), lambda i: (0, i, 0))],
        out_shape=[jax.ShapeDtypeStruct((4, s, 256), BF16), jax.ShapeDtypeStruct((4, s, 256), BF16),
                   jax.ShapeDtypeStruct((4, s, LANES), BF16)],
        compiler_params=_params(("parallel",)),
    )(z, z, z, cos_m, sin_m, gq, wq, gkv, wkv, qg, kg)


def _mla_prep_bwd(z, cos_m, sin_m, gq, wq, gkv, wkv, qg, kg, dq, dk, dv, tm):
    s = z.shape[0]

    def body(mq_ref, mkv_ref, mkr_ref, cos_ref, sin_ref, gq_ref, wq_ref, gkv_ref, wkv_ref, qg_ref, kg_ref,
             dq_ref, dk_ref, dv_ref,
             dmq_ref, dmkv_ref, dmkr_ref, dwq_ref, dwkv_ref, dgq_ref, dgkv_ref, dqg_ref, dkg_ref, dqf, dkv):
        @pl.when(pl.program_id(0) == 0)
        def _():
            for r in (dwq_ref, dwkv_ref, dgq_ref, dgkv_ref, dqg_ref, dkg_ref):
                r[...] = jnp.zeros_like(r)

        qn, rq0, qnb, kvn, rk0, kvnb, qf, kv = _mla_latents(mq_ref, mkv_ref, gq_ref, gkv_ref, wq_ref, wkv_ref)
        kpe = mkr_ref[...]
        cos, sin = cos_ref[...], sin_ref[...]
        qg, kg = qg_ref[...], kg_ref[...]
        dkpe = jnp.zeros_like(kpe)
        inv = 1.0 / MLA_QK

        def norm_bwd(a, b, r, da_n, db_n, g):
            ga, gb = g[:, 0:LANES], g[:, LANES:]
            dg_a = jnp.sum(da_n * a * r, axis=0, keepdims=True)
            dg_b = jnp.sum(db_n * b * r, axis=0, keepdims=True)
            ua, ub = da_n * ga, db_n * gb
            dt = (jnp.sum(ua * a, axis=-1, keepdims=True) + jnp.sum(ub * b, axis=-1, keepdims=True)) * inv
            r3 = r * r * r
            return r * ua - a * (r3 * dt), r * ub - b * (r3 * dt), dg_a, dg_b

        for h, (qa, qb, rq, ka, rk) in enumerate(_mla_heads(qf, kv, kpe, qg, kg, cos, sin)):
            dqa, dqb, dga, dgb = norm_bwd(qa, qb, rq, dq_ref[h, :, 0:LANES] * ATTN_SCALE,
                                          _rope_t(dq_ref[h, :, LANES:] * ATTN_SCALE, cos, sin), qg)
            dqf[:, LANES * h:LANES * (h + 1)] = dqa
            dqf[:, 512 + LANES * h:512 + LANES * (h + 1)] = dqb
            dqg_ref[:, 0:LANES] += dga
            dqg_ref[:, LANES:] += dgb
            ln2 = math.log(2.0)
            dka, dkb, dga, dgb = norm_bwd(ka, kpe, rk, dk_ref[h, :, 0:LANES] * ln2,
                                          _rope_t(dk_ref[h, :, LANES:] * ln2, cos, sin), kg)
            dkv[:, 256 * h:256 * h + LANES] = dka
            dkv[:, 256 * h + LANES:256 * (h + 1)] = dv_ref[h]
            dkpe = dkpe + dkb
            dkg_ref[:, 0:LANES] += dga
            dkg_ref[:, LANES:] += dgb
        dmkr_ref[...] = dkpe.astype(BF16)

        def latent_bwd(dfull, w_ref, nb, n, r, g_ref, dw_ref, dg_ref, dlat_ref):
            db = dfull.astype(BF16)
            dn = _dot(db, w_ref[...], NT)
            dw_ref[...] += _dot(nb, db, TN)
            dg_ref[...] += jnp.sum(dn * n, axis=0, keepdims=True)
            u = dn * g_ref[...]
            dlat_ref[...] = (r * (u - n * jnp.mean(u * n, axis=-1, keepdims=True))).astype(BF16)

        latent_bwd(dqf[...], wq_ref, qnb, qn, rq0, gq_ref, dwq_ref, dgq_ref, dmq_ref)
        latent_bwd(dkv[...], wkv_ref, kvnb, kvn, rk0, gkv_ref, dwkv_ref, dgkv_ref, dmkv_ref)

    full = lambda shape: pl.BlockSpec(shape, lambda i: (0,) * len(shape))
    return pl.pallas_call(
        body, name="mla_prep_bwd", grid=(s // tm,),
        in_specs=[pl.BlockSpec((tm, 512), lambda i: (i, M_Q // 4)),
                  pl.BlockSpec((tm, 256), lambda i: (i, M_KV // 2)),
                  pl.BlockSpec((tm, LANES), lambda i: (i, M_KR)),
                  pl.BlockSpec((tm, LANES), lambda i: (i, 0)),
                  pl.BlockSpec((tm, LANES), lambda i: (i, 0)),
                  full((1, 512)), full((512, 1024)), full((1, 256)), full((256, 1024)), full((1, 256)), full((1, 256)),
                  pl.BlockSpec((4, tm, 256), lambda i: (0, i, 0)), pl.BlockSpec((4, tm, 256), lambda i: (0, i, 0)),
                  pl.BlockSpec((4, tm, LANES), lambda i: (0, i, 0))],
        out_specs=[pl.BlockSpec((tm, 512), lambda i: (i, 0)), pl.BlockSpec((tm, 256), lambda i: (i, 0)),
                   pl.BlockSpec((tm, LANES), lambda i: (i, 0)),
                   full((512, 1024)), full((256, 1024)), full((1, 512)), full((1, 256)), full((1, 256)), full((1, 256))],
        out_shape=[jax.ShapeDtypeStruct((s, 512), BF16), jax.ShapeDtypeStruct((s, 256), BF16),
                   jax.ShapeDtypeStruct((s, LANES), BF16),
                   jax.ShapeDtypeStruct((512, 1024), F32), jax.ShapeDtypeStruct((256, 1024), F32),
                   jax.ShapeDtypeStruct((1, 512), F32), jax.ShapeDtypeStruct((1, 256), F32),
                   jax.ShapeDtypeStruct((1, 256), F32), jax.ShapeDtypeStruct((1, 256), F32)],
        scratch_shapes=[pltpu.VMEM((tm, 1024), F32), pltpu.VMEM((tm, 1024), F32)],
        compiler_params=_params(("arbitrary",)),
    )(z, z, z, cos_m, sin_m, gq, wq, gkv, wkv, qg, kg, dq, dk, dv)


def _attn_fwd(q, k, v, z, tq, rider=None):
    s = q.shape[1]

    def body(q_ref, k_ref, v_ref, g_ref, o_ref, y_ref, lse_ref):
        sc = _dot(q_ref[...], k_ref[...], NT)
        m = jnp.max(sc, axis=-1, keepdims=True)
        p = jnp.exp2(sc - m)
        l = jnp.sum(p, axis=-1, keepdims=True)
        o = _dot(p.astype(BF16), v_ref[...]) / l
        o_ref[...] = o
        y_ref[...] = (_silu(g_ref[...]) * o).astype(BF16)
        lse_ref[...] = m + jnp.log2(l)

    return _call(
        body, "attn_fwd", (4, s // tq),
        [pl.BlockSpec((None, tq, 256), lambda h, i: (h, i, 0)),
         pl.BlockSpec((None, s, 256), lambda h, i: (h, 0, 0)),
         pl.BlockSpec((None, s, LANES), lambda h, i: (h, 0, 0)),
         pl.BlockSpec((tq, LANES), lambda h, i: (i, M_G + h))],
        [pl.BlockSpec((tq, LANES), lambda h, i: (i, h)), pl.BlockSpec((tq, LANES), lambda h, i: (i, h)),
         pl.BlockSpec((None, tq, 1), lambda h, i: (h, i, 0))],
        [jax.ShapeDtypeStruct((s, GROUP_W), F32), jax.ShapeDtypeStruct((s, GROUP_W), BF16),
         jax.ShapeDtypeStruct((4, s, 1), F32)],
        [], ("parallel", "parallel"), (q, k, v, z), rider)


def _attn_bwd(q, k, v, z, o, lse, dy, tq, rider=None):
    s = q.shape[1]

    def body(q_ref, k_ref, v_ref, g_ref, o_ref, lse_ref, dy_ref, dq_ref, dk_ref, dv_ref, dg_ref):
        @pl.when(pl.program_id(1) == 0)
        def _():
            dk_ref[...] = jnp.zeros_like(dk_ref)
            dv_ref[...] = jnp.zeros_like(dv_ref)

        gate, ov, dyv = g_ref[...], o_ref[...], dy_ref[...]
        do = dyv * _silu(gate)
        dg_ref[...] = (dyv * ov * _silu_grad(gate)).astype(BF16)
        delta = jnp.sum(do * ov, axis=-1, keepdims=True)
        dob = do.astype(BF16)
        qb, kb = q_ref[...], k_ref[...]
        p = jnp.exp2(_dot(qb, kb, NT) - lse_ref[...])
        dp = _dot(dob, v_ref[...], NT)
        ds = (p * (dp - delta)).astype(BF16)
        dq_ref[...] = _dot(ds, kb)
        dk_ref[...] += _dot(ds, qb, TN)
        dv_ref[...] += _dot(p.astype(BF16), dob, TN)

    return _call(
        body, "attn_bwd", (4, s // tq),
        [pl.BlockSpec((None, tq, 256), lambda h, i: (h, i, 0)),
         pl.BlockSpec((None, s, 256), lambda h, i: (h, 0, 0)),
         pl.BlockSpec((None, s, LANES), lambda h, i: (h, 0, 0)),
         pl.BlockSpec((tq, LANES), lambda h, i: (i, M_G + h)),
         pl.BlockSpec((tq, LANES), lambda h, i: (i, h)),
         pl.BlockSpec((None, tq, 1), lambda h, i: (h, i, 0)),
         pl.BlockSpec((tq, LANES), lambda h, i: (i, 12 + h))],
        [pl.BlockSpec((None, tq, 256), lambda h, i: (h, i, 0)),
         pl.BlockSpec((None, s, 256), lambda h, i: (h, 0, 0)),
         pl.BlockSpec((None, s, LANES), lambda h, i: (h, 0, 0)),
         pl.BlockSpec((tq, LANES), lambda h, i: (i, h))],
        [jax.ShapeDtypeStruct((4, s, 256), F32), jax.ShapeDtypeStruct((4, s, 256), F32),
         jax.ShapeDtypeStruct((4, s, LANES), F32), jax.ShapeDtypeStruct((s, GROUP_W), BF16)],
        [], ("parallel", "arbitrary"), (q, k, v, z, o, lse, dy), rider)


def _adam(parts, w, m, v, name, tr):
    r, c = w.shape
    tr = min(tr, r)
    c1 = 1.0 - ADAM_B1 ** ADAM_STEP
    c2 = 1.0 - ADAM_B2 ** ADAM_STEP

    def body(p_ref, w_ref, m_ref, v_ref, g_ref, d_ref, nm_ref, nv_ref):
        g = p_ref[0].astype(F32)
        for i in range(1, N_DEV):
            g = g + p_ref[i].astype(F32)
        nm = ADAM_B1 * m_ref[...] + (1.0 - ADAM_B1) * g
        nv = ADAM_B2 * v_ref[...] + (1.0 - ADAM_B2) * (g * g)
        g_ref[...] = g
        nm_ref[...] = nm
        nv_ref[...] = nv
        d_ref[...] = -ADAM_LR * ((nm / c1) / (jnp.sqrt(nv / c2) + ADAM_EPS) + ADAM_WD * w_ref[...])

    blk = lambda: pl.BlockSpec((tr, c), lambda i: (i, 0))
    return pl.pallas_call(
        body, name=name, grid=(r // tr,),
        in_specs=[pl.BlockSpec((N_DEV, tr, c), lambda i: (0, i, 0)), blk(), blk(), blk()],
        out_specs=[blk(), blk(), blk(), blk()],
        out_shape=[jax.ShapeDtypeStruct((r, c), F32)] * 4,
        compiler_params=_params(("parallel",)),
    )(parts, w, m, v)


def _adam_columns(parts, w, m, v, name, tc, rider=None):
    nl, r, c = w.shape
    pieces = [p for layer in parts for p in layer]
    nh = len(parts[0])
    rp = pieces[0].shape[2]
    tc = min(tc, rp)
    ncb = rp // tc
    c1 = 1.0 - ADAM_B1 ** ADAM_STEP
    c2 = 1.0 - ADAM_B2 ** ADAM_STEP

    def body(*refs):
        p_refs, (w_ref, m_ref, v_ref, g_ref, d_ref, nm_ref, nv_ref) = refs[:len(pieces)], refs[len(pieces):]
        for h in range(nh):
            @pl.when(pl.program_id(0) == h)
            def _(h=h):
                for l in range(nl):
                    p_ref = p_refs[l * nh + h]
                    g = p_ref[0].astype(F32)
                    for i in range(1, N_DEV):
                        g = g + p_ref[i].astype(F32)
                    nm = ADAM_B1 * m_ref[:, l, :] + (1.0 - ADAM_B1) * g
                    nv = ADAM_B2 * v_ref[:, l, :] + (1.0 - ADAM_B2) * (g * g)
                    g_ref[:, l, :] = g
                    nm_ref[:, l, :] = nm
                    nv_ref[:, l, :] = nv
                    d_ref[:, l, :] = -ADAM_LR * ((nm / c1) / (jnp.sqrt(nv / c2) + ADAM_EPS) + ADAM_WD * w_ref[:, l, :])

    def part_spec(j):
        return pl.BlockSpec((N_DEV, c, tc), lambda h, i: (0, 0, jnp.clip((h - j % nh) * ncb + i, 0, ncb - 1)))

    blk = lambda: pl.BlockSpec((c, nl, tc), lambda h, i: (0, 0, h * ncb + i))
    t = lambda a: jnp.transpose(a, (2, 0, 1))
    *res, = _call(body, name, (nh, ncb), [part_spec(j) for j in range(len(pieces))] + [blk(), blk(), blk()],
                  [blk(), blk(), blk(), blk()], [jax.ShapeDtypeStruct((c, nl, r), F32)] * 4, [],
                  ("arbitrary",) * 2, (*pieces, t(w), t(m), t(v)), rider)
    return [jnp.transpose(a, (1, 2, 0)) for a in res[:4]] + res[4:]


def _adam_layers(parts, w, m, v, name, tr, rider=None):
    nl, r, c = w.shape
    pieces = [p for layer in parts for p in layer]
    rp = pieces[0].shape[1]
    tr = min(tr, rp)
    nr, nrp = r // tr, rp // tr
    c1 = 1.0 - ADAM_B1 ** ADAM_STEP
    c2 = 1.0 - ADAM_B2 ** ADAM_STEP

    def body(*refs):
        p_refs, (w_ref, m_ref, v_ref, g_ref, d_ref, nm_ref, nv_ref) = refs[:len(pieces)], refs[len(pieces):]
        at = pl.program_id(0) * nr + pl.program_id(1)
        for j in range(len(pieces)):
            @pl.when(jnp.logical_and(at >= j * nrp, at < (j + 1) * nrp))
            def _(p_ref=p_refs[j]):
                g = p_ref[0].astype(F32)
                for i in range(1, N_DEV):
                    g = g + p_ref[i].astype(F32)
                nm = ADAM_B1 * m_ref[...] + (1.0 - ADAM_B1) * g
                nv = ADAM_B2 * v_ref[...] + (1.0 - ADAM_B2) * (g * g)
                g_ref[...] = g
                nm_ref[...] = nm
                nv_ref[...] = nv
                d_ref[...] = -ADAM_LR * ((nm / c1) / (jnp.sqrt(nv / c2) + ADAM_EPS) + ADAM_WD * w_ref[...])

    def part_spec(j):
        return pl.BlockSpec((N_DEV, tr, c), lambda ll, i: (0, jnp.clip(ll * nr + i - j * nrp, 0, nrp - 1), 0))

    blk = lambda: pl.BlockSpec((None, tr, c), lambda ll, i: (ll, i, 0))
    return _call(body, name, (nl, nr), [part_spec(j) for j in range(len(pieces))] + [blk(), blk(), blk()],
                 [blk(), blk(), blk(), blk()], [jax.ShapeDtypeStruct((nl, r, c), F32)] * 4, [],
                 ("arbitrary", "arbitrary"), (*pieces, w, m, v), rider)


REPLICATED = ("norm_g", "ret_norm_g", "gla_ba_f", "gla_ba_b", "gla_norm_g", "pool_w", "pool_scale",
              "mla_q_norm_g", "mla_kv_norm_g", "mla_qk_norm_q", "mla_qk_norm_k")
REPLICATED_EARLY = REPLICATED[1:]
SMALL_SHARDED = ("mla_wq_b", "mla_wkv_b", "gla_wa2_f", "gla_wa2_b")
WEIGHTS = ("norm_g", "w_in", "ret_norm_g", "gla_wa2_f", "gla_ba_f", "gla_wa2_b", "gla_ba_b", "gla_norm_g", "pool_w",
           "pool_scale", "mla_q_norm_g", "mla_wq_b", "mla_kv_norm_g", "mla_wkv_b", "mla_qk_norm_q", "mla_qk_norm_k",
           "w_out")


PACK_ROWS = 16


def _packed_rows(a):
    rows = a.size // LANES
    return rows, -(-rows // PACK_ROWS) * PACK_ROWS


def _pack(arrays, dtype):
    parts = []
    for a in arrays:
        rows, padded = _packed_rows(a)
        parts.append(jnp.pad(a.reshape(rows, LANES).astype(dtype), ((0, padded - rows), (0, 0))))
    return jnp.concatenate(parts, axis=0)


def _unpack(packed, like):
    out, at = [], 0
    for a in like:
        rows, padded = _packed_rows(a)
        out.append(packed[..., at:at + rows, :].reshape(packed.shape[:-2] + a.shape))
        at += padded
    return out


def _columns_by_device(g):
    l, r, n = g.shape
    return g.reshape(l, r, N_DEV, n // N_DEV).transpose(2, 0, 1, 3)


def _gathered_columns(g, l, r, c):
    return g.reshape(N_DEV, l, r, c).transpose(1, 2, 0, 3).reshape(l, r, N_DEV * c)


def _layer_forward(x, wts, late_wts, tables, tm, tq, ride_inproj=None, ride_attn=None, target=None):
    cos_r, sin_r, cos_m, sin_m, tab, _ = tables
    z, h, *carried_in = _inproj(x, wts["norm_g"], wts["w_in"], min(x.shape[0], 2 * tm), rider=ride_inproj)
    wts.update(late_wts("inproj", carried_in))
    o_a, y_a = _ret_fwd(z, cos_r, sin_r, tab, wts["ret_norm_g"])
    o_b, y_b = _gla_fwd(z, wts["wa_f"], wts["wa_b"], wts["gla_ba_f"], wts["gla_ba_b"], wts["gla_norm_g"])
    y_c = _pool_fwd(z, wts["pool_w"], wts["pool_scale"])
    q, k, v = _mla_prep(z, cos_m, sin_m, wts["mla_q_norm_g"], wts["wq"], wts["mla_kv_norm_g"], wts["wkv"],
                        wts["qk_q"], wts["qk_k"], tm)
    o_d, y_d, lse, *carried_attn = _attn_fwd(q, k, v, z, tq, rider=ride_attn)
    wts.update(late_wts("attn", carried_attn))
    y = jnp.concatenate([y_a, y_b, y_c, y_d], axis=1)
    w_out = wts["w_out"]
    if target is None:
        x_next = _mm(y, w_out, "nn", "outproj", tm, D_MODEL, 1024, add=x)
    else:
        x_next = _mm(y, w_out, "nn", "outproj_loss", tm, D_MODEL, 1024, tail=_loss_tail(x, target))
    saved = dict(x=x, z=z, h=h, o_a=o_a, o_b=o_b, o_d=o_d, lse=lse, q=q, k=k, v=v, y=y, w_out=w_out)
    return x_next, saved, carried_in, carried_attn


def _layer_backward(dx, sv, wts, tables, tm, tq, rides):
    cos_r, sin_r, cos_m, sin_m, tab, tab_sw = tables
    z = sv["z"]
    g = {}
    carried = {}

    def rider(name):
        return rides[name](g) if name in rides else None

    def landed(name, results, n_own):
        if name in rides:
            carried[name] = list(results[n_own:])
        return results[:n_own]

    g["w_out"] = _mm(sv["y"], dx, "tn", "d_w_out", 2048, 1024, 1024, out_dtype=BF16)
    dy = _mm(dx, sv["w_out"], "nt", "d_y", tm, 2048, 1024)

    do_a, dg_a, g["ret_norm_g"] = _normgate_bwd(sv["o_a"], z, A_G, dy, 0, wts["ret_norm_g"], tm)
    dq_a, dk_a, dv_a = landed("ret", _ret_bwd(z, do_a, cos_r, sin_r, tab, tab_sw, rider=rider("ret")), 3)

    do_b, dg_b, g["gla_norm_g"] = _normgate_bwd(sv["o_b"], z, B_G, dy, 1, wts["gla_norm_g"], tm)
    dq_b, dk_b, dv_b, d_ga, d_waf, d_wab, g["gla_ba_f"], g["gla_ba_b"] = landed("gla", _gla_bwd(
        z, do_b, wts["wa_f"], wts["wa_b"], wts["gla_ba_f"], wts["gla_ba_b"], rider=rider("gla")), 8)
    g["gla_wa2_f"] = d_waf[0:GLA_RANK]
    g["gla_wa2_b"] = d_wab[GLA_RANK:2 * GLA_RANK]

    du_c, dg_c, g["pool_w"], g["pool_scale"] = _pool_bwd(z, dy, wts["pool_w"], wts["pool_scale"])

    d_q, d_k, d_v, dg_d = landed("attn", _attn_bwd(sv["q"], sv["k"], sv["v"], z, sv["o_d"], sv["lse"], dy, tq,
                                                   rider=rider("attn")), 4)
    (d_mq, d_mkv, d_mkr, d_wq, g["mla_wkv_b"], g["mla_q_norm_g"], g["mla_kv_norm_g"], d_qg, d_kg) = _mla_prep_bwd(
        z, cos_m, sin_m, wts["mla_q_norm_g"], wts["wq"], wts["mla_kv_norm_g"], wts["wkv"], wts["qk_q"], wts["qk_k"],
        d_q, d_k, d_v, tm)
    g["mla_wq_b"] = _unpad_wq(d_wq)
    g["mla_qk_norm_q"] = d_qg[:, _QK_INV]
    g["mla_qk_norm_k"] = d_kg[:, _QK_INV]

    dz = jnp.concatenate([dq_a, dk_a, dv_a, dg_a, dq_b, dk_b, dv_b, dg_b, d_mq, du_c, dg_c, dg_d, d_mkv,
                          d_ga.astype(BF16), d_mkr], axis=1)
    h, half = sv["h"], D_MODEL // 2
    for name, cols in (("d_w_in_a", h[:, :half]), ("d_w_in_b", h[:, half:])):
        res = _mm(dz, cols, "tn", name, 2048, 1024, 1024, out_dtype=BF16, rider=rider(name))
        (d_wt,) = landed(name, res if name in rides else [res], 1)
        g["w_in" + name[-2:]] = _split_w_in(d_wt)
    dx_in, g["norm_g"] = landed("d_h", _mm(dz, wts["w_in"], "nn", "d_h", tm, D_MODEL, 1024, rider=rider("d_h"),
                                           tail=_norm_bwd_tail(sv["x"], wts["norm_g"], dx)), 2)
    return dx_in, g, carried


def kernel(x, norm_g, w_in, ret_norm_g, gla_wa2_f, gla_ba_f, gla_wa2_b, gla_ba_b, gla_norm_g, pool_w, pool_scale, mla_q_norm_g, mla_wq_b, mla_kv_norm_g, mla_wkv_b, mla_qk_norm_q, mla_qk_norm_k, w_out, loss_target, m_norm_g, m_w_in, m_ret_norm_g, m_gla_wa2_f, m_gla_ba_f, m_gla_wa2_b, m_gla_ba_b, m_gla_norm_g, m_pool_w, m_pool_scale, m_mla_q_norm_g, m_mla_wq_b, m_mla_kv_norm_g, m_mla_wkv_b, m_mla_qk_norm_q, m_mla_qk_norm_k, m_w_out, v_norm_g, v_w_in, v_ret_norm_g, v_gla_wa2_f, v_gla_ba_f, v_gla_wa2_b, v_gla_ba_b, v_gla_norm_g, v_pool_w, v_pool_scale, v_mla_q_norm_g, v_mla_wq_b, v_mla_kv_norm_g, v_mla_wkv_b, v_mla_qk_norm_q, v_mla_qk_norm_k, v_w_out):
    w = dict(norm_g=norm_g, w_in=w_in, ret_norm_g=ret_norm_g, gla_wa2_f=gla_wa2_f, gla_ba_f=gla_ba_f,
             gla_wa2_b=gla_wa2_b, gla_ba_b=gla_ba_b, gla_norm_g=gla_norm_g, pool_w=pool_w, pool_scale=pool_scale,
             mla_q_norm_g=mla_q_norm_g, mla_wq_b=mla_wq_b, mla_kv_norm_g=mla_kv_norm_g, mla_wkv_b=mla_wkv_b,
             mla_qk_norm_q=mla_qk_norm_q, mla_qk_norm_k=mla_qk_norm_k, w_out=w_out)
    m = dict(norm_g=m_norm_g, w_in=m_w_in, ret_norm_g=m_ret_norm_g, gla_wa2_f=m_gla_wa2_f, gla_ba_f=m_gla_ba_f,
             gla_wa2_b=m_gla_wa2_b, gla_ba_b=m_gla_ba_b, gla_norm_g=m_gla_norm_g, pool_w=m_pool_w,
             pool_scale=m_pool_scale, mla_q_norm_g=m_mla_q_norm_g, mla_wq_b=m_mla_wq_b, mla_kv_norm_g=m_mla_kv_norm_g,
             mla_wkv_b=m_mla_wkv_b, mla_qk_norm_q=m_mla_qk_norm_q, mla_qk_norm_k=m_mla_qk_norm_k, w_out=m_w_out)
    v = dict(norm_g=v_norm_g, w_in=v_w_in, ret_norm_g=v_ret_norm_g, gla_wa2_f=v_gla_wa2_f, gla_ba_f=v_gla_ba_f,
             gla_wa2_b=v_gla_wa2_b, gla_ba_b=v_gla_ba_b, gla_norm_g=v_gla_norm_g, pool_w=v_pool_w,
             pool_scale=v_pool_scale, mla_q_norm_g=v_mla_q_norm_g, mla_wq_b=v_mla_wq_b, mla_kv_norm_g=v_mla_kv_norm_g,
             mla_wkv_b=v_mla_wkv_b, mla_qk_norm_q=v_mla_qk_norm_q, mla_qk_norm_k=v_mla_qk_norm_k, w_out=v_w_out)
    xs, target = x[0], loss_target[0]
    s = xs.shape[0]
    tm, tq = min(s, 512), min(s, 256)
    c_in = w_in.shape[2]

    w_in_b = jnp.transpose(w_in, (2, 0, 1)).astype(BF16)
    w_out_b = w_out.astype(BF16).reshape(-1, D_MODEL)
    (w_in_g0,) = _exchange([("gather", w_in_b[:, 0])], "gather_first")
    tables = _rope_tables(s) + _ret_tables()

    def early_weights(l, w_in_g):
        return dict(
            norm_g=norm_g[l][None], w_in=_assemble_w_in(w_in_g), ret_norm_g=ret_norm_g[l][None],
            gla_ba_f=gla_ba_f[l][None], gla_ba_b=gla_ba_b[l][None],
            gla_norm_g=gla_norm_g[l][None], pool_w=pool_w[l], pool_scale=pool_scale[l][None],
            mla_q_norm_g=mla_q_norm_g[l][None], mla_kv_norm_g=mla_kv_norm_g[l][None],
            qk_q=_pad_qk_gain(mla_qk_norm_q[l]), qk_k=_pad_qk_gain(mla_qk_norm_k[l]))

    def small_weights(l, small_g):
        shards = _unpack(small_g, [w[n] for n in SMALL_SHARDED])
        full = {n: _gathered_columns(shards[i], *w[n].shape)[l] for i, n in enumerate(SMALL_SHARDED)}
        wa_f = jnp.zeros((LANES, 2 * LANES), BF16).at[0:GLA_RANK].set(full["gla_wa2_f"])
        wa_b = jnp.zeros((LANES, 2 * LANES), BF16).at[GLA_RANK:2 * GLA_RANK].set(full["gla_wa2_b"])
        return dict(wa_f=wa_f, wa_b=wa_b, wq=_pad_wq(full["mla_wq_b"]), wkv=full["mla_wkv_b"])

    def w_out_layer(l, w_out_g):
        return dict(w_out=w_out_g.reshape(N_DEV, DEPTH, -1, D_MODEL)[:, l].reshape(-1, D_MODEL))

    by_owner = lambda g_w_out: g_w_out.reshape(N_DEV, -1, D_MODEL)

    layers = [early_weights(0, w_in_g0), None]
    x1, sv0, (w_in_g1, small_g), (w_out_g,) = _layer_forward(
        xs, layers[0], lambda carrier, got: small_weights(0, got[1]) if carrier == "inproj" else w_out_layer(0, got[0]),
        tables, tm, tq,
        ride_inproj=[("gather", w_in_b[:, 1]), ("gather", _pack([w[n] for n in SMALL_SHARDED], BF16))],
        ride_attn=("gather", w_out_b))
    layers[1] = early_weights(1, w_in_g1)
    (dx, loss_row), sv1, _, _ = _layer_forward(
        x1, layers[1], lambda carrier, got: {**small_weights(1, small_g), **w_out_layer(1, w_out_g)}, tables, tm, tq,
        target=target)
    loss = lax.psum(loss_row[0, 0], ("x", "y", "c"))

    def small_jobs(g):
        grads = (g, g1)
        full = {n: jnp.stack([grads[l][n].reshape(w[n].shape[1:]) if n in REPLICATED else grads[l][n]
                              for l in range(DEPTH)]) for n in SMALL_SHARDED + REPLICATED_EARLY}
        small_c = jax.vmap(lambda *shards: _pack(shards, F32))(*[_columns_by_device(full[n]) for n in SMALL_SHARDED])
        return [("scatter", small_c),
                ("gather", _pack([full[n] for n in REPLICATED_EARLY], F32))]

    dx, g1, got1 = _layer_backward(dx, sv1, layers[1], tables, tm, tq, {
        "attn": lambda g: ("scatter", by_owner(g["w_out"]))})
    dx, g0, got0 = _layer_backward(dx, sv0, layers[0], tables, tm, tq, {
        "gla": lambda g: ("scatter", g1["w_in_b"]),
        "attn": lambda g: [("scatter", g1["w_in_a"]), ("scatter", by_owner(g["w_out"]))],
        "d_w_in_a": small_jobs,
        "d_w_in_b": lambda g: ("scatter", g["w_in_a"]),
        "d_h": lambda g: ("scatter", g["w_in_b"])})
    in_parts = ((got0["d_w_in_b"][0], got0["d_h"][0]), (got0["attn"][0], got0["gla"][0]))
    out_parts = ((got0["attn"][1],), (got1["attn"][0],))
    small_parts, rep_parts = got0["d_w_in_a"]
    norm_pack = _pack([jnp.stack([g0["norm_g"][0], g1["norm_g"][0]])], F32)

    out = {}
    out["w_in"] = _adam_columns(in_parts, w_in, m_w_in, v_w_in, "adam_w_in", 256)
    *out["w_out"], norm_parts = _adam_layers(out_parts, w_out, m_w_out, v_w_out, "adam_w_out", 128,
                                             rider=("gather", norm_pack))
    for names, parts, label in ((SMALL_SHARDED, small_parts, "adam_small"),
                                (REPLICATED_EARLY, rep_parts, "adam_replicated"), (("norm_g",), norm_parts, "adam_norm")):
        res = _adam(parts, _pack([w[n] for n in names], F32), _pack([m[n] for n in names], F32),
                    _pack([v[n] for n in names], F32), label, 2048)
        for n, *vals in zip(names, *[_unpack(a, [w[n] for n in names]) for a in res]):
            out[n] = vals

    return (loss, dx[None], *[out[n][0] for n in WEIGHTS], *[out[n][1] for n in WEIGHTS],
            *[out[n][2] for n in WEIGHTS], *[out[n][3] for n in WEIGHTS])
```

```python
import functools
import math

import numpy as np
import jax
import jax.numpy as jnp
from jax import lax
from jax.experimental import pallas as pl
from jax.experimental.pallas import tpu as pltpu

F32 = jnp.float32
BF16 = jnp.bfloat16

N_DEV = 8
D_MODEL = 2048
DEPTH = 2
GROUP_W = 512
EPS = 1e-6
ROPE_THETA = 10000.0
LANES = 128

RET_HD = 128
RET_CHUNK = 256
RET_UNROLL = 8
GLA_CHUNK = 64
GLA_UNROLL = 16
GLA_CUM_ROWS = 256
GLA_DK = 64
GLA_TAU = 16.0
GLA_RANK = 16
POOL_WINDOWS = (2, 4, 8, 16)
MLA_QK = 192
MLA_ROPE = 64
ATTN_SCALE = MLA_QK ** -0.5
ATTN_Q_SCALE = ATTN_SCALE * math.log2(math.e)
IN_COLS = 5984

ADAM_LR = 0.001
ADAM_B1 = 0.9
ADAM_B2 = 0.999
ADAM_EPS = 1e-08
ADAM_WD = 0.01
ADAM_STEP = 10

A_Q, A_K, A_V, A_G = 0, 4, 8, 12
B_Q, B_K, B_V, B_G = 16, 18, 20, 24
M_Q, C_V, C_G, M_G = 28, 32, 36, 40
M_KV, GA, M_KR = 44, 46, 47
ZP_COLS = 48 * LANES

VMEM_LIMIT = 56 * 1024 * 1024


def _params(sem, vmem=VMEM_LIMIT):
    return pltpu.CompilerParams(dimension_semantics=sem, vmem_limit_bytes=vmem)


def _sigmoid(x):
    return 1.0 / (1.0 + jnp.exp(-x))


def _silu(x):
    return x * _sigmoid(x)


def _silu_grad(x):
    s = _sigmoid(x)
    return s * (1.0 + x * (1.0 - s))


def _dot(a, b, dims=(((1,), (0,)), ((), ()))):
    return lax.dot_general(a, b, dims, preferred_element_type=F32)


NT = (((1,), (1,)), ((), ()))
TN = (((0,), (0,)), ((), ()))


def _chunk_loop(n, body, init, unroll):
    unroll = math.gcd(n, unroll)

    def trip(t, carry):
        for u in range(unroll):
            carry = body(t * unroll + u, carry)
        return carry

    return lax.fori_loop(0, n // unroll, trip, init)


def _roll_lanes_half(x):
    return pltpu.roll(x, 64, 1)


def _wq_perm():
    idx = np.zeros((1024,), np.int32)
    ok = np.zeros((1024,), bool)
    for h in range(4):
        idx[128 * h:128 * h + 128] = 192 * h + np.arange(128)
        ok[128 * h:128 * h + 128] = True
        base = 512 + 128 * h
        idx[base:base + 32] = 192 * h + 128 + np.arange(32)
        ok[base:base + 32] = True
        idx[base + 64:base + 96] = 192 * h + 160 + np.arange(32)
        ok[base + 64:base + 96] = True
    inv = np.zeros((768,), np.int32)
    inv[idx[ok]] = np.nonzero(ok)[0]
    return idx, ok, inv


_WQ_IDX, _WQ_OK, _WQ_INV = _wq_perm()


def _pad_wq(wq):
    return jnp.where(jnp.asarray(_WQ_OK)[None, :], wq[:, _WQ_IDX], 0).astype(wq.dtype)


def _unpad_wq(wqp):
    return wqp[:, _WQ_INV]


def _qk_idx():
    idx = np.zeros((256,), np.int32)
    ok = np.zeros((256,), bool)
    idx[0:128] = np.arange(128)
    ok[0:128] = True
    idx[128:160] = 128 + np.arange(32)
    ok[128:160] = True
    idx[192:224] = 160 + np.arange(32)
    ok[192:224] = True
    inv = np.zeros((192,), np.int32)
    inv[idx[ok]] = np.nonzero(ok)[0]
    return idx, ok, inv


_QK_IDX, _QK_OK, _QK_INV = _qk_idx()


def _pad_qk_gain(g):
    return jnp.where(jnp.asarray(_QK_OK), g[_QK_IDX], 0.0).reshape(1, 256)


def _rope_tables(s):
    def tabs(dim):
        inv = 1.0 / (ROPE_THETA ** (jnp.arange(0, dim, 2, dtype=F32) / dim))
        ang = jnp.arange(s, dtype=F32)[:, None] * inv[None, :]
        return jnp.cos(ang), jnp.sin(ang)
    cr, sr = tabs(RET_HD)
    cos_r = jnp.concatenate([cr, cr], axis=1)
    sin_r = jnp.concatenate([-sr, sr], axis=1)
    cm, sm = tabs(MLA_ROPE)
    zz = jnp.zeros_like(cm)
    cos_m = jnp.concatenate([cm, zz, cm, zz], axis=1)
    sin_m = jnp.concatenate([-sm, zz, sm, zz], axis=1)
    return cos_r, sin_r, cos_m, sin_m


def _rope(x, cos, sin):
    return x * cos + _roll_lanes_half(x) * sin


def _rope_t(x, cos, sin):
    return x * cos + _roll_lanes_half(x * sin)


def _ret_tables():
    c = RET_CHUNK
    gamma_f = 1.0 - 2.0 ** (-5.0 - jnp.arange(4, dtype=F32))
    gamma_b = gamma_f[::-1]
    idx = jnp.arange(c, dtype=F32)
    diff = idx[:, None] - idx[None, :]

    def build(g1, g2):
        l1 = jnp.log(g1)[:, None, None]
        l2 = jnp.log(g2)[:, None, None]
        d1 = jnp.where(diff >= 0, jnp.exp(jnp.maximum(diff, 0.0)[None] * l1), 0.0)
        d2 = jnp.where(diff <= 0, jnp.exp(jnp.maximum(-diff, 0.0)[None] * l2), 0.0)
        ones = jnp.ones((1, c, LANES), F32)
        col = idx[None, :, None]
        qdf = jnp.exp((col + 1.0) * l1) * ones
        kdf = jnp.exp((c - 1.0 - col) * l1) * ones
        qdb = jnp.exp((c - col) * l2) * ones
        kdb = jnp.exp(col * l2) * ones
        cd1 = jnp.exp(c * l1) * ones
        cd2 = jnp.exp(c * l2) * ones
        return jnp.concatenate([d1 + d2, qdf, kdf, qdb, kdb, cd1, cd2], axis=2)

    return build(gamma_f, gamma_b), build(gamma_b, gamma_f)


MESH = pl.DeviceIdType.MESH
ANY = pl.BlockSpec(memory_space=pl.ANY)
_RELATIONS = ((0, 0, 1), (1, 0, 0), (0, 1, 0), (1, 1, 0), (1, 0, 1), (0, 1, 1), (1, 1, 1))


def _position():
    return lax.axis_index("x"), lax.axis_index("y"), lax.axis_index("c")


def _gather_copies(x_ref, out_ref, send_sems, recv_sems, local_sem, starting):
    x, y, cc = _position()
    me, sibling = (x, y, cc), (x, y, 1 - cc)
    chips = [(1 - x, y), (x, 1 - y), (1 - x, 1 - y)]

    def slab(px, py, pc):
        return out_ref.at[4 * px + 2 * py + pc]

    def copy(k, block, to, src=None):
        return pltpu.make_async_remote_copy(
            src_ref=slab(*block) if src is None else src, dst_ref=slab(*block),
            send_sem=send_sems.at[k], recv_sem=recv_sems.at[k], device_id=to, device_id_type=MESH)

    mine = pltpu.make_async_copy(x_ref, slab(*me), local_sem)
    first = [copy(0, me, sibling, src=x_ref)] + [copy(1 + j, me, (*chip, cc), src=x_ref) for j, chip in enumerate(chips)]
    if starting:
        return mine, first
    passed = [copy(4 + j, (*chip, cc), sibling) for j, chip in enumerate(chips)]
    arrivals = [copy(1 + j, (*chip, cc), me) for j, chip in enumerate(chips)]
    late = [copy(0, sibling, me)] + [copy(4 + j, (*chip, 1 - cc), me) for j, chip in enumerate(chips)]
    return mine, first, passed, arrivals, late


def _gather_start(*refs):
    mine, first = _gather_copies(*refs, starting=True)
    mine.start()
    for cp in first:
        cp.start()


def _gather_finish(*refs):
    mine, first, passed, arrivals, late = _gather_copies(*refs, starting=False)
    for arrived, onward in zip(arrivals, passed):
        arrived.wait_recv()
        onward.start()
    for cp in late:
        cp.wait_recv()
    for cp in first + passed:
        cp.wait_send()
    mine.wait()


def _scatter_copies(c_ref, out_ref, send_sems, recv_sems, local_sem):
    x, y, cc = _position()
    me = 4 * x + 2 * y + cc
    mine = pltpu.make_async_copy(c_ref.at[me], out_ref.at[me], local_sem)
    copies = []
    for k, (fx, fy, fc) in enumerate(_RELATIONS):
        px = 1 - x if fx else x
        py = 1 - y if fy else y
        pc = 1 - cc if fc else cc
        copies.append(pltpu.make_async_remote_copy(
            src_ref=c_ref.at[4 * px + 2 * py + pc], dst_ref=out_ref.at[me],
            send_sem=send_sems.at[k], recv_sem=recv_sems.at[k], device_id=(px, py, pc), device_id_type=MESH))
    return mine, copies


def _scatter_start(*refs):
    mine, copies = _scatter_copies(*refs)
    mine.start()
    for cp in copies:
        cp.start()


def _scatter_finish(*refs):
    mine, copies = _scatter_copies(*refs)
    for cp in copies:
        cp.wait()
    mine.wait()


_EXCHANGES = {"gather": (_gather_start, _gather_finish), "scatter": (_scatter_start, _scatter_finish)}


def _exchange_scratch():
    return [pltpu.SemaphoreType.DMA((7,)), pltpu.SemaphoreType.DMA((7,)), pltpu.SemaphoreType.DMA]


def _exchange_out(kind, src):
    return jax.ShapeDtypeStruct(((N_DEV,) + src.shape) if kind == "gather" else src.shape, src.dtype)


def _exchange(jobs, name):
    n = len(jobs)

    def body(*refs):
        srcs, outs, sems = refs[:n], refs[n:2 * n], refs[2 * n:]
        for half in (0, 1):
            for i, (kind, _) in enumerate(jobs):
                _EXCHANGES[kind][half](srcs[i], outs[i], *sems[3 * i:3 * i + 3])

    return pl.pallas_call(
        body, name=name, out_shape=[_exchange_out(kind, src) for kind, src in jobs],
        in_specs=[ANY] * n, out_specs=[ANY] * n,
        scratch_shapes=[sem for _ in jobs for sem in _exchange_scratch()])(*[src for _, src in jobs])


def _call(body, name, grid, in_specs, out_specs, out_shape, scratch, sem, args, rider=None):
    if rider is None:
        return pl.pallas_call(body, name=name, grid=grid, in_specs=in_specs, out_specs=out_specs, out_shape=out_shape,
                              scratch_shapes=scratch, compiler_params=_params(sem))(*args)
    jobs = rider if isinstance(rider, list) else [rider]
    ni, no, ns, nj = len(in_specs), len(out_specs), len(scratch), len(jobs)

    def carried(*refs):
        ins, rsrcs = refs[:ni], refs[ni:ni + nj]
        outs, routs = refs[ni + nj:ni + nj + no], refs[ni + nj + no:ni + 2 * nj + no]
        scr, sems = refs[ni + 2 * nj + no:ni + 2 * nj + no + ns], refs[ni + 2 * nj + no + ns:]
        ids = [pl.program_id(a) for a in range(len(grid))]
        is_first = functools.reduce(jnp.logical_and, [i == 0 for i in ids])
        is_last = functools.reduce(jnp.logical_and, [i == g - 1 for i, g in zip(ids, grid)])

        def half(which):
            for j, (kind, _) in enumerate(jobs):
                _EXCHANGES[kind][which](rsrcs[j], routs[j], *sems[3 * j:3 * j + 3])

        @pl.when(is_first)
        def _():
            half(0)

        body(*ins, *outs, *scr)

        @pl.when(is_last)
        def _():
            half(1)

    return pl.pallas_call(
        carried, name=name, grid=grid, in_specs=list(in_specs) + [ANY] * nj, out_specs=list(out_specs) + [ANY] * nj,
        out_shape=list(out_shape) + [_exchange_out(kind, src) for kind, src in jobs],
        scratch_shapes=list(scratch) + [sem for _ in jobs for sem in _exchange_scratch()],
        compiler_params=_params(("arbitrary",) * len(grid)))(*args, *[src for _, src in jobs])


def _inproj(x, g, wt, tm, tn=512, rider=None):
    s, d = x.shape
    n = wt.shape[0]

    def body(x_ref, g_ref, w_ref, z_ref, h_ref, hs):
        @pl.when(pl.program_id(1) == 0)
        def _():
            xv = x_ref[...]
            r = lax.rsqrt(jnp.mean(xv * xv, axis=-1, keepdims=True) + EPS)
            hv = (xv * r * g_ref[...]).astype(BF16)
            hs[...] = hv
            h_ref[...] = hv
        z_ref[...] = _dot(hs[...], w_ref[...], NT)

    return _call(
        body, "inproj", (s // tm, n // tn),
        [pl.BlockSpec((tm, d), lambda i, j: (i, 0)),
         pl.BlockSpec((1, d), lambda i, j: (0, 0)),
         pl.BlockSpec((tn, d), lambda i, j: (j, 0))],
        [pl.BlockSpec((tm, tn), lambda i, j: (i, j)), pl.BlockSpec((tm, d), lambda i, j: (i, 0))],
        [jax.ShapeDtypeStruct((s, n), F32), jax.ShapeDtypeStruct((s, d), BF16)],
        [pltpu.VMEM((tm, d), BF16)], ("parallel", "arbitrary"), (x, g, wt), rider)


def _relayout_plan():
    runs = ((0, 3584, 0), (3584, 3616, GA * LANES), (3616, 4640, C_V * LANES), (4640, 5152, M_Q * LANES),
            (5152, 5408, M_KV * LANES), (5408, 5440, M_KR * LANES), (5440, 5472, M_KR * LANES + 64),
            (5472, 5984, M_G * LANES))
    shard = IN_COLS // N_DEV
    plan = []
    for d in range(N_DEV):
        lo, hi = shard * d, shard * (d + 1)
        for a, b, p in runs:
            s, e = max(a, lo), min(b, hi)
            if s < e:
                plan.append((d, s - lo, p + (s - a), e - s))
    return plan


def _assemble_w_in(g, tc=512):
    _, c, r = g.shape
    tc = min(tc, r)

    def body(g_ref, o_ref):
        o_ref[...] = jnp.zeros_like(o_ref)
        for d, at, to, w in _relayout_plan():
            o_ref[to:to + w, :] = g_ref[d, at:at + w, :]

    return pl.pallas_call(
        body, name="assemble_w_in", grid=(r // tc,),
        in_specs=[pl.BlockSpec((N_DEV, c, tc), lambda i: (0, 0, i))],
        out_specs=pl.BlockSpec((ZP_COLS, tc), lambda i: (0, i)),
        out_shape=jax.ShapeDtypeStruct((ZP_COLS, r), g.dtype),
        compiler_params=_params(("parallel",)),
    )(g)


def _split_w_in(wt, tc=512):
    r = wt.shape[1]
    c = IN_COLS // N_DEV
    tc = min(tc, r)

    def body(w_ref, o_ref):
        for d, at, to, w in _relayout_plan():
            o_ref[d, at:at + w, :] = w_ref[to:to + w, :]

    return pl.pallas_call(
        body, name="split_w_in", grid=(r // tc,),
        in_specs=[pl.BlockSpec((ZP_COLS, tc), lambda i: (0, i))],
        out_specs=pl.BlockSpec((N_DEV, c, tc), lambda i: (0, 0, i)),
        out_shape=jax.ShapeDtypeStruct((N_DEV, c, r), wt.dtype),
        compiler_params=_params(("parallel",)),
    )(wt)


def _mm(a, b, mode, name, tm, tn, tk, add=None, out_dtype=F32, rider=None, tail=None):
    if mode == "tn":
        k, m = a.shape
    else:
        m, k = a.shape
    n = b.shape[0] if mode == "nt" else b.shape[1]
    tm, tn, tk = min(tm, m), min(tn, n), min(tk, k)
    nk = k // tk
    dims = {"nn": (((1,), (0,)), ((), ())), "nt": NT, "tn": TN}[mode]
    if tail is None:
        def plain(acc, i, extra_refs, out_refs):
            out_refs[0][...] = (acc + extra_refs[0][...] if extra_refs else acc).astype(out_dtype)
        tail = ([(add, "tile")] if add is not None else [], [(out_dtype, "tile")], plain)
    extra, outs, fn = tail
    spec = {"tile": pl.BlockSpec((tm, tn), lambda i, j, kk: (i, j)),
            "row": pl.BlockSpec((1, tn), lambda i, j, kk: (0, j)),
            "lanes": pl.BlockSpec((1, LANES), lambda i, j, kk: (0, 0))}
    shape = {"tile": (m, n), "row": (1, n), "lanes": (1, LANES)}
    ne, no = len(extra), len(outs)

    def body(*refs):
        a_ref, b_ref = refs[:2]
        extra_refs, out_refs, acc = refs[2:2 + ne], refs[2 + ne:2 + ne + no], refs[2 + ne + no]
        i, kk = pl.program_id(0), pl.program_id(2)

        @pl.when(kk == 0)
        def _():
            acc[...] = jnp.zeros_like(acc)

        acc[...] += _dot(a_ref[...].astype(BF16), b_ref[...].astype(BF16), dims)

        @pl.when(kk == nk - 1)
        def _():
            fn(acc[...], i, extra_refs, out_refs)

    a_spec = (pl.BlockSpec((tk, tm), lambda i, j, kk: (kk, i)) if mode == "tn"
              else pl.BlockSpec((tm, tk), lambda i, j, kk: (i, kk)))
    b_spec = (pl.BlockSpec((tn, tk), lambda i, j, kk: (j, kk)) if mode == "nt"
              else pl.BlockSpec((tk, tn), lambda i, j, kk: (kk, j)))
    summed = any(kind != "tile" for _, kind in outs)
    res = _call(body, name, (m // tm, n // tn, nk), [a_spec, b_spec] + [spec[kind] for _, kind in extra],
                [spec[kind] for _, kind in outs], [jax.ShapeDtypeStruct(shape[kind], dt) for dt, kind in outs],
                [pltpu.VMEM((tm, tn), F32)], ("arbitrary",) * 3 if summed else ("parallel", "parallel", "arbitrary"),
                [a, b] + [arr for arr, _ in extra], rider)
    return res[0] if (rider is None and no == 1) else res


def _norm_bwd_tail(x, g, dres):
    def fn(dh, i, extra_refs, out_refs):
        x_ref, g_ref, dres_ref = extra_refs
        dx_ref, dg_ref = out_refs

        @pl.when(i == 0)
        def _():
            dg_ref[...] = jnp.zeros_like(dg_ref)

        xv = x_ref[...]
        r = lax.rsqrt(jnp.mean(xv * xv, axis=-1, keepdims=True) + EPS)
        nv = xv * r
        dg_ref[...] += jnp.sum(dh * nv, axis=0, keepdims=True)
        u = dh * g_ref[...]
        dx_ref[...] = dres_ref[...] + r * (u - nv * jnp.mean(u * nv, axis=-1, keepdims=True))

    return [(x, "tile"), (g, "row"), (dres, "tile")], [(F32, "tile"), (F32, "row")], fn


def _loss_tail(x, target):
    d = x.shape[1]

    def fn(acc, i, extra_refs, out_refs):
        x_ref, t_ref = extra_refs
        dx_ref, loss_ref = out_refs

        @pl.when(i == 0)
        def _():
            loss_ref[...] = jnp.zeros_like(loss_ref)

        err = acc + x_ref[...] - t_ref[...]
        dx_ref[...] = err * (1.0 / d)
        per_tok = jnp.mean(err * err, axis=-1, keepdims=True)
        loss_ref[...] += 0.5 * jnp.sum(per_tok, axis=0, keepdims=True)

    return [(x, "tile"), (target, "tile")], [(F32, "tile"), (F32, "lanes")], fn


def _ret_core(q_ref, k_ref, v_ref, tab_ref, out_ref, back_ref, nchunk):
    c = RET_CHUNK

    def rows(n):
        return pl.ds(pl.multiple_of(n * c, c), c)

    zero = jnp.zeros((LANES, LANES), F32)

    def plane(i, n=c):
        return tab_ref[0:n, c + LANES * i:c + LANES * (i + 1)]

    def fwd(n, st):
        r = rows(n)
        q, k, vb = q_ref[r, :], k_ref[r, :], v_ref[r, :].astype(BF16)
        sc = _dot(q.astype(BF16), k.astype(BF16), NT) * tab_ref[:, 0:c]
        o = _dot(sc.astype(BF16), vb)
        o = o + _dot((q * plane(0)).astype(BF16), st.astype(BF16))
        out_ref[r, :] = o
        return st * plane(4, LANES) + _dot((k * plane(1)).astype(BF16), vb, TN)

    def bwd(i, st):
        r = rows(nchunk - 1 - i)
        q, k, vb = q_ref[r, :], k_ref[r, :], v_ref[r, :].astype(BF16)
        back_ref[r, :] = _dot((q * plane(2)).astype(BF16), st.astype(BF16))
        return st * plane(5, LANES) + _dot((k * plane(3)).astype(BF16), vb, TN)

    def both(i, states):
        return fwd(i, states[0]), bwd(i, states[1])

    _chunk_loop(nchunk, both, (zero, zero), RET_UNROLL)
    out_ref[...] += back_ref[...]


def _ret_fwd(z, cos_r, sin_r, tab, norm_g):
    s = z.shape[0]
    nchunk = s // RET_CHUNK
    scale = RET_HD ** -0.5
    col = lambda base: pl.BlockSpec((s, LANES), lambda h: (0, base + h), pipeline_mode=pl.Buffered(1))

    def body(q_ref, k_ref, v_ref, g_ref, cos_ref, sin_ref, tab_ref, ng_ref, o_ref, y_ref, qh, kh, back):
        qh[...] = _rope(q_ref[...], cos_ref[...], sin_ref[...])
        kh[...] = _rope(k_ref[...], cos_ref[...], sin_ref[...]) * scale
        _ret_core(qh, kh, v_ref, tab_ref, o_ref, back, nchunk)
        o = o_ref[...]
        r = lax.rsqrt(jnp.mean(o * o, axis=-1, keepdims=True) + EPS)
        y_ref[...] = (_silu(g_ref[...]) * (o * r * ng_ref[...])).astype(BF16)

    return pl.pallas_call(
        body, name="ret_fwd", grid=(4,),
        in_specs=[col(A_Q), col(A_K), col(A_V), col(A_G),
                  pl.BlockSpec((s, LANES), lambda h: (0, 0), pipeline_mode=pl.Buffered(1)),
                  pl.BlockSpec((s, LANES), lambda h: (0, 0), pipeline_mode=pl.Buffered(1)),
                  pl.BlockSpec((None, RET_CHUNK, RET_CHUNK + 6 * LANES), lambda h: (h, 0, 0)),
                  pl.BlockSpec((1, LANES), lambda h: (0, h))],
        out_specs=[pl.BlockSpec((s, LANES), lambda h: (0, h)), pl.BlockSpec((s, LANES), lambda h: (0, h))],
        out_shape=[jax.ShapeDtypeStruct((s, GROUP_W), F32), jax.ShapeDtypeStruct((s, GROUP_W), BF16)],
        scratch_shapes=[pltpu.VMEM((s, LANES), F32)] * 3,
        compiler_params=_params(("arbitrary",)),
    )(z, z, z, z, cos_r, sin_r, tab, norm_g)


def _ret_bwd(z, d_o, cos_r, sin_r, tab, tab_sw, rider=None):
    s = z.shape[0]
    nchunk = s // RET_CHUNK
    scale = RET_HD ** -0.5
    col = lambda base: pl.BlockSpec((s, LANES), lambda h: (0, base + h), pipeline_mode=pl.Buffered(1))
    whole = lambda: pl.BlockSpec((s, LANES), lambda h: (0, 0), pipeline_mode=pl.Buffered(1))
    tabspec = lambda: pl.BlockSpec((None, RET_CHUNK, RET_CHUNK + 6 * LANES), lambda h: (h, 0, 0))
    outspec = lambda: pl.BlockSpec((s, LANES), lambda h: (0, h))

    def body(q_ref, k_ref, v_ref, do_ref, cos_ref, sin_ref, tab_ref, tsw_ref, dq_ref, dk_ref, dv_ref,
             qh, kh, tmp, back):
        cos, sin = cos_ref[...], sin_ref[...]
        qh[...] = _rope(q_ref[...], cos, sin)
        kh[...] = _rope(k_ref[...], cos, sin) * scale
        _ret_core(kh, qh, do_ref, tsw_ref, tmp, back, nchunk)
        dv_ref[...] = tmp[...].astype(BF16)
        _ret_core(do_ref, v_ref, kh, tab_ref, tmp, back, nchunk)
        dq_ref[...] = _rope_t(tmp[...], cos, sin).astype(BF16)
        _ret_core(v_ref, do_ref, qh, tsw_ref, tmp, back, nchunk)
        dk_ref[...] = _rope_t(tmp[...] * scale, cos, sin).astype(BF16)

    return _call(
        body, "ret_bwd", (4,),
        [col(A_Q), col(A_K), col(A_V),
         pl.BlockSpec((s, LANES), lambda h: (0, h), pipeline_mode=pl.Buffered(1)),
         whole(), whole(), tabspec(), tabspec()],
        [outspec(), outspec(), outspec()],
        [jax.ShapeDtypeStruct((s, GROUP_W), BF16)] * 3,
        [pltpu.VMEM((s, LANES), F32)] * 4,
        ("arbitrary",), (z, z, z, d_o, cos_r, sin_r, tab, tab_sw), rider)


def _normgate_bwd(o, z, gate_blk, dy, dy_blk, norm_g, tm):
    s = o.shape[0]

    def body(o_ref, g_ref, dy_ref, ng_ref, do_ref, dg_ref, dng_ref):
        @pl.when(pl.program_id(0) == 0)
        def _():
            dng_ref[...] = jnp.zeros_like(dng_ref)

        for h in range(4):
            sl = slice(LANES * h, LANES * (h + 1))
            ov, gv, dyv, ng = o_ref[:, sl], g_ref[:, sl], dy_ref[:, sl], ng_ref[:, sl]
            r = lax.rsqrt(jnp.mean(ov * ov, axis=-1, keepdims=True) + EPS)
            on = ov * r
            dn = dyv * _silu(gv)
            u = dn * ng
            do_ref[:, sl] = r * (u - on * jnp.mean(u * on, axis=-1, keepdims=True))
            dg_ref[:, sl] = (dyv * (on * ng) * _silu_grad(gv)).astype(BF16)
            dng_ref[:, sl] += jnp.sum(dn * on, axis=0, keepdims=True)

    return pl.pallas_call(
        body, name="normgate_bwd", grid=(s // tm,),
        in_specs=[pl.BlockSpec((tm, GROUP_W), lambda i: (i, 0)),
                  pl.BlockSpec((tm, GROUP_W), lambda i: (i, gate_blk // 4)),
                  pl.BlockSpec((tm, GROUP_W), lambda i: (i, dy_blk)),
                  pl.BlockSpec((1, GROUP_W), lambda i: (0, 0))],
        out_specs=[pl.BlockSpec((tm, GROUP_W), lambda i: (i, 0)), pl.BlockSpec((tm, GROUP_W), lambda i: (i, 0)),
                   pl.BlockSpec((1, GROUP_W), lambda i: (0, 0))],
        out_shape=[jax.ShapeDtypeStruct((s, GROUP_W), F32), jax.ShapeDtypeStruct((s, GROUP_W), BF16),
                   jax.ShapeDtypeStruct((1, GROUP_W), F32)],
        compiler_params=_params(("arbitrary",)),
    )(o, z, dy, norm_g)


def _log_sigmoid(x):
    return jnp.minimum(x, 0.0) - jnp.log(1.0 + jnp.exp(-jnp.abs(x)))


def _gla_consts():
    c = GLA_CHUNK
    row = lax.broadcasted_iota(jnp.int32, (c, c), 0)
    colm = lax.broadcasted_iota(jnp.int32, (c, c), 1)
    lane = lax.broadcasted_iota(jnp.int32, (1, LANES), 1)
    low = row >= colm
    up = colm >= row
    heads = ((lane < GLA_DK).astype(F32), (lane >= GLA_DK).astype(F32))
    return low, up, heads


def _chunk_running_sum(x, suffix):
    rows = x.shape[0]
    pos = jnp.bitwise_and(lax.broadcasted_iota(jnp.int32, (rows, 1), 0), GLA_CHUNK - 1)
    k = 1
    while k < GLA_CHUNK:
        if suffix:
            x = x + jnp.where(pos < GLA_CHUNK - k, pltpu.roll(x, rows - k, 0), 0.0)
        else:
            x = x + jnp.where(pos >= k, pltpu.roll(x, k, 0), 0.0)
        k *= 2
    return x


def _gla_chunk(cum_ref, d, n):
    c = GLA_CHUNK
    cum = cum_ref[d, pl.ds(pl.multiple_of(n * c, c), c), :]
    last = cum_ref[d, pl.ds(n * c + (c - 1 if d == 0 else 0), 1), :]
    eq = jnp.exp(cum)
    ek = jnp.exp(-cum)
    el = jnp.exp(last - cum)
    dec = jnp.exp(last)
    return eq, ek, el, dec


def _gla_gates(ga_ref, wa_ref, ba_ref, cum_ref, s, suffix):
    rows = min(s, GLA_CUM_ROWS)

    def step(i, carry):
        r = pl.ds(pl.multiple_of(i * rows, rows), rows)
        pre = _dot(ga_ref[r, :].astype(BF16), wa_ref[...].astype(BF16)) + ba_ref[...]
        cum_ref[r, :] = _chunk_running_sum(_log_sigmoid(pre) * (1.0 / GLA_TAU), suffix)
        return carry
    lax.fori_loop(0, s // rows, step, 0)


def _gla_fwd(z, wa_f, wa_b, ba_f, ba_b, norm_g):
    s = z.shape[0]
    c = GLA_CHUNK
    nchunk = s // c
    scale = GLA_DK ** -0.5
    tm = min(s, 512)
    one = pl.Buffered(1)

    def body(q_ref, k_ref, v_ref, ga_ref, g_ref, waf_ref, wab_ref, baf_ref, bab_ref, ng_ref, o_ref, y_ref,
             la_s):
        low, up, heads = _gla_consts()
        _gla_gates(ga_ref, waf_ref, baf_ref, la_s.at[0], s, False)
        _gla_gates(ga_ref, wab_ref, bab_ref, la_s.at[1], s, True)
        for d in range(2):
            tri = (low, up)[d]

            def step(i, states):
                n = i if d == 0 else nchunk - 1 - i
                r = pl.ds(pl.multiple_of(n * c, c), c)
                q = q_ref[r, :] * scale
                k = k_ref[r, :]
                eq, ek, el, dec = _gla_chunk(la_s, d, n)
                qt = q * eq
                ktb = (k * ek).astype(BF16)
                kl = k * el
                new_states = []
                for hh in range(2):
                    cols = slice(LANES * hh, LANES * (hh + 1))
                    vb = v_ref[r, cols].astype(BF16)
                    qm = (qt * heads[hh]).astype(BF16)
                    a = jnp.where(tri, _dot(qm, ktb, NT), 0.0)
                    o = _dot(a.astype(BF16), vb) + _dot(qm, states[hh].astype(BF16), NT)
                    if d == 0:
                        o_ref[r, cols] = o
                    else:
                        o_ref[r, cols] += o
                    new_states.append(states[hh] * dec + _dot(vb, (kl * heads[hh]).astype(BF16), TN))
                return tuple(new_states)

            zero = jnp.zeros((LANES, LANES), F32)
            _chunk_loop(nchunk, step, (zero, zero), GLA_UNROLL)

        def epi(i, carry):
            r = pl.ds(pl.multiple_of(i * tm, tm), tm)
            for hh in range(2):
                cols = slice(LANES * hh, LANES * (hh + 1))
                o = o_ref[r, cols]
                rr = lax.rsqrt(jnp.mean(o * o, axis=-1, keepdims=True) + EPS)
                y_ref[r, cols] = (_silu(g_ref[r, cols]) * (o * rr * ng_ref[:, cols])).astype(BF16)
            return carry

        lax.fori_loop(0, s // tm, epi, 0)

    w2 = 2 * LANES
    return pl.pallas_call(
        body, name="gla_fwd", grid=(2,),
        in_specs=[pl.BlockSpec((s, LANES), lambda p: (0, B_Q + p), pipeline_mode=one),
                  pl.BlockSpec((s, LANES), lambda p: (0, B_K + p), pipeline_mode=one),
                  pl.BlockSpec((s, w2), lambda p: (0, B_V // 2 + p), pipeline_mode=one),
                  pl.BlockSpec((s, LANES), lambda p: (0, GA), pipeline_mode=one),
                  pl.BlockSpec((s, w2), lambda p: (0, B_G // 2 + p), pipeline_mode=one),
                  pl.BlockSpec((LANES, LANES), lambda p: (0, p)),
                  pl.BlockSpec((LANES, LANES), lambda p: (0, p)),
                  pl.BlockSpec((1, LANES), lambda p: (0, p)),
                  pl.BlockSpec((1, LANES), lambda p: (0, p)),
                  pl.BlockSpec((1, w2), lambda p: (0, p))],
        out_specs=[pl.BlockSpec((s, w2), lambda p: (0, p)), pl.BlockSpec((s, w2), lambda p: (0, p))],
        out_shape=[jax.ShapeDtypeStruct((s, GROUP_W), F32), jax.ShapeDtypeStruct((s, GROUP_W), BF16)],
        scratch_shapes=[pltpu.VMEM((2, s, LANES), F32)],
        compiler_params=_params(("arbitrary",)),
    )(z, z, z, z, z, wa_f, wa_b, ba_f, ba_b, norm_g)


def _gla_bwd(z, d_o, wa_f, wa_b, ba_f, ba_b, rider=None):
    s = z.shape[0]
    c = GLA_CHUNK
    nchunk = s // c
    scale = GLA_DK ** -0.5
    tm = min(s, GLA_CUM_ROWS)
    one = pl.Buffered(1)

    def body(q_ref, k_ref, v_ref, ga_ref, do_ref, waf_ref, wab_ref, baf_ref, bab_ref,
             dq_ref, dk_ref, dv_ref, dga_ref, dwaf_ref, dwab_ref, dbaf_ref, dbab_ref,
             la_s, dla_s, stash, dq_s, dk_s, dv_s):
        low, up, heads = _gla_consts()
        rowi = lax.broadcasted_iota(jnp.int32, (c, 1), 0)
        _gla_gates(ga_ref, waf_ref, baf_ref, la_s.at[0], s, False)
        _gla_gates(ga_ref, wab_ref, bab_ref, la_s.at[1], s, True)
        for d in range(2):
            tri = (low, up)[d]
            last_row = (rowi == (c - 1 if d == 0 else 0)).astype(F32)
            order = (lambda i: i) if d == 0 else (lambda i: nchunk - 1 - i)
            zero = jnp.zeros((LANES, LANES), F32)

            def states(i, sts):
                n = order(i)
                r = pl.ds(pl.multiple_of(n * c, c), c)
                k = k_ref[r, :]
                _, _, el, dec = _gla_chunk(la_s, d, n)
                kl = k * el
                new = []
                for hh in range(2):
                    cols = slice(LANES * hh, LANES * (hh + 1))
                    stash[hh, n] = sts[hh]
                    new.append(sts[hh] * dec + _dot(v_ref[r, cols].astype(BF16), (kl * heads[hh]).astype(BF16), TN))
                return tuple(new)

            _chunk_loop(nchunk, states, (zero, zero), GLA_UNROLL)

            def step(i, dsts):
                n = order(nchunk - 1 - i)
                r = pl.ds(pl.multiple_of(n * c, c), c)
                q = q_ref[r, :] * scale
                k = k_ref[r, :]
                eq, ek, el, dec = _gla_chunk(la_s, d, n)
                qt = q * eq
                kt = k * ek
                kl = k * el
                ktb = kt.astype(BF16)
                dqt = jnp.zeros((c, LANES), F32)
                dkt = jnp.zeros((c, LANES), F32)
                dkl = jnp.zeros((c, LANES), F32)
                ddec = jnp.zeros((1, LANES), F32)
                new = []
                for hh in range(2):
                    cols = slice(LANES * hh, LANES * (hh + 1))
                    vb = v_ref[r, cols].astype(BF16)
                    dob = do_ref[r, cols].astype(BF16)
                    qm = (qt * heads[hh]).astype(BF16)
                    a = jnp.where(tri, _dot(qm, ktb, NT), 0.0).astype(BF16)
                    da = jnp.where(tri, _dot(dob, vb, NT), 0.0).astype(BF16)
                    sn = stash[hh, n]
                    dst = dsts[hh]
                    dstb = dst.astype(BF16)
                    dqt = dqt + (_dot(da, ktb) + _dot(dob, sn.astype(BF16))) * heads[hh]
                    dkt = dkt + _dot(da, qm, TN)
                    dv = _dot(a, dob, TN) + _dot((kl * heads[hh]).astype(BF16), dstb, NT)
                    dkl = dkl + _dot(vb, dstb)
                    ddec = ddec + jnp.sum(dst * sn, axis=0, keepdims=True)
                    new.append(dst * dec + _dot(dob, qm, TN))
                    if d == 0:
                        dv_s[r, cols] = dv
                    else:
                        dv_ref[r, cols] = (dv_s[r, cols] + dv).astype(BF16)
                dlast = ddec * dec + jnp.sum(dkl * kl, axis=0, keepdims=True)
                dq = dqt * eq * scale
                dk = dkt * ek + dkl * el
                dcum = dqt * qt - dkt * kt - dkl * kl + last_row * dlast
                dla_s[d, r, :] = dcum
                if d == 0:
                    dq_s[r, :] = dq
                    dk_s[r, :] = dk
                else:
                    dq_ref[r, :] = (dq_s[r, :] + dq).astype(BF16)
                    dk_ref[r, :] = (dk_s[r, :] + dk).astype(BF16)
                return tuple(new)

            _chunk_loop(nchunk, step, (zero, zero), GLA_UNROLL)

        first = pl.program_id(0) == 0
        for d, (wa_ref, ba_ref, dwa_ref, dba_ref) in enumerate(
                ((waf_ref, baf_ref, dwaf_ref, dbaf_ref), (wab_ref, bab_ref, dwab_ref, dbab_ref))):
            dwa_ref[...] = jnp.zeros_like(dwa_ref)
            dba_ref[...] = jnp.zeros_like(dba_ref)

            def gates(i, carry):
                r = pl.ds(pl.multiple_of(i * tm, tm), tm)
                gab = ga_ref[r, :].astype(BF16)
                wab16 = wa_ref[...].astype(BF16)
                pre = _dot(gab, wab16) + ba_ref[...]
                dla = _chunk_running_sum(dla_s[d, r, :], suffix=(d == 0))
                dpre = dla * (1.0 / GLA_TAU) * _sigmoid(-pre)
                dpb = dpre.astype(BF16)
                dwa_ref[...] += _dot(gab, dpb, TN)
                dba_ref[...] += jnp.sum(dpre, axis=0, keepdims=True)
                dga = _dot(dpb, wab16, NT)
                if d == 0:
                    @pl.when(first)
                    def _():
                        dga_ref[r, :] = dga

                    @pl.when(jnp.logical_not(first))
                    def _():
                        dga_ref[r, :] += dga
                else:
                    dga_ref[r, :] += dga
                return carry

            lax.fori_loop(0, s // tm, gates, 0)

    w2 = 2 * LANES
    return _call(
        body, "gla_bwd", (2,),
        [pl.BlockSpec((s, LANES), lambda p: (0, B_Q + p), pipeline_mode=one),
         pl.BlockSpec((s, LANES), lambda p: (0, B_K + p), pipeline_mode=one),
         pl.BlockSpec((s, w2), lambda p: (0, B_V // 2 + p), pipeline_mode=one),
         pl.BlockSpec((s, LANES), lambda p: (0, GA), pipeline_mode=one),
         pl.BlockSpec((s, w2), lambda p: (0, p), pipeline_mode=one),
         pl.BlockSpec((LANES, LANES), lambda p: (0, p)),
         pl.BlockSpec((LANES, LANES), lambda p: (0, p)),
         pl.BlockSpec((1, LANES), lambda p: (0, p)),
         pl.BlockSpec((1, LANES), lambda p: (0, p))],
        [pl.BlockSpec((s, LANES), lambda p: (0, p), pipeline_mode=one),
         pl.BlockSpec((s, LANES), lambda p: (0, p), pipeline_mode=one),
         pl.BlockSpec((s, w2), lambda p: (0, p), pipeline_mode=one),
         pl.BlockSpec((s, LANES), lambda p: (0, 0), pipeline_mode=one),
         pl.BlockSpec((LANES, LANES), lambda p: (0, p)),
         pl.BlockSpec((LANES, LANES), lambda p: (0, p)),
         pl.BlockSpec((1, LANES), lambda p: (0, p)),
         pl.BlockSpec((1, LANES), lambda p: (0, p))],
        [jax.ShapeDtypeStruct((s, w2), BF16), jax.ShapeDtypeStruct((s, w2), BF16),
         jax.ShapeDtypeStruct((s, GROUP_W), BF16), jax.ShapeDtypeStruct((s, LANES), F32),
         jax.ShapeDtypeStruct((LANES, w2), F32), jax.ShapeDtypeStruct((LANES, w2), F32),
         jax.ShapeDtypeStruct((1, w2), F32), jax.ShapeDtypeStruct((1, w2), F32)],
        [pltpu.VMEM((2, s, LANES), F32), pltpu.VMEM((2, s, LANES), F32),
         pltpu.VMEM((2, nchunk, LANES, LANES), F32),
         pltpu.VMEM((s, LANES), F32), pltpu.VMEM((s, LANES), F32), pltpu.VMEM((s, w2), F32)],
        ("arbitrary",), (z, z, z, z, d_o, wa_f, wa_b, ba_f, ba_b), rider)


def _shift_rows(x, d, rowi):
    s = x.shape[0]
    if d == 0:
        return x
    y = pltpu.roll(x, d % s, 0)
    keep = (rowi >= d) if d > 0 else (rowi < s + d)
    return jnp.where(keep, y, 0.0)


def _run_sum(x, m, step, rowi):
    acc, n = x, 1
    while n < m:
        acc = acc + _shift_rows(acc, step * n, rowi)
        n *= 2
    return acc


def _pool_counts(s, w, rowi):
    hi = jnp.minimum(rowi + w // 2, s)
    lo = jnp.maximum(rowi - w // 2, 0)
    return (hi - lo).astype(F32)


def _pooled(u, w, rowi):
    s = u.shape[0]
    win = _shift_rows(_run_sum(u, w // 2, 1, rowi), 1, rowi) + _run_sum(u, w // 2, -1, rowi)
    return win / _pool_counts(s, w, rowi) - u


def _pool_fwd(z, pool_w, pool_scale):
    s = z.shape[0]
    one = pl.Buffered(1)

    def body(u_ref, g_ref, w_ref, sc_ref, y_ref):
        rowi = lax.broadcasted_iota(jnp.int32, (s, 1), 0)
        for g, w in enumerate(POOL_WINDOWS):
            cols = slice(LANES * g, LANES * (g + 1))
            pooled = _pooled(u_ref[:, cols], w, rowi)
            mixed = _dot(pooled.astype(BF16), w_ref[g].astype(BF16))
            y_ref[:, cols] = (_silu(g_ref[:, cols]) * (mixed * sc_ref[:, cols])).astype(BF16)

    return pl.pallas_call(
        body, name="pool_fwd", grid=(1,),
        in_specs=[pl.BlockSpec((s, GROUP_W), lambda i: (0, C_V // 4), pipeline_mode=one),
                  pl.BlockSpec((s, GROUP_W), lambda i: (0, C_G // 4), pipeline_mode=one),
                  pl.BlockSpec((4, LANES, LANES), lambda i: (0, 0, 0)),
                  pl.BlockSpec((1, GROUP_W), lambda i: (0, 0))],
        out_specs=pl.BlockSpec((s, GROUP_W), lambda i: (0, 0), pipeline_mode=one),
        out_shape=jax.ShapeDtypeStruct((s, GROUP_W), BF16),
        compiler_params=_params(("arbitrary",)),
    )(z, z, pool_w, pool_scale)


def _pool_bwd(z, dy, pool_w, pool_scale):
    s = z.shape[0]
    one = pl.Buffered(1)

    def body(u_ref, g_ref, dy_ref, w_ref, sc_ref, du_ref, dg_ref, dw_ref, dsc_ref):
        rowi = lax.broadcasted_iota(jnp.int32, (s, 1), 0)
        for g, w in enumerate(POOL_WINDOWS):
            cols = slice(LANES * g, LANES * (g + 1))
            gate, dyv, sc = g_ref[:, cols], dy_ref[:, cols], sc_ref[:, cols]
            wb = w_ref[g].astype(BF16)
            pooled = _pooled(u_ref[:, cols], w, rowi)
            pb = pooled.astype(BF16)
            mixed = _dot(pb, wb)
            dg_ref[:, cols] = (dyv * (mixed * sc) * _silu_grad(gate)).astype(BF16)
            dt = dyv * _silu(gate)
            dsc_ref[:, cols] = jnp.sum(dt * mixed, axis=0, keepdims=True)
            dmb = (dt * sc).astype(BF16)
            dw_ref[g] = _dot(pb, dmb, TN)
            dpool = _dot(dmb, wb, NT)
            e = dpool / _pool_counts(s, w, rowi)
            du_ref[:, cols] = (_run_sum(e, w // 2, 1, rowi) + _shift_rows(_run_sum(e, w // 2, -1, rowi), -1, rowi)
                               - dpool).astype(BF16)

    return pl.pallas_call(
        body, name="pool_bwd", grid=(1,),
        in_specs=[pl.BlockSpec((s, GROUP_W), lambda i: (0, C_V // 4), pipeline_mode=one),
                  pl.BlockSpec((s, GROUP_W), lambda i: (0, C_G // 4), pipeline_mode=one),
                  pl.BlockSpec((s, GROUP_W), lambda i: (0, 2), pipeline_mode=one),
                  pl.BlockSpec((4, LANES, LANES), lambda i: (0, 0, 0)),
                  pl.BlockSpec((1, GROUP_W), lambda i: (0, 0))],
        out_specs=[pl.BlockSpec((s, GROUP_W), lambda i: (0, 0), pipeline_mode=one),
                   pl.BlockSpec((s, GROUP_W), lambda i: (0, 0), pipeline_mode=one),
                   pl.BlockSpec((4, LANES, LANES), lambda i: (0, 0, 0)),
                   pl.BlockSpec((1, GROUP_W), lambda i: (0, 0))],
        out_shape=[jax.ShapeDtypeStruct((s, GROUP_W), BF16), jax.ShapeDtypeStruct((s, GROUP_W), BF16),
                   jax.ShapeDtypeStruct((4, LANES, LANES), F32), jax.ShapeDtypeStruct((1, GROUP_W), F32)],
        compiler_params=_params(("arbitrary",)),
    )(z, z, dy, pool_w, pool_scale)


def _mla_heads(qf, kv, kpe, qg, kg, cos, sin):
    out = []
    for h in range(4):
        qa = qf[:, LANES * h:LANES * (h + 1)]
        qb = qf[:, 512 + LANES * h:512 + LANES * (h + 1)]
        ka = kv[:, 256 * h:256 * h + LANES]
        rq = lax.rsqrt((jnp.sum(qa * qa, axis=-1, keepdims=True) + jnp.sum(qb * qb, axis=-1, keepdims=True))
                       * (1.0 / MLA_QK) + EPS)
        rk = lax.rsqrt((jnp.sum(ka * ka, axis=-1, keepdims=True) + jnp.sum(kpe * kpe, axis=-1, keepdims=True))
                       * (1.0 / MLA_QK) + EPS)
        out.append((qa, qb, rq, ka, rk))
    return out


def _mla_latents(mq_ref, mkv_ref, gq_ref, gkv_ref, wq_ref, wkv_ref):
    mq = mq_ref[...]
    rq = lax.rsqrt(jnp.mean(mq * mq, axis=-1, keepdims=True) + EPS)
    qn = mq * rq
    qnb = (qn * gq_ref[...]).astype(BF16)
    mkv = mkv_ref[...]
    rk = lax.rsqrt(jnp.mean(mkv * mkv, axis=-1, keepdims=True) + EPS)
    kvn = mkv * rk
    kvnb = (kvn * gkv_ref[...]).astype(BF16)
    qf = _dot(qnb, wq_ref[...])
    kv = _dot(kvnb, wkv_ref[...])
    return qn, rq, qnb, kvn, rk, kvnb, qf, kv


def _mla_prep(z, cos_m, sin_m, gq, wq, gkv, wkv, qg, kg, tm):
    s = z.shape[0]

    def body(mq_ref, mkv_ref, mkr_ref, cos_ref, sin_ref, gq_ref, wq_ref, gkv_ref, wkv_ref, qg_ref, kg_ref,
             q_ref, k_ref, v_ref):
        _, _, _, _, _, _, qf, kv = _mla_latents(mq_ref, mkv_ref, gq_ref, gkv_ref, wq_ref, wkv_ref)
        kpe = mkr_ref[...]
        cos, sin = cos_ref[...], sin_ref[...]
        qg, kg = qg_ref[...], kg_ref[...]
        for h, (qa, qb, rq, ka, rk) in enumerate(_mla_heads(qf, kv, kpe, qg, kg, cos, sin)):
            q_ref[h, :, 0:LANES] = (qa * rq * qg[:, 0:LANES] * ATTN_Q_SCALE).astype(BF16)
            q_ref[h, :, LANES:] = (_rope(qb * rq * qg[:, LANES:], cos, sin) * ATTN_Q_SCALE).astype(BF16)
            k_ref[h, :, 0:LANES] = (ka * rk * kg[:, 0:LANES]).astype(BF16)
            k_ref[h, :, LANES:] = _rope(kpe * rk * kg[:, LANES:], cos, sin).astype(BF16)
            v_ref[h] = kv[:, 256 * h + LANES:256 * (h + 1)].astype(BF16)

    full = lambda shape: pl.BlockSpec(shape, lambda i: (0,) * len(shape))
    return pl.pallas_call(
        body, name="mla_prep", grid=(s // tm,),
        in_specs=[pl.BlockSpec((tm, 512), lambda i: (i, M_Q // 4)),
                  pl.BlockSpec((tm, 256), lambda i: (i, M_KV // 2)),
                  pl.BlockSpec((tm, LANES), lambda i: (i, M_KR)),
                  pl.BlockSpec((tm, LANES), lambda i: (i, 0)),
                  pl.BlockSpec((tm, LANES), lambda i: (i, 0)),
                  full((1, 512)), full((512, 1024)), full((1, 256)), full((256, 1024)), full((1, 256)), full((1, 256))],
        out_specs=[pl.BlockSpec((4, tm, 256), lambda i: (0, i, 0)), pl.BlockSpec((4, tm, 256), lambda i: (0, i, 0)),
                   pl.BlockSpec((4, tm, LANES), lambda i: (0, i, 0))],
        out_shape=[jax.ShapeDtypeStruct((4, s, 256), BF16), jax.ShapeDtypeStruct((4, s, 256), BF16),
                   jax.ShapeDtypeStruct((4, s, LANES), BF16)],
        compiler_params=_params(("parallel",)),
    )(z, z, z, cos_m, sin_m, gq, wq, gkv, wkv, qg, kg)


def _mla_prep_bwd(z, cos_m, sin_m, gq, wq, gkv, wkv, qg, kg, dq, dk, dv, tm):
    s = z.shape[0]

    def body(mq_ref, mkv_ref, mkr_ref, cos_ref, sin_ref, gq_ref, wq_ref, gkv_ref, wkv_ref, qg_ref, kg_ref,
             dq_ref, dk_ref, dv_ref,
             dmq_ref, dmkv_ref, dmkr_ref, dwq_ref, dwkv_ref, dgq_ref, dgkv_ref, dqg_ref, dkg_ref, dqf, dkv):
        @pl.when(pl.program_id(0) == 0)
        def _():
            for r in (dwq_ref, dwkv_ref, dgq_ref, dgkv_ref, dqg_ref, dkg_ref):
                r[...] = jnp.zeros_like(r)

        qn, rq0, qnb, kvn, rk0, kvnb, qf, kv = _mla_latents(mq_ref, mkv_ref, gq_ref, gkv_ref, wq_ref, wkv_ref)
        kpe = mkr_ref[...]
        cos, sin = cos_ref[...], sin_ref[...]
        qg, kg = qg_ref[...], kg_ref[...]
        dkpe = jnp.zeros_like(kpe)
        inv = 1.0 / MLA_QK

        def norm_bwd(a, b, r, da_n, db_n, g):
            ga, gb = g[:, 0:LANES], g[:, LANES:]
            dg_a = jnp.sum(da_n * a * r, axis=0, keepdims=True)
            dg_b = jnp.sum(db_n * b * r, axis=0, keepdims=True)
            ua, ub = da_n * ga, db_n * gb
            dt = (jnp.sum(ua * a, axis=-1, keepdims=True) + jnp.sum(ub * b, axis=-1, keepdims=True)) * inv
            r3 = r * r * r
            return r * ua - a * (r3 * dt), r * ub - b * (r3 * dt), dg_a, dg_b

        for h, (qa, qb, rq, ka, rk) in enumerate(_mla_heads(qf, kv, kpe, qg, kg, cos, sin)):
            dqa, dqb, dga, dgb = norm_bwd(qa, qb, rq, dq_ref[h, :, 0:LANES] * ATTN_SCALE,
                                          _rope_t(dq_ref[h, :, LANES:] * ATTN_SCALE, cos, sin), qg)
            dqf[:, LANES * h:LANES * (h + 1)] = dqa
            dqf[:, 512 + LANES * h:512 + LANES * (h + 1)] = dqb
            dqg_ref[:, 0:LANES] += dga
            dqg_ref[:, LANES:] += dgb
            ln2 = math.log(2.0)
            dka, dkb, dga, dgb = norm_bwd(ka, kpe, rk, dk_ref[h, :, 0:LANES] * ln2,
                                          _rope_t(dk_ref[h, :, LANES:] * ln2, cos, sin), kg)
            dkv[:, 256 * h:256 * h + LANES] = dka
            dkv[:, 256 * h + LANES:256 * (h + 1)] = dv_ref[h]
            dkpe = dkpe + dkb
            dkg_ref[:, 0:LANES] += dga
            dkg_ref[:, LANES:] += dgb
        dmkr_ref[...] = dkpe.astype(BF16)

        def latent_bwd(dfull, w_ref, nb, n, r, g_ref, dw_ref, dg_ref, dlat_ref):
            db = dfull.astype(BF16)
            dn = _dot(db, w_ref[...], NT)
            dw_ref[...] += _dot(nb, db, TN)
            dg_ref[...] += jnp.sum(dn * n, axis=0, keepdims=True)
            u = dn * g_ref[...]
            dlat_ref[...] = (r * (u - n * jnp.mean(u * n, axis=-1, keepdims=True))).astype(BF16)

        latent_bwd(dqf[...], wq_ref, qnb, qn, rq0, gq_ref, dwq_ref, dgq_ref, dmq_ref)
        latent_bwd(dkv[...], wkv_ref, kvnb, kvn, rk0, gkv_ref, dwkv_ref, dgkv_ref, dmkv_ref)

    full = lambda shape: pl.BlockSpec(shape, lambda i: (0,) * len(shape))
    return pl.pallas_call(
        body, name="mla_prep_bwd", grid=(s // tm,),
        in_specs=[pl.BlockSpec((tm, 512), lambda i: (i, M_Q // 4)),
                  pl.BlockSpec((tm, 256), lambda i: (i, M_KV // 2)),
                  pl.BlockSpec((tm, LANES), lambda i: (i, M_KR)),
                  pl.BlockSpec((tm, LANES), lambda i: (i, 0)),
                  pl.BlockSpec((tm, LANES), lambda i: (i, 0)),
                  full((1, 512)), full((512, 1024)), full((1, 256)), full((256, 1024)), full((1, 256)), full((1, 256)),
                  pl.BlockSpec((4, tm, 256), lambda i: (0, i, 0)), pl.BlockSpec((4, tm, 256), lambda i: (0, i, 0)),
                  pl.BlockSpec((4, tm, LANES), lambda i: (0, i, 0))],
        out_specs=[pl.BlockSpec((tm, 512), lambda i: (i, 0)), pl.BlockSpec((tm, 256), lambda i: (i, 0)),
                   pl.BlockSpec((tm, LANES), lambda i: (i, 0)),
                   full((512, 1024)), full((256, 1024)), full((1, 512)), full((1, 256)), full((1, 256)), full((1, 256))],
        out_shape=[jax.ShapeDtypeStruct((s, 512), BF16), jax.ShapeDtypeStruct((s, 256), BF16),
                   jax.ShapeDtypeStruct((s, LANES), BF16),
                   jax.ShapeDtypeStruct((512, 1024), F32), jax.ShapeDtypeStruct((256, 1024), F32),
                   jax.ShapeDtypeStruct((1, 512), F32), jax.ShapeDtypeStruct((1, 256), F32),
                   jax.ShapeDtypeStruct((1, 256), F32), jax.ShapeDtypeStruct((1, 256), F32)],
        scratch_shapes=[pltpu.VMEM((tm, 1024), F32), pltpu.VMEM((tm, 1024), F32)],
        compiler_params=_params(("arbitrary",)),
    )(z, z, z, cos_m, sin_m, gq, wq, gkv, wkv, qg, kg, dq, dk, dv)


def _attn_fwd(q, k, v, z, tq, rider=None):
    s = q.shape[1]

    def body(q_ref, k_ref, v_ref, g_ref, o_ref, y_ref, lse_ref):
        sc = _dot(q_ref[...], k_ref[...], NT)
        m = jnp.max(sc, axis=-1, keepdims=True)
        p = jnp.exp2(sc - m)
        l = jnp.sum(p, axis=-1, keepdims=True)
        o = _dot(p.astype(BF16), v_ref[...]) / l
        o_ref[...] = o
        y_ref[...] = (_silu(g_ref[...]) * o).astype(BF16)
        lse_ref[...] = m + jnp.log2(l)

    return _call(
        body, "attn_fwd", (4, s // tq),
        [pl.BlockSpec((None, tq, 256), lambda h, i: (h, i, 0)),
         pl.BlockSpec((None, s, 256), lambda h, i: (h, 0, 0)),
         pl.BlockSpec((None, s, LANES), lambda h, i: (h, 0, 0)),
         pl.BlockSpec((tq, LANES), lambda h, i: (i, M_G + h))],
        [pl.BlockSpec((tq, LANES), lambda h, i: (i, h)), pl.BlockSpec((tq, LANES), lambda h, i: (i, h)),
         pl.BlockSpec((None, tq, 1), lambda h, i: (h, i, 0))],
        [jax.ShapeDtypeStruct((s, GROUP_W), F32), jax.ShapeDtypeStruct((s, GROUP_W), BF16),
         jax.ShapeDtypeStruct((4, s, 1), F32)],
        [], ("parallel", "parallel"), (q, k, v, z), rider)


def _attn_bwd(q, k, v, z, o, lse, dy, tq, rider=None):
    s = q.shape[1]

    def body(q_ref, k_ref, v_ref, g_ref, o_ref, lse_ref, dy_ref, dq_ref, dk_ref, dv_ref, dg_ref):
        @pl.when(pl.program_id(1) == 0)
        def _():
            dk_ref[...] = jnp.zeros_like(dk_ref)
            dv_ref[...] = jnp.zeros_like(dv_ref)

        gate, ov, dyv = g_ref[...], o_ref[...], dy_ref[...]
        do = dyv * _silu(gate)
        dg_ref[...] = (dyv * ov * _silu_grad(gate)).astype(BF16)
        delta = jnp.sum(do * ov, axis=-1, keepdims=True)
        dob = do.astype(BF16)
        qb, kb = q_ref[...], k_ref[...]
        p = jnp.exp2(_dot(qb, kb, NT) - lse_ref[...])
        dp = _dot(dob, v_ref[...], NT)
        ds = (p * (dp - delta)).astype(BF16)
        dq_ref[...] = _dot(ds, kb)
        dk_ref[...] += _dot(ds, qb, TN)
        dv_ref[...] += _dot(p.astype(BF16), dob, TN)

    return _call(
        body, "attn_bwd", (4, s // tq),
        [pl.BlockSpec((None, tq, 256), lambda h, i: (h, i, 0)),
         pl.BlockSpec((None, s, 256), lambda h, i: (h, 0, 0)),
         pl.BlockSpec((None, s, LANES), lambda h, i: (h, 0, 0)),
         pl.BlockSpec((tq, LANES), lambda h, i: (i, M_G + h)),
         pl.BlockSpec((tq, LANES), lambda h, i: (i, h)),
         pl.BlockSpec((None, tq, 1), lambda h, i: (h, i, 0)),
         pl.BlockSpec((tq, LANES), lambda h, i: (i, 12 + h))],
        [pl.BlockSpec((None, tq, 256), lambda h, i: (h, i, 0)),
         pl.BlockSpec((None, s, 256), lambda h, i: (h, 0, 0)),
         pl.BlockSpec((None, s, LANES), lambda h, i: (h, 0, 0)),
         pl.BlockSpec((tq, LANES), lambda h, i: (i, h))],
        [jax.ShapeDtypeStruct((4, s, 256), F32), jax.ShapeDtypeStruct((4, s, 256), F32),
         jax.ShapeDtypeStruct((4, s, LANES), F32), jax.ShapeDtypeStruct((s, GROUP_W), BF16)],
        [], ("parallel", "arbitrary"), (q, k, v, z, o, lse, dy), rider)


def _adam(parts, w, m, v, name, tr):
    r, c = w.shape
    tr = min(tr, r)
    c1 = 1.0 - ADAM_B1 ** ADAM_STEP
    c2 = 1.0 - ADAM_B2 ** ADAM_STEP

    def body(p_ref, w_ref, m_ref, v_ref, g_ref, d_ref, nm_ref, nv_ref):
        g = p_ref[0].astype(F32)
        for i in range(1, N_DEV):
            g = g + p_ref[i].astype(F32)
        nm = ADAM_B1 * m_ref[...] + (1.0 - ADAM_B1) * g
        nv = ADAM_B2 * v_ref[...] + (1.0 - ADAM_B2) * (g * g)
        g_ref[...] = g
        nm_ref[...] = nm
        nv_ref[...] = nv
        d_ref[...] = -ADAM_LR * ((nm / c1) / (jnp.sqrt(nv / c2) + ADAM_EPS) + ADAM_WD * w_ref[...])

    blk = lambda: pl.BlockSpec((tr, c), lambda i: (i, 0))
    return pl.pallas_call(
        body, name=name, grid=(r // tr,),
        in_specs=[pl.BlockSpec((N_DEV, tr, c), lambda i: (0, i, 0)), blk(), blk(), blk()],
        out_specs=[blk(), blk(), blk(), blk()],
        out_shape=[jax.ShapeDtypeStruct((r, c), F32)] * 4,
        compiler_params=_params(("parallel",)),
    )(parts, w, m, v)


def _adam_columns(parts, w, m, v, name, tc, rider=None):
    nl, r, c = w.shape
    pieces = [p for layer in parts for p in layer]
    nh = len(parts[0])
    rp = pieces[0].shape[2]
    tc = min(tc, rp)
    ncb = rp // tc
    c1 = 1.0 - ADAM_B1 ** ADAM_STEP
    c2 = 1.0 - ADAM_B2 ** ADAM_STEP

    def body(*refs):
        p_refs, (w_ref, m_ref, v_ref, g_ref, d_ref, nm_ref, nv_ref) = refs[:len(pieces)], refs[len(pieces):]
        for h in range(nh):
            @pl.when(pl.program_id(0) == h)
            def _(h=h):
                for l in range(nl):
                    p_ref = p_refs[l * nh + h]
                    g = p_ref[0].astype(F32)
                    for i in range(1, N_DEV):
                        g = g + p_ref[i].astype(F32)
                    nm = ADAM_B1 * m_ref[:, l, :] + (1.0 - ADAM_B1) * g
                    nv = ADAM_B2 * v_ref[:, l, :] + (1.0 - ADAM_B2) * (g * g)
                    g_ref[:, l, :] = g
                    nm_ref[:, l, :] = nm
                    nv_ref[:, l, :] = nv
                    d_ref[:, l, :] = -ADAM_LR * ((nm / c1) / (jnp.sqrt(nv / c2) + ADAM_EPS) + ADAM_WD * w_ref[:, l, :])

    def part_spec(j):
        return pl.BlockSpec((N_DEV, c, tc), lambda h, i: (0, 0, jnp.clip((h - j % nh) * ncb + i, 0, ncb - 1)))

    blk = lambda: pl.BlockSpec((c, nl, tc), lambda h, i: (0, 0, h * ncb + i))
    t = lambda a: jnp.transpose(a, (2, 0, 1))
    *res, = _call(body, name, (nh, ncb), [part_spec(j) for j in range(len(pieces))] + [blk(), blk(), blk()],
                  [blk(), blk(), blk(), blk()], [jax.ShapeDtypeStruct((c, nl, r), F32)] * 4, [],
                  ("arbitrary",) * 2, (*pieces, t(w), t(m), t(v)), rider)
    return [jnp.transpose(a, (1, 2, 0)) for a in res[:4]] + res[4:]


def _adam_layers(parts, w, m, v, name, tr, rider=None):
    nl, r, c = w.shape
    pieces = [p for layer in parts for p in layer]
    rp = pieces[0].shape[1]
    tr = min(tr, rp)
    nr, nrp = r // tr, rp // tr
    c1 = 1.0 - ADAM_B1 ** ADAM_STEP
    c2 = 1.0 - ADAM_B2 ** ADAM_STEP

    def body(*refs):
        p_refs, (w_ref, m_ref, v_ref, g_ref, d_ref, nm_ref, nv_ref) = refs[:len(pieces)], refs[len(pieces):]
        at = pl.program_id(0) * nr + pl.program_id(1)
        for j in range(len(pieces)):
            @pl.when(jnp.logical_and(at >= j * nrp, at < (j + 1) * nrp))
            def _(p_ref=p_refs[j]):
                g = p_ref[0].astype(F32)
                for i in range(1, N_DEV):
                    g = g + p_ref[i].astype(F32)
                nm = ADAM_B1 * m_ref[...] + (1.0 - ADAM_B1) * g
                nv = ADAM_B2 * v_ref[...] + (1.0 - ADAM_B2) * (g * g)
                g_ref[...] = g
                nm_ref[...] = nm
                nv_ref[...] = nv
                d_ref[...] = -ADAM_LR * ((nm / c1) / (jnp.sqrt(nv / c2) + ADAM_EPS) + ADAM_WD * w_ref[...])

    def part_spec(j):
        return pl.BlockSpec((N_DEV, tr, c), lambda ll, i: (0, jnp.clip(ll * nr + i - j * nrp, 0, nrp - 1), 0))

    blk = lambda: pl.BlockSpec((None, tr, c), lambda ll, i: (ll, i, 0))
    return _call(body, name, (nl, nr), [part_spec(j) for j in range(len(pieces))] + [blk(), blk(), blk()],
                 [blk(), blk(), blk(), blk()], [jax.ShapeDtypeStruct((nl, r, c), F32)] * 4, [],
                 ("arbitrary", "arbitrary"), (*pieces, w, m, v), rider)


REPLICATED = ("norm_g", "ret_norm_g", "gla_ba_f", "gla_ba_b", "gla_norm_g", "pool_w", "pool_scale",
              "mla_q_norm_g", "mla_kv_norm_g", "mla_qk_norm_q", "mla_qk_norm_k")
REPLICATED_EARLY = REPLICATED[1:]
SMALL_SHARDED = ("mla_wq_b", "mla_wkv_b", "gla_wa2_f", "gla_wa2_b")
WEIGHTS = ("norm_g", "w_in", "ret_norm_g", "gla_wa2_f", "gla_ba_f", "gla_wa2_b", "gla_ba_b", "gla_norm_g", "pool_w",
           "pool_scale", "mla_q_norm_g", "mla_wq_b", "mla_kv_norm_g", "mla_wkv_b", "mla_qk_norm_q", "mla_qk_norm_k",
           "w_out")


PACK_ROWS = 16


def _packed_rows(a):
    rows = a.size // LANES
    return rows, -(-rows // PACK_ROWS) * PACK_ROWS


def _pack(arrays, dtype):
    parts = []
    for a in arrays:
        rows, padded = _packed_rows(a)
        parts.append(jnp.pad(a.reshape(rows, LANES).astype(dtype), ((0, padded - rows), (0, 0))))
    return jnp.concatenate(parts, axis=0)


def _unpack(packed, like):
    out, at = [], 0
    for a in like:
        rows, padded = _packed_rows(a)
        out.append(packed[..., at:at + rows, :].reshape(packed.shape[:-2] + a.shape))
        at += padded
    return out


def _columns_by_device(g):
    l, r, n = g.shape
    return g.reshape(l, r, N_DEV, n // N_DEV).transpose(2, 0, 1, 3)


def _gathered_columns(g, l, r, c):
    return g.reshape(N_DEV, l, r, c).transpose(1, 2, 0, 3).reshape(l, r, N_DEV * c)


def _layer_forward(x, wts, late_wts, tables, tm, tq, ride_inproj=None, ride_attn=None, target=None):
    cos_r, sin_r, cos_m, sin_m, tab, _ = tables
    z, h, *carried_in = _inproj(x, wts["norm_g"], wts["w_in"], min(x.shape[0], 2 * tm), rider=ride_inproj)
    wts.update(late_wts("inproj", carried_in))
    o_a, y_a = _ret_fwd(z, cos_r, sin_r, tab, wts["ret_norm_g"])
    o_b, y_b = _gla_fwd(z, wts["wa_f"], wts["wa_b"], wts["gla_ba_f"], wts["gla_ba_b"], wts["gla_norm_g"])
    y_c = _pool_fwd(z, wts["pool_w"], wts["pool_scale"])
    q, k, v = _mla_prep(z, cos_m, sin_m, wts["mla_q_norm_g"], wts["wq"], wts["mla_kv_norm_g"], wts["wkv"],
                        wts["qk_q"], wts["qk_k"], tm)
    o_d, y_d, lse, *carried_attn = _attn_fwd(q, k, v, z, tq, rider=ride_attn)
    wts.update(late_wts("attn", carried_attn))
    y = jnp.concatenate([y_a, y_b, y_c, y_d], axis=1)
    w_out = wts["w_out"]
    if target is None:
        x_next = _mm(y, w_out, "nn", "outproj", tm, D_MODEL, 1024, add=x)
    else:
        x_next = _mm(y, w_out, "nn", "outproj_loss", tm, D_MODEL, 1024, tail=_loss_tail(x, target))
    saved = dict(x=x, z=z, h=h, o_a=o_a, o_b=o_b, o_d=o_d, lse=lse, q=q, k=k, v=v, y=y, w_out=w_out)
    return x_next, saved, carried_in, carried_attn


def _layer_backward(dx, sv, wts, tables, tm, tq, rides):
    cos_r, sin_r, cos_m, sin_m, tab, tab_sw = tables
    z = sv["z"]
    g = {}
    carried = {}

    def rider(name):
        return rides[name](g) if name in rides else None

    def landed(name, results, n_own):
        if name in rides:
            carried[name] = list(results[n_own:])
        return results[:n_own]

    g["w_out"] = _mm(sv["y"], dx, "tn", "d_w_out", 2048, 1024, 1024, out_dtype=BF16)
    dy = _mm(dx, sv["w_out"], "nt", "d_y", tm, 2048, 1024)

    do_a, dg_a, g["ret_norm_g"] = _normgate_bwd(sv["o_a"], z, A_G, dy, 0, wts["ret_norm_g"], tm)
    dq_a, dk_a, dv_a = landed("ret", _ret_bwd(z, do_a, cos_r, sin_r, tab, tab_sw, rider=rider("ret")), 3)

    do_b, dg_b, g["gla_norm_g"] = _normgate_bwd(sv["o_b"], z, B_G, dy, 1, wts["gla_norm_g"], tm)
    dq_b, dk_b, dv_b, d_ga, d_waf, d_wab, g["gla_ba_f"], g["gla_ba_b"] = landed("gla", _gla_bwd(
        z, do_b, wts["wa_f"], wts["wa_b"], wts["gla_ba_f"], wts["gla_ba_b"], rider=rider("gla")), 8)
    g["gla_wa2_f"] = d_waf[0:GLA_RANK]
    g["gla_wa2_b"] = d_wab[GLA_RANK:2 * GLA_RANK]

    du_c, dg_c, g["pool_w"], g["pool_scale"] = _pool_bwd(z, dy, wts["pool_w"], wts["pool_scale"])

    d_q, d_k, d_v, dg_d = landed("attn", _attn_bwd(sv["q"], sv["k"], sv["v"], z, sv["o_d"], sv["lse"], dy, tq,
                                                   rider=rider("attn")), 4)
    (d_mq, d_mkv, d_mkr, d_wq, g["mla_wkv_b"], g["mla_q_norm_g"], g["mla_kv_norm_g"], d_qg, d_kg) = _mla_prep_bwd(
        z, cos_m, sin_m, wts["mla_q_norm_g"], wts["wq"], wts["mla_kv_norm_g"], wts["wkv"], wts["qk_q"], wts["qk_k"],
        d_q, d_k, d_v, tm)
    g["mla_wq_b"] = _unpad_wq(d_wq)
    g["mla_qk_norm_q"] = d_qg[:, _QK_INV]
    g["mla_qk_norm_k"] = d_kg[:, _QK_INV]

    dz = jnp.concatenate([dq_a, dk_a, dv_a, dg_a, dq_b, dk_b, dv_b, dg_b, d_mq, du_c, dg_c, dg_d, d_mkv,
                          d_ga.astype(BF16), d_mkr], axis=1)
    h, half = sv["h"], D_MODEL // 2
    for name, cols in (("d_w_in_a", h[:, :half]), ("d_w_in_b", h[:, half:])):
        res = _mm(dz, cols, "tn", name, 2048, 1024, 1024, out_dtype=BF16, rider=rider(name))
        (d_wt,) = landed(name, res if name in rides else [res], 1)
        g["w_in" + name[-2:]] = _split_w_in(d_wt)
    dx_in, g["norm_g"] = landed("d_h", _mm(dz, wts["w_in"], "nn", "d_h", tm, D_MODEL, 1024, rider=rider("d_h"),
                                           tail=_norm_bwd_tail(sv["x"], wts["norm_g"], dx)), 2)
    return dx_in, g, carried


def kernel(x, norm_g, w_in, ret_norm_g, gla_wa2_f, gla_ba_f, gla_wa2_b, gla_ba_b, gla_norm_g, pool_w, pool_scale, mla_q_norm_g, mla_wq_b, mla_kv_norm_g, mla_wkv_b, mla_qk_norm_q, mla_qk_norm_k, w_out, loss_target, m_norm_g, m_w_in, m_ret_norm_g, m_gla_wa2_f, m_gla_ba_f, m_gla_wa2_b, m_gla_ba_b, m_gla_norm_g, m_pool_w, m_pool_scale, m_mla_q_norm_g, m_mla_wq_b, m_mla_kv_norm_g, m_mla_wkv_b, m_mla_qk_norm_q, m_mla_qk_norm_k, m_w_out, v_norm_g, v_w_in, v_ret_norm_g, v_gla_wa2_f, v_gla_ba_f, v_gla_wa2_b, v_gla_ba_b, v_gla_norm_g, v_pool_w, v_pool_scale, v_mla_q_norm_g, v_mla_wq_b, v_mla_kv_norm_g, v_mla_wkv_b, v_mla_qk_norm_q, v_mla_qk_norm_k, v_w_out):
    w = dict(norm_g=norm_g, w_in=w_in, ret_norm_g=ret_norm_g, gla_wa2_f=gla_wa2_f, gla_ba_f=gla_ba_f,
             gla_wa2_b=gla_wa2_b, gla_ba_b=gla_ba_b, gla_norm_g=gla_norm_g, pool_w=pool_w, pool_scale=pool_scale,
             mla_q_norm_g=mla_q_norm_g, mla_wq_b=mla_wq_b, mla_kv_norm_g=mla_kv_norm_g, mla_wkv_b=mla_wkv_b,
             mla_qk_norm_q=mla_qk_norm_q, mla_qk_norm_k=mla_qk_norm_k, w_out=w_out)
    m = dict(norm_g=m_norm_g, w_in=m_w_in, ret_norm_g=m_ret_norm_g, gla_wa2_f=m_gla_wa2_f, gla_ba_f=m_gla_ba_f,
             gla_wa2_b=m_gla_wa2_b, gla_ba_b=m_gla_ba_b, gla_norm_g=m_gla_norm_g, pool_w=m_pool_w,
             pool_scale=m_pool_scale, mla_q_norm_g=m_mla_q_norm_g, mla_wq_b=m_mla_wq_b, mla_kv_norm_g=m_mla_kv_norm_g,
             mla_wkv_b=m_mla_wkv_b, mla_qk_norm_q=m_mla_qk_norm_q, mla_qk_norm_k=m_mla_qk_norm_k, w_out=m_w_out)
    v = dict(norm_g=v_norm_g, w_in=v_w_in, ret_norm_g=v_ret_norm_g, gla_wa2_f=v_gla_wa2_f, gla_ba_f=v_gla_ba_f,
             gla_wa2_b=v_gla_wa2_b, gla_ba_b=v_gla_ba_b, gla_norm_g=v_gla_norm_g, pool_w=v_pool_w,
             pool_scale=v_pool_scale, mla_q_norm_g=v_mla_q_norm_g, mla_wq_b=v_mla_wq_b, mla_kv_norm_g=v_mla_kv_norm_g,
             mla_wkv_b=v_mla_wkv_b, mla_qk_norm_q=v_mla_qk_norm_q, mla_qk_norm_k=v_mla_qk_norm_k, w_out=v_w_out)
    xs, target = x[0], loss_target[0]
    s = xs.shape[0]
    tm, tq = min(s, 512), min(s, 256)
    c_in = w_in.shape[2]

    w_in_b = jnp.transpose(w_in, (2, 0, 1)).astype(BF16)
    w_out_b = w_out.astype(BF16).reshape(-1, D_MODEL)
    (w_in_g0,) = _exchange([("gather", w_in_b[:, 0])], "gather_first")
    tables = _rope_tables(s) + _ret_tables()

    def early_weights(l, w_in_g):
        return dict(
            norm_g=norm_g[l][None], w_in=_assemble_w_in(w_in_g), ret_norm_g=ret_norm_g[l][None],
            gla_ba_f=gla_ba_f[l][None], gla_ba_b=gla_ba_b[l][None],
            gla_norm_g=gla_norm_g[l][None], pool_w=pool_w[l], pool_scale=pool_scale[l][None],
            mla_q_norm_g=mla_q_norm_g[l][None], mla_kv_norm_g=mla_kv_norm_g[l][None],
            qk_q=_pad_qk_gain(mla_qk_norm_q[l]), qk_k=_pad_qk_gain(mla_qk_norm_k[l]))

    def small_weights(l, small_g):
        shards = _unpack(small_g, [w[n] for n in SMALL_SHARDED])
        full = {n: _gathered_columns(shards[i], *w[n].shape)[l] for i, n in enumerate(SMALL_SHARDED)}
        wa_f = jnp.zeros((LANES, 2 * LANES), BF16).at[0:GLA_RANK].set(full["gla_wa2_f"])
        wa_b = jnp.zeros((LANES, 2 * LANES), BF16).at[GLA_RANK:2 * GLA_RANK].set(full["gla_wa2_b"])
        return dict(wa_f=wa_f, wa_b=wa_b, wq=_pad_wq(full["mla_wq_b"]), wkv=full["mla_wkv_b"])

    def w_out_layer(l, w_out_g):
        return dict(w_out=w_out_g.reshape(N_DEV, DEPTH, -1, D_MODEL)[:, l].reshape(-1, D_MODEL))

    by_owner = lambda g_w_out: g_w_out.reshape(N_DEV, -1, D_MODEL)

    layers = [early_weights(0, w_in_g0), None]
    x1, sv0, (w_in_g1, small_g), (w_out_g,) = _layer_forward(
        xs, layers[0], lambda carrier, got: small_weights(0, got[1]) if carrier == "inproj" else w_out_layer(0, got[0]),
        tables, tm, tq,
        ride_inproj=[("gather", w_in_b[:, 1]), ("gather", _pack([w[n] for n in SMALL_SHARDED], BF16))],
        ride_attn=("gather", w_out_b))
    layers[1] = early_weights(1, w_in_g1)
    (dx, loss_row), sv1, _, _ = _layer_forward(
        x1, layers[1], lambda carrier, got: {**small_weights(1, small_g), **w_out_layer(1, w_out_g)}, tables, tm, tq,
        target=target)
    loss = lax.psum(loss_row[0, 0], ("x", "y", "c"))

    def small_jobs(g):
        grads = (g, g1)
        full = {n: jnp.stack([grads[l][n].reshape(w[n].shape[1:]) if n in REPLICATED else grads[l][n]
                              for l in range(DEPTH)]) for n in SMALL_SHARDED + REPLICATED_EARLY}
        small_c = jax.vmap(lambda *shards: _pack(shards, F32))(*[_columns_by_device(full[n]) for n in SMALL_SHARDED])
        return [("scatter", small_c),
                ("gather", _pack([full[n] for n in REPLICATED_EARLY], F32))]

    dx, g1, got1 = _layer_backward(dx, sv1, layers[1], tables, tm, tq, {
        "attn": lambda g: ("scatter", by_owner(g["w_out"]))})
    dx, g0, got0 = _layer_backward(dx, sv0, layers[0], tables, tm, tq, {
        "gla": lambda g: ("scatter", g1["w_in_b"]),
        "attn": lambda g: [("scatter", g1["w_in_a"]), ("scatter", by_owner(g["w_out"]))],
        "d_w_in_a": small_jobs,
        "d_w_in_b": lambda g: ("scatter", g["w_in_a"]),
        "d_h": lambda g: ("scatter", g["w_in_b"])})
    in_parts = ((got0["d_w_in_b"][0], got0["d_h"][0]), (got0["attn"][0], got0["gla"][0]))
    out_parts = ((got0["attn"][1],), (got1["attn"][0],))
    small_parts, rep_parts = got0["d_w_in_a"]
    norm_pack = _pack([jnp.stack([g0["norm_g"][0], g1["norm_g"][0]])], F32)

    out = {}
    out["w_in"] = _adam_columns(in_parts, w_in, m_w_in, v_w_in, "adam_w_in", 256)
    *out["w_out"], norm_parts = _adam_layers(out_parts, w_out, m_w_out, v_w_out, "adam_w_out", 128,
                                             rider=("gather", norm_pack))
    for names, parts, label in ((SMALL_SHARDED, small_parts, "adam_small"),
                                (REPLICATED_EARLY, rep_parts, "adam_replicated"), (("norm_g",), norm_parts, "adam_norm")):
        res = _adam(parts, _pack([w[n] for n in names], F32), _pack([m[n] for n in names], F32),
                    _pack([v[n] for n in names], F32), label, 2048)
        for n, *vals in zip(names, *[_unpack(a, [w[n] for n in names]) for a in res]):
            out[n] = vals

    return (loss, dx[None], *[out[n][0] for n in WEIGHTS], *[out[n][1] for n in WEIGHTS],
            *[out[n][2] for n in WEIGHTS], *[out[n][3] for n in WEIGHTS])
```

```python
import functools
import math

import numpy as np
import jax
import jax.numpy as jnp
from jax import lax
from jax.experimental import pallas as pl
from jax.experimental.pallas import tpu as pltpu

F32 = jnp.float32
BF16 = jnp.bfloat16

N_DEV = 8
D_MODEL = 2048
DEPTH = 2
GROUP_W = 512
EPS = 1e-6
ROPE_THETA = 10000.0
LANES = 128

RET_HD = 128
RET_CHUNK = 256
RET_UNROLL = 16
GLA_CHUNK = 64
GLA_UNROLL = 16
GLA_CUM_ROWS = 256
GLA_DK = 64
GLA_TAU = 16.0
GLA_RANK = 16
POOL_WINDOWS = (2, 4, 8, 16)
MLA_QK = 192
MLA_ROPE = 64
ATTN_SCALE = MLA_QK ** -0.5
ATTN_Q_SCALE = ATTN_SCALE * math.log2(math.e)
IN_COLS = 5984

ADAM_LR = 0.001
ADAM_B1 = 0.9
ADAM_B2 = 0.999
ADAM_EPS = 1e-08
ADAM_WD = 0.01
ADAM_STEP = 10

A_Q, A_K, A_V, A_G = 0, 4, 8, 12
B_Q, B_K, B_V, B_G = 16, 18, 20, 24
M_Q, C_V, C_G, M_G = 28, 32, 36, 40
M_KV, GA, M_KR = 44, 46, 47
ZP_COLS = 48 * LANES

VMEM_LIMIT = 56 * 1024 * 1024


def _params(sem, vmem=VMEM_LIMIT):
    return pltpu.CompilerParams(dimension_semantics=sem, vmem_limit_bytes=vmem)


def _sigmoid(x):
    return 1.0 / (1.0 + jnp.exp(-x))


def _silu(x):
    return x * _sigmoid(x)


def _silu_grad(x):
    s = _sigmoid(x)
    return s * (1.0 + x * (1.0 - s))


def _dot(a, b, dims=(((1,), (0,)), ((), ()))):
    return lax.dot_general(a, b, dims, preferred_element_type=F32)


NT = (((1,), (1,)), ((), ()))
TN = (((0,), (0,)), ((), ()))


def _chunk_loop(n, body, init, unroll):
    unroll = math.gcd(n, unroll)

    def trip(t, carry):
        for u in range(unroll):
            carry = body(t * unroll + u, carry)
        return carry

    return lax.fori_loop(0, n // unroll, trip, init)


def _roll_lanes_half(x):
    return pltpu.roll(x, 64, 1)


def _wq_perm():
    idx = np.zeros((1024,), np.int32)
    ok = np.zeros((1024,), bool)
    for h in range(4):
        idx[128 * h:128 * h + 128] = 192 * h + np.arange(128)
        ok[128 * h:128 * h + 128] = True
        base = 512 + 128 * h
        idx[base:base + 32] = 192 * h + 128 + np.arange(32)
        ok[base:base + 32] = True
        idx[base + 64:base + 96] = 192 * h + 160 + np.arange(32)
        ok[base + 64:base + 96] = True
    inv = np.zeros((768,), np.int32)
    inv[idx[ok]] = np.nonzero(ok)[0]
    return idx, ok, inv


_WQ_IDX, _WQ_OK, _WQ_INV = _wq_perm()


def _pad_wq(wq):
    return jnp.where(jnp.asarray(_WQ_OK)[None, :], wq[:, _WQ_IDX], 0).astype(wq.dtype)


def _unpad_wq(wqp):
    return wqp[:, _WQ_INV]


def _qk_idx():
    idx = np.zeros((256,), np.int32)
    ok = np.zeros((256,), bool)
    idx[0:128] = np.arange(128)
    ok[0:128] = True
    idx[128:160] = 128 + np.arange(32)
    ok[128:160] = True
    idx[192:224] = 160 + np.arange(32)
    ok[192:224] = True
    inv = np.zeros((192,), np.int32)
    inv[idx[ok]] = np.nonzero(ok)[0]
    return idx, ok, inv


_QK_IDX, _QK_OK, _QK_INV = _qk_idx()


def _pad_qk_gain(g):
    return jnp.where(jnp.asarray(_QK_OK), g[_QK_IDX], 0.0).reshape(1, 256)


def _rope_tables(s):
    def tabs(dim):
        inv = 1.0 / (ROPE_THETA ** (jnp.arange(0, dim, 2, dtype=F32) / dim))
        ang = jnp.arange(s, dtype=F32)[:, None] * inv[None, :]
        return jnp.cos(ang), jnp.sin(ang)
    cr, sr = tabs(RET_HD)
    cos_r = jnp.concatenate([cr, cr], axis=1)
    sin_r = jnp.concatenate([-sr, sr], axis=1)
    cm, sm = tabs(MLA_ROPE)
    zz = jnp.zeros_like(cm)
    cos_m = jnp.concatenate([cm, zz, cm, zz], axis=1)
    sin_m = jnp.concatenate([-sm, zz, sm, zz], axis=1)
    return cos_r, sin_r, cos_m, sin_m


def _rope(x, cos, sin):
    return x * cos + _roll_lanes_half(x) * sin


def _rope_t(x, cos, sin):
    return x * cos + _roll_lanes_half(x * sin)


def _ret_tables():
    c = RET_CHUNK
    gamma_f = 1.0 - 2.0 ** (-5.0 - jnp.arange(4, dtype=F32))
    gamma_b = gamma_f[::-1]
    idx = jnp.arange(c, dtype=F32)
    diff = idx[:, None] - idx[None, :]

    def build(g1, g2):
        l1 = jnp.log(g1)[:, None, None]
        l2 = jnp.log(g2)[:, None, None]
        d1 = jnp.where(diff >= 0, jnp.exp(jnp.maximum(diff, 0.0)[None] * l1), 0.0)
        d2 = jnp.where(diff <= 0, jnp.exp(jnp.maximum(-diff, 0.0)[None] * l2), 0.0)
        ones = jnp.ones((1, c, LANES), F32)
        col = idx[None, :, None]
        qdf = jnp.exp((col + 1.0) * l1) * ones
        kdf = jnp.exp((c - 1.0 - col) * l1) * ones
        qdb = jnp.exp((c - col) * l2) * ones
        kdb = jnp.exp(col * l2) * ones
        cd1 = jnp.exp(c * l1) * ones
        cd2 = jnp.exp(c * l2) * ones
        return jnp.concatenate([d1 + d2, qdf, kdf, qdb, kdb, cd1, cd2], axis=2)

    return build(gamma_f, gamma_b), build(gamma_b, gamma_f)


MESH = pl.DeviceIdType.MESH
ANY = pl.BlockSpec(memory_space=pl.ANY)
_RELATIONS = ((0, 0, 1), (1, 0, 0), (0, 1, 0), (1, 1, 0), (1, 0, 1), (0, 1, 1), (1, 1, 1))


def _position():
    return lax.axis_index("x"), lax.axis_index("y"), lax.axis_index("c")


def _gather_copies(x_ref, out_ref, send_sems, recv_sems, local_sem, starting):
    x, y, cc = _position()
    me, sibling = (x, y, cc), (x, y, 1 - cc)
    chips = [(1 - x, y), (x, 1 - y), (1 - x, 1 - y)]

    def slab(px, py, pc):
        return out_ref.at[4 * px + 2 * py + pc]

    def copy(k, block, to, src=None):
        return pltpu.make_async_remote_copy(
            src_ref=slab(*block) if src is None else src, dst_ref=slab(*block),
            send_sem=send_sems.at[k], recv_sem=recv_sems.at[k], device_id=to, device_id_type=MESH)

    mine = pltpu.make_async_copy(x_ref, slab(*me), local_sem)
    first = [copy(0, me, sibling, src=x_ref)] + [copy(1 + j, me, (*chip, cc), src=x_ref) for j, chip in enumerate(chips)]
    if starting:
        return mine, first
    passed = [copy(4 + j, (*chip, cc), sibling) for j, chip in enumerate(chips)]
    arrivals = [copy(1 + j, (*chip, cc), me) for j, chip in enumerate(chips)]
    late = [copy(0, sibling, me)] + [copy(4 + j, (*chip, 1 - cc), me) for j, chip in enumerate(chips)]
    return mine, first, passed, arrivals, late


def _gather_start(*refs):
    mine, first = _gather_copies(*refs, starting=True)
    mine.start()
    for cp in first:
        cp.start()


def _gather_finish(*refs):
    mine, first, passed, arrivals, late = _gather_copies(*refs, starting=False)
    for arrived, onward in zip(arrivals, passed):
        arrived.wait_recv()
        onward.start()
    for cp in late:
        cp.wait_recv()
    for cp in first + passed:
        cp.wait_send()
    mine.wait()


def _scatter_copies(c_ref, out_ref, send_sems, recv_sems, local_sem):
    x, y, cc = _position()
    me = 4 * x + 2 * y + cc
    mine = pltpu.make_async_copy(c_ref.at[me], out_ref.at[me], local_sem)
    copies = []
    for k, (fx, fy, fc) in enumerate(_RELATIONS):
        px = 1 - x if fx else x
        py = 1 - y if fy else y
        pc = 1 - cc if fc else cc
        copies.append(pltpu.make_async_remote_copy(
            src_ref=c_ref.at[4 * px + 2 * py + pc], dst_ref=out_ref.at[me],
            send_sem=send_sems.at[k], recv_sem=recv_sems.at[k], device_id=(px, py, pc), device_id_type=MESH))
    return mine, copies


def _scatter_start(*refs):
    mine, copies = _scatter_copies(*refs)
    mine.start()
    for cp in copies:
        cp.start()


def _scatter_finish(*refs):
    mine, copies = _scatter_copies(*refs)
    for cp in copies:
        cp.wait()
    mine.wait()


_EXCHANGES = {"gather": (_gather_start, _gather_finish), "scatter": (_scatter_start, _scatter_finish)}


def _exchange_scratch():
    return [pltpu.SemaphoreType.DMA((7,)), pltpu.SemaphoreType.DMA((7,)), pltpu.SemaphoreType.DMA]


def _exchange_out(kind, src):
    return jax.ShapeDtypeStruct(((N_DEV,) + src.shape) if kind == "gather" else src.shape, src.dtype)


def _exchange(jobs, name):
    n = len(jobs)

    def body(*refs):
        srcs, outs, sems = refs[:n], refs[n:2 * n], refs[2 * n:]
        for half in (0, 1):
            for i, (kind, _) in enumerate(jobs):
                _EXCHANGES[kind][half](srcs[i], outs[i], *sems[3 * i:3 * i + 3])

    return pl.pallas_call(
        body, name=name, out_shape=[_exchange_out(kind, src) for kind, src in jobs],
        in_specs=[ANY] * n, out_specs=[ANY] * n,
        scratch_shapes=[sem for _ in jobs for sem in _exchange_scratch()])(*[src for _, src in jobs])


def _call(body, name, grid, in_specs, out_specs, out_shape, scratch, sem, args, rider=None):
    if rider is None:
        return pl.pallas_call(body, name=name, grid=grid, in_specs=in_specs, out_specs=out_specs, out_shape=out_shape,
                              scratch_shapes=scratch, compiler_params=_params(sem))(*args)
    jobs = rider if isinstance(rider, list) else [rider]
    ni, no, ns, nj = len(in_specs), len(out_specs), len(scratch), len(jobs)

    def carried(*refs):
        ins, rsrcs = refs[:ni], refs[ni:ni + nj]
        outs, routs = refs[ni + nj:ni + nj + no], refs[ni + nj + no:ni + 2 * nj + no]
        scr, sems = refs[ni + 2 * nj + no:ni + 2 * nj + no + ns], refs[ni + 2 * nj + no + ns:]
        ids = [pl.program_id(a) for a in range(len(grid))]
        is_first = functools.reduce(jnp.logical_and, [i == 0 for i in ids])
        is_last = functools.reduce(jnp.logical_and, [i == g - 1 for i, g in zip(ids, grid)])

        def half(which):
            for j, (kind, _) in enumerate(jobs):
                _EXCHANGES[kind][which](rsrcs[j], routs[j], *sems[3 * j:3 * j + 3])

        @pl.when(is_first)
        def _():
            half(0)

        body(*ins, *outs, *scr)

        @pl.when(is_last)
        def _():
            half(1)

    return pl.pallas_call(
        carried, name=name, grid=grid, in_specs=list(in_specs) + [ANY] * nj, out_specs=list(out_specs) + [ANY] * nj,
        out_shape=list(out_shape) + [_exchange_out(kind, src) for kind, src in jobs],
        scratch_shapes=list(scratch) + [sem for _ in jobs for sem in _exchange_scratch()],
        compiler_params=_params(("arbitrary",) * len(grid)))(*args, *[src for _, src in jobs])


def _inproj(x, g, wt, tm, tn=512, rider=None):
    s, d = x.shape
    n = wt.shape[0]

    def body(x_ref, g_ref, w_ref, z_ref, h_ref, hs):
        @pl.when(pl.program_id(1) == 0)
        def _():
            xv = x_ref[...]
            r = lax.rsqrt(jnp.mean(xv * xv, axis=-1, keepdims=True) + EPS)
            hv = (xv * r * g_ref[...]).astype(BF16)
            hs[...] = hv
            h_ref[...] = hv
        z_ref[...] = _dot(hs[...], w_ref[...], NT)

    return _call(
        body, "inproj", (s // tm, n // tn),
        [pl.BlockSpec((tm, d), lambda i, j: (i, 0)),
         pl.BlockSpec((1, d), lambda i, j: (0, 0)),
         pl.BlockSpec((tn, d), lambda i, j: (j, 0))],
        [pl.BlockSpec((tm, tn), lambda i, j: (i, j)), pl.BlockSpec((tm, d), lambda i, j: (i, 0))],
        [jax.ShapeDtypeStruct((s, n), F32), jax.ShapeDtypeStruct((s, d), BF16)],
        [pltpu.VMEM((tm, d), BF16)], ("parallel", "arbitrary"), (x, g, wt), rider)


def _relayout_plan():
    runs = ((0, 3584, 0), (3584, 3616, GA * LANES), (3616, 4640, C_V * LANES), (4640, 5152, M_Q * LANES),
            (5152, 5408, M_KV * LANES), (5408, 5440, M_KR * LANES), (5440, 5472, M_KR * LANES + 64),
            (5472, 5984, M_G * LANES))
    shard = IN_COLS // N_DEV
    plan = []
    for d in range(N_DEV):
        lo, hi = shard * d, shard * (d + 1)
        for a, b, p in runs:
            s, e = max(a, lo), min(b, hi)
            if s < e:
                plan.append((d, s - lo, p + (s - a), e - s))
    return plan


def _assemble_w_in(g, tc=512):
    _, c, r = g.shape
    tc = min(tc, r)

    def body(g_ref, o_ref):
        o_ref[...] = jnp.zeros_like(o_ref)
        for d, at, to, w in _relayout_plan():
            o_ref[to:to + w, :] = g_ref[d, at:at + w, :]

    return pl.pallas_call(
        body, name="assemble_w_in", grid=(r // tc,),
        in_specs=[pl.BlockSpec((N_DEV, c, tc), lambda i: (0, 0, i))],
        out_specs=pl.BlockSpec((ZP_COLS, tc), lambda i: (0, i)),
        out_shape=jax.ShapeDtypeStruct((ZP_COLS, r), g.dtype),
        compiler_params=_params(("parallel",)),
    )(g)


def _split_w_in(wt, tc=512):
    r = wt.shape[1]
    c = IN_COLS // N_DEV
    tc = min(tc, r)

    def body(w_ref, o_ref):
        for d, at, to, w in _relayout_plan():
            o_ref[d, at:at + w, :] = w_ref[to:to + w, :]

    return pl.pallas_call(
        body, name="split_w_in", grid=(r // tc,),
        in_specs=[pl.BlockSpec((ZP_COLS, tc), lambda i: (0, i))],
        out_specs=pl.BlockSpec((N_DEV, c, tc), lambda i: (0, 0, i)),
        out_shape=jax.ShapeDtypeStruct((N_DEV, c, r), wt.dtype),
        compiler_params=_params(("parallel",)),
    )(wt)


def _mm(a, b, mode, name, tm, tn, tk, add=None, out_dtype=F32, rider=None, tail=None):
    if mode == "tn":
        k, m = a.shape
    else:
        m, k = a.shape
    n = b.shape[0] if mode == "nt" else b.shape[1]
    tm, tn, tk = min(tm, m), min(tn, n), min(tk, k)
    nk = k // tk
    dims = {"nn": (((1,), (0,)), ((), ())), "nt": NT, "tn": TN}[mode]
    if tail is None:
        def plain(acc, i, extra_refs, out_refs):
            out_refs[0][...] = (acc + extra_refs[0][...] if extra_refs else acc).astype(out_dtype)
        tail = ([(add, "tile")] if add is not None else [], [(out_dtype, "tile")], plain)
    extra, outs, fn = tail
    spec = {"tile": pl.BlockSpec((tm, tn), lambda i, j, kk: (i, j)),
            "row": pl.BlockSpec((1, tn), lambda i, j, kk: (0, j)),
            "lanes": pl.BlockSpec((1, LANES), lambda i, j, kk: (0, 0))}
    shape = {"tile": (m, n), "row": (1, n), "lanes": (1, LANES)}
    ne, no = len(extra), len(outs)

    def body(*refs):
        a_ref, b_ref = refs[:2]
        extra_refs, out_refs, acc = refs[2:2 + ne], refs[2 + ne:2 + ne + no], refs[2 + ne + no]
        i, kk = pl.program_id(0), pl.program_id(2)

        @pl.when(kk == 0)
        def _():
            acc[...] = jnp.zeros_like(acc)

        acc[...] += _dot(a_ref[...].astype(BF16), b_ref[...].astype(BF16), dims)

        @pl.when(kk == nk - 1)
        def _():
            fn(acc[...], i, extra_refs, out_refs)

    a_spec = (pl.BlockSpec((tk, tm), lambda i, j, kk: (kk, i)) if mode == "tn"
              else pl.BlockSpec((tm, tk), lambda i, j, kk: (i, kk)))
    b_spec = (pl.BlockSpec((tn, tk), lambda i, j, kk: (j, kk)) if mode == "nt"
              else pl.BlockSpec((tk, tn), lambda i, j, kk: (kk, j)))
    summed = any(kind != "tile" for _, kind in outs)
    res = _call(body, name, (m // tm, n // tn, nk), [a_spec, b_spec] + [spec[kind] for _, kind in extra],
                [spec[kind] for _, kind in outs], [jax.ShapeDtypeStruct(shape[kind], dt) for dt, kind in outs],
                [pltpu.VMEM((tm, tn), F32)], ("arbitrary",) * 3 if summed else ("parallel", "parallel", "arbitrary"),
                [a, b] + [arr for arr, _ in extra], rider)
    return res[0] if (rider is None and no == 1) else res


def _norm_bwd_tail(x, g, dres):
    def fn(dh, i, extra_refs, out_refs):
        x_ref, g_ref, dres_ref = extra_refs
        dx_ref, dg_ref = out_refs

        @pl.when(i == 0)
        def _():
            dg_ref[...] = jnp.zeros_like(dg_ref)

        xv = x_ref[...]
        r = lax.rsqrt(jnp.mean(xv * xv, axis=-1, keepdims=True) + EPS)
        nv = xv * r
        dg_ref[...] += jnp.sum(dh * nv, axis=0, keepdims=True)
        u = dh * g_ref[...]
        dx_ref[...] = dres_ref[...] + r * (u - nv * jnp.mean(u * nv, axis=-1, keepdims=True))

    return [(x, "tile"), (g, "row"), (dres, "tile")], [(F32, "tile"), (F32, "row")], fn


def _loss_tail(x, target):
    d = x.shape[1]

    def fn(acc, i, extra_refs, out_refs):
        x_ref, t_ref = extra_refs
        dx_ref, loss_ref = out_refs

        @pl.when(i == 0)
        def _():
            loss_ref[...] = jnp.zeros_like(loss_ref)

        err = acc + x_ref[...] - t_ref[...]
        dx_ref[...] = err * (1.0 / d)
        per_tok = jnp.mean(err * err, axis=-1, keepdims=True)
        loss_ref[...] += 0.5 * jnp.sum(per_tok, axis=0, keepdims=True)

    return [(x, "tile"), (target, "tile")], [(F32, "tile"), (F32, "lanes")], fn


def _ret_core(q_ref, k_ref, v_ref, tab_ref, out_ref, back_ref, nchunk):
    c = RET_CHUNK

    def rows(n):
        return pl.ds(pl.multiple_of(n * c, c), c)

    zero = jnp.zeros((LANES, LANES), F32)

    def plane(i, n=c):
        return tab_ref[0:n, c + LANES * i:c + LANES * (i + 1)]

    def fwd(n, st):
        r = rows(n)
        q, k, vb = q_ref[r, :], k_ref[r, :], v_ref[r, :].astype(BF16)
        sc = _dot(q.astype(BF16), k.astype(BF16), NT) * tab_ref[:, 0:c]
        o = _dot(sc.astype(BF16), vb)
        o = o + _dot((q * plane(0)).astype(BF16), st.astype(BF16))
        out_ref[r, :] = o
        return st * plane(4, LANES) + _dot((k * plane(1)).astype(BF16), vb, TN)

    def bwd(i, st):
        r = rows(nchunk - 1 - i)
        q, k, vb = q_ref[r, :], k_ref[r, :], v_ref[r, :].astype(BF16)
        back_ref[r, :] = _dot((q * plane(2)).astype(BF16), st.astype(BF16))
        return st * plane(5, LANES) + _dot((k * plane(3)).astype(BF16), vb, TN)

    def both(i, states):
        return fwd(i, states[0]), bwd(i, states[1])

    _chunk_loop(nchunk, both, (zero, zero), RET_UNROLL)
    out_ref[...] += back_ref[...]


def _ret_fwd(z, cos_r, sin_r, tab, norm_g):
    s = z.shape[0]
    nchunk = s // RET_CHUNK
    scale = RET_HD ** -0.5
    col = lambda base: pl.BlockSpec((s, LANES), lambda h: (0, base + h), pipeline_mode=pl.Buffered(1))

    def body(q_ref, k_ref, v_ref, g_ref, cos_ref, sin_ref, tab_ref, ng_ref, o_ref, y_ref, qh, kh, back):
        qh[...] = _rope(q_ref[...], cos_ref[...], sin_ref[...])
        kh[...] = _rope(k_ref[...], cos_ref[...], sin_ref[...]) * scale
        _ret_core(qh, kh, v_ref, tab_ref, o_ref, back, nchunk)
        o = o_ref[...]
        r = lax.rsqrt(jnp.mean(o * o, axis=-1, keepdims=True) + EPS)
        y_ref[...] = (_silu(g_ref[...]) * (o * r * ng_ref[...])).astype(BF16)

    return pl.pallas_call(
        body, name="ret_fwd", grid=(4,),
        in_specs=[col(A_Q), col(A_K), col(A_V), col(A_G),
                  pl.BlockSpec((s, LANES), lambda h: (0, 0), pipeline_mode=pl.Buffered(1)),
                  pl.BlockSpec((s, LANES), lambda h: (0, 0), pipeline_mode=pl.Buffered(1)),
                  pl.BlockSpec((None, RET_CHUNK, RET_CHUNK + 6 * LANES), lambda h: (h, 0, 0)),
                  pl.BlockSpec((1, LANES), lambda h: (0, h))],
        out_specs=[pl.BlockSpec((s, LANES), lambda h: (0, h)), pl.BlockSpec((s, LANES), lambda h: (0, h))],
        out_shape=[jax.ShapeDtypeStruct((s, GROUP_W), F32), jax.ShapeDtypeStruct((s, GROUP_W), BF16)],
        scratch_shapes=[pltpu.VMEM((s, LANES), F32)] * 3,
        compiler_params=_params(("arbitrary",)),
    )(z, z, z, z, cos_r, sin_r, tab, norm_g)


def _ret_bwd(z, d_o, cos_r, sin_r, tab, tab_sw, rider=None):
    s = z.shape[0]
    nchunk = s // RET_CHUNK
    scale = RET_HD ** -0.5
    col = lambda base: pl.BlockSpec((s, LANES), lambda h: (0, base + h), pipeline_mode=pl.Buffered(1))
    whole = lambda: pl.BlockSpec((s, LANES), lambda h: (0, 0), pipeline_mode=pl.Buffered(1))
    tabspec = lambda: pl.BlockSpec((None, RET_CHUNK, RET_CHUNK + 6 * LANES), lambda h: (h, 0, 0))
    outspec = lambda: pl.BlockSpec((s, LANES), lambda h: (0, h))

    def body(q_ref, k_ref, v_ref, do_ref, cos_ref, sin_ref, tab_ref, tsw_ref, dq_ref, dk_ref, dv_ref,
             qh, kh, tmp, back):
        cos, sin = cos_ref[...], sin_ref[...]
        qh[...] = _rope(q_ref[...], cos, sin)
        kh[...] = _rope(k_ref[...], cos, sin) * scale
        _ret_core(kh, qh, do_ref, tsw_ref, tmp, back, nchunk)
        dv_ref[...] = tmp[...].astype(BF16)
        _ret_core(do_ref, v_ref, kh, tab_ref, tmp, back, nchunk)
        dq_ref[...] = _rope_t(tmp[...], cos, sin).astype(BF16)
        _ret_core(v_ref, do_ref, qh, tsw_ref, tmp, back, nchunk)
        dk_ref[...] = _rope_t(tmp[...] * scale, cos, sin).astype(BF16)

    return _call(
        body, "ret_bwd", (4,),
        [col(A_Q), col(A_K), col(A_V),
         pl.BlockSpec((s, LANES), lambda h: (0, h), pipeline_mode=pl.Buffered(1)),
         whole(), whole(), tabspec(), tabspec()],
        [outspec(), outspec(), outspec()],
        [jax.ShapeDtypeStruct((s, GROUP_W), BF16)] * 3,
        [pltpu.VMEM((s, LANES), F32)] * 4,
        ("arbitrary",), (z, z, z, d_o, cos_r, sin_r, tab, tab_sw), rider)


def _normgate_bwd(o, z, gate_blk, dy, dy_blk, norm_g, tm):
    s = o.shape[0]

    def body(o_ref, g_ref, dy_ref, ng_ref, do_ref, dg_ref, dng_ref):
        @pl.when(pl.program_id(0) == 0)
        def _():
            dng_ref[...] = jnp.zeros_like(dng_ref)

        for h in range(4):
            sl = slice(LANES * h, LANES * (h + 1))
            ov, gv, dyv, ng = o_ref[:, sl], g_ref[:, sl], dy_ref[:, sl], ng_ref[:, sl]
            r = lax.rsqrt(jnp.mean(ov * ov, axis=-1, keepdims=True) + EPS)
            on = ov * r
            dn = dyv * _silu(gv)
            u = dn * ng
            do_ref[:, sl] = r * (u - on * jnp.mean(u * on, axis=-1, keepdims=True))
            dg_ref[:, sl] = (dyv * (on * ng) * _silu_grad(gv)).astype(BF16)
            dng_ref[:, sl] += jnp.sum(dn * on, axis=0, keepdims=True)

    return pl.pallas_call(
        body, name="normgate_bwd", grid=(s // tm,),
        in_specs=[pl.BlockSpec((tm, GROUP_W), lambda i: (i, 0)),
                  pl.BlockSpec((tm, GROUP_W), lambda i: (i, gate_blk // 4)),
                  pl.BlockSpec((tm, GROUP_W), lambda i: (i, dy_blk)),
                  pl.BlockSpec((1, GROUP_W), lambda i: (0, 0))],
        out_specs=[pl.BlockSpec((tm, GROUP_W), lambda i: (i, 0)), pl.BlockSpec((tm, GROUP_W), lambda i: (i, 0)),
                   pl.BlockSpec((1, GROUP_W), lambda i: (0, 0))],
        out_shape=[jax.ShapeDtypeStruct((s, GROUP_W), F32), jax.ShapeDtypeStruct((s, GROUP_W), BF16),
                   jax.ShapeDtypeStruct((1, GROUP_W), F32)],
        compiler_params=_params(("arbitrary",)),
    )(o, z, dy, norm_g)


def _log_sigmoid(x):
    return jnp.minimum(x, 0.0) - jnp.log(1.0 + jnp.exp(-jnp.abs(x)))


def _gla_consts():
    c = GLA_CHUNK
    row = lax.broadcasted_iota(jnp.int32, (c, c), 0)
    colm = lax.broadcasted_iota(jnp.int32, (c, c), 1)
    lane = lax.broadcasted_iota(jnp.int32, (1, LANES), 1)
    low = row >= colm
    up = colm >= row
    heads = ((lane < GLA_DK).astype(F32), (lane >= GLA_DK).astype(F32))
    return low, up, heads


def _chunk_running_sum(x, suffix):
    rows = x.shape[0]
    pos = jnp.bitwise_and(lax.broadcasted_iota(jnp.int32, (rows, 1), 0), GLA_CHUNK - 1)
    k = 1
    while k < GLA_CHUNK:
        if suffix:
            x = x + jnp.where(pos < GLA_CHUNK - k, pltpu.roll(x, rows - k, 0), 0.0)
        else:
            x = x + jnp.where(pos >= k, pltpu.roll(x, k, 0), 0.0)
        k *= 2
    return x


def _gla_chunk(cum_ref, d, n):
    c = GLA_CHUNK
    cum = cum_ref[d, pl.ds(pl.multiple_of(n * c, c), c), :]
    last = cum_ref[d, pl.ds(n * c + (c - 1 if d == 0 else 0), 1), :]
    eq = jnp.exp(cum)
    ek = jnp.exp(-cum)
    el = jnp.exp(last - cum)
    dec = jnp.exp(last)
    return eq, ek, el, dec


def _gla_gates(ga_ref, wa_ref, ba_ref, cum_ref, s, suffix):
    rows = min(s, GLA_CUM_ROWS)

    def step(i, carry):
        r = pl.ds(pl.multiple_of(i * rows, rows), rows)
        pre = _dot(ga_ref[r, :].astype(BF16), wa_ref[...].astype(BF16)) + ba_ref[...]
        cum_ref[r, :] = _chunk_running_sum(_log_sigmoid(pre) * (1.0 / GLA_TAU), suffix)
        return carry
    lax.fori_loop(0, s // rows, step, 0)


def _gla_fwd(z, wa_f, wa_b, ba_f, ba_b, norm_g):
    s = z.shape[0]
    c = GLA_CHUNK
    nchunk = s // c
    scale = GLA_DK ** -0.5
    tm = min(s, 512)
    one = pl.Buffered(1)

    def body(q_ref, k_ref, v_ref, ga_ref, g_ref, waf_ref, wab_ref, baf_ref, bab_ref, ng_ref, o_ref, y_ref,
             la_s):
        low, up, heads = _gla_consts()
        _gla_gates(ga_ref, waf_ref, baf_ref, la_s.at[0], s, False)
        _gla_gates(ga_ref, wab_ref, bab_ref, la_s.at[1], s, True)
        for d in range(2):
            tri = (low, up)[d]

            def step(i, states):
                n = i if d == 0 else nchunk - 1 - i
                r = pl.ds(pl.multiple_of(n * c, c), c)
                q = q_ref[r, :] * scale
                k = k_ref[r, :]
                eq, ek, el, dec = _gla_chunk(la_s, d, n)
                qt = q * eq
                ktb = (k * ek).astype(BF16)
                kl = k * el
                new_states = []
                for hh in range(2):
                    cols = slice(LANES * hh, LANES * (hh + 1))
                    vb = v_ref[r, cols].astype(BF16)
                    qm = (qt * heads[hh]).astype(BF16)
                    a = jnp.where(tri, _dot(qm, ktb, NT), 0.0)
                    o = _dot(a.astype(BF16), vb) + _dot(qm, states[hh].astype(BF16), NT)
                    if d == 0:
                        o_ref[r, cols] = o
                    else:
                        o_ref[r, cols] += o
                    new_states.append(states[hh] * dec + _dot(vb, (kl * heads[hh]).astype(BF16), TN))
                return tuple(new_states)

            zero = jnp.zeros((LANES, LANES), F32)
            _chunk_loop(nchunk, step, (zero, zero), GLA_UNROLL)

        def epi(i, carry):
            r = pl.ds(pl.multiple_of(i * tm, tm), tm)
            for hh in range(2):
                cols = slice(LANES * hh, LANES * (hh + 1))
                o = o_ref[r, cols]
                rr = lax.rsqrt(jnp.mean(o * o, axis=-1, keepdims=True) + EPS)
                y_ref[r, cols] = (_silu(g_ref[r, cols]) * (o * rr * ng_ref[:, cols])).astype(BF16)
            return carry

        lax.fori_loop(0, s // tm, epi, 0)

    w2 = 2 * LANES
    return pl.pallas_call(
        body, name="gla_fwd", grid=(2,),
        in_specs=[pl.BlockSpec((s, LANES), lambda p: (0, B_Q + p), pipeline_mode=one),
                  pl.BlockSpec((s, LANES), lambda p: (0, B_K + p), pipeline_mode=one),
                  pl.BlockSpec((s, w2), lambda p: (0, B_V // 2 + p), pipeline_mode=one),
                  pl.BlockSpec((s, LANES), lambda p: (0, GA), pipeline_mode=one),
                  pl.BlockSpec((s, w2), lambda p: (0, B_G // 2 + p), pipeline_mode=one),
                  pl.BlockSpec((LANES, LANES), lambda p: (0, p)),
                  pl.BlockSpec((LANES, LANES), lambda p: (0, p)),
                  pl.BlockSpec((1, LANES), lambda p: (0, p)),
                  pl.BlockSpec((1, LANES), lambda p: (0, p)),
                  pl.BlockSpec((1, w2), lambda p: (0, p))],
        out_specs=[pl.BlockSpec((s, w2), lambda p: (0, p)), pl.BlockSpec((s, w2), lambda p: (0, p))],
        out_shape=[jax.ShapeDtypeStruct((s, GROUP_W), F32), jax.ShapeDtypeStruct((s, GROUP_W), BF16)],
        scratch_shapes=[pltpu.VMEM((2, s, LANES), F32)],
        compiler_params=_params(("arbitrary",)),
    )(z, z, z, z, z, wa_f, wa_b, ba_f, ba_b, norm_g)


def _gla_bwd(z, d_o, wa_f, wa_b, ba_f, ba_b, rider=None):
    s = z.shape[0]
    c = GLA_CHUNK
    nchunk = s // c
    scale = GLA_DK ** -0.5
    tm = min(s, GLA_CUM_ROWS)
    one = pl.Buffered(1)

    def body(q_ref, k_ref, v_ref, ga_ref, do_ref, waf_ref, wab_ref, baf_ref, bab_ref,
             dq_ref, dk_ref, dv_ref, dga_ref, dwaf_ref, dwab_ref, dbaf_ref, dbab_ref,
             la_s, dla_s, stash, dq_s, dk_s, dv_s):
        low, up, heads = _gla_consts()
        rowi = lax.broadcasted_iota(jnp.int32, (c, 1), 0)
        _gla_gates(ga_ref, waf_ref, baf_ref, la_s.at[0], s, False)
        _gla_gates(ga_ref, wab_ref, bab_ref, la_s.at[1], s, True)
        for d in range(2):
            tri = (low, up)[d]
            last_row = (rowi == (c - 1 if d == 0 else 0)).astype(F32)
            order = (lambda i: i) if d == 0 else (lambda i: nchunk - 1 - i)
            zero = jnp.zeros((LANES, LANES), F32)

            def states(i, sts):
                n = order(i)
                r = pl.ds(pl.multiple_of(n * c, c), c)
                k = k_ref[r, :]
                _, _, el, dec = _gla_chunk(la_s, d, n)
                kl = k * el
                new = []
                for hh in range(2):
                    cols = slice(LANES * hh, LANES * (hh + 1))
                    stash[hh, n] = sts[hh]
                    new.append(sts[hh] * dec + _dot(v_ref[r, cols].astype(BF16), (kl * heads[hh]).astype(BF16), TN))
                return tuple(new)

            _chunk_loop(nchunk, states, (zero, zero), GLA_UNROLL)

            def step(i, dsts):
                n = order(nchunk - 1 - i)
                r = pl.ds(pl.multiple_of(n * c, c), c)
                q = q_ref[r, :] * scale
                k = k_ref[r, :]
                eq, ek, el, dec = _gla_chunk(la_s, d, n)
                qt = q * eq
                kt = k * ek
                kl = k * el
                ktb = kt.astype(BF16)
                dqt = jnp.zeros((c, LANES), F32)
                dkt = jnp.zeros((c, LANES), F32)
                dkl = jnp.zeros((c, LANES), F32)
                ddec = jnp.zeros((1, LANES), F32)
                new = []
                for hh in range(2):
                    cols = slice(LANES * hh, LANES * (hh + 1))
                    vb = v_ref[r, cols].astype(BF16)
                    dob = do_ref[r, cols].astype(BF16)
                    qm = (qt * heads[hh]).astype(BF16)
                    a = jnp.where(tri, _dot(qm, ktb, NT), 0.0).astype(BF16)
                    da = jnp.where(tri, _dot(dob, vb, NT), 0.0).astype(BF16)
                    sn = stash[hh, n]
                    dst = dsts[hh]
                    dstb = dst.astype(BF16)
                    dqt = dqt + (_dot(da, ktb) + _dot(dob, sn.astype(BF16))) * heads[hh]
                    dkt = dkt + _dot(da, qm, TN)
                    dv = _dot(a, dob, TN) + _dot((kl * heads[hh]).astype(BF16), dstb, NT)
                    dkl = dkl + _dot(vb, dstb)
                    ddec = ddec + jnp.sum(dst * sn, axis=0, keepdims=True)
                    new.append(dst * dec + _dot(dob, qm, TN))
                    if d == 0:
                        dv_s[r, cols] = dv
                    else:
                        dv_ref[r, cols] = (dv_s[r, cols] + dv).astype(BF16)
                dlast = ddec * dec + jnp.sum(dkl * kl, axis=0, keepdims=True)
                dq = dqt * eq * scale
                dk = dkt * ek + dkl * el
                dcum = dqt * qt - dkt * kt - dkl * kl + last_row * dlast
                dla_s[d, r, :] = dcum
                if d == 0:
                    dq_s[r, :] = dq
                    dk_s[r, :] = dk
                else:
                    dq_ref[r, :] = (dq_s[r, :] + dq).astype(BF16)
                    dk_ref[r, :] = (dk_s[r, :] + dk).astype(BF16)
                return tuple(new)

            _chunk_loop(nchunk, step, (zero, zero), GLA_UNROLL)

        first = pl.program_id(0) == 0
        for d, (wa_ref, ba_ref, dwa_ref, dba_ref) in enumerate(
                ((waf_ref, baf_ref, dwaf_ref, dbaf_ref), (wab_ref, bab_ref, dwab_ref, dbab_ref))):
            dwa_ref[...] = jnp.zeros_like(dwa_ref)
            dba_ref[...] = jnp.zeros_like(dba_ref)

            def gates(i, carry):
                r = pl.ds(pl.multiple_of(i * tm, tm), tm)
                gab = ga_ref[r, :].astype(BF16)
                wab16 = wa_ref[...].astype(BF16)
                pre = _dot(gab, wab16) + ba_ref[...]
                dla = _chunk_running_sum(dla_s[d, r, :], suffix=(d == 0))
                dpre = dla * (1.0 / GLA_TAU) * _sigmoid(-pre)
                dpb = dpre.astype(BF16)
                dwa_ref[...] += _dot(gab, dpb, TN)
                dba_ref[...] += jnp.sum(dpre, axis=0, keepdims=True)
                dga = _dot(dpb, wab16, NT)
                if d == 0:
                    @pl.when(first)
                    def _():
                        dga_ref[r, :] = dga

                    @pl.when(jnp.logical_not(first))
                    def _():
                        dga_ref[r, :] += dga
                else:
                    dga_ref[r, :] += dga
                return carry

            lax.fori_loop(0, s // tm, gates, 0)

    w2 = 2 * LANES
    return _call(
        body, "gla_bwd", (2,),
        [pl.BlockSpec((s, LANES), lambda p: (0, B_Q + p), pipeline_mode=one),
         pl.BlockSpec((s, LANES), lambda p: (0, B_K + p), pipeline_mode=one),
         pl.BlockSpec((s, w2), lambda p: (0, B_V // 2 + p), pipeline_mode=one),
         pl.BlockSpec((s, LANES), lambda p: (0, GA), pipeline_mode=one),
         pl.BlockSpec((s, w2), lambda p: (0, p), pipeline_mode=one),
         pl.BlockSpec((LANES, LANES), lambda p: (0, p)),
         pl.BlockSpec((LANES, LANES), lambda p: (0, p)),
         pl.BlockSpec((1, LANES), lambda p: (0, p)),
         pl.BlockSpec((1, LANES), lambda p: (0, p))],
        [pl.BlockSpec((s, LANES), lambda p: (0, p), pipeline_mode=one),
         pl.BlockSpec((s, LANES), lambda p: (0, p), pipeline_mode=one),
         pl.BlockSpec((s, w2), lambda p: (0, p), pipeline_mode=one),
         pl.BlockSpec((s, LANES), lambda p: (0, 0), pipeline_mode=one),
         pl.BlockSpec((LANES, LANES), lambda p: (0, p)),
         pl.BlockSpec((LANES, LANES), lambda p: (0, p)),
         pl.BlockSpec((1, LANES), lambda p: (0, p)),
         pl.BlockSpec((1, LANES), lambda p: (0, p))],
        [jax.ShapeDtypeStruct((s, w2), BF16), jax.ShapeDtypeStruct((s, w2), BF16),
         jax.ShapeDtypeStruct((s, GROUP_W), BF16), jax.ShapeDtypeStruct((s, LANES), F32),
         jax.ShapeDtypeStruct((LANES, w2), F32), jax.ShapeDtypeStruct((LANES, w2), F32),
         jax.ShapeDtypeStruct((1, w2), F32), jax.ShapeDtypeStruct((1, w2), F32)],
        [pltpu.VMEM((2, s, LANES), F32), pltpu.VMEM((2, s, LANES), F32),
         pltpu.VMEM((2, nchunk, LANES, LANES), F32),
         pltpu.VMEM((s, LANES), F32), pltpu.VMEM((s, LANES), F32), pltpu.VMEM((s, w2), F32)],
        ("arbitrary",), (z, z, z, z, d_o, wa_f, wa_b, ba_f, ba_b), rider)


def _shift_rows(x, d, rowi):
    s = x.shape[0]
    if d == 0:
        return x
    y = pltpu.roll(x, d % s, 0)
    keep = (rowi >= d) if d > 0 else (rowi < s + d)
    return jnp.where(keep, y, 0.0)


def _run_sum(x, m, step, rowi):
    acc, n = x, 1
    while n < m:
        acc = acc + _shift_rows(acc, step * n, rowi)
        n *= 2
    return acc


def _pool_counts(s, w, rowi):
    hi = jnp.minimum(rowi + w // 2, s)
    lo = jnp.maximum(rowi - w // 2, 0)
    return (hi - lo).astype(F32)


def _pooled(u, w, rowi):
    s = u.shape[0]
    win = _shift_rows(_run_sum(u, w // 2, 1, rowi), 1, rowi) + _run_sum(u, w // 2, -1, rowi)
    return win / _pool_counts(s, w, rowi) - u


def _pool_fwd(z, pool_w, pool_scale):
    s = z.shape[0]
    one = pl.Buffered(1)

    def body(u_ref, g_ref, w_ref, sc_ref, y_ref):
        rowi = lax.broadcasted_iota(jnp.int32, (s, 1), 0)
        for g, w in enumerate(POOL_WINDOWS):
            cols = slice(LANES * g, LANES * (g + 1))
            pooled = _pooled(u_ref[:, cols], w, rowi)
            mixed = _dot(pooled.astype(BF16), w_ref[g].astype(BF16))
            y_ref[:, cols] = (_silu(g_ref[:, cols]) * (mixed * sc_ref[:, cols])).astype(BF16)

    return pl.pallas_call(
        body, name="pool_fwd", grid=(1,),
        in_specs=[pl.BlockSpec((s, GROUP_W), lambda i: (0, C_V // 4), pipeline_mode=one),
                  pl.BlockSpec((s, GROUP_W), lambda i: (0, C_G // 4), pipeline_mode=one),
                  pl.BlockSpec((4, LANES, LANES), lambda i: (0, 0, 0)),
                  pl.BlockSpec((1, GROUP_W), lambda i: (0, 0))],
        out_specs=pl.BlockSpec((s, GROUP_W), lambda i: (0, 0), pipeline_mode=one),
        out_shape=jax.ShapeDtypeStruct((s, GROUP_W), BF16),
        compiler_params=_params(("arbitrary",)),
    )(z, z, pool_w, pool_scale)


def _pool_bwd(z, dy, pool_w, pool_scale):
    s = z.shape[0]
    one = pl.Buffered(1)

    def body(u_ref, g_ref, dy_ref, w_ref, sc_ref, du_ref, dg_ref, dw_ref, dsc_ref):
        rowi = lax.broadcasted_iota(jnp.int32, (s, 1), 0)
        for g, w in enumerate(POOL_WINDOWS):
            cols = slice(LANES * g, LANES * (g + 1))
            gate, dyv, sc = g_ref[:, cols], dy_ref[:, cols], sc_ref[:, cols]
            wb = w_ref[g].astype(BF16)
            pooled = _pooled(u_ref[:, cols], w, rowi)
            pb = pooled.astype(BF16)
            mixed = _dot(pb, wb)
            dg_ref[:, cols] = (dyv * (mixed * sc) * _silu_grad(gate)).astype(BF16)
            dt = dyv * _silu(gate)
            dsc_ref[:, cols] = jnp.sum(dt * mixed, axis=0, keepdims=True)
            dmb = (dt * sc).astype(BF16)
            dw_ref[g] = _dot(pb, dmb, TN)
            dpool = _dot(dmb, wb, NT)
            e = dpool / _pool_counts(s, w, rowi)
            du_ref[:, cols] = (_run_sum(e, w // 2, 1, rowi) + _shift_rows(_run_sum(e, w // 2, -1, rowi), -1, rowi)
                               - dpool).astype(BF16)

    return pl.pallas_call(
        body, name="pool_bwd", grid=(1,),
        in_specs=[pl.BlockSpec((s, GROUP_W), lambda i: (0, C_V // 4), pipeline_mode=one),
                  pl.BlockSpec((s, GROUP_W), lambda i: (0, C_G // 4), pipeline_mode=one),
                  pl.BlockSpec((s, GROUP_W), lambda i: (0, 2), pipeline_mode=one),
                  pl.BlockSpec((4, LANES, LANES), lambda i: (0, 0, 0)),
                  pl.BlockSpec((1, GROUP_W), lambda i: (0, 0))],
        out_specs=[pl.BlockSpec((s, GROUP_W), lambda i: (0, 0), pipeline_mode=one),
                   pl.BlockSpec((s, GROUP_W), lambda i: (0, 0), pipeline_mode=one),
                   pl.BlockSpec((4, LANES, LANES), lambda i: (0, 0, 0)),
                   pl.BlockSpec((1, GROUP_W), lambda i: (0, 0))],
        out_shape=[jax.ShapeDtypeStruct((s, GROUP_W), BF16), jax.ShapeDtypeStruct((s, GROUP_W), BF16),
                   jax.ShapeDtypeStruct((4, LANES, LANES), F32), jax.ShapeDtypeStruct((1, GROUP_W), F32)],
        compiler_params=_params(("arbitrary",)),
    )(z, z, dy, pool_w, pool_scale)


def _mla_heads(qf, kv, kpe, qg, kg, cos, sin):
    out = []
    for h in range(4):
        qa = qf[:, LANES * h:LANES * (h + 1)]
        qb = qf[:, 512 + LANES * h:512 + LANES * (h + 1)]
        ka = kv[:, 256 * h:256 * h + LANES]
        rq = lax.rsqrt((jnp.sum(qa * qa, axis=-1, keepdims=True) + jnp.sum(qb * qb, axis=-1, keepdims=True))
                       * (1.0 / MLA_QK) + EPS)
        rk = lax.rsqrt((jnp.sum(ka * ka, axis=-1, keepdims=True) + jnp.sum(kpe * kpe, axis=-1, keepdims=True))
                       * (1.0 / MLA_QK) + EPS)
        out.append((qa, qb, rq, ka, rk))
    return out


def _mla_latents(mq_ref, mkv_ref, gq_ref, gkv_ref, wq_ref, wkv_ref):
    mq = mq_ref[...]
    rq = lax.rsqrt(jnp.mean(mq * mq, axis=-1, keepdims=True) + EPS)
    qn = mq * rq
    qnb = (qn * gq_ref[...]).astype(BF16)
    mkv = mkv_ref[...]
    rk = lax.rsqrt(jnp.mean(mkv * mkv, axis=-1, keepdims=True) + EPS)
    kvn = mkv * rk
    kvnb = (kvn * gkv_ref[...]).astype(BF16)
    qf = _dot(qnb, wq_ref[...])
    kv = _dot(kvnb, wkv_ref[...])
    return qn, rq, qnb, kvn, rk, kvnb, qf, kv


def _mla_prep(z, cos_m, sin_m, gq, wq, gkv, wkv, qg, kg, tm):
    s = z.shape[0]

    def body(mq_ref, mkv_ref, mkr_ref, cos_ref, sin_ref, gq_ref, wq_ref, gkv_ref, wkv_ref, qg_ref, kg_ref,
             q_ref, k_ref, v_ref):
        _, _, _, _, _, _, qf, kv = _mla_latents(mq_ref, mkv_ref, gq_ref, gkv_ref, wq_ref, wkv_ref)
        kpe = mkr_ref[...]
        cos, sin = cos_ref[...], sin_ref[...]
        qg, kg = qg_ref[...], kg_ref[...]
        for h, (qa, qb, rq, ka, rk) in enumerate(_mla_heads(qf, kv, kpe, qg, kg, cos, sin)):
            q_ref[h, :, 0:LANES] = (qa * rq * qg[:, 0:LANES] * ATTN_Q_SCALE).astype(BF16)
            q_ref[h, :, LANES:] = (_rope(qb * rq * qg[:, LANES:], cos, sin) * ATTN_Q_SCALE).astype(BF16)
            k_ref[h, :, 0:LANES] = (ka * rk * kg[:, 0:LANES]).astype(BF16)
            k_ref[h, :, LANES:] = _rope(kpe * rk * kg[:, LANES:], cos, sin).astype(BF16)
            v_ref[h] = kv[:, 256 * h + LANES:256 * (h + 1)].astype(BF16)

    full = lambda shape: pl.BlockSpec(shape, lambda i: (0,) * len(shape))
    return pl.pallas_call(
        body, name="mla_prep", grid=(s // tm,),
        in_specs=[pl.BlockSpec((tm, 512), lambda i: (i, M_Q // 4)),
                  pl.BlockSpec((tm, 256), lambda i: (i, M_KV // 2)),
                  pl.BlockSpec((tm, LANES), lambda i: (i, M_KR)),
                  pl.BlockSpec((tm, LANES), lambda i: (i, 0)),
                  pl.BlockSpec((tm, LANES), lambda i: (i, 0)),
                  full((1, 512)), full((512, 1024)), full((1, 256)), full((256, 1024)), full((1, 256)), full((1, 256))],
        out_specs=[pl.BlockSpec((4, tm, 256), lambda i: (0, i, 0)), pl.BlockSpec((4, tm, 256), lambda i: (0, i, 0)),
                   pl.BlockSpec((4, tm, LANES), lambda i: (0, i, 0))],
        out_shape=[jax.ShapeDtypeStruct((4, s, 256), BF16), jax.ShapeDtypeStruct((4, s, 256), BF16),
                   jax.ShapeDtypeStruct((4, s, LANES), BF16)],
        compiler_params=_params(("parallel",)),
    )(z, z, z, cos_m, sin_m, gq, wq, gkv, wkv, qg, kg)


def _mla_prep_bwd(z, cos_m, sin_m, gq, wq, gkv, wkv, qg, kg, dq, dk, dv, tm):
    s = z.shape[0]

    def body(mq_ref, mkv_ref, mkr_ref, cos_ref, sin_ref, gq_ref, wq_ref, gkv_ref, wkv_ref, qg_ref, kg_ref,
             dq_ref, dk_ref, dv_ref,
             dmq_ref, dmkv_ref, dmkr_ref, dwq_ref, dwkv_ref, dgq_ref, dgkv_ref, dqg_ref, dkg_ref, dqf, dkv):
        @pl.when(pl.program_id(0) == 0)
        def _():
            for r in (dwq_ref, dwkv_ref, dgq_ref, dgkv_ref, dqg_ref, dkg_ref):
                r[...] = jnp.zeros_like(r)

        qn, rq0, qnb, kvn, rk0, kvnb, qf, kv = _mla_latents(mq_ref, mkv_ref, gq_ref, gkv_ref, wq_ref, wkv_ref)
        kpe = mkr_ref[...]
        cos, sin = cos_ref[...], sin_ref[...]
        qg, kg = qg_ref[...], kg_ref[...]
        dkpe = jnp.zeros_like(kpe)
        inv = 1.0 / MLA_QK

        def norm_bwd(a, b, r, da_n, db_n, g):
            ga, gb = g[:, 0:LANES], g[:, LANES:]
            dg_a = jnp.sum(da_n * a * r, axis=0, keepdims=True)
            dg_b = jnp.sum(db_n * b * r, axis=0, keepdims=True)
            ua, ub = da_n * ga, db_n * gb
            dt = (jnp.sum(ua * a, axis=-1, keepdims=True) + jnp.sum(ub * b, axis=-1, keepdims=True)) * inv
            r3 = r * r * r
            return r * ua - a * (r3 * dt), r * ub - b * (r3 * dt), dg_a, dg_b

        for h, (qa, qb, rq, ka, rk) in enumerate(_mla_heads(qf, kv, kpe, qg, kg, cos, sin)):
            dqa, dqb, dga, dgb = norm_bwd(qa, qb, rq, dq_ref[h, :, 0:LANES] * ATTN_SCALE,
                                          _rope_t(dq_ref[h, :, LANES:] * ATTN_SCALE, cos, sin), qg)
            dqf[:, LANES * h:LANES * (h + 1)] = dqa
            dqf[:, 512 + LANES * h:512 + LANES * (h + 1)] = dqb
            dqg_ref[:, 0:LANES] += dga
            dqg_ref[:, LANES:] += dgb
            ln2 = math.log(2.0)
            dka, dkb, dga, dgb = norm_bwd(ka, kpe, rk, dk_ref[h, :, 0:LANES] * ln2,
                                          _rope_t(dk_ref[h, :, LANES:] * ln2, cos, sin), kg)
            dkv[:, 256 * h:256 * h + LANES] = dka
            dkv[:, 256 * h + LANES:256 * (h + 1)] = dv_ref[h]
            dkpe = dkpe + dkb
            dkg_ref[:, 0:LANES] += dga
            dkg_ref[:, LANES:] += dgb
        dmkr_ref[...] = dkpe.astype(BF16)

        def latent_bwd(dfull, w_ref, nb, n, r, g_ref, dw_ref, dg_ref, dlat_ref):
            db = dfull.astype(BF16)
            dn = _dot(db, w_ref[...], NT)
            dw_ref[...] += _dot(nb, db, TN)
            dg_ref[...] += jnp.sum(dn * n, axis=0, keepdims=True)
            u = dn * g_ref[...]
            dlat_ref[...] = (r * (u - n * jnp.mean(u * n, axis=-1, keepdims=True))).astype(BF16)

        latent_bwd(dqf[...], wq_ref, qnb, qn, rq0, gq_ref, dwq_ref, dgq_ref, dmq_ref)
        latent_bwd(dkv[...], wkv_ref, kvnb, kvn, rk0, gkv_ref, dwkv_ref, dgkv_ref, dmkv_ref)

    full = lambda shape: pl.BlockSpec(shape, lambda i: (0,) * len(shape))
    return pl.pallas_call(
        body, name="mla_prep_bwd", grid=(s // tm,),
        in_specs=[pl.BlockSpec((tm, 512), lambda i: (i, M_Q // 4)),
                  pl.BlockSpec((tm, 256), lambda i: (i, M_KV // 2)),
                  pl.BlockSpec((tm, LANES), lambda i: (i, M_KR)),
                  pl.BlockSpec((tm, LANES), lambda i: (i, 0)),
                  pl.BlockSpec((tm, LANES), lambda i: (i, 0)),
                  full((1, 512)), full((512, 1024)), full((1, 256)), full((256, 1024)), full((1, 256)), full((1, 256)),
                  pl.BlockSpec((4, tm, 256), lambda i: (0, i, 0)), pl.BlockSpec((4, tm, 256), lambda i: (0, i, 0)),
                  pl.BlockSpec((4, tm, LANES), lambda i: (0, i, 0))],
        out_specs=[pl.BlockSpec((tm, 512), lambda i: (i, 0)), pl.BlockSpec((tm, 256), lambda i: (i, 0)),
                   pl.BlockSpec((tm, LANES), lambda i: (i, 0)),
                   full((512, 1024)), full((256, 1024)), full((1, 512)), full((1, 256)), full((1, 256)), full((1, 256))],
        out_shape=[jax.ShapeDtypeStruct((s, 512), BF16), jax.ShapeDtypeStruct((s, 256), BF16),
                   jax.ShapeDtypeStruct((s, LANES), BF16),
                   jax.ShapeDtypeStruct((512, 1024), F32), jax.ShapeDtypeStruct((256, 1024), F32),
                   jax.ShapeDtypeStruct((1, 512), F32), jax.ShapeDtypeStruct((1, 256), F32),
                   jax.ShapeDtypeStruct((1, 256), F32), jax.ShapeDtypeStruct((1, 256), F32)],
        scratch_shapes=[pltpu.VMEM((tm, 1024), F32), pltpu.VMEM((tm, 1024), F32)],
        compiler_params=_params(("arbitrary",)),
    )(z, z, z, cos_m, sin_m, gq, wq, gkv, wkv, qg, kg, dq, dk, dv)


def _attn_fwd(q, k, v, z, tq, rider=None):
    s = q.shape[1]

    def body(q_ref, k_ref, v_ref, g_ref, o_ref, y_ref, lse_ref):
        sc = _dot(q_ref[...], k_ref[...], NT)
        m = jnp.max(sc, axis=-1, keepdims=True)
        p = jnp.exp2(sc - m)
        l = jnp.sum(p, axis=-1, keepdims=True)
        o = _dot(p.astype(BF16), v_ref[...]) / l
        o_ref[...] = o
        y_ref[...] = (_silu(g_ref[...]) * o).astype(BF16)
        lse_ref[...] = m + jnp.log2(l)

    return _call(
        body, "attn_fwd", (4, s // tq),
        [pl.BlockSpec((None, tq, 256), lambda h, i: (h, i, 0)),
         pl.BlockSpec((None, s, 256), lambda h, i: (h, 0, 0)),
         pl.BlockSpec((None, s, LANES), lambda h, i: (h, 0, 0)),
         pl.BlockSpec((tq, LANES), lambda h, i: (i, M_G + h))],
        [pl.BlockSpec((tq, LANES), lambda h, i: (i, h)), pl.BlockSpec((tq, LANES), lambda h, i: (i, h)),
         pl.BlockSpec((None, tq, 1), lambda h, i: (h, i, 0))],
        [jax.ShapeDtypeStruct((s, GROUP_W), F32), jax.ShapeDtypeStruct((s, GROUP_W), BF16),
         jax.ShapeDtypeStruct((4, s, 1), F32)],
        [], ("parallel", "parallel"), (q, k, v, z), rider)


def _attn_bwd(q, k, v, z, o, lse, dy, tq, rider=None):
    s = q.shape[1]

    def body(q_ref, k_ref, v_ref, g_ref, o_ref, lse_ref, dy_ref, dq_ref, dk_ref, dv_ref, dg_ref):
        @pl.when(pl.program_id(1) == 0)
        def _():
            dk_ref[...] = jnp.zeros_like(dk_ref)
            dv_ref[...] = jnp.zeros_like(dv_ref)

        gate, ov, dyv = g_ref[...], o_ref[...], dy_ref[...]
        do = dyv * _silu(gate)
        dg_ref[...] = (dyv * ov * _silu_grad(gate)).astype(BF16)
        delta = jnp.sum(do * ov, axis=-1, keepdims=True)
        dob = do.astype(BF16)
        qb, kb = q_ref[...], k_ref[...]
        p = jnp.exp2(_dot(qb, kb, NT) - lse_ref[...])
        dp = _dot(dob, v_ref[...], NT)
        ds = (p * (dp - delta)).astype(BF16)
        dq_ref[...] = _dot(ds, kb)
        dk_ref[...] += _dot(ds, qb, TN)
        dv_ref[...] += _dot(p.astype(BF16), dob, TN)

    return _call(
        body, "attn_bwd", (4, s // tq),
        [pl.BlockSpec((None, tq, 256), lambda h, i: (h, i, 0)),
         pl.BlockSpec((None, s, 256), lambda h, i: (h, 0, 0)),
         pl.BlockSpec((None, s, LANES), lambda h, i: (h, 0, 0)),
         pl.BlockSpec((tq, LANES), lambda h, i: (i, M_G + h)),
         pl.BlockSpec((tq, LANES), lambda h, i: (i, h)),
         pl.BlockSpec((None, tq, 1), lambda h, i: (h, i, 0)),
         pl.BlockSpec((tq, LANES), lambda h, i: (i, 12 + h))],
        [pl.BlockSpec((None, tq, 256), lambda h, i: (h, i, 0)),
         pl.BlockSpec((None, s, 256), lambda h, i: (h, 0, 0)),
         pl.BlockSpec((None, s, LANES), lambda h, i: (h, 0, 0)),
         pl.BlockSpec((tq, LANES), lambda h, i: (i, h))],
        [jax.ShapeDtypeStruct((4, s, 256), F32), jax.ShapeDtypeStruct((4, s, 256), F32),
         jax.ShapeDtypeStruct((4, s, LANES), F32), jax.ShapeDtypeStruct((s, GROUP_W), BF16)],
        [], ("parallel", "arbitrary"), (q, k, v, z, o, lse, dy), rider)


def _adam(parts, w, m, v, name, tr):
    r, c = w.shape
    tr = min(tr, r)
    c1 = 1.0 - ADAM_B1 ** ADAM_STEP
    c2 = 1.0 - ADAM_B2 ** ADAM_STEP

    def body(p_ref, w_ref, m_ref, v_ref, g_ref, d_ref, nm_ref, nv_ref):
        g = p_ref[0].astype(F32)
        for i in range(1, N_DEV):
            g = g + p_ref[i].astype(F32)
        nm = ADAM_B1 * m_ref[...] + (1.0 - ADAM_B1) * g
        nv = ADAM_B2 * v_ref[...] + (1.0 - ADAM_B2) * (g * g)
        g_ref[...] = g
        nm_ref[...] = nm
        nv_ref[...] = nv
        d_ref[...] = -ADAM_LR * ((nm / c1) / (jnp.sqrt(nv / c2) + ADAM_EPS) + ADAM_WD * w_ref[...])

    blk = lambda: pl.BlockSpec((tr, c), lambda i: (i, 0))
    return pl.pallas_call(
        body, name=name, grid=(r // tr,),
        in_specs=[pl.BlockSpec((N_DEV, tr, c), lambda i: (0, i, 0)), blk(), blk(), blk()],
        out_specs=[blk(), blk(), blk(), blk()],
        out_shape=[jax.ShapeDtypeStruct((r, c), F32)] * 4,
        compiler_params=_params(("parallel",)),
    )(parts, w, m, v)


def _adam_columns(parts, w, m, v, name, tc, rider=None):
    nl, r, c = w.shape
    pieces = [p for layer in parts for p in layer]
    nh = len(parts[0])
    rp = pieces[0].shape[2]
    tc = min(tc, rp)
    ncb = rp // tc
    c1 = 1.0 - ADAM_B1 ** ADAM_STEP
    c2 = 1.0 - ADAM_B2 ** ADAM_STEP

    def body(*refs):
        p_refs, (w_ref, m_ref, v_ref, g_ref, d_ref, nm_ref, nv_ref) = refs[:len(pieces)], refs[len(pieces):]
        for h in range(nh):
            @pl.when(pl.program_id(0) == h)
            def _(h=h):
                for l in range(nl):
                    p_ref = p_refs[l * nh + h]
                    g = p_ref[0].astype(F32)
                    for i in range(1, N_DEV):
                        g = g + p_ref[i].astype(F32)
                    nm = ADAM_B1 * m_ref[:, l, :] + (1.0 - ADAM_B1) * g
                    nv = ADAM_B2 * v_ref[:, l, :] + (1.0 - ADAM_B2) * (g * g)
                    g_ref[:, l, :] = g
                    nm_ref[:, l, :] = nm
                    nv_ref[:, l, :] = nv
                    d_ref[:, l, :] = -ADAM_LR * ((nm / c1) / (jnp.sqrt(nv / c2) + ADAM_EPS) + ADAM_WD * w_ref[:, l, :])

    def part_spec(j):
        return pl.BlockSpec((N_DEV, c, tc), lambda h, i: (0, 0, jnp.clip((h - j % nh) * ncb + i, 0, ncb - 1)))

    blk = lambda: pl.BlockSpec((c, nl, tc), lambda h, i: (0, 0, h * ncb + i))
    t = lambda a: jnp.transpose(a, (2, 0, 1))
    *res, = _call(body, name, (nh, ncb), [part_spec(j) for j in range(len(pieces))] + [blk(), blk(), blk()],
                  [blk(), blk(), blk(), blk()], [jax.ShapeDtypeStruct((c, nl, r), F32)] * 4, [],
                  ("arbitrary",) * 2, (*pieces, t(w), t(m), t(v)), rider)
    return [jnp.transpose(a, (1, 2, 0)) for a in res[:4]] + res[4:]


def _adam_layers(parts, w, m, v, name, tr, rider=None):
    nl, r, c = w.shape
    pieces = [p for layer in parts for p in layer]
    rp = pieces[0].shape[1]
    tr = min(tr, rp)
    nr, nrp = r // tr, rp // tr
    c1 = 1.0 - ADAM_B1 ** ADAM_STEP
    c2 = 1.0 - ADAM_B2 ** ADAM_STEP

    def body(*refs):
        p_refs, (w_ref, m_ref, v_ref, g_ref, d_ref, nm_ref, nv_ref) = refs[:len(pieces)], refs[len(pieces):]
        at = pl.program_id(0) * nr + pl.program_id(1)
        for j in range(len(pieces)):
            @pl.when(jnp.logical_and(at >= j * nrp, at < (j + 1) * nrp))
            def _(p_ref=p_refs[j]):
                g = p_ref[0].astype(F32)
                for i in range(1, N_DEV):
                    g = g + p_ref[i].astype(F32)
                nm = ADAM_B1 * m_ref[...] + (1.0 - ADAM_B1) * g
                nv = ADAM_B2 * v_ref[...] + (1.0 - ADAM_B2) * (g * g)
                g_ref[...] = g
                nm_ref[...] = nm
                nv_ref[...] = nv
                d_ref[...] = -ADAM_LR * ((nm / c1) / (jnp.sqrt(nv / c2) + ADAM_EPS) + ADAM_WD * w_ref[...])

    def part_spec(j):
        return pl.BlockSpec((N_DEV, tr, c), lambda ll, i: (0, jnp.clip(ll * nr + i - j * nrp, 0, nrp - 1), 0))

    blk = lambda: pl.BlockSpec((None, tr, c), lambda ll, i: (ll, i, 0))
    return _call(body, name, (nl, nr), [part_spec(j) for j in range(len(pieces))] + [blk(), blk(), blk()],
                 [blk(), blk(), blk(), blk()], [jax.ShapeDtypeStruct((nl, r, c), F32)] * 4, [],
                 ("arbitrary", "arbitrary"), (*pieces, w, m, v), rider)


REPLICATED = ("norm_g", "ret_norm_g", "gla_ba_f", "gla_ba_b", "gla_norm_g", "pool_w", "pool_scale",
              "mla_q_norm_g", "mla_kv_norm_g", "mla_qk_norm_q", "mla_qk_norm_k")
REPLICATED_EARLY = REPLICATED[1:]
SMALL_SHARDED = ("mla_wq_b", "mla_wkv_b", "gla_wa2_f", "gla_wa2_b")
WEIGHTS = ("norm_g", "w_in", "ret_norm_g", "gla_wa2_f", "gla_ba_f", "gla_wa2_b", "gla_ba_b", "gla_norm_g", "pool_w",
           "pool_scale", "mla_q_norm_g", "mla_wq_b", "mla_kv_norm_g", "mla_wkv_b", "mla_qk_norm_q", "mla_qk_norm_k",
           "w_out")


PACK_ROWS = 16


def _packed_rows(a):
    rows = a.size // LANES
    return rows, -(-rows // PACK_ROWS) * PACK_ROWS


def _pack(arrays, dtype):
    parts = []
    for a in arrays:
        rows, padded = _packed_rows(a)
        parts.append(jnp.pad(a.reshape(rows, LANES).astype(dtype), ((0, padded - rows), (0, 0))))
    return jnp.concatenate(parts, axis=0)


def _unpack(packed, like):
    out, at = [], 0
    for a in like:
        rows, padded = _packed_rows(a)
        out.append(packed[..., at:at + rows, :].reshape(packed.shape[:-2] + a.shape))
        at += padded
    return out


def _columns_by_device(g):
    l, r, n = g.shape
    return g.reshape(l, r, N_DEV, n // N_DEV).transpose(2, 0, 1, 3)


def _gathered_columns(g, l, r, c):
    return g.reshape(N_DEV, l, r, c).transpose(1, 2, 0, 3).reshape(l, r, N_DEV * c)


def _layer_forward(x, wts, late_wts, tables, tm, tq, ride_inproj=None, ride_attn=None, target=None):
    cos_r, sin_r, cos_m, sin_m, tab, _ = tables
    z, h, *carried_in = _inproj(x, wts["norm_g"], wts["w_in"], min(x.shape[0], 2 * tm), rider=ride_inproj)
    wts.update(late_wts("inproj", carried_in))
    o_a, y_a = _ret_fwd(z, cos_r, sin_r, tab, wts["ret_norm_g"])
    o_b, y_b = _gla_fwd(z, wts["wa_f"], wts["wa_b"], wts["gla_ba_f"], wts["gla_ba_b"], wts["gla_norm_g"])
    y_c = _pool_fwd(z, wts["pool_w"], wts["pool_scale"])
    q, k, v = _mla_prep(z, cos_m, sin_m, wts["mla_q_norm_g"], wts["wq"], wts["mla_kv_norm_g"], wts["wkv"],
                        wts["qk_q"], wts["qk_k"], tm)
    o_d, y_d, lse, *carried_attn = _attn_fwd(q, k, v, z, tq, rider=ride_attn)
    wts.update(late_wts("attn", carried_attn))
    y = jnp.concatenate([y_a, y_b, y_c, y_d], axis=1)
    w_out = wts["w_out"]
    if target is None:
        x_next = _mm(y, w_out, "nn", "outproj", tm, D_MODEL, 1024, add=x)
    else:
        x_next = _mm(y, w_out, "nn", "outproj_loss", tm, D_MODEL, 1024, tail=_loss_tail(x, target))
    saved = dict(x=x, z=z, h=h, o_a=o_a, o_b=o_b, o_d=o_d, lse=lse, q=q, k=k, v=v, y=y, w_out=w_out)
    return x_next, saved, carried_in, carried_attn


def _layer_backward(dx, sv, wts, tables, tm, tq, rides):
    cos_r, sin_r, cos_m, sin_m, tab, tab_sw = tables
    z = sv["z"]
    g = {}
    carried = {}

    def rider(name):
        return rides[name](g) if name in rides else None

    def landed(name, results, n_own):
        if name in rides:
            carried[name] = list(results[n_own:])
        return results[:n_own]

    g["w_out"] = _mm(sv["y"], dx, "tn", "d_w_out", 2048, 1024, 1024, out_dtype=BF16)
    dy = _mm(dx, sv["w_out"], "nt", "d_y", tm, 2048, 1024)

    do_a, dg_a, g["ret_norm_g"] = _normgate_bwd(sv["o_a"], z, A_G, dy, 0, wts["ret_norm_g"], tm)
    dq_a, dk_a, dv_a = landed("ret", _ret_bwd(z, do_a, cos_r, sin_r, tab, tab_sw, rider=rider("ret")), 3)

    do_b, dg_b, g["gla_norm_g"] = _normgate_bwd(sv["o_b"], z, B_G, dy, 1, wts["gla_norm_g"], tm)
    dq_b, dk_b, dv_b, d_ga, d_waf, d_wab, g["gla_ba_f"], g["gla_ba_b"] = landed("gla", _gla_bwd(
        z, do_b, wts["wa_f"], wts["wa_b"], wts["gla_ba_f"], wts["gla_ba_b"], rider=rider("gla")), 8)
    g["gla_wa2_f"] = d_waf[0:GLA_RANK]
    g["gla_wa2_b"] = d_wab[GLA_RANK:2 * GLA_RANK]

    du_c, dg_c, g["pool_w"], g["pool_scale"] = _pool_bwd(z, dy, wts["pool_w"], wts["pool_scale"])

    d_q, d_k, d_v, dg_d = landed("attn", _attn_bwd(sv["q"], sv["k"], sv["v"], z, sv["o_d"], sv["lse"], dy, tq,
                                                   rider=rider("attn")), 4)
    (d_mq, d_mkv, d_mkr, d_wq, g["mla_wkv_b"], g["mla_q_norm_g"], g["mla_kv_norm_g"], d_qg, d_kg) = _mla_prep_bwd(
        z, cos_m, sin_m, wts["mla_q_norm_g"], wts["wq"], wts["mla_kv_norm_g"], wts["wkv"], wts["qk_q"], wts["qk_k"],
        d_q, d_k, d_v, tm)
    g["mla_wq_b"] = _unpad_wq(d_wq)
    g["mla_qk_norm_q"] = d_qg[:, _QK_INV]
    g["mla_qk_norm_k"] = d_kg[:, _QK_INV]

    dz = jnp.concatenate([dq_a, dk_a, dv_a, dg_a, dq_b, dk_b, dv_b, dg_b, d_mq, du_c, dg_c, dg_d, d_mkv,
                          d_ga.astype(BF16), d_mkr], axis=1)
    h, half = sv["h"], D_MODEL // 2
    for name, cols in (("d_w_in_a", h[:, :half]), ("d_w_in_b", h[:, half:])):
        res = _mm(dz, cols, "tn", name, 2048, 1024, 1024, out_dtype=BF16, rider=rider(name))
        (d_wt,) = landed(name, res if name in rides else [res], 1)
        g["w_in" + name[-2:]] = _split_w_in(d_wt)
    dx_in, g["norm_g"] = landed("d_h", _mm(dz, wts["w_in"], "nn", "d_h", tm, D_MODEL, 1024, rider=rider("d_h"),
                                           tail=_norm_bwd_tail(sv["x"], wts["norm_g"], dx)), 2)
    return dx_in, g, carried


def kernel(x, norm_g, w_in, ret_norm_g, gla_wa2_f, gla_ba_f, gla_wa2_b, gla_ba_b, gla_norm_g, pool_w, pool_scale, mla_q_norm_g, mla_wq_b, mla_kv_norm_g, mla_wkv_b, mla_qk_norm_q, mla_qk_norm_k, w_out, loss_target, m_norm_g, m_w_in, m_ret_norm_g, m_gla_wa2_f, m_gla_ba_f, m_gla_wa2_b, m_gla_ba_b, m_gla_norm_g, m_pool_w, m_pool_scale, m_mla_q_norm_g, m_mla_wq_b, m_mla_kv_norm_g, m_mla_wkv_b, m_mla_qk_norm_q, m_mla_qk_norm_k, m_w_out, v_norm_g, v_w_in, v_ret_norm_g, v_gla_wa2_f, v_gla_ba_f, v_gla_wa2_b, v_gla_ba_b, v_gla_norm_g, v_pool_w, v_pool_scale, v_mla_q_norm_g, v_mla_wq_b, v_mla_kv_norm_g, v_mla_wkv_b, v_mla_qk_norm_q, v_mla_qk_norm_k, v_w_out):
    w = dict(norm_g=norm_g, w_in=w_in, ret_norm_g=ret_norm_g, gla_wa2_f=gla_wa2_f, gla_ba_f=gla_ba_f,
             gla_wa2_b=gla_wa2_b, gla_ba_b=gla_ba_b, gla_norm_g=gla_norm_g, pool_w=pool_w, pool_scale=pool_scale,
             mla_q_norm_g=mla_q_norm_g, mla_wq_b=mla_wq_b, mla_kv_norm_g=mla_kv_norm_g, mla_wkv_b=mla_wkv_b,
             mla_qk_norm_q=mla_qk_norm_q, mla_qk_norm_k=mla_qk_norm_k, w_out=w_out)
    m = dict(norm_g=m_norm_g, w_in=m_w_in, ret_norm_g=m_ret_norm_g, gla_wa2_f=m_gla_wa2_f, gla_ba_f=m_gla_ba_f,
             gla_wa2_b=m_gla_wa2_b, gla_ba_b=m_gla_ba_b, gla_norm_g=m_gla_norm_g, pool_w=m_pool_w,
             pool_scale=m_pool_scale, mla_q_norm_g=m_mla_q_norm_g, mla_wq_b=m_mla_wq_b, mla_kv_norm_g=m_mla_kv_norm_g,
             mla_wkv_b=m_mla_wkv_b, mla_qk_norm_q=m_mla_qk_norm_q, mla_qk_norm_k=m_mla_qk_norm_k, w_out=m_w_out)
    v = dict(norm_g=v_norm_g, w_in=v_w_in, ret_norm_g=v_ret_norm_g, gla_wa2_f=v_gla_wa2_f, gla_ba_f=v_gla_ba_f,
             gla_wa2_b=v_gla_wa2_b, gla_ba_b=v_gla_ba_b, gla_norm_g=v_gla_norm_g, pool_w=v_pool_w,
             pool_scale=v_pool_scale, mla_q_norm_g=v_mla_q_norm_g, mla_wq_b=v_mla_wq_b, mla_kv_norm_g=v_mla_kv_norm_g,
             mla_wkv_b=v_mla_wkv_b, mla_qk_norm_q=v_mla_qk_norm_q, mla_qk_norm_k=v_mla_qk_norm_k, w_out=v_w_out)
    xs, target = x[0], loss_target[0]
    s = xs.shape[0]
    tm, tq = min(s, 512), min(s, 256)
    c_in = w_in.shape[2]

    w_in_b = jnp.transpose(w_in, (2, 0, 1)).astype(BF16)
    w_out_b = w_out.astype(BF16).reshape(-1, D_MODEL)
    (w_in_g0,) = _exchange([("gather", w_in_b[:, 0])], "gather_first")
    tables = _rope_tables(s) + _ret_tables()

    def early_weights(l, w_in_g):
        return dict(
            norm_g=norm_g[l][None], w_in=_assemble_w_in(w_in_g), ret_norm_g=ret_norm_g[l][None],
            gla_ba_f=gla_ba_f[l][None], gla_ba_b=gla_ba_b[l][None],
            gla_norm_g=gla_norm_g[l][None], pool_w=pool_w[l], pool_scale=pool_scale[l][None],
            mla_q_norm_g=mla_q_norm_g[l][None], mla_kv_norm_g=mla_kv_norm_g[l][None],
            qk_q=_pad_qk_gain(mla_qk_norm_q[l]), qk_k=_pad_qk_gain(mla_qk_norm_k[l]))

    def small_weights(l, small_g):
        shards = _unpack(small_g, [w[n] for n in SMALL_SHARDED])
        full = {n: _gathered_columns(shards[i], *w[n].shape)[l] for i, n in enumerate(SMALL_SHARDED)}
        wa_f = jnp.zeros((LANES, 2 * LANES), BF16).at[0:GLA_RANK].set(full["gla_wa2_f"])
        wa_b = jnp.zeros((LANES, 2 * LANES), BF16).at[GLA_RANK:2 * GLA_RANK].set(full["gla_wa2_b"])
        return dict(wa_f=wa_f, wa_b=wa_b, wq=_pad_wq(full["mla_wq_b"]), wkv=full["mla_wkv_b"])

    def w_out_layer(l, w_out_g):
        return dict(w_out=w_out_g.reshape(N_DEV, DEPTH, -1, D_MODEL)[:, l].reshape(-1, D_MODEL))

    by_owner = lambda g_w_out: g_w_out.reshape(N_DEV, -1, D_MODEL)

    layers = [early_weights(0, w_in_g0), None]
    x1, sv0, (w_in_g1, small_g), (w_out_g,) = _layer_forward(
        xs, layers[0], lambda carrier, got: small_weights(0, got[1]) if carrier == "inproj" else w_out_layer(0, got[0]),
        tables, tm, tq,
        ride_inproj=[("gather", w_in_b[:, 1]), ("gather", _pack([w[n] for n in SMALL_SHARDED], BF16))],
        ride_attn=("gather", w_out_b))
    layers[1] = early_weights(1, w_in_g1)
    (dx, loss_row), sv1, _, _ = _layer_forward(
        x1, layers[1], lambda carrier, got: {**small_weights(1, small_g), **w_out_layer(1, w_out_g)}, tables, tm, tq,
        target=target)
    loss = lax.psum(loss_row[0, 0], ("x", "y", "c"))

    def small_jobs(g):
        grads = (g, g1)
        full = {n: jnp.stack([grads[l][n].reshape(w[n].shape[1:]) if n in REPLICATED else grads[l][n]
                              for l in range(DEPTH)]) for n in SMALL_SHARDED + REPLICATED_EARLY}
        small_c = jax.vmap(lambda *shards: _pack(shards, F32))(*[_columns_by_device(full[n]) for n in SMALL_SHARDED])
        return [("scatter", small_c),
                ("gather", _pack([full[n] for n in REPLICATED_EARLY], F32))]

    dx, g1, got1 = _layer_backward(dx, sv1, layers[1], tables, tm, tq, {
        "attn": lambda g: ("scatter", by_owner(g["w_out"]))})
    dx, g0, got0 = _layer_backward(dx, sv0, layers[0], tables, tm, tq, {
        "gla": lambda g: ("scatter", g1["w_in_b"]),
        "attn": lambda g: [("scatter", g1["w_in_a"]), ("scatter", by_owner(g["w_out"]))],
        "d_w_in_a": small_jobs,
        "d_w_in_b": lambda g: ("scatter", g["w_in_a"]),
        "d_h": lambda g: ("scatter", g["w_in_b"])})
    in_parts = ((got0["d_w_in_b"][0], got0["d_h"][0]), (got0["attn"][0], got0["gla"][0]))
    out_parts = ((got0["attn"][1],), (got1["attn"][0],))
    small_parts, rep_parts = got0["d_w_in_a"]
    norm_pack = _pack([jnp.stack([g0["norm_g"][0], g1["norm_g"][0]])], F32)

    out = {}
    out["w_in"] = _adam_columns(in_parts, w_in, m_w_in, v_w_in, "adam_w_in", 256)
    *out["w_out"], norm_parts = _adam_layers(out_parts, w_out, m_w_out, v_w_out, "adam_w_out", 128,
                                             rider=("gather", norm_pack))
    for names, parts, label in ((SMALL_SHARDED, small_parts, "adam_small"),
                                (REPLICATED_EARLY, rep_parts, "adam_replicated"), (("norm_g",), norm_parts, "adam_norm")):
        res = _adam(parts, _pack([w[n] for n in names], F32), _pack([m[n] for n in names], F32),
                    _pack([v[n] for n in names], F32), label, 2048)
        for n, *vals in zip(names, *[_unpack(a, [w[n] for n in names]) for a in res]):
            out[n] = vals

    return (loss, dx[None], *[out[n][0] for n in WEIGHTS], *[out[n][1] for n in WEIGHTS],
            *[out[n][2] for n in WEIGHTS], *[out[n][3] for n in WEIGHTS])
```

```python
import functools
import math

import numpy as np
import jax
import jax.numpy as jnp
from jax import lax
from jax.experimental import pallas as pl
from jax.experimental.pallas import tpu as pltpu

F32 = jnp.float32
BF16 = jnp.bfloat16

N_DEV = 8
D_MODEL = 2048
DEPTH = 2
GROUP_W = 512
EPS = 1e-6
ROPE_THETA = 10000.0
LANES = 128

RET_HD = 128
RET_CHUNK = 256
RET_UNROLL = 16
GLA_CHUNK = 64
GLA_UNROLL = 32
GLA_CUM_ROWS = 256
GLA_DK = 64
GLA_TAU = 16.0
GLA_RANK = 16
POOL_WINDOWS = (2, 4, 8, 16)
MLA_QK = 192
MLA_ROPE = 64
ATTN_SCALE = MLA_QK ** -0.5
ATTN_Q_SCALE = ATTN_SCALE * math.log2(math.e)
IN_COLS = 5984

ADAM_LR = 0.001
ADAM_B1 = 0.9
ADAM_B2 = 0.999
ADAM_EPS = 1e-08
ADAM_WD = 0.01
ADAM_STEP = 10

A_Q, A_K, A_V, A_G = 0, 4, 8, 12
B_Q, B_K, B_V, B_G = 16, 18, 20, 24
M_Q, C_V, C_G, M_G = 28, 32, 36, 40
M_KV, GA, M_KR = 44, 46, 47
ZP_COLS = 48 * LANES

VMEM_LIMIT = 56 * 1024 * 1024


def _params(sem, vmem=VMEM_LIMIT):
    return pltpu.CompilerParams(dimension_semantics=sem, vmem_limit_bytes=vmem)


def _sigmoid(x):
    return 1.0 / (1.0 + jnp.exp(-x))


def _silu(x):
    return x * _sigmoid(x)


def _silu_grad(x):
    s = _sigmoid(x)
    return s * (1.0 + x * (1.0 - s))


def _dot(a, b, dims=(((1,), (0,)), ((), ()))):
    return lax.dot_general(a, b, dims, preferred_element_type=F32)


NT = (((1,), (1,)), ((), ()))
TN = (((0,), (0,)), ((), ()))


def _chunk_loop(n, body, init, unroll):
    unroll = math.gcd(n, unroll)

    def trip(t, carry):
        for u in range(unroll):
            carry = body(t * unroll + u, carry)
        return carry

    return lax.fori_loop(0, n // unroll, trip, init)


def _roll_lanes_half(x):
    return pltpu.roll(x, 64, 1)


def _wq_perm():
    idx = np.zeros((1024,), np.int32)
    ok = np.zeros((1024,), bool)
    for h in range(4):
        idx[128 * h:128 * h + 128] = 192 * h + np.arange(128)
        ok[128 * h:128 * h + 128] = True
        base = 512 + 128 * h
        idx[base:base + 32] = 192 * h + 128 + np.arange(32)
        ok[base:base + 32] = True
        idx[base + 64:base + 96] = 192 * h + 160 + np.arange(32)
        ok[base + 64:base + 96] = True
    inv = np.zeros((768,), np.int32)
    inv[idx[ok]] = np.nonzero(ok)[0]
    return idx, ok, inv


_WQ_IDX, _WQ_OK, _WQ_INV = _wq_perm()


def _pad_wq(wq):
    return jnp.where(jnp.asarray(_WQ_OK)[None, :], wq[:, _WQ_IDX], 0).astype(wq.dtype)


def _unpad_wq(wqp):
    return wqp[:, _WQ_INV]


def _qk_idx():
    idx = np.zeros((256,), np.int32)
    ok = np.zeros((256,), bool)
    idx[0:128] = np.arange(128)
    ok[0:128] = True
    idx[128:160] = 128 + np.arange(32)
    ok[128:160] = True
    idx[192:224] = 160 + np.arange(32)
    ok[192:224] = True
    inv = np.zeros((192,), np.int32)
    inv[idx[ok]] = np.nonzero(ok)[0]
    return idx, ok, inv


_QK_IDX, _QK_OK, _QK_INV = _qk_idx()


def _pad_qk_gain(g):
    return jnp.where(jnp.asarray(_QK_OK), g[_QK_IDX], 0.0).reshape(1, 256)


def _rope_tables(s):
    def tabs(dim):
        inv = 1.0 / (ROPE_THETA ** (jnp.arange(0, dim, 2, dtype=F32) / dim))
        ang = jnp.arange(s, dtype=F32)[:, None] * inv[None, :]
        return jnp.cos(ang), jnp.sin(ang)
    cr, sr = tabs(RET_HD)
    cos_r = jnp.concatenate([cr, cr], axis=1)
    sin_r = jnp.concatenate([-sr, sr], axis=1)
    cm, sm = tabs(MLA_ROPE)
    zz = jnp.zeros_like(cm)
    cos_m = jnp.concatenate([cm, zz, cm, zz], axis=1)
    sin_m = jnp.concatenate([-sm, zz, sm, zz], axis=1)
    return cos_r, sin_r, cos_m, sin_m


def _rope(x, cos, sin):
    return x * cos + _roll_lanes_half(x) * sin


def _rope_t(x, cos, sin):
    return x * cos + _roll_lanes_half(x * sin)


def _ret_tables():
    c = RET_CHUNK
    gamma_f = 1.0 - 2.0 ** (-5.0 - jnp.arange(4, dtype=F32))
    gamma_b = gamma_f[::-1]
    idx = jnp.arange(c, dtype=F32)
    diff = idx[:, None] - idx[None, :]

    def build(g1, g2):
        l1 = jnp.log(g1)[:, None, None]
        l2 = jnp.log(g2)[:, None, None]
        d1 = jnp.where(diff >= 0, jnp.exp(jnp.maximum(diff, 0.0)[None] * l1), 0.0)
        d2 = jnp.where(diff <= 0, jnp.exp(jnp.maximum(-diff, 0.0)[None] * l2), 0.0)
        ones = jnp.ones((1, c, LANES), F32)
        col = idx[None, :, None]
        qdf = jnp.exp((col + 1.0) * l1) * ones
        kdf = jnp.exp((c - 1.0 - col) * l1) * ones
        qdb = jnp.exp((c - col) * l2) * ones
        kdb = jnp.exp(col * l2) * ones
        cd1 = jnp.exp(c * l1) * ones
        cd2 = jnp.exp(c * l2) * ones
        return jnp.concatenate([d1 + d2, qdf, kdf, qdb, kdb, cd1, cd2], axis=2)

    return build(gamma_f, gamma_b), build(gamma_b, gamma_f)


MESH = pl.DeviceIdType.MESH
ANY = pl.BlockSpec(memory_space=pl.ANY)
_RELATIONS = ((0, 0, 1), (1, 0, 0), (0, 1, 0), (1, 1, 0), (1, 0, 1), (0, 1, 1), (1, 1, 1))


def _position():
    return lax.axis_index("x"), lax.axis_index("y"), lax.axis_index("c")


def _gather_copies(x_ref, out_ref, send_sems, recv_sems, local_sem, starting):
    x, y, cc = _position()
    me, sibling = (x, y, cc), (x, y, 1 - cc)
    chips = [(1 - x, y), (x, 1 - y), (1 - x, 1 - y)]

    def slab(px, py, pc):
        return out_ref.at[4 * px + 2 * py + pc]

    def copy(k, block, to, src=None):
        return pltpu.make_async_remote_copy(
            src_ref=slab(*block) if src is None else src, dst_ref=slab(*block),
            send_sem=send_sems.at[k], recv_sem=recv_sems.at[k], device_id=to, device_id_type=MESH)

    mine = pltpu.make_async_copy(x_ref, slab(*me), local_sem)
    first = [copy(0, me, sibling, src=x_ref)] + [copy(1 + j, me, (*chip, cc), src=x_ref) for j, chip in enumerate(chips)]
    if starting:
        return mine, first
    passed = [copy(4 + j, (*chip, cc), sibling) for j, chip in enumerate(chips)]
    arrivals = [copy(1 + j, (*chip, cc), me) for j, chip in enumerate(chips)]
    late = [copy(0, sibling, me)] + [copy(4 + j, (*chip, 1 - cc), me) for j, chip in enumerate(chips)]
    return mine, first, passed, arrivals, late


def _gather_start(*refs):
    mine, first = _gather_copies(*refs, starting=True)
    mine.start()
    for cp in first:
        cp.start()


def _gather_finish(*refs):
    mine, first, passed, arrivals, late = _gather_copies(*refs, starting=False)
    for arrived, onward in zip(arrivals, passed):
        arrived.wait_recv()
        onward.start()
    for cp in late:
        cp.wait_recv()
    for cp in first + passed:
        cp.wait_send()
    mine.wait()


def _scatter_copies(c_ref, out_ref, send_sems, recv_sems, local_sem):
    x, y, cc = _position()
    me = 4 * x + 2 * y + cc
    mine = pltpu.make_async_copy(c_ref.at[me], out_ref.at[me], local_sem)
    copies = []
    for k, (fx, fy, fc) in enumerate(_RELATIONS):
        px = 1 - x if fx else x
        py = 1 - y if fy else y
        pc = 1 - cc if fc else cc
        copies.append(pltpu.make_async_remote_copy(
            src_ref=c_ref.at[4 * px + 2 * py + pc], dst_ref=out_ref.at[me],
            send_sem=send_sems.at[k], recv_sem=recv_sems.at[k], device_id=(px, py, pc), device_id_type=MESH))
    return mine, copies


def _scatter_start(*refs):
    mine, copies = _scatter_copies(*refs)
    mine.start()
    for cp in copies:
        cp.start()


def _scatter_finish(*refs):
    mine, copies = _scatter_copies(*refs)
    for cp in copies:
        cp.wait()
    mine.wait()


_EXCHANGES = {"gather": (_gather_start, _gather_finish), "scatter": (_scatter_start, _scatter_finish)}


def _exchange_scratch():
    return [pltpu.SemaphoreType.DMA((7,)), pltpu.SemaphoreType.DMA((7,)), pltpu.SemaphoreType.DMA]


def _exchange_out(kind, src):
    return jax.ShapeDtypeStruct(((N_DEV,) + src.shape) if kind == "gather" else src.shape, src.dtype)


def _exchange(jobs, name):
    n = len(jobs)

    def body(*refs):
        srcs, outs, sems = refs[:n], refs[n:2 * n], refs[2 * n:]
        for half in (0, 1):
            for i, (kind, _) in enumerate(jobs):
                _EXCHANGES[kind][half](srcs[i], outs[i], *sems[3 * i:3 * i + 3])

    return pl.pallas_call(
        body, name=name, out_shape=[_exchange_out(kind, src) for kind, src in jobs],
        in_specs=[ANY] * n, out_specs=[ANY] * n,
        scratch_shapes=[sem for _ in jobs for sem in _exchange_scratch()])(*[src for _, src in jobs])


def _call(body, name, grid, in_specs, out_specs, out_shape, scratch, sem, args, rider=None):
    if rider is None:
        return pl.pallas_call(body, name=name, grid=grid, in_specs=in_specs, out_specs=out_specs, out_shape=out_shape,
                              scratch_shapes=scratch, compiler_params=_params(sem))(*args)
    jobs = rider if isinstance(rider, list) else [rider]
    ni, no, ns, nj = len(in_specs), len(out_specs), len(scratch), len(jobs)

    def carried(*refs):
        ins, rsrcs = refs[:ni], refs[ni:ni + nj]
        outs, routs = refs[ni + nj:ni + nj + no], refs[ni + nj + no:ni + 2 * nj + no]
        scr, sems = refs[ni + 2 * nj + no:ni + 2 * nj + no + ns], refs[ni + 2 * nj + no + ns:]
        ids = [pl.program_id(a) for a in range(len(grid))]
        is_first = functools.reduce(jnp.logical_and, [i == 0 for i in ids])
        is_last = functools.reduce(jnp.logical_and, [i == g - 1 for i, g in zip(ids, grid)])

        def half(which):
            for j, (kind, _) in enumerate(jobs):
                _EXCHANGES[kind][which](rsrcs[j], routs[j], *sems[3 * j:3 * j + 3])

        @pl.when(is_first)
        def _():
            half(0)

        body(*ins, *outs, *scr)

        @pl.when(is_last)
        def _():
            half(1)

    return pl.pallas_call(
        carried, name=name, grid=grid, in_specs=list(in_specs) + [ANY] * nj, out_specs=list(out_specs) + [ANY] * nj,
        out_shape=list(out_shape) + [_exchange_out(kind, src) for kind, src in jobs],
        scratch_shapes=list(scratch) + [sem for _ in jobs for sem in _exchange_scratch()],
        compiler_params=_params(("arbitrary",) * len(grid)))(*args, *[src for _, src in jobs])


def _inproj(x, g, wt, tm, tn=512, rider=None):
    s, d = x.shape
    n = wt.shape[0]

    def body(x_ref, g_ref, w_ref, z_ref, h_ref, hs):
        @pl.when(pl.program_id(1) == 0)
        def _():
            xv = x_ref[...]
            r = lax.rsqrt(jnp.mean(xv * xv, axis=-1, keepdims=True) + EPS)
            hv = (xv * r * g_ref[...]).astype(BF16)
            hs[...] = hv
            h_ref[...] = hv
        z_ref[...] = _dot(hs[...], w_ref[...], NT)

    return _call(
        body, "inproj", (s // tm, n // tn),
        [pl.BlockSpec((tm, d), lambda i, j: (i, 0)),
         pl.BlockSpec((1, d), lambda i, j: (0, 0)),
         pl.BlockSpec((tn, d), lambda i, j: (j, 0))],
        [pl.BlockSpec((tm, tn), lambda i, j: (i, j)), pl.BlockSpec((tm, d), lambda i, j: (i, 0))],
        [jax.ShapeDtypeStruct((s, n), F32), jax.ShapeDtypeStruct((s, d), BF16)],
        [pltpu.VMEM((tm, d), BF16)], ("parallel", "arbitrary"), (x, g, wt), rider)


def _relayout_plan():
    runs = ((0, 3584, 0), (3584, 3616, GA * LANES), (3616, 4640, C_V * LANES), (4640, 5152, M_Q * LANES),
            (5152, 5408, M_KV * LANES), (5408, 5440, M_KR * LANES), (5440, 5472, M_KR * LANES + 64),
            (5472, 5984, M_G * LANES))
    shard = IN_COLS // N_DEV
    plan = []
    for d in range(N_DEV):
        lo, hi = shard * d, shard * (d + 1)
        for a, b, p in runs:
            s, e = max(a, lo), min(b, hi)
            if s < e:
                plan.append((d, s - lo, p + (s - a), e - s))
    return plan


def _assemble_w_in(g, tc=512):
    _, c, r = g.shape
    tc = min(tc, r)

    def body(g_ref, o_ref):
        o_ref[...] = jnp.zeros_like(o_ref)
        for d, at, to, w in _relayout_plan():
            o_ref[to:to + w, :] = g_ref[d, at:at + w, :]

    return pl.pallas_call(
        body, name="assemble_w_in", grid=(r // tc,),
        in_specs=[pl.BlockSpec((N_DEV, c, tc), lambda i: (0, 0, i))],
        out_specs=pl.BlockSpec((ZP_COLS, tc), lambda i: (0, i)),
        out_shape=jax.ShapeDtypeStruct((ZP_COLS, r), g.dtype),
        compiler_params=_params(("parallel",)),
    )(g)


def _split_w_in(wt, tc=512):
    r = wt.shape[1]
    c = IN_COLS // N_DEV
    tc = min(tc, r)

    def body(w_ref, o_ref):
        for d, at, to, w in _relayout_plan():
            o_ref[d, at:at + w, :] = w_ref[to:to + w, :]

    return pl.pallas_call(
        body, name="split_w_in", grid=(r // tc,),
        in_specs=[pl.BlockSpec((ZP_COLS, tc), lambda i: (0, i))],
        out_specs=pl.BlockSpec((N_DEV, c, tc), lambda i: (0, 0, i)),
        out_shape=jax.ShapeDtypeStruct((N_DEV, c, r), wt.dtype),
        compiler_params=_params(("parallel",)),
    )(wt)


def _mm(a, b, mode, name, tm, tn, tk, add=None, out_dtype=F32, rider=None, tail=None):
    if mode == "tn":
        k, m = a.shape
    else:
        m, k = a.shape
    n = b.shape[0] if mode == "nt" else b.shape[1]
    tm, tn, tk = min(tm, m), min(tn, n), min(tk, k)
    nk = k // tk
    dims = {"nn": (((1,), (0,)), ((), ())), "nt": NT, "tn": TN}[mode]
    if tail is None:
        def plain(acc, i, extra_refs, out_refs):
            out_refs[0][...] = (acc + extra_refs[0][...] if extra_refs else acc).astype(out_dtype)
        tail = ([(add, "tile")] if add is not None else [], [(out_dtype, "tile")], plain)
    extra, outs, fn = tail
    spec = {"tile": pl.BlockSpec((tm, tn), lambda i, j, kk: (i, j)),
            "row": pl.BlockSpec((1, tn), lambda i, j, kk: (0, j)),
            "lanes": pl.BlockSpec((1, LANES), lambda i, j, kk: (0, 0))}
    shape = {"tile": (m, n), "row": (1, n), "lanes": (1, LANES)}
    ne, no = len(extra), len(outs)

    def body(*refs):
        a_ref, b_ref = refs[:2]
        extra_refs, out_refs, acc = refs[2:2 + ne], refs[2 + ne:2 + ne + no], refs[2 + ne + no]
        i, kk = pl.program_id(0), pl.program_id(2)

        @pl.when(kk == 0)
        def _():
            acc[...] = jnp.zeros_like(acc)

        acc[...] += _dot(a_ref[...].astype(BF16), b_ref[...].astype(BF16), dims)

        @pl.when(kk == nk - 1)
        def _():
            fn(acc[...], i, extra_refs, out_refs)

    a_spec = (pl.BlockSpec((tk, tm), lambda i, j, kk: (kk, i)) if mode == "tn"
              else pl.BlockSpec((tm, tk), lambda i, j, kk: (i, kk)))
    b_spec = (pl.BlockSpec((tn, tk), lambda i, j, kk: (j, kk)) if mode == "nt"
              else pl.BlockSpec((tk, tn), lambda i, j, kk: (kk, j)))
    summed = any(kind != "tile" for _, kind in outs)
    res = _call(body, name, (m // tm, n // tn, nk), [a_spec, b_spec] + [spec[kind] for _, kind in extra],
                [spec[kind] for _, kind in outs], [jax.ShapeDtypeStruct(shape[kind], dt) for dt, kind in outs],
                [pltpu.VMEM((tm, tn), F32)], ("arbitrary",) * 3 if summed else ("parallel", "parallel", "arbitrary"),
                [a, b] + [arr for arr, _ in extra], rider)
    return res[0] if (rider is None and no == 1) else res


def _norm_bwd_tail(x, g, dres):
    def fn(dh, i, extra_refs, out_refs):
        x_ref, g_ref, dres_ref = extra_refs
        dx_ref, dg_ref = out_refs

        @pl.when(i == 0)
        def _():
            dg_ref[...] = jnp.zeros_like(dg_ref)

        xv = x_ref[...]
        r = lax.rsqrt(jnp.mean(xv * xv, axis=-1, keepdims=True) + EPS)
        nv = xv * r
        dg_ref[...] += jnp.sum(dh * nv, axis=0, keepdims=True)
        u = dh * g_ref[...]
        dx_ref[...] = dres_ref[...] + r * (u - nv * jnp.mean(u * nv, axis=-1, keepdims=True))

    return [(x, "tile"), (g, "row"), (dres, "tile")], [(F32, "tile"), (F32, "row")], fn


def _loss_tail(x, target):
    d = x.shape[1]

    def fn(acc, i, extra_refs, out_refs):
        x_ref, t_ref = extra_refs
        dx_ref, loss_ref = out_refs

        @pl.when(i == 0)
        def _():
            loss_ref[...] = jnp.zeros_like(loss_ref)

        err = acc + x_ref[...] - t_ref[...]
        dx_ref[...] = err * (1.0 / d)
        per_tok = jnp.mean(err * err, axis=-1, keepdims=True)
        loss_ref[...] += 0.5 * jnp.sum(per_tok, axis=0, keepdims=True)

    return [(x, "tile"), (target, "tile")], [(F32, "tile"), (F32, "lanes")], fn


def _ret_core(q_ref, k_ref, v_ref, tab_ref, out_ref, back_ref, nchunk):
    c = RET_CHUNK

    def rows(n):
        return pl.ds(pl.multiple_of(n * c, c), c)

    zero = jnp.zeros((LANES, LANES), F32)

    def plane(i, n=c):
        return tab_ref[0:n, c + LANES * i:c + LANES * (i + 1)]

    def fwd(n, st):
        r = rows(n)
        q, k, vb = q_ref[r, :], k_ref[r, :], v_ref[r, :].astype(BF16)
        sc = _dot(q.astype(BF16), k.astype(BF16), NT) * tab_ref[:, 0:c]
        o = _dot(sc.astype(BF16), vb)
        o = o + _dot((q * plane(0)).astype(BF16), st.astype(BF16))
        out_ref[r, :] = o
        return st * plane(4, LANES) + _dot((k * plane(1)).astype(BF16), vb, TN)

    def bwd(i, st):
        r = rows(nchunk - 1 - i)
        q, k, vb = q_ref[r, :], k_ref[r, :], v_ref[r, :].astype(BF16)
        back_ref[r, :] = _dot((q * plane(2)).astype(BF16), st.astype(BF16))
        return st * plane(5, LANES) + _dot((k * plane(3)).astype(BF16), vb, TN)

    def both(i, states):
        return fwd(i, states[0]), bwd(i, states[1])

    _chunk_loop(nchunk, both, (zero, zero), RET_UNROLL)
    out_ref[...] += back_ref[...]


def _ret_fwd(z, cos_r, sin_r, tab, norm_g):
    s = z.shape[0]
    nchunk = s // RET_CHUNK
    scale = RET_HD ** -0.5
    col = lambda base: pl.BlockSpec((s, LANES), lambda h: (0, base + h), pipeline_mode=pl.Buffered(1))

    def body(q_ref, k_ref, v_ref, g_ref, cos_ref, sin_ref, tab_ref, ng_ref, o_ref, y_ref, qh, kh, back):
        qh[...] = _rope(q_ref[...], cos_ref[...], sin_ref[...])
        kh[...] = _rope(k_ref[...], cos_ref[...], sin_ref[...]) * scale
        _ret_core(qh, kh, v_ref, tab_ref, o_ref, back, nchunk)
        o = o_ref[...]
        r = lax.rsqrt(jnp.mean(o * o, axis=-1, keepdims=True) + EPS)
        y_ref[...] = (_silu(g_ref[...]) * (o * r * ng_ref[...])).astype(BF16)

    return pl.pallas_call(
        body, name="ret_fwd", grid=(4,),
        in_specs=[col(A_Q), col(A_K), col(A_V), col(A_G),
                  pl.BlockSpec((s, LANES), lambda h: (0, 0), pipeline_mode=pl.Buffered(1)),
                  pl.BlockSpec((s, LANES), lambda h: (0, 0), pipeline_mode=pl.Buffered(1)),
                  pl.BlockSpec((None, RET_CHUNK, RET_CHUNK + 6 * LANES), lambda h: (h, 0, 0)),
                  pl.BlockSpec((1, LANES), lambda h: (0, h))],
        out_specs=[pl.BlockSpec((s, LANES), lambda h: (0, h)), pl.BlockSpec((s, LANES), lambda h: (0, h))],
        out_shape=[jax.ShapeDtypeStruct((s, GROUP_W), F32), jax.ShapeDtypeStruct((s, GROUP_W), BF16)],
        scratch_shapes=[pltpu.VMEM((s, LANES), F32)] * 3,
        compiler_params=_params(("arbitrary",)),
    )(z, z, z, z, cos_r, sin_r, tab, norm_g)


def _ret_bwd(z, d_o, cos_r, sin_r, tab, tab_sw, rider=None):
    s = z.shape[0]
    nchunk = s // RET_CHUNK
    scale = RET_HD ** -0.5
    col = lambda base: pl.BlockSpec((s, LANES), lambda h: (0, base + h), pipeline_mode=pl.Buffered(1))
    whole = lambda: pl.BlockSpec((s, LANES), lambda h: (0, 0), pipeline_mode=pl.Buffered(1))
    tabspec = lambda: pl.BlockSpec((None, RET_CHUNK, RET_CHUNK + 6 * LANES), lambda h: (h, 0, 0))
    outspec = lambda: pl.BlockSpec((s, LANES), lambda h: (0, h))

    def body(q_ref, k_ref, v_ref, do_ref, cos_ref, sin_ref, tab_ref, tsw_ref, dq_ref, dk_ref, dv_ref,
             qh, kh, tmp, back):
        cos, sin = cos_ref[...], sin_ref[...]
        qh[...] = _rope(q_ref[...], cos, sin)
        kh[...] = _rope(k_ref[...], cos, sin) * scale
        _ret_core(kh, qh, do_ref, tsw_ref, tmp, back, nchunk)
        dv_ref[...] = tmp[...].astype(BF16)
        _ret_core(do_ref, v_ref, kh, tab_ref, tmp, back, nchunk)
        dq_ref[...] = _rope_t(tmp[...], cos, sin).astype(BF16)
        _ret_core(v_ref, do_ref, qh, tsw_ref, tmp, back, nchunk)
        dk_ref[...] = _rope_t(tmp[...] * scale, cos, sin).astype(BF16)

    return _call(
        body, "ret_bwd", (4,),
        [col(A_Q), col(A_K), col(A_V),
         pl.BlockSpec((s, LANES), lambda h: (0, h), pipeline_mode=pl.Buffered(1)),
         whole(), whole(), tabspec(), tabspec()],
        [outspec(), outspec(), outspec()],
        [jax.ShapeDtypeStruct((s, GROUP_W), BF16)] * 3,
        [pltpu.VMEM((s, LANES), F32)] * 4,
        ("arbitrary",), (z, z, z, d_o, cos_r, sin_r, tab, tab_sw), rider)


def _normgate_bwd(o, z, gate_blk, dy, dy_blk, norm_g, tm):
    s = o.shape[0]

    def body(o_ref, g_ref, dy_ref, ng_ref, do_ref, dg_ref, dng_ref):
        @pl.when(pl.program_id(0) == 0)
        def _():
            dng_ref[...] = jnp.zeros_like(dng_ref)

        for h in range(4):
            sl = slice(LANES * h, LANES * (h + 1))
            ov, gv, dyv, ng = o_ref[:, sl], g_ref[:, sl], dy_ref[:, sl], ng_ref[:, sl]
            r = lax.rsqrt(jnp.mean(ov * ov, axis=-1, keepdims=True) + EPS)
            on = ov * r
            dn = dyv * _silu(gv)
            u = dn * ng
            do_ref[:, sl] = r * (u - on * jnp.mean(u * on, axis=-1, keepdims=True))
            dg_ref[:, sl] = (dyv * (on * ng) * _silu_grad(gv)).astype(BF16)
            dng_ref[:, sl] += jnp.sum(dn * on, axis=0, keepdims=True)

    return pl.pallas_call(
        body, name="normgate_bwd", grid=(s // tm,),
        in_specs=[pl.BlockSpec((tm, GROUP_W), lambda i: (i, 0)),
                  pl.BlockSpec((tm, GROUP_W), lambda i: (i, gate_blk // 4)),
                  pl.BlockSpec((tm, GROUP_W), lambda i: (i, dy_blk)),
                  pl.BlockSpec((1, GROUP_W), lambda i: (0, 0))],
        out_specs=[pl.BlockSpec((tm, GROUP_W), lambda i: (i, 0)), pl.BlockSpec((tm, GROUP_W), lambda i: (i, 0)),
                   pl.BlockSpec((1, GROUP_W), lambda i: (0, 0))],
        out_shape=[jax.ShapeDtypeStruct((s, GROUP_W), F32), jax.ShapeDtypeStruct((s, GROUP_W), BF16),
                   jax.ShapeDtypeStruct((1, GROUP_W), F32)],
        compiler_params=_params(("arbitrary",)),
    )(o, z, dy, norm_g)


def _log_sigmoid(x):
    return jnp.minimum(x, 0.0) - jnp.log(1.0 + jnp.exp(-jnp.abs(x)))


def _gla_consts():
    c = GLA_CHUNK
    row = lax.broadcasted_iota(jnp.int32, (c, c), 0)
    colm = lax.broadcasted_iota(jnp.int32, (c, c), 1)
    lane = lax.broadcasted_iota(jnp.int32, (1, LANES), 1)
    low = row >= colm
    up = colm >= row
    heads = ((lane < GLA_DK).astype(F32), (lane >= GLA_DK).astype(F32))
    return low, up, heads


def _chunk_running_sum(x, suffix):
    rows = x.shape[0]
    pos = jnp.bitwise_and(lax.broadcasted_iota(jnp.int32, (rows, 1), 0), GLA_CHUNK - 1)
    k = 1
    while k < GLA_CHUNK:
        if suffix:
            x = x + jnp.where(pos < GLA_CHUNK - k, pltpu.roll(x, rows - k, 0), 0.0)
        else:
            x = x + jnp.where(pos >= k, pltpu.roll(x, k, 0), 0.0)
        k *= 2
    return x


def _gla_chunk(cum_ref, d, n):
    c = GLA_CHUNK
    cum = cum_ref[d, pl.ds(pl.multiple_of(n * c, c), c), :]
    last = cum_ref[d, pl.ds(n * c + (c - 1 if d == 0 else 0), 1), :]
    eq = jnp.exp(cum)
    ek = jnp.exp(-cum)
    el = jnp.exp(last - cum)
    dec = jnp.exp(last)
    return eq, ek, el, dec


def _gla_gates(ga_ref, wa_ref, ba_ref, cum_ref, s, suffix):
    rows = min(s, GLA_CUM_ROWS)

    def step(i, carry):
        r = pl.ds(pl.multiple_of(i * rows, rows), rows)
        pre = _dot(ga_ref[r, :].astype(BF16), wa_ref[...].astype(BF16)) + ba_ref[...]
        cum_ref[r, :] = _chunk_running_sum(_log_sigmoid(pre) * (1.0 / GLA_TAU), suffix)
        return carry
    lax.fori_loop(0, s // rows, step, 0)


def _gla_fwd(z, wa_f, wa_b, ba_f, ba_b, norm_g):
    s = z.shape[0]
    c = GLA_CHUNK
    nchunk = s // c
    scale = GLA_DK ** -0.5
    tm = min(s, 512)
    one = pl.Buffered(1)

    def body(q_ref, k_ref, v_ref, ga_ref, g_ref, waf_ref, wab_ref, baf_ref, bab_ref, ng_ref, o_ref, y_ref,
             la_s):
        low, up, heads = _gla_consts()
        _gla_gates(ga_ref, waf_ref, baf_ref, la_s.at[0], s, False)
        _gla_gates(ga_ref, wab_ref, bab_ref, la_s.at[1], s, True)
        for d in range(2):
            tri = (low, up)[d]

            def step(i, states):
                n = i if d == 0 else nchunk - 1 - i
                r = pl.ds(pl.multiple_of(n * c, c), c)
                q = q_ref[r, :] * scale
                k = k_ref[r, :]
                eq, ek, el, dec = _gla_chunk(la_s, d, n)
                qt = q * eq
                ktb = (k * ek).astype(BF16)
                kl = k * el
                new_states = []
                for hh in range(2):
                    cols = slice(LANES * hh, LANES * (hh + 1))
                    vb = v_ref[r, cols].astype(BF16)
                    qm = (qt * heads[hh]).astype(BF16)
                    a = jnp.where(tri, _dot(qm, ktb, NT), 0.0)
                    o = _dot(a.astype(BF16), vb) + _dot(qm, states[hh].astype(BF16), NT)
                    if d == 0:
                        o_ref[r, cols] = o
                    else:
                        o_ref[r, cols] += o
                    new_states.append(states[hh] * dec + _dot(vb, (kl * heads[hh]).astype(BF16), TN))
                return tuple(new_states)

            zero = jnp.zeros((LANES, LANES), F32)
            _chunk_loop(nchunk, step, (zero, zero), GLA_UNROLL)

        def epi(i, carry):
            r = pl.ds(pl.multiple_of(i * tm, tm), tm)
            for hh in range(2):
                cols = slice(LANES * hh, LANES * (hh + 1))
                o = o_ref[r, cols]
                rr = lax.rsqrt(jnp.mean(o * o, axis=-1, keepdims=True) + EPS)
                y_ref[r, cols] = (_silu(g_ref[r, cols]) * (o * rr * ng_ref[:, cols])).astype(BF16)
            return carry

        lax.fori_loop(0, s // tm, epi, 0)

    w2 = 2 * LANES
    return pl.pallas_call(
        body, name="gla_fwd", grid=(2,),
        in_specs=[pl.BlockSpec((s, LANES), lambda p: (0, B_Q + p), pipeline_mode=one),
                  pl.BlockSpec((s, LANES), lambda p: (0, B_K + p), pipeline_mode=one),
                  pl.BlockSpec((s, w2), lambda p: (0, B_V // 2 + p), pipeline_mode=one),
                  pl.BlockSpec((s, LANES), lambda p: (0, GA), pipeline_mode=one),
                  pl.BlockSpec((s, w2), lambda p: (0, B_G // 2 + p), pipeline_mode=one),
                  pl.BlockSpec((LANES, LANES), lambda p: (0, p)),
                  pl.BlockSpec((LANES, LANES), lambda p: (0, p)),
                  pl.BlockSpec((1, LANES), lambda p: (0, p)),
                  pl.BlockSpec((1, LANES), lambda p: (0, p)),
                  pl.BlockSpec((1, w2), lambda p: (0, p))],
        out_specs=[pl.BlockSpec((s, w2), lambda p: (0, p)), pl.BlockSpec((s, w2), lambda p: (0, p))],
        out_shape=[jax.ShapeDtypeStruct((s, GROUP_W), F32), jax.ShapeDtypeStruct((s, GROUP_W), BF16)],
        scratch_shapes=[pltpu.VMEM((2, s, LANES), F32)],
        compiler_params=_params(("arbitrary",)),
    )(z, z, z, z, z, wa_f, wa_b, ba_f, ba_b, norm_g)


def _gla_bwd(z, d_o, wa_f, wa_b, ba_f, ba_b, rider=None):
    s = z.shape[0]
    c = GLA_CHUNK
    nchunk = s // c
    scale = GLA_DK ** -0.5
    tm = min(s, GLA_CUM_ROWS)
    one = pl.Buffered(1)

    def body(q_ref, k_ref, v_ref, ga_ref, do_ref, waf_ref, wab_ref, baf_ref, bab_ref,
             dq_ref, dk_ref, dv_ref, dga_ref, dwaf_ref, dwab_ref, dbaf_ref, dbab_ref,
             la_s, dla_s, stash, dq_s, dk_s, dv_s):
        low, up, heads = _gla_consts()
        rowi = lax.broadcasted_iota(jnp.int32, (c, 1), 0)
        _gla_gates(ga_ref, waf_ref, baf_ref, la_s.at[0], s, False)
        _gla_gates(ga_ref, wab_ref, bab_ref, la_s.at[1], s, True)
        for d in range(2):
            tri = (low, up)[d]
            last_row = (rowi == (c - 1 if d == 0 else 0)).astype(F32)
            order = (lambda i: i) if d == 0 else (lambda i: nchunk - 1 - i)
            zero = jnp.zeros((LANES, LANES), F32)

            def states(i, sts):
                n = order(i)
                r = pl.ds(pl.multiple_of(n * c, c), c)
                k = k_ref[r, :]
                _, _, el, dec = _gla_chunk(la_s, d, n)
                kl = k * el
                new = []
                for hh in range(2):
                    cols = slice(LANES * hh, LANES * (hh + 1))
                    stash[hh, n] = sts[hh]
                    new.append(sts[hh] * dec + _dot(v_ref[r, cols].astype(BF16), (kl * heads[hh]).astype(BF16), TN))
                return tuple(new)

            _chunk_loop(nchunk, states, (zero, zero), GLA_UNROLL)

            def step(i, dsts):
                n = order(nchunk - 1 - i)
                r = pl.ds(pl.multiple_of(n * c, c), c)
                q = q_ref[r, :] * scale
                k = k_ref[r, :]
                eq, ek, el, dec = _gla_chunk(la_s, d, n)
                qt = q * eq
                kt = k * ek
                kl = k * el
                ktb = kt.astype(BF16)
                dqt = jnp.zeros((c, LANES), F32)
                dkt = jnp.zeros((c, LANES), F32)
                dkl = jnp.zeros((c, LANES), F32)
                ddec = jnp.zeros((1, LANES), F32)
                new = []
                for hh in range(2):
                    cols = slice(LANES * hh, LANES * (hh + 1))
                    vb = v_ref[r, cols].astype(BF16)
                    dob = do_ref[r, cols].astype(BF16)
                    qm = (qt * heads[hh]).astype(BF16)
                    a = jnp.where(tri, _dot(qm, ktb, NT), 0.0).astype(BF16)
                    da = jnp.where(tri, _dot(dob, vb, NT), 0.0).astype(BF16)
                    sn = stash[hh, n]
                    dst = dsts[hh]
                    dstb = dst.astype(BF16)
                    dqt = dqt + (_dot(da, ktb) + _dot(dob, sn.astype(BF16))) * heads[hh]
                    dkt = dkt + _dot(da, qm, TN)
                    dv = _dot(a, dob, TN) + _dot((kl * heads[hh]).astype(BF16), dstb, NT)
                    dkl = dkl + _dot(vb, dstb)
                    ddec = ddec + jnp.sum(dst * sn, axis=0, keepdims=True)
                    new.append(dst * dec + _dot(dob, qm, TN))
                    if d == 0:
                        dv_s[r, cols] = dv
                    else:
                        dv_ref[r, cols] = (dv_s[r, cols] + dv).astype(BF16)
                dlast = ddec * dec + jnp.sum(dkl * kl, axis=0, keepdims=True)
                dq = dqt * eq * scale
                dk = dkt * ek + dkl * el
                dcum = dqt * qt - dkt * kt - dkl * kl + last_row * dlast
                dla_s[d, r, :] = dcum
                if d == 0:
                    dq_s[r, :] = dq
                    dk_s[r, :] = dk
                else:
                    dq_ref[r, :] = (dq_s[r, :] + dq).astype(BF16)
                    dk_ref[r, :] = (dk_s[r, :] + dk).astype(BF16)
                return tuple(new)

            _chunk_loop(nchunk, step, (zero, zero), GLA_UNROLL)

        first = pl.program_id(0) == 0
        for d, (wa_ref, ba_ref, dwa_ref, dba_ref) in enumerate(
                ((waf_ref, baf_ref, dwaf_ref, dbaf_ref), (wab_ref, bab_ref, dwab_ref, dbab_ref))):
            dwa_ref[...] = jnp.zeros_like(dwa_ref)
            dba_ref[...] = jnp.zeros_like(dba_ref)

            def gates(i, carry):
                r = pl.ds(pl.multiple_of(i * tm, tm), tm)
                gab = ga_ref[r, :].astype(BF16)
                wab16 = wa_ref[...].astype(BF16)
                pre = _dot(gab, wab16) + ba_ref[...]
                dla = _chunk_running_sum(dla_s[d, r, :], suffix=(d == 0))
                dpre = dla * (1.0 / GLA_TAU) * _sigmoid(-pre)
                dpb = dpre.astype(BF16)
                dwa_ref[...] += _dot(gab, dpb, TN)
                dba_ref[...] += jnp.sum(dpre, axis=0, keepdims=True)
                dga = _dot(dpb, wab16, NT)
                if d == 0:
                    @pl.when(first)
                    def _():
                        dga_ref[r, :] = dga

                    @pl.when(jnp.logical_not(first))
                    def _():
                        dga_ref[r, :] += dga
                else:
                    dga_ref[r, :] += dga
                return carry

            lax.fori_loop(0, s // tm, gates, 0)

    w2 = 2 * LANES
    return _call(
        body, "gla_bwd", (2,),
        [pl.BlockSpec((s, LANES), lambda p: (0, B_Q + p), pipeline_mode=one),
         pl.BlockSpec((s, LANES), lambda p: (0, B_K + p), pipeline_mode=one),
         pl.BlockSpec((s, w2), lambda p: (0, B_V // 2 + p), pipeline_mode=one),
         pl.BlockSpec((s, LANES), lambda p: (0, GA), pipeline_mode=one),
         pl.BlockSpec((s, w2), lambda p: (0, p), pipeline_mode=one),
         pl.BlockSpec((LANES, LANES), lambda p: (0, p)),
         pl.BlockSpec((LANES, LANES), lambda p: (0, p)),
         pl.BlockSpec((1, LANES), lambda p: (0, p)),
         pl.BlockSpec((1, LANES), lambda p: (0, p))],
        [pl.BlockSpec((s, LANES), lambda p: (0, p), pipeline_mode=one),
         pl.BlockSpec((s, LANES), lambda p: (0, p), pipeline_mode=one),
         pl.BlockSpec((s, w2), lambda p: (0, p), pipeline_mode=one),
         pl.BlockSpec((s, LANES), lambda p: (0, 0), pipeline_mode=one),
         pl.BlockSpec((LANES, LANES), lambda p: (0, p)),
         pl.BlockSpec((LANES, LANES), lambda p: (0, p)),
         pl.BlockSpec((1, LANES), lambda p: (0, p)),
         pl.BlockSpec((1, LANES), lambda p: (0, p))],
        [jax.ShapeDtypeStruct((s, w2), BF16), jax.ShapeDtypeStruct((s, w2), BF16),
         jax.ShapeDtypeStruct((s, GROUP_W), BF16), jax.ShapeDtypeStruct((s, LANES), F32),
         jax.ShapeDtypeStruct((LANES, w2), F32), jax.ShapeDtypeStruct((LANES, w2), F32),
         jax.ShapeDtypeStruct((1, w2), F32), jax.ShapeDtypeStruct((1, w2), F32)],
        [pltpu.VMEM((2, s, LANES), F32), pltpu.VMEM((2, s, LANES), F32),
         pltpu.VMEM((2, nchunk, LANES, LANES), F32),
         pltpu.VMEM((s, LANES), F32), pltpu.VMEM((s, LANES), F32), pltpu.VMEM((s, w2), F32)],
        ("arbitrary",), (z, z, z, z, d_o, wa_f, wa_b, ba_f, ba_b), rider)


def _shift_rows(x, d, rowi):
    s = x.shape[0]
    if d == 0:
        return x
    y = pltpu.roll(x, d % s, 0)
    keep = (rowi >= d) if d > 0 else (rowi < s + d)
    return jnp.where(keep, y, 0.0)


def _run_sum(x, m, step, rowi):
    acc, n = x, 1
    while n < m:
        acc = acc + _shift_rows(acc, step * n, rowi)
        n *= 2
    return acc


def _pool_counts(s, w, rowi):
    hi = jnp.minimum(rowi + w // 2, s)
    lo = jnp.maximum(rowi - w // 2, 0)
    return (hi - lo).astype(F32)


def _pooled(u, w, rowi):
    s = u.shape[0]
    win = _shift_rows(_run_sum(u, w // 2, 1, rowi), 1, rowi) + _run_sum(u, w // 2, -1, rowi)
    return win / _pool_counts(s, w, rowi) - u


def _pool_fwd(z, pool_w, pool_scale):
    s = z.shape[0]
    one = pl.Buffered(1)

    def body(u_ref, g_ref, w_ref, sc_ref, y_ref):
        rowi = lax.broadcasted_iota(jnp.int32, (s, 1), 0)
        for g, w in enumerate(POOL_WINDOWS):
            cols = slice(LANES * g, LANES * (g + 1))
            pooled = _pooled(u_ref[:, cols], w, rowi)
            mixed = _dot(pooled.astype(BF16), w_ref[g].astype(BF16))
            y_ref[:, cols] = (_silu(g_ref[:, cols]) * (mixed * sc_ref[:, cols])).astype(BF16)

    return pl.pallas_call(
        body, name="pool_fwd", grid=(1,),
        in_specs=[pl.BlockSpec((s, GROUP_W), lambda i: (0, C_V // 4), pipeline_mode=one),
                  pl.BlockSpec((s, GROUP_W), lambda i: (0, C_G // 4), pipeline_mode=one),
                  pl.BlockSpec((4, LANES, LANES), lambda i: (0, 0, 0)),
                  pl.BlockSpec((1, GROUP_W), lambda i: (0, 0))],
        out_specs=pl.BlockSpec((s, GROUP_W), lambda i: (0, 0), pipeline_mode=one),
        out_shape=jax.ShapeDtypeStruct((s, GROUP_W), BF16),
        compiler_params=_params(("arbitrary",)),
    )(z, z, pool_w, pool_scale)


def _pool_bwd(z, dy, pool_w, pool_scale):
    s = z.shape[0]
    one = pl.Buffered(1)

    def body(u_ref, g_ref, dy_ref, w_ref, sc_ref, du_ref, dg_ref, dw_ref, dsc_ref):
        rowi = lax.broadcasted_iota(jnp.int32, (s, 1), 0)
        for g, w in enumerate(POOL_WINDOWS):
            cols = slice(LANES * g, LANES * (g + 1))
            gate, dyv, sc = g_ref[:, cols], dy_ref[:, cols], sc_ref[:, cols]
            wb = w_ref[g].astype(BF16)
            pooled = _pooled(u_ref[:, cols], w, rowi)
            pb = pooled.astype(BF16)
            mixed = _dot(pb, wb)
            dg_ref[:, cols] = (dyv * (mixed * sc) * _silu_grad(gate)).astype(BF16)
            dt = dyv * _silu(gate)
            dsc_ref[:, cols] = jnp.sum(dt * mixed, axis=0, keepdims=True)
            dmb = (dt * sc).astype(BF16)
            dw_ref[g] = _dot(pb, dmb, TN)
            dpool = _dot(dmb, wb, NT)
            e = dpool / _pool_counts(s, w, rowi)
            du_ref[:, cols] = (_run_sum(e, w // 2, 1, rowi) + _shift_rows(_run_sum(e, w // 2, -1, rowi), -1, rowi)
                               - dpool).astype(BF16)

    return pl.pallas_call(
        body, name="pool_bwd", grid=(1,),
        in_specs=[pl.BlockSpec((s, GROUP_W), lambda i: (0, C_V // 4), pipeline_mode=one),
                  pl.BlockSpec((s, GROUP_W), lambda i: (0, C_G // 4), pipeline_mode=one),
                  pl.BlockSpec((s, GROUP_W), lambda i: (0, 2), pipeline_mode=one),
                  pl.BlockSpec((4, LANES, LANES), lambda i: (0, 0, 0)),
                  pl.BlockSpec((1, GROUP_W), lambda i: (0, 0))],
        out_specs=[pl.BlockSpec((s, GROUP_W), lambda i: (0, 0), pipeline_mode=one),
                   pl.BlockSpec((s, GROUP_W), lambda i: (0, 0), pipeline_mode=one),
                   pl.BlockSpec((4, LANES, LANES), lambda i: (0, 0, 0)),
                   pl.BlockSpec((1, GROUP_W), lambda i: (0, 0))],
        out_shape=[jax.ShapeDtypeStruct((s, GROUP_W), BF16), jax.ShapeDtypeStruct((s, GROUP_W), BF16),
                   jax.ShapeDtypeStruct((4, LANES, LANES), F32), jax.ShapeDtypeStruct((1, GROUP_W), F32)],
        compiler_params=_params(("arbitrary",)),
    )(z, z, dy, pool_w, pool_scale)


def _mla_heads(qf, kv, kpe, qg, kg, cos, sin):
    out = []
    for h in range(4):
        qa = qf[:, LANES * h:LANES * (h + 1)]
        qb = qf[:, 512 + LANES * h:512 + LANES * (h + 1)]
        ka = kv[:, 256 * h:256 * h + LANES]
        rq = lax.rsqrt((jnp.sum(qa * qa, axis=-1, keepdims=True) + jnp.sum(qb * qb, axis=-1, keepdims=True))
                       * (1.0 / MLA_QK) + EPS)
        rk = lax.rsqrt((jnp.sum(ka * ka, axis=-1, keepdims=True) + jnp.sum(kpe * kpe, axis=-1, keepdims=True))
                       * (1.0 / MLA_QK) + EPS)
        out.append((qa, qb, rq, ka, rk))
    return out


def _mla_latents(mq_ref, mkv_ref, gq_ref, gkv_ref, wq_ref, wkv_ref):
    mq = mq_ref[...]
    rq = lax.rsqrt(jnp.mean(mq * mq, axis=-1, keepdims=True) + EPS)
    qn = mq * rq
    qnb = (qn * gq_ref[...]).astype(BF16)
    mkv = mkv_ref[...]
    rk = lax.rsqrt(jnp.mean(mkv * mkv, axis=-1, keepdims=True) + EPS)
    kvn = mkv * rk
    kvnb = (kvn * gkv_ref[...]).astype(BF16)
    qf = _dot(qnb, wq_ref[...])
    kv = _dot(kvnb, wkv_ref[...])
    return qn, rq, qnb, kvn, rk, kvnb, qf, kv


def _mla_prep(z, cos_m, sin_m, gq, wq, gkv, wkv, qg, kg, tm):
    s = z.shape[0]

    def body(mq_ref, mkv_ref, mkr_ref, cos_ref, sin_ref, gq_ref, wq_ref, gkv_ref, wkv_ref, qg_ref, kg_ref,
             q_ref, k_ref, v_ref):
        _, _, _, _, _, _, qf, kv = _mla_latents(mq_ref, mkv_ref, gq_ref, gkv_ref, wq_ref, wkv_ref)
        kpe = mkr_ref[...]
        cos, sin = cos_ref[...], sin_ref[...]
        qg, kg = qg_ref[...], kg_ref[...]
        for h, (qa, qb, rq, ka, rk) in enumerate(_mla_heads(qf, kv, kpe, qg, kg, cos, sin)):
            q_ref[h, :, 0:LANES] = (qa * rq * qg[:, 0:LANES] * ATTN_Q_SCALE).astype(BF16)
            q_ref[h, :, LANES:] = (_rope(qb * rq * qg[:, LANES:], cos, sin) * ATTN_Q_SCALE).astype(BF16)
            k_ref[h, :, 0:LANES] = (ka * rk * kg[:, 0:LANES]).astype(BF16)
            k_ref[h, :, LANES:] = _rope(kpe * rk * kg[:, LANES:], cos, sin).astype(BF16)
            v_ref[h] = kv[:, 256 * h + LANES:256 * (h + 1)].astype(BF16)

    full = lambda shape: pl.BlockSpec(shape, lambda i: (0,) * len(shape))
    return pl.pallas_call(
        body, name="mla_prep", grid=(s // tm,),
        in_specs=[pl.BlockSpec((tm, 512), lambda i: (i, M_Q // 4)),
                  pl.BlockSpec((tm, 256), lambda i: (i, M_KV // 2)),
                  pl.BlockSpec((tm, LANES), lambda i: (i, M_KR)),
                  pl.BlockSpec((tm, LANES), lambda i: (i, 0)),
                  pl.BlockSpec((tm, LANES), lambda i: (i, 0)),
                  full((1, 512)), full((512, 1024)), full((1, 256)), full((256, 1024)), full((1, 256)), full((1, 256))],
        out_specs=[pl.BlockSpec((4, tm, 256), lambda i: (0, i, 0)), pl.BlockSpec((4, tm, 256), lambda i: (0, i, 0)),
                   pl.BlockSpec((4, tm, LANES), lambda i: (0, i, 0))],
        out_shape=[jax.ShapeDtypeStruct((4, s, 256), BF16), jax.ShapeDtypeStruct((4, s, 256), BF16),
                   jax.ShapeDtypeStruct((4, s, LANES), BF16)],
        compiler_params=_params(("parallel",)),
    )(z, z, z, cos_m, sin_m, gq, wq, gkv, wkv, qg, kg)


def _mla_prep_bwd(z, cos_m, sin_m, gq, wq, gkv, wkv, qg, kg, dq, dk, dv, tm):
    s = z.shape[0]

    def body(mq_ref, mkv_ref, mkr_ref, cos_ref, sin_ref, gq_ref, wq_ref, gkv_ref, wkv_ref, qg_ref, kg_ref,
             dq_ref, dk_ref, dv_ref,
             dmq_ref, dmkv_ref, dmkr_ref, dwq_ref, dwkv_ref, dgq_ref, dgkv_ref, dqg_ref, dkg_ref, dqf, dkv):
        @pl.when(pl.program_id(0) == 0)
        def _():
            for r in (dwq_ref, dwkv_ref, dgq_ref, dgkv_ref, dqg_ref, dkg_ref):
                r[...] = jnp.zeros_like(r)

        qn, rq0, qnb, kvn, rk0, kvnb, qf, kv = _mla_latents(mq_ref, mkv_ref, gq_ref, gkv_ref, wq_ref, wkv_ref)
        kpe = mkr_ref[...]
        cos, sin = cos_ref[...], sin_ref[...]
        qg, kg = qg_ref[...], kg_ref[...]
        dkpe = jnp.zeros_like(kpe)
        inv = 1.0 / MLA_QK

        def norm_bwd(a, b, r, da_n, db_n, g):
            ga, gb = g[:, 0:LANES], g[:, LANES:]
            dg_a = jnp.sum(da_n * a * r, axis=0, keepdims=True)
            dg_b = jnp.sum(db_n * b * r, axis=0, keepdims=True)
            ua, ub = da_n * ga, db_n * gb
            dt = (jnp.sum(ua * a, axis=-1, keepdims=True) + jnp.sum(ub * b, axis=-1, keepdims=True)) * inv
            r3 = r * r * r
            return r * ua - a * (r3 * dt), r * ub - b * (r3 * dt), dg_a, dg_b

        for h, (qa, qb, rq, ka, rk) in enumerate(_mla_heads(qf, kv, kpe, qg, kg, cos, sin)):
            dqa, dqb, dga, dgb = norm_bwd(qa, qb, rq, dq_ref[h, :, 0:LANES] * ATTN_SCALE,
                                          _rope_t(dq_ref[h, :, LANES:] * ATTN_SCALE, cos, sin), qg)
            dqf[:, LANES * h:LANES * (h + 1)] = dqa
            dqf[:, 512 + LANES * h:512 + LANES * (h + 1)] = dqb
            dqg_ref[:, 0:LANES] += dga
            dqg_ref[:, LANES:] += dgb
            ln2 = math.log(2.0)
            dka, dkb, dga, dgb = norm_bwd(ka, kpe, rk, dk_ref[h, :, 0:LANES] * ln2,
                                          _rope_t(dk_ref[h, :, LANES:] * ln2, cos, sin), kg)
            dkv[:, 256 * h:256 * h + LANES] = dka
            dkv[:, 256 * h + LANES:256 * (h + 1)] = dv_ref[h]
            dkpe = dkpe + dkb
            dkg_ref[:, 0:LANES] += dga
            dkg_ref[:, LANES:] += dgb
        dmkr_ref[...] = dkpe.astype(BF16)

        def latent_bwd(dfull, w_ref, nb, n, r, g_ref, dw_ref, dg_ref, dlat_ref):
            db = dfull.astype(BF16)
            dn = _dot(db, w_ref[...], NT)
            dw_ref[...] += _dot(nb, db, TN)
            dg_ref[...] += jnp.sum(dn * n, axis=0, keepdims=True)
            u = dn * g_ref[...]
            dlat_ref[...] = (r * (u - n * jnp.mean(u * n, axis=-1, keepdims=True))).astype(BF16)

        latent_bwd(dqf[...], wq_ref, qnb, qn, rq0, gq_ref, dwq_ref, dgq_ref, dmq_ref)
        latent_bwd(dkv[...], wkv_ref, kvnb, kvn, rk0, gkv_ref, dwkv_ref, dgkv_ref, dmkv_ref)

    full = lambda shape: pl.BlockSpec(shape, lambda i: (0,) * len(shape))
    return pl.pallas_call(
        body, name="mla_prep_bwd", grid=(s // tm,),
        in_specs=[pl.BlockSpec((tm, 512), lambda i: (i, M_Q // 4)),
                  pl.BlockSpec((tm, 256), lambda i: (i, M_KV // 2)),
                  pl.BlockSpec((tm, LANES), lambda i: (i, M_KR)),
                  pl.BlockSpec((tm, LANES), lambda i: (i, 0)),
                  pl.BlockSpec((tm, LANES), lambda i: (i, 0)),
                  full((1, 512)), full((512, 1024)), full((1, 256)), full((256, 1024)), full((1, 256)), full((1, 256)),
                  pl.BlockSpec((4, tm, 256), lambda i: (0, i, 0)), pl.BlockSpec((4, tm, 256), lambda i: (0, i, 0)),
                  pl.BlockSpec((4, tm, LANES), lambda i: (0, i, 0))],
        out_specs=[pl.BlockSpec((tm, 512), lambda i: (i, 0)), pl.BlockSpec((tm, 256), lambda i: (i, 0)),
                   pl.BlockSpec((tm, LANES), lambda i: (i, 0)),
                   full((512, 1024)), full((256, 1024)), full((1, 512)), full((1, 256)), full((1, 256)), full((1, 256))],
        out_shape=[jax.ShapeDtypeStruct((s, 512), BF16), jax.ShapeDtypeStruct((s, 256), BF16),
                   jax.ShapeDtypeStruct((s, LANES), BF16),
                   jax.ShapeDtypeStruct((512, 1024), F32), jax.ShapeDtypeStruct((256, 1024), F32),
                   jax.ShapeDtypeStruct((1, 512), F32), jax.ShapeDtypeStruct((1, 256), F32),
                   jax.ShapeDtypeStruct((1, 256), F32), jax.ShapeDtypeStruct((1, 256), F32)],
        scratch_shapes=[pltpu.VMEM((tm, 1024), F32), pltpu.VMEM((tm, 1024), F32)],
        compiler_params=_params(("arbitrary",)),
    )(z, z, z, cos_m, sin_m, gq, wq, gkv, wkv, qg, kg, dq, dk, dv)


def _attn_fwd(q, k, v, z, tq, rider=None):
    s = q.shape[1]

    def body(q_ref, k_ref, v_ref, g_ref, o_ref, y_ref, lse_ref):
        sc = _dot(q_ref[...], k_ref[...], NT)
        m = jnp.max(sc, axis=-1, keepdims=True)
        p = jnp.exp2(sc - m)
        l = jnp.sum(p, axis=-1, keepdims=True)
        o = _dot(p.astype(BF16), v_ref[...]) / l
        o_ref[...] = o
        y_ref[...] = (_silu(g_ref[...]) * o).astype(BF16)
        lse_ref[...] = m + jnp.log2(l)

    return _call(
        body, "attn_fwd", (4, s // tq),
        [pl.BlockSpec((None, tq, 256), lambda h, i: (h, i, 0)),
         pl.BlockSpec((None, s, 256), lambda h, i: (h, 0, 0)),
         pl.BlockSpec((None, s, LANES), lambda h, i: (h, 0, 0)),
         pl.BlockSpec((tq, LANES), lambda h, i: (i, M_G + h))],
        [pl.BlockSpec((tq, LANES), lambda h, i: (i, h)), pl.BlockSpec((tq, LANES), lambda h, i: (i, h)),
         pl.BlockSpec((None, tq, 1), lambda h, i: (h, i, 0))],
        [jax.ShapeDtypeStruct((s, GROUP_W), F32), jax.ShapeDtypeStruct((s, GROUP_W), BF16),
         jax.ShapeDtypeStruct((4, s, 1), F32)],
        [], ("parallel", "parallel"), (q, k, v, z), rider)


def _attn_bwd(q, k, v, z, o, lse, dy, tq, rider=None):
    s = q.shape[1]

    def body(q_ref, k_ref, v_ref, g_ref, o_ref, lse_ref, dy_ref, dq_ref, dk_ref, dv_ref, dg_ref):
        @pl.when(pl.program_id(1) == 0)
        def _():
            dk_ref[...] = jnp.zeros_like(dk_ref)
            dv_ref[...] = jnp.zeros_like(dv_ref)

        gate, ov, dyv = g_ref[...], o_ref[...], dy_ref[...]
        do = dyv * _silu(gate)
        dg_ref[...] = (dyv * ov * _silu_grad(gate)).astype(BF16)
        delta = jnp.sum(do * ov, axis=-1, keepdims=True)
        dob = do.astype(BF16)
        qb, kb = q_ref[...], k_ref[...]
        p = jnp.exp2(_dot(qb, kb, NT) - lse_ref[...])
        dp = _dot(dob, v_ref[...], NT)
        ds = (p * (dp - delta)).astype(BF16)
        dq_ref[...] = _dot(ds, kb)
        dk_ref[...] += _dot(ds, qb, TN)
        dv_ref[...] += _dot(p.astype(BF16), dob, TN)

    return _call(
        body, "attn_bwd", (4, s // tq),
        [pl.BlockSpec((None, tq, 256), lambda h, i: (h, i, 0)),
         pl.BlockSpec((None, s, 256), lambda h, i: (h, 0, 0)),
         pl.BlockSpec((None, s, LANES), lambda h, i: (h, 0, 0)),
         pl.BlockSpec((tq, LANES), lambda h, i: (i, M_G + h)),
         pl.BlockSpec((tq, LANES), lambda h, i: (i, h)),
         pl.BlockSpec((None, tq, 1), lambda h, i: (h, i, 0)),
         pl.BlockSpec((tq, LANES), lambda h, i: (i, 12 + h))],
        [pl.BlockSpec((None, tq, 256), lambda h, i: (h, i, 0)),
         pl.BlockSpec((None, s, 256), lambda h, i: (h, 0, 0)),
         pl.BlockSpec((None, s, LANES), lambda h, i: (h, 0, 0)),
         pl.BlockSpec((tq, LANES), lambda h, i: (i, h))],
        [jax.ShapeDtypeStruct((4, s, 256), F32), jax.ShapeDtypeStruct((4, s, 256), F32),
         jax.ShapeDtypeStruct((4, s, LANES), F32), jax.ShapeDtypeStruct((s, GROUP_W), BF16)],
        [], ("parallel", "arbitrary"), (q, k, v, z, o, lse, dy), rider)


def _adam(parts, w, m, v, name, tr):
    r, c = w.shape
    tr = min(tr, r)
    c1 = 1.0 - ADAM_B1 ** ADAM_STEP
    c2 = 1.0 - ADAM_B2 ** ADAM_STEP

    def body(p_ref, w_ref, m_ref, v_ref, g_ref, d_ref, nm_ref, nv_ref):
        g = p_ref[0].astype(F32)
        for i in range(1, N_DEV):
            g = g + p_ref[i].astype(F32)
        nm = ADAM_B1 * m_ref[...] + (1.0 - ADAM_B1) * g
        nv = ADAM_B2 * v_ref[...] + (1.0 - ADAM_B2) * (g * g)
        g_ref[...] = g
        nm_ref[...] = nm
        nv_ref[...] = nv
        d_ref[...] = -ADAM_LR * ((nm / c1) / (jnp.sqrt(nv / c2) + ADAM_EPS) + ADAM_WD * w_ref[...])

    blk = lambda: pl.BlockSpec((tr, c), lambda i: (i, 0))
    return pl.pallas_call(
        body, name=name, grid=(r // tr,),
        in_specs=[pl.BlockSpec((N_DEV, tr, c), lambda i: (0, i, 0)), blk(), blk(), blk()],
        out_specs=[blk(), blk(), blk(), blk()],
        out_shape=[jax.ShapeDtypeStruct((r, c), F32)] * 4,
        compiler_params=_params(("parallel",)),
    )(parts, w, m, v)


def _adam_columns(parts, w, m, v, name, tc, rider=None):
    nl, r, c = w.shape
    pieces = [p for layer in parts for p in layer]
    nh = len(parts[0])
    rp = pieces[0].shape[2]
    tc = min(tc, rp)
    ncb = rp // tc
    c1 = 1.0 - ADAM_B1 ** ADAM_STEP
    c2 = 1.0 - ADAM_B2 ** ADAM_STEP

    def body(*refs):
        p_refs, (w_ref, m_ref, v_ref, g_ref, d_ref, nm_ref, nv_ref) = refs[:len(pieces)], refs[len(pieces):]
        for h in range(nh):
            @pl.when(pl.program_id(0) == h)
            def _(h=h):
                for l in range(nl):
                    p_ref = p_refs[l * nh + h]
                    g = p_ref[0].astype(F32)
                    for i in range(1, N_DEV):
                        g = g + p_ref[i].astype(F32)
                    nm = ADAM_B1 * m_ref[:, l, :] + (1.0 - ADAM_B1) * g
                    nv = ADAM_B2 * v_ref[:, l, :] + (1.0 - ADAM_B2) * (g * g)
                    g_ref[:, l, :] = g
                    nm_ref[:, l, :] = nm
                    nv_ref[:, l, :] = nv
                    d_ref[:, l, :] = -ADAM_LR * ((nm / c1) / (jnp.sqrt(nv / c2) + ADAM_EPS) + ADAM_WD * w_ref[:, l, :])

    def part_spec(j):
        return pl.BlockSpec((N_DEV, c, tc), lambda h, i: (0, 0, jnp.clip((h - j % nh) * ncb + i, 0, ncb - 1)))

    blk = lambda: pl.BlockSpec((c, nl, tc), lambda h, i: (0, 0, h * ncb + i))
    t = lambda a: jnp.transpose(a, (2, 0, 1))
    *res, = _call(body, name, (nh, ncb), [part_spec(j) for j in range(len(pieces))] + [blk(), blk(), blk()],
                  [blk(), blk(), blk(), blk()], [jax.ShapeDtypeStruct((c, nl, r), F32)] * 4, [],
                  ("arbitrary",) * 2, (*pieces, t(w), t(m), t(v)), rider)
    return [jnp.transpose(a, (1, 2, 0)) for a in res[:4]] + res[4:]


def _adam_layers(parts, w, m, v, name, tr, rider=None):
    nl, r, c = w.shape
    pieces = [p for layer in parts for p in layer]
    rp = pieces[0].shape[1]
    tr = min(tr, rp)
    nr, nrp = r // tr, rp // tr
    c1 = 1.0 - ADAM_B1 ** ADAM_STEP
    c2 = 1.0 - ADAM_B2 ** ADAM_STEP

    def body(*refs):
        p_refs, (w_ref, m_ref, v_ref, g_ref, d_ref, nm_ref, nv_ref) = refs[:len(pieces)], refs[len(pieces):]
        at = pl.program_id(0) * nr + pl.program_id(1)
        for j in range(len(pieces)):
            @pl.when(jnp.logical_and(at >= j * nrp, at < (j + 1) * nrp))
            def _(p_ref=p_refs[j]):
                g = p_ref[0].astype(F32)
                for i in range(1, N_DEV):
                    g = g + p_ref[i].astype(F32)
                nm = ADAM_B1 * m_ref[...] + (1.0 - ADAM_B1) * g
                nv = ADAM_B2 * v_ref[...] + (1.0 - ADAM_B2) * (g * g)
                g_ref[...] = g
                nm_ref[...] = nm
                nv_ref[...] = nv
                d_ref[...] = -ADAM_LR * ((nm / c1) / (jnp.sqrt(nv / c2) + ADAM_EPS) + ADAM_WD * w_ref[...])

    def part_spec(j):
        return pl.BlockSpec((N_DEV, tr, c), lambda ll, i: (0, jnp.clip(ll * nr + i - j * nrp, 0, nrp - 1), 0))

    blk = lambda: pl.BlockSpec((None, tr, c), lambda ll, i: (ll, i, 0))
    return _call(body, name, (nl, nr), [part_spec(j) for j in range(len(pieces))] + [blk(), blk(), blk()],
                 [blk(), blk(), blk(), blk()], [jax.ShapeDtypeStruct((nl, r, c), F32)] * 4, [],
                 ("arbitrary", "arbitrary"), (*pieces, w, m, v), rider)


REPLICATED = ("norm_g", "ret_norm_g", "gla_ba_f", "gla_ba_b", "gla_norm_g", "pool_w", "pool_scale",
              "mla_q_norm_g", "mla_kv_norm_g", "mla_qk_norm_q", "mla_qk_norm_k")
REPLICATED_EARLY = REPLICATED[1:]
SMALL_SHARDED = ("mla_wq_b", "mla_wkv_b", "gla_wa2_f", "gla_wa2_b")
WEIGHTS = ("norm_g", "w_in", "ret_norm_g", "gla_wa2_f", "gla_ba_f", "gla_wa2_b", "gla_ba_b", "gla_norm_g", "pool_w",
           "pool_scale", "mla_q_norm_g", "mla_wq_b", "mla_kv_norm_g", "mla_wkv_b", "mla_qk_norm_q", "mla_qk_norm_k",
           "w_out")


PACK_ROWS = 16


def _packed_rows(a):
    rows = a.size // LANES
    return rows, -(-rows // PACK_ROWS) * PACK_ROWS


def _pack(arrays, dtype):
    parts = []
    for a in arrays:
        rows, padded = _packed_rows(a)
        parts.append(jnp.pad(a.reshape(rows, LANES).astype(dtype), ((0, padded - rows), (0, 0))))
    return jnp.concatenate(parts, axis=0)


def _unpack(packed, like):
    out, at = [], 0
    for a in like:
        rows, padded = _packed_rows(a)
        out.append(packed[..., at:at + rows, :].reshape(packed.shape[:-2] + a.shape))
        at += padded
    return out


def _columns_by_device(g):
    l, r, n = g.shape
    return g.reshape(l, r, N_DEV, n // N_DEV).transpose(2, 0, 1, 3)


def _gathered_columns(g, l, r, c):
    return g.reshape(N_DEV, l, r, c).transpose(1, 2, 0, 3).reshape(l, r, N_DEV * c)


def _layer_forward(x, wts, late_wts, tables, tm, tq, ride_inproj=None, ride_attn=None, target=None):
    cos_r, sin_r, cos_m, sin_m, tab, _ = tables
    z, h, *carried_in = _inproj(x, wts["norm_g"], wts["w_in"], min(x.shape[0], 2 * tm), rider=ride_inproj)
    wts.update(late_wts("inproj", carried_in))
    o_a, y_a = _ret_fwd(z, cos_r, sin_r, tab, wts["ret_norm_g"])
    o_b, y_b = _gla_fwd(z, wts["wa_f"], wts["wa_b"], wts["gla_ba_f"], wts["gla_ba_b"], wts["gla_norm_g"])
    y_c = _pool_fwd(z, wts["pool_w"], wts["pool_scale"])
    q, k, v = _mla_prep(z, cos_m, sin_m, wts["mla_q_norm_g"], wts["wq"], wts["mla_kv_norm_g"], wts["wkv"],
                        wts["qk_q"], wts["qk_k"], tm)
    o_d, y_d, lse, *carried_attn = _attn_fwd(q, k, v, z, tq, rider=ride_attn)
    wts.update(late_wts("attn", carried_attn))
    y = jnp.concatenate([y_a, y_b, y_c, y_d], axis=1)
    w_out = wts["w_out"]
    if target is None:
        x_next = _mm(y, w_out, "nn", "outproj", tm, D_MODEL, 1024, add=x)
    else:
        x_next = _mm(y, w_out, "nn", "outproj_loss", tm, D_MODEL, 1024, tail=_loss_tail(x, target))
    saved = dict(x=x, z=z, h=h, o_a=o_a, o_b=o_b, o_d=o_d, lse=lse, q=q, k=k, v=v, y=y, w_out=w_out)
    return x_next, saved, carried_in, carried_attn


def _layer_backward(dx, sv, wts, tables, tm, tq, rides):
    cos_r, sin_r, cos_m, sin_m, tab, tab_sw = tables
    z = sv["z"]
    g = {}
    carried = {}

    def rider(name):
        return rides[name](g) if name in rides else None

    def landed(name, results, n_own):
        if name in rides:
            carried[name] = list(results[n_own:])
        return results[:n_own]

    g["w_out"] = _mm(sv["y"], dx, "tn", "d_w_out", 2048, 1024, 1024, out_dtype=BF16)
    dy = _mm(dx, sv["w_out"], "nt", "d_y", tm, 2048, 1024)

    do_a, dg_a, g["ret_norm_g"] = _normgate_bwd(sv["o_a"], z, A_G, dy, 0, wts["ret_norm_g"], tm)
    dq_a, dk_a, dv_a = landed("ret", _ret_bwd(z, do_a, cos_r, sin_r, tab, tab_sw, rider=rider("ret")), 3)

    do_b, dg_b, g["gla_norm_g"] = _normgate_bwd(sv["o_b"], z, B_G, dy, 1, wts["gla_norm_g"], tm)
    dq_b, dk_b, dv_b, d_ga, d_waf, d_wab, g["gla_ba_f"], g["gla_ba_b"] = landed("gla", _gla_bwd(
        z, do_b, wts["wa_f"], wts["wa_b"], wts["gla_ba_f"], wts["gla_ba_b"], rider=rider("gla")), 8)
    g["gla_wa2_f"] = d_waf[0:GLA_RANK]
    g["gla_wa2_b"] = d_wab[GLA_RANK:2 * GLA_RANK]

    du_c, dg_c, g["pool_w"], g["pool_scale"] = _pool_bwd(z, dy, wts["pool_w"], wts["pool_scale"])

    d_q, d_k, d_v, dg_d = landed("attn", _attn_bwd(sv["q"], sv["k"], sv["v"], z, sv["o_d"], sv["lse"], dy, tq,
                                                   rider=rider("attn")), 4)
    (d_mq, d_mkv, d_mkr, d_wq, g["mla_wkv_b"], g["mla_q_norm_g"], g["mla_kv_norm_g"], d_qg, d_kg) = _mla_prep_bwd(
        z, cos_m, sin_m, wts["mla_q_norm_g"], wts["wq"], wts["mla_kv_norm_g"], wts["wkv"], wts["qk_q"], wts["qk_k"],
        d_q, d_k, d_v, tm)
    g["mla_wq_b"] = _unpad_wq(d_wq)
    g["mla_qk_norm_q"] = d_qg[:, _QK_INV]
    g["mla_qk_norm_k"] = d_kg[:, _QK_INV]

    dz = jnp.concatenate([dq_a, dk_a, dv_a, dg_a, dq_b, dk_b, dv_b, dg_b, d_mq, du_c, dg_c, dg_d, d_mkv,
                          d_ga.astype(BF16), d_mkr], axis=1)
    h, half = sv["h"], D_MODEL // 2
    for name, cols in (("d_w_in_a", h[:, :half]), ("d_w_in_b", h[:, half:])):
        res = _mm(dz, cols, "tn", name, 2048, 1024, 1024, out_dtype=BF16, rider=rider(name))
        (d_wt,) = landed(name, res if name in rides else [res], 1)
        g["w_in" + name[-2:]] = _split_w_in(d_wt)
    dx_in, g["norm_g"] = landed("d_h", _mm(dz, wts["w_in"], "nn", "d_h", tm, D_MODEL, 1024, rider=rider("d_h"),
                                           tail=_norm_bwd_tail(sv["x"], wts["norm_g"], dx)), 2)
    return dx_in, g, carried


def kernel(x, norm_g, w_in, ret_norm_g, gla_wa2_f, gla_ba_f, gla_wa2_b, gla_ba_b, gla_norm_g, pool_w, pool_scale, mla_q_norm_g, mla_wq_b, mla_kv_norm_g, mla_wkv_b, mla_qk_norm_q, mla_qk_norm_k, w_out, loss_target, m_norm_g, m_w_in, m_ret_norm_g, m_gla_wa2_f, m_gla_ba_f, m_gla_wa2_b, m_gla_ba_b, m_gla_norm_g, m_pool_w, m_pool_scale, m_mla_q_norm_g, m_mla_wq_b, m_mla_kv_norm_g, m_mla_wkv_b, m_mla_qk_norm_q, m_mla_qk_norm_k, m_w_out, v_norm_g, v_w_in, v_ret_norm_g, v_gla_wa2_f, v_gla_ba_f, v_gla_wa2_b, v_gla_ba_b, v_gla_norm_g, v_pool_w, v_pool_scale, v_mla_q_norm_g, v_mla_wq_b, v_mla_kv_norm_g, v_mla_wkv_b, v_mla_qk_norm_q, v_mla_qk_norm_k, v_w_out):
    w = dict(norm_g=norm_g, w_in=w_in, ret_norm_g=ret_norm_g, gla_wa2_f=gla_wa2_f, gla_ba_f=gla_ba_f,
             gla_wa2_b=gla_wa2_b, gla_ba_b=gla_ba_b, gla_norm_g=gla_norm_g, pool_w=pool_w, pool_scale=pool_scale,
             mla_q_norm_g=mla_q_norm_g, mla_wq_b=mla_wq_b, mla_kv_norm_g=mla_kv_norm_g, mla_wkv_b=mla_wkv_b,
             mla_qk_norm_q=mla_qk_norm_q, mla_qk_norm_k=mla_qk_norm_k, w_out=w_out)
    m = dict(norm_g=m_norm_g, w_in=m_w_in, ret_norm_g=m_ret_norm_g, gla_wa2_f=m_gla_wa2_f, gla_ba_f=m_gla_ba_f,
             gla_wa2_b=m_gla_wa2_b, gla_ba_b=m_gla_ba_b, gla_norm_g=m_gla_norm_g, pool_w=m_pool_w,
             pool_scale=m_pool_scale, mla_q_norm_g=m_mla_q_norm_g, mla_wq_b=m_mla_wq_b, mla_kv_norm_g=m_mla_kv_norm_g,
             mla_wkv_b=m_mla_wkv_b, mla_qk_norm_q=m_mla_qk_norm_q, mla_qk_norm_k=m_mla_qk_norm_k, w_out=m_w_out)
    v = dict(norm_g=v_norm_g, w_in=v_w_in, ret_norm_g=v_ret_norm_g, gla_wa2_f=v_gla_wa2_f, gla_ba_f=v_gla_ba_f,
             gla_wa2_b=v_gla_wa2_b, gla_ba_b=v_gla_ba_b, gla_norm_g=v_gla_norm_g, pool_w=v_pool_w,
             pool_scale=v_pool_scale, mla_q_norm_g=v_mla_q_norm_g, mla_wq_b=v_mla_wq_b, mla_kv_norm_g=v_mla_kv_norm_g,
             mla_wkv_b=v_mla_wkv_b, mla_qk_norm_q=v_mla_qk_norm_q, mla_qk_norm_k=v_mla_qk_norm_k, w_out=v_w_out)
    xs, target = x[0], loss_target[0]
    s = xs.shape[0]
    tm, tq = min(s, 512), min(s, 256)
    c_in = w_in.shape[2]

    w_in_b = jnp.transpose(w_in, (2, 0, 1)).astype(BF16)
    w_out_b = w_out.astype(BF16).reshape(-1, D_MODEL)
    (w_in_g0,) = _exchange([("gather", w_in_b[:, 0])], "gather_first")
    tables = _rope_tables(s) + _ret_tables()

    def early_weights(l, w_in_g):
        return dict(
            norm_g=norm_g[l][None], w_in=_assemble_w_in(w_in_g), ret_norm_g=ret_norm_g[l][None],
            gla_ba_f=gla_ba_f[l][None], gla_ba_b=gla_ba_b[l][None],
            gla_norm_g=gla_norm_g[l][None], pool_w=pool_w[l], pool_scale=pool_scale[l][None],
            mla_q_norm_g=mla_q_norm_g[l][None], mla_kv_norm_g=mla_kv_norm_g[l][None],
            qk_q=_pad_qk_gain(mla_qk_norm_q[l]), qk_k=_pad_qk_gain(mla_qk_norm_k[l]))

    def small_weights(l, small_g):
        shards = _unpack(small_g, [w[n] for n in SMALL_SHARDED])
        full = {n: _gathered_columns(shards[i], *w[n].shape)[l] for i, n in enumerate(SMALL_SHARDED)}
        wa_f = jnp.zeros((LANES, 2 * LANES), BF16).at[0:GLA_RANK].set(full["gla_wa2_f"])
        wa_b = jnp.zeros((LANES, 2 * LANES), BF16).at[GLA_RANK:2 * GLA_RANK].set(full["gla_wa2_b"])
        return dict(wa_f=wa_f, wa_b=wa_b, wq=_pad_wq(full["mla_wq_b"]), wkv=full["mla_wkv_b"])

    def w_out_layer(l, w_out_g):
        return dict(w_out=w_out_g.reshape(N_DEV, DEPTH, -1, D_MODEL)[:, l].reshape(-1, D_MODEL))

    by_owner = lambda g_w_out: g_w_out.reshape(N_DEV, -1, D_MODEL)

    layers = [early_weights(0, w_in_g0), None]
    x1, sv0, (w_in_g1, small_g), (w_out_g,) = _layer_forward(
        xs, layers[0], lambda carrier, got: small_weights(0, got[1]) if carrier == "inproj" else w_out_layer(0, got[0]),
        tables, tm, tq,
        ride_inproj=[("gather", w_in_b[:, 1]), ("gather", _pack([w[n] for n in SMALL_SHARDED], BF16))],
        ride_attn=("gather", w_out_b))
    layers[1] = early_weights(1, w_in_g1)
    (dx, loss_row), sv1, _, _ = _layer_forward(
        x1, layers[1], lambda carrier, got: {**small_weights(1, small_g), **w_out_layer(1, w_out_g)}, tables, tm, tq,
        target=target)
    loss = lax.psum(loss_row[0, 0], ("x", "y", "c"))

    def small_jobs(g):
        grads = (g, g1)
        full = {n: jnp.stack([grads[l][n].reshape(w[n].shape[1:]) if n in REPLICATED else grads[l][n]
                              for l in range(DEPTH)]) for n in SMALL_SHARDED + REPLICATED_EARLY}
        small_c = jax.vmap(lambda *shards: _pack(shards, F32))(*[_columns_by_device(full[n]) for n in SMALL_SHARDED])
        return [("scatter", small_c),
                ("gather", _pack([full[n] for n in REPLICATED_EARLY], F32))]

    dx, g1, got1 = _layer_backward(dx, sv1, layers[1], tables, tm, tq, {
        "attn": lambda g: ("scatter", by_owner(g["w_out"]))})
    dx, g0, got0 = _layer_backward(dx, sv0, layers[0], tables, tm, tq, {
        "gla": lambda g: ("scatter", g1["w_in_b"]),
        "attn": lambda g: [("scatter", g1["w_in_a"]), ("scatter", by_owner(g["w_out"]))],
        "d_w_in_a": small_jobs,
        "d_w_in_b": lambda g: ("scatter", g["w_in_a"]),
        "d_h": lambda g: ("scatter", g["w_in_b"])})
    in_parts = ((got0["d_w_in_b"][0], got0["d_h"][0]), (got0["attn"][0], got0["gla"][0]))
    out_parts = ((got0["attn"][1],), (got1["attn"][0],))
    small_parts, rep_parts = got0["d_w_in_a"]
    norm_pack = _pack([jnp.stack([g0["norm_g"][0], g1["norm_g"][0]])], F32)

    out = {}
    out["w_in"] = _adam_columns(in_parts, w_in, m_w_in, v_w_in, "adam_w_in", 256)
    *out["w_out"], norm_parts = _adam_layers(out_parts, w_out, m_w_out, v_w_out, "adam_w_out", 128,
                                             rider=("gather", norm_pack))
    for names, parts, label in ((SMALL_SHARDED, small_parts, "adam_small"),
                                (REPLICATED_EARLY, rep_parts, "adam_replicated"), (("norm_g",), norm_parts, "adam_norm")):
        res = _adam(parts, _pack([w[n] for n in names], F32), _pack([m[n] for n in names], F32),
                    _pack([v[n] for n in names], F32), label, 2048)
        for n, *vals in zip(names, *[_unpack(a, [w[n] for n in names]) for a in res]):
            out[n] = vals

    return (loss, dx[None], *[out[n][0] for n in WEIGHTS], *[out[n][1] for n in WEIGHTS],
            *[out[n][2] for n in WEIGHTS], *[out[n][3] for n in WEIGHTS])
```
